```python
import jax, jax.numpy as jnp
from jax import lax
import numpy as np

D_MODEL = 1024
BATCH = 32
SEQ = 256
DEPTH = 2
DEC_BATCH = 8
DEC_SEQ = 1024
PAST_LEN = 512

GRID_W = 64
N_EVEN = (DEPTH + 1) // 2
N_ODD = DEPTH // 2
RMS_EPS = 1e-6
N_MOD = 6

HG_HEADS = 4
HG_DK = 128
HG_DV = 128
HG_KW = HG_HEADS * HG_DK
HG_VW = HG_HEADS * HG_DV
HG_CHUNK = 16

ML_HEADS = 4
ML_DK = 128
ML_DV = 128
ML_KW = ML_HEADS * ML_DK
ML_VW = ML_HEADS * ML_DV
ML_CHUNK = 64
ML_N_GATES = 4 * ML_HEADS
ML_FGATE_BIAS = 3.0

SHORT_CONV = 3

EVEN_SPLIT = (HG_KW, HG_VW, HG_VW, HG_KW, HG_KW, ML_KW, ML_KW, ML_VW, ML_VW, ML_N_GATES)
EVEN_PROJ = 3 * HG_KW + 2 * HG_VW + 2 * ML_KW + 2 * ML_VW + ML_N_GATES
MIX_OUT = HG_VW + ML_VW

HY_ORDER = 2
HY_BANDS = 16
HY_EMB = 1 + 2 * HY_BANDS
HY_HIDDEN = 64
HY_FILT = 2 * HY_ORDER * D_MODEL
HY_MIN_DECAY = 3.07
HY_MAX_DECAY = 15.35

N_EXPERTS = 32
TOP_K = 4
D_FF = D_MODEL
SWIGLU_LIMIT = 7.0
SWIGLU_ALPHA = 1.702

kernel_name = 'hybrid_prefix_diffusion_step'


def rmsnorm(x, g):
    xf = x.astype(jnp.float32)
    y = xf * lax.rsqrt(jnp.mean(xf * xf, axis=-1, keepdims=True) + RMS_EPS)
    return (y * g.astype(jnp.float32)).astype(x.dtype)


def head_rms(x):
    return x * lax.rsqrt(jnp.mean(x * x, axis=-1, keepdims=True) + RMS_EPS)


def split_heads(x, n):
    b, l, w = x.shape
    return x.reshape(b, l, n, w // n).transpose(0, 2, 1, 3).astype(jnp.float32)


def merge_heads(x):
    b, h, l, d = x.shape
    return x.transpose(0, 2, 1, 3).reshape(b, l, h * d)


def rev(x):
    return jnp.flip(x, axis=2)


def short_conv(x, w):
    k = w.shape[0]
    pad = k // 2
    L = x.shape[1]
    xp = jnp.pad(x, ((0, 0), (pad, pad), (0, 0)))
    y = xp[:, 0:L] * w[0]
    for j in range(1, k):
        y = y + xp[:, j:j + L] * w[j]
    return y


def grid_positions(n_tok, d):
    f32 = jnp.float32
    rows = n_tok // GRID_W
    r, col = jnp.meshgrid(jnp.arange(rows, dtype=f32), jnp.arange(GRID_W, dtype=f32), indexing='ij')
    r = r.reshape(-1)
    col = col.reshape(-1)
    quarter = d // 4
    inv = 1.0 / (10000.0 ** (jnp.arange(quarter, dtype=f32) / quarter))
    ar = r[:, None] * inv[None]
    ac = col[:, None] * inv[None]
    return jnp.concatenate([jnp.sin(ar), jnp.cos(ar), jnp.sin(ac), jnp.cos(ac)], axis=-1)


def gla_chunked(q, k, v, log_f, s0):
    f32 = jnp.float32
    bsz, nh, L, dk = q.shape
    dv = v.shape[-1]
    C = HG_CHUNK
    n = L // C
    q = q.reshape(bsz, nh, n, C, dk)
    k = k.reshape(bsz, nh, n, C, dk)
    v = v.reshape(bsz, nh, n, C, dv)
    b = jnp.cumsum(log_f.reshape(bsz, nh, n, C, dk), axis=3)
    causal = jnp.tril(jnp.ones((C, C), dtype=bool))[:, :, None]
    diff = b[:, :, :, :, None, :] - b[:, :, :, None, :, :]
    decay = jnp.exp(jnp.where(causal, diff, -jnp.inf))
    attn = jnp.einsum('bhntd,bhnsd,bhntsd->bhnts', q, k, decay)
    intra = jnp.einsum('bhnts,bhnsv->bhntv', attn, v)
    b_last = b[:, :, :, -1:, :]
    q_in = q * jnp.exp(b)
    k_out = k * jnp.exp(b_last - b)
    chunk_decay = jnp.exp(b_last[:, :, :, 0, :])

    def step(s, xs):
        qc, kc, vc, dc = xs
        o = jnp.einsum('bhtd,bhdv->bhtv', qc, s)
        s = dc[..., None] * s + jnp.einsum('bhsd,bhsv->bhdv', kc, vc)
        return s, o

    xs = tuple(jnp.moveaxis(a, 2, 0) for a in (q_in, k_out, v, chunk_decay))
    s_fin, inter = lax.scan(step, s0.astype(f32), xs)
    out = intra + jnp.moveaxis(inter, 0, 2)
    return out.reshape(bsz, nh, L, dv), s_fin


def mlstm_chunked(q, k, v, ig, log_f, c0, n0, m0):
    f32 = jnp.float32
    bsz, nh, L, dk = q.shape
    dv = v.shape[-1]
    C = ML_CHUNK
    n = L // C
    q = q.reshape(bsz, nh, n, C, dk)
    k = k.reshape(bsz, nh, n, C, dk)
    v = v.reshape(bsz, nh, n, C, dv)
    ig = ig.reshape(bsz, nh, n, C)
    b = jnp.cumsum(log_f.reshape(bsz, nh, n, C), axis=3)
    causal = jnp.tril(jnp.ones((C, C), dtype=bool))
    dmat = jnp.where(causal, b[..., :, None] - b[..., None, :] + ig[..., None, :], -jnp.inf)
    qk = jnp.einsum('bhntd,bhnsd->bhnts', q, k)

    def step(carry, xs):
        c_prev, n_prev, m_prev = carry
        qc, kc, vc, bc, igc, qkc, dc = xs
        m_t = jnp.maximum(bc + m_prev[..., None], jnp.max(dc, axis=-1))
        p = jnp.exp(dc - m_t[..., None]) * qkc
        inter = jnp.exp(bc + m_prev[..., None] - m_t)
        num = inter[..., None] * jnp.einsum('bhtd,bhdv->bhtv', qc, c_prev) + jnp.einsum('bhts,bhsv->bhtv', p, vc)
        den = inter * jnp.einsum('bhtd,bhd->bht', qc, n_prev) + jnp.sum(p, axis=-1)
        h = num / jnp.maximum(jnp.abs(den), jnp.exp(-m_t))[..., None]
        m_new = m_t[..., -1]
        w = jnp.exp(bc[..., -1:] - bc + igc - m_new[..., None])
        dec = jnp.exp(bc[..., -1] + m_prev - m_new)
        c_new = dec[..., None, None] * c_prev + jnp.einsum('bhs,bhsd,bhsv->bhdv', w, kc, vc)
        n_new = dec[..., None] * n_prev + jnp.einsum('bhs,bhsd->bhd', w, kc)
        return (c_new, n_new, m_new), h

    xs = tuple(jnp.moveaxis(a, 2, 0) for a in (q, k, v, b, ig, qk, dmat))
    carry0 = (c0.astype(f32), n0.astype(f32), m0.astype(f32))
    (c_f, n_f, m_f), h = lax.scan(step, carry0, xs)
    return jnp.moveaxis(h, 0, 2).reshape(bsz, nh, L, dv), (c_f, n_f, m_f)


def even_mixer(h, w_in, gate_b, conv_w, lb, w_out, hg_s0, ml_c0, ml_n0, ml_m0):
    f32 = jnp.float32
    bsz, L, _ = h.shape
    p = h @ w_in
    cuts = np.cumsum(EVEN_SPLIT)[:-1].tolist()
    hq, hi, hg, hff, hfb, mq, mk, mv, mo, mg = jnp.split(p, cuts, axis=-1)
    lbh = lb.astype(f32).reshape(HG_HEADS, 1, HG_DK)
    q = split_heads(hq, HG_HEADS)
    iv = split_heads(hi, HG_HEADS)
    f_f = lbh + (1.0 - lbh) * jax.nn.sigmoid(split_heads(hff, HG_HEADS))
    f_b = lbh + (1.0 - lbh) * jax.nn.sigmoid(split_heads(hfb, HG_HEADS))
    o_f, s_f = gla_chunked(q, 1.0 - f_f, iv, jnp.log(f_f), hg_s0[:, 0])
    o_b, s_b = gla_chunked(rev(q), rev(1.0 - f_b), rev(iv), rev(jnp.log(f_b)), hg_s0[:, 1])
    o_hg = merge_heads(head_rms(o_f + rev(o_b))) * jax.nn.silu(hg.astype(f32))
    qk = jax.nn.silu(short_conv(jnp.concatenate([mq, mk], axis=-1), conv_w))
    mq_c, mk_c = jnp.split(qk, 2, axis=-1)
    q2 = split_heads(mq_c, ML_HEADS)
    k2 = split_heads(mk_c, ML_HEADS) * (ML_DK ** -0.5)
    v2 = split_heads(mv, ML_HEADS)
    g = (mg + gate_b).astype(f32).reshape(bsz, L, 4, ML_HEADS).transpose(2, 0, 3, 1)
    ig_f, lf_f = g[0], jax.nn.log_sigmoid(g[1])
    ig_b, lf_b = g[2], jax.nn.log_sigmoid(g[3])
    h_f, (c_f, n_f, m_f) = mlstm_chunked(q2, k2, v2, ig_f, lf_f, ml_c0[:, 0], ml_n0[:, 0], ml_m0[:, 0])
    h_b, (c_b, n_b, m_b) = mlstm_chunked(rev(q2), rev(k2), rev(v2), rev(ig_b), rev(lf_b),
                                         ml_c0[:, 1], ml_n0[:, 1], ml_m0[:, 1])
    o_ml = merge_heads(head_rms(h_f + rev(h_b))) * jax.nn.sigmoid(mo.astype(f32))
    y = jnp.concatenate([o_hg, o_ml], axis=-1).astype(h.dtype) @ w_out
    states = (jnp.stack([s_f, s_b], axis=1), jnp.stack([c_f, c_b], axis=1),
              jnp.stack([n_f, n_b], axis=1), jnp.stack([m_f, m_b], axis=1))
    return y, states


def hyena_filters(L, w1, b1, w2, b2, w3, b3, freq, log_rate):
    f32 = jnp.float32
    t = jnp.arange(L, dtype=f32)
    t_norm = t / (L - 1)
    bands = jnp.linspace(1e-4, HY_BANDS - 1, HY_BANDS, dtype=f32)
    ang = (2.0 * np.pi / L) * t[:, None] * bands[None, :]
    z = jnp.concatenate([t_norm[:, None], jnp.cos(ang), jnp.sin(ang)], axis=-1)
    a = jnp.sin(freq[0].astype(f32) * (z @ w1.astype(f32) + b1.astype(f32)))
    a = jnp.sin(freq[1].astype(f32) * (a @ w2.astype(f32) + b2.astype(f32)))
    filt = (a @ w3.astype(f32) + b3.astype(f32)) * jnp.exp(-t_norm[:, None] * jnp.exp(log_rate.astype(f32)))
    filt = filt.reshape(L, 2, HY_ORDER, D_MODEL)
    return filt * lax.rsqrt(jnp.sum(filt * filt, axis=(0, 1), keepdims=True))


def long_conv_bidir(z, h_fwd, h_bwd):
    L = z.shape[1]
    kern = jnp.concatenate([h_fwd, jnp.zeros_like(h_fwd[:1]), h_bwd[:0:-1]], axis=0)
    zf = jnp.fft.rfft(z.astype(jnp.float32), n=2 * L, axis=1)
    kf = jnp.fft.rfft(kern, axis=0)
    return jnp.fft.irfft(zf * kf[None], n=2 * L, axis=1)[:, :L]


def hyena_mixer(h, w_in, conv_w, w1, b1, w2, b2, w3, b3, freq, log_rate, bias, w_out):
    L = h.shape[1]
    u = short_conv(h @ w_in, conv_w).astype(jnp.float32)
    v, x1, x2 = jnp.split(u, 3, axis=-1)
    filt = hyena_filters(L, w1, b1, w2, b2, w3, b3, freq, log_rate)
    z = v
    for o, gate in enumerate((x1, x2)):
        z = gate * (long_conv_bidir(z, filt[:, 0, o], filt[:, 1, o]) + z * bias[o].astype(jnp.float32))
    return z.astype(h.dtype) @ w_out


def moe(h, w_router, b_router, w_gu, b_gu, w_down, b_down):
    bsz, L, d = h.shape
    x = h.reshape(-1, d)
    logits = (x @ w_router + b_router).astype(jnp.float32)
    vals, idx = lax.top_k(logits, TOP_K)
    wts = jax.nn.softmax(vals, axis=-1)
    gates = jnp.sum(jax.nn.one_hot(idx, N_EXPERTS, dtype=jnp.float32) * wts[..., None], axis=1)
    y = jnp.zeros(x.shape, jnp.float32)
    for e in range(N_EXPERTS):
        gu = x @ w_gu[e] + b_gu[e]
        gl, up = gu[:, 0::2], gu[:, 1::2]
        gl = jnp.minimum(gl, SWIGLU_LIMIT)
        up = jnp.clip(up, -SWIGLU_LIMIT, SWIGLU_LIMIT)
        act = (up + 1.0) * gl * jax.nn.sigmoid(SWIGLU_ALPHA * gl)
        y = y + gates[:, e:e + 1] * (act @ w_down[e] + b_down[e])
    return y.astype(h.dtype).reshape(bsz, L, d)


def trunk(x, cond, st_hg, st_c, st_n, st_m, P):
    lb_all = jnp.cumsum(jax.nn.softmax(P['hg_lb'].astype(jnp.float32), axis=0), axis=0)
    new_states = []
    for l in range(DEPTH):
        mod = jax.nn.silu(cond) @ P['w_mod'][l] + P['b_mod'][l]
        sh1, sc1, g1, sh2, sc2, g2 = jnp.split(mod[:, None, :], N_MOD, axis=-1)
        hdn = (rmsnorm(x, P['norm_g'][l, 0]) * (1.0 + sc1) + sh1).astype(x.dtype)
        if l % 2 == 0:
            e = l // 2
            y, st = even_mixer(hdn, P['ev_w_in'][e], P['ev_gate_b'][e], P['ev_conv'][e], lb_all[l],
                               P['ev_w_out'][e], st_hg[:, e], st_c[:, e], st_n[:, e], st_m[:, e])
            new_states.append(st)
        else:
            o = l // 2
            y = hyena_mixer(hdn, P['hy_w_in'][o], P['hy_conv'][o], P['hy_w1'][o], P['hy_b1'][o],
                            P['hy_w2'][o], P['hy_b2'][o], P['hy_w3'][o], P['hy_b3'][o],
                            P['hy_freq'][o], P['hy_log_rate'][o], P['hy_bias'][o], P['hy_w_out'][o])
        x = x + (g1 * y).astype(x.dtype)
        hdn = (rmsnorm(x, P['norm_g'][l, 1]) * (1.0 + sc2) + sh2).astype(x.dtype)
        y = moe(hdn, P['w_router'][l], P['b_router'][l], P['w_gu'][l], P['b_gu'][l], P['w_down'][l], P['b_down'][l])
        x = x + (g2 * y).astype(x.dtype)
    x = rmsnorm(x, P['final_g'])
    stacked = tuple(jnp.stack([s[i] for s in new_states], axis=1) for i in range(4))
    return x, stacked


def setup_inputs(seed: int = 0) -> dict:
    key = jax.random.key(seed)
    ks = iter(jax.random.split(key, 48))
    f32 = jnp.float32
    D = D_MODEL

    def nrm(shape, scale):
        return scale * jax.random.normal(next(ks), shape, f32)

    gate_base = jnp.repeat(jnp.array([0.0, ML_FGATE_BIAS, 0.0, ML_FGATE_BIAS], f32), ML_HEADS)
    rate_base = jnp.log(jnp.linspace(HY_MIN_DECAY, HY_MAX_DECAY, D, dtype=f32))
    return {
        'x_prompt': nrm((BATCH, SEQ, D), 1.0),
        'x_sample': nrm((DEC_BATCH, DEC_SEQ, D), 1.0),
        'state_hgrn': nrm((DEC_BATCH, N_EVEN, 2, HG_HEADS, HG_DK, HG_DV), 0.5),
        'state_mlstm_c': nrm((DEC_BATCH, N_EVEN, 2, ML_HEADS, ML_DK, ML_DV), 0.5),
        'state_mlstm_n': nrm((DEC_BATCH, N_EVEN, 2, ML_HEADS, ML_DK), 0.5),
        'state_mlstm_m': nrm((DEC_BATCH, N_EVEN, 2, ML_HEADS), 1.0),
        'c': nrm((DEC_BATCH, D), 1.0),
        'c_ctx': nrm((D,), 1.0),
        'norm_g': 1.0 + nrm((DEPTH, 2, D), 0.02),
        'final_g': 1.0 + nrm((D,), 0.02),
        'w_mod': nrm((DEPTH, D, N_MOD * D), 0.5 * D ** -0.5),
        'b_mod': nrm((DEPTH, N_MOD * D), 0.02),
        'ev_w_in': nrm((N_EVEN, D, EVEN_PROJ), D ** -0.5),
        'ev_gate_b': jnp.tile(gate_base, (N_EVEN, 1)) + nrm((N_EVEN, ML_N_GATES), 0.1),
        'ev_conv': nrm((N_EVEN, SHORT_CONV, 2 * ML_KW), 0.5),
        'hg_lb': nrm((DEPTH + 1, HG_KW), 0.1),
        'ev_w_out': nrm((N_EVEN, MIX_OUT, D), MIX_OUT ** -0.5),
        'hy_w_in': nrm((N_ODD, D, 3 * D), D ** -0.5),
        'hy_conv': nrm((N_ODD, SHORT_CONV, 3 * D), 0.5),
        'hy_w1': nrm((N_ODD, HY_EMB, HY_HIDDEN), HY_EMB ** -0.5),
        'hy_b1': nrm((N_ODD, HY_HIDDEN), 0.02),
        'hy_w2': nrm((N_ODD, HY_HIDDEN, HY_HIDDEN), HY_HIDDEN ** -0.5),
        'hy_b2': nrm((N_ODD, HY_HIDDEN), 0.02),
        'hy_w3': nrm((N_ODD, HY_HIDDEN, HY_FILT), HY_HIDDEN ** -0.5),
        'hy_b3': nrm((N_ODD, HY_FILT), 0.02),
        'hy_freq': 1.0 + nrm((N_ODD, 2, HY_HIDDEN), 0.1),
        'hy_log_rate': jnp.tile(rate_base, (N_ODD, 2 * HY_ORDER)) + nrm((N_ODD, HY_FILT), 0.05),
        'hy_bias': nrm((N_ODD, HY_ORDER, D), 0.5),
        'hy_w_out': nrm((N_ODD, D, D), D ** -0.5),
        'w_router': nrm((DEPTH, D, N_EXPERTS), D ** -0.5),
        'b_router': nrm((DEPTH, N_EXPERTS), 0.01),
        'w_gu': nrm((DEPTH, N_EXPERTS, D, 2 * D_FF), D ** -0.5),
        'b_gu': nrm((DEPTH, N_EXPERTS, 2 * D_FF), 0.02),
        'w_down': nrm((DEPTH, N_EXPERTS, D_FF, D), D_FF ** -0.5),
        'b_down': nrm((DEPTH, N_EXPERTS, D), 0.02),
    }


def reference(x_prompt, x_sample, state_hgrn, state_mlstm_c, state_mlstm_n, state_mlstm_m, c, c_ctx,
              norm_g, final_g, w_mod, b_mod, ev_w_in, ev_gate_b, ev_conv, hg_lb, ev_w_out,
              hy_w_in, hy_conv, hy_w1, hy_b1, hy_w2, hy_b2, hy_w3, hy_b3, hy_freq, hy_log_rate, hy_bias, hy_w_out,
              w_router, b_router, w_gu, b_gu, w_down, b_down):
    f32 = jnp.float32
    P = {
        'norm_g': norm_g, 'final_g': final_g, 'w_mod': w_mod, 'b_mod': b_mod,
        'ev_w_in': ev_w_in, 'ev_gate_b': ev_gate_b, 'ev_conv': ev_conv, 'hg_lb': hg_lb, 'ev_w_out': ev_w_out,
        'hy_w_in': hy_w_in, 'hy_conv': hy_conv, 'hy_w1': hy_w1, 'hy_b1': hy_b1, 'hy_w2': hy_w2, 'hy_b2': hy_b2,
        'hy_w3': hy_w3, 'hy_b3': hy_b3, 'hy_freq': hy_freq, 'hy_log_rate': hy_log_rate, 'hy_bias': hy_bias,
        'hy_w_out': hy_w_out, 'w_router': w_router, 'b_router': b_router, 'w_gu': w_gu, 'b_gu': b_gu,
        'w_down': w_down, 'b_down': b_down,
    }
    bp = x_prompt.shape[0]
    z_hg = jnp.zeros((bp, N_EVEN, 2, HG_HEADS, HG_DK, HG_DV), f32)
    z_c = jnp.zeros((bp, N_EVEN, 2, ML_HEADS, ML_DK, ML_DV), f32)
    z_n = jnp.zeros((bp, N_EVEN, 2, ML_HEADS, ML_DK), f32)
    z_m = jnp.zeros((bp, N_EVEN, 2, ML_HEADS), f32)
    y_prompt, (new_hgrn, new_mlstm_c, new_mlstm_n, new_mlstm_m) = trunk(
        x_prompt, c_ctx[None, :], z_hg, z_c, z_n, z_m, P)
    pos = grid_positions(x_sample.shape[1], D_MODEL).astype(x_sample.dtype)
    y_sample, _ = trunk(x_sample + pos[None], c, state_hgrn, state_mlstm_c, state_mlstm_n, state_mlstm_m, P)
    return (y_prompt, y_sample, new_hgrn, new_mlstm_c, new_mlstm_n, new_mlstm_m)
```

```python
import functools

import numpy as np
import jax
import jax.numpy as jnp
from jax import lax
from jax.experimental import pallas as pl
from jax.experimental.pallas import tpu as pltpu

F32 = jnp.float32
BF16 = jnp.bfloat16

D_MODEL = 1024
BATCH = 32
SEQ = 256
DEPTH = 2
DEC_BATCH = 8
DEC_SEQ = 1024
GRID_W = 64
RMS_EPS = 1e-6
N_MOD = 6

HEADS = 4
HEAD_DIM = 128
GROUP_W = HEADS * HEAD_DIM
N_GATES = 4 * HEADS
EVEN_MAIN = 9 * GROUP_W

HY_ORDER = 2
HY_BANDS = 16
HY_EMB = 1 + 2 * HY_BANDS
HY_HIDDEN = 64

N_EXPERTS = 32
TOP_K = 4
SWIGLU_LIMIT = 7.0
SWIGLU_ALPHA = 1.702

N_PROMPT = BATCH * SEQ
N_SAMPLE = DEC_BATCH * DEC_SEQ
N_TOK = N_PROMPT + N_SAMPLE
N_COND = 16

TOK_TILE = 256
SCAN_CHUNK = 128
SUB = 16
MOE_TILE = 256
MOE_ROWS = N_TOK * TOP_K + N_EXPERTS * MOE_TILE
COPY_ROWS = 2048
COPY_WINDOW = 16

VMEM_LIMIT = 56 * 1024 * 1024


def _cparams(*sem):
    return pltpu.CompilerParams(dimension_semantics=sem, vmem_limit_bytes=VMEM_LIMIT)


def _split3(x):
    hi = x.astype(BF16)
    r = x - hi.astype(F32)
    mid = r.astype(BF16)
    lo = (r - mid.astype(F32)).astype(BF16)
    return hi, mid, lo


def _dot(a, b):
    return jnp.dot(a, b, preferred_element_type=F32)


def _dot_nt(a, b):
    return lax.dot_general(a, b, (((1,), (1,)), ((), ())), preferred_element_type=F32)


def _dot_tn(a, b):
    return lax.dot_general(a, b, (((0,), (0,)), ((), ())), preferred_element_type=F32)


def _dot_w3(a_exact_bf16, x):
    hi, mid, lo = _split3(x)
    return _dot(a_exact_bf16, hi) + _dot(a_exact_bf16, mid) + _dot(a_exact_bf16, lo)


def _sigmoid(x):
    return 1.0 / (1.0 + jnp.exp(-x))


def _silu(x):
    return x * _sigmoid(x)


def _log_sigmoid(x):
    return jnp.minimum(x, 0.0) - jnp.log(1.0 + jnp.exp(-jnp.abs(x)))


def _tile_cond_row(i):
    n_prompt_tiles = N_PROMPT // TOK_TILE
    tiles_per_seq = DEC_SEQ // TOK_TILE
    return jnp.where(i < n_prompt_tiles, 0, 1 + (i - n_prompt_tiles) // tiles_per_seq)


def _mod_kernel(cond_ref, w_ref, b_ref, o_ref):
    a = _silu(cond_ref[...]).astype(BF16)
    o_ref[...] = _dot(a, w_ref[...].astype(BF16)) + b_ref[...]


def _modulation(cond, w_mod, b_mod):
    d = D_MODEL
    out = pl.pallas_call(
        _mod_kernel,
        grid=(DEPTH, N_MOD),
        in_specs=[
            pl.BlockSpec((N_COND, d), lambda l, j: (0, 0)),
            pl.BlockSpec((None, d, d), lambda l, j: (l, 0, j)),
            pl.BlockSpec((None, 1, d), lambda l, j: (l, 0, j)),
        ],
        out_specs=pl.BlockSpec((None, None, N_COND, d), lambda l, j: (l, j, 0, 0)),
        out_shape=jax.ShapeDtypeStruct((DEPTH, N_MOD, N_COND, d), F32),
        compiler_params=_cparams("parallel", "parallel"),
        name="modulation",
    )(cond, w_mod, b_mod.reshape(DEPTH, 1, N_MOD * d))
    return out.transpose(0, 2, 1, 3)


def _norm_mod(x, g_row, scale_row, shift_row):
    ms = jnp.mean(x * x, axis=-1, keepdims=True)
    y = x * lax.rsqrt(ms + RMS_EPS) * g_row
    return y * (1.0 + scale_row) + shift_row


def _proj_even_kernel(x_ref, pos_ref, mod_ref, g_ref, w_ref, wg_ref, wgt_ref, gb_ref, gbt_ref,
                      xres_ref, p_ref, gate_ref, gate_t_ref):
    x = x_ref[...] + pos_ref[...]
    xres_ref[...] = x
    h = _norm_mod(x, g_ref[...], mod_ref[1:2, :], mod_ref[0:1, :]).astype(BF16)
    p_ref[...] = _dot(h, w_ref[...])
    gate_ref[...] = _dot(h, wg_ref[...]) + gb_ref[...]
    gate_t_ref[...] = _dot_nt(wgt_ref[...], h) + gbt_ref[...]


def _proj_even(x, pos_tab, mod_l, norm_g, w_in, gate_b):
    d = D_MODEL
    n_tiles = N_TOK // TOK_TILE
    n_prompt_tiles = N_PROMPT // TOK_TILE
    tiles_per_seq = DEC_SEQ // TOK_TILE
    w_main = w_in[:, :EVEN_MAIN].astype(BF16)
    w_gate = w_in[:, EVEN_MAIN:].astype(BF16)

    def pos_map(i):
        return (jnp.where(i < n_prompt_tiles, 0, 1 + (i - n_prompt_tiles) % tiles_per_seq), 0)

    return pl.pallas_call(
        _proj_even_kernel,
        grid=(n_tiles,),
        in_specs=[
            pl.BlockSpec((TOK_TILE, d), lambda i: (i, 0)),
            pl.BlockSpec((TOK_TILE, d), pos_map),
            pl.BlockSpec((None, N_MOD, d), lambda i: (_tile_cond_row(i), 0, 0)),
            pl.BlockSpec((1, d), lambda i: (0, 0)),
            pl.BlockSpec((d, EVEN_MAIN), lambda i: (0, 0)),
            pl.BlockSpec((d, N_GATES), lambda i: (0, 0)),
            pl.BlockSpec((N_GATES, d), lambda i: (0, 0)),
            pl.BlockSpec((1, N_GATES), lambda i: (0, 0)),
            pl.BlockSpec((N_GATES, 1), lambda i: (0, 0)),
        ],
        out_specs=[
            pl.BlockSpec((TOK_TILE, d), lambda i: (i, 0)),
            pl.BlockSpec((TOK_TILE, EVEN_MAIN), lambda i: (i, 0)),
            pl.BlockSpec((TOK_TILE, N_GATES), lambda i: (i, 0)),
            pl.BlockSpec((N_GATES, TOK_TILE), lambda i: (0, i)),
        ],
        out_shape=[
            jax.ShapeDtypeStruct((N_TOK, d), F32),
            jax.ShapeDtypeStruct((N_TOK, EVEN_MAIN), F32),
            jax.ShapeDtypeStruct((N_TOK, N_GATES), F32),
            jax.ShapeDtypeStruct((N_GATES, N_TOK), F32),
        ],
        compiler_params=_cparams("parallel"),
        name="proj_even",
    )(x, pos_tab, mod_l, norm_g.reshape(1, d), w_main, w_gate, w_gate.T,
      gate_b.reshape(1, N_GATES), gate_b.reshape(N_GATES, 1))


def _proj_odd_kernel(x_ref, mod_ref, g_ref, w_ref, p_ref):
    h = _norm_mod(x_ref[...], g_ref[...], mod_ref[1:2, :], mod_ref[0:1, :]).astype(BF16)
    p_ref[...] = _dot(h, w_ref[...])


def _proj_odd(x, mod_l, norm_g, w_in):
    d = D_MODEL
    width = w_in.shape[1]
    return pl.pallas_call(
        _proj_odd_kernel,
        grid=(N_TOK // TOK_TILE,),
        in_specs=[
            pl.BlockSpec((TOK_TILE, d), lambda i: (i, 0)),
            pl.BlockSpec((None, N_MOD, d), lambda i: (_tile_cond_row(i), 0, 0)),
            pl.BlockSpec((1, d), lambda i: (0, 0)),
            pl.BlockSpec((d, width), lambda i: (0, 0)),
        ],
        out_specs=pl.BlockSpec((TOK_TILE, width), lambda i: (i, 0)),
        out_shape=jax.ShapeDtypeStruct((N_TOK, width), F32),
        compiler_params=_cparams("parallel"),
        name="proj_odd",
    )(x, mod_l, norm_g.reshape(1, d), w_in.astype(BF16))


def _hgrn_chunk(reverse, q, k, v, lf, st):
    c = SCAN_CHUNK
    row = lax.broadcasted_iota(jnp.int32, (c, c), 0)
    col = lax.broadcasted_iota(jnp.int32, (c, c), 1)
    tri = (col >= row) if reverse else (col <= row)
    b = _dot_w3(jnp.where(tri, 1.0, 0.0).astype(BF16), lf)

    lane = lax.broadcasted_iota(jnp.int32, (SUB, c), 1)
    sub_row = lax.broadcasted_iota(jnp.int32, (SUB, 1), 0)
    chunk_row = lax.broadcasted_iota(jnp.int32, (c, 1), 0)
    rows = []
    for i in range(c // SUB):
        lo, hi = i * SUB, (i + 1) * SUB
        qi, ki, bi = q[lo:hi], k[lo:hi], b[lo:hi]
        if reverse:
            has_off, edge = hi < c, hi
            outside = chunk_row >= hi
        else:
            has_off, edge = lo > 0, lo - 1
            outside = chunk_row < lo
        if has_off:
            beta = b[edge:edge + 1]
            qs = qi * jnp.exp(bi - beta)
            ks = k * jnp.exp(jnp.where(outside, beta - b, -jnp.inf))
            a_row = _dot_nt(qs.astype(BF16), ks.astype(BF16))
        else:
            a_row = jnp.zeros((SUB, c), F32)
        for s in range(SUB):
            valid = (sub_row <= s) if reverse else (sub_row >= s)
            dlog = jnp.where(valid, bi - bi[s:s + 1], -jnp.inf)
            a_col = jnp.sum(jnp.exp(dlog) * qi * ki[s:s + 1], axis=-1, keepdims=True)
            a_row = jnp.where(lane == lo + s, a_col, a_row)
        rows.append(a_row)
    attn = jnp.concatenate(rows, axis=0)
    o = _dot(attn.astype(BF16), v.astype(BF16)) + _dot_nt((q * jnp.exp(b)).astype(BF16), st.astype(BF16))
    b_exit = b[0:1] if reverse else b[c - 1:c]
    k_out = k * jnp.exp(b_exit - b)
    st_new = jnp.exp(b_exit) * st + _dot_tn(v.astype(BF16), k_out.astype(BF16))
    return o, st_new


def _hgrn_kernel(seq_len, layer, q_ref, i_ref, g_ref, ff_ref, fb_ref, lb_ref, s0_ref,
                 o_ref, s_out_ref, of_ref, ob_ref, st_ref):
    c = SCAN_CHUNK
    n_chunks = seq_len // c
    lbp = lb_ref[...]
    e = jnp.exp(lbp - jnp.max(lbp, axis=0, keepdims=True))
    lb = jnp.sum(e[0:layer + 1], axis=0, keepdims=True) / jnp.sum(e, axis=0, keepdims=True)

    st_ref[0] = s0_ref[0].T
    st_ref[1] = s0_ref[1].T

    def body(n, carry):
        for reverse in (False, True):
            d = 1 if reverse else 0
            base = pl.multiple_of((n_chunks - 1 - n if reverse else n) * c, c)
            sl = pl.ds(base, c)
            f = lb + (1.0 - lb) * _sigmoid((fb_ref if reverse else ff_ref)[sl, :])
            o, st_new = _hgrn_chunk(reverse, q_ref[sl, :], 1.0 - f, i_ref[sl, :], jnp.log(f), st_ref[d])
            st_ref[d] = st_new
            (ob_ref if reverse else of_ref)[sl, :] = o
        return carry

    lax.fori_loop(0, n_chunks, body, 0)
    o = of_ref[...] + ob_ref[...]
    o = o * lax.rsqrt(jnp.mean(o * o, axis=-1, keepdims=True) + RMS_EPS)
    o_ref[...] = o * _silu(g_ref[...])
    s_out_ref[0] = st_ref[0].T
    s_out_ref[1] = st_ref[1].T


def _hgrn(p, hg_lb, layer, s0, n_seq, seq_len, tok_offset):
    hd = HEAD_DIM
    row0 = tok_offset // seq_len

    def col(part):
        return pl.BlockSpec((seq_len, hd), lambda b, h: (row0 + b, part * HEADS + h))

    state_spec = pl.BlockSpec((None, 2, None, hd, hd), lambda b, h: (b, 0, h, 0, 0))
    return pl.pallas_call(
        functools.partial(_hgrn_kernel, seq_len, layer),
        grid=(n_seq, HEADS),
        in_specs=[col(0), col(1), col(2), col(3), col(4),
                  pl.BlockSpec((DEPTH + 1, hd), lambda b, h: (0, h)),
                  state_spec],
        out_specs=[pl.BlockSpec((seq_len, hd), lambda b, h: (b, h)), state_spec],
        out_shape=[jax.ShapeDtypeStruct((n_seq * seq_len, GROUP_W), F32),
                   jax.ShapeDtypeStruct((n_seq, 2, HEADS, hd, hd), F32)],
        scratch_shapes=[pltpu.VMEM((seq_len, hd), F32), pltpu.VMEM((seq_len, hd), F32),
                        pltpu.VMEM((2, hd, hd), F32)],
        compiler_params=_cparams("parallel", "parallel"),
        name=f"hgrn_l{seq_len}",
    )(p, p, p, p, p, hg_lb, s0)


def _short_conv3(x, w):
    n = x.shape[0]
    r = lax.broadcasted_iota(jnp.int32, (n, 1), 0)
    prev = jnp.where(r == 0, 0.0, pltpu.roll(x, 1, 0))
    nxt = jnp.where(r == n - 1, 0.0, pltpu.roll(x, n - 1, 0))
    return prev * w[0:1] + x * w[1:2] + nxt * w[2:3]


def _mlstm_chunk(reverse, q, k, v, ig_c, fg_c, ig_r, fg_r, cm, nv, m_prev):
    c = SCAN_CHUNK
    row = lax.broadcasted_iota(jnp.int32, (c, c), 0)
    col = lax.broadcasted_iota(jnp.int32, (c, c), 1)
    tri = (col >= row) if reverse else (col <= row)
    lf_c = _log_sigmoid(fg_c)
    lf_r = _log_sigmoid(fg_r)
    b_c = jnp.sum(jnp.where(tri, lf_r, 0.0), axis=-1, keepdims=True)
    tri_t = (row >= col) if reverse else (row <= col)
    b_r = jnp.sum(jnp.where(tri_t, lf_c, 0.0), axis=0, keepdims=True)
    dmat = jnp.where(tri, b_c - b_r + ig_r, -jnp.inf)
    m_t = jnp.maximum(b_c + m_prev, jnp.max(dmat, axis=-1, keepdims=True))
    qb, kb, vb = q.astype(BF16), k.astype(BF16), v.astype(BF16)
    p = jnp.exp(dmat - m_t) * _dot_nt(qb, kb)
    inter = jnp.exp(b_c + m_prev - m_t)
    num = inter * _dot(qb, cm.astype(BF16)) + _dot(p.astype(BF16), vb)
    den = inter * jnp.sum(q * nv, axis=-1, keepdims=True) + jnp.sum(p, axis=-1, keepdims=True)
    h = num / jnp.maximum(jnp.abs(den), jnp.exp(-m_t))
    last = 0 if reverse else c - 1
    m_new = m_t[last:last + 1]
    b_exit = b_c[last:last + 1]
    w_c = jnp.exp(b_exit - b_c + ig_c - m_new)
    dec = jnp.exp(b_exit + m_prev - m_new)
    kw = k * w_c
    cm_new = dec * cm + _dot_tn(kw.astype(BF16), vb)
    nv_new = dec * nv + jnp.sum(kw, axis=0, keepdims=True)
    return h, cm_new, nv_new, m_new


def _mlstm_kernel(seq_len, q_ref, k_ref, v_ref, og_ref, gate_ref, gate_t_ref, cwq_ref, cwk_ref,
                  c0_ref, n0_ref, m0_ref,
                  o_ref, c_out_ref, n_out_ref, m_out_ref,
                  q2_ref, k2_ref, hf_ref, hb_ref, c_ref, n_ref, m_ref):
    c = SCAN_CHUNK
    n_chunks = seq_len // c
    q2_ref[...] = _silu(_short_conv3(q_ref[...], cwq_ref[...]))
    k2_ref[...] = _silu(_short_conv3(k_ref[...], cwk_ref[...])) * (HEAD_DIM ** -0.5)
    c_ref[...] = c0_ref[...]
    n_ref[...] = n0_ref[...]
    m_ref[...] = m0_ref[...]

    def body(n, carry):
        for reverse in (False, True):
            d = 1 if reverse else 0
            base = pl.multiple_of((n_chunks - 1 - n if reverse else n) * c, c)
            sl = pl.ds(base, c)
            gc = gate_ref[sl, :]
            gr = gate_t_ref[:, sl]
            h, cm, nv, m_new = _mlstm_chunk(
                reverse, q2_ref[sl, :], k2_ref[sl, :], v_ref[sl, :],
                gc[:, 2 * d:2 * d + 1], gc[:, 2 * d + 1:2 * d + 2],
                gr[2 * d:2 * d + 1, :], gr[2 * d + 1:2 * d + 2, :],
                c_ref[d], n_ref[d], m_ref[d])
            c_ref[d] = cm
            n_ref[d] = nv
            m_ref[d] = m_new
            (hb_ref if reverse else hf_ref)[sl, :] = h
        return carry

    lax.fori_loop(0, n_chunks, body, 0)
    h = hf_ref[...] + hb_ref[...]
    h = h * lax.rsqrt(jnp.mean(h * h, axis=-1, keepdims=True) + RMS_EPS)
    o_ref[...] = h * _sigmoid(og_ref[...])
    c_out_ref[...] = c_ref[...]
    n_out_ref[...] = n_ref[...]
    m_out_ref[...] = m_ref[...]


def _mlstm(p, gate_h, gate_t_h, conv_w, c0, n0, m0, n_seq, seq_len, tok_offset):
    hd = HEAD_DIM
    row0 = tok_offset // seq_len

    def col(part):
        return pl.BlockSpec((seq_len, hd), lambda b, h: (row0 + b, part * HEADS + h))

    c_spec = pl.BlockSpec((None, 2, None, hd, hd), lambda b, h: (b, 0, h, 0, 0))
    n_spec = pl.BlockSpec((None, 2, None, 1, hd), lambda b, h: (b, 0, h, 0, 0))
    m_spec = pl.BlockSpec((None, 2, None, 1, 1), lambda b, h: (b, 0, h, 0, 0))
    return pl.pallas_call(
        functools.partial(_mlstm_kernel, seq_len),
        grid=(n_seq, HEADS),
        in_specs=[col(5), col(6), col(7), col(8),
                  pl.BlockSpec((None, seq_len, 4), lambda b, h: (h, row0 + b, 0)),
                  pl.BlockSpec((None, 4, seq_len), lambda b, h: (h, 0, row0 + b)),
                  pl.BlockSpec((3, hd), lambda b, h: (0, h)),
                  pl.BlockSpec((3, hd), lambda b, h: (0, HEADS + h)),
                  c_spec, n_spec, m_spec],
        out_specs=[pl.BlockSpec((seq_len, hd), lambda b, h: (b, h)), c_spec, n_spec, m_spec],
        out_shape=[jax.ShapeDtypeStruct((n_seq * seq_len, GROUP_W), F32),
                   jax.ShapeDtypeStruct((n_seq, 2, HEADS, hd, hd), F32),
                   jax.ShapeDtypeStruct((n_seq, 2, HEADS, 1, hd), F32),
                   jax.ShapeDtypeStruct((n_seq, 2, HEADS, 1, 1), F32)],
        scratch_shapes=[pltpu.VMEM((seq_len, hd), F32), pltpu.VMEM((seq_len, hd), F32),
                        pltpu.VMEM((seq_len, hd), F32), pltpu.VMEM((seq_len, hd), F32),
                        pltpu.VMEM((2, hd, hd), F32), pltpu.VMEM((2, 1, hd), F32),
                        pltpu.VMEM((2, 1, 1), F32)],
        compiler_params=_cparams("parallel", "parallel"),
        name=f"mlstm_l{seq_len}",
    )(p, p, p, p, gate_h, gate_t_h, conv_w, conv_w, c0, n0, m0)


def _dft_tables(seq_len):
    n = 2 * seq_len
    k = jnp.arange(seq_len, dtype=jnp.int32)[:, None]
    t = jnp.arange(seq_len, dtype=jnp.int32)[None, :]
    ang = ((k * t) % n).astype(F32) * (2.0 * np.pi / n)
    fc = jnp.cos(ang)
    fs = jnp.sin(ang)
    nyq = jnp.where(t % 2 == 0, 1.0, -1.0).astype(F32)
    fs = jnp.where(k == 0, nyq, fs)
    return jnp.concatenate([fc, fs], axis=0)


def _filter_kernel(seq_len, z_ref, w1_ref, b1_ref, w2_ref, b2_ref, w3f_ref, w3b_ref, b3f_ref, b3b_ref,
                   f0_ref, f1_ref, rf_ref, rb_ref, fhi_ref, flo_ref, kf_ref):
    hp = lax.Precision.HIGHEST
    n = 2 * seq_len
    z = z_ref[...]
    a = jnp.sin(f0_ref[...] * (jnp.dot(z, w1_ref[...], precision=hp, preferred_element_type=F32) + b1_ref[...]))
    a = jnp.sin(f1_ref[...] * (jnp.dot(a, w2_ref[...], precision=hp, preferred_element_type=F32) + b2_ref[...]))
    t_norm = z[:, 0:1]
    hf = (jnp.dot(a, w3f_ref[...], precision=hp, preferred_element_type=F32) + b3f_ref[...]) \
        * jnp.exp(-t_norm * jnp.exp(rf_ref[...]))
    hb = (jnp.dot(a, w3b_ref[...], precision=hp, preferred_element_type=F32) + b3b_ref[...]) \
        * jnp.exp(-t_norm * jnp.exp(rb_ref[...]))
    inv = lax.rsqrt(jnp.sum(hf * hf, axis=0, keepdims=True) + jnp.sum(hb * hb, axis=0, keepdims=True))
    hf = hf * inv
    r = lax.broadcasted_iota(jnp.int32, (seq_len, 1), 0)
    hb = jnp.where(r == 0, 0.0, hb * inv)
    sh, sm, sl = _split3(hf + hb)
    dh, dm, dl = _split3(hf - hb)
    fhi = fhi_ref[...]
    flo = flo_ref[...]
    kc = _dot(fhi[:seq_len], sh) + _dot(fhi[:seq_len], sm) + _dot(fhi[:seq_len], sl) \
        + _dot(flo[:seq_len], sh) + _dot(flo[:seq_len], sm)
    ks = _dot(fhi[seq_len:], dh) + _dot(fhi[seq_len:], dm) + _dot(fhi[seq_len:], dl) \
        + _dot(flo[seq_len:], dh) + _dot(flo[seq_len:], dm)
    sign = jnp.where(r % 2 == 0, 1.0, -1.0)
    k_nyq = jnp.sum(sign * (hf + hb), axis=0, keepdims=True)
    ks = jnp.where(r == 0, k_nyq, ks)
    scale = jnp.where(r == 0, 1.0 / n, 2.0 / n)
    kf_ref[0:seq_len, :] = kc * scale
    kf_ref[seq_len:n, :] = ks * scale


def _hyena_filters(seq_len, w1, b1, w2, b2, w3, b3, freq, log_rate, f_tab):
    d = D_MODEL
    cb = 256
    t = jnp.arange(seq_len, dtype=F32)
    t_norm = t / (seq_len - 1)
    bands = jnp.linspace(1e-4, HY_BANDS - 1, HY_BANDS, dtype=F32)
    ang = (2.0 * np.pi / seq_len) * t[:, None] * bands[None, :]
    z = jnp.concatenate([t_norm[:, None], jnp.cos(ang), jnp.sin(ang)], axis=-1)
    kpad = 128 - HY_EMB
    z = jnp.pad(z, ((0, 0), (0, kpad)))
    w1p = jnp.pad(w1, ((0, kpad), (0, 0)))
    f_hi = f_tab.astype(BF16)
    f_lo = (f_tab - f_hi.astype(F32)).astype(BF16)
    n_cb = d // cb
    hh = HY_HIDDEN
    row = lambda a: a.reshape(1, -1)
    const = lambda shape: pl.BlockSpec(shape, lambda o, j: (0,) * len(shape))
    fwd = lambda rows: pl.BlockSpec((rows, cb), lambda o, j: (0, o * n_cb + j))
    bwd = lambda rows: pl.BlockSpec((rows, cb), lambda o, j: (0, (HY_ORDER + o) * n_cb + j))
    return pl.pallas_call(
        functools.partial(_filter_kernel, seq_len),
        grid=(HY_ORDER, n_cb),
        in_specs=[const((seq_len, 128)), const((128, hh)), const((1, hh)), const((hh, hh)), const((1, hh)),
                  fwd(hh), bwd(hh), fwd(1), bwd(1),
                  const((1, hh)), const((1, hh)), fwd(1), bwd(1),
                  const((2 * seq_len, seq_len)), const((2 * seq_len, seq_len))],
        out_specs=pl.BlockSpec((None, 2 * seq_len, cb), lambda o, j: (o, 0, j)),
        out_shape=jax.ShapeDtypeStruct((HY_ORDER, 2 * seq_len, d), F32),
        compiler_params=_cparams("parallel", "parallel"),
        name=f"hyena_filter_l{seq_len}",
    )(z, w1p, row(b1), w2, row(b2), w3, w3, row(b3), row(b3),
      row(freq[0]), row(freq[1]), row(log_rate), row(log_rate), f_hi, f_lo)


def _hyena_kernel(seq_len, v_ref, x1_ref, x2_ref, cwv_ref, cw1_ref, cw2_ref, bias_ref, kf_ref,
                  f_ref, ft_ref, o_ref):
    r = lax.broadcasted_iota(jnp.int32, (seq_len, 1), 0)
    z = _short_conv3(v_ref[...], cwv_ref[...])
    gates = (_short_conv3(x1_ref[...], cw1_ref[...]), _short_conv3(x2_ref[...], cw2_ref[...]))
    for o in range(HY_ORDER):
        zf = _dot(f_ref[...], z.astype(BF16))
        a, bm = zf[:seq_len], zf[seq_len:]
        kc, ks = kf_ref[o, 0:seq_len, :], kf_ref[o, seq_len:2 * seq_len, :]
        yc = a * kc - jnp.where(r == 0, 0.0, bm * ks)
        ys = jnp.where(r == 0, bm * ks, a * ks + bm * kc)
        y = _dot(ft_ref[:, 0:seq_len], yc.astype(BF16)) + _dot(ft_ref[:, seq_len:2 * seq_len], ys.astype(BF16))
        z = gates[o] * (y + z * bias_ref[o:o + 1, :])
    o_ref[...] = z


def _hyena(u, conv_w, bias, kf, f_tab, n_seq, seq_len, tok_offset):
    d = D_MODEL
    cb = 256
    n_cb = d // cb
    row0 = tok_offset // seq_len
    f_bf = f_tab.astype(BF16)

    def part(k):
        return pl.BlockSpec((seq_len, cb), lambda j, b: (row0 + b, k * n_cb + j))

    def cw(k):
        return pl.BlockSpec((3, cb), lambda j, b: (0, k * n_cb + j))

    return pl.pallas_call(
        functools.partial(_hyena_kernel, seq_len),
        grid=(n_cb, n_seq),
        in_specs=[part(0), part(1), part(2), cw(0), cw(1), cw(2),
                  pl.BlockSpec((HY_ORDER, cb), lambda j, b: (0, j)),
                  pl.BlockSpec((HY_ORDER, 2 * seq_len, cb), lambda j, b: (0, 0, j)),
                  pl.BlockSpec((2 * seq_len, seq_len), lambda j, b: (0, 0)),
                  pl.BlockSpec((seq_len, 2 * seq_len), lambda j, b: (0, 0))],
        out_specs=pl.BlockSpec((seq_len, cb), lambda j, b: (b, j)),
        out_shape=jax.ShapeDtypeStruct((n_seq * seq_len, d), F32),
        compiler_params=_cparams("parallel", "parallel"),
        name=f"hyena_l{seq_len}",
    )(u, u, u, conv_w, conv_w, conv_w, bias, kf, f_bf, f_bf.T)


def _mix_out_kernel(n_parts, *refs):
    o_refs = refs[:n_parts]
    w_refs = refs[n_parts:2 * n_parts]
    x_ref, mod_ref, g_ref, wr_hi_ref, wr_lo_ref, br_ref = refs[2 * n_parts:2 * n_parts + 6]
    x1_ref, h_ref, eid_ref, wts_ref, rank_ref, cnt_ref, run_ref = refs[2 * n_parts + 6:]
    i = pl.program_id(0)
    t = TOK_TILE
    ne = N_EXPERTS

    y = _dot(o_refs[0][...].astype(BF16), w_refs[0][...])
    for j in range(1, n_parts):
        y = y + _dot(o_refs[j][...].astype(BF16), w_refs[j][...])
    x1 = x_ref[...] + mod_ref[2:3, :] * y
    x1_ref[...] = x1
    h = _norm_mod(x1, g_ref[...], mod_ref[4:5, :], mod_ref[3:4, :])
    h_ref[...] = h
    h_hi = h.astype(BF16)
    h_lo = (h - h_hi.astype(F32)).astype(BF16)
    logits = _dot_nt(wr_hi_ref[...], h_hi) + _dot_nt(wr_lo_ref[...], h_hi) + _dot_nt(wr_hi_ref[...], h_lo) \
        + br_ref[...]

    @pl.when(i == 0)
    def _():
        run_ref[...] = jnp.zeros_like(run_ref)

    e_iota = lax.broadcasted_iota(jnp.int32, (ne, t), 0)
    r2 = lax.broadcasted_iota(jnp.int32, (t, t), 0)
    c2 = lax.broadcasted_iota(jnp.int32, (t, t), 1)
    before = jnp.where(r2 < c2, 1.0, 0.0).astype(BF16)
    running = run_ref[...]
    vals, eids, ranks = [], [], []
    for _k in range(TOP_K):
        m = jnp.max(logits, axis=0, keepdims=True)
        eid = jnp.min(jnp.where(logits == m, e_iota, ne), axis=0, keepdims=True)
        sel = e_iota == eid
        logits = jnp.where(sel, -jnp.inf, logits)
        onehot = jnp.where(sel, 1.0, 0.0)
        earlier = _dot(onehot.astype(BF16), before)
        ranks.append(jnp.sum(onehot * (running + earlier), axis=0, keepdims=True))
        running = running + jnp.sum(onehot, axis=1, keepdims=True)
        vals.append(m)
        eids.append(eid)
    run_ref[...] = running
    cnt_ref[...] = running
    v = jnp.concatenate(vals, axis=0)
    ex = jnp.exp(v - v[0:1])
    wts_ref[...] = ex / jnp.sum(ex, axis=0, keepdims=True)
    eid_ref[...] = jnp.concatenate(eids, axis=0)
    rank_ref[...] = jnp.concatenate(ranks, axis=0).astype(jnp.int32)


def _mix_out(parts, x, mod_l, norm_g, w_router, b_router):
    d = D_MODEL
    t = TOK_TILE
    ne = N_EXPERTS
    n_parts = len(parts)
    wr_t = w_router.T
    wr_hi = wr_t.astype(BF16)
    wr_lo = (wr_t - wr_hi.astype(F32)).astype(BF16)
    in_specs = [pl.BlockSpec((t, o.shape[1]), lambda i: (i, 0)) for o, _ in parts]
    in_specs += [pl.BlockSpec(w.shape, lambda i: (0, 0)) for _, w in parts]
    in_specs += [
        pl.BlockSpec((t, d), lambda i: (i, 0)),
        pl.BlockSpec((None, N_MOD, d), lambda i: (_tile_cond_row(i), 0, 0)),
        pl.BlockSpec((1, d), lambda i: (0, 0)),
        pl.BlockSpec((ne, d), lambda i: (0, 0)),
        pl.BlockSpec((ne, d), lambda i: (0, 0)),
        pl.BlockSpec((ne, 1), lambda i: (0, 0)),
    ]
    tok_major = pl.BlockSpec((TOP_K, t), lambda i: (0, i))
    return pl.pallas_call(
        functools.partial(_mix_out_kernel, n_parts),
        grid=(N_TOK // t,),
        in_specs=in_specs,
        out_specs=[pl.BlockSpec((t, d), lambda i: (i, 0)), pl.BlockSpec((t, d), lambda i: (i, 0)),
                   tok_major, tok_major, tok_major, pl.BlockSpec((ne, 1), lambda i: (0, 0))],
        out_shape=[jax.ShapeDtypeStruct((N_TOK, d), F32), jax.ShapeDtypeStruct((N_TOK, d), F32),
                   jax.ShapeDtypeStruct((TOP_K, N_TOK), jnp.int32), jax.ShapeDtypeStruct((TOP_K, N_TOK), F32),
                   jax.ShapeDtypeStruct((TOP_K, N_TOK), jnp.int32), jax.ShapeDtypeStruct((ne, 1), F32)],
        scratch_shapes=[pltpu.VMEM((ne, 1), F32)],
        compiler_params=_cparams("arbitrary"),
        name="mix_out_router",
    )(*[o for o, _ in parts], *[w.astype(BF16) for _, w in parts], x, mod_l, norm_g.reshape(1, d),
      wr_hi, wr_lo, b_router.reshape(ne, 1))


def _row_copy_kernel(scatter, n_src, idx_ref, src_ref, *rest):
    dst_ref, sem = rest[-2], rest[-1]
    base = pl.program_id(0) * COPY_ROWS

    def copy(j):
        slot = base + j
        moved = idx_ref[0, 0, j]
        s, d = (lax.rem(slot, n_src), moved) if scatter else (moved, slot)
        return pltpu.make_async_copy(src_ref.at[pl.ds(s, 1)], dst_ref.at[pl.ds(d, 1)], sem)

    def body(j, carry):
        @pl.when(j >= COPY_WINDOW)
        def _():
            copy(j - COPY_WINDOW).wait()
        copy(j).start()
        return carry

    lax.fori_loop(0, COPY_ROWS, body, 0)

    def drain(j, carry):
        copy(j).wait()
        return carry

    lax.fori_loop(COPY_ROWS - COPY_WINDOW, COPY_ROWS, drain, 0)


def _row_scatter(src, idx, dst_init):
    n = idx.shape[0]
    steps = n // COPY_ROWS
    return pl.pallas_call(
        functools.partial(_row_copy_kernel, True, src.shape[0]),
        grid=(steps,),
        in_specs=[pl.BlockSpec((1, 1, COPY_ROWS), lambda i: (i, 0, 0), memory_space=pltpu.SMEM),
                  pl.BlockSpec(memory_space=pl.ANY),
                  pl.BlockSpec(memory_space=pl.ANY)],
        out_specs=pl.BlockSpec(memory_space=pl.ANY),
        out_shape=jax.ShapeDtypeStruct(dst_init.shape, dst_init.dtype),
        scratch_shapes=[pltpu.SemaphoreType.DMA(())],
        input_output_aliases={2: 0},
        compiler_params=pltpu.CompilerParams(dimension_semantics=("arbitrary",)),
        name="row_scatter",
    )(idx.reshape(steps, 1, COPY_ROWS), src, dst_init)


def _row_gather(src, idx):
    n = idx.shape[0]
    steps = n // COPY_ROWS
    return pl.pallas_call(
        functools.partial(_row_copy_kernel, False, src.shape[0]),
        grid=(steps,),
        in_specs=[pl.BlockSpec((1, 1, COPY_ROWS), lambda i: (i, 0, 0), memory_space=pltpu.SMEM),
                  pl.BlockSpec(memory_space=pl.ANY)],
        out_specs=pl.BlockSpec(memory_space=pl.ANY),
        out_shape=jax.ShapeDtypeStruct((n, src.shape[1]), src.dtype),
        scratch_shapes=[pltpu.SemaphoreType.DMA(())],
        compiler_params=pltpu.CompilerParams(dimension_semantics=("arbitrary",)),
        name="row_gather",
    )(idx.reshape(steps, 1, COPY_ROWS), src)


def _experts_kernel(te_ref, nv_ref, x_ref, wg_ref, wu_ref, bg_ref, bu_ref, wd_ref, bd_ref, y_ref):
    i = pl.program_id(0)

    @pl.when(i < nv_ref[0])
    def _():
        x = x_ref[...].astype(BF16)
        gl = _dot(x, wg_ref[...]) + bg_ref[...]
        up = _dot(x, wu_ref[...]) + bu_ref[...]
        gl = jnp.minimum(gl, SWIGLU_LIMIT)
        up = jnp.clip(up, -SWIGLU_LIMIT, SWIGLU_LIMIT)
        act = (up + 1.0) * gl * _sigmoid(SWIGLU_ALPHA * gl)
        y_ref[...] = _dot(act.astype(BF16), wd_ref[...]) + bd_ref[...]

    @pl.when(i >= nv_ref[0])
    def _():
        y_ref[...] = jnp.zeros_like(y_ref)


def _experts(xs, tile_expert, n_valid, w_gate, w_up, b_gate, b_up, w_down, b_down):
    d = D_MODEL
    tm = MOE_TILE
    n_tiles = MOE_ROWS // tm
    ff = w_gate.shape[-1]
    wspec = lambda k, n: pl.BlockSpec((None, k, n), lambda i, te, nv: (te[i], 0, 0))
    grid_spec = pltpu.PrefetchScalarGridSpec(
        num_scalar_prefetch=2,
        grid=(n_tiles,),
        in_specs=[pl.BlockSpec((tm, d), lambda i, te, nv: (i, 0)),
                  wspec(d, ff), wspec(d, ff), wspec(1, ff), wspec(1, ff), wspec(ff, d), wspec(1, d)],
        out_specs=pl.BlockSpec((tm, d), lambda i, te, nv: (i, 0)),
    )
    return pl.pallas_call(
        _experts_kernel,
        grid_spec=grid_spec,
        out_shape=jax.ShapeDtypeStruct((MOE_ROWS, d), F32),
        compiler_params=_cparams("arbitrary"),
        name="experts",
    )(tile_expert, n_valid, xs, w_gate, w_up, b_gate, b_up, w_down, b_down)


def _combine_kernel(final, x_ref, g_ref, w_ref, mod_ref, fg_ref, o_ref):
    y = w_ref[:, 0:1] * g_ref[0]
    for k in range(1, TOP_K):
        y = y + w_ref[:, k:k + 1] * g_ref[k]
    x = x_ref[...] + mod_ref[5:6, :] * y
    if final:
        x = x * lax.rsqrt(jnp.mean(x * x, axis=-1, keepdims=True) + RMS_EPS) * fg_ref[...]
    o_ref[...] = x


def _combine(x1, gathered, wts, mod_l, final_g, final):
    d = D_MODEL
    t = TOK_TILE
    return pl.pallas_call(
        functools.partial(_combine_kernel, final),
        grid=(N_TOK // t,),
        in_specs=[pl.BlockSpec((t, d), lambda i: (i, 0)),
                  pl.BlockSpec((TOP_K, t, d), lambda i: (0, i, 0)),
                  pl.BlockSpec((t, TOP_K), lambda i: (i, 0)),
                  pl.BlockSpec((None, N_MOD, d), lambda i: (_tile_cond_row(i), 0, 0)),
                  pl.BlockSpec((1, d), lambda i: (0, 0))],
        out_specs=pl.BlockSpec((t, d), lambda i: (i, 0)),
        out_shape=jax.ShapeDtypeStruct((N_TOK, d), F32),
        compiler_params=_cparams("parallel"),
        name="moe_combine",
    )(x1, gathered, wts, mod_l, final_g.reshape(1, d))


def _moe(x1, h, eid, wts, rank, counts, mod_l, w_gu, b_gu, w_down, b_down, final_g, final):
    d = D_MODEL
    tm = MOE_TILE
    n_tiles = MOE_ROWS // tm
    cnt = counts.reshape(N_EXPERTS).astype(jnp.int32)
    gsz = ((cnt + tm - 1) // tm) * tm
    ends = jnp.cumsum(gsz)
    offs = ends - gsz
    pos = offs[eid] + rank
    tile_start = jnp.arange(n_tiles, dtype=jnp.int32) * tm
    tile_expert = jnp.minimum(jnp.sum((ends[None, :] <= tile_start[:, None]).astype(jnp.int32), axis=1),
                              N_EXPERTS - 1)
    n_valid = (ends[-1:] // tm).astype(jnp.int32)
    last_valid = jnp.maximum(n_valid[0] - 1, 0)
    tile_expert = jnp.where(jnp.arange(n_tiles) < n_valid[0], tile_expert, tile_expert[last_valid])

    xs = _row_scatter(h, pos.reshape(-1), jnp.zeros((MOE_ROWS, d), h.dtype))
    ys = _experts(xs, tile_expert, n_valid,
                  w_gu[:, :, 0::2].astype(BF16), w_gu[:, :, 1::2].astype(BF16),
                  b_gu[:, None, 0::2], b_gu[:, None, 1::2], w_down.astype(BF16), b_down[:, None, :])
    gathered = _row_gather(ys, pos.reshape(-1))
    return _combine(x1, gathered.reshape(TOP_K, N_TOK, d), wts.T, mod_l, final_g, final)


def _grid_positions(n_tok, d):
    rows = n_tok // GRID_W
    r, col = jnp.meshgrid(jnp.arange(rows, dtype=F32), jnp.arange(GRID_W, dtype=F32), indexing='ij')
    r = r.reshape(-1)
    col = col.reshape(-1)
    quarter = d // 4
    inv = 1.0 / (10000.0 ** (jnp.arange(quarter, dtype=F32) / quarter))
    ar = r[:, None] * inv[None]
    ac = col[:, None] * inv[None]
    return jnp.concatenate([jnp.sin(ar), jnp.cos(ar), jnp.sin(ac), jnp.cos(ac)], axis=-1)


def kernel(x_prompt, x_sample, state_hgrn, state_mlstm_c, state_mlstm_n, state_mlstm_m, c, c_ctx,
           norm_g, final_g, w_mod, b_mod, ev_w_in, ev_gate_b, ev_conv, hg_lb, ev_w_out,
           hy_w_in, hy_conv, hy_w1, hy_b1, hy_w2, hy_b2, hy_w3, hy_b3, hy_freq, hy_log_rate, hy_bias, hy_w_out,
           w_router, b_router, w_gu, b_gu, w_down, b_down):
    d = D_MODEL
    hd = HEAD_DIM
    cond = jnp.concatenate([c_ctx[None], c, jnp.zeros((N_COND - 1 - DEC_BATCH, d), F32)], axis=0)
    mod = _modulation(cond, w_mod, b_mod)
    x = jnp.concatenate([x_prompt.reshape(N_PROMPT, d), x_sample.reshape(N_SAMPLE, d)], axis=0)
    pos_tab = jnp.concatenate([jnp.zeros((TOK_TILE, d), F32), _grid_positions(DEC_SEQ, d)], axis=0)

    groups = ((BATCH, SEQ, 0), (DEC_BATCH, DEC_SEQ, N_PROMPT))
    new_states = None
    for l in range(DEPTH):
        if l % 2 == 0:
            e = l // 2
            if l == 0:
                x, p, gate, gate_t = _proj_even(x, pos_tab, mod[l], norm_g[l, 0], ev_w_in[e], ev_gate_b[e])
            else:
                raise NotImplementedError("only the first layer adds grid positions")
            gate_h = gate.reshape(N_TOK, 4, HEADS).transpose(2, 0, 1)
            gate_t_h = gate_t.reshape(4, HEADS, N_TOK).transpose(1, 0, 2)
            o_hg, o_ml = [], []
            for gi, (n_seq, seq_len, off) in enumerate(groups):
                if gi == 0:
                    s0 = jnp.zeros((n_seq, 2, HEADS, hd, hd), F32)
                    c0 = jnp.zeros((n_seq, 2, HEADS, hd, hd), F32)
                    n0 = jnp.zeros((n_seq, 2, HEADS, 1, hd), F32)
                    m0 = jnp.zeros((n_seq, 2, HEADS, 1, 1), F32)
                else:
                    s0 = state_hgrn[:, e]
                    c0 = state_mlstm_c[:, e]
                    n0 = state_mlstm_n[:, e].reshape(n_seq, 2, HEADS, 1, hd)
                    m0 = state_mlstm_m[:, e].reshape(n_seq, 2, HEADS, 1, 1)
                og, s_fin = _hgrn(p, hg_lb, l, s0, n_seq, seq_len, off)
                om, c_fin, n_fin, m_fin = _mlstm(p, gate_h, gate_t_h, ev_conv[e], c0, n0, m0, n_seq, seq_len, off)
                o_hg.append(og)
                o_ml.append(om)
                if gi == 0:
                    new_states = (s_fin[:, None], c_fin[:, None],
                                  n_fin.reshape(n_seq, 1, 2, HEADS, hd), m_fin.reshape(n_seq, 1, 2, HEADS))
            parts = [(jnp.concatenate(o_hg, axis=0), ev_w_out[e][:GROUP_W]),
                     (jnp.concatenate(o_ml, axis=0), ev_w_out[e][GROUP_W:])]
        else:
            o = l // 2
            u = _proj_odd(x, mod[l], norm_g[l, 0], hy_w_in[o])
            zs = []
            for n_seq, seq_len, off in groups:
                f_tab = _dft_tables(seq_len)
                kf = _hyena_filters(seq_len, hy_w1[o], hy_b1[o], hy_w2[o], hy_b2[o], hy_w3[o], hy_b3[o],
                                    hy_freq[o], hy_log_rate[o], f_tab)
                zs.append(_hyena(u, hy_conv[o], hy_bias[o], kf, f_tab, n_seq, seq_len, off))
            parts = [(jnp.concatenate(zs, axis=0), hy_w_out[o])]
        x1, h, eid, wts, rank, counts = _mix_out(parts, x, mod[l], norm_g[l, 1], w_router[l], b_router[l])
        x = _moe(x1, h, eid, wts, rank, counts, mod[l], w_gu[l], b_gu[l], w_down[l], b_down[l],
                 final_g, final=(l == DEPTH - 1))

    y_prompt = x[:N_PROMPT].reshape(BATCH, SEQ, d)
    y_sample = x[N_PROMPT:].reshape(DEC_BATCH, DEC_SEQ, d)
    return (y_prompt, y_sample) + new_states
```

```python
import functools

import numpy as np
import jax
import jax.numpy as jnp
from jax import lax
from jax.experimental import pallas as pl
from jax.experimental.pallas import tpu as pltpu

F32 = jnp.float32
BF16 = jnp.bfloat16

D_MODEL = 1024
BATCH = 32
SEQ = 256
DEPTH = 2
DEC_BATCH = 8
DEC_SEQ = 1024
GRID_W = 64
RMS_EPS = 1e-6
N_MOD = 6

HEADS = 4
HEAD_DIM = 128
GROUP_W = HEADS * HEAD_DIM
N_GATES = 4 * HEADS
EVEN_MAIN = 9 * GROUP_W

HY_ORDER = 2
HY_BANDS = 16
HY_EMB = 1 + 2 * HY_BANDS
HY_HIDDEN = 64

N_EXPERTS = 32
TOP_K = 4
SWIGLU_LIMIT = 7.0
SWIGLU_ALPHA = 1.702

N_PROMPT = BATCH * SEQ
N_SAMPLE = DEC_BATCH * DEC_SEQ
N_TOK = N_PROMPT + N_SAMPLE
N_COND = 16

TOK_TILE = 256
SCAN_CHUNK = 128
SUB = 16
MOE_TILE = 256
MOE_ROWS = N_TOK * TOP_K + N_EXPERTS * MOE_TILE
DEINT_COLS = 512
COPY_ROWS = 2048
COPY_UNROLL = 8

VMEM_LIMIT = 56 * 1024 * 1024


def _cparams(*sem):
    return pltpu.CompilerParams(dimension_semantics=sem, vmem_limit_bytes=VMEM_LIMIT)


def _split3(x):
    hi = x.astype(BF16)
    r = x - hi.astype(F32)
    mid = r.astype(BF16)
    lo = (r - mid.astype(F32)).astype(BF16)
    return hi, mid, lo


def _dot(a, b):
    return jnp.dot(a, b, preferred_element_type=F32)


def _dot_nt(a, b):
    return lax.dot_general(a, b, (((1,), (1,)), ((), ())), preferred_element_type=F32)


def _dot_tn(a, b):
    return lax.dot_general(a, b, (((0,), (0,)), ((), ())), preferred_element_type=F32)


def _dot_w3(a_exact_bf16, x):
    hi, mid, lo = _split3(x)
    return _dot(a_exact_bf16, hi) + _dot(a_exact_bf16, mid) + _dot(a_exact_bf16, lo)


def _sigmoid(x):
    return 1.0 / (1.0 + jnp.exp(-x))


def _silu(x):
    return x * _sigmoid(x)


def _log_sigmoid(x):
    return jnp.minimum(x, 0.0) - jnp.log(1.0 + jnp.exp(-jnp.abs(x)))


def _tile_cond_row(i):
    n_prompt_tiles = N_PROMPT // TOK_TILE
    tiles_per_seq = DEC_SEQ // TOK_TILE
    return jnp.where(i < n_prompt_tiles, 0, 1 + (i - n_prompt_tiles) // tiles_per_seq)


def _mod_kernel(cond_ref, w_ref, b_ref, o_ref):
    a = _silu(cond_ref[...]).astype(BF16)
    o_ref[...] = _dot(a, w_ref[...].astype(BF16)) + b_ref[...]


def _modulation(cond, w_mod, b_mod):
    d = D_MODEL
    out = pl.pallas_call(
        _mod_kernel,
        grid=(DEPTH, N_MOD),
        in_specs=[
            pl.BlockSpec((N_COND, d), lambda l, j: (0, 0)),
            pl.BlockSpec((None, d, d), lambda l, j: (l, 0, j)),
            pl.BlockSpec((None, 1, d), lambda l, j: (l, 0, j)),
        ],
        out_specs=pl.BlockSpec((None, None, N_COND, d), lambda l, j: (l, j, 0, 0)),
        out_shape=jax.ShapeDtypeStruct((DEPTH, N_MOD, N_COND, d), F32),
        compiler_params=_cparams("parallel", "parallel"),
        name="modulation",
    )(cond, w_mod, b_mod.reshape(DEPTH, 1, N_MOD * d))
    return out.transpose(0, 2, 1, 3)


def _norm_mod(x, g_row, scale_row, shift_row):
    ms = jnp.mean(x * x, axis=-1, keepdims=True)
    y = x * lax.rsqrt(ms + RMS_EPS) * g_row
    return y * (1.0 + scale_row) + shift_row


def _proj_even_kernel(x_ref, pos_ref, mod_ref, g_ref, w_ref, wg_ref, wgt_ref, gb_ref, gbt_ref,
                      xres_ref, p_ref, gate_ref, gate_t_ref):
    x = x_ref[...] + pos_ref[...]
    xres_ref[...] = x
    h = _norm_mod(x, g_ref[...], mod_ref[1:2, :], mod_ref[0:1, :]).astype(BF16)
    p_ref[...] = _dot(h, w_ref[...])
    gate_ref[...] = _dot(h, wg_ref[...]) + gb_ref[...]
    gate_t_ref[...] = _dot_nt(wgt_ref[...], h) + gbt_ref[...]


def _proj_even(x, pos_tab, mod_l, norm_g, w_in, gate_b):
    d = D_MODEL
    n_tiles = N_TOK // TOK_TILE
    n_prompt_tiles = N_PROMPT // TOK_TILE
    tiles_per_seq = DEC_SEQ // TOK_TILE
    w_main = w_in[:, :EVEN_MAIN].astype(BF16)
    w_gate = w_in[:, EVEN_MAIN:].astype(BF16)

    def pos_map(i):
        return (jnp.where(i < n_prompt_tiles, 0, 1 + (i - n_prompt_tiles) % tiles_per_seq), 0)

    return pl.pallas_call(
        _proj_even_kernel,
        grid=(n_tiles,),
        in_specs=[
            pl.BlockSpec((TOK_TILE, d), lambda i: (i, 0)),
            pl.BlockSpec((TOK_TILE, d), pos_map),
            pl.BlockSpec((None, N_MOD, d), lambda i: (_tile_cond_row(i), 0, 0)),
            pl.BlockSpec((1, d), lambda i: (0, 0)),
            pl.BlockSpec((d, EVEN_MAIN), lambda i: (0, 0)),
            pl.BlockSpec((d, N_GATES), lambda i: (0, 0)),
            pl.BlockSpec((N_GATES, d), lambda i: (0, 0)),
            pl.BlockSpec((1, N_GATES), lambda i: (0, 0)),
            pl.BlockSpec((N_GATES, 1), lambda i: (0, 0)),
        ],
        out_specs=[
            pl.BlockSpec((TOK_TILE, d), lambda i: (i, 0)),
            pl.BlockSpec((TOK_TILE, EVEN_MAIN), lambda i: (i, 0)),
            pl.BlockSpec((TOK_TILE, N_GATES), lambda i: (i, 0)),
            pl.BlockSpec((N_GATES, TOK_TILE), lambda i: (0, i)),
        ],
        out_shape=[
            jax.ShapeDtypeStruct((N_TOK, d), F32),
            jax.ShapeDtypeStruct((N_TOK, EVEN_MAIN), F32),
            jax.ShapeDtypeStruct((N_TOK, N_GATES), F32),
            jax.ShapeDtypeStruct((N_GATES, N_TOK), F32),
        ],
        compiler_params=_cparams("parallel"),
        name="proj_even",
    )(x, pos_tab, mod_l, norm_g.reshape(1, d), w_main, w_gate, w_gate.T,
      gate_b.reshape(1, N_GATES), gate_b.reshape(N_GATES, 1))


def _proj_odd_kernel(x_ref, mod_ref, g_ref, w_ref, p_ref):
    h = _norm_mod(x_ref[...], g_ref[...], mod_ref[1:2, :], mod_ref[0:1, :]).astype(BF16)
    p_ref[...] = _dot(h, w_ref[...])


def _proj_odd(x, mod_l, norm_g, w_in):
    d = D_MODEL
    width = w_in.shape[1]
    return pl.pallas_call(
        _proj_odd_kernel,
        grid=(N_TOK // TOK_TILE,),
        in_specs=[
            pl.BlockSpec((TOK_TILE, d), lambda i: (i, 0)),
            pl.BlockSpec((None, N_MOD, d), lambda i: (_tile_cond_row(i), 0, 0)),
            pl.BlockSpec((1, d), lambda i: (0, 0)),
            pl.BlockSpec((d, width), lambda i: (0, 0)),
        ],
        out_specs=pl.BlockSpec((TOK_TILE, width), lambda i: (i, 0)),
        out_shape=jax.ShapeDtypeStruct((N_TOK, width), F32),
        compiler_params=_cparams("parallel"),
        name="proj_odd",
    )(x, mod_l, norm_g.reshape(1, d), w_in.astype(BF16))


def _hgrn_chunk(reverse, q, k, v, lf, st):
    c = SCAN_CHUNK
    row = lax.broadcasted_iota(jnp.int32, (c, c), 0)
    col = lax.broadcasted_iota(jnp.int32, (c, c), 1)
    tri = (col >= row) if reverse else (col <= row)
    b = _dot_w3(jnp.where(tri, 1.0, 0.0).astype(BF16), lf)

    lane = lax.broadcasted_iota(jnp.int32, (SUB, c), 1)
    sub_row = lax.broadcasted_iota(jnp.int32, (SUB, 1), 0)
    chunk_row = lax.broadcasted_iota(jnp.int32, (c, 1), 0)
    rows = []
    for i in range(c // SUB):
        lo, hi = i * SUB, (i + 1) * SUB
        qi, ki, bi = q[lo:hi], k[lo:hi], b[lo:hi]
        if reverse:
            has_off, edge = hi < c, hi
            outside = chunk_row >= hi
        else:
            has_off, edge = lo > 0, lo - 1
            outside = chunk_row < lo
        if has_off:
            beta = b[edge:edge + 1]
            qs = qi * jnp.exp(bi - beta)
            ks = k * jnp.exp(jnp.where(outside, beta - b, -jnp.inf))
            a_row = _dot_nt(qs.astype(BF16), ks.astype(BF16))
        else:
            a_row = jnp.zeros((SUB, c), F32)
        for s in range(SUB):
            valid = (sub_row <= s) if reverse else (sub_row >= s)
            dlog = jnp.where(valid, bi - bi[s:s + 1], -jnp.inf)
            a_col = jnp.sum(jnp.exp(dlog) * qi * ki[s:s + 1], axis=-1, keepdims=True)
            a_row = jnp.where(lane == lo + s, a_col, a_row)
        rows.append(a_row)
    attn = jnp.concatenate(rows, axis=0)
    o = _dot(attn.astype(BF16), v.astype(BF16)) + _dot_nt((q * jnp.exp(b)).astype(BF16), st.astype(BF16))
    b_exit = b[0:1] if reverse else b[c - 1:c]
    k_out = k * jnp.exp(b_exit - b)
    st_new = jnp.exp(b_exit) * st + _dot_tn(v.astype(BF16), k_out.astype(BF16))
    return o, st_new


def _hgrn_kernel(seq_len, layer, q_ref, i_ref, g_ref, ff_ref, fb_ref, lb_ref, s0_ref,
                 o_ref, s_out_ref, of_ref, ob_ref, st_ref):
    c = SCAN_CHUNK
    n_chunks = seq_len // c
    lbp = lb_ref[...]
    e = jnp.exp(lbp - jnp.max(lbp, axis=0, keepdims=True))
    lb = jnp.sum(e[0:layer + 1], axis=0, keepdims=True) / jnp.sum(e, axis=0, keepdims=True)

    st_ref[0] = s0_ref[0].T
    st_ref[1] = s0_ref[1].T

    def body(n, carry):
        for reverse in (False, True):
            d = 1 if reverse else 0
            base = pl.multiple_of((n_chunks - 1 - n if reverse else n) * c, c)
            sl = pl.ds(base, c)
            f = lb + (1.0 - lb) * _sigmoid((fb_ref if reverse else ff_ref)[sl, :])
            o, st_new = _hgrn_chunk(reverse, q_ref[sl, :], 1.0 - f, i_ref[sl, :], jnp.log(f), st_ref[d])
            st_ref[d] = st_new
            (ob_ref if reverse else of_ref)[sl, :] = o
        return carry

    lax.fori_loop(0, n_chunks, body, 0)
    o = of_ref[...] + ob_ref[...]
    o = o * lax.rsqrt(jnp.mean(o * o, axis=-1, keepdims=True) + RMS_EPS)
    o_ref[...] = o * _silu(g_ref[...])
    s_out_ref[0] = st_ref[0].T
    s_out_ref[1] = st_ref[1].T


def _hgrn(p, hg_lb, layer, s0, n_seq, seq_len, tok_offset):
    hd = HEAD_DIM
    row0 = tok_offset // seq_len

    def col(part):
        return pl.BlockSpec((seq_len, hd), lambda b, h: (row0 + b, part * HEADS + h))

    state_spec = pl.BlockSpec((None, 2, None, hd, hd), lambda b, h: (b, 0, h, 0, 0))
    return pl.pallas_call(
        functools.partial(_hgrn_kernel, seq_len, layer),
        grid=(n_seq, HEADS),
        in_specs=[col(0), col(1), col(2), col(3), col(4),
                  pl.BlockSpec((DEPTH + 1, hd), lambda b, h: (0, h)),
                  state_spec],
        out_specs=[pl.BlockSpec((seq_len, hd), lambda b, h: (b, h)), state_spec],
        out_shape=[jax.ShapeDtypeStruct((n_seq * seq_len, GROUP_W), F32),
                   jax.ShapeDtypeStruct((n_seq, 2, HEADS, hd, hd), F32)],
        scratch_shapes=[pltpu.VMEM((seq_len, hd), F32), pltpu.VMEM((seq_len, hd), F32),
                        pltpu.VMEM((2, hd, hd), F32)],
        compiler_params=_cparams("parallel", "parallel"),
        name=f"hgrn_l{seq_len}",
    )(p, p, p, p, p, hg_lb, s0)


def _short_conv3(x, w):
    n = x.shape[0]
    r = lax.broadcasted_iota(jnp.int32, (n, 1), 0)
    prev = jnp.where(r == 0, 0.0, pltpu.roll(x, 1, 0))
    nxt = jnp.where(r == n - 1, 0.0, pltpu.roll(x, n - 1, 0))
    return prev * w[0:1] + x * w[1:2] + nxt * w[2:3]


def _mlstm_chunk(reverse, q, k, v, ig_c, fg_c, ig_r, fg_r, cm, nv, m_prev):
    c = SCAN_CHUNK
    row = lax.broadcasted_iota(jnp.int32, (c, c), 0)
    col = lax.broadcasted_iota(jnp.int32, (c, c), 1)
    tri = (col >= row) if reverse else (col <= row)
    lf_c = _log_sigmoid(fg_c)
    lf_r = _log_sigmoid(fg_r)
    b_c = jnp.sum(jnp.where(tri, lf_r, 0.0), axis=-1, keepdims=True)
    tri_t = (row >= col) if reverse else (row <= col)
    b_r = jnp.sum(jnp.where(tri_t, lf_c, 0.0), axis=0, keepdims=True)
    dmat = jnp.where(tri, b_c - b_r + ig_r, -jnp.inf)
    m_t = jnp.maximum(b_c + m_prev, jnp.max(dmat, axis=-1, keepdims=True))
    qb, kb, vb = q.astype(BF16), k.astype(BF16), v.astype(BF16)
    p = jnp.exp(dmat - m_t) * _dot_nt(qb, kb)
    inter = jnp.exp(b_c + m_prev - m_t)
    num = inter * _dot(qb, cm.astype(BF16)) + _dot(p.astype(BF16), vb)
    den = inter * jnp.sum(q * nv, axis=-1, keepdims=True) + jnp.sum(p, axis=-1, keepdims=True)
    h = num / jnp.maximum(jnp.abs(den), jnp.exp(-m_t))
    last = 0 if reverse else c - 1
    m_new = m_t[last:last + 1]
    b_exit = b_c[last:last + 1]
    w_c = jnp.exp(b_exit - b_c + ig_c - m_new)
    dec = jnp.exp(b_exit + m_prev - m_new)
    kw = k * w_c
    cm_new = dec * cm + _dot_tn(kw.astype(BF16), vb)
    nv_new = dec * nv + jnp.sum(kw, axis=0, keepdims=True)
    return h, cm_new, nv_new, m_new


def _mlstm_kernel(seq_len, q_ref, k_ref, v_ref, og_ref, gate_ref, gate_t_ref, cwq_ref, cwk_ref,
                  c0_ref, n0_ref, m0_ref,
                  o_ref, c_out_ref, n_out_ref, m_out_ref,
                  q2_ref, k2_ref, hf_ref, hb_ref, c_ref, n_ref, m_ref):
    c = SCAN_CHUNK
    n_chunks = seq_len // c
    q2_ref[...] = _silu(_short_conv3(q_ref[...], cwq_ref[...]))
    k2_ref[...] = _silu(_short_conv3(k_ref[...], cwk_ref[...])) * (HEAD_DIM ** -0.5)
    c_ref[...] = c0_ref[...]
    n_ref[...] = n0_ref[...]
    m_ref[...] = m0_ref[...]

    def body(n, carry):
        for reverse in (False, True):
            d = 1 if reverse else 0
            base = pl.multiple_of((n_chunks - 1 - n if reverse else n) * c, c)
            sl = pl.ds(base, c)
            gc = gate_ref[sl, :]
            gr = gate_t_ref[:, sl]
            h, cm, nv, m_new = _mlstm_chunk(
                reverse, q2_ref[sl, :], k2_ref[sl, :], v_ref[sl, :],
                gc[:, 2 * d:2 * d + 1], gc[:, 2 * d + 1:2 * d + 2],
                gr[2 * d:2 * d + 1, :], gr[2 * d + 1:2 * d + 2, :],
                c_ref[d], n_ref[d], m_ref[d])
            c_ref[d] = cm
            n_ref[d] = nv
            m_ref[d] = m_new
            (hb_ref if reverse else hf_ref)[sl, :] = h
        return carry

    lax.fori_loop(0, n_chunks, body, 0)
    h = hf_ref[...] + hb_ref[...]
    h = h * lax.rsqrt(jnp.mean(h * h, axis=-1, keepdims=True) + RMS_EPS)
    o_ref[...] = h * _sigmoid(og_ref[...])
    c_out_ref[...] = c_ref[...]
    n_out_ref[...] = n_ref[...]
    m_out_ref[...] = m_ref[...]


def _mlstm(p, gate_h, gate_t_h, conv_w, c0, n0, m0, n_seq, seq_len, tok_offset):
    hd = HEAD_DIM
    row0 = tok_offset // seq_len

    def col(part):
        return pl.BlockSpec((seq_len, hd), lambda b, h: (row0 + b, part * HEADS + h))

    c_spec = pl.BlockSpec((None, 2, None, hd, hd), lambda b, h: (b, 0, h, 0, 0))
    n_spec = pl.BlockSpec((None, 2, None, 1, hd), lambda b, h: (b, 0, h, 0, 0))
    m_spec = pl.BlockSpec((None, 2, None, 1, 1), lambda b, h: (b, 0, h, 0, 0))
    return pl.pallas_call(
        functools.partial(_mlstm_kernel, seq_len),
        grid=(n_seq, HEADS),
        in_specs=[col(5), col(6), col(7), col(8),
                  pl.BlockSpec((None, seq_len, 4), lambda b, h: (h, row0 + b, 0)),
                  pl.BlockSpec((None, 4, seq_len), lambda b, h: (h, 0, row0 + b)),
                  pl.BlockSpec((3, hd), lambda b, h: (0, h)),
                  pl.BlockSpec((3, hd), lambda b, h: (0, HEADS + h)),
                  c_spec, n_spec, m_spec],
        out_specs=[pl.BlockSpec((seq_len, hd), lambda b, h: (b, h)), c_spec, n_spec, m_spec],
        out_shape=[jax.ShapeDtypeStruct((n_seq * seq_len, GROUP_W), F32),
                   jax.ShapeDtypeStruct((n_seq, 2, HEADS, hd, hd), F32),
                   jax.ShapeDtypeStruct((n_seq, 2, HEADS, 1, hd), F32),
                   jax.ShapeDtypeStruct((n_seq, 2, HEADS, 1, 1), F32)],
        scratch_shapes=[pltpu.VMEM((seq_len, hd), F32), pltpu.VMEM((seq_len, hd), F32),
                        pltpu.VMEM((seq_len, hd), F32), pltpu.VMEM((seq_len, hd), F32),
                        pltpu.VMEM((2, hd, hd), F32), pltpu.VMEM((2, 1, hd), F32),
                        pltpu.VMEM((2, 1, 1), F32)],
        compiler_params=_cparams("parallel", "parallel"),
        name=f"mlstm_l{seq_len}",
    )(p, p, p, p, gate_h, gate_t_h, conv_w, conv_w, c0, n0, m0)


def _dft_tables(seq_len):
    n = 2 * seq_len
    k = jnp.arange(seq_len, dtype=jnp.int32)[:, None]
    t = jnp.arange(seq_len, dtype=jnp.int32)[None, :]
    ang = ((k * t) % n).astype(F32) * (2.0 * np.pi / n)
    fc = jnp.cos(ang)
    fs = jnp.sin(ang)
    nyq = jnp.where(t % 2 == 0, 1.0, -1.0).astype(F32)
    fs = jnp.where(k == 0, nyq, fs)
    return jnp.concatenate([fc, fs], axis=0)


def _filter_kernel(seq_len, z_ref, w1_ref, b1_ref, w2_ref, b2_ref, w3f_ref, w3b_ref, b3f_ref, b3b_ref,
                   f0_ref, f1_ref, rf_ref, rb_ref, fhi_ref, flo_ref, kf_ref):
    hp = lax.Precision.HIGHEST
    n = 2 * seq_len
    z = z_ref[...]
    a = jnp.sin(f0_ref[...] * (jnp.dot(z, w1_ref[...], precision=hp, preferred_element_type=F32) + b1_ref[...]))
    a = jnp.sin(f1_ref[...] * (jnp.dot(a, w2_ref[...], precision=hp, preferred_element_type=F32) + b2_ref[...]))
    t_norm = z[:, 0:1]
    hf = (jnp.dot(a, w3f_ref[...], precision=hp, preferred_element_type=F32) + b3f_ref[...]) \
        * jnp.exp(-t_norm * jnp.exp(rf_ref[...]))
    hb = (jnp.dot(a, w3b_ref[...], precision=hp, preferred_element_type=F32) + b3b_ref[...]) \
        * jnp.exp(-t_norm * jnp.exp(rb_ref[...]))
    inv = lax.rsqrt(jnp.sum(hf * hf, axis=0, keepdims=True) + jnp.sum(hb * hb, axis=0, keepdims=True))
    hf = hf * inv
    r = lax.broadcasted_iota(jnp.int32, (seq_len, 1), 0)
    hb = jnp.where(r == 0, 0.0, hb * inv)
    sh, sm, sl = _split3(hf + hb)
    dh, dm, dl = _split3(hf - hb)
    fhi = fhi_ref[...]
    flo = flo_ref[...]
    kc = _dot(fhi[:seq_len], sh) + _dot(fhi[:seq_len], sm) + _dot(fhi[:seq_len], sl) \
        + _dot(flo[:seq_len], sh) + _dot(flo[:seq_len], sm)
    ks = _dot(fhi[seq_len:], dh) + _dot(fhi[seq_len:], dm) + _dot(fhi[seq_len:], dl) \
        + _dot(flo[seq_len:], dh) + _dot(flo[seq_len:], dm)
    sign = jnp.where(r % 2 == 0, 1.0, -1.0)
    k_nyq = jnp.sum(sign * (hf + hb), axis=0, keepdims=True)
    ks = jnp.where(r == 0, k_nyq, ks)
    scale = jnp.where(r == 0, 1.0 / n, 2.0 / n)
    kf_ref[0:seq_len, :] = kc * scale
    kf_ref[seq_len:n, :] = ks * scale


def _hyena_filters(seq_len, w1, b1, w2, b2, w3, b3, freq, log_rate, f_tab):
    d = D_MODEL
    cb = 256
    t = jnp.arange(seq_len, dtype=F32)
    t_norm = t / (seq_len - 1)
    bands = jnp.linspace(1e-4, HY_BANDS - 1, HY_BANDS, dtype=F32)
    ang = (2.0 * np.pi / seq_len) * t[:, None] * bands[None, :]
    z = jnp.concatenate([t_norm[:, None], jnp.cos(ang), jnp.sin(ang)], axis=-1)
    kpad = 128 - HY_EMB
    z = jnp.pad(z, ((0, 0), (0, kpad)))
    w1p = jnp.pad(w1, ((0, kpad), (0, 0)))
    f_hi = f_tab.astype(BF16)
    f_lo = (f_tab - f_hi.astype(F32)).astype(BF16)
    n_cb = d // cb
    hh = HY_HIDDEN
    row = lambda a: a.reshape(1, -1)
    const = lambda shape: pl.BlockSpec(shape, lambda o, j: (0,) * len(shape))
    fwd = lambda rows: pl.BlockSpec((rows, cb), lambda o, j: (0, o * n_cb + j))
    bwd = lambda rows: pl.BlockSpec((rows, cb), lambda o, j: (0, (HY_ORDER + o) * n_cb + j))
    return pl.pallas_call(
        functools.partial(_filter_kernel, seq_len),
        grid=(HY_ORDER, n_cb),
        in_specs=[const((seq_len, 128)), const((128, hh)), const((1, hh)), const((hh, hh)), const((1, hh)),
                  fwd(hh), bwd(hh), fwd(1), bwd(1),
                  const((1, hh)), const((1, hh)), fwd(1), bwd(1),
                  const((2 * seq_len, seq_len)), const((2 * seq_len, seq_len))],
        out_specs=pl.BlockSpec((None, 2 * seq_len, cb), lambda o, j: (o, 0, j)),
        out_shape=jax.ShapeDtypeStruct((HY_ORDER, 2 * seq_len, d), F32),
        compiler_params=_cparams("parallel", "parallel"),
        name=f"hyena_filter_l{seq_len}",
    )(z, w1p, row(b1), w2, row(b2), w3, w3, row(b3), row(b3),
      row(freq[0]), row(freq[1]), row(log_rate), row(log_rate), f_hi, f_lo)


def _hyena_kernel(seq_len, v_ref, x1_ref, x2_ref, cwv_ref, cw1_ref, cw2_ref, bias_ref, kf_ref,
                  f_ref, ft_ref, o_ref):
    r = lax.broadcasted_iota(jnp.int32, (seq_len, 1), 0)
    z = _short_conv3(v_ref[...], cwv_ref[...])
    gates = (_short_conv3(x1_ref[...], cw1_ref[...]), _short_conv3(x2_ref[...], cw2_ref[...]))
    for o in range(HY_ORDER):
        zf = _dot(f_ref[...], z.astype(BF16))
        a, bm = zf[:seq_len], zf[seq_len:]
        kc, ks = kf_ref[o, 0:seq_len, :], kf_ref[o, seq_len:2 * seq_len, :]
        yc = a * kc - jnp.where(r == 0, 0.0, bm * ks)
        ys = jnp.where(r == 0, bm * ks, a * ks + bm * kc)
        y = _dot(ft_ref[:, 0:seq_len], yc.astype(BF16)) + _dot(ft_ref[:, seq_len:2 * seq_len], ys.astype(BF16))
        z = gates[o] * (y + z * bias_ref[o:o + 1, :])
    o_ref[...] = z


def _hyena(u, conv_w, bias, kf, f_tab, n_seq, seq_len, tok_offset):
    d = D_MODEL
    cb = 256
    n_cb = d // cb
    row0 = tok_offset // seq_len
    f_bf = f_tab.astype(BF16)

    def part(k):
        return pl.BlockSpec((seq_len, cb), lambda j, b: (row0 + b, k * n_cb + j))

    def cw(k):
        return pl.BlockSpec((3, cb), lambda j, b: (0, k * n_cb + j))

    return pl.pallas_call(
        functools.partial(_hyena_kernel, seq_len),
        grid=(n_cb, n_seq),
        in_specs=[part(0), part(1), part(2), cw(0), cw(1), cw(2),
                  pl.BlockSpec((HY_ORDER, cb), lambda j, b: (0, j)),
                  pl.BlockSpec((HY_ORDER, 2 * seq_len, cb), lambda j, b: (0, 0, j)),
                  pl.BlockSpec((2 * seq_len, seq_len), lambda j, b: (0, 0)),
                  pl.BlockSpec((seq_len, 2 * seq_len), lambda j, b: (0, 0))],
        out_specs=pl.BlockSpec((seq_len, cb), lambda j, b: (b, j)),
        out_shape=jax.ShapeDtypeStruct((n_seq * seq_len, d), F32),
        compiler_params=_cparams("parallel", "parallel"),
        name=f"hyena_l{seq_len}",
    )(u, u, u, conv_w, conv_w, conv_w, bias, kf, f_bf, f_bf.T)


def _mix_out_kernel(n_parts, *refs):
    o_refs = refs[:n_parts]
    w_refs = refs[n_parts:2 * n_parts]
    x_ref, mod_ref, g_ref, wr_hi_ref, wr_lo_ref, br_ref = refs[2 * n_parts:2 * n_parts + 6]
    x1_ref, h_ref, eid_ref, wts_ref, rank_ref, cnt_ref, run_ref = refs[2 * n_parts + 6:]
    i = pl.program_id(0)
    t = TOK_TILE
    ne = N_EXPERTS

    y = _dot(o_refs[0][...].astype(BF16), w_refs[0][...])
    for j in range(1, n_parts):
        y = y + _dot(o_refs[j][...].astype(BF16), w_refs[j][...])
    x1 = x_ref[...] + mod_ref[2:3, :] * y
    x1_ref[...] = x1
    h = _norm_mod(x1, g_ref[...], mod_ref[4:5, :], mod_ref[3:4, :])
    h_ref[...] = h
    h_hi = h.astype(BF16)
    h_lo = (h - h_hi.astype(F32)).astype(BF16)
    logits = _dot_nt(wr_hi_ref[...], h_hi) + _dot_nt(wr_lo_ref[...], h_hi) + _dot_nt(wr_hi_ref[...], h_lo) \
        + br_ref[...]

    @pl.when(i == 0)
    def _():
        run_ref[...] = jnp.zeros_like(run_ref)

    e_iota = lax.broadcasted_iota(jnp.int32, (ne, t), 0)
    r2 = lax.broadcasted_iota(jnp.int32, (t, t), 0)
    c2 = lax.broadcasted_iota(jnp.int32, (t, t), 1)
    before = jnp.where(r2 < c2, 1.0, 0.0).astype(BF16)
    running = run_ref[...]
    vals, eids, ranks = [], [], []
    for _k in range(TOP_K):
        m = jnp.max(logits, axis=0, keepdims=True)
        eid = jnp.min(jnp.where(logits == m, e_iota, ne), axis=0, keepdims=True)
        sel = e_iota == eid
        logits = jnp.where(sel, -jnp.inf, logits)
        onehot = jnp.where(sel, 1.0, 0.0)
        earlier = _dot(onehot.astype(BF16), before)
        ranks.append(jnp.sum(onehot * (running + earlier), axis=0, keepdims=True))
        running = running + jnp.sum(onehot, axis=1, keepdims=True)
        vals.append(m)
        eids.append(eid)
    run_ref[...] = running
    cnt_ref[...] = running
    v = jnp.concatenate(vals, axis=0)
    ex = jnp.exp(v - v[0:1])
    wts_ref[...] = ex / jnp.sum(ex, axis=0, keepdims=True)
    eid_ref[...] = jnp.concatenate(eids, axis=0)
    rank_ref[...] = jnp.concatenate(ranks, axis=0).astype(jnp.int32)


def _mix_out(parts, x, mod_l, norm_g, w_router, b_router):
    d = D_MODEL
    t = TOK_TILE
    ne = N_EXPERTS
    n_parts = len(parts)
    wr_t = w_router.T
    wr_hi = wr_t.astype(BF16)
    wr_lo = (wr_t - wr_hi.astype(F32)).astype(BF16)
    in_specs = [pl.BlockSpec((t, o.shape[1]), lambda i: (i, 0)) for o, _ in parts]
    in_specs += [pl.BlockSpec(w.shape, lambda i: (0, 0)) for _, w in parts]
    in_specs += [
        pl.BlockSpec((t, d), lambda i: (i, 0)),
        pl.BlockSpec((None, N_MOD, d), lambda i: (_tile_cond_row(i), 0, 0)),
        pl.BlockSpec((1, d), lambda i: (0, 0)),
        pl.BlockSpec((ne, d), lambda i: (0, 0)),
        pl.BlockSpec((ne, d), lambda i: (0, 0)),
        pl.BlockSpec((ne, 1), lambda i: (0, 0)),
    ]
    tok_major = pl.BlockSpec((TOP_K, t), lambda i: (0, i))
    return pl.pallas_call(
        functools.partial(_mix_out_kernel, n_parts),
        grid=(N_TOK // t,),
        in_specs=in_specs,
        out_specs=[pl.BlockSpec((t, d), lambda i: (i, 0)), pl.BlockSpec((t, d), lambda i: (i, 0)),
                   tok_major, tok_major, tok_major, pl.BlockSpec((ne, 1), lambda i: (0, 0))],
        out_shape=[jax.ShapeDtypeStruct((N_TOK, d), F32), jax.ShapeDtypeStruct((N_TOK, d), F32),
                   jax.ShapeDtypeStruct((TOP_K, N_TOK), jnp.int32), jax.ShapeDtypeStruct((TOP_K, N_TOK), F32),
                   jax.ShapeDtypeStruct((TOP_K, N_TOK), jnp.int32), jax.ShapeDtypeStruct((ne, 1), F32)],
        scratch_shapes=[pltpu.VMEM((ne, 1), F32)],
        compiler_params=_cparams("arbitrary"),
        name="mix_out_router",
    )(*[o for o, _ in parts], *[w.astype(BF16) for _, w in parts], x, mod_l, norm_g.reshape(1, d),
      wr_hi, wr_lo, b_router.reshape(ne, 1))


def _row_copy_kernel(gather, n_src, idx_ref, src_ref, *rest):
    dst_ref, sem = rest[-2], rest[-1]
    base = pl.program_id(0) * COPY_ROWS
    src_base = 0 if (gather or n_src == 1) else lax.rem(base, n_src)

    def copy(j):
        moved = idx_ref[0, 0, j]
        if gather:
            s, d = moved, base + j
        elif n_src == 1:
            s, d = 0, moved
        else:
            s, d = src_base + j, moved
        return pltpu.make_async_copy(src_ref.at[pl.ds(s, 1)], dst_ref.at[pl.ds(d, 1)], sem)

    def issue(j, carry):
        for u in range(COPY_UNROLL):
            copy(j * COPY_UNROLL + u).start()
        return carry

    def drain(j, carry):
        for u in range(COPY_UNROLL):
            copy(j * COPY_UNROLL + u).wait()
        return carry

    lax.fori_loop(0, COPY_ROWS // COPY_UNROLL, issue, 0)
    lax.fori_loop(0, COPY_ROWS // COPY_UNROLL, drain, 0)


def _row_scatter(src, idx, dst=None, dst_rows=None):
    n = idx.shape[0]
    n_src = src.shape[0]
    if not (n_src == 1 or n_src % COPY_ROWS == 0) or n % COPY_ROWS:
        raise ValueError("row counts must be whole copy steps")
    steps = n // COPY_ROWS
    in_specs = [pl.BlockSpec((1, 1, COPY_ROWS), lambda i: (i, 0, 0), memory_space=pltpu.SMEM),
                pl.BlockSpec(memory_space=pl.ANY)]
    args = [idx.reshape(steps, 1, COPY_ROWS), src]
    aliases = {}
    if dst is not None:
        in_specs.append(pl.BlockSpec(memory_space=pl.ANY))
        args.append(dst)
        aliases = {2: 0}
        dst_rows = dst.shape[0]
    return pl.pallas_call(
        functools.partial(_row_copy_kernel, False, n_src),
        grid=(steps,),
        in_specs=in_specs,
        out_specs=pl.BlockSpec(memory_space=pl.ANY),
        out_shape=jax.ShapeDtypeStruct((dst_rows, src.shape[1]), src.dtype),
        scratch_shapes=[pltpu.SemaphoreType.DMA(())],
        input_output_aliases=aliases,
        compiler_params=pltpu.CompilerParams(dimension_semantics=("arbitrary",)),
        name="row_scatter" if n_src > 1 else "row_fill",
    )(*args)


def _row_gather(src, idx):
    n = idx.shape[0]
    if n % COPY_ROWS:
        raise ValueError("row count must be whole copy steps")
    steps = n // COPY_ROWS
    return pl.pallas_call(
        functools.partial(_row_copy_kernel, True, src.shape[0]),
        grid=(steps,),
        in_specs=[pl.BlockSpec((1, 1, COPY_ROWS), lambda i: (i, 0, 0), memory_space=pltpu.SMEM),
                  pl.BlockSpec(memory_space=pl.ANY)],
        out_specs=pl.BlockSpec(memory_space=pl.ANY),
        out_shape=jax.ShapeDtypeStruct((n, src.shape[1]), src.dtype),
        scratch_shapes=[pltpu.SemaphoreType.DMA(())],
        compiler_params=pltpu.CompilerParams(dimension_semantics=("arbitrary",)),
        name="row_gather",
    )(idx.reshape(steps, 1, COPY_ROWS), src)


def _experts_kernel(te_ref, first_ref, nv_ref, x_ref, wgu_ref, bg_ref, bu_ref, wd_ref, bd_ref,
                    sel_even_ref, sel_odd_ref, y_ref, wg_ref, wu_ref, wdb_ref):
    del te_ref
    i = pl.program_id(0)
    valid = i < nv_ref[0]
    half = DEINT_COLS // 2

    @pl.when(jnp.logical_and(valid, first_ref[i] == 1))
    def _():
        for c in range(wgu_ref.shape[1] // DEINT_COLS):
            w = wgu_ref[:, c * DEINT_COLS:(c + 1) * DEINT_COLS].astype(BF16)
            wg_ref[:, c * half:(c + 1) * half] = _dot(w, sel_even_ref[...]).astype(BF16)
            wu_ref[:, c * half:(c + 1) * half] = _dot(w, sel_odd_ref[...]).astype(BF16)
        wdb_ref[...] = wd_ref[...].astype(BF16)

    @pl.when(valid)
    def _():
        x = x_ref[...].astype(BF16)
        gl = _dot(x, wg_ref[...]) + bg_ref[...]
        up = _dot(x, wu_ref[...]) + bu_ref[...]
        gl = jnp.minimum(gl, SWIGLU_LIMIT)
        up = jnp.clip(up, -SWIGLU_LIMIT, SWIGLU_LIMIT)
        act = (up + 1.0) * gl * _sigmoid(SWIGLU_ALPHA * gl)
        y_ref[...] = _dot(act.astype(BF16), wdb_ref[...]) + bd_ref[...]

    @pl.when(jnp.logical_not(valid))
    def _():
        y_ref[...] = jnp.zeros_like(y_ref)


def _experts(xs, tile_expert, tile_first, n_valid, w_gu, b_gate, b_up, w_down, b_down):
    d = D_MODEL
    tm = MOE_TILE
    n_tiles = MOE_ROWS // tm
    ff = w_down.shape[1]
    half = DEINT_COLS // 2
    r = jnp.arange(DEINT_COLS)[:, None]
    c = jnp.arange(half)[None, :]
    sel_even = (r == 2 * c).astype(BF16)
    sel_odd = (r == 2 * c + 1).astype(BF16)
    wspec = lambda k, n: pl.BlockSpec((None, k, n), lambda i, te, fi, nv: (te[i], 0, 0))
    const = lambda shape: pl.BlockSpec(shape, lambda i, te, fi, nv: (0, 0))
    grid_spec = pltpu.PrefetchScalarGridSpec(
        num_scalar_prefetch=3,
        grid=(n_tiles,),
        in_specs=[pl.BlockSpec((tm, d), lambda i, te, fi, nv: (i, 0)),
                  wspec(d, 2 * ff), wspec(1, ff), wspec(1, ff), wspec(ff, d), wspec(1, d),
                  const((DEINT_COLS, half)), const((DEINT_COLS, half))],
        out_specs=pl.BlockSpec((tm, d), lambda i, te, fi, nv: (i, 0)),
        scratch_shapes=[pltpu.VMEM((d, ff), BF16), pltpu.VMEM((d, ff), BF16), pltpu.VMEM((ff, d), BF16)],
    )
    return pl.pallas_call(
        _experts_kernel,
        grid_spec=grid_spec,
        out_shape=jax.ShapeDtypeStruct((MOE_ROWS, d), F32),
        compiler_params=_cparams("arbitrary"),
        name="experts",
    )(tile_expert, tile_first, n_valid, xs, w_gu, b_gate, b_up, w_down, b_down, sel_even, sel_odd)


def _combine_kernel(final, x_ref, g_ref, w_ref, mod_ref, fg_ref, o_ref):
    y = w_ref[:, 0:1] * g_ref[0]
    for k in range(1, TOP_K):
        y = y + w_ref[:, k:k + 1] * g_ref[k]
    x = x_ref[...] + mod_ref[5:6, :] * y
    if final:
        x = x * lax.rsqrt(jnp.mean(x * x, axis=-1, keepdims=True) + RMS_EPS) * fg_ref[...]
    o_ref[...] = x


def _combine(x1, gathered, wts, mod_l, final_g, final):
    d = D_MODEL
    t = TOK_TILE
    return pl.pallas_call(
        functools.partial(_combine_kernel, final),
        grid=(N_TOK // t,),
        in_specs=[pl.BlockSpec((t, d), lambda i: (i, 0)),
                  pl.BlockSpec((TOP_K, t, d), lambda i: (0, i, 0)),
                  pl.BlockSpec((t, TOP_K), lambda i: (i, 0)),
                  pl.BlockSpec((None, N_MOD, d), lambda i: (_tile_cond_row(i), 0, 0)),
                  pl.BlockSpec((1, d), lambda i: (0, 0))],
        out_specs=pl.BlockSpec((t, d), lambda i: (i, 0)),
        out_shape=jax.ShapeDtypeStruct((N_TOK, d), F32),
        compiler_params=_cparams("parallel"),
        name="moe_combine",
    )(x1, gathered, wts, mod_l, final_g.reshape(1, d))


def _moe(x1, h, eid, wts, rank, counts, mod_l, w_gu, b_gu, w_down, b_down, final_g, final):
    d = D_MODEL
    tm = MOE_TILE
    n_tiles = MOE_ROWS // tm
    cnt = counts.reshape(N_EXPERTS).astype(jnp.int32)
    gsz = ((cnt + tm - 1) // tm) * tm
    ends = jnp.cumsum(gsz)
    offs = ends - gsz
    e_ids = jnp.arange(N_EXPERTS, dtype=jnp.int32)
    pos = jnp.sum(jnp.where(eid[..., None] == e_ids, offs, 0), axis=-1) + rank
    tile_start = jnp.arange(n_tiles, dtype=jnp.int32) * tm
    tile_expert = jnp.minimum(jnp.sum((ends[None, :] <= tile_start[:, None]).astype(jnp.int32), axis=1),
                              N_EXPERTS - 1)
    n_valid = (ends[-1:] // tm).astype(jnp.int32)
    last_valid = jnp.maximum(n_valid[0] - 1, 0)
    tile_expert = jnp.where(jnp.arange(n_tiles) < n_valid[0], tile_expert, tile_expert[last_valid])
    tile_first = jnp.concatenate([jnp.ones((1,), jnp.int32),
                                  (tile_expert[1:] != tile_expert[:-1]).astype(jnp.int32)])
    j = jnp.arange(tm, dtype=jnp.int32)[None, :]
    pad_used = j < (gsz - cnt)[:, None]
    n_unused_before = jnp.cumsum((~pad_used).reshape(-1).astype(jnp.int32)) - 1
    pad_pos = jnp.where(pad_used, (offs + cnt)[:, None] + j,
                        ends[-1] + n_unused_before.reshape(N_EXPERTS, tm))

    xs = _row_scatter(h, pos.reshape(-1), dst_rows=MOE_ROWS)
    xs = _row_scatter(jnp.zeros((1, d), h.dtype), pad_pos.reshape(-1), dst=xs)
    ys = _experts(xs, tile_expert, tile_first, n_valid, w_gu,
                  b_gu[:, None, 0::2], b_gu[:, None, 1::2], w_down, b_down[:, None, :])
    gathered = _row_gather(ys, pos.reshape(-1))
    return _combine(x1, gathered.reshape(TOP_K, N_TOK, d), wts.T, mod_l, final_g, final)


def _grid_positions(n_tok, d):
    rows = n_tok // GRID_W
    r, col = jnp.meshgrid(jnp.arange(rows, dtype=F32), jnp.arange(GRID_W, dtype=F32), indexing='ij')
    r = r.reshape(-1)
    col = col.reshape(-1)
    quarter = d // 4
    inv = 1.0 / (10000.0 ** (jnp.arange(quarter, dtype=F32) / quarter))
    ar = r[:, None] * inv[None]
    ac = col[:, None] * inv[None]
    return jnp.concatenate([jnp.sin(ar), jnp.cos(ar), jnp.sin(ac), jnp.cos(ac)], axis=-1)


def kernel(x_prompt, x_sample, state_hgrn, state_mlstm_c, state_mlstm_n, state_mlstm_m, c, c_ctx,
           norm_g, final_g, w_mod, b_mod, ev_w_in, ev_gate_b, ev_conv, hg_lb, ev_w_out,
           hy_w_in, hy_conv, hy_w1, hy_b1, hy_w2, hy_b2, hy_w3, hy_b3, hy_freq, hy_log_rate, hy_bias, hy_w_out,
           w_router, b_router, w_gu, b_gu, w_down, b_down):
    d = D_MODEL
    hd = HEAD_DIM
    cond = jnp.concatenate([c_ctx[None], c, jnp.zeros((N_COND - 1 - DEC_BATCH, d), F32)], axis=0)
    mod = _modulation(cond, w_mod, b_mod)
    x = jnp.concatenate([x_prompt.reshape(N_PROMPT, d), x_sample.reshape(N_SAMPLE, d)], axis=0)
    pos_tab = jnp.concatenate([jnp.zeros((TOK_TILE, d), F32), _grid_positions(DEC_SEQ, d)], axis=0)

    groups = ((BATCH, SEQ, 0), (DEC_BATCH, DEC_SEQ, N_PROMPT))
    new_states = None
    for l in range(DEPTH):
        if l % 2 == 0:
            e = l // 2
            if l == 0:
                x, p, gate, gate_t = _proj_even(x, pos_tab, mod[l], norm_g[l, 0], ev_w_in[e], ev_gate_b[e])
            else:
                raise NotImplementedError("only the first layer adds grid positions")
            gate_h = gate.reshape(N_TOK, 4, HEADS).transpose(2, 0, 1)
            gate_t_h = gate_t.reshape(4, HEADS, N_TOK).transpose(1, 0, 2)
            o_hg, o_ml = [], []
            for gi, (n_seq, seq_len, off) in enumerate(groups):
                if gi == 0:
                    s0 = jnp.zeros((n_seq, 2, HEADS, hd, hd), F32)
                    c0 = jnp.zeros((n_seq, 2, HEADS, hd, hd), F32)
                    n0 = jnp.zeros((n_seq, 2, HEADS, 1, hd), F32)
                    m0 = jnp.zeros((n_seq, 2, HEADS, 1, 1), F32)
                else:
                    s0 = state_hgrn[:, e]
                    c0 = state_mlstm_c[:, e]
                    n0 = state_mlstm_n[:, e].reshape(n_seq, 2, HEADS, 1, hd)
                    m0 = state_mlstm_m[:, e].reshape(n_seq, 2, HEADS, 1, 1)
                og, s_fin = _hgrn(p, hg_lb, l, s0, n_seq, seq_len, off)
                om, c_fin, n_fin, m_fin = _mlstm(p, gate_h, gate_t_h, ev_conv[e], c0, n0, m0, n_seq, seq_len, off)
                o_hg.append(og)
                o_ml.append(om)
                if gi == 0:
                    new_states = (s_fin[:, None], c_fin[:, None],
                                  n_fin.reshape(n_seq, 1, 2, HEADS, hd), m_fin.reshape(n_seq, 1, 2, HEADS))
            parts = [(jnp.concatenate(o_hg, axis=0), ev_w_out[e][:GROUP_W]),
                     (jnp.concatenate(o_ml, axis=0), ev_w_out[e][GROUP_W:])]
        else:
            o = l // 2
            u = _proj_odd(x, mod[l], norm_g[l, 0], hy_w_in[o])
            zs = []
            for n_seq, seq_len, off in groups:
                f_tab = _dft_tables(seq_len)
                kf = _hyena_filters(seq_len, hy_w1[o], hy_b1[o], hy_w2[o], hy_b2[o], hy_w3[o], hy_b3[o],
                                    hy_freq[o], hy_log_rate[o], f_tab)
                zs.append(_hyena(u, hy_conv[o], hy_bias[o], kf, f_tab, n_seq, seq_len, off))
            parts = [(jnp.concatenate(zs, axis=0), hy_w_out[o])]
        x1, h, eid, wts, rank, counts = _mix_out(parts, x, mod[l], norm_g[l, 1], w_router[l], b_router[l])
        x = _moe(x1, h, eid, wts, rank, counts, mod[l], w_gu[l], b_gu[l], w_down[l], b_down[l],
                 final_g, final=(l == DEPTH - 1))

    y_prompt = x[:N_PROMPT].reshape(BATCH, SEQ, d)
    y_sample = x[N_PROMPT:].reshape(DEC_BATCH, DEC_SEQ, d)
    return (y_prompt, y_sample) + new_states
```

```python
import functools

import numpy as np
import jax
import jax.numpy as jnp
from jax import lax
from jax.experimental import pallas as pl
from jax.experimental.pallas import tpu as pltpu
from jax.experimental.pallas import tpu_sc as plsc

F32 = jnp.float32
BF16 = jnp.bfloat16

D_MODEL = 1024
BATCH = 32
SEQ = 256
DEPTH = 2
DEC_BATCH = 8
DEC_SEQ = 1024
GRID_W = 64
RMS_EPS = 1e-6
N_MOD = 6

HEADS = 4
HEAD_DIM = 128
GROUP_W = HEADS * HEAD_DIM
N_GATES = 4 * HEADS
EVEN_MAIN = 9 * GROUP_W

HY_ORDER = 2
HY_BANDS = 16
HY_EMB = 1 + 2 * HY_BANDS
HY_HIDDEN = 64

N_EXPERTS = 32
TOP_K = 4
SWIGLU_LIMIT = 7.0
SWIGLU_ALPHA = 1.702

N_PROMPT = BATCH * SEQ
N_SAMPLE = DEC_BATCH * DEC_SEQ
N_TOK = N_PROMPT + N_SAMPLE
N_COND = 16

TOK_TILE = 256
SCAN_CHUNK = 128
SUB = 16
MOE_TILE = 256
MOE_ROWS = N_TOK * TOP_K + N_EXPERTS * MOE_TILE
DEINT_COLS = 512
COPY_ROWS = 2048
COPY_UNROLL = 8
SC_ROWS = 64

VMEM_LIMIT = 56 * 1024 * 1024


def _cparams(*sem):
    return pltpu.CompilerParams(dimension_semantics=sem, vmem_limit_bytes=VMEM_LIMIT)


def _split3(x):
    hi = x.astype(BF16)
    r = x - hi.astype(F32)
    mid = r.astype(BF16)
    lo = (r - mid.astype(F32)).astype(BF16)
    return hi, mid, lo


def _dot(a, b):
    return jnp.dot(a, b, preferred_element_type=F32)


def _dot_nt(a, b):
    return lax.dot_general(a, b, (((1,), (1,)), ((), ())), preferred_element_type=F32)


def _dot_tn(a, b):
    return lax.dot_general(a, b, (((0,), (0,)), ((), ())), preferred_element_type=F32)


def _dot_w3(a_exact_bf16, x):
    hi, mid, lo = _split3(x)
    return _dot(a_exact_bf16, hi) + _dot(a_exact_bf16, mid) + _dot(a_exact_bf16, lo)


def _sigmoid(x):
    return 1.0 / (1.0 + jnp.exp(-x))


def _silu(x):
    return x * _sigmoid(x)


def _log_sigmoid(x):
    return jnp.minimum(x, 0.0) - jnp.log(1.0 + jnp.exp(-jnp.abs(x)))


def _tile_cond_row(i):
    n_prompt_tiles = N_PROMPT // TOK_TILE
    tiles_per_seq = DEC_SEQ // TOK_TILE
    return jnp.where(i < n_prompt_tiles, 0, 1 + (i - n_prompt_tiles) // tiles_per_seq)


def _mod_kernel(cond_ref, w_ref, b_ref, o_ref):
    a = _silu(cond_ref[...]).astype(BF16)
    o_ref[...] = _dot(a, w_ref[...].astype(BF16)) + b_ref[...]


def _modulation(cond, w_mod, b_mod):
    d = D_MODEL
    out = pl.pallas_call(
        _mod_kernel,
        grid=(DEPTH, N_MOD),
        in_specs=[
            pl.BlockSpec((N_COND, d), lambda l, j: (0, 0)),
            pl.BlockSpec((None, d, d), lambda l, j: (l, 0, j)),
            pl.BlockSpec((None, 1, d), lambda l, j: (l, 0, j)),
        ],
        out_specs=pl.BlockSpec((None, None, N_COND, d), lambda l, j: (l, j, 0, 0)),
        out_shape=jax.ShapeDtypeStruct((DEPTH, N_MOD, N_COND, d), F32),
        compiler_params=_cparams("parallel", "parallel"),
        name="modulation",
    )(cond, w_mod, b_mod.reshape(DEPTH, 1, N_MOD * d))
    return out.transpose(0, 2, 1, 3)


def _norm_mod(x, g_row, scale_row, shift_row):
    ms = jnp.mean(x * x, axis=-1, keepdims=True)
    y = x * lax.rsqrt(ms + RMS_EPS) * g_row
    return y * (1.0 + scale_row) + shift_row


def _proj_even_kernel(x_ref, pos_ref, mod_ref, g_ref, w_ref, wg_ref, wgt_ref, gb_ref, gbt_ref,
                      xres_ref, p_ref, gate_ref, gate_t_ref):
    x = x_ref[...] + pos_ref[...]
    xres_ref[...] = x
    h = _norm_mod(x, g_ref[...], mod_ref[1:2, :], mod_ref[0:1, :]).astype(BF16)
    p_ref[...] = _dot(h, w_ref[...])
    gate_ref[...] = _dot(h, wg_ref[...]) + gb_ref[...]
    gate_t_ref[...] = _dot_nt(wgt_ref[...], h) + gbt_ref[...]


def _proj_even(x, pos_tab, mod_l, norm_g, w_in, gate_b):
    d = D_MODEL
    n_tiles = N_TOK // TOK_TILE
    n_prompt_tiles = N_PROMPT // TOK_TILE
    tiles_per_seq = DEC_SEQ // TOK_TILE
    w_main = w_in[:, :EVEN_MAIN].astype(BF16)
    w_gate = w_in[:, EVEN_MAIN:].astype(BF16)

    def pos_map(i):
        return (jnp.where(i < n_prompt_tiles, 0, 1 + (i - n_prompt_tiles) % tiles_per_seq), 0)

    return pl.pallas_call(
        _proj_even_kernel,
        grid=(n_tiles,),
        in_specs=[
            pl.BlockSpec((TOK_TILE, d), lambda i: (i, 0)),
            pl.BlockSpec((TOK_TILE, d), pos_map),
            pl.BlockSpec((None, N_MOD, d), lambda i: (_tile_cond_row(i), 0, 0)),
            pl.BlockSpec((1, d), lambda i: (0, 0)),
            pl.BlockSpec((d, EVEN_MAIN), lambda i: (0, 0)),
            pl.BlockSpec((d, N_GATES), lambda i: (0, 0)),
            pl.BlockSpec((N_GATES, d), lambda i: (0, 0)),
            pl.BlockSpec((1, N_GATES), lambda i: (0, 0)),
            pl.BlockSpec((N_GATES, 1), lambda i: (0, 0)),
        ],
        out_specs=[
            pl.BlockSpec((TOK_TILE, d), lambda i: (i, 0)),
            pl.BlockSpec((TOK_TILE, EVEN_MAIN), lambda i: (i, 0)),
            pl.BlockSpec((TOK_TILE, N_GATES), lambda i: (i, 0)),
            pl.BlockSpec((N_GATES, TOK_TILE), lambda i: (0, i)),
        ],
        out_shape=[
            jax.ShapeDtypeStruct((N_TOK, d), F32),
            jax.ShapeDtypeStruct((N_TOK, EVEN_MAIN), F32),
            jax.ShapeDtypeStruct((N_TOK, N_GATES), F32),
            jax.ShapeDtypeStruct((N_GATES, N_TOK), F32),
        ],
        compiler_params=_cparams("parallel"),
        name="proj_even",
    )(x, pos_tab, mod_l, norm_g.reshape(1, d), w_main, w_gate, w_gate.T,
      gate_b.reshape(1, N_GATES), gate_b.reshape(N_GATES, 1))


def _proj_odd_kernel(x_ref, mod_ref, g_ref, w_ref, p_ref):
    h = _norm_mod(x_ref[...], g_ref[...], mod_ref[1:2, :], mod_ref[0:1, :]).astype(BF16)
    p_ref[...] = _dot(h, w_ref[...])


def _proj_odd(x, mod_l, norm_g, w_in):
    d = D_MODEL
    width = w_in.shape[1]
    return pl.pallas_call(
        _proj_odd_kernel,
        grid=(N_TOK // TOK_TILE,),
        in_specs=[
            pl.BlockSpec((TOK_TILE, d), lambda i: (i, 0)),
            pl.BlockSpec((None, N_MOD, d), lambda i: (_tile_cond_row(i), 0, 0)),
            pl.BlockSpec((1, d), lambda i: (0, 0)),
            pl.BlockSpec((d, width), lambda i: (0, 0)),
        ],
        out_specs=pl.BlockSpec((TOK_TILE, width), lambda i: (i, 0)),
        out_shape=jax.ShapeDtypeStruct((N_TOK, width), F32),
        compiler_params=_cparams("parallel"),
        name="proj_odd",
    )(x, mod_l, norm_g.reshape(1, d), w_in.astype(BF16))


def _hgrn_chunk(reverse, q, k, v, lf, st):
    c = SCAN_CHUNK
    row = lax.broadcasted_iota(jnp.int32, (c, c), 0)
    col = lax.broadcasted_iota(jnp.int32, (c, c), 1)
    tri = (col >= row) if reverse else (col <= row)
    b = _dot_w3(jnp.where(tri, 1.0, 0.0).astype(BF16), lf)

    lane = lax.broadcasted_iota(jnp.int32, (SUB, c), 1)
    sub_row = lax.broadcasted_iota(jnp.int32, (SUB, 1), 0)
    chunk_row = lax.broadcasted_iota(jnp.int32, (c, 1), 0)
    rows = []
    for i in range(c // SUB):
        lo, hi = i * SUB, (i + 1) * SUB
        qi, ki, bi = q[lo:hi], k[lo:hi], b[lo:hi]
        if reverse:
            has_off, edge = hi < c, hi
            outside = chunk_row >= hi
        else:
            has_off, edge = lo > 0, lo - 1
            outside = chunk_row < lo
        if has_off:
            beta = b[edge:edge + 1]
            qs = qi * jnp.exp(bi - beta)
            ks = k * jnp.exp(jnp.where(outside, beta - b, -jnp.inf))
            a_row = _dot_nt(qs.astype(BF16), ks.astype(BF16))
        else:
            a_row = jnp.zeros((SUB, c), F32)
        for s in range(SUB):
            valid = (sub_row <= s) if reverse else (sub_row >= s)
            dlog = jnp.where(valid, bi - bi[s:s + 1], -jnp.inf)
            a_col = jnp.sum(jnp.exp(dlog) * qi * ki[s:s + 1], axis=-1, keepdims=True)
            a_row = jnp.where(lane == lo + s, a_col, a_row)
        rows.append(a_row)
    attn = jnp.concatenate(rows, axis=0)
    o = _dot(attn.astype(BF16), v.astype(BF16)) + _dot_nt((q * jnp.exp(b)).astype(BF16), st.astype(BF16))
    b_exit = b[0:1] if reverse else b[c - 1:c]
    k_out = k * jnp.exp(b_exit - b)
    st_new = jnp.exp(b_exit) * st + _dot_tn(v.astype(BF16), k_out.astype(BF16))
    return o, st_new


def _hgrn_kernel(seq_len, layer, q_ref, i_ref, g_ref, ff_ref, fb_ref, lb_ref, s0_ref,
                 o_ref, s_out_ref, of_ref, ob_ref, st_ref):
    c = SCAN_CHUNK
    n_chunks = seq_len // c
    lbp = lb_ref[...]
    e = jnp.exp(lbp - jnp.max(lbp, axis=0, keepdims=True))
    lb = jnp.sum(e[0:layer + 1], axis=0, keepdims=True) / jnp.sum(e, axis=0, keepdims=True)

    st_ref[0] = s0_ref[0].T
    st_ref[1] = s0_ref[1].T

    def body(n, carry):
        for reverse in (False, True):
            d = 1 if reverse else 0
            base = pl.multiple_of((n_chunks - 1 - n if reverse else n) * c, c)
            sl = pl.ds(base, c)
            f = lb + (1.0 - lb) * _sigmoid((fb_ref if reverse else ff_ref)[sl, :])
            o, st_new = _hgrn_chunk(reverse, q_ref[sl, :], 1.0 - f, i_ref[sl, :], jnp.log(f), st_ref[d])
            st_ref[d] = st_new
            (ob_ref if reverse else of_ref)[sl, :] = o
        return carry

    lax.fori_loop(0, n_chunks, body, 0)
    o = of_ref[...] + ob_ref[...]
    o = o * lax.rsqrt(jnp.mean(o * o, axis=-1, keepdims=True) + RMS_EPS)
    o_ref[...] = o * _silu(g_ref[...])
    s_out_ref[0] = st_ref[0].T
    s_out_ref[1] = st_ref[1].T


def _hgrn(p, hg_lb, layer, s0, n_seq, seq_len, tok_offset):
    hd = HEAD_DIM
    row0 = tok_offset // seq_len

    def col(part):
        return pl.BlockSpec((seq_len, hd), lambda b, h: (row0 + b, part * HEADS + h))

    state_spec = pl.BlockSpec((None, 2, None, hd, hd), lambda b, h: (b, 0, h, 0, 0))
    return pl.pallas_call(
        functools.partial(_hgrn_kernel, seq_len, layer),
        grid=(n_seq, HEADS),
        in_specs=[col(0), col(1), col(2), col(3), col(4),
                  pl.BlockSpec((DEPTH + 1, hd), lambda b, h: (0, h)),
                  state_spec],
        out_specs=[pl.BlockSpec((seq_len, hd), lambda b, h: (b, h)), state_spec],
        out_shape=[jax.ShapeDtypeStruct((n_seq * seq_len, GROUP_W), F32),
                   jax.ShapeDtypeStruct((n_seq, 2, HEADS, hd, hd), F32)],
        scratch_shapes=[pltpu.VMEM((seq_len, hd), F32), pltpu.VMEM((seq_len, hd), F32),
                        pltpu.VMEM((2, hd, hd), F32)],
        compiler_params=_cparams("parallel", "parallel"),
        name=f"hgrn_l{seq_len}",
    )(p, p, p, p, p, hg_lb, s0)


def _short_conv3(x, w):
    n = x.shape[0]
    r = lax.broadcasted_iota(jnp.int32, (n, 1), 0)
    prev = jnp.where(r == 0, 0.0, pltpu.roll(x, 1, 0))
    nxt = jnp.where(r == n - 1, 0.0, pltpu.roll(x, n - 1, 0))
    return prev * w[0:1] + x * w[1:2] + nxt * w[2:3]


def _mlstm_chunk(reverse, q, k, v, ig_c, fg_c, ig_r, fg_r, cm, nv, m_prev):
    c = SCAN_CHUNK
    row = lax.broadcasted_iota(jnp.int32, (c, c), 0)
    col = lax.broadcasted_iota(jnp.int32, (c, c), 1)
    tri = (col >= row) if reverse else (col <= row)
    lf_c = _log_sigmoid(fg_c)
    lf_r = _log_sigmoid(fg_r)
    b_c = jnp.sum(jnp.where(tri, lf_r, 0.0), axis=-1, keepdims=True)
    tri_t = (row >= col) if reverse else (row <= col)
    b_r = jnp.sum(jnp.where(tri_t, lf_c, 0.0), axis=0, keepdims=True)
    dmat = jnp.where(tri, b_c - b_r + ig_r, -jnp.inf)
    m_t = jnp.maximum(b_c + m_prev, jnp.max(dmat, axis=-1, keepdims=True))
    qb, kb, vb = q.astype(BF16), k.astype(BF16), v.astype(BF16)
    p = jnp.exp(dmat - m_t) * _dot_nt(qb, kb)
    inter = jnp.exp(b_c + m_prev - m_t)
    num = inter * _dot(qb, cm.astype(BF16)) + _dot(p.astype(BF16), vb)
    den = inter * jnp.sum(q * nv, axis=-1, keepdims=True) + jnp.sum(p, axis=-1, keepdims=True)
    h = num / jnp.maximum(jnp.abs(den), jnp.exp(-m_t))
    last = 0 if reverse else c - 1
    m_new = m_t[last:last + 1]
    b_exit = b_c[last:last + 1]
    w_c = jnp.exp(b_exit - b_c + ig_c - m_new)
    dec = jnp.exp(b_exit + m_prev - m_new)
    kw = k * w_c
    cm_new = dec * cm + _dot_tn(kw.astype(BF16), vb)
    nv_new = dec * nv + jnp.sum(kw, axis=0, keepdims=True)
    return h, cm_new, nv_new, m_new


def _mlstm_kernel(seq_len, q_ref, k_ref, v_ref, og_ref, gate_ref, gate_t_ref, cwq_ref, cwk_ref,
                  c0_ref, n0_ref, m0_ref,
                  o_ref, c_out_ref, n_out_ref, m_out_ref,
                  q2_ref, k2_ref, hf_ref, hb_ref, c_ref, n_ref, m_ref):
    c = SCAN_CHUNK
    n_chunks = seq_len // c
    q2_ref[...] = _silu(_short_conv3(q_ref[...], cwq_ref[...]))
    k2_ref[...] = _silu(_short_conv3(k_ref[...], cwk_ref[...])) * (HEAD_DIM ** -0.5)
    c_ref[...] = c0_ref[...]
    n_ref[...] = n0_ref[...]
    m_ref[...] = m0_ref[...]

    def body(n, carry):
        for reverse in (False, True):
            d = 1 if reverse else 0
            base = pl.multiple_of((n_chunks - 1 - n if reverse else n) * c, c)
            sl = pl.ds(base, c)
            gc = gate_ref[sl, :]
            gr = gate_t_ref[:, sl]
            h, cm, nv, m_new = _mlstm_chunk(
                reverse, q2_ref[sl, :], k2_ref[sl, :], v_ref[sl, :],
                gc[:, 2 * d:2 * d + 1], gc[:, 2 * d + 1:2 * d + 2],
                gr[2 * d:2 * d + 1, :], gr[2 * d + 1:2 * d + 2, :],
                c_ref[d], n_ref[d], m_ref[d])
            c_ref[d] = cm
            n_ref[d] = nv
            m_ref[d] = m_new
            (hb_ref if reverse else hf_ref)[sl, :] = h
        return carry

    lax.fori_loop(0, n_chunks, body, 0)
    h = hf_ref[...] + hb_ref[...]
    h = h * lax.rsqrt(jnp.mean(h * h, axis=-1, keepdims=True) + RMS_EPS)
    o_ref[...] = h * _sigmoid(og_ref[...])
    c_out_ref[...] = c_ref[...]
    n_out_ref[...] = n_ref[...]
    m_out_ref[...] = m_ref[...]


def _mlstm(p, gate_h, gate_t_h, conv_w, c0, n0, m0, n_seq, seq_len, tok_offset):
    hd = HEAD_DIM
    row0 = tok_offset // seq_len

    def col(part):
        return pl.BlockSpec((seq_len, hd), lambda b, h: (row0 + b, part * HEADS + h))

    c_spec = pl.BlockSpec((None, 2, None, hd, hd), lambda b, h: (b, 0, h, 0, 0))
    n_spec = pl.BlockSpec((None, 2, None, 1, hd), lambda b, h: (b, 0, h, 0, 0))
    m_spec = pl.BlockSpec((None, 2, None, 1, 1), lambda b, h: (b, 0, h, 0, 0))
    return pl.pallas_call(
        functools.partial(_mlstm_kernel, seq_len),
        grid=(n_seq, HEADS),
        in_specs=[col(5), col(6), col(7), col(8),
                  pl.BlockSpec((None, seq_len, 4), lambda b, h: (h, row0 + b, 0)),
                  pl.BlockSpec((None, 4, seq_len), lambda b, h: (h, 0, row0 + b)),
                  pl.BlockSpec((3, hd), lambda b, h: (0, h)),
                  pl.BlockSpec((3, hd), lambda b, h: (0, HEADS + h)),
                  c_spec, n_spec, m_spec],
        out_specs=[pl.BlockSpec((seq_len, hd), lambda b, h: (b, h)), c_spec, n_spec, m_spec],
        out_shape=[jax.ShapeDtypeStruct((n_seq * seq_len, GROUP_W), F32),
                   jax.ShapeDtypeStruct((n_seq, 2, HEADS, hd, hd), F32),
                   jax.ShapeDtypeStruct((n_seq, 2, HEADS, 1, hd), F32),
                   jax.ShapeDtypeStruct((n_seq, 2, HEADS, 1, 1), F32)],
        scratch_shapes=[pltpu.VMEM((seq_len, hd), F32), pltpu.VMEM((seq_len, hd), F32),
                        pltpu.VMEM((seq_len, hd), F32), pltpu.VMEM((seq_len, hd), F32),
                        pltpu.VMEM((2, hd, hd), F32), pltpu.VMEM((2, 1, hd), F32),
                        pltpu.VMEM((2, 1, 1), F32)],
        compiler_params=_cparams("parallel", "parallel"),
        name=f"mlstm_l{seq_len}",
    )(p, p, p, p, gate_h, gate_t_h, conv_w, conv_w, c0, n0, m0)


def _dft_tables(seq_len):
    n = 2 * seq_len
    k = jnp.arange(seq_len, dtype=jnp.int32)[:, None]
    t = jnp.arange(seq_len, dtype=jnp.int32)[None, :]
    ang = ((k * t) % n).astype(F32) * (2.0 * np.pi / n)
    fc = jnp.cos(ang)
    fs = jnp.sin(ang)
    nyq = jnp.where(t % 2 == 0, 1.0, -1.0).astype(F32)
    fs = jnp.where(k == 0, nyq, fs)
    return jnp.concatenate([fc, fs], axis=0)


def _filter_kernel(seq_len, z_ref, w1_ref, b1_ref, w2_ref, b2_ref, w3f_ref, w3b_ref, b3f_ref, b3b_ref,
                   f0_ref, f1_ref, rf_ref, rb_ref, fhi_ref, flo_ref, kf_ref):
    hp = lax.Precision.HIGHEST
    n = 2 * seq_len
    z = z_ref[...]
    a = jnp.sin(f0_ref[...] * (jnp.dot(z, w1_ref[...], precision=hp, preferred_element_type=F32) + b1_ref[...]))
    a = jnp.sin(f1_ref[...] * (jnp.dot(a, w2_ref[...], precision=hp, preferred_element_type=F32) + b2_ref[...]))
    t_norm = z[:, 0:1]
    hf = (jnp.dot(a, w3f_ref[...], precision=hp, preferred_element_type=F32) + b3f_ref[...]) \
        * jnp.exp(-t_norm * jnp.exp(rf_ref[...]))
    hb = (jnp.dot(a, w3b_ref[...], precision=hp, preferred_element_type=F32) + b3b_ref[...]) \
        * jnp.exp(-t_norm * jnp.exp(rb_ref[...]))
    inv = lax.rsqrt(jnp.sum(hf * hf, axis=0, keepdims=True) + jnp.sum(hb * hb, axis=0, keepdims=True))
    hf = hf * inv
    r = lax.broadcasted_iota(jnp.int32, (seq_len, 1), 0)
    hb = jnp.where(r == 0, 0.0, hb * inv)
    sh, sm, sl = _split3(hf + hb)
    dh, dm, dl = _split3(hf - hb)
    fhi = fhi_ref[...]
    flo = flo_ref[...]
    kc = _dot(fhi[:seq_len], sh) + _dot(fhi[:seq_len], sm) + _dot(fhi[:seq_len], sl) \
        + _dot(flo[:seq_len], sh) + _dot(flo[:seq_len], sm)
    ks = _dot(fhi[seq_len:], dh) + _dot(fhi[seq_len:], dm) + _dot(fhi[seq_len:], dl) \
        + _dot(flo[seq_len:], dh) + _dot(flo[seq_len:], dm)
    sign = jnp.where(r % 2 == 0, 1.0, -1.0)
    k_nyq = jnp.sum(sign * (hf + hb), axis=0, keepdims=True)
    ks = jnp.where(r == 0, k_nyq, ks)
    scale = jnp.where(r == 0, 1.0 / n, 2.0 / n)
    kf_ref[0:seq_len, :] = kc * scale
    kf_ref[seq_len:n, :] = ks * scale


def _hyena_filters(seq_len, w1, b1, w2, b2, w3, b3, freq, log_rate, f_tab):
    d = D_MODEL
    cb = 256
    t = jnp.arange(seq_len, dtype=F32)
    t_norm = t / (seq_len - 1)
    bands = jnp.linspace(1e-4, HY_BANDS - 1, HY_BANDS, dtype=F32)
    ang = (2.0 * np.pi / seq_len) * t[:, None] * bands[None, :]
    z = jnp.concatenate([t_norm[:, None], jnp.cos(ang), jnp.sin(ang)], axis=-1)
    kpad = 128 - HY_EMB
    z = jnp.pad(z, ((0, 0), (0, kpad)))
    w1p = jnp.pad(w1, ((0, kpad), (0, 0)))
    f_hi = f_tab.astype(BF16)
    f_lo = (f_tab - f_hi.astype(F32)).astype(BF16)
    n_cb = d // cb
    hh = HY_HIDDEN
    row = lambda a: a.reshape(1, -1)
    const = lambda shape: pl.BlockSpec(shape, lambda o, j: (0,) * len(shape))
    fwd = lambda rows: pl.BlockSpec((rows, cb), lambda o, j: (0, o * n_cb + j))
    bwd = lambda rows: pl.BlockSpec((rows, cb), lambda o, j: (0, (HY_ORDER + o) * n_cb + j))
    return pl.pallas_call(
        functools.partial(_filter_kernel, seq_len),
        grid=(HY_ORDER, n_cb),
        in_specs=[const((seq_len, 128)), const((128, hh)), const((1, hh)), const((hh, hh)), const((1, hh)),
                  fwd(hh), bwd(hh), fwd(1), bwd(1),
                  const((1, hh)), const((1, hh)), fwd(1), bwd(1),
                  const((2 * seq_len, seq_len)), const((2 * seq_len, seq_len))],
        out_specs=pl.BlockSpec((None, 2 * seq_len, cb), lambda o, j: (o, 0, j)),
        out_shape=jax.ShapeDtypeStruct((HY_ORDER, 2 * seq_len, d), F32),
        compiler_params=_cparams("parallel", "parallel"),
        name=f"hyena_filter_l{seq_len}",
    )(z, w1p, row(b1), w2, row(b2), w3, w3, row(b3), row(b3),
      row(freq[0]), row(freq[1]), row(log_rate), row(log_rate), f_hi, f_lo)


def _hyena_kernel(seq_len, v_ref, x1_ref, x2_ref, cwv_ref, cw1_ref, cw2_ref, bias_ref, kf_ref,
                  f_ref, ft_ref, o_ref):
    r = lax.broadcasted_iota(jnp.int32, (seq_len, 1), 0)
    z = _short_conv3(v_ref[...], cwv_ref[...])
    gates = (_short_conv3(x1_ref[...], cw1_ref[...]), _short_conv3(x2_ref[...], cw2_ref[...]))
    for o in range(HY_ORDER):
        zf = _dot(f_ref[...], z.astype(BF16))
        a, bm = zf[:seq_len], zf[seq_len:]
        kc, ks = kf_ref[o, 0:seq_len, :], kf_ref[o, seq_len:2 * seq_len, :]
        yc = a * kc - jnp.where(r == 0, 0.0, bm * ks)
        ys = jnp.where(r == 0, bm * ks, a * ks + bm * kc)
        y = _dot(ft_ref[:, 0:seq_len], yc.astype(BF16)) + _dot(ft_ref[:, seq_len:2 * seq_len], ys.astype(BF16))
        z = gates[o] * (y + z * bias_ref[o:o + 1, :])
    o_ref[...] = z


def _hyena(u, conv_w, bias, kf, f_tab, n_seq, seq_len, tok_offset):
    d = D_MODEL
    cb = 256
    n_cb = d // cb
    row0 = tok_offset // seq_len
    f_bf = f_tab.astype(BF16)

    def part(k):
        return pl.BlockSpec((seq_len, cb), lambda j, b: (row0 + b, k * n_cb + j))

    def cw(k):
        return pl.BlockSpec((3, cb), lambda j, b: (0, k * n_cb + j))

    return pl.pallas_call(
        functools.partial(_hyena_kernel, seq_len),
        grid=(n_cb, n_seq),
        in_specs=[part(0), part(1), part(2), cw(0), cw(1), cw(2),
                  pl.BlockSpec((HY_ORDER, cb), lambda j, b: (0, j)),
                  pl.BlockSpec((HY_ORDER, 2 * seq_len, cb), lambda j, b: (0, 0, j)),
                  pl.BlockSpec((2 * seq_len, seq_len), lambda j, b: (0, 0)),
                  pl.BlockSpec((seq_len, 2 * seq_len), lambda j, b: (0, 0))],
        out_specs=pl.BlockSpec((seq_len, cb), lambda j, b: (b, j)),
        out_shape=jax.ShapeDtypeStruct((n_seq * seq_len, d), F32),
        compiler_params=_cparams("parallel", "parallel"),
        name=f"hyena_l{seq_len}",
    )(u, u, u, conv_w, conv_w, conv_w, bias, kf, f_bf, f_bf.T)


def _mix_out_kernel(n_parts, *refs):
    o_refs = refs[:n_parts]
    w_refs = refs[n_parts:2 * n_parts]
    x_ref, mod_ref, g_ref, wr_hi_ref, wr_lo_ref, br_ref = refs[2 * n_parts:2 * n_parts + 6]
    x1_ref, h_ref, eid_ref, wts_ref, rank_ref, cnt_ref, run_ref = refs[2 * n_parts + 6:]
    i = pl.program_id(0)
    t = TOK_TILE
    ne = N_EXPERTS

    y = _dot(o_refs[0][...].astype(BF16), w_refs[0][...])
    for j in range(1, n_parts):
        y = y + _dot(o_refs[j][...].astype(BF16), w_refs[j][...])
    x1 = x_ref[...] + mod_ref[2:3, :] * y
    x1_ref[...] = x1
    h = _norm_mod(x1, g_ref[...], mod_ref[4:5, :], mod_ref[3:4, :])
    h_ref[...] = h
    h_hi = h.astype(BF16)
    h_lo = (h - h_hi.astype(F32)).astype(BF16)
    logits = _dot_nt(wr_hi_ref[...], h_hi) + _dot_nt(wr_lo_ref[...], h_hi) + _dot_nt(wr_hi_ref[...], h_lo) \
        + br_ref[...]

    @pl.when(i == 0)
    def _():
        run_ref[...] = jnp.zeros_like(run_ref)

    e_iota = lax.broadcasted_iota(jnp.int32, (ne, t), 0)
    r2 = lax.broadcasted_iota(jnp.int32, (t, t), 0)
    c2 = lax.broadcasted_iota(jnp.int32, (t, t), 1)
    before = jnp.where(r2 < c2, 1.0, 0.0).astype(BF16)
    running = run_ref[...]
    vals, eids, ranks = [], [], []
    for _k in range(TOP_K):
        m = jnp.max(logits, axis=0, keepdims=True)
        eid = jnp.min(jnp.where(logits == m, e_iota, ne), axis=0, keepdims=True)
        sel = e_iota == eid
        logits = jnp.where(sel, -jnp.inf, logits)
        onehot = jnp.where(sel, 1.0, 0.0)
        earlier = _dot(onehot.astype(BF16), before)
        ranks.append(jnp.sum(onehot * (running + earlier), axis=0, keepdims=True))
        running = running + jnp.sum(onehot, axis=1, keepdims=True)
        vals.append(m)
        eids.append(eid)
    run_ref[...] = running
    cnt_ref[...] = running
    v = jnp.concatenate(vals, axis=0)
    ex = jnp.exp(v - v[0:1])
    wts_ref[...] = ex / jnp.sum(ex, axis=0, keepdims=True)
    eid_ref[...] = jnp.concatenate(eids, axis=0)
    rank_ref[...] = jnp.concatenate(ranks, axis=0).astype(jnp.int32)


def _mix_out(parts, x, mod_l, norm_g, w_router, b_router):
    d = D_MODEL
    t = TOK_TILE
    ne = N_EXPERTS
    n_parts = len(parts)
    wr_t = w_router.T
    wr_hi = wr_t.astype(BF16)
    wr_lo = (wr_t - wr_hi.astype(F32)).astype(BF16)
    in_specs = [pl.BlockSpec((t, o.shape[1]), lambda i: (i, 0)) for o, _ in parts]
    in_specs += [pl.BlockSpec(w.shape, lambda i: (0, 0)) for _, w in parts]
    in_specs += [
        pl.BlockSpec((t, d), lambda i: (i, 0)),
        pl.BlockSpec((None, N_MOD, d), lambda i: (_tile_cond_row(i), 0, 0)),
        pl.BlockSpec((1, d), lambda i: (0, 0)),
        pl.BlockSpec((ne, d), lambda i: (0, 0)),
        pl.BlockSpec((ne, d), lambda i: (0, 0)),
        pl.BlockSpec((ne, 1), lambda i: (0, 0)),
    ]
    tok_major = pl.BlockSpec((TOP_K, t), lambda i: (0, i))
    return pl.pallas_call(
        functools.partial(_mix_out_kernel, n_parts),
        grid=(N_TOK // t,),
        in_specs=in_specs,
        out_specs=[pl.BlockSpec((t, d), lambda i: (i, 0)), pl.BlockSpec((t, d), lambda i: (i, 0)),
                   tok_major, tok_major, tok_major, pl.BlockSpec((ne, 1), lambda i: (0, 0))],
        out_shape=[jax.ShapeDtypeStruct((N_TOK, d), F32), jax.ShapeDtypeStruct((N_TOK, d), F32),
                   jax.ShapeDtypeStruct((TOP_K, N_TOK), jnp.int32), jax.ShapeDtypeStruct((TOP_K, N_TOK), F32),
                   jax.ShapeDtypeStruct((TOP_K, N_TOK), jnp.int32), jax.ShapeDtypeStruct((ne, 1), F32)],
        scratch_shapes=[pltpu.VMEM((ne, 1), F32)],
        compiler_params=_cparams("arbitrary"),
        name="mix_out_router",
    )(*[o for o, _ in parts], *[w.astype(BF16) for _, w in parts], x, mod_l, norm_g.reshape(1, d),
      wr_hi, wr_lo, b_router.reshape(ne, 1))


def _row_copy_kernel(gather, n_src, idx_ref, src_ref, *rest):
    dst_ref, sem = rest[-2], rest[-1]
    base = pl.program_id(0) * COPY_ROWS
    src_base = 0 if (gather or n_src == 1) else lax.rem(base, n_src)

    def copy(j):
        moved = idx_ref[0, 0, j]
        if gather:
            s, d = moved, base + j
        elif n_src == 1:
            s, d = 0, moved
        else:
            s, d = src_base + j, moved
        return pltpu.make_async_copy(src_ref.at[pl.ds(s, 1)], dst_ref.at[pl.ds(d, 1)], sem)

    def issue(j, carry):
        for u in range(COPY_UNROLL):
            copy(j * COPY_UNROLL + u).start()
        return carry

    def drain(j, carry):
        for u in range(COPY_UNROLL):
            copy(j * COPY_UNROLL + u).wait()
        return carry

    lax.fori_loop(0, COPY_ROWS // COPY_UNROLL, issue, 0)
    lax.fori_loop(0, COPY_ROWS // COPY_UNROLL, drain, 0)


def _row_scatter(src, idx, dst=None, dst_rows=None):
    n = idx.shape[0]
    n_src = src.shape[0]
    if not (n_src == 1 or n_src % COPY_ROWS == 0) or n % COPY_ROWS:
        raise ValueError("row counts must be whole copy steps")
    steps = n // COPY_ROWS
    in_specs = [pl.BlockSpec((1, 1, COPY_ROWS), lambda i: (i, 0, 0), memory_space=pltpu.SMEM),
                pl.BlockSpec(memory_space=pl.ANY)]
    args = [idx.reshape(steps, 1, COPY_ROWS), src]
    aliases = {}
    if dst is not None:
        in_specs.append(pl.BlockSpec(memory_space=pl.ANY))
        args.append(dst)
        aliases = {2: 0}
        dst_rows = dst.shape[0]
    return pl.pallas_call(
        functools.partial(_row_copy_kernel, False, n_src),
        grid=(steps,),
        in_specs=in_specs,
        out_specs=pl.BlockSpec(memory_space=pl.ANY),
        out_shape=jax.ShapeDtypeStruct((dst_rows, src.shape[1]), src.dtype),
        scratch_shapes=[pltpu.SemaphoreType.DMA(())],
        input_output_aliases=aliases,
        compiler_params=pltpu.CompilerParams(dimension_semantics=("arbitrary",)),
        name="row_scatter" if n_src > 1 else "row_fill",
    )(*args)


def _row_gather(src, idx):
    n = idx.shape[0]
    if n % COPY_ROWS:
        raise ValueError("row count must be whole copy steps")
    steps = n // COPY_ROWS
    return pl.pallas_call(
        functools.partial(_row_copy_kernel, True, src.shape[0]),
        grid=(steps,),
        in_specs=[pl.BlockSpec((1, 1, COPY_ROWS), lambda i: (i, 0, 0), memory_space=pltpu.SMEM),
                  pl.BlockSpec(memory_space=pl.ANY)],
        out_specs=pl.BlockSpec(memory_space=pl.ANY),
        out_shape=jax.ShapeDtypeStruct((n, src.shape[1]), src.dtype),
        scratch_shapes=[pltpu.SemaphoreType.DMA(())],
        compiler_params=pltpu.CompilerParams(dimension_semantics=("arbitrary",)),
        name="row_gather",
    )(idx.reshape(steps, 1, COPY_ROWS), src)


def _sc_row_gather(src, idx):
    n = idx.shape[0]
    width = src.shape[1]
    sc = plsc.get_sparse_core_info()
    n_workers = sc.num_cores * sc.num_subcores
    per_worker = n // n_workers
    if n % n_workers or per_worker % SC_ROWS:
        raise ValueError("row count must be whole SparseCore steps on every subcore")
    mesh = plsc.VectorSubcoreMesh(core_axis_name="core", subcore_axis_name="subcore")

    @functools.partial(pl.kernel, out_type=jax.ShapeDtypeStruct((n, width), src.dtype), mesh=mesh,
                       scratch_types=[pltpu.VMEM((per_worker,), jnp.int32),
                                      pltpu.VMEM((SC_ROWS, width), src.dtype)],
                       name="sc_row_gather")
    def gather(src_hbm, idx_hbm, dst_hbm, idx_vmem, rows_vmem):
        worker = lax.axis_index("subcore") * sc.num_cores + lax.axis_index("core")
        base = worker * per_worker
        pltpu.sync_copy(idx_hbm.at[pl.ds(base, per_worker)], idx_vmem)

        @pl.loop(0, per_worker // SC_ROWS)
        def _(c):
            pltpu.sync_copy(src_hbm.at[idx_vmem.at[pl.ds(c * SC_ROWS, SC_ROWS)]], rows_vmem)
            pltpu.sync_copy(rows_vmem, dst_hbm.at[pl.ds(base + c * SC_ROWS, SC_ROWS)])

    return gather(src, idx)


def _experts_kernel(te_ref, first_ref, nv_ref, x_ref, wgu_ref, bg_ref, bu_ref, wd_ref, bd_ref,
                    sel_even_ref, sel_odd_ref, y_ref, wg_ref, wu_ref, wdb_ref):
    del te_ref
    i = pl.program_id(0)
    valid = i < nv_ref[0]
    half = DEINT_COLS // 2

    @pl.when(jnp.logical_and(valid, first_ref[i] == 1))
    def _():
        for c in range(wgu_ref.shape[1] // DEINT_COLS):
            w = wgu_ref[:, c * DEINT_COLS:(c + 1) * DEINT_COLS].astype(BF16)
            wg_ref[:, c * half:(c + 1) * half] = _dot(w, sel_even_ref[...]).astype(BF16)
            wu_ref[:, c * half:(c + 1) * half] = _dot(w, sel_odd_ref[...]).astype(BF16)
        wdb_ref[...] = wd_ref[...].astype(BF16)

    @pl.when(valid)
    def _():
        x = x_ref[...].astype(BF16)
        gl = _dot(x, wg_ref[...]) + bg_ref[...]
        up = _dot(x, wu_ref[...]) + bu_ref[...]
        gl = jnp.minimum(gl, SWIGLU_LIMIT)
        up = jnp.clip(up, -SWIGLU_LIMIT, SWIGLU_LIMIT)
        act = (up + 1.0) * gl * _sigmoid(SWIGLU_ALPHA * gl)
        y_ref[...] = _dot(act.astype(BF16), wdb_ref[...]) + bd_ref[...]

    @pl.when(jnp.logical_not(valid))
    def _():
        y_ref[...] = jnp.zeros_like(y_ref)


def _experts(xs, tile_expert, tile_first, n_valid, w_gu, b_gate, b_up, w_down, b_down):
    d = D_MODEL
    tm = MOE_TILE
    n_tiles = MOE_ROWS // tm
    ff = w_down.shape[1]
    half = DEINT_COLS // 2
    r = jnp.arange(DEINT_COLS)[:, None]
    c = jnp.arange(half)[None, :]
    sel_even = (r == 2 * c).astype(BF16)
    sel_odd = (r == 2 * c + 1).astype(BF16)
    wspec = lambda k, n: pl.BlockSpec((None, k, n), lambda i, te, fi, nv: (te[i], 0, 0))
    const = lambda shape: pl.BlockSpec(shape, lambda i, te, fi, nv: (0, 0))
    grid_spec = pltpu.PrefetchScalarGridSpec(
        num_scalar_prefetch=3,
        grid=(n_tiles,),
        in_specs=[pl.BlockSpec((tm, d), lambda i, te, fi, nv: (i, 0)),
                  wspec(d, 2 * ff), wspec(1, ff), wspec(1, ff), wspec(ff, d), wspec(1, d),
                  const((DEINT_COLS, half)), const((DEINT_COLS, half))],
        out_specs=pl.BlockSpec((tm, d), lambda i, te, fi, nv: (i, 0)),
        scratch_shapes=[pltpu.VMEM((d, ff), BF16), pltpu.VMEM((d, ff), BF16), pltpu.VMEM((ff, d), BF16)],
    )
    return pl.pallas_call(
        _experts_kernel,
        grid_spec=grid_spec,
        out_shape=jax.ShapeDtypeStruct((MOE_ROWS, d), F32),
        compiler_params=_cparams("arbitrary"),
        name="experts",
    )(tile_expert, tile_first, n_valid, xs, w_gu, b_gate, b_up, w_down, b_down, sel_even, sel_odd)


def _combine_kernel(final, x_ref, g_ref, w_ref, mod_ref, fg_ref, o_ref):
    y = w_ref[:, 0:1] * g_ref[0]
    for k in range(1, TOP_K):
        y = y + w_ref[:, k:k + 1] * g_ref[k]
    x = x_ref[...] + mod_ref[5:6, :] * y
    if final:
        x = x * lax.rsqrt(jnp.mean(x * x, axis=-1, keepdims=True) + RMS_EPS) * fg_ref[...]
    o_ref[...] = x


def _combine(x1, gathered, wts, mod_l, final_g, final):
    d = D_MODEL
    t = TOK_TILE
    return pl.pallas_call(
        functools.partial(_combine_kernel, final),
        grid=(N_TOK // t,),
        in_specs=[pl.BlockSpec((t, d), lambda i: (i, 0)),
                  pl.BlockSpec((TOP_K, t, d), lambda i: (0, i, 0)),
                  pl.BlockSpec((t, TOP_K), lambda i: (i, 0)),
                  pl.BlockSpec((None, N_MOD, d), lambda i: (_tile_cond_row(i), 0, 0)),
                  pl.BlockSpec((1, d), lambda i: (0, 0))],
        out_specs=pl.BlockSpec((t, d), lambda i: (i, 0)),
        out_shape=jax.ShapeDtypeStruct((N_TOK, d), F32),
        compiler_params=_cparams("parallel"),
        name="moe_combine",
    )(x1, gathered, wts, mod_l, final_g.reshape(1, d))


def _moe(x1, h, eid, wts, rank, counts, mod_l, w_gu, b_gu, w_down, b_down, final_g, final):
    d = D_MODEL
    tm = MOE_TILE
    n_tiles = MOE_ROWS // tm
    cnt = counts.reshape(N_EXPERTS).astype(jnp.int32)
    gsz = ((cnt + tm - 1) // tm) * tm
    ends = jnp.cumsum(gsz)
    offs = ends - gsz
    e_ids = jnp.arange(N_EXPERTS, dtype=jnp.int32)
    pos = jnp.sum(jnp.where(eid[..., None] == e_ids, offs, 0), axis=-1) + rank
    tile_start = jnp.arange(n_tiles, dtype=jnp.int32) * tm
    tile_expert = jnp.minimum(jnp.sum((ends[None, :] <= tile_start[:, None]).astype(jnp.int32), axis=1),
                              N_EXPERTS - 1)
    n_valid = (ends[-1:] // tm).astype(jnp.int32)
    last_valid = jnp.maximum(n_valid[0] - 1, 0)
    tile_expert = jnp.where(jnp.arange(n_tiles) < n_valid[0], tile_expert, tile_expert[last_valid])
    tile_first = jnp.concatenate([jnp.ones((1,), jnp.int32),
                                  (tile_expert[1:] != tile_expert[:-1]).astype(jnp.int32)])
    j = jnp.arange(tm, dtype=jnp.int32)[None, :]
    pad_used = j < (gsz - cnt)[:, None]
    n_unused_before = jnp.cumsum((~pad_used).reshape(-1).astype(jnp.int32)) - 1
    pad_pos = jnp.where(pad_used, (offs + cnt)[:, None] + j,
                        ends[-1] + n_unused_before.reshape(N_EXPERTS, tm))
    slot_of_row = jnp.argsort(jnp.concatenate([pos.reshape(-1), pad_pos.reshape(-1)]))
    token_of_row = (slot_of_row % N_TOK).astype(jnp.int32)

    xs = _sc_row_gather(h, token_of_row)
    ys = _experts(xs, tile_expert, tile_first, n_valid, w_gu,
                  b_gu[:, None, 0::2], b_gu[:, None, 1::2], w_down, b_down[:, None, :])
    gathered = _sc_row_gather(ys, pos.reshape(-1))
    return _combine(x1, gathered.reshape(TOP_K, N_TOK, d), wts.T, mod_l, final_g, final)


def _grid_positions(n_tok, d):
    rows = n_tok // GRID_W
    r, col = jnp.meshgrid(jnp.arange(rows, dtype=F32), jnp.arange(GRID_W, dtype=F32), indexing='ij')
    r = r.reshape(-1)
    col = col.reshape(-1)
    quarter = d // 4
    inv = 1.0 / (10000.0 ** (jnp.arange(quarter, dtype=F32) / quarter))
    ar = r[:, None] * inv[None]
    ac = col[:, None] * inv[None]
    return jnp.concatenate([jnp.sin(ar), jnp.cos(ar), jnp.sin(ac), jnp.cos(ac)], axis=-1)


def kernel(x_prompt, x_sample, state_hgrn, state_mlstm_c, state_mlstm_n, state_mlstm_m, c, c_ctx,
           norm_g, final_g, w_mod, b_mod, ev_w_in, ev_gate_b, ev_conv, hg_lb, ev_w_out,
           hy_w_in, hy_conv, hy_w1, hy_b1, hy_w2, hy_b2, hy_w3, hy_b3, hy_freq, hy_log_rate, hy_bias, hy_w_out,
           w_router, b_router, w_gu, b_gu, w_down, b_down):
    d = D_MODEL
    hd = HEAD_DIM
    cond = jnp.concatenate([c_ctx[None], c, jnp.zeros((N_COND - 1 - DEC_BATCH, d), F32)], axis=0)
    mod = _modulation(cond, w_mod, b_mod)
    x = jnp.concatenate([x_prompt.reshape(N_PROMPT, d), x_sample.reshape(N_SAMPLE, d)], axis=0)
    pos_tab = jnp.concatenate([jnp.zeros((TOK_TILE, d), F32), _grid_positions(DEC_SEQ, d)], axis=0)

    groups = ((BATCH, SEQ, 0), (DEC_BATCH, DEC_SEQ, N_PROMPT))
    new_states = None
    for l in range(DEPTH):
        if l % 2 == 0:
            e = l // 2
            if l == 0:
                x, p, gate, gate_t = _proj_even(x, pos_tab, mod[l], norm_g[l, 0], ev_w_in[e], ev_gate_b[e])
            else:
                raise NotImplementedError("only the first layer adds grid positions")
            gate_h = gate.reshape(N_TOK, 4, HEADS).transpose(2, 0, 1)
            gate_t_h = gate_t.reshape(4, HEADS, N_TOK).transpose(1, 0, 2)
            o_hg, o_ml = [], []
            for gi, (n_seq, seq_len, off) in enumerate(groups):
                if gi == 0:
                    s0 = jnp.zeros((n_seq, 2, HEADS, hd, hd), F32)
                    c0 = jnp.zeros((n_seq, 2, HEADS, hd, hd), F32)
                    n0 = jnp.zeros((n_seq, 2, HEADS, 1, hd), F32)
                    m0 = jnp.zeros((n_seq, 2, HEADS, 1, 1), F32)
                else:
                    s0 = state_hgrn[:, e]
                    c0 = state_mlstm_c[:, e]
                    n0 = state_mlstm_n[:, e].reshape(n_seq, 2, HEADS, 1, hd)
                    m0 = state_mlstm_m[:, e].reshape(n_seq, 2, HEADS, 1, 1)
                og, s_fin = _hgrn(p, hg_lb, l, s0, n_seq, seq_len, off)
                om, c_fin, n_fin, m_fin = _mlstm(p, gate_h, gate_t_h, ev_conv[e], c0, n0, m0, n_seq, seq_len, off)
                o_hg.append(og)
                o_ml.append(om)
                if gi == 0:
                    new_states = (s_fin[:, None], c_fin[:, None],
                                  n_fin.reshape(n_seq, 1, 2, HEADS, hd), m_fin.reshape(n_seq, 1, 2, HEADS))
            parts = [(jnp.concatenate(o_hg, axis=0), ev_w_out[e][:GROUP_W]),
                     (jnp.concatenate(o_ml, axis=0), ev_w_out[e][GROUP_W:])]
        else:
            o = l // 2
            u = _proj_odd(x, mod[l], norm_g[l, 0], hy_w_in[o])
            zs = []
            for n_seq, seq_len, off in groups:
                f_tab = _dft_tables(seq_len)
                kf = _hyena_filters(seq_len, hy_w1[o], hy_b1[o], hy_w2[o], hy_b2[o], hy_w3[o], hy_b3[o],
                                    hy_freq[o], hy_log_rate[o], f_tab)
                zs.append(_hyena(u, hy_conv[o], hy_bias[o], kf, f_tab, n_seq, seq_len, off))
            parts = [(jnp.concatenate(zs, axis=0), hy_w_out[o])]
        x1, h, eid, wts, rank, counts = _mix_out(parts, x, mod[l], norm_g[l, 1], w_router[l], b_router[l])
        x = _moe(x1, h, eid, wts, rank, counts, mod[l], w_gu[l], b_gu[l], w_down[l], b_down[l],
                 final_g, final=(l == DEPTH - 1))

    y_prompt = x[:N_PROMPT].reshape(BATCH, SEQ, d)
    y_sample = x[N_PROMPT:].reshape(DEC_BATCH, DEC_SEQ, d)
    return (y_prompt, y_sample) + new_states
```

```python
import functools

import numpy as np
import jax
import jax.numpy as jnp
from jax import lax
from jax.experimental import pallas as pl
from jax.experimental.pallas import tpu as pltpu
from jax.experimental.pallas import tpu_sc as plsc

F32 = jnp.float32
BF16 = jnp.bfloat16

D_MODEL = 1024
BATCH = 32
SEQ = 256
DEPTH = 2
DEC_BATCH = 8
DEC_SEQ = 1024
GRID_W = 64
RMS_EPS = 1e-6
N_MOD = 6

HEADS = 4
HEAD_DIM = 128
GROUP_W = HEADS * HEAD_DIM
N_GATES = 4 * HEADS
EVEN_MAIN = 9 * GROUP_W

HY_ORDER = 2
HY_BANDS = 16
HY_EMB = 1 + 2 * HY_BANDS
HY_HIDDEN = 64

N_EXPERTS = 32
TOP_K = 4
SWIGLU_LIMIT = 7.0
SWIGLU_ALPHA = 1.702

N_PROMPT = BATCH * SEQ
N_SAMPLE = DEC_BATCH * DEC_SEQ
N_TOK = N_PROMPT + N_SAMPLE
N_COND = 16

TOK_TILE = 256
SCAN_CHUNK = 128
SUB = 16
MOE_TILE = 256
MOE_ROWS = N_TOK * TOP_K + N_EXPERTS * MOE_TILE
DEINT_COLS = 512
SC_ROWS = 64

VMEM_LIMIT = 56 * 1024 * 1024


def _cparams(*sem):
    return pltpu.CompilerParams(dimension_semantics=sem, vmem_limit_bytes=VMEM_LIMIT)


def _split3(x):
    hi = x.astype(BF16)
    r = x - hi.astype(F32)
    mid = r.astype(BF16)
    lo = (r - mid.astype(F32)).astype(BF16)
    return hi, mid, lo


def _dot(a, b):
    return jnp.dot(a, b, preferred_element_type=F32)


def _dot_nt(a, b):
    return lax.dot_general(a, b, (((1,), (1,)), ((), ())), preferred_element_type=F32)


def _dot_tn(a, b):
    return lax.dot_general(a, b, (((0,), (0,)), ((), ())), preferred_element_type=F32)


def _dot_w3(a_exact_bf16, x):
    hi, mid, lo = _split3(x)
    return _dot(a_exact_bf16, hi) + _dot(a_exact_bf16, mid) + _dot(a_exact_bf16, lo)


def _sigmoid(x):
    return 1.0 / (1.0 + jnp.exp(-x))


def _silu(x):
    return x * _sigmoid(x)


def _log_sigmoid(x):
    return jnp.minimum(x, 0.0) - jnp.log(1.0 + jnp.exp(-jnp.abs(x)))


def _tile_cond_row(i):
    n_prompt_tiles = N_PROMPT // TOK_TILE
    tiles_per_seq = DEC_SEQ // TOK_TILE
    return jnp.where(i < n_prompt_tiles, 0, 1 + (i - n_prompt_tiles) // tiles_per_seq)


def _mod_kernel(cond_ref, w_ref, b_ref, o_ref):
    a = _silu(cond_ref[...]).astype(BF16)
    o_ref[...] = _dot(a, w_ref[...].astype(BF16)) + b_ref[...]


def _modulation(cond, w_mod, b_mod):
    d = D_MODEL
    out = pl.pallas_call(
        _mod_kernel,
        grid=(DEPTH, N_MOD),
        in_specs=[
            pl.BlockSpec((N_COND, d), lambda l, j: (0, 0)),
            pl.BlockSpec((None, d, d), lambda l, j: (l, 0, j)),
            pl.BlockSpec((None, 1, d), lambda l, j: (l, 0, j)),
        ],
        out_specs=pl.BlockSpec((None, None, N_COND, d), lambda l, j: (l, j, 0, 0)),
        out_shape=jax.ShapeDtypeStruct((DEPTH, N_MOD, N_COND, d), F32),
        compiler_params=_cparams("parallel", "parallel"),
        name="modulation",
    )(cond, w_mod, b_mod.reshape(DEPTH, 1, N_MOD * d))
    return out.transpose(0, 2, 1, 3)


def _norm_mod(x, g_row, scale_row, shift_row):
    ms = jnp.mean(x * x, axis=-1, keepdims=True)
    y = x * lax.rsqrt(ms + RMS_EPS) * g_row
    return y * (1.0 + scale_row) + shift_row


def _proj_even_kernel(xp_ref, xs_ref, pos_ref, mod_ref, g_ref, w_ref, wg_ref, wgt_ref, gb_ref, gbt_ref,
                      xres_ref, p_ref, gate_ref, gate_t_ref):
    from_prompt = pl.program_id(0) < N_PROMPT // TOK_TILE
    x = jnp.where(from_prompt, xp_ref[...], xs_ref[...]) + pos_ref[...]
    xres_ref[...] = x
    h = _norm_mod(x, g_ref[...], mod_ref[1:2, :], mod_ref[0:1, :]).astype(BF16)
    p_ref[...] = _dot(h, w_ref[...])
    gate_ref[...] = _dot(h, wg_ref[...]) + gb_ref[...]
    gate_t_ref[...] = _dot_nt(wgt_ref[...], h) + gbt_ref[...]


def _proj_even(x_prompt, x_sample, pos_tab, mod_l, norm_g, w_in, gate_b):
    d = D_MODEL
    n_tiles = N_TOK // TOK_TILE
    n_prompt_tiles = N_PROMPT // TOK_TILE
    tiles_per_seq = DEC_SEQ // TOK_TILE
    w_main = w_in[:, :EVEN_MAIN].astype(BF16)
    w_gate = w_in[:, EVEN_MAIN:].astype(BF16)

    def pos_map(i):
        return (jnp.where(i < n_prompt_tiles, 0, 1 + (i - n_prompt_tiles) % tiles_per_seq), 0)

    return pl.pallas_call(
        _proj_even_kernel,
        grid=(n_tiles,),
        in_specs=[
            pl.BlockSpec((TOK_TILE, d), lambda i: (jnp.minimum(i, n_prompt_tiles - 1), 0)),
            pl.BlockSpec((TOK_TILE, d), lambda i: (jnp.maximum(i - n_prompt_tiles, 0), 0)),
            pl.BlockSpec((TOK_TILE, d), pos_map),
            pl.BlockSpec((None, N_MOD, d), lambda i: (_tile_cond_row(i), 0, 0)),
            pl.BlockSpec((1, d), lambda i: (0, 0)),
            pl.BlockSpec((d, EVEN_MAIN), lambda i: (0, 0)),
            pl.BlockSpec((d, N_GATES), lambda i: (0, 0)),
            pl.BlockSpec((N_GATES, d), lambda i: (0, 0)),
            pl.BlockSpec((1, N_GATES), lambda i: (0, 0)),
            pl.BlockSpec((N_GATES, 1), lambda i: (0, 0)),
        ],
        out_specs=[
            pl.BlockSpec((TOK_TILE, d), lambda i: (i, 0)),
            pl.BlockSpec((TOK_TILE, EVEN_MAIN), lambda i: (i, 0)),
            pl.BlockSpec((TOK_TILE, N_GATES), lambda i: (i, 0)),
            pl.BlockSpec((N_GATES, TOK_TILE), lambda i: (0, i)),
        ],
        out_shape=[
            jax.ShapeDtypeStruct((N_TOK, d), F32),
            jax.ShapeDtypeStruct((N_TOK, EVEN_MAIN), F32),
            jax.ShapeDtypeStruct((N_TOK, N_GATES), F32),
            jax.ShapeDtypeStruct((N_GATES, N_TOK), F32),
        ],
        compiler_params=_cparams("parallel"),
        name="proj_even",
    )(x_prompt, x_sample, pos_tab, mod_l, norm_g.reshape(1, d), w_main, w_gate, w_gate.T,
      gate_b.reshape(1, N_GATES), gate_b.reshape(N_GATES, 1))


def _proj_odd_kernel(x_ref, mod_ref, g_ref, w_ref, p_ref):
    h = _norm_mod(x_ref[...], g_ref[...], mod_ref[1:2, :], mod_ref[0:1, :]).astype(BF16)
    p_ref[...] = _dot(h, w_ref[...])


def _proj_odd(x, mod_l, norm_g, w_in):
    d = D_MODEL
    width = w_in.shape[1]
    return pl.pallas_call(
        _proj_odd_kernel,
        grid=(N_TOK // TOK_TILE,),
        in_specs=[
            pl.BlockSpec((TOK_TILE, d), lambda i: (i, 0)),
            pl.BlockSpec((None, N_MOD, d), lambda i: (_tile_cond_row(i), 0, 0)),
            pl.BlockSpec((1, d), lambda i: (0, 0)),
            pl.BlockSpec((d, width), lambda i: (0, 0)),
        ],
        out_specs=pl.BlockSpec((TOK_TILE, width), lambda i: (i, 0)),
        out_shape=jax.ShapeDtypeStruct((N_TOK, width), F32),
        compiler_params=_cparams("parallel"),
        name="proj_odd",
    )(x, mod_l, norm_g.reshape(1, d), w_in.astype(BF16))


def _hgrn_chunk(reverse, q, k, v, lf, st):
    c = SCAN_CHUNK
    row = lax.broadcasted_iota(jnp.int32, (c, c), 0)
    col = lax.broadcasted_iota(jnp.int32, (c, c), 1)
    tri = (col >= row) if reverse else (col <= row)
    b = _dot_w3(jnp.where(tri, 1.0, 0.0).astype(BF16), lf)

    lane = lax.broadcasted_iota(jnp.int32, (SUB, c), 1)
    sub_row = lax.broadcasted_iota(jnp.int32, (SUB, 1), 0)
    chunk_row = lax.broadcasted_iota(jnp.int32, (c, 1), 0)
    rows = []
    for i in range(c // SUB):
        lo, hi = i * SUB, (i + 1) * SUB
        qi, ki, bi = q[lo:hi], k[lo:hi], b[lo:hi]
        if reverse:
            has_off, edge = hi < c, hi
            outside = chunk_row >= hi
        else:
            has_off, edge = lo > 0, lo - 1
            outside = chunk_row < lo
        if has_off:
            beta = b[edge:edge + 1]
            qs = qi * jnp.exp(bi - beta)
            ks = k * jnp.exp(jnp.where(outside, beta - b, -jnp.inf))
            a_row = _dot_nt(qs.astype(BF16), ks.astype(BF16))
        else:
            a_row = jnp.zeros((SUB, c), F32)
        for s in range(SUB):
            valid = (sub_row <= s) if reverse else (sub_row >= s)
            dlog = jnp.where(valid, bi - bi[s:s + 1], -jnp.inf)
            a_col = jnp.sum(jnp.exp(dlog) * qi * ki[s:s + 1], axis=-1, keepdims=True)
            a_row = jnp.where(lane == lo + s, a_col, a_row)
        rows.append(a_row)
    attn = jnp.concatenate(rows, axis=0)
    o = _dot(attn.astype(BF16), v.astype(BF16)) + _dot_nt((q * jnp.exp(b)).astype(BF16), st.astype(BF16))
    b_exit = b[0:1] if reverse else b[c - 1:c]
    k_out = k * jnp.exp(b_exit - b)
    st_new = jnp.exp(b_exit) * st + _dot_tn(v.astype(BF16), k_out.astype(BF16))
    return o, st_new


def _hgrn_kernel(seq_len, layer, q_ref, i_ref, g_ref, ff_ref, fb_ref, lb_ref, s0_ref,
                 o_ref, s_out_ref, of_ref, ob_ref, st_ref):
    c = SCAN_CHUNK
    n_chunks = seq_len // c
    lbp = lb_ref[...]
    e = jnp.exp(lbp - jnp.max(lbp, axis=0, keepdims=True))
    lb = jnp.sum(e[0:layer + 1], axis=0, keepdims=True) / jnp.sum(e, axis=0, keepdims=True)

    st_ref[0] = s0_ref[0].T
    st_ref[1] = s0_ref[1].T

    def body(n, carry):
        for reverse in (False, True):
            d = 1 if reverse else 0
            base = pl.multiple_of((n_chunks - 1 - n if reverse else n) * c, c)
            sl = pl.ds(base, c)
            f = lb + (1.0 - lb) * _sigmoid((fb_ref if reverse else ff_ref)[sl, :])
            o, st_new = _hgrn_chunk(reverse, q_ref[sl, :], 1.0 - f, i_ref[sl, :], jnp.log(f), st_ref[d])
            st_ref[d] = st_new
            (ob_ref if reverse else of_ref)[sl, :] = o
        return carry

    lax.fori_loop(0, n_chunks, body, 0)
    o = of_ref[...] + ob_ref[...]
    o = o * lax.rsqrt(jnp.mean(o * o, axis=-1, keepdims=True) + RMS_EPS)
    o_ref[...] = o * _silu(g_ref[...])
    s_out_ref[0] = st_ref[0].T
    s_out_ref[1] = st_ref[1].T


def _hgrn(p, hg_lb, layer, s0, n_seq, seq_len, tok_offset):
    hd = HEAD_DIM
    row0 = tok_offset // seq_len

    def col(part):
        return pl.BlockSpec((seq_len, hd), lambda b, h: (row0 + b, part * HEADS + h))

    state_spec = pl.BlockSpec((None, 2, None, hd, hd), lambda b, h: (b, 0, h, 0, 0))
    return pl.pallas_call(
        functools.partial(_hgrn_kernel, seq_len, layer),
        grid=(n_seq, HEADS),
        in_specs=[col(0), col(1), col(2), col(3), col(4),
                  pl.BlockSpec((DEPTH + 1, hd), lambda b, h: (0, h)),
                  state_spec],
        out_specs=[pl.BlockSpec((seq_len, hd), lambda b, h: (b, h)), state_spec],
        out_shape=[jax.ShapeDtypeStruct((n_seq * seq_len, GROUP_W), F32),
                   jax.ShapeDtypeStruct((n_seq, 2, HEADS, hd, hd), F32)],
        scratch_shapes=[pltpu.VMEM((seq_len, hd), F32), pltpu.VMEM((seq_len, hd), F32),
                        pltpu.VMEM((2, hd, hd), F32)],
        compiler_params=_cparams("parallel", "parallel"),
        name=f"hgrn_l{seq_len}",
    )(p, p, p, p, p, hg_lb, s0)


def _short_conv3(x, w):
    n = x.shape[0]
    r = lax.broadcasted_iota(jnp.int32, (n, 1), 0)
    prev = jnp.where(r == 0, 0.0, pltpu.roll(x, 1, 0))
    nxt = jnp.where(r == n - 1, 0.0, pltpu.roll(x, n - 1, 0))
    return prev * w[0:1] + x * w[1:2] + nxt * w[2:3]


def _mlstm_chunk(reverse, q, k, v, ig_c, fg_c, ig_r, fg_r, cm, nv, m_prev):
    c = SCAN_CHUNK
    row = lax.broadcasted_iota(jnp.int32, (c, c), 0)
    col = lax.broadcasted_iota(jnp.int32, (c, c), 1)
    tri = (col >= row) if reverse else (col <= row)
    lf_c = _log_sigmoid(fg_c)
    lf_r = _log_sigmoid(fg_r)
    b_c = jnp.sum(jnp.where(tri, lf_r, 0.0), axis=-1, keepdims=True)
    tri_t = (row >= col) if reverse else (row <= col)
    b_r = jnp.sum(jnp.where(tri_t, lf_c, 0.0), axis=0, keepdims=True)
    dmat = jnp.where(tri, b_c - b_r + ig_r, -jnp.inf)
    m_t = jnp.maximum(b_c + m_prev, jnp.max(dmat, axis=-1, keepdims=True))
    qb, kb, vb = q.astype(BF16), k.astype(BF16), v.astype(BF16)
    p = jnp.exp(dmat - m_t) * _dot_nt(qb, kb)
    inter = jnp.exp(b_c + m_prev - m_t)
    num = inter * _dot(qb, cm.astype(BF16)) + _dot(p.astype(BF16), vb)
    den = inter * jnp.sum(q * nv, axis=-1, keepdims=True) + jnp.sum(p, axis=-1, keepdims=True)
    h = num / jnp.maximum(jnp.abs(den), jnp.exp(-m_t))
    last = 0 if reverse else c - 1
    m_new = m_t[last:last + 1]
    b_exit = b_c[last:last + 1]
    w_c = jnp.exp(b_exit - b_c + ig_c - m_new)
    dec = jnp.exp(b_exit + m_prev - m_new)
    kw = k * w_c
    cm_new = dec * cm + _dot_tn(kw.astype(BF16), vb)
    nv_new = dec * nv + jnp.sum(kw, axis=0, keepdims=True)
    return h, cm_new, nv_new, m_new


def _mlstm_kernel(seq_len, q_ref, k_ref, v_ref, og_ref, gate_ref, gate_t_ref, cwq_ref, cwk_ref,
                  c0_ref, n0_ref, m0_ref,
                  o_ref, c_out_ref, n_out_ref, m_out_ref,
                  q2_ref, k2_ref, hf_ref, hb_ref, c_ref, n_ref, m_ref):
    c = SCAN_CHUNK
    n_chunks = seq_len // c
    q2_ref[...] = _silu(_short_conv3(q_ref[...], cwq_ref[...]))
    k2_ref[...] = _silu(_short_conv3(k_ref[...], cwk_ref[...])) * (HEAD_DIM ** -0.5)
    c_ref[...] = c0_ref[...]
    n_ref[...] = n0_ref[...]
    m_ref[...] = m0_ref[...]

    def body(n, carry):
        for reverse in (False, True):
            d = 1 if reverse else 0
            base = pl.multiple_of((n_chunks - 1 - n if reverse else n) * c, c)
            sl = pl.ds(base, c)
            gc = gate_ref[sl, :]
            gr = gate_t_ref[:, sl]
            h, cm, nv, m_new = _mlstm_chunk(
                reverse, q2_ref[sl, :], k2_ref[sl, :], v_ref[sl, :],
                gc[:, 2 * d:2 * d + 1], gc[:, 2 * d + 1:2 * d + 2],
                gr[2 * d:2 * d + 1, :], gr[2 * d + 1:2 * d + 2, :],
                c_ref[d], n_ref[d], m_ref[d])
            c_ref[d] = cm
            n_ref[d] = nv
            m_ref[d] = m_new
            (hb_ref if reverse else hf_ref)[sl, :] = h
        return carry

    lax.fori_loop(0, n_chunks, body, 0)
    h = hf_ref[...] + hb_ref[...]
    h = h * lax.rsqrt(jnp.mean(h * h, axis=-1, keepdims=True) + RMS_EPS)
    o_ref[...] = h * _sigmoid(og_ref[...])
    c_out_ref[...] = c_ref[...]
    n_out_ref[...] = n_ref[...]
    m_out_ref[...] = m_ref[...]


def _mlstm(p, gate_h, gate_t_h, conv_w, c0, n0, m0, n_seq, seq_len, tok_offset):
    hd = HEAD_DIM
    row0 = tok_offset // seq_len

    def col(part):
        return pl.BlockSpec((seq_len, hd), lambda b, h: (row0 + b, part * HEADS + h))

    c_spec = pl.BlockSpec((None, 2, None, hd, hd), lambda b, h: (b, 0, h, 0, 0))
    n_spec = pl.BlockSpec((None, 2, None, 1, hd), lambda b, h: (b, 0, h, 0, 0))
    m_spec = pl.BlockSpec((None, 2, None, 1, 1), lambda b, h: (b, 0, h, 0, 0))
    return pl.pallas_call(
        functools.partial(_mlstm_kernel, seq_len),
        grid=(n_seq, HEADS),
        in_specs=[col(5), col(6), col(7), col(8),
                  pl.BlockSpec((None, seq_len, 4), lambda b, h: (h, row0 + b, 0)),
                  pl.BlockSpec((None, 4, seq_len), lambda b, h: (h, 0, row0 + b)),
                  pl.BlockSpec((3, hd), lambda b, h: (0, h)),
                  pl.BlockSpec((3, hd), lambda b, h: (0, HEADS + h)),
                  c_spec, n_spec, m_spec],
        out_specs=[pl.BlockSpec((seq_len, hd), lambda b, h: (b, h)), c_spec, n_spec, m_spec],
        out_shape=[jax.ShapeDtypeStruct((n_seq * seq_len, GROUP_W), F32),
                   jax.ShapeDtypeStruct((n_seq, 2, HEADS, hd, hd), F32),
                   jax.ShapeDtypeStruct((n_seq, 2, HEADS, 1, hd), F32),
                   jax.ShapeDtypeStruct((n_seq, 2, HEADS, 1, 1), F32)],
        scratch_shapes=[pltpu.VMEM((seq_len, hd), F32), pltpu.VMEM((seq_len, hd), F32),
                        pltpu.VMEM((seq_len, hd), F32), pltpu.VMEM((seq_len, hd), F32),
                        pltpu.VMEM((2, hd, hd), F32), pltpu.VMEM((2, 1, hd), F32),
                        pltpu.VMEM((2, 1, 1), F32)],
        compiler_params=_cparams("parallel", "parallel"),
        name=f"mlstm_l{seq_len}",
    )(p, p, p, p, gate_h, gate_t_h, conv_w, conv_w, c0, n0, m0)


def _dft_tables(seq_len):
    n = 2 * seq_len
    k = jnp.arange(seq_len, dtype=jnp.int32)[:, None]
    t = jnp.arange(seq_len, dtype=jnp.int32)[None, :]
    ang = ((k * t) % n).astype(F32) * (2.0 * np.pi / n)
    fc = jnp.cos(ang)
    fs = jnp.sin(ang)
    nyq = jnp.where(t % 2 == 0, 1.0, -1.0).astype(F32)
    fs = jnp.where(k == 0, nyq, fs)
    return jnp.concatenate([fc, fs], axis=0)


def _filter_kernel(seq_len, z_ref, w1_ref, b1_ref, w2_ref, b2_ref, w3f_ref, w3b_ref, b3f_ref, b3b_ref,
                   f0_ref, f1_ref, rf_ref, rb_ref, fhi_ref, flo_ref, kf_ref):
    hp = lax.Precision.HIGHEST
    n = 2 * seq_len
    z = z_ref[...]
    a = jnp.sin(f0_ref[...] * (jnp.dot(z, w1_ref[...], precision=hp, preferred_element_type=F32) + b1_ref[...]))
    a = jnp.sin(f1_ref[...] * (jnp.dot(a, w2_ref[...], precision=hp, preferred_element_type=F32) + b2_ref[...]))
    t_norm = z[:, 0:1]
    hf = (jnp.dot(a, w3f_ref[...], precision=hp, preferred_element_type=F32) + b3f_ref[...]) \
        * jnp.exp(-t_norm * jnp.exp(rf_ref[...]))
    hb = (jnp.dot(a, w3b_ref[...], precision=hp, preferred_element_type=F32) + b3b_ref[...]) \
        * jnp.exp(-t_norm * jnp.exp(rb_ref[...]))
    inv = lax.rsqrt(jnp.sum(hf * hf, axis=0, keepdims=True) + jnp.sum(hb * hb, axis=0, keepdims=True))
    hf = hf * inv
    r = lax.broadcasted_iota(jnp.int32, (seq_len, 1), 0)
    hb = jnp.where(r == 0, 0.0, hb * inv)
    sh, sm, sl = _split3(hf + hb)
    dh, dm, dl = _split3(hf - hb)
    fhi = fhi_ref[...]
    flo = flo_ref[...]
    kc = _dot(fhi[:seq_len], sh) + _dot(fhi[:seq_len], sm) + _dot(fhi[:seq_len], sl) \
        + _dot(flo[:seq_len], sh) + _dot(flo[:seq_len], sm)
    ks = _dot(fhi[seq_len:], dh) + _dot(fhi[seq_len:], dm) + _dot(fhi[seq_len:], dl) \
        + _dot(flo[seq_len:], dh) + _dot(flo[seq_len:], dm)
    sign = jnp.where(r % 2 == 0, 1.0, -1.0)
    k_nyq = jnp.sum(sign * (hf + hb), axis=0, keepdims=True)
    ks = jnp.where(r == 0, k_nyq, ks)
    scale = jnp.where(r == 0, 1.0 / n, 2.0 / n)
    kf_ref[0:seq_len, :] = kc * scale
    kf_ref[seq_len:n, :] = ks * scale


def _hyena_filters(seq_len, w1, b1, w2, b2, w3, b3, freq, log_rate, f_tab):
    d = D_MODEL
    cb = 256
    t = jnp.arange(seq_len, dtype=F32)
    t_norm = t / (seq_len - 1)
    bands = jnp.linspace(1e-4, HY_BANDS - 1, HY_BANDS, dtype=F32)
    ang = (2.0 * np.pi / seq_len) * t[:, None] * bands[None, :]
    z = jnp.concatenate([t_norm[:, None], jnp.cos(ang), jnp.sin(ang)], axis=-1)
    kpad = 128 - HY_EMB
    z = jnp.pad(z, ((0, 0), (0, kpad)))
    w1p = jnp.pad(w1, ((0, kpad), (0, 0)))
    f_hi = f_tab.astype(BF16)
    f_lo = (f_tab - f_hi.astype(F32)).astype(BF16)
    n_cb = d // cb
    hh = HY_HIDDEN
    row = lambda a: a.reshape(1, -1)
    const = lambda shape: pl.BlockSpec(shape, lambda o, j: (0,) * len(shape))
    fwd = lambda rows: pl.BlockSpec((rows, cb), lambda o, j: (0, o * n_cb + j))
    bwd = lambda rows: pl.BlockSpec((rows, cb), lambda o, j: (0, (HY_ORDER + o) * n_cb + j))
    return pl.pallas_call(
        functools.partial(_filter_kernel, seq_len),
        grid=(HY_ORDER, n_cb),
        in_specs=[const((seq_len, 128)), const((128, hh)), const((1, hh)), const((hh, hh)), const((1, hh)),
                  fwd(hh), bwd(hh), fwd(1), bwd(1),
                  const((1, hh)), const((1, hh)), fwd(1), bwd(1),
                  const((2 * seq_len, seq_len)), const((2 * seq_len, seq_len))],
        out_specs=pl.BlockSpec((None, 2 * seq_len, cb), lambda o, j: (o, 0, j)),
        out_shape=jax.ShapeDtypeStruct((HY_ORDER, 2 * seq_len, d), F32),
        compiler_params=_cparams("parallel", "parallel"),
        name=f"hyena_filter_l{seq_len}",
    )(z, w1p, row(b1), w2, row(b2), w3, w3, row(b3), row(b3),
      row(freq[0]), row(freq[1]), row(log_rate), row(log_rate), f_hi, f_lo)


def _hyena_kernel(seq_len, v_ref, x1_ref, x2_ref, cwv_ref, cw1_ref, cw2_ref, bias_ref, kf_ref,
                  f_ref, ft_ref, o_ref):
    r = lax.broadcasted_iota(jnp.int32, (seq_len, 1), 0)
    z = _short_conv3(v_ref[...], cwv_ref[...])
    gates = (_short_conv3(x1_ref[...], cw1_ref[...]), _short_conv3(x2_ref[...], cw2_ref[...]))
    for o in range(HY_ORDER):
        zf = _dot(f_ref[...], z.astype(BF16))
        a, bm = zf[:seq_len], zf[seq_len:]
        kc, ks = kf_ref[o, 0:seq_len, :], kf_ref[o, seq_len:2 * seq_len, :]
        yc = a * kc - jnp.where(r == 0, 0.0, bm * ks)
        ys = jnp.where(r == 0, bm * ks, a * ks + bm * kc)
        y = _dot(ft_ref[:, 0:seq_len], yc.astype(BF16)) + _dot(ft_ref[:, seq_len:2 * seq_len], ys.astype(BF16))
        z = gates[o] * (y + z * bias_ref[o:o + 1, :])
    o_ref[...] = z


def _hyena(u, conv_w, bias, kf, f_tab, n_seq, seq_len, tok_offset):
    d = D_MODEL
    cb = 256
    n_cb = d // cb
    row0 = tok_offset // seq_len
    f_bf = f_tab.astype(BF16)

    def part(k):
        return pl.BlockSpec((seq_len, cb), lambda j, b: (row0 + b, k * n_cb + j))

    def cw(k):
        return pl.BlockSpec((3, cb), lambda j, b: (0, k * n_cb + j))

    return pl.pallas_call(
        functools.partial(_hyena_kernel, seq_len),
        grid=(n_cb, n_seq),
        in_specs=[part(0), part(1), part(2), cw(0), cw(1), cw(2),
                  pl.BlockSpec((HY_ORDER, cb), lambda j, b: (0, j)),
                  pl.BlockSpec((HY_ORDER, 2 * seq_len, cb), lambda j, b: (0, 0, j)),
                  pl.BlockSpec((2 * seq_len, seq_len), lambda j, b: (0, 0)),
                  pl.BlockSpec((seq_len, 2 * seq_len), lambda j, b: (0, 0))],
        out_specs=pl.BlockSpec((seq_len, cb), lambda j, b: (b, j)),
        out_shape=jax.ShapeDtypeStruct((n_seq * seq_len, d), F32),
        compiler_params=_cparams("parallel", "parallel"),
        name=f"hyena_l{seq_len}",
    )(u, u, u, conv_w, conv_w, conv_w, bias, kf, f_bf, f_bf.T)


def _mix_out_kernel(n_parts, *refs):
    o_refs = refs[:2 * n_parts]
    w_refs = refs[2 * n_parts:3 * n_parts]
    x_ref, mod_ref, g_ref, wr_hi_ref, wr_lo_ref, br_ref = refs[3 * n_parts:3 * n_parts + 6]
    x1_ref, h_ref, eid_ref, wts_ref, rank_ref, cnt_ref, run_ref = refs[3 * n_parts + 6:]
    i = pl.program_id(0)
    t = TOK_TILE
    ne = N_EXPERTS

    from_prompt = i < N_PROMPT // TOK_TILE
    y = None
    for j in range(n_parts):
        o = jnp.where(from_prompt, o_refs[2 * j][...], o_refs[2 * j + 1][...])
        yj = _dot(o.astype(BF16), w_refs[j][...])
        y = yj if y is None else y + yj
    x1 = x_ref[...] + mod_ref[2:3, :] * y
    x1_ref[...] = x1
    h = _norm_mod(x1, g_ref[...], mod_ref[4:5, :], mod_ref[3:4, :])
    h_ref[...] = h
    h_hi = h.astype(BF16)
    h_lo = (h - h_hi.astype(F32)).astype(BF16)
    logits = _dot_nt(wr_hi_ref[...], h_hi) + _dot_nt(wr_lo_ref[...], h_hi) + _dot_nt(wr_hi_ref[...], h_lo) \
        + br_ref[...]

    @pl.when(i == 0)
    def _():
        run_ref[...] = jnp.zeros_like(run_ref)

    e_iota = lax.broadcasted_iota(jnp.int32, (ne, t), 0)
    r2 = lax.broadcasted_iota(jnp.int32, (t, t), 0)
    c2 = lax.broadcasted_iota(jnp.int32, (t, t), 1)
    before = jnp.where(r2 < c2, 1.0, 0.0).astype(BF16)
    running = run_ref[...]
    vals, eids, ranks = [], [], []
    for _k in range(TOP_K):
        m = jnp.max(logits, axis=0, keepdims=True)
        eid = jnp.min(jnp.where(logits == m, e_iota, ne), axis=0, keepdims=True)
        sel = e_iota == eid
        logits = jnp.where(sel, -jnp.inf, logits)
        onehot = jnp.where(sel, 1.0, 0.0)
        earlier = _dot(onehot.astype(BF16), before)
        ranks.append(jnp.sum(onehot * (running + earlier), axis=0, keepdims=True))
        running = running + jnp.sum(onehot, axis=1, keepdims=True)
        vals.append(m)
        eids.append(eid)
    run_ref[...] = running
    cnt_ref[...] = running
    v = jnp.concatenate(vals, axis=0)
    ex = jnp.exp(v - v[0:1])
    wts_ref[...] = ex / jnp.sum(ex, axis=0, keepdims=True)
    eid_ref[...] = jnp.concatenate(eids, axis=0)
    rank_ref[...] = jnp.concatenate(ranks, axis=0).astype(jnp.int32)


def _mix_out(parts, x, mod_l, norm_g, w_router, b_router):
    d = D_MODEL
    t = TOK_TILE
    ne = N_EXPERTS
    n_parts = len(parts)
    n_prompt_tiles = N_PROMPT // t
    wr_t = w_router.T
    wr_hi = wr_t.astype(BF16)
    wr_lo = (wr_t - wr_hi.astype(F32)).astype(BF16)
    in_specs = []
    for o_p, _, _ in parts:
        in_specs.append(pl.BlockSpec((t, o_p.shape[1]), lambda i: (jnp.minimum(i, n_prompt_tiles - 1), 0)))
        in_specs.append(pl.BlockSpec((t, o_p.shape[1]), lambda i: (jnp.maximum(i - n_prompt_tiles, 0), 0)))
    in_specs += [pl.BlockSpec(w.shape, lambda i: (0, 0)) for _, _, w in parts]
    in_specs += [
        pl.BlockSpec((t, d), lambda i: (i, 0)),
        pl.BlockSpec((None, N_MOD, d), lambda i: (_tile_cond_row(i), 0, 0)),
        pl.BlockSpec((1, d), lambda i: (0, 0)),
        pl.BlockSpec((ne, d), lambda i: (0, 0)),
        pl.BlockSpec((ne, d), lambda i: (0, 0)),
        pl.BlockSpec((ne, 1), lambda i: (0, 0)),
    ]
    tok_major = pl.BlockSpec((TOP_K, t), lambda i: (0, i))
    return pl.pallas_call(
        functools.partial(_mix_out_kernel, n_parts),
        grid=(N_TOK // t,),
        in_specs=in_specs,
        out_specs=[pl.BlockSpec((t, d), lambda i: (i, 0)), pl.BlockSpec((t, d), lambda i: (i, 0)),
                   tok_major, tok_major, tok_major, pl.BlockSpec((ne, 1), lambda i: (0, 0))],
        out_shape=[jax.ShapeDtypeStruct((N_TOK, d), F32), jax.ShapeDtypeStruct((N_TOK, d), F32),
                   jax.ShapeDtypeStruct((TOP_K, N_TOK), jnp.int32), jax.ShapeDtypeStruct((TOP_K, N_TOK), F32),
                   jax.ShapeDtypeStruct((TOP_K, N_TOK), jnp.int32), jax.ShapeDtypeStruct((ne, 1), F32)],
        scratch_shapes=[pltpu.VMEM((ne, 1), F32)],
        compiler_params=_cparams("arbitrary"),
        name="mix_out_router",
    )(*[o for part in parts for o in part[:2]], *[w.astype(BF16) for _, _, w in parts], x, mod_l,
      norm_g.reshape(1, d), wr_hi, wr_lo, b_router.reshape(ne, 1))


def _sc_row_gather(src, idx):
    n = idx.shape[0]
    width = src.shape[1]
    sc = plsc.get_sparse_core_info()
    n_workers = sc.num_cores * sc.num_subcores
    per_worker = n // n_workers
    if n % n_workers or per_worker % SC_ROWS:
        raise ValueError("row count must be whole SparseCore steps on every subcore")
    mesh = plsc.VectorSubcoreMesh(core_axis_name="core", subcore_axis_name="subcore")

    @functools.partial(pl.kernel, out_type=jax.ShapeDtypeStruct((n, width), src.dtype), mesh=mesh,
                       scratch_types=[pltpu.VMEM((per_worker,), jnp.int32),
                                      pltpu.VMEM((SC_ROWS, width), src.dtype)],
                       name="sc_row_gather")
    def gather(src_hbm, idx_hbm, dst_hbm, idx_vmem, rows_vmem):
        worker = lax.axis_index("subcore") * sc.num_cores + lax.axis_index("core")
        base = worker * per_worker
        pltpu.sync_copy(idx_hbm.at[pl.ds(base, per_worker)], idx_vmem)

        @pl.loop(0, per_worker // SC_ROWS)
        def _(c):
            pltpu.sync_copy(src_hbm.at[idx_vmem.at[pl.ds(c * SC_ROWS, SC_ROWS)]], rows_vmem)
            pltpu.sync_copy(rows_vmem, dst_hbm.at[pl.ds(base + c * SC_ROWS, SC_ROWS)])

    return gather(src, idx)


def _experts_kernel(te_ref, first_ref, nv_ref, x_ref, wgu_ref, bg_ref, bu_ref, wd_ref, bd_ref,
                    sel_even_ref, sel_odd_ref, y_ref, wg_ref, wu_ref, wdb_ref):
    del te_ref
    i = pl.program_id(0)
    valid = i < nv_ref[0]
    half = DEINT_COLS // 2

    @pl.when(jnp.logical_and(valid, first_ref[i] == 1))
    def _():
        for c in range(wgu_ref.shape[1] // DEINT_COLS):
            w = wgu_ref[:, c * DEINT_COLS:(c + 1) * DEINT_COLS].astype(BF16)
            wg_ref[:, c * half:(c + 1) * half] = _dot(w, sel_even_ref[...]).astype(BF16)
            wu_ref[:, c * half:(c + 1) * half] = _dot(w, sel_odd_ref[...]).astype(BF16)
        wdb_ref[...] = wd_ref[...].astype(BF16)

    @pl.when(valid)
    def _():
        x = x_ref[...].astype(BF16)
        gl = _dot(x, wg_ref[...]) + bg_ref[...]
        up = _dot(x, wu_ref[...]) + bu_ref[...]
        gl = jnp.minimum(gl, SWIGLU_LIMIT)
        up = jnp.clip(up, -SWIGLU_LIMIT, SWIGLU_LIMIT)
        act = (up + 1.0) * gl * _sigmoid(SWIGLU_ALPHA * gl)
        y_ref[...] = _dot(act.astype(BF16), wdb_ref[...]) + bd_ref[...]

    @pl.when(jnp.logical_not(valid))
    def _():
        y_ref[...] = jnp.zeros_like(y_ref)


def _experts(xs, tile_expert, tile_first, n_valid, layer, w_gu, b_gate, b_up, w_down, b_down):
    d = D_MODEL
    tm = MOE_TILE
    n_tiles = MOE_ROWS // tm
    ff = w_down.shape[2]
    half = DEINT_COLS // 2
    r = jnp.arange(DEINT_COLS)[:, None]
    c = jnp.arange(half)[None, :]
    sel_even = (r == 2 * c).astype(BF16)
    sel_odd = (r == 2 * c + 1).astype(BF16)
    wspec = lambda k, n: pl.BlockSpec((None, None, k, n), lambda i, te, fi, nv: (layer, te[i], 0, 0))
    const = lambda shape: pl.BlockSpec(shape, lambda i, te, fi, nv: (0, 0))
    grid_spec = pltpu.PrefetchScalarGridSpec(
        num_scalar_prefetch=3,
        grid=(n_tiles,),
        in_specs=[pl.BlockSpec((tm, d), lambda i, te, fi, nv: (i, 0)),
                  wspec(d, 2 * ff), wspec(1, ff), wspec(1, ff), wspec(ff, d), wspec(1, d),
                  const((DEINT_COLS, half)), const((DEINT_COLS, half))],
        out_specs=pl.BlockSpec((tm, d), lambda i, te, fi, nv: (i, 0)),
        scratch_shapes=[pltpu.VMEM((d, ff), BF16), pltpu.VMEM((d, ff), BF16), pltpu.VMEM((ff, d), BF16)],
    )
    return pl.pallas_call(
        _experts_kernel,
        grid_spec=grid_spec,
        out_shape=jax.ShapeDtypeStruct((MOE_ROWS, d), F32),
        compiler_params=_cparams("arbitrary"),
        name="experts",
    )(tile_expert, tile_first, n_valid, xs, w_gu, b_gate, b_up, w_down, b_down, sel_even, sel_odd)


def _combine_kernel(final, x_ref, g_ref, w_ref, mod_ref, fg_ref, *o_refs):
    y = w_ref[:, 0:1] * g_ref[0]
    for k in range(1, TOP_K):
        y = y + w_ref[:, k:k + 1] * g_ref[k]
    x = x_ref[...] + mod_ref[5:6, :] * y
    if not final:
        o_refs[0][...] = x
        return
    x = x * lax.rsqrt(jnp.mean(x * x, axis=-1, keepdims=True) + RMS_EPS) * fg_ref[...]
    from_prompt = pl.program_id(0) < N_PROMPT // TOK_TILE

    @pl.when(from_prompt)
    def _():
        o_refs[0][...] = x

    @pl.when(jnp.logical_not(from_prompt))
    def _():
        o_refs[1][...] = x


def _combine(x1, gathered, wts, mod_l, final_g, final):
    d = D_MODEL
    t = TOK_TILE
    n_prompt_tiles = N_PROMPT // t
    if final:
        out_specs = [pl.BlockSpec((t, d), lambda i: (jnp.minimum(i, n_prompt_tiles - 1), 0)),
                     pl.BlockSpec((t, d), lambda i: (jnp.maximum(i - n_prompt_tiles, 0), 0))]
        out_shape = [jax.ShapeDtypeStruct((N_PROMPT, d), F32), jax.ShapeDtypeStruct((N_SAMPLE, d), F32)]
    else:
        out_specs = pl.BlockSpec((t, d), lambda i: (i, 0))
        out_shape = jax.ShapeDtypeStruct((N_TOK, d), F32)
    return pl.pallas_call(
        functools.partial(_combine_kernel, final),
        grid=(N_TOK // t,),
        in_specs=[pl.BlockSpec((t, d), lambda i: (i, 0)),
                  pl.BlockSpec((TOP_K, t, d), lambda i: (0, i, 0)),
                  pl.BlockSpec((t, TOP_K), lambda i: (i, 0)),
                  pl.BlockSpec((None, N_MOD, d), lambda i: (_tile_cond_row(i), 0, 0)),
                  pl.BlockSpec((1, d), lambda i: (0, 0))],
        out_specs=out_specs,
        out_shape=out_shape,
        compiler_params=_cparams("arbitrary"),
        name="moe_combine",
    )(x1, gathered, wts, mod_l, final_g.reshape(1, d))


def _moe(x1, h, eid, wts, rank, counts, mod_l, layer, w_gu, b_gu, w_down, b_down, final_g, final):
    d = D_MODEL
    tm = MOE_TILE
    n_tiles = MOE_ROWS // tm
    cnt = counts.reshape(N_EXPERTS).astype(jnp.int32)
    gsz = ((cnt + tm - 1) // tm) * tm
    ends = jnp.cumsum(gsz)
    offs = ends - gsz
    e_ids = jnp.arange(N_EXPERTS, dtype=jnp.int32)
    pos = jnp.sum(jnp.where(eid[..., None] == e_ids, offs, 0), axis=-1) + rank
    tile_start = jnp.arange(n_tiles, dtype=jnp.int32) * tm
    tile_expert = jnp.minimum(jnp.sum((ends[None, :] <= tile_start[:, None]).astype(jnp.int32), axis=1),
                              N_EXPERTS - 1)
    n_valid = (ends[-1:] // tm).astype(jnp.int32)
    last_valid = jnp.maximum(n_valid[0] - 1, 0)
    tile_expert = jnp.where(jnp.arange(n_tiles) < n_valid[0], tile_expert, tile_expert[last_valid])
    tile_first = jnp.concatenate([jnp.ones((1,), jnp.int32),
                                  (tile_expert[1:] != tile_expert[:-1]).astype(jnp.int32)])
    j = jnp.arange(tm, dtype=jnp.int32)[None, :]
    pad_used = j < (gsz - cnt)[:, None]
    n_unused_before = jnp.cumsum((~pad_used).reshape(-1).astype(jnp.int32)) - 1
    pad_pos = jnp.where(pad_used, (offs + cnt)[:, None] + j,
                        ends[-1] + n_unused_before.reshape(N_EXPERTS, tm))
    slot_of_row = jnp.argsort(jnp.concatenate([pos.reshape(-1), pad_pos.reshape(-1)]))
    token_of_row = (slot_of_row % N_TOK).astype(jnp.int32)

    xs = _sc_row_gather(h, token_of_row)
    ys = _experts(xs, tile_expert, tile_first, n_valid, layer, w_gu,
                  b_gu[:, :, None, 0::2], b_gu[:, :, None, 1::2], w_down, b_down[:, :, None, :])
    gathered = _sc_row_gather(ys, pos.reshape(-1))
    return _combine(x1, gathered.reshape(TOP_K, N_TOK, d), wts.T, mod_l, final_g, final)


def _grid_positions(n_tok, d):
    rows = n_tok // GRID_W
    r, col = jnp.meshgrid(jnp.arange(rows, dtype=F32), jnp.arange(GRID_W, dtype=F32), indexing='ij')
    r = r.reshape(-1)
    col = col.reshape(-1)
    quarter = d // 4
    inv = 1.0 / (10000.0 ** (jnp.arange(quarter, dtype=F32) / quarter))
    ar = r[:, None] * inv[None]
    ac = col[:, None] * inv[None]
    return jnp.concatenate([jnp.sin(ar), jnp.cos(ar), jnp.sin(ac), jnp.cos(ac)], axis=-1)


def kernel(x_prompt, x_sample, state_hgrn, state_mlstm_c, state_mlstm_n, state_mlstm_m, c, c_ctx,
           norm_g, final_g, w_mod, b_mod, ev_w_in, ev_gate_b, ev_conv, hg_lb, ev_w_out,
           hy_w_in, hy_conv, hy_w1, hy_b1, hy_w2, hy_b2, hy_w3, hy_b3, hy_freq, hy_log_rate, hy_bias, hy_w_out,
           w_router, b_router, w_gu, b_gu, w_down, b_down):
    d = D_MODEL
    hd = HEAD_DIM
    cond = jnp.concatenate([c_ctx[None], c, jnp.zeros((N_COND - 1 - DEC_BATCH, d), F32)], axis=0)
    mod = _modulation(cond, w_mod, b_mod)
    pos_tab = jnp.concatenate([jnp.zeros((TOK_TILE, d), F32), _grid_positions(DEC_SEQ, d)], axis=0)

    groups = ((BATCH, SEQ, 0), (DEC_BATCH, DEC_SEQ, N_PROMPT))
    new_states = None
    for l in range(DEPTH):
        if l % 2 == 0:
            e = l // 2
            if l == 0:
                x, p, gate, gate_t = _proj_even(x_prompt.reshape(N_PROMPT, d), x_sample.reshape(N_SAMPLE, d),
                                                pos_tab, mod[l], norm_g[l, 0], ev_w_in[e], ev_gate_b[e])
            else:
                raise NotImplementedError("only the first layer adds grid positions")
            gate_h = gate.reshape(N_TOK, 4, HEADS).transpose(2, 0, 1)
            gate_t_h = gate_t.reshape(4, HEADS, N_TOK).transpose(1, 0, 2)
            o_hg, o_ml = [], []
            for gi, (n_seq, seq_len, off) in enumerate(groups):
                if gi == 0:
                    s0 = jnp.zeros((n_seq, 2, HEADS, hd, hd), F32)
                    c0 = jnp.zeros((n_seq, 2, HEADS, hd, hd), F32)
                    n0 = jnp.zeros((n_seq, 2, HEADS, 1, hd), F32)
                    m0 = jnp.zeros((n_seq, 2, HEADS, 1, 1), F32)
                else:
                    s0 = state_hgrn[:, e]
                    c0 = state_mlstm_c[:, e]
                    n0 = state_mlstm_n[:, e].reshape(n_seq, 2, HEADS, 1, hd)
                    m0 = state_mlstm_m[:, e].reshape(n_seq, 2, HEADS, 1, 1)
                og, s_fin = _hgrn(p, hg_lb, l, s0, n_seq, seq_len, off)
                om, c_fin, n_fin, m_fin = _mlstm(p, gate_h, gate_t_h, ev_conv[e], c0, n0, m0, n_seq, seq_len, off)
                o_hg.append(og)
                o_ml.append(om)
                if gi == 0:
                    new_states = (s_fin[:, None], c_fin[:, None],
                                  n_fin.reshape(n_seq, 1, 2, HEADS, hd), m_fin.reshape(n_seq, 1, 2, HEADS))
            parts = [(*o_hg, ev_w_out[e][:GROUP_W]), (*o_ml, ev_w_out[e][GROUP_W:])]
        else:
            o = l // 2
            u = _proj_odd(x, mod[l], norm_g[l, 0], hy_w_in[o])
            zs = []
            for n_seq, seq_len, off in groups:
                f_tab = _dft_tables(seq_len)
                kf = _hyena_filters(seq_len, hy_w1[o], hy_b1[o], hy_w2[o], hy_b2[o], hy_w3[o], hy_b3[o],
                                    hy_freq[o], hy_log_rate[o], f_tab)
                zs.append(_hyena(u, hy_conv[o], hy_bias[o], kf, f_tab, n_seq, seq_len, off))
            parts = [(*zs, hy_w_out[o])]
        x1, h, eid, wts, rank, counts = _mix_out(parts, x, mod[l], norm_g[l, 1], w_router[l], b_router[l])
        x = _moe(x1, h, eid, wts, rank, counts, mod[l], l, w_gu, b_gu, w_down, b_down,
                 final_g, final=(l == DEPTH - 1))

    y_prompt, y_sample = x
    return (y_prompt.reshape(BATCH, SEQ, d), y_sample.reshape(DEC_BATCH, DEC_SEQ, d)) + new_states
```

```python
import functools

import numpy as np
import jax
import jax.numpy as jnp
from jax import lax
from jax.experimental import pallas as pl
from jax.experimental.pallas import tpu as pltpu
from jax.experimental.pallas import tpu_sc as plsc

F32 = jnp.float32
BF16 = jnp.bfloat16

D_MODEL = 1024
BATCH = 32
SEQ = 256
DEPTH = 2
DEC_BATCH = 8
DEC_SEQ = 1024
GRID_W = 64
RMS_EPS = 1e-6
N_MOD = 6
LOG2_E = 1.4426950408889634

HEADS = 4
HEAD_DIM = 128
GROUP_W = HEADS * HEAD_DIM
N_GATES = 4 * HEADS
EVEN_MAIN = 9 * GROUP_W

HY_ORDER = 2
HY_BANDS = 16
HY_EMB = 1 + 2 * HY_BANDS
HY_HIDDEN = 64
HY_FREQ_CHUNK = 256
HY_STEP_ROWS = 1024

N_EXPERTS = 32
TOP_K = 4
SWIGLU_LIMIT = 7.0
SWIGLU_ALPHA = 1.702

N_PROMPT = BATCH * SEQ
N_SAMPLE = DEC_BATCH * DEC_SEQ
N_TOK = N_PROMPT + N_SAMPLE
N_COND = 16

TOK_TILE = 256
SCAN_CHUNK = 128
SUB = 16
MOE_TILE = 256
MOE_ROWS = N_TOK * TOP_K + N_EXPERTS * MOE_TILE
DEINT_COLS = 256
SC_STEP_BYTES = 256 * 1024
SC_MAX_INDICES = 128

VMEM_LIMIT = 56 * 1024 * 1024


def _cparams(*sem):
    return pltpu.CompilerParams(dimension_semantics=sem, vmem_limit_bytes=VMEM_LIMIT)


def _split3(x):
    hi = x.astype(BF16)
    r = x - hi.astype(F32)
    mid = r.astype(BF16)
    lo = (r - mid.astype(F32)).astype(BF16)
    return hi, mid, lo


def _dot(a, b):
    return jnp.dot(a, b, preferred_element_type=F32)


def _dot_nt(a, b):
    return lax.dot_general(a, b, (((1,), (1,)), ((), ())), preferred_element_type=F32)


def _dot_tn(a, b):
    return lax.dot_general(a, b, (((0,), (0,)), ((), ())), preferred_element_type=F32)


def _dot_w3(a_exact_bf16, x):
    hi, mid, lo = _split3(x)
    return _dot(a_exact_bf16, hi) + _dot(a_exact_bf16, mid) + _dot(a_exact_bf16, lo)


def _sigmoid(x):
    return 1.0 / (1.0 + jnp.exp(-x))


def _silu(x):
    return x * _sigmoid(x)


def _log_sigmoid(x):
    return jnp.minimum(x, 0.0) - jnp.log(1.0 + jnp.exp(-jnp.abs(x)))


def _tile_cond_row(i):
    n_prompt_tiles = N_PROMPT // TOK_TILE
    tiles_per_seq = DEC_SEQ // TOK_TILE
    return jnp.where(i < n_prompt_tiles, 0, 1 + (i - n_prompt_tiles) // tiles_per_seq)


def _mod_kernel(cond_ref, w_ref, b_ref, o_ref):
    a = _silu(cond_ref[...]).astype(BF16)
    o_ref[...] = _dot(a, w_ref[...].astype(BF16)) + b_ref[...]


def _modulation(cond, w_mod, b_mod):
    d = D_MODEL
    out = pl.pallas_call(
        _mod_kernel,
        grid=(DEPTH, N_MOD),
        in_specs=[
            pl.BlockSpec((N_COND, d), lambda l, j: (0, 0)),
            pl.BlockSpec((None, d, d), lambda l, j: (l, 0, j)),
            pl.BlockSpec((None, 1, d), lambda l, j: (l, 0, j)),
        ],
        out_specs=pl.BlockSpec((None, None, N_COND, d), lambda l, j: (l, j, 0, 0)),
        out_shape=jax.ShapeDtypeStruct((DEPTH, N_MOD, N_COND, d), F32),
        compiler_params=_cparams("parallel", "parallel"),
        name="modulation",
    )(cond, w_mod, b_mod.reshape(DEPTH, 1, N_MOD * d))
    return out.transpose(0, 2, 1, 3)


def _norm_mod(x, g_row, scale_row, shift_row):
    ms = jnp.mean(x * x, axis=-1, keepdims=True)
    y = x * lax.rsqrt(ms + RMS_EPS) * g_row
    return y * (1.0 + scale_row) + shift_row


def _proj_even_kernel(xp_ref, xs_ref, pos_ref, mod_ref, g_ref, w_ref, wg_ref, wgt_ref, gb_ref, gbt_ref,
                      xres_ref, p_ref, gate_ref, gate_t_ref):
    from_prompt = pl.program_id(0) < N_PROMPT // TOK_TILE
    x = jnp.where(from_prompt, xp_ref[...], xs_ref[...]) + pos_ref[...]
    xres_ref[...] = x
    h = _norm_mod(x, g_ref[...], mod_ref[1:2, :], mod_ref[0:1, :]).astype(BF16)
    p_ref[...] = _dot(h, w_ref[...])
    gate_ref[...] = _dot(h, wg_ref[...]) + gb_ref[...]
    gate_t_ref[...] = _dot_nt(wgt_ref[...], h) + gbt_ref[...]


def _proj_even(x_prompt, x_sample, pos_tab, mod_l, norm_g, w_in, gate_b):
    d = D_MODEL
    n_tiles = N_TOK // TOK_TILE
    n_prompt_tiles = N_PROMPT // TOK_TILE
    tiles_per_seq = DEC_SEQ // TOK_TILE
    w_main = w_in[:, :EVEN_MAIN].astype(BF16)
    w_gate = w_in[:, EVEN_MAIN:].astype(BF16)

    def pos_map(i):
        return (jnp.where(i < n_prompt_tiles, 0, 1 + (i - n_prompt_tiles) % tiles_per_seq), 0)

    return pl.pallas_call(
        _proj_even_kernel,
        grid=(n_tiles,),
        in_specs=[
            pl.BlockSpec((TOK_TILE, d), lambda i: (jnp.minimum(i, n_prompt_tiles - 1), 0)),
            pl.BlockSpec((TOK_TILE, d), lambda i: (jnp.maximum(i - n_prompt_tiles, 0), 0)),
            pl.BlockSpec((TOK_TILE, d), pos_map),
            pl.BlockSpec((None, N_MOD, d), lambda i: (_tile_cond_row(i), 0, 0)),
            pl.BlockSpec((1, d), lambda i: (0, 0)),
            pl.BlockSpec((d, EVEN_MAIN), lambda i: (0, 0)),
            pl.BlockSpec((d, N_GATES), lambda i: (0, 0)),
            pl.BlockSpec((N_GATES, d), lambda i: (0, 0)),
            pl.BlockSpec((1, N_GATES), lambda i: (0, 0)),
            pl.BlockSpec((N_GATES, 1), lambda i: (0, 0)),
        ],
        out_specs=[
            pl.BlockSpec((TOK_TILE, d), lambda i: (i, 0)),
            pl.BlockSpec((TOK_TILE, EVEN_MAIN), lambda i: (i, 0)),
            pl.BlockSpec((TOK_TILE, N_GATES), lambda i: (i, 0)),
            pl.BlockSpec((N_GATES, TOK_TILE), lambda i: (0, i)),
        ],
        out_shape=[
            jax.ShapeDtypeStruct((N_TOK, d), F32),
            jax.ShapeDtypeStruct((N_TOK, EVEN_MAIN), F32),
            jax.ShapeDtypeStruct((N_TOK, N_GATES), F32),
            jax.ShapeDtypeStruct((N_GATES, N_TOK), F32),
        ],
        compiler_params=_cparams("parallel"),
        name="proj_even",
    )(x_prompt, x_sample, pos_tab, mod_l, norm_g.reshape(1, d), w_main, w_gate, w_gate.T,
      gate_b.reshape(1, N_GATES), gate_b.reshape(N_GATES, 1))


def _proj_odd_kernel(x_ref, mod_ref, g_ref, w_ref, p_ref):
    h = _norm_mod(x_ref[...], g_ref[...], mod_ref[1:2, :], mod_ref[0:1, :]).astype(BF16)
    p_ref[...] = _dot(h, w_ref[...])


def _proj_odd(x, mod_l, norm_g, w_in):
    d = D_MODEL
    width = w_in.shape[1]
    return pl.pallas_call(
        _proj_odd_kernel,
        grid=(N_TOK // TOK_TILE,),
        in_specs=[
            pl.BlockSpec((TOK_TILE, d), lambda i: (i, 0)),
            pl.BlockSpec((None, N_MOD, d), lambda i: (_tile_cond_row(i), 0, 0)),
            pl.BlockSpec((1, d), lambda i: (0, 0)),
            pl.BlockSpec((d, width), lambda i: (0, 0)),
        ],
        out_specs=pl.BlockSpec((TOK_TILE, width), lambda i: (i, 0)),
        out_shape=jax.ShapeDtypeStruct((N_TOK, width), F32),
        compiler_params=_cparams("parallel"),
        name="proj_odd",
    )(x, mod_l, norm_g.reshape(1, d), w_in.astype(BF16))


def _hgrn_chunk(reverse, q, k, v, lf, st):
    c = SCAN_CHUNK
    row = lax.broadcasted_iota(jnp.int32, (c, c), 0)
    col = lax.broadcasted_iota(jnp.int32, (c, c), 1)
    tri = (col >= row) if reverse else (col <= row)
    b = _dot_w3(jnp.where(tri, 1.0, 0.0).astype(BF16), lf)

    b2 = b * LOG2_E
    lane = lax.broadcasted_iota(jnp.int32, (SUB, c), 1)
    chunk_row = lax.broadcasted_iota(jnp.int32, (c, 1), 0)
    rows = []
    for i in range(c // SUB):
        lo, hi = i * SUB, (i + 1) * SUB
        qi, ki, bi, bi2 = q[lo:hi], k[lo:hi], b[lo:hi], b2[lo:hi]
        if reverse:
            has_off, edge = hi < c, hi
            outside = chunk_row >= hi
        else:
            has_off, edge = lo > 0, lo - 1
            outside = chunk_row < lo
        if has_off:
            beta = b[edge:edge + 1]
            qs = qi * jnp.exp(bi - beta)
            ks = k * jnp.exp(jnp.where(outside, beta - b, -jnp.inf))
            a_row = _dot_nt(qs.astype(BF16), ks.astype(BF16))
        else:
            a_row = jnp.zeros((SUB, c), F32)
        for s in range(SUB):
            a_col = jnp.sum(jnp.exp2(bi2 - bi2[s:s + 1]) * qi * ki[s:s + 1], axis=-1, keepdims=True)
            a_row = jnp.where(lane == lo + s, a_col, a_row)
        rows.append(a_row)
    attn = jnp.where(tri, jnp.concatenate(rows, axis=0), 0.0)
    o = _dot(attn.astype(BF16), v.astype(BF16)) + _dot_nt((q * jnp.exp(b)).astype(BF16), st.astype(BF16))
    b_exit = b[0:1] if reverse else b[c - 1:c]
    k_out = k * jnp.exp(b_exit - b)
    st_new = jnp.exp(b_exit) * st + _dot_tn(v.astype(BF16), k_out.astype(BF16))
    return o, st_new


def _hgrn_kernel(seq_len, layer, q_ref, i_ref, g_ref, ff_ref, fb_ref, lb_ref, s0_ref,
                 o_ref, s_out_ref, of_ref, ob_ref, st_ref):
    c = SCAN_CHUNK
    n_chunks = seq_len // c
    lbp = lb_ref[...]
    e = jnp.exp(lbp - jnp.max(lbp, axis=0, keepdims=True))
    lb = jnp.sum(e[0:layer + 1], axis=0, keepdims=True) / jnp.sum(e, axis=0, keepdims=True)

    st_ref[0] = s0_ref[0].T
    st_ref[1] = s0_ref[1].T

    def body(n, carry):
        results = []
        for reverse in (False, True):
            d = 1 if reverse else 0
            base = pl.multiple_of((n_chunks - 1 - n if reverse else n) * c, c)
            sl = pl.ds(base, c)
            f = lb + (1.0 - lb) * _sigmoid((fb_ref if reverse else ff_ref)[sl, :])
            results.append((sl, _hgrn_chunk(reverse, q_ref[sl, :], 1.0 - f, i_ref[sl, :], jnp.log(f),
                                            st_ref[d])))
        for d, (sl, (o, st_new)) in enumerate(results):
            st_ref[d] = st_new
            (ob_ref if d else of_ref)[sl, :] = o
        return carry

    lax.fori_loop(0, n_chunks, body, 0)
    o = of_ref[...] + ob_ref[...]
    o = o * lax.rsqrt(jnp.mean(o * o, axis=-1, keepdims=True) + RMS_EPS)
    o_ref[...] = o * _silu(g_ref[...])
    s_out_ref[0] = st_ref[0].T
    s_out_ref[1] = st_ref[1].T


def _hgrn(p, hg_lb, layer, s0, n_seq, seq_len, tok_offset):
    hd = HEAD_DIM
    row0 = tok_offset // seq_len

    def col(part):
        return pl.BlockSpec((seq_len, hd), lambda b, h: (row0 + b, part * HEADS + h))

    state_spec = pl.BlockSpec((None, 2, None, hd, hd), lambda b, h: (b, 0, h, 0, 0))
    return pl.pallas_call(
        functools.partial(_hgrn_kernel, seq_len, layer),
        grid=(n_seq, HEADS),
        in_specs=[col(0), col(1), col(2), col(3), col(4),
                  pl.BlockSpec((DEPTH + 1, hd), lambda b, h: (0, h)),
                  state_spec],
        out_specs=[pl.BlockSpec((seq_len, hd), lambda b, h: (b, h)), state_spec],
        out_shape=[jax.ShapeDtypeStruct((n_seq * seq_len, GROUP_W), F32),
                   jax.ShapeDtypeStruct((n_seq, 2, HEADS, hd, hd), F32)],
        scratch_shapes=[pltpu.VMEM((seq_len, hd), F32), pltpu.VMEM((seq_len, hd), F32),
                        pltpu.VMEM((2, hd, hd), F32)],
        compiler_params=_cparams("parallel", "parallel"),
        name=f"hgrn_l{seq_len}",
    )(p, p, p, p, p, hg_lb, s0)


def _short_conv3(x, w):
    n = x.shape[0]
    r = lax.broadcasted_iota(jnp.int32, (n, 1), 0)
    prev = jnp.where(r == 0, 0.0, pltpu.roll(x, 1, 0))
    nxt = jnp.where(r == n - 1, 0.0, pltpu.roll(x, n - 1, 0))
    return prev * w[0:1] + x * w[1:2] + nxt * w[2:3]


def _mlstm_chunk(reverse, q, k, v, ig_c, fg_c, ig_r, fg_r, cm, nv, m_prev):
    c = SCAN_CHUNK
    row = lax.broadcasted_iota(jnp.int32, (c, c), 0)
    col = lax.broadcasted_iota(jnp.int32, (c, c), 1)
    tri = (col >= row) if reverse else (col <= row)
    lf_c = _log_sigmoid(fg_c)
    lf_r = _log_sigmoid(fg_r)
    b_c = jnp.sum(jnp.where(tri, lf_r, 0.0), axis=-1, keepdims=True)
    tri_t = (row >= col) if reverse else (row <= col)
    b_r = jnp.sum(jnp.where(tri_t, lf_c, 0.0), axis=0, keepdims=True)
    dmat = jnp.where(tri, b_c - b_r + ig_r, -jnp.inf)
    m_t = jnp.maximum(b_c + m_prev, jnp.max(dmat, axis=-1, keepdims=True))
    qb, kb, vb = q.astype(BF16), k.astype(BF16), v.astype(BF16)
    p = jnp.exp(dmat - m_t) * _dot_nt(qb, kb)
    inter = jnp.exp(b_c + m_prev - m_t)
    num = inter * _dot(qb, cm.astype(BF16)) + _dot(p.astype(BF16), vb)
    den = inter * jnp.sum(q * nv, axis=-1, keepdims=True) + jnp.sum(p, axis=-1, keepdims=True)
    h = num / jnp.maximum(jnp.abs(den), jnp.exp(-m_t))
    last = 0 if reverse else c - 1
    m_new = m_t[last:last + 1]
    b_exit = b_c[last:last + 1]
    w_c = jnp.exp(b_exit - b_c + ig_c - m_new)
    dec = jnp.exp(b_exit + m_prev - m_new)
    kw = k * w_c
    cm_new = dec * cm + _dot_tn(kw.astype(BF16), vb)
    nv_new = dec * nv + jnp.sum(kw, axis=0, keepdims=True)
    return h, cm_new, nv_new, m_new


def _mlstm_kernel(seq_len, q_ref, k_ref, v_ref, og_ref, gate_ref, gate_t_ref, cwq_ref, cwk_ref,
                  c0_ref, n0_ref, m0_ref,
                  o_ref, c_out_ref, n_out_ref, m_out_ref,
                  q2_ref, k2_ref, hf_ref, hb_ref, c_ref, n_ref, m_ref):
    c = SCAN_CHUNK
    hd = HEAD_DIM
    n_chunks = seq_len // c
    q2_ref[...] = _silu(_short_conv3(q_ref[...], cwq_ref[...]))
    k2_ref[...] = _silu(_short_conv3(k_ref[...], cwk_ref[...])) * (HEAD_DIM ** -0.5)
    c_ref[...] = c0_ref[...]
    n_ref[...] = n0_ref[...]
    m_ref[...] = m0_ref[...]

    def body(n, carry):
        results = []
        for reverse in (False, True):
            d = 1 if reverse else 0
            base = pl.multiple_of((n_chunks - 1 - n if reverse else n) * c, c)
            sl = pl.ds(base, c)
            for hh in range(HEADS):
                cols = slice(hh * hd, (hh + 1) * hd)
                gc = gate_ref[hh, sl, :]
                gr = gate_t_ref[hh, :, sl]
                results.append((d, hh, sl, cols, _mlstm_chunk(
                    reverse, q2_ref[sl, cols], k2_ref[sl, cols], v_ref[sl, cols],
                    gc[:, 2 * d:2 * d + 1], gc[:, 2 * d + 1:2 * d + 2],
                    gr[2 * d:2 * d + 1, :], gr[2 * d + 1:2 * d + 2, :],
                    c_ref[d, hh], n_ref[d, hh], m_ref[d, hh])))
        for d, hh, sl, cols, (h, cm, nv, m_new) in results:
            c_ref[d, hh] = cm
            n_ref[d, hh] = nv
            m_ref[d, hh] = m_new
            (hb_ref if d else hf_ref)[sl, cols] = h
        return carry

    lax.fori_loop(0, n_chunks, body, 0)
    for hh in range(HEADS):
        cols = slice(hh * hd, (hh + 1) * hd)
        h = hf_ref[:, cols] + hb_ref[:, cols]
        h = h * lax.rsqrt(jnp.mean(h * h, axis=-1, keepdims=True) + RMS_EPS)
        o_ref[:, cols] = h * _sigmoid(og_ref[:, cols])
    c_out_ref[...] = c_ref[...]
    n_out_ref[...] = n_ref[...]
    m_out_ref[...] = m_ref[...]


def _mlstm(p, gate_h, gate_t_h, conv_w, c0, n0, m0, n_seq, seq_len, tok_offset):
    hd = HEAD_DIM
    gw = GROUP_W
    row0 = tok_offset // seq_len

    def col(part):
        return pl.BlockSpec((seq_len, gw), lambda b: (row0 + b, part))

    c_spec = pl.BlockSpec((None, 2, HEADS, hd, hd), lambda b: (b, 0, 0, 0, 0))
    n_spec = pl.BlockSpec((None, 2, HEADS, 1, hd), lambda b: (b, 0, 0, 0, 0))
    m_spec = pl.BlockSpec((None, 2, HEADS, 1, 1), lambda b: (b, 0, 0, 0, 0))
    return pl.pallas_call(
        functools.partial(_mlstm_kernel, seq_len),
        grid=(n_seq,),
        in_specs=[col(5), col(6), col(7), col(8),
                  pl.BlockSpec((HEADS, seq_len, 4), lambda b: (0, row0 + b, 0)),
                  pl.BlockSpec((HEADS, 4, seq_len), lambda b: (0, 0, row0 + b)),
                  pl.BlockSpec((3, gw), lambda b: (0, 0)),
                  pl.BlockSpec((3, gw), lambda b: (0, 1)),
                  c_spec, n_spec, m_spec],
        out_specs=[pl.BlockSpec((seq_len, gw), lambda b: (b, 0)), c_spec, n_spec, m_spec],
        out_shape=[jax.ShapeDtypeStruct((n_seq * seq_len, gw), F32),
                   jax.ShapeDtypeStruct((n_seq, 2, HEADS, hd, hd), F32),
                   jax.ShapeDtypeStruct((n_seq, 2, HEADS, 1, hd), F32),
                   jax.ShapeDtypeStruct((n_seq, 2, HEADS, 1, 1), F32)],
        scratch_shapes=[pltpu.VMEM((seq_len, gw), F32), pltpu.VMEM((seq_len, gw), F32),
                        pltpu.VMEM((seq_len, gw), F32), pltpu.VMEM((seq_len, gw), F32),
                        pltpu.VMEM((2, HEADS, hd, hd), F32), pltpu.VMEM((2, HEADS, 1, hd), F32),
                        pltpu.VMEM((2, HEADS, 1, 1), F32)],
        compiler_params=_cparams("parallel"),
        name=f"mlstm_l{seq_len}",
    )(p, p, p, p, gate_h, gate_t_h, conv_w, conv_w, c0, n0, m0)


def _dft_tables(seq_len):
    n = 2 * seq_len
    k = jnp.arange(seq_len, dtype=jnp.int32)[:, None]
    t = jnp.arange(seq_len, dtype=jnp.int32)[None, :]
    ang = ((k * t) % n).astype(F32) * (2.0 * np.pi / n)
    fc = jnp.cos(ang)
    fs = jnp.sin(ang)
    nyq = jnp.where(t % 2 == 0, 1.0, -1.0).astype(F32)
    fs = jnp.where(k == 0, nyq, fs)
    return jnp.concatenate([fc, fs], axis=0)


def _filter_kernel(seq_len, z_ref, w1_ref, b1_ref, w2_ref, b2_ref, w3f_ref, w3b_ref, b3f_ref, b3b_ref,
                   f0_ref, f1_ref, rf_ref, rb_ref, fhi_ref, flo_ref, kf_ref):
    hp = lax.Precision.HIGHEST
    n = 2 * seq_len
    z = z_ref[...]
    a = jnp.sin(f0_ref[...] * (jnp.dot(z, w1_ref[...], precision=hp, preferred_element_type=F32) + b1_ref[...]))
    a = jnp.sin(f1_ref[...] * (jnp.dot(a, w2_ref[...], precision=hp, preferred_element_type=F32) + b2_ref[...]))
    t_norm = z[:, 0:1]
    hf = (jnp.dot(a, w3f_ref[...], precision=hp, preferred_element_type=F32) + b3f_ref[...]) \
        * jnp.exp(-t_norm * jnp.exp(rf_ref[...]))
    hb = (jnp.dot(a, w3b_ref[...], precision=hp, preferred_element_type=F32) + b3b_ref[...]) \
        * jnp.exp(-t_norm * jnp.exp(rb_ref[...]))
    inv = lax.rsqrt(jnp.sum(hf * hf, axis=0, keepdims=True) + jnp.sum(hb * hb, axis=0, keepdims=True))
    hf = hf * inv
    r = lax.broadcasted_iota(jnp.int32, (seq_len, 1), 0)
    hb = jnp.where(r == 0, 0.0, hb * inv)
    sh, sm, sl = _split3(hf + hb)
    dh, dm, dl = _split3(hf - hb)
    fhi = fhi_ref[...]
    flo = flo_ref[...]
    kc = _dot(fhi[:seq_len], sh) + _dot(fhi[:seq_len], sm) + _dot(fhi[:seq_len], sl) \
        + _dot(flo[:seq_len], sh) + _dot(flo[:seq_len], sm)
    ks = _dot(fhi[seq_len:], dh) + _dot(fhi[seq_len:], dm) + _dot(fhi[seq_len:], dl) \
        + _dot(flo[seq_len:], dh) + _dot(flo[seq_len:], dm)
    sign = jnp.where(r % 2 == 0, 1.0, -1.0)
    k_nyq = jnp.sum(sign * (hf + hb), axis=0, keepdims=True)
    ks = jnp.where(r == 0, k_nyq, ks)
    scale = jnp.where(r == 0, 1.0 / n, 2.0 / n)
    kf_ref[0:seq_len, :] = kc * scale
    kf_ref[seq_len:n, :] = ks * scale


def _hyena_filters(seq_len, w1, b1, w2, b2, w3, b3, freq, log_rate, f_tab):
    d = D_MODEL
    cb = 256
    t = jnp.arange(seq_len, dtype=F32)
    t_norm = t / (seq_len - 1)
    bands = jnp.linspace(1e-4, HY_BANDS - 1, HY_BANDS, dtype=F32)
    ang = (2.0 * np.pi / seq_len) * t[:, None] * bands[None, :]
    z = jnp.concatenate([t_norm[:, None], jnp.cos(ang), jnp.sin(ang)], axis=-1)
    kpad = 128 - HY_EMB
    z = jnp.pad(z, ((0, 0), (0, kpad)))
    w1p = jnp.pad(w1, ((0, kpad), (0, 0)))
    f_hi = f_tab.astype(BF16)
    f_lo = (f_tab - f_hi.astype(F32)).astype(BF16)
    n_cb = d // cb
    hh = HY_HIDDEN
    row = lambda a: a.reshape(1, -1)
    const = lambda shape: pl.BlockSpec(shape, lambda o, j: (0,) * len(shape))
    fwd = lambda rows: pl.BlockSpec((rows, cb), lambda o, j: (0, o * n_cb + j))
    bwd = lambda rows: pl.BlockSpec((rows, cb), lambda o, j: (0, (HY_ORDER + o) * n_cb + j))
    return pl.pallas_call(
        functools.partial(_filter_kernel, seq_len),
        grid=(HY_ORDER, n_cb),
        in_specs=[const((seq_len, 128)), const((128, hh)), const((1, hh)), const((hh, hh)), const((1, hh)),
                  fwd(hh), bwd(hh), fwd(1), bwd(1),
                  const((1, hh)), const((1, hh)), fwd(1), bwd(1),
                  const((2 * seq_len, seq_len)), const((2 * seq_len, seq_len))],
        out_specs=pl.BlockSpec((None, 2 * seq_len, cb), lambda o, j: (o, 0, j)),
        out_shape=jax.ShapeDtypeStruct((HY_ORDER, 2 * seq_len, d), F32),
        compiler_params=_cparams("parallel", "parallel"),
        name=f"hyena_filter_l{seq_len}",
    )(z, w1p, row(b1), w2, row(b2), w3, w3, row(b3), row(b3),
      row(freq[0]), row(freq[1]), row(log_rate), row(log_rate), f_hi, f_lo)


def _hyena_kernel(seq_len, seqs, v_ref, x1_ref, x2_ref, cwv_ref, cw1_ref, cw2_ref, bias_ref, kf_ref,
                  f_ref, ft_ref, o_ref, z_ref, zb_ref, y_ref):
    kc = min(HY_FREQ_CHUNK, seq_len)
    n_k = seq_len // kc
    r = lax.broadcasted_iota(jnp.int32, (kc, 1), 0)
    gate_refs = ((x1_ref, cw1_ref), (x2_ref, cw2_ref))
    for s in range(seqs):
        rows = slice(s * seq_len, (s + 1) * seq_len)
        z_ref[s] = _short_conv3(v_ref[rows, :], cwv_ref[...])
    for o in range(HY_ORDER):
        for s in range(seqs):
            zb_ref[s] = z_ref[s].astype(BF16)
            y_ref[s] = jnp.zeros(y_ref.shape[1:], F32)

        def freq_chunk(j, carry):
            r0 = pl.multiple_of(j * kc, kc)
            k_cos = kf_ref[o, pl.ds(r0, kc), :]
            k_sin = kf_ref[o, pl.ds(seq_len + r0, kc), :]
            real_row = jnp.logical_and(r == 0, j == 0)
            for s in range(seqs):
                a = _dot(f_ref[pl.ds(r0, kc), :], zb_ref[s])
                bm = _dot(f_ref[pl.ds(seq_len + r0, kc), :], zb_ref[s])
                yc = a * k_cos - jnp.where(real_row, 0.0, bm * k_sin)
                ys = jnp.where(real_row, bm * k_sin, a * k_sin + bm * k_cos)
                y_ref[s] += _dot(ft_ref[j], yc.astype(BF16)) + _dot(ft_ref[n_k + j], ys.astype(BF16))
            return carry

        lax.fori_loop(0, n_k, freq_chunk, 0)
        x_ref, cw_ref = gate_refs[o]
        for s in range(seqs):
            rows = slice(s * seq_len, (s + 1) * seq_len)
            gate = _short_conv3(x_ref[rows, :], cw_ref[...])
            z_ref[s] = gate * (y_ref[s] + z_ref[s] * bias_ref[o:o + 1, :])
    for s in range(seqs):
        o_ref[s * seq_len:(s + 1) * seq_len, :] = z_ref[s]


def _hyena(u, conv_w, bias, kf, f_tab, n_seq, seq_len, tok_offset):
    d = D_MODEL
    cb = 256
    n_cb = d // cb
    seqs = max(1, HY_STEP_ROWS // seq_len)
    rows = seqs * seq_len
    row0 = tok_offset // rows
    kc = min(HY_FREQ_CHUNK, seq_len)
    n_k = seq_len // kc
    f_bf = f_tab.astype(BF16)
    ft = f_bf.T.reshape(seq_len, 2 * n_k, kc).transpose(1, 0, 2)

    def part(k):
        return pl.BlockSpec((rows, cb), lambda j, b: (row0 + b, k * n_cb + j))

    def cw(k):
        return pl.BlockSpec((3, cb), lambda j, b: (0, k * n_cb + j))

    return pl.pallas_call(
        functools.partial(_hyena_kernel, seq_len, seqs),
        grid=(n_cb, n_seq // seqs),
        in_specs=[part(0), part(1), part(2), cw(0), cw(1), cw(2),
                  pl.BlockSpec((HY_ORDER, cb), lambda j, b: (0, j)),
                  pl.BlockSpec((HY_ORDER, 2 * seq_len, cb), lambda j, b: (0, 0, j)),
                  pl.BlockSpec((2 * seq_len, seq_len), lambda j, b: (0, 0)),
                  pl.BlockSpec((2 * n_k, seq_len, kc), lambda j, b: (0, 0, 0))],
        out_specs=pl.BlockSpec((rows, cb), lambda j, b: (b, j)),
        out_shape=jax.ShapeDtypeStruct((n_seq * seq_len, d), F32),
        scratch_shapes=[pltpu.VMEM((seqs, seq_len, cb), F32), pltpu.VMEM((seqs, seq_len, cb), BF16),
                        pltpu.VMEM((seqs, seq_len, cb), F32)],
        compiler_params=_cparams("parallel", "parallel"),
        name=f"hyena_l{seq_len}",
    )(u, u, u, conv_w, conv_w, conv_w, bias, kf, f_bf, ft)


def _mix_out_kernel(n_parts, *refs):
    o_refs = refs[:2 * n_parts]
    w_refs = refs[2 * n_parts:3 * n_parts]
    x_ref, mod_ref, g_ref, wr_hi_ref, wr_lo_ref, br_ref = refs[3 * n_parts:3 * n_parts + 6]
    x1_ref, h_ref, eid_ref, wts_ref, rank_ref, cnt_ref, run_ref = refs[3 * n_parts + 6:]
    i = pl.program_id(0)
    t = TOK_TILE
    ne = N_EXPERTS

    from_prompt = i < N_PROMPT // TOK_TILE
    y = None
    for j in range(n_parts):
        o = jnp.where(from_prompt, o_refs[2 * j][...], o_refs[2 * j + 1][...])
        yj = _dot(o.astype(BF16), w_refs[j][...])
        y = yj if y is None else y + yj
    x1 = x_ref[...] + mod_ref[2:3, :] * y
    x1_ref[...] = x1
    h = _norm_mod(x1, g_ref[...], mod_ref[4:5, :], mod_ref[3:4, :])
    h_ref[...] = _pack_bf16_halves(h)
    h_hi = h.astype(BF16)
    h_lo = (h - h_hi.astype(F32)).astype(BF16)
    logits = _dot_nt(wr_hi_ref[...], h_hi) + _dot_nt(wr_lo_ref[...], h_hi) + _dot_nt(wr_hi_ref[...], h_lo) \
        + br_ref[...]

    @pl.when(i == 0)
    def _():
        run_ref[...] = jnp.zeros_like(run_ref)

    e_iota = lax.broadcasted_iota(jnp.int32, (ne, t), 0)
    r2 = lax.broadcasted_iota(jnp.int32, (t, t), 0)
    c2 = lax.broadcasted_iota(jnp.int32, (t, t), 1)
    before = jnp.where(r2 < c2, 1.0, 0.0).astype(BF16)
    running = run_ref[...]
    vals, eids, ranks = [], [], []
    for _k in range(TOP_K):
        m = jnp.max(logits, axis=0, keepdims=True)
        eid = jnp.min(jnp.where(logits == m, e_iota, ne), axis=0, keepdims=True)
        sel = e_iota == eid
        logits = jnp.where(sel, -jnp.inf, logits)
        onehot = jnp.where(sel, 1.0, 0.0)
        earlier = _dot(onehot.astype(BF16), before)
        ranks.append(jnp.sum(onehot * (running + earlier), axis=0, keepdims=True))
        running = running + jnp.sum(onehot, axis=1, keepdims=True)
        vals.append(m)
        eids.append(eid)
    run_ref[...] = running
    cnt_ref[...] = running
    v = jnp.concatenate(vals, axis=0)
    ex = jnp.exp(v - v[0:1])
    wts_ref[...] = ex / jnp.sum(ex, axis=0, keepdims=True)
    eid_ref[...] = jnp.concatenate(eids, axis=0)
    rank_ref[...] = jnp.concatenate(ranks, axis=0).astype(jnp.int32)


def _mix_out(parts, x, mod_l, norm_g, w_router, b_router):
    d = D_MODEL
    t = TOK_TILE
    ne = N_EXPERTS
    n_parts = len(parts)
    n_prompt_tiles = N_PROMPT // t
    wr_t = w_router.T
    wr_hi = wr_t.astype(BF16)
    wr_lo = (wr_t - wr_hi.astype(F32)).astype(BF16)
    in_specs = []
    for o_p, _, _ in parts:
        in_specs.append(pl.BlockSpec((t, o_p.shape[1]), lambda i: (jnp.minimum(i, n_prompt_tiles - 1), 0)))
        in_specs.append(pl.BlockSpec((t, o_p.shape[1]), lambda i: (jnp.maximum(i - n_prompt_tiles, 0), 0)))
    in_specs += [pl.BlockSpec(w.shape, lambda i: (0, 0)) for _, _, w in parts]
    in_specs += [
        pl.BlockSpec((t, d), lambda i: (i, 0)),
        pl.BlockSpec((None, N_MOD, d), lambda i: (_tile_cond_row(i), 0, 0)),
        pl.BlockSpec((1, d), lambda i: (0, 0)),
        pl.BlockSpec((ne, d), lambda i: (0, 0)),
        pl.BlockSpec((ne, d), lambda i: (0, 0)),
        pl.BlockSpec((ne, 1), lambda i: (0, 0)),
    ]
    tok_major = pl.BlockSpec((TOP_K, t), lambda i: (0, i))
    return pl.pallas_call(
        functools.partial(_mix_out_kernel, n_parts),
        grid=(N_TOK // t,),
        in_specs=in_specs,
        out_specs=[pl.BlockSpec((t, d), lambda i: (i, 0)), pl.BlockSpec((t, d // 2), lambda i: (i, 0)),
                   tok_major, tok_major, tok_major, pl.BlockSpec((ne, 1), lambda i: (0, 0))],
        out_shape=[jax.ShapeDtypeStruct((N_TOK, d), F32), jax.ShapeDtypeStruct((N_TOK, d // 2), jnp.uint32),
                   jax.ShapeDtypeStruct((TOP_K, N_TOK), jnp.int32), jax.ShapeDtypeStruct((TOP_K, N_TOK), F32),
                   jax.ShapeDtypeStruct((TOP_K, N_TOK), jnp.int32), jax.ShapeDtypeStruct((ne, 1), F32)],
        scratch_shapes=[pltpu.VMEM((ne, 1), F32)],
        compiler_params=_cparams("arbitrary"),
        name="mix_out_router",
    )(*[o for part in parts for o in part[:2]], *[w.astype(BF16) for _, _, w in parts], x, mod_l,
      norm_g.reshape(1, d), wr_hi, wr_lo, b_router.reshape(ne, 1))


def _sc_row_gather(src, idx):
    n = idx.shape[0]
    width = src.shape[1]
    step_rows = min(SC_MAX_INDICES, SC_STEP_BYTES // (width * 4))
    sc = plsc.get_sparse_core_info()
    n_workers = sc.num_cores * sc.num_subcores
    per_worker = n // n_workers
    if n % n_workers or per_worker % step_rows:
        raise ValueError("row count must be whole SparseCore steps on every subcore")
    mesh = plsc.VectorSubcoreMesh(core_axis_name="core", subcore_axis_name="subcore")

    @functools.partial(pl.kernel, out_type=jax.ShapeDtypeStruct((n, width), src.dtype), mesh=mesh,
                       scratch_types=[pltpu.VMEM((per_worker,), jnp.int32),
                                      pltpu.VMEM((step_rows, width), src.dtype)],
                       name="sc_row_gather")
    def gather(src_hbm, idx_hbm, dst_hbm, idx_vmem, rows_vmem):
        worker = lax.axis_index("subcore") * sc.num_cores + lax.axis_index("core")
        base = worker * per_worker
        pltpu.sync_copy(idx_hbm.at[pl.ds(base, per_worker)], idx_vmem)

        @pl.loop(0, per_worker // step_rows)
        def _(c):
            pltpu.sync_copy(src_hbm.at[idx_vmem.at[pl.ds(c * step_rows, step_rows)]], rows_vmem)
            pltpu.sync_copy(rows_vmem, dst_hbm.at[pl.ds(base + c * step_rows, step_rows)])

    return gather(src, idx)


def _pack_bf16_halves(x):
    w = x.shape[1] // 2
    bits = pltpu.bitcast(x.astype(BF16).astype(F32), jnp.uint32)
    return bits[:, :w] | (bits[:, w:] >> 16)


def _unpack_bf16_halves(p):
    hi = pltpu.bitcast(p & jnp.uint32(0xFFFF0000), F32).astype(BF16)
    lo = pltpu.bitcast(p << 16, F32).astype(BF16)
    return hi, lo


def _experts_kernel(layer, te_ref, first_ref, slot_ref, next_ref, nv_ref,
                    x_ref, wgu_hbm, bg_ref, bu_ref, wd_hbm, bd_ref, sel_ref,
                    y_ref, wgu_buf, wd_buf, wg_ref, wu_ref, wdb_ref, sem):
    i = pl.program_id(0)
    valid = i < nv_ref[0]
    half = DEINT_COLS // 2
    k_half = x_ref.shape[1]

    def fetch(expert, slot):
        return (pltpu.make_async_copy(wgu_hbm.at[layer, expert], wgu_buf.at[slot], sem.at[slot, 0]),
                pltpu.make_async_copy(wd_hbm.at[layer, expert], wd_buf.at[slot], sem.at[slot, 1]))

    @pl.when(i == 0)
    def _():
        for cp in fetch(te_ref[0], 0):
            cp.start()

    @pl.when(jnp.logical_and(valid, first_ref[i] == 1))
    def _():
        slot = slot_ref[i]
        for cp in fetch(te_ref[i], slot):
            cp.wait()

        @pl.when(next_ref[i] >= 0)
        def _():
            for cp in fetch(next_ref[i], 1 - slot):
                cp.start()

        for c in range(wgu_buf.shape[2] // DEINT_COLS):
            w = wgu_buf[slot, :, c * DEINT_COLS:(c + 1) * DEINT_COLS].astype(BF16)
            split = _dot(w, sel_ref[...]).astype(BF16)
            wg_ref[:, c * half:(c + 1) * half] = split[:, :half]
            wu_ref[:, c * half:(c + 1) * half] = split[:, half:]
        wdb_ref[...] = wd_buf[slot].astype(BF16)

    @pl.when(valid)
    def _():
        x_l, x_r = _unpack_bf16_halves(x_ref[...])
        gl = _dot(x_l, wg_ref[:k_half, :]) + _dot(x_r, wg_ref[k_half:, :]) + bg_ref[...]
        up = _dot(x_l, wu_ref[:k_half, :]) + _dot(x_r, wu_ref[k_half:, :]) + bu_ref[...]
        gl = jnp.minimum(gl, SWIGLU_LIMIT)
        up = jnp.clip(up, -SWIGLU_LIMIT, SWIGLU_LIMIT)
        act = (up + 1.0) * gl * _sigmoid(SWIGLU_ALPHA * gl)
        y_ref[...] = _dot(act.astype(BF16), wdb_ref[...]) + bd_ref[...]

    @pl.when(jnp.logical_not(valid))
    def _():
        y_ref[...] = jnp.zeros_like(y_ref)


def _experts(xs, tile_expert, tile_first, n_valid, layer, w_gu, b_gate, b_up, w_down, b_down):
    d = D_MODEL
    tm = MOE_TILE
    n_tiles = MOE_ROWS // tm
    ff = w_down.shape[2]
    half = DEINT_COLS // 2
    r = jnp.arange(DEINT_COLS)[:, None]
    c = jnp.arange(DEINT_COLS)[None, :]
    sel = (r == jnp.where(c < half, 2 * c, 2 * (c - half) + 1)).astype(BF16)
    group = jnp.cumsum(tile_first) - 1
    tile_slot = (group % 2).astype(jnp.int32)
    is_last_group = group == group[-1]
    following = jnp.concatenate([tile_expert[1:], tile_expert[-1:]])
    idx = jnp.arange(n_tiles, dtype=jnp.int32)
    group_end = jnp.max(jnp.where(group[None, :] == group[:, None], idx[None, :], -1), axis=1)
    tile_next = jnp.where(is_last_group, -1, following[group_end]).astype(jnp.int32)
    wspec = lambda k, n: pl.BlockSpec((None, None, k, n), lambda i, *_: (layer, _[0][i], 0, 0))
    grid_spec = pltpu.PrefetchScalarGridSpec(
        num_scalar_prefetch=5,
        grid=(n_tiles,),
        in_specs=[pl.BlockSpec((tm, d // 2), lambda i, *_: (i, 0)),
                  pl.BlockSpec(memory_space=pl.ANY), wspec(1, ff), wspec(1, ff),
                  pl.BlockSpec(memory_space=pl.ANY), wspec(1, d),
                  pl.BlockSpec((DEINT_COLS, DEINT_COLS), lambda i, *_: (0, 0))],
        out_specs=pl.BlockSpec((tm, d), lambda i, *_: (i, 0)),
        scratch_shapes=[pltpu.VMEM((2, d, 2 * ff), F32), pltpu.VMEM((2, ff, d), F32),
                        pltpu.VMEM((d, ff), BF16), pltpu.VMEM((d, ff), BF16), pltpu.VMEM((ff, d), BF16),
                        pltpu.SemaphoreType.DMA((2, 2))],
    )
    return pl.pallas_call(
        functools.partial(_experts_kernel, layer),
        grid_spec=grid_spec,
        out_shape=jax.ShapeDtypeStruct((MOE_ROWS, d), F32),
        compiler_params=_cparams("arbitrary"),
        name="experts",
    )(tile_expert, tile_first, tile_slot, tile_next, n_valid, xs, w_gu, b_gate, b_up, w_down, b_down, sel)


def _combine_kernel(final, x_ref, g_ref, w_ref, mod_ref, fg_ref, *o_refs):
    y = w_ref[:, 0:1] * g_ref[0]
    for k in range(1, TOP_K):
        y = y + w_ref[:, k:k + 1] * g_ref[k]
    x = x_ref[...] + mod_ref[5:6, :] * y
    if not final:
        o_refs[0][...] = x
        return
    x = x * lax.rsqrt(jnp.mean(x * x, axis=-1, keepdims=True) + RMS_EPS) * fg_ref[...]
    from_prompt = pl.program_id(0) < N_PROMPT // TOK_TILE

    @pl.when(from_prompt)
    def _():
        o_refs[0][...] = x

    @pl.when(jnp.logical_not(from_prompt))
    def _():
        o_refs[1][...] = x


def _combine(x1, gathered, wts, mod_l, final_g, final):
    d = D_MODEL
    t = TOK_TILE
    n_prompt_tiles = N_PROMPT // t
    if final:
        out_specs = [pl.BlockSpec((t, d), lambda i: (jnp.minimum(i, n_prompt_tiles - 1), 0)),
                     pl.BlockSpec((t, d), lambda i: (jnp.maximum(i - n_prompt_tiles, 0), 0))]
        out_shape = [jax.ShapeDtypeStruct((N_PROMPT, d), F32), jax.ShapeDtypeStruct((N_SAMPLE, d), F32)]
    else:
        out_specs = pl.BlockSpec((t, d), lambda i: (i, 0))
        out_shape = jax.ShapeDtypeStruct((N_TOK, d), F32)
    return pl.pallas_call(
        functools.partial(_combine_kernel, final),
        grid=(N_TOK // t,),
        in_specs=[pl.BlockSpec((t, d), lambda i: (i, 0)),
                  pl.BlockSpec((TOP_K, t, d), lambda i: (0, i, 0)),
                  pl.BlockSpec((t, TOP_K), lambda i: (i, 0)),
                  pl.BlockSpec((None, N_MOD, d), lambda i: (_tile_cond_row(i), 0, 0)),
                  pl.BlockSpec((1, d), lambda i: (0, 0))],
        out_specs=out_specs,
        out_shape=out_shape,
        compiler_params=_cparams("arbitrary"),
        name="moe_combine",
    )(x1, gathered, wts, mod_l, final_g.reshape(1, d))


def _moe(x1, h, eid, wts, rank, counts, mod_l, layer, w_gu, b_gu, w_down, b_down, final_g, final):
    d = D_MODEL
    tm = MOE_TILE
    n_tiles = MOE_ROWS // tm
    cnt = counts.reshape(N_EXPERTS).astype(jnp.int32)
    gsz = ((cnt + tm - 1) // tm) * tm
    ends = jnp.cumsum(gsz)
    offs = ends - gsz
    e_ids = jnp.arange(N_EXPERTS, dtype=jnp.int32)
    pos = jnp.sum(jnp.where(eid[..., None] == e_ids, offs, 0), axis=-1) + rank
    tile_start = jnp.arange(n_tiles, dtype=jnp.int32) * tm
    tile_expert = jnp.minimum(jnp.sum((ends[None, :] <= tile_start[:, None]).astype(jnp.int32), axis=1),
                              N_EXPERTS - 1)
    n_valid = (ends[-1:] // tm).astype(jnp.int32)
    last_valid = jnp.maximum(n_valid[0] - 1, 0)
    tile_expert = jnp.where(jnp.arange(n_tiles) < n_valid[0], tile_expert, tile_expert[last_valid])
    tile_first = jnp.concatenate([jnp.ones((1,), jnp.int32),
                                  (tile_expert[1:] != tile_expert[:-1]).astype(jnp.int32)])
    j = jnp.arange(tm, dtype=jnp.int32)[None, :]
    pad_used = j < (gsz - cnt)[:, None]
    n_unused_before = jnp.cumsum((~pad_used).reshape(-1).astype(jnp.int32)) - 1
    pad_pos = jnp.where(pad_used, (offs + cnt)[:, None] + j,
                        ends[-1] + n_unused_before.reshape(N_EXPERTS, tm))
    slot_of_row = jnp.argsort(jnp.concatenate([pos.reshape(-1), pad_pos.reshape(-1)]))
    token_of_row = (slot_of_row % N_TOK).astype(jnp.int32)

    xs = _sc_row_gather(h, token_of_row)
    ys = _experts(xs, tile_expert, tile_first, n_valid, layer, w_gu,
                  b_gu[:, :, None, 0::2], b_gu[:, :, None, 1::2], w_down, b_down[:, :, None, :])
    gathered = _sc_row_gather(ys, pos.reshape(-1))
    return _combine(x1, gathered.reshape(TOP_K, N_TOK, d), wts.T, mod_l, final_g, final)


def _grid_positions(n_tok, d):
    rows = n_tok // GRID_W
    r, col = jnp.meshgrid(jnp.arange(rows, dtype=F32), jnp.arange(GRID_W, dtype=F32), indexing='ij')
    r = r.reshape(-1)
    col = col.reshape(-1)
    quarter = d // 4
    inv = 1.0 / (10000.0 ** (jnp.arange(quarter, dtype=F32) / quarter))
    ar = r[:, None] * inv[None]
    ac = col[:, None] * inv[None]
    return jnp.concatenate([jnp.sin(ar), jnp.cos(ar), jnp.sin(ac), jnp.cos(ac)], axis=-1)


def kernel(x_prompt, x_sample, state_hgrn, state_mlstm_c, state_mlstm_n, state_mlstm_m, c, c_ctx,
           norm_g, final_g, w_mod, b_mod, ev_w_in, ev_gate_b, ev_conv, hg_lb, ev_w_out,
           hy_w_in, hy_conv, hy_w1, hy_b1, hy_w2, hy_b2, hy_w3, hy_b3, hy_freq, hy_log_rate, hy_bias, hy_w_out,
           w_router, b_router, w_gu, b_gu, w_down, b_down):
    d = D_MODEL
    hd = HEAD_DIM
    cond = jnp.concatenate([c_ctx[None], c, jnp.zeros((N_COND - 1 - DEC_BATCH, d), F32)], axis=0)
    mod = _modulation(cond, w_mod, b_mod)
    pos_tab = jnp.concatenate([jnp.zeros((TOK_TILE, d), F32), _grid_positions(DEC_SEQ, d)], axis=0)

    groups = ((BATCH, SEQ, 0), (DEC_BATCH, DEC_SEQ, N_PROMPT))
    new_states = None
    for l in range(DEPTH):
        if l % 2 == 0:
            e = l // 2
            if l == 0:
                x, p, gate, gate_t = _proj_even(x_prompt.reshape(N_PROMPT, d), x_sample.reshape(N_SAMPLE, d),
                                                pos_tab, mod[l], norm_g[l, 0], ev_w_in[e], ev_gate_b[e])
            else:
                raise NotImplementedError("only the first layer adds grid positions")
            gate_h = gate.reshape(N_TOK, 4, HEADS).transpose(2, 0, 1)
            gate_t_h = gate_t.reshape(4, HEADS, N_TOK).transpose(1, 0, 2)
            o_hg, o_ml = [], []
            for gi, (n_seq, seq_len, off) in enumerate(groups):
                if gi == 0:
                    s0 = jnp.zeros((n_seq, 2, HEADS, hd, hd), F32)
                    c0 = jnp.zeros((n_seq, 2, HEADS, hd, hd), F32)
                    n0 = jnp.zeros((n_seq, 2, HEADS, 1, hd), F32)
                    m0 = jnp.zeros((n_seq, 2, HEADS, 1, 1), F32)
                else:
                    s0 = state_hgrn[:, e]
                    c0 = state_mlstm_c[:, e]
                    n0 = state_mlstm_n[:, e].reshape(n_seq, 2, HEADS, 1, hd)
                    m0 = state_mlstm_m[:, e].reshape(n_seq, 2, HEADS, 1, 1)
                og, s_fin = _hgrn(p, hg_lb, l, s0, n_seq, seq_len, off)
                om, c_fin, n_fin, m_fin = _mlstm(p, gate_h, gate_t_h, ev_conv[e], c0, n0, m0, n_seq, seq_len, off)
                o_hg.append(og)
                o_ml.append(om)
                if gi == 0:
                    new_states = (s_fin[:, None], c_fin[:, None],
                                  n_fin.reshape(n_seq, 1, 2, HEADS, hd), m_fin.reshape(n_seq, 1, 2, HEADS))
            parts = [(*o_hg, ev_w_out[e][:GROUP_W]), (*o_ml, ev_w_out[e][GROUP_W:])]
        else:
            o = l // 2
            u = _proj_odd(x, mod[l], norm_g[l, 0], hy_w_in[o])
            zs = []
            for n_seq, seq_len, off in groups:
                f_tab = _dft_tables(seq_len)
                kf = _hyena_filters(seq_len, hy_w1[o], hy_b1[o], hy_w2[o], hy_b2[o], hy_w3[o], hy_b3[o],
                                    hy_freq[o], hy_log_rate[o], f_tab)
                zs.append(_hyena(u, hy_conv[o], hy_bias[o], kf, f_tab, n_seq, seq_len, off))
            parts = [(*zs, hy_w_out[o])]
        x1, h, eid, wts, rank, counts = _mix_out(parts, x, mod[l], norm_g[l, 1], w_router[l], b_router[l])
        x = _moe(x1, h, eid, wts, rank, counts, mod[l], l, w_gu, b_gu, w_down, b_down,
                 final_g, final=(l == DEPTH - 1))

    y_prompt, y_sample = x
    return (y_prompt.reshape(BATCH, SEQ, d), y_sample.reshape(DEC_BATCH, DEC_SEQ, d)) + new_states
```

```python
import functools

import numpy as np
import jax
import jax.numpy as jnp
from jax import lax
from jax.experimental import pallas as pl
from jax.experimental.pallas import tpu as pltpu
from jax.experimental.pallas import tpu_sc as plsc

F32 = jnp.float32
BF16 = jnp.bfloat16

D_MODEL = 1024
BATCH = 32
SEQ = 256
DEPTH = 2
DEC_BATCH = 8
DEC_SEQ = 1024
GRID_W = 64
RMS_EPS = 1e-6
N_MOD = 6
LOG2_E = 1.4426950408889634
LANES = 128

HEADS = 4
HEAD_DIM = 128
GROUP_W = HEADS * HEAD_DIM
N_GATES = 4 * HEADS
EVEN_MAIN = 9 * GROUP_W

HY_ORDER = 2
HY_BANDS = 16
HY_EMB = 1 + 2 * HY_BANDS
HY_HIDDEN = 64
HY_FREQ_CHUNK = 512
HY_STEP_ROWS = 1024

N_EXPERTS = 32
TOP_K = 4
SWIGLU_LIMIT = 7.0
SWIGLU_ALPHA = 1.702

N_PROMPT = BATCH * SEQ
N_SAMPLE = DEC_BATCH * DEC_SEQ
N_TOK = N_PROMPT + N_SAMPLE
N_COND = 16

TOK_TILE = 256
SCAN_CHUNK = 128
SUB = 16
MOE_TILE = 256
MOE_ROWS = N_TOK * TOP_K + N_EXPERTS * MOE_TILE
DEINT_COLS = 256
SC_STEP_BYTES = 256 * 1024
SC_MAX_INDICES = 128

VMEM_LIMIT = 56 * 1024 * 1024


def _cparams(*sem):
    return pltpu.CompilerParams(dimension_semantics=sem, vmem_limit_bytes=VMEM_LIMIT)


def _split3(x):
    hi = x.astype(BF16)
    r = x - hi.astype(F32)
    mid = r.astype(BF16)
    lo = (r - mid.astype(F32)).astype(BF16)
    return hi, mid, lo


def _dot(a, b):
    return jnp.dot(a, b, preferred_element_type=F32)


def _dot_nt(a, b):
    return lax.dot_general(a, b, (((1,), (1,)), ((), ())), preferred_element_type=F32)


def _dot_tn(a, b):
    return lax.dot_general(a, b, (((0,), (0,)), ((), ())), preferred_element_type=F32)


def _dot_w3(a_exact_bf16, x):
    hi, mid, lo = _split3(x)
    return _dot(a_exact_bf16, hi) + _dot(a_exact_bf16, mid) + _dot(a_exact_bf16, lo)


def _sigmoid(x):
    return 1.0 / (1.0 + jnp.exp(-x))


def _silu(x):
    return x * _sigmoid(x)


def _log_sigmoid(x):
    return jnp.minimum(x, 0.0) - jnp.log(1.0 + jnp.exp(-jnp.abs(x)))


def _tile_cond_row(i):
    n_prompt_tiles = N_PROMPT // TOK_TILE
    tiles_per_seq = DEC_SEQ // TOK_TILE
    return jnp.where(i < n_prompt_tiles, 0, 1 + (i - n_prompt_tiles) // tiles_per_seq)


def _mod_kernel(cond_ref, w_ref, b_ref, o_ref):
    a = _silu(cond_ref[...]).astype(BF16)
    o_ref[...] = _dot(a, w_ref[...].astype(BF16)) + b_ref[...]


def _modulation(cond, w_mod, b_mod):
    d = D_MODEL
    out = pl.pallas_call(
        _mod_kernel,
        grid=(DEPTH, N_MOD),
        in_specs=[
            pl.BlockSpec((N_COND, d), lambda l, j: (0, 0)),
            pl.BlockSpec((None, d, d), lambda l, j: (l, 0, j)),
            pl.BlockSpec((None, 1, d), lambda l, j: (l, 0, j)),
        ],
        out_specs=pl.BlockSpec((None, None, N_COND, d), lambda l, j: (l, j, 0, 0)),
        out_shape=jax.ShapeDtypeStruct((DEPTH, N_MOD, N_COND, d), F32),
        compiler_params=_cparams("parallel", "parallel"),
        name="modulation",
    )(cond, w_mod, b_mod.reshape(DEPTH, 1, N_MOD * d))
    return out.transpose(0, 2, 1, 3)


def _norm_mod(x, g_row, scale_row, shift_row):
    ms = jnp.mean(x * x, axis=-1, keepdims=True)
    y = x * lax.rsqrt(ms + RMS_EPS) * g_row
    return y * (1.0 + scale_row) + shift_row


def _proj_even_kernel(xp_ref, xs_ref, pos_ref, mod_ref, g_ref, w_ref, wgt_ref, gbt_ref,
                      xres_ref, p_ref, gate_t_ref):
    from_prompt = pl.program_id(0) < N_PROMPT // TOK_TILE
    x = jnp.where(from_prompt, xp_ref[...], xs_ref[...]) + pos_ref[...]
    xres_ref[...] = x
    h = _norm_mod(x, g_ref[...], mod_ref[1:2, :], mod_ref[0:1, :]).astype(BF16)
    p_ref[...] = _dot(h, w_ref[...])
    gate_t_ref[...] = _dot_nt(wgt_ref[...], h) + gbt_ref[...]


def _proj_even(x_prompt, x_sample, pos_tab, mod_l, norm_g, w_in, gate_b):
    d = D_MODEL
    n_tiles = N_TOK // TOK_TILE
    n_prompt_tiles = N_PROMPT // TOK_TILE
    tiles_per_seq = DEC_SEQ // TOK_TILE
    w_main = w_in[:, :EVEN_MAIN].astype(BF16)
    w_gate = w_in[:, EVEN_MAIN:].astype(BF16)

    def pos_map(i):
        return (jnp.where(i < n_prompt_tiles, 0, 1 + (i - n_prompt_tiles) % tiles_per_seq), 0)

    return pl.pallas_call(
        _proj_even_kernel,
        grid=(n_tiles,),
        in_specs=[
            pl.BlockSpec((TOK_TILE, d), lambda i: (jnp.minimum(i, n_prompt_tiles - 1), 0)),
            pl.BlockSpec((TOK_TILE, d), lambda i: (jnp.maximum(i - n_prompt_tiles, 0), 0)),
            pl.BlockSpec((TOK_TILE, d), pos_map),
            pl.BlockSpec((None, N_MOD, d), lambda i: (_tile_cond_row(i), 0, 0)),
            pl.BlockSpec((1, d), lambda i: (0, 0)),
            pl.BlockSpec((d, EVEN_MAIN), lambda i: (0, 0)),
            pl.BlockSpec((N_GATES, d), lambda i: (0, 0)),
            pl.BlockSpec((N_GATES, 1), lambda i: (0, 0)),
        ],
        out_specs=[
            pl.BlockSpec((TOK_TILE, d), lambda i: (i, 0)),
            pl.BlockSpec((TOK_TILE, EVEN_MAIN), lambda i: (i, 0)),
            pl.BlockSpec((N_GATES, TOK_TILE), lambda i: (0, i)),
        ],
        out_shape=[
            jax.ShapeDtypeStruct((N_TOK, d), F32),
            jax.ShapeDtypeStruct((N_TOK, EVEN_MAIN), F32),
            jax.ShapeDtypeStruct((N_GATES, N_TOK), F32),
        ],
        compiler_params=_cparams("parallel"),
        name="proj_even",
    )(x_prompt, x_sample, pos_tab, mod_l, norm_g.reshape(1, d), w_main, w_gate.T, gate_b.reshape(N_GATES, 1))


def _proj_odd_kernel(x_ref, mod_ref, g_ref, w_ref, p_ref):
    h = _norm_mod(x_ref[...], g_ref[...], mod_ref[1:2, :], mod_ref[0:1, :]).astype(BF16)
    p_ref[...] = _dot(h, w_ref[...])


def _proj_odd(x, mod_l, norm_g, w_in):
    d = D_MODEL
    width = w_in.shape[1]
    return pl.pallas_call(
        _proj_odd_kernel,
        grid=(N_TOK // TOK_TILE,),
        in_specs=[
            pl.BlockSpec((TOK_TILE, d), lambda i: (i, 0)),
            pl.BlockSpec((None, N_MOD, d), lambda i: (_tile_cond_row(i), 0, 0)),
            pl.BlockSpec((1, d), lambda i: (0, 0)),
            pl.BlockSpec((d, width), lambda i: (0, 0)),
        ],
        out_specs=pl.BlockSpec((TOK_TILE, width), lambda i: (i, 0)),
        out_shape=jax.ShapeDtypeStruct((N_TOK, width), F32),
        compiler_params=_cparams("parallel"),
        name="proj_odd",
    )(x, mod_l, norm_g.reshape(1, d), w_in.astype(BF16))


def _hgrn_chunks(chains):
    c = SCAN_CHUNK
    n = range(len(chains))
    rev, q, k, v, lf, st = zip(*chains)
    row = lax.broadcasted_iota(jnp.int32, (c, c), 0)
    col = lax.broadcasted_iota(jnp.int32, (c, c), 1)
    tris = {False: col <= row, True: col >= row}
    tris_b = {r: jnp.where(t, 1.0, 0.0).astype(BF16) for r, t in tris.items()}
    b = [_dot_w3(tris_b[rev[j]], lf[j]) for j in n]
    b2 = [b[j] * LOG2_E for j in n]
    lane = lax.broadcasted_iota(jnp.int32, (SUB, c), 1)
    chunk_row = lax.broadcasted_iota(jnp.int32, (c, 1), 0)
    rows = [[] for _ in n]
    for i in range(c // SUB):
        lo, hi = i * SUB, (i + 1) * SUB
        a_row = []
        for j in n:
            if rev[j]:
                has_off, edge, outside = hi < c, hi, chunk_row >= hi
            else:
                has_off, edge, outside = lo > 0, lo - 1, chunk_row < lo
            if has_off:
                beta = b[j][edge:edge + 1]
                qs = q[j][lo:hi] * jnp.exp(b[j][lo:hi] - beta)
                ks = k[j] * jnp.exp(jnp.where(outside, beta - b[j], -jnp.inf))
                a_row.append(_dot_nt(qs.astype(BF16), ks.astype(BF16)))
            else:
                a_row.append(jnp.zeros((SUB, c), F32))
        for s in range(SUB):
            for j in n:
                bi2 = b2[j][lo:hi]
                a_col = jnp.sum(jnp.exp2(bi2 - bi2[s:s + 1]) * q[j][lo:hi] * k[j][lo + s:lo + s + 1],
                                axis=-1, keepdims=True)
                a_row[j] = jnp.where(lane == lo + s, a_col, a_row[j])
        for j in n:
            rows[j].append(a_row[j])
    out = []
    for j in n:
        attn = jnp.where(tris[rev[j]], jnp.concatenate(rows[j], axis=0), 0.0)
        o = _dot(attn.astype(BF16), v[j].astype(BF16)) \
            + _dot_nt((q[j] * jnp.exp(b[j])).astype(BF16), st[j].astype(BF16))
        b_exit = b[j][0:1] if rev[j] else b[j][c - 1:c]
        k_out = k[j] * jnp.exp(b_exit - b[j])
        st_new = jnp.exp(b_exit) * st[j] + _dot_tn(v[j].astype(BF16), k_out.astype(BF16))
        out.append((o, st_new))
    return out


def _hgrn_kernel(seq_len, layer, q_ref, i_ref, g_ref, ff_ref, fb_ref, lb_ref, s0_ref,
                 o_ref, s_out_ref, of_ref, ob_ref, st_ref):
    c = SCAN_CHUNK
    n_chunks = seq_len // c
    lbp = lb_ref[...]
    e = jnp.exp(lbp - jnp.max(lbp, axis=0, keepdims=True))
    lb = jnp.sum(e[0:layer + 1], axis=0, keepdims=True) / jnp.sum(e, axis=0, keepdims=True)

    st_ref[0] = s0_ref[0].T
    st_ref[1] = s0_ref[1].T

    def body(n, carry):
        slices, chains = [], []
        for d in range(2):
            sl = pl.ds(pl.multiple_of((n_chunks - 1 - n if d else n) * c, c), c)
            f = lb + (1.0 - lb) * _sigmoid((fb_ref if d else ff_ref)[sl, :])
            slices.append(sl)
            chains.append((bool(d), q_ref[sl, :], 1.0 - f, i_ref[sl, :], jnp.log(f), st_ref[d]))
        for d, (sl, (o, st_new)) in enumerate(zip(slices, _hgrn_chunks(chains))):
            st_ref[d] = st_new
            (ob_ref if d else of_ref)[sl, :] = o
        return carry

    lax.fori_loop(0, n_chunks, body, 0)
    o = of_ref[...] + ob_ref[...]
    o = o * lax.rsqrt(jnp.mean(o * o, axis=-1, keepdims=True) + RMS_EPS)
    o_ref[...] = o * _silu(g_ref[...])
    s_out_ref[0] = st_ref[0].T
    s_out_ref[1] = st_ref[1].T


def _hgrn(p, hg_lb, layer, s0, n_seq, seq_len, tok_offset):
    hd = HEAD_DIM
    row0 = tok_offset // seq_len

    def col(part):
        return pl.BlockSpec((seq_len, hd), lambda b, h: (row0 + b, part * HEADS + h))

    state_spec = pl.BlockSpec((None, 2, None, hd, hd), lambda b, h: (b, 0, h, 0, 0))
    return pl.pallas_call(
        functools.partial(_hgrn_kernel, seq_len, layer),
        grid=(n_seq, HEADS),
        in_specs=[col(0), col(1), col(2), col(3), col(4),
                  pl.BlockSpec((DEPTH + 1, hd), lambda b, h: (0, h)),
                  state_spec],
        out_specs=[pl.BlockSpec((seq_len, hd), lambda b, h: (b, h)), state_spec],
        out_shape=[jax.ShapeDtypeStruct((n_seq * seq_len, GROUP_W), F32),
                   jax.ShapeDtypeStruct((n_seq, 2, HEADS, hd, hd), F32)],
        scratch_shapes=[pltpu.VMEM((seq_len, hd), F32), pltpu.VMEM((seq_len, hd), F32),
                        pltpu.VMEM((2, hd, hd), F32)],
        compiler_params=_cparams("parallel", "parallel"),
        name=f"hgrn_l{seq_len}",
    )(p, p, p, p, p, hg_lb, s0)


def _short_conv3(x, w):
    n = x.shape[0]
    r = lax.broadcasted_iota(jnp.int32, (n, 1), 0)
    prev = jnp.where(r == 0, 0.0, pltpu.roll(x, 1, 0))
    nxt = jnp.where(r == n - 1, 0.0, pltpu.roll(x, n - 1, 0))
    return prev * w[0:1] + x * w[1:2] + nxt * w[2:3]


def _split2(x):
    hi = x.astype(BF16)
    return hi, (x - hi.astype(F32)).astype(BF16)


def _mlstm_chunks(chains):
    c = SCAN_CHUNK
    row = lax.broadcasted_iota(jnp.int32, (c, c), 0)
    col = lax.broadcasted_iota(jnp.int32, (c, c), 1)
    eye_b = jnp.where(row == col, 1.0, 0.0).astype(BF16)
    tris = {False: row <= col, True: row >= col}
    tris_b = {r: jnp.where(t, 1.0, 0.0).astype(BF16) for r, t in tris.items()}
    n = range(len(chains))
    rev, q, k, vt, ig, fg, ct, nv, m_prev = zip(*chains)

    def each(fn):
        return [fn(i) for i in n]

    def dot3(parts, rhs, nt=False):
        d = _dot_nt if nt else _dot
        return d(parts[0], rhs) + d(parts[1], rhs) + d(parts[2], rhs)

    lf = each(lambda i: _split3(jnp.broadcast_to(_log_sigmoid(fg[i]), (8, c))))
    b = each(lambda i: dot3(lf[i], tris_b[rev[i]])[0:1])
    us = each(lambda i: _split3(jnp.broadcast_to(ig[i] - b[i], (c, c))))
    u = each(lambda i: _dot_nt(eye_b, us[i][0]) + _dot_nt(eye_b, us[i][1]) + _dot_nt(eye_b, us[i][2]))
    dmat = each(lambda i: jnp.where(tris[rev[i]], b[i] + u[i], -jnp.inf))
    m_t = each(lambda i: jnp.maximum(b[i] + m_prev[i], jnp.max(dmat[i], axis=0, keepdims=True)))
    qb = each(lambda i: q[i].astype(BF16))
    kb = each(lambda i: k[i].astype(BF16))
    kq = each(lambda i: _dot_nt(kb[i], qb[i]))
    p = each(lambda i: jnp.exp(dmat[i] - m_t[i]) * kq[i])
    inter = each(lambda i: jnp.exp(b[i] + m_prev[i] - m_t[i]))
    ns = each(lambda i: _split2(jnp.broadcast_to(nv[i], (8, nv[i].shape[1]))))
    qs = each(lambda i: _split2(q[i]))
    qn = each(lambda i: (_dot_nt(ns[i][0], qs[i][0]) + _dot_nt(ns[i][1], qs[i][0])
                         + _dot_nt(ns[i][0], qs[i][1]))[0:1])
    den = each(lambda i: inter[i] * qn[i] + jnp.sum(p[i], axis=0, keepdims=True))
    scale = each(lambda i: 1.0 / jnp.maximum(jnp.abs(den[i]), jnp.exp(-m_t[i])))
    cq = each(lambda i: _dot_nt(ct[i].astype(BF16), qb[i]))
    vp = each(lambda i: _dot(vt[i].astype(BF16), p[i].astype(BF16)))
    ht = each(lambda i: (inter[i] * cq[i] + vp[i]) * scale[i])
    last = each(lambda i: 0 if rev[i] else c - 1)
    m_new = each(lambda i: m_t[i][:, last[i]:last[i] + 1])
    b_exit = each(lambda i: b[i][:, last[i]:last[i] + 1])
    w = each(lambda i: jnp.exp(b_exit[i] - b[i] + ig[i] - m_new[i]))
    dec = each(lambda i: jnp.exp(b_exit[i] + m_prev[i] - m_new[i]))
    vk = each(lambda i: _dot((vt[i] * w[i]).astype(BF16), kb[i]))
    ct_new = each(lambda i: dec[i] * ct[i] + vk[i])
    ws = each(lambda i: _split2(jnp.broadcast_to(w[i], (8, c))))
    ks = each(lambda i: _split2(k[i]))
    wk = each(lambda i: (_dot(ws[i][0], ks[i][0]) + _dot(ws[i][1], ks[i][0]) + _dot(ws[i][0], ks[i][1]))[0:1])
    nv_new = each(lambda i: dec[i] * nv[i] + wk[i])
    return [(ht[i], ct_new[i], nv_new[i], m_new[i]) for i in n]


def _mlstm_kernel(seq_len, q_ref, k_ref, v_ref, og_ref, gate_t_ref, cwq_ref, cwk_ref,
                  c0_ref, n0_ref, m0_ref,
                  o_ref, c_out_ref, n_out_ref, m_out_ref,
                  q2_ref, k2_ref, vt_ref, hf_ref, hb_ref, ct_ref, n_ref, m_ref):
    c = SCAN_CHUNK
    hd = HEAD_DIM
    n_chunks = seq_len // c
    q2_ref[...] = _silu(_short_conv3(q_ref[...], cwq_ref[...]))
    k2_ref[...] = _silu(_short_conv3(k_ref[...], cwk_ref[...])) * (HEAD_DIM ** -0.5)
    for hh in range(HEADS):
        cols = slice(hh * hd, (hh + 1) * hd)
        for j in range(n_chunks):
            vt_ref[hh, :, j * c:(j + 1) * c] = v_ref[j * c:(j + 1) * c, cols].T
        for d in range(2):
            ct_ref[d, hh] = c0_ref[d, hh].T
    n_ref[...] = n0_ref[...]
    m_ref[...] = m0_ref[...]

    def body(n, carry):
        where, chains = [], []
        for d in range(2):
            sl = pl.ds(pl.multiple_of((n_chunks - 1 - n if d else n) * c, c), c)
            for hh in range(HEADS):
                cols = slice(hh * hd, (hh + 1) * hd)
                gr = gate_t_ref[hh, :, sl]
                where.append((d, hh, sl))
                chains.append((bool(d), q2_ref[sl, cols], k2_ref[sl, cols], vt_ref[hh, :, sl],
                               gr[2 * d:2 * d + 1, :], gr[2 * d + 1:2 * d + 2, :],
                               ct_ref[d, hh], n_ref[d, hh], m_ref[d, hh]))
        for (d, hh, sl), (ht, ct, nv, m_new) in zip(where, _mlstm_chunks(chains)):
            ct_ref[d, hh] = ct
            n_ref[d, hh] = nv
            m_ref[d, hh] = m_new
            (hb_ref if d else hf_ref)[hh, :, sl] = ht
        return carry

    lax.fori_loop(0, n_chunks, body, 0)
    for hh in range(HEADS):
        cols = slice(hh * hd, (hh + 1) * hd)
        for j in range(n_chunks):
            rows = slice(j * c, (j + 1) * c)
            h = (hf_ref[hh, :, rows] + hb_ref[hh, :, rows]).T
            h = h * lax.rsqrt(jnp.mean(h * h, axis=-1, keepdims=True) + RMS_EPS)
            o_ref[rows, cols] = h * _sigmoid(og_ref[rows, cols])
        for d in range(2):
            c_out_ref[d, hh] = ct_ref[d, hh].T
    n_out_ref[...] = n_ref[...]
    m_out_ref[...] = m_ref[...]


def _mlstm(p, gate_t_h, conv_w, c0, n0, m0, n_seq, seq_len, tok_offset):
    hd = HEAD_DIM
    gw = GROUP_W
    row0 = tok_offset // seq_len

    def col(part):
        return pl.BlockSpec((seq_len, gw), lambda b: (row0 + b, part))

    c_spec = pl.BlockSpec((None, 2, HEADS, hd, hd), lambda b: (b, 0, 0, 0, 0))
    n_spec = pl.BlockSpec((None, 2, HEADS, 1, hd), lambda b: (b, 0, 0, 0, 0))
    m_spec = pl.BlockSpec((None, 2, HEADS, 1, 1), lambda b: (b, 0, 0, 0, 0))
    return pl.pallas_call(
        functools.partial(_mlstm_kernel, seq_len),
        grid=(n_seq,),
        in_specs=[col(5), col(6), col(7), col(8),
                  pl.BlockSpec((HEADS, 4, seq_len), lambda b: (0, 0, row0 + b)),
                  pl.BlockSpec((3, gw), lambda b: (0, 0)),
                  pl.BlockSpec((3, gw), lambda b: (0, 1)),
                  c_spec, n_spec, m_spec],
        out_specs=[pl.BlockSpec((seq_len, gw), lambda b: (b, 0)), c_spec, n_spec, m_spec],
        out_shape=[jax.ShapeDtypeStruct((n_seq * seq_len, gw), F32),
                   jax.ShapeDtypeStruct((n_seq, 2, HEADS, hd, hd), F32),
                   jax.ShapeDtypeStruct((n_seq, 2, HEADS, 1, hd), F32),
                   jax.ShapeDtypeStruct((n_seq, 2, HEADS, 1, 1), F32)],
        scratch_shapes=[pltpu.VMEM((seq_len, gw), F32), pltpu.VMEM((seq_len, gw), F32),
                        pltpu.VMEM((HEADS, hd, seq_len), F32),
                        pltpu.VMEM((HEADS, hd, seq_len), F32), pltpu.VMEM((HEADS, hd, seq_len), F32),
                        pltpu.VMEM((2, HEADS, hd, hd), F32), pltpu.VMEM((2, HEADS, 1, hd), F32),
                        pltpu.VMEM((2, HEADS, 1, 1), F32)],
        compiler_params=_cparams("parallel"),
        name=f"mlstm_l{seq_len}",
    )(p, p, p, p, gate_t_h, conv_w, conv_w, c0, n0, m0)


def _dft_tables(seq_len):
    n = 2 * seq_len
    k = jnp.arange(seq_len, dtype=jnp.int32)[:, None]
    t = jnp.arange(seq_len, dtype=jnp.int32)[None, :]
    ang = ((k * t) % n).astype(F32) * (2.0 * np.pi / n)
    fc = jnp.cos(ang)
    fs = jnp.sin(ang)
    nyq = jnp.where(t % 2 == 0, 1.0, -1.0).astype(F32)
    fs = jnp.where(k == 0, nyq, fs)
    return jnp.concatenate([fc, fs], axis=0)


def _filter_kernel(seq_len, z_ref, w1_ref, b1_ref, w2_ref, b2_ref, w3f_ref, w3b_ref, b3f_ref, b3b_ref,
                   f0_ref, f1_ref, rf_ref, rb_ref, fhi_ref, flo_ref, kf_ref, a_ref):
    hp = lax.Precision.HIGHEST
    n = 2 * seq_len
    z = z_ref[...]

    @pl.when(jnp.logical_and(pl.program_id(0) == 0, pl.program_id(1) == 0))
    def _():
        a1 = jnp.sin(f0_ref[...] * (jnp.dot(z, w1_ref[...], precision=hp, preferred_element_type=F32)
                                    + b1_ref[...]))
        a_ref[...] = jnp.sin(f1_ref[...] * (jnp.dot(a1, w2_ref[...], precision=hp, preferred_element_type=F32)
                                            + b2_ref[...]))

    a = a_ref[...]
    t_norm = z[:, 0:1]
    hf = (jnp.dot(a, w3f_ref[...], precision=hp, preferred_element_type=F32) + b3f_ref[...]) \
        * jnp.exp(-t_norm * jnp.exp(rf_ref[...]))
    hb = (jnp.dot(a, w3b_ref[...], precision=hp, preferred_element_type=F32) + b3b_ref[...]) \
        * jnp.exp(-t_norm * jnp.exp(rb_ref[...]))
    inv = lax.rsqrt(jnp.sum(hf * hf, axis=0, keepdims=True) + jnp.sum(hb * hb, axis=0, keepdims=True))
    hf = hf * inv
    r = lax.broadcasted_iota(jnp.int32, (seq_len, 1), 0)
    hb = jnp.where(r == 0, 0.0, hb * inv)
    sh, sm, sl = _split3(hf + hb)
    dh, dm, dl = _split3(hf - hb)
    fhi = fhi_ref[...]
    flo = flo_ref[...]
    kc = _dot(fhi[:seq_len], sh) + _dot(fhi[:seq_len], sm) + _dot(fhi[:seq_len], sl) \
        + _dot(flo[:seq_len], sh) + _dot(flo[:seq_len], sm)
    ks = _dot(fhi[seq_len:], dh) + _dot(fhi[seq_len:], dm) + _dot(fhi[seq_len:], dl) \
        + _dot(flo[seq_len:], dh) + _dot(flo[seq_len:], dm)
    sign = jnp.where(r % 2 == 0, 1.0, -1.0)
    k_nyq = jnp.sum(sign * (hf + hb), axis=0, keepdims=True)
    ks = jnp.where(r == 0, k_nyq, ks)
    scale = jnp.where(r == 0, 1.0 / n, 2.0 / n)
    kf_ref[0:seq_len, :] = kc * scale
    kf_ref[seq_len:n, :] = ks * scale


def _hyena_filters(seq_len, w1, b1, w2, b2, w3, b3, freq, log_rate, f_tab):
    d = D_MODEL
    cb = 256
    t = jnp.arange(seq_len, dtype=F32)
    t_norm = t / (seq_len - 1)
    bands = jnp.linspace(1e-4, HY_BANDS - 1, HY_BANDS, dtype=F32)
    ang = (2.0 * np.pi / seq_len) * t[:, None] * bands[None, :]
    z = jnp.concatenate([t_norm[:, None], jnp.cos(ang), jnp.sin(ang)], axis=-1)
    kpad = 128 - HY_EMB
    z = jnp.pad(z, ((0, 0), (0, kpad)))
    w1p = jnp.pad(w1, ((0, kpad), (0, 0)))
    f_hi = f_tab.astype(BF16)
    f_lo = (f_tab - f_hi.astype(F32)).astype(BF16)
    n_cb = d // cb
    hh = HY_HIDDEN
    row = lambda a: a.reshape(1, -1)
    const = lambda shape: pl.BlockSpec(shape, lambda o, j: (0,) * len(shape))
    fwd = lambda rows: pl.BlockSpec((rows, cb), lambda o, j: (0, o * n_cb + j))
    bwd = lambda rows: pl.BlockSpec((rows, cb), lambda o, j: (0, (HY_ORDER + o) * n_cb + j))
    return pl.pallas_call(
        functools.partial(_filter_kernel, seq_len),
        grid=(HY_ORDER, n_cb),
        in_specs=[const((seq_len, 128)), const((128, hh)), const((1, hh)), const((hh, hh)), const((1, hh)),
                  fwd(hh), bwd(hh), fwd(1), bwd(1),
                  const((1, hh)), const((1, hh)), fwd(1), bwd(1),
                  const((2 * seq_len, seq_len)), const((2 * seq_len, seq_len))],
        out_specs=pl.BlockSpec((None, 2 * seq_len, cb), lambda o, j: (o, 0, j)),
        out_shape=jax.ShapeDtypeStruct((HY_ORDER, 2 * seq_len, d), F32),
        scratch_shapes=[pltpu.VMEM((seq_len, hh), F32)],
        compiler_params=_cparams("arbitrary", "arbitrary"),
        name=f"hyena_filter_l{seq_len}",
    )(z, w1p, row(b1), w2, row(b2), w3, w3, row(b3), row(b3),
      row(freq[0]), row(freq[1]), row(log_rate), row(log_rate), f_hi, f_lo)


def _hyena_kernel(seq_len, seqs, v_ref, x1_ref, x2_ref, cwv_ref, cw1_ref, cw2_ref, bias_ref, kf_ref,
                  f_ref, ft_ref, o_ref, z_ref, zb_ref, y_ref):
    kc = min(HY_FREQ_CHUNK, seq_len)
    n_k = seq_len // kc
    r = lax.broadcasted_iota(jnp.int32, (kc, 1), 0)
    gate_refs = ((x1_ref, cw1_ref), (x2_ref, cw2_ref))
    for s in range(seqs):
        rows = slice(s * seq_len, (s + 1) * seq_len)
        z_ref[s] = _short_conv3(v_ref[rows, :], cwv_ref[...])
    for o in range(HY_ORDER):
        for s in range(seqs):
            zb_ref[s] = z_ref[s].astype(BF16)
            y_ref[s] = jnp.zeros(y_ref.shape[1:], F32)

        def freq_chunk(j, carry):
            r0 = pl.multiple_of(j * kc, kc)
            k_cos = kf_ref[o, pl.ds(r0, kc), :]
            k_sin = kf_ref[o, pl.ds(seq_len + r0, kc), :]
            real_row = jnp.logical_and(r == 0, j == 0)
            for s in range(seqs):
                a = _dot(f_ref[pl.ds(r0, kc), :], zb_ref[s])
                bm = _dot(f_ref[pl.ds(seq_len + r0, kc), :], zb_ref[s])
                yc = a * k_cos - jnp.where(real_row, 0.0, bm * k_sin)
                ys = jnp.where(real_row, bm * k_sin, a * k_sin + bm * k_cos)
                y_ref[s] += _dot(ft_ref[j], yc.astype(BF16)) + _dot(ft_ref[n_k + j], ys.astype(BF16))
            return carry

        lax.fori_loop(0, n_k, freq_chunk, 0)
        x_ref, cw_ref = gate_refs[o]
        for s in range(seqs):
            rows = slice(s * seq_len, (s + 1) * seq_len)
            gate = _short_conv3(x_ref[rows, :], cw_ref[...])
            z_ref[s] = gate * (y_ref[s] + z_ref[s] * bias_ref[o:o + 1, :])
    for s in range(seqs):
        o_ref[s * seq_len:(s + 1) * seq_len, :] = z_ref[s]


def _hyena(u, conv_w, bias, kf, f_tab, n_seq, seq_len, tok_offset):
    d = D_MODEL
    cb = 256
    n_cb = d // cb
    seqs = max(1, HY_STEP_ROWS // seq_len)
    rows = seqs * seq_len
    row0 = tok_offset // rows
    kc = min(HY_FREQ_CHUNK, seq_len)
    n_k = seq_len // kc
    f_bf = f_tab.astype(BF16)
    ft = f_bf.T.reshape(seq_len, 2 * n_k, kc).transpose(1, 0, 2)

    def part(k):
        return pl.BlockSpec((rows, cb), lambda j, b: (row0 + b, k * n_cb + j))

    def cw(k):
        return pl.BlockSpec((3, cb), lambda j, b: (0, k * n_cb + j))

    return pl.pallas_call(
        functools.partial(_hyena_kernel, seq_len, seqs),
        grid=(n_cb, n_seq // seqs),
        in_specs=[part(0), part(1), part(2), cw(0), cw(1), cw(2),
                  pl.BlockSpec((HY_ORDER, cb), lambda j, b: (0, j)),
                  pl.BlockSpec((HY_ORDER, 2 * seq_len, cb), lambda j, b: (0, 0, j)),
                  pl.BlockSpec((2 * seq_len, seq_len), lambda j, b: (0, 0)),
                  pl.BlockSpec((2 * n_k, seq_len, kc), lambda j, b: (0, 0, 0))],
        out_specs=pl.BlockSpec((rows, cb), lambda j, b: (b, j)),
        out_shape=jax.ShapeDtypeStruct((n_seq * seq_len, d), F32),
        scratch_shapes=[pltpu.VMEM((seqs, seq_len, cb), F32), pltpu.VMEM((seqs, seq_len, cb), BF16),
                        pltpu.VMEM((seqs, seq_len, cb), F32)],
        compiler_params=_cparams("parallel", "parallel"),
        name=f"hyena_l{seq_len}",
    )(u, u, u, conv_w, conv_w, conv_w, bias, kf, f_bf, ft)


def _mix_out_kernel(n_parts, *refs):
    o_refs = refs[:2 * n_parts]
    w_refs = refs[2 * n_parts:3 * n_parts]
    x_ref, mod_ref, g_ref, wr_hi_ref, wr_lo_ref, br_ref = refs[3 * n_parts:3 * n_parts + 6]
    x1_ref, h_ref, eid_ref, wts_ref, rank_ref, cnt_ref, run_ref = refs[3 * n_parts + 6:]
    i = pl.program_id(0)
    t = TOK_TILE
    ne = N_EXPERTS

    from_prompt = i < N_PROMPT // TOK_TILE
    y = None
    for j in range(n_parts):
        o = jnp.where(from_prompt, o_refs[2 * j][...], o_refs[2 * j + 1][...])
        yj = _dot(o.astype(BF16), w_refs[j][...])
        y = yj if y is None else y + yj
    x1 = x_ref[...] + mod_ref[2:3, :] * y
    x1_ref[...] = x1
    h = _norm_mod(x1, g_ref[...], mod_ref[4:5, :], mod_ref[3:4, :])
    h_ref[...] = _pack_bf16_halves(h)
    h_hi = h.astype(BF16)
    h_lo = (h - h_hi.astype(F32)).astype(BF16)
    logits = (_dot(h_hi, wr_hi_ref[...]) + _dot(h_hi, wr_lo_ref[...]) + _dot(h_lo, wr_hi_ref[...])).T[:ne] \
        + br_ref[...]

    @pl.when(i == 0)
    def _():
        run_ref[...] = jnp.zeros_like(run_ref)

    e_iota = lax.broadcasted_iota(jnp.int32, (ne, t), 0)
    vals, eids, onehots = [], [], []
    for _k in range(TOP_K):
        m = jnp.max(logits, axis=0, keepdims=True)
        eid = jnp.min(jnp.where(logits == m, e_iota, ne), axis=0, keepdims=True)
        sel = e_iota == eid
        logits = jnp.where(sel, -jnp.inf, logits)
        onehots.append(jnp.where(sel, 1.0, 0.0))
        vals.append(m)
        eids.append(eid)
    r2 = lax.broadcasted_iota(jnp.int32, (t, t), 0)
    c2 = lax.broadcasted_iota(jnp.int32, (t, t), 1)
    before = jnp.where(r2 < c2, 1.0, 0.0).astype(BF16)
    earlier = _dot(jnp.concatenate(onehots, axis=0).astype(BF16), before)
    running = run_ref[...]
    ranks = []
    for k, onehot in enumerate(onehots):
        ranks.append(jnp.sum(onehot * (running + earlier[k * ne:(k + 1) * ne]), axis=0, keepdims=True))
        running = running + jnp.sum(onehot, axis=1, keepdims=True)
    run_ref[...] = running
    cnt_ref[...] = running
    v = jnp.concatenate(vals, axis=0)
    ex = jnp.exp(v - v[0:1])
    wts_ref[...] = ex / jnp.sum(ex, axis=0, keepdims=True)
    eid_ref[...] = jnp.concatenate(eids, axis=0)
    rank_ref[...] = jnp.concatenate(ranks, axis=0).astype(jnp.int32)


def _mix_out(parts, x, mod_l, norm_g, w_router, b_router):
    d = D_MODEL
    t = TOK_TILE
    ne = N_EXPERTS
    n_parts = len(parts)
    n_prompt_tiles = N_PROMPT // t
    wr = jnp.pad(w_router, ((0, 0), (0, LANES - ne)))
    wr_hi = wr.astype(BF16)
    wr_lo = (wr - wr_hi.astype(F32)).astype(BF16)
    in_specs = []
    for o_p, _, _ in parts:
        in_specs.append(pl.BlockSpec((t, o_p.shape[1]), lambda i: (jnp.minimum(i, n_prompt_tiles - 1), 0)))
        in_specs.append(pl.BlockSpec((t, o_p.shape[1]), lambda i: (jnp.maximum(i - n_prompt_tiles, 0), 0)))
    in_specs += [pl.BlockSpec(w.shape, lambda i: (0, 0)) for _, _, w in parts]
    in_specs += [
        pl.BlockSpec((t, d), lambda i: (i, 0)),
        pl.BlockSpec((None, N_MOD, d), lambda i: (_tile_cond_row(i), 0, 0)),
        pl.BlockSpec((1, d), lambda i: (0, 0)),
        pl.BlockSpec((d, LANES), lambda i: (0, 0)),
        pl.BlockSpec((d, LANES), lambda i: (0, 0)),
        pl.BlockSpec((ne, 1), lambda i: (0, 0)),
    ]
    tok_major = pl.BlockSpec((TOP_K, t), lambda i: (0, i))
    return pl.pallas_call(
        functools.partial(_mix_out_kernel, n_parts),
        grid=(N_TOK // t,),
        in_specs=in_specs,
        out_specs=[pl.BlockSpec((t, d), lambda i: (i, 0)), pl.BlockSpec((t, d // 2), lambda i: (i, 0)),
                   tok_major, tok_major, tok_major, pl.BlockSpec((ne, 1), lambda i: (0, 0))],
        out_shape=[jax.ShapeDtypeStruct((N_TOK, d), F32), jax.ShapeDtypeStruct((N_TOK, d // 2), jnp.uint32),
                   jax.ShapeDtypeStruct((TOP_K, N_TOK), jnp.int32), jax.ShapeDtypeStruct((TOP_K, N_TOK), F32),
                   jax.ShapeDtypeStruct((TOP_K, N_TOK), jnp.int32), jax.ShapeDtypeStruct((ne, 1), F32)],
        scratch_shapes=[pltpu.VMEM((ne, 1), F32)],
        compiler_params=_cparams("arbitrary"),
        name="mix_out_router",
    )(*[o for part in parts for o in part[:2]], *[w.astype(BF16) for _, _, w in parts], x, mod_l,
      norm_g.reshape(1, d), wr_hi, wr_lo, b_router.reshape(ne, 1))


def _sc_row_gather(src, idx):
    n = idx.shape[0]
    width = src.shape[1]
    step_rows = min(SC_MAX_INDICES, SC_STEP_BYTES // (width * 4))
    sc = plsc.get_sparse_core_info()
    n_workers = sc.num_cores * sc.num_subcores
    per_worker = n // n_workers
    if n % n_workers or per_worker % step_rows:
        raise ValueError("row count must be whole SparseCore steps on every subcore")
    mesh = plsc.VectorSubcoreMesh(core_axis_name="core", subcore_axis_name="subcore")

    @functools.partial(pl.kernel, out_type=jax.ShapeDtypeStruct((n, width), src.dtype), mesh=mesh,
                       scratch_types=[pltpu.VMEM((per_worker,), jnp.int32),
                                      pltpu.VMEM((step_rows, width), src.dtype)],
                       name="sc_row_gather")
    def gather(src_hbm, idx_hbm, dst_hbm, idx_vmem, rows_vmem):
        worker = lax.axis_index("subcore") * sc.num_cores + lax.axis_index("core")
        base = worker * per_worker
        pltpu.sync_copy(idx_hbm.at[pl.ds(base, per_worker)], idx_vmem)

        @pl.loop(0, per_worker // step_rows)
        def _(c):
            pltpu.sync_copy(src_hbm.at[idx_vmem.at[pl.ds(c * step_rows, step_rows)]], rows_vmem)
            pltpu.sync_copy(rows_vmem, dst_hbm.at[pl.ds(base + c * step_rows, step_rows)])

    return gather(src, idx)


def _pack_bf16_halves(x):
    w = x.shape[1] // 2
    bits = pltpu.bitcast(x.astype(BF16).astype(F32), jnp.uint32)
    return bits[:, :w] | (bits[:, w:] >> 16)


def _unpack_bf16_halves(p):
    hi = pltpu.bitcast(p & jnp.uint32(0xFFFF0000), F32).astype(BF16)
    lo = pltpu.bitcast(p << 16, F32).astype(BF16)
    return hi, lo


def _experts_kernel(layer, te_ref, first_ref, slot_ref, next_ref, nv_ref,
                    x_ref, wgu_hbm, bg_ref, bu_ref, wd_hbm, bd_ref, sel_ref,
                    y_ref, wgu_buf, wd_buf, wg_ref, wu_ref, wdb_ref, sem):
    i = pl.program_id(0)
    valid = i < nv_ref[0]
    half = DEINT_COLS // 2
    k_half = x_ref.shape[1]

    def fetch(expert, slot):
        return (pltpu.make_async_copy(wgu_hbm.at[layer, expert], wgu_buf.at[slot], sem.at[slot, 0]),
                pltpu.make_async_copy(wd_hbm.at[layer, expert], wd_buf.at[slot], sem.at[slot, 1]))

    @pl.when(i == 0)
    def _():
        for cp in fetch(te_ref[0], 0):
            cp.start()

    @pl.when(jnp.logical_and(valid, first_ref[i] == 1))
    def _():
        slot = slot_ref[i]
        for cp in fetch(te_ref[i], slot):
            cp.wait()

        @pl.when(next_ref[i] >= 0)
        def _():
            for cp in fetch(next_ref[i], 1 - slot):
                cp.start()

        for c in range(wgu_buf.shape[2] // DEINT_COLS):
            w = wgu_buf[slot, :, c * DEINT_COLS:(c + 1) * DEINT_COLS].astype(BF16)
            split = _dot(w, sel_ref[...]).astype(BF16)
            wg_ref[:, c * half:(c + 1) * half] = split[:, :half]
            wu_ref[:, c * half:(c + 1) * half] = split[:, half:]
        wdb_ref[...] = wd_buf[slot].astype(BF16)

    @pl.when(valid)
    def _():
        x_l, x_r = _unpack_bf16_halves(x_ref[...])
        gl = _dot(x_l, wg_ref[:k_half, :]) + _dot(x_r, wg_ref[k_half:, :]) + bg_ref[...]
        up = _dot(x_l, wu_ref[:k_half, :]) + _dot(x_r, wu_ref[k_half:, :]) + bu_ref[...]
        gl = jnp.minimum(gl, SWIGLU_LIMIT)
        up = jnp.clip(up, -SWIGLU_LIMIT, SWIGLU_LIMIT)
        act = (up + 1.0) * gl * _sigmoid(SWIGLU_ALPHA * gl)
        y_ref[...] = _pack_bf16_halves(_dot(act.astype(BF16), wdb_ref[...]) + bd_ref[...])

    @pl.when(jnp.logical_not(valid))
    def _():
        y_ref[...] = jnp.zeros_like(y_ref)


def _experts(xs, tile_expert, tile_first, n_valid, layer, w_gu, b_gate, b_up, w_down, b_down):
    d = D_MODEL
    tm = MOE_TILE
    n_tiles = MOE_ROWS // tm
    ff = w_down.shape[2]
    half = DEINT_COLS // 2
    r = jnp.arange(DEINT_COLS)[:, None]
    c = jnp.arange(DEINT_COLS)[None, :]
    sel = (r == jnp.where(c < half, 2 * c, 2 * (c - half) + 1)).astype(BF16)
    group = jnp.cumsum(tile_first) - 1
    tile_slot = (group % 2).astype(jnp.int32)
    is_last_group = group == group[-1]
    following = jnp.concatenate([tile_expert[1:], tile_expert[-1:]])
    idx = jnp.arange(n_tiles, dtype=jnp.int32)
    group_end = jnp.max(jnp.where(group[None, :] == group[:, None], idx[None, :], -1), axis=1)
    tile_next = jnp.where(is_last_group, -1, following[group_end]).astype(jnp.int32)
    wspec = lambda k, n: pl.BlockSpec((None, None, k, n), lambda i, *_: (layer, _[0][i], 0, 0))
    grid_spec = pltpu.PrefetchScalarGridSpec(
        num_scalar_prefetch=5,
        grid=(n_tiles,),
        in_specs=[pl.BlockSpec((tm, d // 2), lambda i, *_: (i, 0)),
                  pl.BlockSpec(memory_space=pl.ANY), wspec(1, ff), wspec(1, ff),
                  pl.BlockSpec(memory_space=pl.ANY), wspec(1, d),
                  pl.BlockSpec((DEINT_COLS, DEINT_COLS), lambda i, *_: (0, 0))],
        out_specs=pl.BlockSpec((tm, d // 2), lambda i, *_: (i, 0)),
        scratch_shapes=[pltpu.VMEM((2, d, 2 * ff), F32), pltpu.VMEM((2, ff, d), F32),
                        pltpu.VMEM((d, ff), BF16), pltpu.VMEM((d, ff), BF16), pltpu.VMEM((ff, d), BF16),
                        pltpu.SemaphoreType.DMA((2, 2))],
    )
    return pl.pallas_call(
        functools.partial(_experts_kernel, layer),
        grid_spec=grid_spec,
        out_shape=jax.ShapeDtypeStruct((MOE_ROWS, d // 2), jnp.uint32),
        compiler_params=_cparams("arbitrary"),
        name="experts",
    )(tile_expert, tile_first, tile_slot, tile_next, n_valid, xs, w_gu, b_gate, b_up, w_down, b_down, sel)


def _combine_kernel(final, x_ref, g_ref, w_ref, mod_ref, fg_ref, *o_refs):
    y_l = y_r = None
    for k in range(TOP_K):
        g_l, g_r = _unpack_bf16_halves(g_ref[k])
        wk = w_ref[:, k:k + 1]
        y_l = wk * g_l.astype(F32) if y_l is None else y_l + wk * g_l.astype(F32)
        y_r = wk * g_r.astype(F32) if y_r is None else y_r + wk * g_r.astype(F32)
    x = x_ref[...] + mod_ref[5:6, :] * jnp.concatenate([y_l, y_r], axis=1)
    if not final:
        o_refs[0][...] = x
        return
    x = x * lax.rsqrt(jnp.mean(x * x, axis=-1, keepdims=True) + RMS_EPS) * fg_ref[...]
    from_prompt = pl.program_id(0) < N_PROMPT // TOK_TILE

    @pl.when(from_prompt)
    def _():
        o_refs[0][...] = x

    @pl.when(jnp.logical_not(from_prompt))
    def _():
        o_refs[1][...] = x


def _combine(x1, gathered, wts, mod_l, final_g, final):
    d = D_MODEL
    t = TOK_TILE
    n_prompt_tiles = N_PROMPT // t
    if final:
        out_specs = [pl.BlockSpec((t, d), lambda i: (jnp.minimum(i, n_prompt_tiles - 1), 0)),
                     pl.BlockSpec((t, d), lambda i: (jnp.maximum(i - n_prompt_tiles, 0), 0))]
        out_shape = [jax.ShapeDtypeStruct((N_PROMPT, d), F32), jax.ShapeDtypeStruct((N_SAMPLE, d), F32)]
    else:
        out_specs = pl.BlockSpec((t, d), lambda i: (i, 0))
        out_shape = jax.ShapeDtypeStruct((N_TOK, d), F32)
    return pl.pallas_call(
        functools.partial(_combine_kernel, final),
        grid=(N_TOK // t,),
        in_specs=[pl.BlockSpec((t, d), lambda i: (i, 0)),
                  pl.BlockSpec((TOP_K, t, d // 2), lambda i: (0, i, 0)),
                  pl.BlockSpec((t, TOP_K), lambda i: (i, 0)),
                  pl.BlockSpec((None, N_MOD, d), lambda i: (_tile_cond_row(i), 0, 0)),
                  pl.BlockSpec((1, d), lambda i: (0, 0))],
        out_specs=out_specs,
        out_shape=out_shape,
        compiler_params=_cparams("arbitrary"),
        name="moe_combine",
    )(x1, gathered, wts, mod_l, final_g.reshape(1, d))


def _moe(x1, h, eid, wts, rank, counts, mod_l, layer, w_gu, b_gu, w_down, b_down, final_g, final):
    d = D_MODEL
    tm = MOE_TILE
    n_tiles = MOE_ROWS // tm
    cnt = counts.reshape(N_EXPERTS).astype(jnp.int32)
    gsz = ((cnt + tm - 1) // tm) * tm
    ends = jnp.cumsum(gsz)
    offs = ends - gsz
    e_ids = jnp.arange(N_EXPERTS, dtype=jnp.int32)
    pos = jnp.sum(jnp.where(eid[..., None] == e_ids, offs, 0), axis=-1) + rank
    tile_start = jnp.arange(n_tiles, dtype=jnp.int32) * tm
    tile_expert = jnp.minimum(jnp.sum((ends[None, :] <= tile_start[:, None]).astype(jnp.int32), axis=1),
                              N_EXPERTS - 1)
    n_valid = (ends[-1:] // tm).astype(jnp.int32)
    last_valid = jnp.maximum(n_valid[0] - 1, 0)
    tile_expert = jnp.where(jnp.arange(n_tiles) < n_valid[0], tile_expert, tile_expert[last_valid])
    tile_first = jnp.concatenate([jnp.ones((1,), jnp.int32),
                                  (tile_expert[1:] != tile_expert[:-1]).astype(jnp.int32)])
    j = jnp.arange(tm, dtype=jnp.int32)[None, :]
    pad_used = j < (gsz - cnt)[:, None]
    n_unused_before = jnp.cumsum((~pad_used).reshape(-1).astype(jnp.int32)) - 1
    pad_pos = jnp.where(pad_used, (offs + cnt)[:, None] + j,
                        ends[-1] + n_unused_before.reshape(N_EXPERTS, tm))
    row_of_slot = jnp.concatenate([pos.reshape(-1), pad_pos.reshape(-1)])
    token_of_slot = jnp.arange(row_of_slot.shape[0], dtype=jnp.int32) % N_TOK
    token_of_row = jnp.sort(row_of_slot * N_TOK + token_of_slot) % N_TOK

    xs = _sc_row_gather(h, token_of_row)
    ys = _experts(xs, tile_expert, tile_first, n_valid, layer, w_gu,
                  b_gu[:, :, None, 0::2], b_gu[:, :, None, 1::2], w_down, b_down[:, :, None, :])
    gathered = _sc_row_gather(ys, pos.reshape(-1))
    return _combine(x1, gathered.reshape(TOP_K, N_TOK, d // 2), wts.T, mod_l, final_g, final)


def _grid_positions(n_tok, d):
    rows = n_tok // GRID_W
    r, col = jnp.meshgrid(jnp.arange(rows, dtype=F32), jnp.arange(GRID_W, dtype=F32), indexing='ij')
    r = r.reshape(-1)
    col = col.reshape(-1)
    quarter = d // 4
    inv = 1.0 / (10000.0 ** (jnp.arange(quarter, dtype=F32) / quarter))
    ar = r[:, None] * inv[None]
    ac = col[:, None] * inv[None]
    return jnp.concatenate([jnp.sin(ar), jnp.cos(ar), jnp.sin(ac), jnp.cos(ac)], axis=-1)


def kernel(x_prompt, x_sample, state_hgrn, state_mlstm_c, state_mlstm_n, state_mlstm_m, c, c_ctx,
           norm_g, final_g, w_mod, b_mod, ev_w_in, ev_gate_b, ev_conv, hg_lb, ev_w_out,
           hy_w_in, hy_conv, hy_w1, hy_b1, hy_w2, hy_b2, hy_w3, hy_b3, hy_freq, hy_log_rate, hy_bias, hy_w_out,
           w_router, b_router, w_gu, b_gu, w_down, b_down):
    d = D_MODEL
    hd = HEAD_DIM
    cond = jnp.concatenate([c_ctx[None], c, jnp.zeros((N_COND - 1 - DEC_BATCH, d), F32)], axis=0)
    mod = _modulation(cond, w_mod, b_mod)
    pos_tab = jnp.concatenate([jnp.zeros((TOK_TILE, d), F32), _grid_positions(DEC_SEQ, d)], axis=0)

    groups = ((BATCH, SEQ, 0), (DEC_BATCH, DEC_SEQ, N_PROMPT))
    new_states = None
    for l in range(DEPTH):
        if l % 2 == 0:
            e = l // 2
            if l == 0:
                x, p, gate_t = _proj_even(x_prompt.reshape(N_PROMPT, d), x_sample.reshape(N_SAMPLE, d),
                                          pos_tab, mod[l], norm_g[l, 0], ev_w_in[e], ev_gate_b[e])
            else:
                raise NotImplementedError("only the first layer adds grid positions")
            gate_t_h = gate_t.reshape(4, HEADS, N_TOK).transpose(1, 0, 2)
            o_hg, o_ml = [], []
            for gi, (n_seq, seq_len, off) in enumerate(groups):
                if gi == 0:
                    s0 = jnp.zeros((n_seq, 2, HEADS, hd, hd), F32)
                    c0 = jnp.zeros((n_seq, 2, HEADS, hd, hd), F32)
                    n0 = jnp.zeros((n_seq, 2, HEADS, 1, hd), F32)
                    m0 = jnp.zeros((n_seq, 2, HEADS, 1, 1), F32)
                else:
                    s0 = state_hgrn[:, e]
                    c0 = state_mlstm_c[:, e]
                    n0 = state_mlstm_n[:, e].reshape(n_seq, 2, HEADS, 1, hd)
                    m0 = state_mlstm_m[:, e].reshape(n_seq, 2, HEADS, 1, 1)
                og, s_fin = _hgrn(p, hg_lb, l, s0, n_seq, seq_len, off)
                om, c_fin, n_fin, m_fin = _mlstm(p, gate_t_h, ev_conv[e], c0, n0, m0, n_seq, seq_len, off)
                o_hg.append(og)
                o_ml.append(om)
                if gi == 0:
                    new_states = (s_fin[:, None], c_fin[:, None],
                                  n_fin.reshape(n_seq, 1, 2, HEADS, hd), m_fin.reshape(n_seq, 1, 2, HEADS))
            parts = [(*o_hg, ev_w_out[e][:GROUP_W]), (*o_ml, ev_w_out[e][GROUP_W:])]
        else:
            o = l // 2
            u = _proj_odd(x, mod[l], norm_g[l, 0], hy_w_in[o])
            zs = []
            for n_seq, seq_len, off in groups:
                f_tab = _dft_tables(seq_len)
                kf = _hyena_filters(seq_len, hy_w1[o], hy_b1[o], hy_w2[o], hy_b2[o], hy_w3[o], hy_b3[o],
                                    hy_freq[o], hy_log_rate[o], f_tab)
                zs.append(_hyena(u, hy_conv[o], hy_bias[o], kf, f_tab, n_seq, seq_len, off))
            parts = [(*zs, hy_w_out[o])]
        x1, h, eid, wts, rank, counts = _mix_out(parts, x, mod[l], norm_g[l, 1], w_router[l], b_router[l])
        x = _moe(x1, h, eid, wts, rank, counts, mod[l], l, w_gu, b_gu, w_down, b_down,
                 final_g, final=(l == DEPTH - 1))

    y_prompt, y_sample = x
    return (y_prompt.reshape(BATCH, SEQ, d), y_sample.reshape(DEC_BATCH, DEC_SEQ, d)) + new_states
```

```python
import functools

import numpy as np
import jax
import jax.numpy as jnp
from jax import lax
from jax.experimental import pallas as pl
from jax.experimental.pallas import tpu as pltpu
from jax.experimental.pallas import tpu_sc as plsc

F32 = jnp.float32
BF16 = jnp.bfloat16

D_MODEL = 1024
BATCH = 32
SEQ = 256
DEPTH = 2
DEC_BATCH = 8
DEC_SEQ = 1024
GRID_W = 64
RMS_EPS = 1e-6
N_MOD = 6
LOG2_E = 1.4426950408889634
LANES = 128

HEADS = 4
HEAD_DIM = 128
GROUP_W = HEADS * HEAD_DIM
N_GATES = 4 * HEADS
EVEN_MAIN = 9 * GROUP_W

HY_ORDER = 2
HY_BANDS = 16
HY_EMB = 1 + 2 * HY_BANDS
HY_HIDDEN = 64
HY_FREQ_CHUNK = 512
HY_STEP_ROWS = 1024

N_EXPERTS = 32
TOP_K = 4
SWIGLU_LIMIT = 7.0
SWIGLU_ALPHA = 1.702

N_PROMPT = BATCH * SEQ
N_SAMPLE = DEC_BATCH * DEC_SEQ
N_TOK = N_PROMPT + N_SAMPLE
N_COND = 16

TOK_TILE = 256
SCAN_CHUNK = 128
SUB = 16
HALF = 8
MOE_TILE = 256
MOE_ROWS = N_TOK * TOP_K + N_EXPERTS * MOE_TILE
DEINT_COLS = 256
SC_STEP_BYTES = 256 * 1024
SC_MAX_INDICES = 128

VMEM_LIMIT = 56 * 1024 * 1024


def _cparams(*sem):
    return pltpu.CompilerParams(dimension_semantics=sem, vmem_limit_bytes=VMEM_LIMIT)


def _split3(x):
    hi = x.astype(BF16)
    r = x - hi.astype(F32)
    mid = r.astype(BF16)
    lo = (r - mid.astype(F32)).astype(BF16)
    return hi, mid, lo


def _dot(a, b):
    return jnp.dot(a, b, preferred_element_type=F32)


def _dot_nt(a, b):
    return lax.dot_general(a, b, (((1,), (1,)), ((), ())), preferred_element_type=F32)


def _dot_tn(a, b):
    return lax.dot_general(a, b, (((0,), (0,)), ((), ())), preferred_element_type=F32)


def _dot_w3(a_exact_bf16, x):
    hi, mid, lo = _split3(x)
    return _dot(a_exact_bf16, hi) + _dot(a_exact_bf16, mid) + _dot(a_exact_bf16, lo)


def _sigmoid(x):
    return 1.0 / (1.0 + jnp.exp(-x))


def _silu(x):
    return x * _sigmoid(x)


def _log_sigmoid(x):
    return jnp.minimum(x, 0.0) - jnp.log(1.0 + jnp.exp(-jnp.abs(x)))


def _tile_cond_row(i):
    n_prompt_tiles = N_PROMPT // TOK_TILE
    tiles_per_seq = DEC_SEQ // TOK_TILE
    return jnp.where(i < n_prompt_tiles, 0, 1 + (i - n_prompt_tiles) // tiles_per_seq)


def _mod_kernel(cond_ref, w_ref, b_ref, o_ref):
    a = _silu(cond_ref[...]).astype(BF16)
    o_ref[...] = _dot(a, w_ref[...].astype(BF16)) + b_ref[...]


def _modulation(cond, w_mod, b_mod):
    d = D_MODEL
    out = pl.pallas_call(
        _mod_kernel,
        grid=(DEPTH, N_MOD),
        in_specs=[
            pl.BlockSpec((N_COND, d), lambda l, j: (0, 0)),
            pl.BlockSpec((None, d, d), lambda l, j: (l, 0, j)),
            pl.BlockSpec((None, 1, d), lambda l, j: (l, 0, j)),
        ],
        out_specs=pl.BlockSpec((None, None, N_COND, d), lambda l, j: (l, j, 0, 0)),
        out_shape=jax.ShapeDtypeStruct((DEPTH, N_MOD, N_COND, d), F32),
        compiler_params=_cparams("parallel", "parallel"),
        name="modulation",
    )(cond, w_mod, b_mod.reshape(DEPTH, 1, N_MOD * d))
    return out.transpose(0, 2, 1, 3)


def _norm_mod(x, g_row, scale_row, shift_row):
    ms = jnp.mean(x * x, axis=-1, keepdims=True)
    y = x * lax.rsqrt(ms + RMS_EPS) * g_row
    return y * (1.0 + scale_row) + shift_row


def _proj_even_kernel(xp_ref, xs_ref, pos_ref, mod_ref, g_ref, w_ref, wgt_ref, gbt_ref,
                      xres_ref, p_ref, gate_t_ref):
    from_prompt = pl.program_id(0) < N_PROMPT // TOK_TILE
    x = jnp.where(from_prompt, xp_ref[...], xs_ref[...]) + pos_ref[...]
    xres_ref[...] = x
    h = _norm_mod(x, g_ref[...], mod_ref[1:2, :], mod_ref[0:1, :]).astype(BF16)
    p_ref[...] = _dot(h, w_ref[...])
    gate_t_ref[...] = _dot_nt(wgt_ref[...], h) + gbt_ref[...]


def _proj_even(x_prompt, x_sample, pos_tab, mod_l, norm_g, w_in, gate_b):
    d = D_MODEL
    n_tiles = N_TOK // TOK_TILE
    n_prompt_tiles = N_PROMPT // TOK_TILE
    tiles_per_seq = DEC_SEQ // TOK_TILE
    w_main = w_in[:, :EVEN_MAIN].astype(BF16)
    w_gate = w_in[:, EVEN_MAIN:].astype(BF16)

    def pos_map(i):
        return (jnp.where(i < n_prompt_tiles, 0, 1 + (i - n_prompt_tiles) % tiles_per_seq), 0)

    return pl.pallas_call(
        _proj_even_kernel,
        grid=(n_tiles,),
        in_specs=[
            pl.BlockSpec((TOK_TILE, d), lambda i: (jnp.minimum(i, n_prompt_tiles - 1), 0)),
            pl.BlockSpec((TOK_TILE, d), lambda i: (jnp.maximum(i - n_prompt_tiles, 0), 0)),
            pl.BlockSpec((TOK_TILE, d), pos_map),
            pl.BlockSpec((None, N_MOD, d), lambda i: (_tile_cond_row(i), 0, 0)),
            pl.BlockSpec((1, d), lambda i: (0, 0)),
            pl.BlockSpec((d, EVEN_MAIN), lambda i: (0, 0)),
            pl.BlockSpec((N_GATES, d), lambda i: (0, 0)),
            pl.BlockSpec((N_GATES, 1), lambda i: (0, 0)),
        ],
        out_specs=[
            pl.BlockSpec((TOK_TILE, d), lambda i: (i, 0)),
            pl.BlockSpec((TOK_TILE, EVEN_MAIN), lambda i: (i, 0)),
            pl.BlockSpec((N_GATES, TOK_TILE), lambda i: (0, i)),
        ],
        out_shape=[
            jax.ShapeDtypeStruct((N_TOK, d), F32),
            jax.ShapeDtypeStruct((N_TOK, EVEN_MAIN), F32),
            jax.ShapeDtypeStruct((N_GATES, N_TOK), F32),
        ],
        compiler_params=_cparams("parallel"),
        name="proj_even",
    )(x_prompt, x_sample, pos_tab, mod_l, norm_g.reshape(1, d), w_main, w_gate.T, gate_b.reshape(N_GATES, 1))


def _proj_odd_kernel(x_ref, mod_ref, g_ref, w_ref, p_ref):
    h = _norm_mod(x_ref[...], g_ref[...], mod_ref[1:2, :], mod_ref[0:1, :]).astype(BF16)
    p_ref[...] = _dot(h, w_ref[...])


def _proj_odd(x, mod_l, norm_g, w_in):
    d = D_MODEL
    width = w_in.shape[1]
    return pl.pallas_call(
        _proj_odd_kernel,
        grid=(N_TOK // TOK_TILE,),
        in_specs=[
            pl.BlockSpec((TOK_TILE, d), lambda i: (i, 0)),
            pl.BlockSpec((None, N_MOD, d), lambda i: (_tile_cond_row(i), 0, 0)),
            pl.BlockSpec((1, d), lambda i: (0, 0)),
            pl.BlockSpec((d, width), lambda i: (0, 0)),
        ],
        out_specs=pl.BlockSpec((TOK_TILE, width), lambda i: (i, 0)),
        out_shape=jax.ShapeDtypeStruct((N_TOK, width), F32),
        compiler_params=_cparams("parallel"),
        name="proj_odd",
    )(x, mod_l, norm_g.reshape(1, d), w_in.astype(BF16))


def _hgrn_chunks(chains):
    c = SCAN_CHUNK
    n = range(len(chains))
    rev, q, k, v, lf, st = zip(*chains)
    row = lax.broadcasted_iota(jnp.int32, (c, c), 0)
    col = lax.broadcasted_iota(jnp.int32, (c, c), 1)
    tris = {False: col <= row, True: col >= row}
    tris_b = {r: jnp.where(t, 1.0, 0.0).astype(BF16) for r, t in tris.items()}
    b = [_dot_w3(tris_b[rev[j]], lf[j]) for j in n]
    b2 = [b[j] * LOG2_E for j in n]
    lane_half = lax.broadcasted_iota(jnp.int32, (HALF, c), 1)
    chunk_row = lax.broadcasted_iota(jnp.int32, (c, 1), 0)
    rows = [[] for _ in n]
    for i in range(c // SUB):
        lo, hi = i * SUB, (i + 1) * SUB
        a_row = []
        for j in n:
            if rev[j]:
                has_off, edge, outside = hi < c, hi, chunk_row >= hi
            else:
                has_off, edge, outside = lo > 0, lo - 1, chunk_row < lo
            if has_off:
                beta = b[j][edge:edge + 1]
                qs = q[j][lo:hi] * jnp.exp(b[j][lo:hi] - beta)
                ks = k[j] * jnp.exp(jnp.where(outside, beta - b[j], -jnp.inf))
                a_row.append(_dot_nt(qs.astype(BF16), ks.astype(BF16)))
            else:
                a_row.append(jnp.zeros((SUB, c), F32))
        for half in range(SUB // HALF):
            h0 = lo + half * HALF
            piece = [a_row[j][half * HALF:(half + 1) * HALF] for j in n]
            for s in range(HALF):
                for j in n:
                    bh2 = b2[j][h0:h0 + HALF]
                    a_col = jnp.sum(jnp.exp2(bh2 - bh2[s:s + 1]) * q[j][h0:h0 + HALF] * k[j][h0 + s:h0 + s + 1],
                                    axis=-1, keepdims=True)
                    piece[j] = jnp.where(lane_half == h0 + s, a_col, piece[j])
            for j in n:
                rows[j].append(piece[j])
    second_half = (chunk_row % SUB) >= HALF
    same_block = (row // SUB) == (col // SUB)
    out = []
    for j in n:
        meet = HALF if rev[j] else HALF - 1
        beta = jnp.concatenate([jnp.broadcast_to(b[j][lo + meet:lo + meet + 1], (SUB, b[j].shape[1]))
                                for lo in range(0, c, SUB)], axis=0)
        t_side = jnp.logical_not(second_half) if rev[j] else second_half
        qs = q[j] * jnp.exp(jnp.where(t_side, b[j] - beta, -jnp.inf))
        ks = k[j] * jnp.exp(jnp.where(t_side, -jnp.inf, beta - b[j]))
        cross = jnp.where(same_block, _dot_nt(qs.astype(BF16), ks.astype(BF16)), 0.0)
        attn = jnp.where(tris[rev[j]], jnp.concatenate(rows[j], axis=0) + cross, 0.0)
        o = _dot(attn.astype(BF16), v[j].astype(BF16)) \
            + _dot_nt((q[j] * jnp.exp(b[j])).astype(BF16), st[j].astype(BF16))
        b_exit = b[j][0:1] if rev[j] else b[j][c - 1:c]
        k_out = k[j] * jnp.exp(b_exit - b[j])
        st_new = jnp.exp(b_exit) * st[j] + _dot_tn(v[j].astype(BF16), k_out.astype(BF16))
        out.append((o, st_new))
    return out


def _hgrn_kernel(seq_len, layer, q_ref, i_ref, g_ref, ff_ref, fb_ref, lb_ref, s0_ref,
                 o_ref, s_out_ref, of_ref, ob_ref, st_ref):
    c = SCAN_CHUNK
    n_chunks = seq_len // c
    lbp = lb_ref[...]
    e = jnp.exp(lbp - jnp.max(lbp, axis=0, keepdims=True))
    lb = jnp.sum(e[0:layer + 1], axis=0, keepdims=True) / jnp.sum(e, axis=0, keepdims=True)

    st_ref[0] = s0_ref[0].T
    st_ref[1] = s0_ref[1].T

    def body(n, carry):
        slices, chains = [], []
        for d in range(2):
            sl = pl.ds(pl.multiple_of((n_chunks - 1 - n if d else n) * c, c), c)
            f = lb + (1.0 - lb) * _sigmoid((fb_ref if d else ff_ref)[sl, :])
            slices.append(sl)
            chains.append((bool(d), q_ref[sl, :], 1.0 - f, i_ref[sl, :], jnp.log(f), st_ref[d]))
        for d, (sl, (o, st_new)) in enumerate(zip(slices, _hgrn_chunks(chains))):
            st_ref[d] = st_new
            (ob_ref if d else of_ref)[sl, :] = o
        return carry

    lax.fori_loop(0, n_chunks, body, 0)
    o = of_ref[...] + ob_ref[...]
    o = o * lax.rsqrt(jnp.mean(o * o, axis=-1, keepdims=True) + RMS_EPS)
    o_ref[...] = o * _silu(g_ref[...])
    s_out_ref[0] = st_ref[0].T
    s_out_ref[1] = st_ref[1].T


def _hgrn(p, hg_lb, layer, s0, n_seq, seq_len, tok_offset):
    hd = HEAD_DIM
    row0 = tok_offset // seq_len

    def col(part):
        return pl.BlockSpec((seq_len, hd), lambda b, h: (row0 + b, part * HEADS + h))

    state_spec = pl.BlockSpec((None, 2, None, hd, hd), lambda b, h: (b, 0, h, 0, 0))
    return pl.pallas_call(
        functools.partial(_hgrn_kernel, seq_len, layer),
        grid=(n_seq, HEADS),
        in_specs=[col(0), col(1), col(2), col(3), col(4),
                  pl.BlockSpec((DEPTH + 1, hd), lambda b, h: (0, h)),
                  state_spec],
        out_specs=[pl.BlockSpec((seq_len, hd), lambda b, h: (b, h)), state_spec],
        out_shape=[jax.ShapeDtypeStruct((n_seq * seq_len, GROUP_W), F32),
                   jax.ShapeDtypeStruct((n_seq, 2, HEADS, hd, hd), F32)],
        scratch_shapes=[pltpu.VMEM((seq_len, hd), F32), pltpu.VMEM((seq_len, hd), F32),
                        pltpu.VMEM((2, hd, hd), F32)],
        compiler_params=_cparams("parallel", "parallel"),
        name=f"hgrn_l{seq_len}",
    )(p, p, p, p, p, hg_lb, s0)


def _short_conv3(x, w):
    n = x.shape[0]
    r = lax.broadcasted_iota(jnp.int32, (n, 1), 0)
    prev = jnp.where(r == 0, 0.0, pltpu.roll(x, 1, 0))
    nxt = jnp.where(r == n - 1, 0.0, pltpu.roll(x, n - 1, 0))
    return prev * w[0:1] + x * w[1:2] + nxt * w[2:3]


def _split2(x):
    hi = x.astype(BF16)
    return hi, (x - hi.astype(F32)).astype(BF16)


def _mlstm_chunks(chains):
    c = SCAN_CHUNK
    row = lax.broadcasted_iota(jnp.int32, (c, c), 0)
    col = lax.broadcasted_iota(jnp.int32, (c, c), 1)
    eye_b = jnp.where(row == col, 1.0, 0.0).astype(BF16)
    tris = {False: row <= col, True: row >= col}
    tris_b = {r: jnp.where(t, 1.0, 0.0).astype(BF16) for r, t in tris.items()}
    n = range(len(chains))
    rev, q, k, vt, ig, fg, ct, nv, m_prev = zip(*chains)

    def each(fn):
        return [fn(i) for i in n]

    def dot3(parts, rhs, nt=False):
        d = _dot_nt if nt else _dot
        return d(parts[0], rhs) + d(parts[1], rhs) + d(parts[2], rhs)

    lf = each(lambda i: _split3(jnp.broadcast_to(_log_sigmoid(fg[i]), (8, c))))
    b = each(lambda i: dot3(lf[i], tris_b[rev[i]])[0:1])
    us = each(lambda i: _split3(jnp.broadcast_to(ig[i] - b[i], (c, c))))
    u = each(lambda i: _dot_nt(eye_b, us[i][0]) + _dot_nt(eye_b, us[i][1]) + _dot_nt(eye_b, us[i][2]))
    dmat = each(lambda i: jnp.where(tris[rev[i]], b[i] + u[i], -jnp.inf))
    m_t = each(lambda i: jnp.maximum(b[i] + m_prev[i], jnp.max(dmat[i], axis=0, keepdims=True)))
    qb = each(lambda i: q[i].astype(BF16))
    kb = each(lambda i: k[i].astype(BF16))
    kq = each(lambda i: _dot_nt(kb[i], qb[i]))
    p = each(lambda i: jnp.exp(dmat[i] - m_t[i]) * kq[i])
    inter = each(lambda i: jnp.exp(b[i] + m_prev[i] - m_t[i]))
    ns = each(lambda i: _split2(jnp.broadcast_to(nv[i], (8, nv[i].shape[1]))))
    qs = each(lambda i: _split2(q[i]))
    qn = each(lambda i: (_dot_nt(ns[i][0], qs[i][0]) + _dot_nt(ns[i][1], qs[i][0])
                         + _dot_nt(ns[i][0], qs[i][1]))[0:1])
    den = each(lambda i: inter[i] * qn[i] + jnp.sum(p[i], axis=0, keepdims=True))
    scale = each(lambda i: 1.0 / jnp.maximum(jnp.abs(den[i]), jnp.exp(-m_t[i])))
    cq = each(lambda i: _dot_nt(ct[i].astype(BF16), qb[i]))
    vp = each(lambda i: _dot(vt[i].astype(BF16), p[i].astype(BF16)))
    ht = each(lambda i: (inter[i] * cq[i] + vp[i]) * scale[i])
    last = each(lambda i: 0 if rev[i] else c - 1)
    m_new = each(lambda i: m_t[i][:, last[i]:last[i] + 1])
    b_exit = each(lambda i: b[i][:, last[i]:last[i] + 1])
    w = each(lambda i: jnp.exp(b_exit[i] - b[i] + ig[i] - m_new[i]))
    dec = each(lambda i: jnp.exp(b_exit[i] + m_prev[i] - m_new[i]))
    vk = each(lambda i: _dot((vt[i] * w[i]).astype(BF16), kb[i]))
    ct_new = each(lambda i: dec[i] * ct[i] + vk[i])
    ws = each(lambda i: _split2(jnp.broadcast_to(w[i], (8, c))))
    ks = each(lambda i: _split2(k[i]))
    wk = each(lambda i: (_dot(ws[i][0], ks[i][0]) + _dot(ws[i][1], ks[i][0]) + _dot(ws[i][0], ks[i][1]))[0:1])
    nv_new = each(lambda i: dec[i] * nv[i] + wk[i])
    return [(ht[i], ct_new[i], nv_new[i], m_new[i]) for i in n]


def _mlstm_kernel(seq_len, q_ref, k_ref, v_ref, og_ref, gate_t_ref, cwq_ref, cwk_ref,
                  c0_ref, n0_ref, m0_ref,
                  o_ref, c_out_ref, n_out_ref, m_out_ref,
                  q2_ref, k2_ref, vt_ref, hf_ref, hb_ref, ct_ref, n_ref, m_ref):
    c = SCAN_CHUNK
    hd = HEAD_DIM
    n_chunks = seq_len // c
    q2_ref[...] = _silu(_short_conv3(q_ref[...], cwq_ref[...]))
    k2_ref[...] = _silu(_short_conv3(k_ref[...], cwk_ref[...])) * (HEAD_DIM ** -0.5)
    for hh in range(HEADS):
        cols = slice(hh * hd, (hh + 1) * hd)
        for j in range(n_chunks):
            vt_ref[hh, :, j * c:(j + 1) * c] = v_ref[j * c:(j + 1) * c, cols].T
        for d in range(2):
            ct_ref[d, hh] = c0_ref[d, hh].T
    n_ref[...] = n0_ref[...]
    m_ref[...] = m0_ref[...]

    def body(n, carry):
        where, chains = [], []
        for d in range(2):
            sl = pl.ds(pl.multiple_of((n_chunks - 1 - n if d else n) * c, c), c)
            for hh in range(HEADS):
                cols = slice(hh * hd, (hh + 1) * hd)
                gr = gate_t_ref[hh, :, sl]
                where.append((d, hh, sl))
                chains.append((bool(d), q2_ref[sl, cols], k2_ref[sl, cols], vt_ref[hh, :, sl],
                               gr[2 * d:2 * d + 1, :], gr[2 * d + 1:2 * d + 2, :],
                               ct_ref[d, hh], n_ref[d, hh], m_ref[d, hh]))
        for (d, hh, sl), (ht, ct, nv, m_new) in zip(where, _mlstm_chunks(chains)):
            ct_ref[d, hh] = ct
            n_ref[d, hh] = nv
            m_ref[d, hh] = m_new
            (hb_ref if d else hf_ref)[hh, :, sl] = ht
        return carry

    lax.fori_loop(0, n_chunks, body, 0)
    for hh in range(HEADS):
        cols = slice(hh * hd, (hh + 1) * hd)
        for j in range(n_chunks):
            rows = slice(j * c, (j + 1) * c)
            h = (hf_ref[hh, :, rows] + hb_ref[hh, :, rows]).T
            h = h * lax.rsqrt(jnp.mean(h * h, axis=-1, keepdims=True) + RMS_EPS)
            o_ref[rows, cols] = h * _sigmoid(og_ref[rows, cols])
        for d in range(2):
            c_out_ref[d, hh] = ct_ref[d, hh].T
    n_out_ref[...] = n_ref[...]
    m_out_ref[...] = m_ref[...]


def _mlstm(p, gate_t_h, conv_w, c0, n0, m0, n_seq, seq_len, tok_offset):
    hd = HEAD_DIM
    gw = GROUP_W
    row0 = tok_offset // seq_len

    def col(part):
        return pl.BlockSpec((seq_len, gw), lambda b: (row0 + b, part))

    c_spec = pl.BlockSpec((None, 2, HEADS, hd, hd), lambda b: (b, 0, 0, 0, 0))
    n_spec = pl.BlockSpec((None, 2, HEADS, 1, hd), lambda b: (b, 0, 0, 0, 0))
    m_spec = pl.BlockSpec((None, 2, HEADS, 1, 1), lambda b: (b, 0, 0, 0, 0))
    return pl.pallas_call(
        functools.partial(_mlstm_kernel, seq_len),
        grid=(n_seq,),
        in_specs=[col(5), col(6), col(7), col(8),
                  pl.BlockSpec((HEADS, 4, seq_len), lambda b: (0, 0, row0 + b)),
                  pl.BlockSpec((3, gw), lambda b: (0, 0)),
                  pl.BlockSpec((3, gw), lambda b: (0, 1)),
                  c_spec, n_spec, m_spec],
        out_specs=[pl.BlockSpec((seq_len, gw), lambda b: (b, 0)), c_spec, n_spec, m_spec],
        out_shape=[jax.ShapeDtypeStruct((n_seq * seq_len, gw), F32),
                   jax.ShapeDtypeStruct((n_seq, 2, HEADS, hd, hd), F32),
                   jax.ShapeDtypeStruct((n_seq, 2, HEADS, 1, hd), F32),
                   jax.ShapeDtypeStruct((n_seq, 2, HEADS, 1, 1), F32)],
        scratch_shapes=[pltpu.VMEM((seq_len, gw), F32), pltpu.VMEM((seq_len, gw), F32),
                        pltpu.VMEM((HEADS, hd, seq_len), F32),
                        pltpu.VMEM((HEADS, hd, seq_len), F32), pltpu.VMEM((HEADS, hd, seq_len), F32),
                        pltpu.VMEM((2, HEADS, hd, hd), F32), pltpu.VMEM((2, HEADS, 1, hd), F32),
                        pltpu.VMEM((2, HEADS, 1, 1), F32)],
        compiler_params=_cparams("parallel"),
        name=f"mlstm_l{seq_len}",
    )(p, p, p, p, gate_t_h, conv_w, conv_w, c0, n0, m0)


def _dft_tables(seq_len):
    n = 2 * seq_len
    k = jnp.arange(seq_len, dtype=jnp.int32)[:, None]
    t = jnp.arange(seq_len, dtype=jnp.int32)[None, :]
    ang = ((k * t) % n).astype(F32) * (2.0 * np.pi / n)
    fc = jnp.cos(ang)
    fs = jnp.sin(ang)
    nyq = jnp.where(t % 2 == 0, 1.0, -1.0).astype(F32)
    fs = jnp.where(k == 0, nyq, fs)
    return jnp.concatenate([fc, fs], axis=0)


def _filter_kernel(seq_len, z_ref, w1_ref, b1_ref, w2_ref, b2_ref, w3f_ref, w3b_ref, b3f_ref, b3b_ref,
                   f0_ref, f1_ref, rf_ref, rb_ref, fhi_ref, flo_ref, kf_ref, a_ref):
    hp = lax.Precision.HIGHEST
    n = 2 * seq_len
    z = z_ref[...]

    @pl.when(jnp.logical_and(pl.program_id(0) == 0, pl.program_id(1) == 0))
    def _():
        a1 = jnp.sin(f0_ref[...] * (jnp.dot(z, w1_ref[...], precision=hp, preferred_element_type=F32)
                                    + b1_ref[...]))
        a_ref[...] = jnp.sin(f1_ref[...] * (jnp.dot(a1, w2_ref[...], precision=hp, preferred_element_type=F32)
                                            + b2_ref[...]))

    a = a_ref[...]
    t_norm = z[:, 0:1]
    hf = (jnp.dot(a, w3f_ref[...], precision=hp, preferred_element_type=F32) + b3f_ref[...]) \
        * jnp.exp(-t_norm * jnp.exp(rf_ref[...]))
    hb = (jnp.dot(a, w3b_ref[...], precision=hp, preferred_element_type=F32) + b3b_ref[...]) \
        * jnp.exp(-t_norm * jnp.exp(rb_ref[...]))
    inv = lax.rsqrt(jnp.sum(hf * hf, axis=0, keepdims=True) + jnp.sum(hb * hb, axis=0, keepdims=True))
    hf = hf * inv
    r = lax.broadcasted_iota(jnp.int32, (seq_len, 1), 0)
    hb = jnp.where(r == 0, 0.0, hb * inv)
    sh, sm, sl = _split3(hf + hb)
    dh, dm, dl = _split3(hf - hb)
    fhi = fhi_ref[...]
    flo = flo_ref[...]
    kc = _dot(fhi[:seq_len], sh) + _dot(fhi[:seq_len], sm) + _dot(fhi[:seq_len], sl) \
        + _dot(flo[:seq_len], sh) + _dot(flo[:seq_len], sm)
    ks = _dot(fhi[seq_len:], dh) + _dot(fhi[seq_len:], dm) + _dot(fhi[seq_len:], dl) \
        + _dot(flo[seq_len:], dh) + _dot(flo[seq_len:], dm)
    sign = jnp.where(r % 2 == 0, 1.0, -1.0)
    k_nyq = jnp.sum(sign * (hf + hb), axis=0, keepdims=True)
    ks = jnp.where(r == 0, k_nyq, ks)
    scale = jnp.where(r == 0, 1.0 / n, 2.0 / n)
    kf_ref[0:seq_len, :] = kc * scale
    kf_ref[seq_len:n, :] = ks * scale


def _hyena_filters(seq_len, w1, b1, w2, b2, w3, b3, freq, log_rate, f_tab):
    d = D_MODEL
    cb = 256
    t = jnp.arange(seq_len, dtype=F32)
    t_norm = t / (seq_len - 1)
    bands = jnp.linspace(1e-4, HY_BANDS - 1, HY_BANDS, dtype=F32)
    ang = (2.0 * np.pi / seq_len) * t[:, None] * bands[None, :]
    z = jnp.concatenate([t_norm[:, None], jnp.cos(ang), jnp.sin(ang)], axis=-1)
    kpad = 128 - HY_EMB
    z = jnp.pad(z, ((0, 0), (0, kpad)))
    w1p = jnp.pad(w1, ((0, kpad), (0, 0)))
    f_hi = f_tab.astype(BF16)
    f_lo = (f_tab - f_hi.astype(F32)).astype(BF16)
    n_cb = d // cb
    hh = HY_HIDDEN
    row = lambda a: a.reshape(1, -1)
    const = lambda shape: pl.BlockSpec(shape, lambda o, j: (0,) * len(shape))
    fwd = lambda rows: pl.BlockSpec((rows, cb), lambda o, j: (0, o * n_cb + j))
    bwd = lambda rows: pl.BlockSpec((rows, cb), lambda o, j: (0, (HY_ORDER + o) * n_cb + j))
    return pl.pallas_call(
        functools.partial(_filter_kernel, seq_len),
        grid=(HY_ORDER, n_cb),
        in_specs=[const((seq_len, 128)), const((128, hh)), const((1, hh)), const((hh, hh)), const((1, hh)),
                  fwd(hh), bwd(hh), fwd(1), bwd(1),
                  const((1, hh)), const((1, hh)), fwd(1), bwd(1),
                  const((2 * seq_len, seq_len)), const((2 * seq_len, seq_len))],
        out_specs=pl.BlockSpec((None, 2 * seq_len, cb), lambda o, j: (o, 0, j)),
        out_shape=jax.ShapeDtypeStruct((HY_ORDER, 2 * seq_len, d), F32),
        scratch_shapes=[pltpu.VMEM((seq_len, hh), F32)],
        compiler_params=_cparams("arbitrary", "arbitrary"),
        name=f"hyena_filter_l{seq_len}",
    )(z, w1p, row(b1), w2, row(b2), w3, w3, row(b3), row(b3),
      row(freq[0]), row(freq[1]), row(log_rate), row(log_rate), f_hi, f_lo)


def _hyena_kernel(seq_len, seqs, v_ref, x1_ref, x2_ref, cwv_ref, cw1_ref, cw2_ref, bias_ref, kf_ref,
                  f_ref, ft_ref, o_ref, z_ref, zb_ref, y_ref):
    kc = min(HY_FREQ_CHUNK, seq_len)
    n_k = seq_len // kc
    r = lax.broadcasted_iota(jnp.int32, (kc, 1), 0)
    gate_refs = ((x1_ref, cw1_ref), (x2_ref, cw2_ref))
    for s in range(seqs):
        rows = slice(s * seq_len, (s + 1) * seq_len)
        z_ref[s] = _short_conv3(v_ref[rows, :], cwv_ref[...])
    for o in range(HY_ORDER):
        for s in range(seqs):
            zb_ref[s] = z_ref[s].astype(BF16)
            y_ref[s] = jnp.zeros(y_ref.shape[1:], F32)

        def freq_chunk(j, carry):
            r0 = pl.multiple_of(j * kc, kc)
            k_cos = kf_ref[o, pl.ds(r0, kc), :]
            k_sin = kf_ref[o, pl.ds(seq_len + r0, kc), :]
            real_row = jnp.logical_and(r == 0, j == 0)
            for s in range(seqs):
                a = _dot(f_ref[pl.ds(r0, kc), :], zb_ref[s])
                bm = _dot(f_ref[pl.ds(seq_len + r0, kc), :], zb_ref[s])
                yc = a * k_cos - jnp.where(real_row, 0.0, bm * k_sin)
                ys = jnp.where(real_row, bm * k_sin, a * k_sin + bm * k_cos)
                y_ref[s] += _dot(ft_ref[j], yc.astype(BF16)) + _dot(ft_ref[n_k + j], ys.astype(BF16))
            return carry

        lax.fori_loop(0, n_k, freq_chunk, 0)
        x_ref, cw_ref = gate_refs[o]
        for s in range(seqs):
            rows = slice(s * seq_len, (s + 1) * seq_len)
            gate = _short_conv3(x_ref[rows, :], cw_ref[...])
            z_ref[s] = gate * (y_ref[s] + z_ref[s] * bias_ref[o:o + 1, :])
    for s in range(seqs):
        o_ref[s * seq_len:(s + 1) * seq_len, :] = z_ref[s]


def _hyena(u, conv_w, bias, kf, f_tab, n_seq, seq_len, tok_offset):
    d = D_MODEL
    cb = 256
    n_cb = d // cb
    seqs = max(1, HY_STEP_ROWS // seq_len)
    rows = seqs * seq_len
    row0 = tok_offset // rows
    kc = min(HY_FREQ_CHUNK, seq_len)
    n_k = seq_len // kc
    f_bf = f_tab.astype(BF16)
    ft = f_bf.T.reshape(seq_len, 2 * n_k, kc).transpose(1, 0, 2)

    def part(k):
        return pl.BlockSpec((rows, cb), lambda j, b: (row0 + b, k * n_cb + j))

    def cw(k):
        return pl.BlockSpec((3, cb), lambda j, b: (0, k * n_cb + j))

    return pl.pallas_call(
        functools.partial(_hyena_kernel, seq_len, seqs),
        grid=(n_cb, n_seq // seqs),
        in_specs=[part(0), part(1), part(2), cw(0), cw(1), cw(2),
                  pl.BlockSpec((HY_ORDER, cb), lambda j, b: (0, j)),
                  pl.BlockSpec((HY_ORDER, 2 * seq_len, cb), lambda j, b: (0, 0, j)),
                  pl.BlockSpec((2 * seq_len, seq_len), lambda j, b: (0, 0)),
                  pl.BlockSpec((2 * n_k, seq_len, kc), lambda j, b: (0, 0, 0))],
        out_specs=pl.BlockSpec((rows, cb), lambda j, b: (b, j)),
        out_shape=jax.ShapeDtypeStruct((n_seq * seq_len, d), F32),
        scratch_shapes=[pltpu.VMEM((seqs, seq_len, cb), F32), pltpu.VMEM((seqs, seq_len, cb), BF16),
                        pltpu.VMEM((seqs, seq_len, cb), F32)],
        compiler_params=_cparams("parallel", "parallel"),
        name=f"hyena_l{seq_len}",
    )(u, u, u, conv_w, conv_w, conv_w, bias, kf, f_bf, ft)


def _mix_out_kernel(n_parts, *refs):
    o_refs = refs[:2 * n_parts]
    w_refs = refs[2 * n_parts:3 * n_parts]
    x_ref, mod_ref, g_ref, wr_hi_ref, wr_lo_ref, br_ref = refs[3 * n_parts:3 * n_parts + 6]
    x1_ref, h_ref, eid_ref, wts_ref, rank_ref, cnt_ref, run_ref = refs[3 * n_parts + 6:]
    i = pl.program_id(0)
    t = TOK_TILE
    ne = N_EXPERTS

    from_prompt = i < N_PROMPT // TOK_TILE
    y = None
    for j in range(n_parts):
        o = jnp.where(from_prompt, o_refs[2 * j][...], o_refs[2 * j + 1][...])
        yj = _dot(o.astype(BF16), w_refs[j][...])
        y = yj if y is None else y + yj
    x1 = x_ref[...] + mod_ref[2:3, :] * y
    x1_ref[...] = x1
    h = _norm_mod(x1, g_ref[...], mod_ref[4:5, :], mod_ref[3:4, :])
    h_ref[...] = _pack_bf16_halves(h)
    h_hi = h.astype(BF16)
    h_lo = (h - h_hi.astype(F32)).astype(BF16)
    logits = (_dot(h_hi, wr_hi_ref[...]) + _dot(h_hi, wr_lo_ref[...]) + _dot(h_lo, wr_hi_ref[...])).T[:ne] \
        + br_ref[...]

    @pl.when(i == 0)
    def _():
        run_ref[...] = jnp.zeros_like(run_ref)

    e_iota = lax.broadcasted_iota(jnp.int32, (ne, t), 0)
    vals, eids, onehots = [], [], []
    for _k in range(TOP_K):
        m = jnp.max(logits, axis=0, keepdims=True)
        eid = jnp.min(jnp.where(logits == m, e_iota, ne), axis=0, keepdims=True)
        sel = e_iota == eid
        logits = jnp.where(sel, -jnp.inf, logits)
        onehots.append(jnp.where(sel, 1.0, 0.0))
        vals.append(m)
        eids.append(eid)
    r2 = lax.broadcasted_iota(jnp.int32, (t, t), 0)
    c2 = lax.broadcasted_iota(jnp.int32, (t, t), 1)
    before = jnp.where(r2 < c2, 1.0, 0.0).astype(BF16)
    earlier = _dot(jnp.concatenate(onehots, axis=0).astype(BF16), before)
    running = run_ref[...]
    ranks = []
    for k, onehot in enumerate(onehots):
        ranks.append(jnp.sum(onehot * (running + earlier[k * ne:(k + 1) * ne]), axis=0, keepdims=True))
        running = running + jnp.sum(onehot, axis=1, keepdims=True)
    run_ref[...] = running
    cnt_ref[...] = running
    v = jnp.concatenate(vals, axis=0)
    ex = jnp.exp(v - v[0:1])
    wts_ref[...] = ex / jnp.sum(ex, axis=0, keepdims=True)
    eid_ref[...] = jnp.concatenate(eids, axis=0)
    rank_ref[...] = jnp.concatenate(ranks, axis=0).astype(jnp.int32)


def _mix_out(parts, x, mod_l, norm_g, w_router, b_router):
    d = D_MODEL
    t = TOK_TILE
    ne = N_EXPERTS
    n_parts = len(parts)
    n_prompt_tiles = N_PROMPT // t
    wr = jnp.pad(w_router, ((0, 0), (0, LANES - ne)))
    wr_hi = wr.astype(BF16)
    wr_lo = (wr - wr_hi.astype(F32)).astype(BF16)
    in_specs = []
    for o_p, _, _ in parts:
        in_specs.append(pl.BlockSpec((t, o_p.shape[1]), lambda i: (jnp.minimum(i, n_prompt_tiles - 1), 0)))
        in_specs.append(pl.BlockSpec((t, o_p.shape[1]), lambda i: (jnp.maximum(i - n_prompt_tiles, 0), 0)))
    in_specs += [pl.BlockSpec(w.shape, lambda i: (0, 0)) for _, _, w in parts]
    in_specs += [
        pl.BlockSpec((t, d), lambda i: (i, 0)),
        pl.BlockSpec((None, N_MOD, d), lambda i: (_tile_cond_row(i), 0, 0)),
        pl.BlockSpec((1, d), lambda i: (0, 0)),
        pl.BlockSpec((d, LANES), lambda i: (0, 0)),
        pl.BlockSpec((d, LANES), lambda i: (0, 0)),
        pl.BlockSpec((ne, 1), lambda i: (0, 0)),
    ]
    tok_major = pl.BlockSpec((TOP_K, t), lambda i: (0, i))
    return pl.pallas_call(
        functools.partial(_mix_out_kernel, n_parts),
        grid=(N_TOK // t,),
        in_specs=in_specs,
        out_specs=[pl.BlockSpec((t, d), lambda i: (i, 0)), pl.BlockSpec((t, d // 2), lambda i: (i, 0)),
                   tok_major, tok_major, tok_major, pl.BlockSpec((ne, 1), lambda i: (0, 0))],
        out_shape=[jax.ShapeDtypeStruct((N_TOK, d), F32), jax.ShapeDtypeStruct((N_TOK, d // 2), jnp.uint32),
                   jax.ShapeDtypeStruct((TOP_K, N_TOK), jnp.int32), jax.ShapeDtypeStruct((TOP_K, N_TOK), F32),
                   jax.ShapeDtypeStruct((TOP_K, N_TOK), jnp.int32), jax.ShapeDtypeStruct((ne, 1), F32)],
        scratch_shapes=[pltpu.VMEM((ne, 1), F32)],
        compiler_params=_cparams("arbitrary"),
        name="mix_out_router",
    )(*[o for part in parts for o in part[:2]], *[w.astype(BF16) for _, _, w in parts], x, mod_l,
      norm_g.reshape(1, d), wr_hi, wr_lo, b_router.reshape(ne, 1))


def _sc_row_gather(src, idx):
    n = idx.shape[0]
    width = src.shape[1]
    step_rows = min(SC_MAX_INDICES, SC_STEP_BYTES // (width * 4))
    sc = plsc.get_sparse_core_info()
    n_workers = sc.num_cores * sc.num_subcores
    per_worker = n // n_workers
    if n % n_workers or per_worker % step_rows:
        raise ValueError("row count must be whole SparseCore steps on every subcore")
    mesh = plsc.VectorSubcoreMesh(core_axis_name="core", subcore_axis_name="subcore")

    @functools.partial(pl.kernel, out_type=jax.ShapeDtypeStruct((n, width), src.dtype), mesh=mesh,
                       scratch_types=[pltpu.VMEM((per_worker,), jnp.int32),
                                      pltpu.VMEM((step_rows, width), src.dtype)],
                       name="sc_row_gather")
    def gather(src_hbm, idx_hbm, dst_hbm, idx_vmem, rows_vmem):
        worker = lax.axis_index("subcore") * sc.num_cores + lax.axis_index("core")
        base = worker * per_worker
        pltpu.sync_copy(idx_hbm.at[pl.ds(base, per_worker)], idx_vmem)

        @pl.loop(0, per_worker // step_rows)
        def _(c):
            pltpu.sync_copy(src_hbm.at[idx_vmem.at[pl.ds(c * step_rows, step_rows)]], rows_vmem)
            pltpu.sync_copy(rows_vmem, dst_hbm.at[pl.ds(base + c * step_rows, step_rows)])

    return gather(src, idx)


def _sc_row_scatter(src, row_of_slot):
    n = row_of_slot.shape[0]
    n_src, width = src.shape
    step_rows = min(SC_MAX_INDICES, SC_STEP_BYTES // (width * 4))
    sc = plsc.get_sparse_core_info()
    n_workers = sc.num_cores * sc.num_subcores
    per_worker = n // n_workers
    steps = per_worker // step_rows
    if n % n_workers or per_worker % step_rows or n_src % step_rows:
        raise ValueError("row counts must be whole SparseCore steps on every subcore")
    mesh = plsc.VectorSubcoreMesh(core_axis_name="core", subcore_axis_name="subcore")

    @functools.partial(pl.kernel, out_type=jax.ShapeDtypeStruct((n, width), src.dtype), mesh=mesh,
                       scratch_types=[pltpu.VMEM((steps, step_rows), jnp.int32),
                                      pltpu.VMEM((step_rows, width), src.dtype)],
                       name="sc_row_scatter")
    def scatter(src_hbm, idx_hbm, dst_hbm, idx_vmem, rows_vmem):
        worker = lax.axis_index("subcore") * sc.num_cores + lax.axis_index("core")
        pltpu.sync_copy(idx_hbm.at[worker], idx_vmem)

        @pl.loop(0, steps)
        def _(c):
            src0 = lax.rem(worker * per_worker + c * step_rows, n_src)
            pltpu.sync_copy(src_hbm.at[pl.ds(src0, step_rows)], rows_vmem)
            pltpu.sync_copy(rows_vmem, dst_hbm.at[idx_vmem.at[c]])

    return scatter(src, row_of_slot.reshape(n_workers, steps, step_rows))


def _pack_bf16_halves(x):
    w = x.shape[1] // 2
    bits = pltpu.bitcast(x.astype(BF16).astype(F32), jnp.uint32)
    return bits[:, :w] | (bits[:, w:] >> 16)


def _unpack_bf16_halves(p):
    hi = pltpu.bitcast(p & jnp.uint32(0xFFFF0000), F32).astype(BF16)
    lo = pltpu.bitcast(p << 16, F32).astype(BF16)
    return hi, lo


def _experts_kernel(layer, te_ref, first_ref, slot_ref, next_ref, nv_ref,
                    x_ref, wgu_hbm, bg_ref, bu_ref, wd_hbm, bd_ref, sel_ref,
                    y_ref, wgu_buf, wd_buf, wg_ref, wu_ref, wdb_ref, sem):
    i = pl.program_id(0)
    valid = i < nv_ref[0]
    half = DEINT_COLS // 2
    k_half = x_ref.shape[1]

    def fetch(expert, slot):
        return (pltpu.make_async_copy(wgu_hbm.at[layer, expert], wgu_buf.at[slot], sem.at[slot, 0]),
                pltpu.make_async_copy(wd_hbm.at[layer, expert], wd_buf.at[slot], sem.at[slot, 1]))

    @pl.when(i == 0)
    def _():
        for cp in fetch(te_ref[0], 0):
            cp.start()

    @pl.when(jnp.logical_and(valid, first_ref[i] == 1))
    def _():
        slot = slot_ref[i]
        for cp in fetch(te_ref[i], slot):
            cp.wait()

        @pl.when(next_ref[i] >= 0)
        def _():
            for cp in fetch(next_ref[i], 1 - slot):
                cp.start()

        for c in range(wgu_buf.shape[2] // DEINT_COLS):
            w = wgu_buf[slot, :, c * DEINT_COLS:(c + 1) * DEINT_COLS].astype(BF16)
            split = _dot(w, sel_ref[...]).astype(BF16)
            wg_ref[:, c * half:(c + 1) * half] = split[:, :half]
            wu_ref[:, c * half:(c + 1) * half] = split[:, half:]
        wdb_ref[...] = wd_buf[slot].astype(BF16)

    @pl.when(valid)
    def _():
        x_l, x_r = _unpack_bf16_halves(x_ref[...])
        gl = _dot(x_l, wg_ref[:k_half, :]) + _dot(x_r, wg_ref[k_half:, :]) + bg_ref[...]
        up = _dot(x_l, wu_ref[:k_half, :]) + _dot(x_r, wu_ref[k_half:, :]) + bu_ref[...]
        gl = jnp.minimum(gl, SWIGLU_LIMIT)
        up = jnp.clip(up, -SWIGLU_LIMIT, SWIGLU_LIMIT)
        act = (up + 1.0) * gl * _sigmoid(SWIGLU_ALPHA * gl)
        y_ref[...] = _pack_bf16_halves(_dot(act.astype(BF16), wdb_ref[...]) + bd_ref[...])

    @pl.when(jnp.logical_not(valid))
    def _():
        y_ref[...] = jnp.zeros_like(y_ref)


def _experts(xs, tile_expert, tile_first, n_valid, layer, w_gu, b_gate, b_up, w_down, b_down):
    d = D_MODEL
    tm = MOE_TILE
    n_tiles = MOE_ROWS // tm
    ff = w_down.shape[2]
    half = DEINT_COLS // 2
    r = jnp.arange(DEINT_COLS)[:, None]
    c = jnp.arange(DEINT_COLS)[None, :]
    sel = (r == jnp.where(c < half, 2 * c, 2 * (c - half) + 1)).astype(BF16)
    group = jnp.cumsum(tile_first) - 1
    tile_slot = (group % 2).astype(jnp.int32)
    is_last_group = group == group[-1]
    following = jnp.concatenate([tile_expert[1:], tile_expert[-1:]])
    idx = jnp.arange(n_tiles, dtype=jnp.int32)
    group_end = jnp.max(jnp.where(group[None, :] == group[:, None], idx[None, :], -1), axis=1)
    tile_next = jnp.where(is_last_group, -1, following[group_end]).astype(jnp.int32)
    wspec = lambda k, n: pl.BlockSpec((None, None, k, n), lambda i, *_: (layer, _[0][i], 0, 0))
    grid_spec = pltpu.PrefetchScalarGridSpec(
        num_scalar_prefetch=5,
        grid=(n_tiles,),
        in_specs=[pl.BlockSpec((tm, d // 2), lambda i, *_: (i, 0)),
                  pl.BlockSpec(memory_space=pl.ANY), wspec(1, ff), wspec(1, ff),
                  pl.BlockSpec(memory_space=pl.ANY), wspec(1, d),
                  pl.BlockSpec((DEINT_COLS, DEINT_COLS), lambda i, *_: (0, 0))],
        out_specs=pl.BlockSpec((tm, d // 2), lambda i, *_: (i, 0)),
        scratch_shapes=[pltpu.VMEM((2, d, 2 * ff), F32), pltpu.VMEM((2, ff, d), F32),
                        pltpu.VMEM((d, ff), BF16), pltpu.VMEM((d, ff), BF16), pltpu.VMEM((ff, d), BF16),
                        pltpu.SemaphoreType.DMA((2, 2))],
    )
    return pl.pallas_call(
        functools.partial(_experts_kernel, layer),
        grid_spec=grid_spec,
        out_shape=jax.ShapeDtypeStruct((MOE_ROWS, d // 2), jnp.uint32),
        compiler_params=_cparams("arbitrary"),
        name="experts",
    )(tile_expert, tile_first, tile_slot, tile_next, n_valid, xs, w_gu, b_gate, b_up, w_down, b_down, sel)


def _combine_kernel(final, x_ref, g_ref, w_ref, mod_ref, fg_ref, *o_refs):
    y_l = y_r = None
    for k in range(TOP_K):
        g_l, g_r = _unpack_bf16_halves(g_ref[k])
        wk = w_ref[:, k:k + 1]
        y_l = wk * g_l.astype(F32) if y_l is None else y_l + wk * g_l.astype(F32)
        y_r = wk * g_r.astype(F32) if y_r is None else y_r + wk * g_r.astype(F32)
    x = x_ref[...] + mod_ref[5:6, :] * jnp.concatenate([y_l, y_r], axis=1)
    if not final:
        o_refs[0][...] = x
        return
    x = x * lax.rsqrt(jnp.mean(x * x, axis=-1, keepdims=True) + RMS_EPS) * fg_ref[...]
    from_prompt = pl.program_id(0) < N_PROMPT // TOK_TILE

    @pl.when(from_prompt)
    def _():
        o_refs[0][...] = x

    @pl.when(jnp.logical_not(from_prompt))
    def _():
        o_refs[1][...] = x


def _combine(x1, gathered, wts, mod_l, final_g, final):
    d = D_MODEL
    t = TOK_TILE
    n_prompt_tiles = N_PROMPT // t
    if final:
        out_specs = [pl.BlockSpec((t, d), lambda i: (jnp.minimum(i, n_prompt_tiles - 1), 0)),
                     pl.BlockSpec((t, d), lambda i: (jnp.maximum(i - n_prompt_tiles, 0), 0))]
        out_shape = [jax.ShapeDtypeStruct((N_PROMPT, d), F32), jax.ShapeDtypeStruct((N_SAMPLE, d), F32)]
    else:
        out_specs = pl.BlockSpec((t, d), lambda i: (i, 0))
        out_shape = jax.ShapeDtypeStruct((N_TOK, d), F32)
    return pl.pallas_call(
        functools.partial(_combine_kernel, final),
        grid=(N_TOK // t,),
        in_specs=[pl.BlockSpec((t, d), lambda i: (i, 0)),
                  pl.BlockSpec((TOP_K, t, d // 2), lambda i: (0, i, 0)),
                  pl.BlockSpec((t, TOP_K), lambda i: (i, 0)),
                  pl.BlockSpec((None, N_MOD, d), lambda i: (_tile_cond_row(i), 0, 0)),
                  pl.BlockSpec((1, d), lambda i: (0, 0))],
        out_specs=out_specs,
        out_shape=out_shape,
        compiler_params=_cparams("arbitrary"),
        name="moe_combine",
    )(x1, gathered, wts, mod_l, final_g.reshape(1, d))


def _moe(x1, h, eid, wts, rank, counts, mod_l, layer, w_gu, b_gu, w_down, b_down, final_g, final):
    d = D_MODEL
    tm = MOE_TILE
    n_tiles = MOE_ROWS // tm
    cnt = counts.reshape(N_EXPERTS).astype(jnp.int32)
    gsz = ((cnt + tm - 1) // tm) * tm
    ends = jnp.cumsum(gsz)
    offs = ends - gsz
    e_ids = jnp.arange(N_EXPERTS, dtype=jnp.int32)
    pos = jnp.sum(jnp.where(eid[..., None] == e_ids, offs, 0), axis=-1) + rank
    tile_start = jnp.arange(n_tiles, dtype=jnp.int32) * tm
    tile_expert = jnp.minimum(jnp.sum((ends[None, :] <= tile_start[:, None]).astype(jnp.int32), axis=1),
                              N_EXPERTS - 1)
    n_valid = (ends[-1:] // tm).astype(jnp.int32)
    last_valid = jnp.maximum(n_valid[0] - 1, 0)
    tile_expert = jnp.where(jnp.arange(n_tiles) < n_valid[0], tile_expert, tile_expert[last_valid])
    tile_first = jnp.concatenate([jnp.ones((1,), jnp.int32),
                                  (tile_expert[1:] != tile_expert[:-1]).astype(jnp.int32)])
    j = jnp.arange(tm, dtype=jnp.int32)[None, :]
    pad_used = j < (gsz - cnt)[:, None]
    n_unused_before = jnp.cumsum((~pad_used).reshape(-1).astype(jnp.int32)) - 1
    pad_pos = jnp.where(pad_used, (offs + cnt)[:, None] + j,
                        ends[-1] + n_unused_before.reshape(N_EXPERTS, tm))
    row_of_slot = jnp.concatenate([pos.reshape(-1), pad_pos.reshape(-1)])

    xs = _sc_row_scatter(h, row_of_slot)
    ys = _experts(xs, tile_expert, tile_first, n_valid, layer, w_gu,
                  b_gu[:, :, None, 0::2], b_gu[:, :, None, 1::2], w_down, b_down[:, :, None, :])
    gathered = _sc_row_gather(ys, pos.reshape(-1))
    return _combine(x1, gathered.reshape(TOP_K, N_TOK, d // 2), wts.T, mod_l, final_g, final)


def _grid_positions(n_tok, d):
    rows = n_tok // GRID_W
    r, col = jnp.meshgrid(jnp.arange(rows, dtype=F32), jnp.arange(GRID_W, dtype=F32), indexing='ij')
    r = r.reshape(-1)
    col = col.reshape(-1)
    quarter = d // 4
    inv = 1.0 / (10000.0 ** (jnp.arange(quarter, dtype=F32) / quarter))
    ar = r[:, None] * inv[None]
    ac = col[:, None] * inv[None]
    return jnp.concatenate([jnp.sin(ar), jnp.cos(ar), jnp.sin(ac), jnp.cos(ac)], axis=-1)


def kernel(x_prompt, x_sample, state_hgrn, state_mlstm_c, state_mlstm_n, state_mlstm_m, c, c_ctx,
           norm_g, final_g, w_mod, b_mod, ev_w_in, ev_gate_b, ev_conv, hg_lb, ev_w_out,
           hy_w_in, hy_conv, hy_w1, hy_b1, hy_w2, hy_b2, hy_w3, hy_b3, hy_freq, hy_log_rate, hy_bias, hy_w_out,
           w_router, b_router, w_gu, b_gu, w_down, b_down):
    d = D_MODEL
    hd = HEAD_DIM
    cond = jnp.concatenate([c_ctx[None], c, jnp.zeros((N_COND - 1 - DEC_BATCH, d), F32)], axis=0)
    mod = _modulation(cond, w_mod, b_mod)
    pos_tab = jnp.concatenate([jnp.zeros((TOK_TILE, d), F32), _grid_positions(DEC_SEQ, d)], axis=0)

    groups = ((BATCH, SEQ, 0), (DEC_BATCH, DEC_SEQ, N_PROMPT))
    new_states = None
    for l in range(DEPTH):
        if l % 2 == 0:
            e = l // 2
            if l == 0:
                x, p, gate_t = _proj_even(x_prompt.reshape(N_PROMPT, d), x_sample.reshape(N_SAMPLE, d),
                                          pos_tab, mod[l], norm_g[l, 0], ev_w_in[e], ev_gate_b[e])
            else:
                raise NotImplementedError("only the first layer adds grid positions")
            gate_t_h = gate_t.reshape(4, HEADS, N_TOK).transpose(1, 0, 2)
            o_hg, o_ml = [], []
            for gi, (n_seq, seq_len, off) in enumerate(groups):
                if gi == 0:
                    s0 = jnp.zeros((n_seq, 2, HEADS, hd, hd), F32)
                    c0 = jnp.zeros((n_seq, 2, HEADS, hd, hd), F32)
                    n0 = jnp.zeros((n_seq, 2, HEADS, 1, hd), F32)
                    m0 = jnp.zeros((n_seq, 2, HEADS, 1, 1), F32)
                else:
                    s0 = state_hgrn[:, e]
                    c0 = state_mlstm_c[:, e]
                    n0 = state_mlstm_n[:, e].reshape(n_seq, 2, HEADS, 1, hd)
                    m0 = state_mlstm_m[:, e].reshape(n_seq, 2, HEADS, 1, 1)
                og, s_fin = _hgrn(p, hg_lb, l, s0, n_seq, seq_len, off)
                om, c_fin, n_fin, m_fin = _mlstm(p, gate_t_h, ev_conv[e], c0, n0, m0, n_seq, seq_len, off)
                o_hg.append(og)
                o_ml.append(om)
                if gi == 0:
                    new_states = (s_fin[:, None], c_fin[:, None],
                                  n_fin.reshape(n_seq, 1, 2, HEADS, hd), m_fin.reshape(n_seq, 1, 2, HEADS))
            parts = [(*o_hg, ev_w_out[e][:GROUP_W]), (*o_ml, ev_w_out[e][GROUP_W:])]
        else:
            o = l // 2
            u = _proj_odd(x, mod[l], norm_g[l, 0], hy_w_in[o])
            zs = []
            for n_seq, seq_len, off in groups:
                f_tab = _dft_tables(seq_len)
                kf = _hyena_filters(seq_len, hy_w1[o], hy_b1[o], hy_w2[o], hy_b2[o], hy_w3[o], hy_b3[o],
                                    hy_freq[o], hy_log_rate[o], f_tab)
                zs.append(_hyena(u, hy_conv[o], hy_bias[o], kf, f_tab, n_seq, seq_len, off))
            parts = [(*zs, hy_w_out[o])]
        x1, h, eid, wts, rank, counts = _mix_out(parts, x, mod[l], norm_g[l, 1], w_router[l], b_router[l])
        x = _moe(x1, h, eid, wts, rank, counts, mod[l], l, w_gu, b_gu, w_down, b_down,
                 final_g, final=(l == DEPTH - 1))

    y_prompt, y_sample = x
    return (y_prompt.reshape(BATCH, SEQ, d), y_sample.reshape(DEC_BATCH, DEC_SEQ, d)) + new_states
```

```python
import functools

import numpy as np
import jax
import jax.numpy as jnp
from jax import lax
from jax.experimental import pallas as pl
from jax.experimental.pallas import tpu as pltpu
from jax.experimental.pallas import tpu_sc as plsc

F32 = jnp.float32
BF16 = jnp.bfloat16

D_MODEL = 1024
BATCH = 32
SEQ = 256
DEPTH = 2
DEC_BATCH = 8
DEC_SEQ = 1024
GRID_W = 64
RMS_EPS = 1e-6
N_MOD = 6
LOG2_E = 1.4426950408889634
LANES = 128

HEADS = 4
HEAD_DIM = 128
GROUP_W = HEADS * HEAD_DIM
N_GATES = 4 * HEADS
EVEN_MAIN = 9 * GROUP_W

HY_ORDER = 2
HY_BANDS = 16
HY_EMB = 1 + 2 * HY_BANDS
HY_HIDDEN = 64
HY_FREQ_CHUNK = 512
HY_STEP_ROWS = 1024

N_EXPERTS = 32
TOP_K = 4
SWIGLU_LIMIT = 7.0
SWIGLU_ALPHA = 1.702

N_PROMPT = BATCH * SEQ
N_SAMPLE = DEC_BATCH * DEC_SEQ
N_TOK = N_PROMPT + N_SAMPLE
N_COND = 16

TOK_TILE = 256
SCAN_CHUNK = 128
SUB = 16
HALF = 8
HG_STEP_HEADS = 2
MOE_TILE = 256
MOE_ROWS = N_TOK * TOP_K + N_EXPERTS * MOE_TILE
DEINT_COLS = 256
SC_STEP_BYTES = 256 * 1024
SC_MAX_INDICES = 128

VMEM_LIMIT = 56 * 1024 * 1024


def _cparams(*sem):
    return pltpu.CompilerParams(dimension_semantics=sem, vmem_limit_bytes=VMEM_LIMIT)


def _split3(x):
    hi = x.astype(BF16)
    r = x - hi.astype(F32)
    mid = r.astype(BF16)
    lo = (r - mid.astype(F32)).astype(BF16)
    return hi, mid, lo


def _dot(a, b):
    return jnp.dot(a, b, preferred_element_type=F32)


def _dot_nt(a, b):
    return lax.dot_general(a, b, (((1,), (1,)), ((), ())), preferred_element_type=F32)


def _dot_tn(a, b):
    return lax.dot_general(a, b, (((0,), (0,)), ((), ())), preferred_element_type=F32)


def _dot_w3(a_exact_bf16, x):
    hi, mid, lo = _split3(x)
    return _dot(a_exact_bf16, hi) + _dot(a_exact_bf16, mid) + _dot(a_exact_bf16, lo)


def _sigmoid(x):
    return 1.0 / (1.0 + jnp.exp(-x))


def _silu(x):
    return x * _sigmoid(x)


def _log_sigmoid(x):
    return jnp.minimum(x, 0.0) - jnp.log(1.0 + jnp.exp(-jnp.abs(x)))


def _tile_cond_row(i):
    n_prompt_tiles = N_PROMPT // TOK_TILE
    tiles_per_seq = DEC_SEQ // TOK_TILE
    return jnp.where(i < n_prompt_tiles, 0, 1 + (i - n_prompt_tiles) // tiles_per_seq)


def _mod_kernel(cond_ref, w_ref, b_ref, o_ref):
    a = _silu(cond_ref[...]).astype(BF16)
    o_ref[...] = _dot(a, w_ref[...].astype(BF16)) + b_ref[...]


def _modulation(cond, w_mod, b_mod):
    d = D_MODEL
    out = pl.pallas_call(
        _mod_kernel,
        grid=(DEPTH, N_MOD),
        in_specs=[
            pl.BlockSpec((N_COND, d), lambda l, j: (0, 0)),
            pl.BlockSpec((None, d, d), lambda l, j: (l, 0, j)),
            pl.BlockSpec((None, 1, d), lambda l, j: (l, 0, j)),
        ],
        out_specs=pl.BlockSpec((None, None, N_COND, d), lambda l, j: (l, j, 0, 0)),
        out_shape=jax.ShapeDtypeStruct((DEPTH, N_MOD, N_COND, d), F32),
        compiler_params=_cparams("parallel", "parallel"),
        name="modulation",
    )(cond, w_mod, b_mod.reshape(DEPTH, 1, N_MOD * d))
    return out.transpose(0, 2, 1, 3)


def _norm_mod(x, g_row, scale_row, shift_row):
    ms = jnp.mean(x * x, axis=-1, keepdims=True)
    y = x * lax.rsqrt(ms + RMS_EPS) * g_row
    return y * (1.0 + scale_row) + shift_row


def _proj_even_kernel(xp_ref, xs_ref, pos_ref, mod_ref, g_ref, w_ref, wgt_ref, gbt_ref,
                      xres_ref, p_ref, gate_t_ref):
    from_prompt = pl.program_id(0) < N_PROMPT // TOK_TILE
    x = jnp.where(from_prompt, xp_ref[...], xs_ref[...]) + pos_ref[...]
    xres_ref[...] = x
    h = _norm_mod(x, g_ref[...], mod_ref[1:2, :], mod_ref[0:1, :]).astype(BF16)
    p_ref[...] = _dot(h, w_ref[...])
    gate_t_ref[...] = _dot_nt(wgt_ref[...], h) + gbt_ref[...]


def _proj_even(x_prompt, x_sample, pos_tab, mod_l, norm_g, w_in, gate_b):
    d = D_MODEL
    n_tiles = N_TOK // TOK_TILE
    n_prompt_tiles = N_PROMPT // TOK_TILE
    tiles_per_seq = DEC_SEQ // TOK_TILE
    w_main = w_in[:, :EVEN_MAIN].astype(BF16)
    w_gate = w_in[:, EVEN_MAIN:].astype(BF16)

    def pos_map(i):
        return (jnp.where(i < n_prompt_tiles, 0, 1 + (i - n_prompt_tiles) % tiles_per_seq), 0)

    return pl.pallas_call(
        _proj_even_kernel,
        grid=(n_tiles,),
        in_specs=[
            pl.BlockSpec((TOK_TILE, d), lambda i: (jnp.minimum(i, n_prompt_tiles - 1), 0)),
            pl.BlockSpec((TOK_TILE, d), lambda i: (jnp.maximum(i - n_prompt_tiles, 0), 0)),
            pl.BlockSpec((TOK_TILE, d), pos_map),
            pl.BlockSpec((None, N_MOD, d), lambda i: (_tile_cond_row(i), 0, 0)),
            pl.BlockSpec((1, d), lambda i: (0, 0)),
            pl.BlockSpec((d, EVEN_MAIN), lambda i: (0, 0)),
            pl.BlockSpec((N_GATES, d), lambda i: (0, 0)),
            pl.BlockSpec((N_GATES, 1), lambda i: (0, 0)),
        ],
        out_specs=[
            pl.BlockSpec((TOK_TILE, d), lambda i: (i, 0)),
            pl.BlockSpec((TOK_TILE, EVEN_MAIN), lambda i: (i, 0)),
            pl.BlockSpec((N_GATES, TOK_TILE), lambda i: (0, i)),
        ],
        out_shape=[
            jax.ShapeDtypeStruct((N_TOK, d), F32),
            jax.ShapeDtypeStruct((N_TOK, EVEN_MAIN), F32),
            jax.ShapeDtypeStruct((N_GATES, N_TOK), F32),
        ],
        compiler_params=_cparams("parallel"),
        name="proj_even",
    )(x_prompt, x_sample, pos_tab, mod_l, norm_g.reshape(1, d), w_main, w_gate.T, gate_b.reshape(N_GATES, 1))


def _proj_odd_kernel(x_ref, mod_ref, g_ref, w_ref, p_ref):
    h = _norm_mod(x_ref[...], g_ref[...], mod_ref[1:2, :], mod_ref[0:1, :]).astype(BF16)
    p_ref[...] = _dot(h, w_ref[...])


def _proj_odd(x, mod_l, norm_g, w_in):
    d = D_MODEL
    width = w_in.shape[1]
    return pl.pallas_call(
        _proj_odd_kernel,
        grid=(N_TOK // TOK_TILE,),
        in_specs=[
            pl.BlockSpec((TOK_TILE, d), lambda i: (i, 0)),
            pl.BlockSpec((None, N_MOD, d), lambda i: (_tile_cond_row(i), 0, 0)),
            pl.BlockSpec((1, d), lambda i: (0, 0)),
            pl.BlockSpec((d, width), lambda i: (0, 0)),
        ],
        out_specs=pl.BlockSpec((TOK_TILE, width), lambda i: (i, 0)),
        out_shape=jax.ShapeDtypeStruct((N_TOK, width), F32),
        compiler_params=_cparams("parallel"),
        name="proj_odd",
    )(x, mod_l, norm_g.reshape(1, d), w_in.astype(BF16))


def _hgrn_chunks(chains):
    c = SCAN_CHUNK
    n = range(len(chains))
    rev, q, k, v, lf, st = zip(*chains)
    row = lax.broadcasted_iota(jnp.int32, (c, c), 0)
    col = lax.broadcasted_iota(jnp.int32, (c, c), 1)
    tris = {False: col <= row, True: col >= row}
    tris_b = {r: jnp.where(t, 1.0, 0.0).astype(BF16) for r, t in tris.items()}
    b = [_dot_w3(tris_b[rev[j]], lf[j]) for j in n]
    b2 = [b[j] * LOG2_E for j in n]
    lane_half = lax.broadcasted_iota(jnp.int32, (HALF, c), 1)
    chunk_row = lax.broadcasted_iota(jnp.int32, (c, 1), 0)
    rows = [[] for _ in n]
    for i in range(c // SUB):
        lo, hi = i * SUB, (i + 1) * SUB
        a_row = []
        for j in n:
            if rev[j]:
                has_off, edge, outside = hi < c, hi, chunk_row >= hi
            else:
                has_off, edge, outside = lo > 0, lo - 1, chunk_row < lo
            if has_off:
                beta = b[j][edge:edge + 1]
                qs = q[j][lo:hi] * jnp.exp(b[j][lo:hi] - beta)
                ks = k[j] * jnp.exp(jnp.where(outside, beta - b[j], -jnp.inf))
                a_row.append(_dot_nt(qs.astype(BF16), ks.astype(BF16)))
            else:
                a_row.append(jnp.zeros((SUB, c), F32))
        for half in range(SUB // HALF):
            h0 = lo + half * HALF
            piece = [a_row[j][half * HALF:(half + 1) * HALF] for j in n]
            for s in range(HALF):
                for j in n:
                    bh2 = b2[j][h0:h0 + HALF]
                    a_col = jnp.sum(jnp.exp2(bh2 - bh2[s:s + 1]) * q[j][h0:h0 + HALF] * k[j][h0 + s:h0 + s + 1],
                                    axis=-1, keepdims=True)
                    piece[j] = jnp.where(lane_half == h0 + s, a_col, piece[j])
            for j in n:
                rows[j].append(piece[j])
    second_half = (chunk_row % SUB) >= HALF
    same_block = (row // SUB) == (col // SUB)
    out = []
    for j in n:
        meet = HALF if rev[j] else HALF - 1
        beta = jnp.concatenate([jnp.broadcast_to(b[j][lo + meet:lo + meet + 1], (SUB, b[j].shape[1]))
                                for lo in range(0, c, SUB)], axis=0)
        t_side = jnp.logical_not(second_half) if rev[j] else second_half
        qs = q[j] * jnp.exp(jnp.where(t_side, b[j] - beta, -jnp.inf))
        ks = k[j] * jnp.exp(jnp.where(t_side, -jnp.inf, beta - b[j]))
        cross = jnp.where(same_block, _dot_nt(qs.astype(BF16), ks.astype(BF16)), 0.0)
        attn = jnp.where(tris[rev[j]], jnp.concatenate(rows[j], axis=0) + cross, 0.0)
        o = _dot(attn.astype(BF16), v[j].astype(BF16)) \
            + _dot_nt((q[j] * jnp.exp(b[j])).astype(BF16), st[j].astype(BF16))
        b_exit = b[j][0:1] if rev[j] else b[j][c - 1:c]
        k_out = k[j] * jnp.exp(b_exit - b[j])
        st_new = jnp.exp(b_exit) * st[j] + _dot_tn(v[j].astype(BF16), k_out.astype(BF16))
        out.append((o, st_new))
    return out


def _hgrn_kernel(seq_len, layer, q_ref, i_ref, g_ref, ff_ref, fb_ref, lb_ref, s0_ref,
                 o_ref, s_out_ref, of_ref, ob_ref, st_ref):
    c = SCAN_CHUNK
    hd = HEAD_DIM
    n_chunks = seq_len // c
    lbp = lb_ref[...]
    e = jnp.exp(lbp - jnp.max(lbp, axis=0, keepdims=True))
    lb = jnp.sum(e[0:layer + 1], axis=0, keepdims=True) / jnp.sum(e, axis=0, keepdims=True)

    for d in range(2):
        for hh in range(HG_STEP_HEADS):
            st_ref[d, hh] = s0_ref[d, hh].T

    def body(n, carry):
        where, chains = [], []
        for d in range(2):
            sl = pl.ds(pl.multiple_of((n_chunks - 1 - n if d else n) * c, c), c)
            for hh in range(HG_STEP_HEADS):
                cols = slice(hh * hd, (hh + 1) * hd)
                f = lb[:, cols] + (1.0 - lb[:, cols]) * _sigmoid((fb_ref if d else ff_ref)[sl, cols])
                where.append((d, hh, sl, cols))
                chains.append((bool(d), q_ref[sl, cols], 1.0 - f, i_ref[sl, cols], jnp.log(f), st_ref[d, hh]))
        for (d, hh, sl, cols), (o, st_new) in zip(where, _hgrn_chunks(chains)):
            st_ref[d, hh] = st_new
            (ob_ref if d else of_ref)[sl, cols] = o
        return carry

    lax.fori_loop(0, n_chunks, body, 0)
    for hh in range(HG_STEP_HEADS):
        cols = slice(hh * hd, (hh + 1) * hd)
        o = of_ref[:, cols] + ob_ref[:, cols]
        o = o * lax.rsqrt(jnp.mean(o * o, axis=-1, keepdims=True) + RMS_EPS)
        o_ref[:, cols] = o * _silu(g_ref[:, cols])
        for d in range(2):
            s_out_ref[d, hh] = st_ref[d, hh].T


def _hgrn(p, hg_lb, layer, s0, n_seq, seq_len, tok_offset):
    hd = HEAD_DIM
    sh = HG_STEP_HEADS
    steps_per_seq = HEADS // sh
    row0 = tok_offset // seq_len

    def col(part):
        return pl.BlockSpec((seq_len, sh * hd), lambda b, h: (row0 + b, part * steps_per_seq + h))

    state_spec = pl.BlockSpec((None, 2, sh, hd, hd), lambda b, h: (b, 0, h, 0, 0))
    return pl.pallas_call(
        functools.partial(_hgrn_kernel, seq_len, layer),
        grid=(n_seq, steps_per_seq),
        in_specs=[col(0), col(1), col(2), col(3), col(4),
                  pl.BlockSpec((DEPTH + 1, sh * hd), lambda b, h: (0, h)),
                  state_spec],
        out_specs=[pl.BlockSpec((seq_len, sh * hd), lambda b, h: (b, h)), state_spec],
        out_shape=[jax.ShapeDtypeStruct((n_seq * seq_len, GROUP_W), F32),
                   jax.ShapeDtypeStruct((n_seq, 2, HEADS, hd, hd), F32)],
        scratch_shapes=[pltpu.VMEM((seq_len, sh * hd), F32), pltpu.VMEM((seq_len, sh * hd), F32),
                        pltpu.VMEM((2, sh, hd, hd), F32)],
        compiler_params=_cparams("parallel", "parallel"),
        name=f"hgrn_l{seq_len}",
    )(p, p, p, p, p, hg_lb, s0)


def _short_conv3(x, w):
    n = x.shape[0]
    r = lax.broadcasted_iota(jnp.int32, (n, 1), 0)
    prev = jnp.where(r == 0, 0.0, pltpu.roll(x, 1, 0))
    nxt = jnp.where(r == n - 1, 0.0, pltpu.roll(x, n - 1, 0))
    return prev * w[0:1] + x * w[1:2] + nxt * w[2:3]


def _split2(x):
    hi = x.astype(BF16)
    return hi, (x - hi.astype(F32)).astype(BF16)


def _mlstm_chunks(chains):
    c = SCAN_CHUNK
    row = lax.broadcasted_iota(jnp.int32, (c, c), 0)
    col = lax.broadcasted_iota(jnp.int32, (c, c), 1)
    eye_b = jnp.where(row == col, 1.0, 0.0).astype(BF16)
    tris = {False: row <= col, True: row >= col}
    tris_b = {r: jnp.where(t, 1.0, 0.0).astype(BF16) for r, t in tris.items()}
    n = range(len(chains))
    rev, q, k, vt, ig, fg, ct, nv, m_prev = zip(*chains)

    def each(fn):
        return [fn(i) for i in n]

    def dot3(parts, rhs, nt=False):
        d = _dot_nt if nt else _dot
        return d(parts[0], rhs) + d(parts[1], rhs) + d(parts[2], rhs)

    lf = each(lambda i: _split3(jnp.broadcast_to(_log_sigmoid(fg[i]), (8, c))))
    b = each(lambda i: dot3(lf[i], tris_b[rev[i]])[0:1])
    us = each(lambda i: _split3(jnp.broadcast_to(ig[i] - b[i], (c, c))))
    u = each(lambda i: _dot_nt(eye_b, us[i][0]) + _dot_nt(eye_b, us[i][1]) + _dot_nt(eye_b, us[i][2]))
    dmat = each(lambda i: jnp.where(tris[rev[i]], b[i] + u[i], -jnp.inf))
    m_t = each(lambda i: jnp.maximum(b[i] + m_prev[i], jnp.max(dmat[i], axis=0, keepdims=True)))
    qb = each(lambda i: q[i].astype(BF16))
    kb = each(lambda i: k[i].astype(BF16))
    kq = each(lambda i: _dot_nt(kb[i], qb[i]))
    p = each(lambda i: jnp.exp(dmat[i] - m_t[i]) * kq[i])
    inter = each(lambda i: jnp.exp(b[i] + m_prev[i] - m_t[i]))
    ns = each(lambda i: _split2(jnp.broadcast_to(nv[i], (8, nv[i].shape[1]))))
    qs = each(lambda i: _split2(q[i]))
    qn = each(lambda i: (_dot_nt(ns[i][0], qs[i][0]) + _dot_nt(ns[i][1], qs[i][0])
                         + _dot_nt(ns[i][0], qs[i][1]))[0:1])
    den = each(lambda i: inter[i] * qn[i] + jnp.sum(p[i], axis=0, keepdims=True))
    scale = each(lambda i: 1.0 / jnp.maximum(jnp.abs(den[i]), jnp.exp(-m_t[i])))
    cq = each(lambda i: _dot_nt(ct[i].astype(BF16), qb[i]))
    vp = each(lambda i: _dot(vt[i].astype(BF16), p[i].astype(BF16)))
    ht = each(lambda i: (inter[i] * cq[i] + vp[i]) * scale[i])
    last = each(lambda i: 0 if rev[i] else c - 1)
    m_new = each(lambda i: m_t[i][:, last[i]:last[i] + 1])
    b_exit = each(lambda i: b[i][:, last[i]:last[i] + 1])
    w = each(lambda i: jnp.exp(b_exit[i] - b[i] + ig[i] - m_new[i]))
    dec = each(lambda i: jnp.exp(b_exit[i] + m_prev[i] - m_new[i]))
    vk = each(lambda i: _dot((vt[i] * w[i]).astype(BF16), kb[i]))
    ct_new = each(lambda i: dec[i] * ct[i] + vk[i])
    ws = each(lambda i: _split2(jnp.broadcast_to(w[i], (8, c))))
    ks = each(lambda i: _split2(k[i]))
    wk = each(lambda i: (_dot(ws[i][0], ks[i][0]) + _dot(ws[i][1], ks[i][0]) + _dot(ws[i][0], ks[i][1]))[0:1])
    nv_new = each(lambda i: dec[i] * nv[i] + wk[i])
    return [(ht[i], ct_new[i], nv_new[i], m_new[i]) for i in n]


def _mlstm_kernel(seq_len, q_ref, k_ref, v_ref, og_ref, gate_t_ref, cwq_ref, cwk_ref,
                  c0_ref, n0_ref, m0_ref,
                  o_ref, c_out_ref, n_out_ref, m_out_ref,
                  q2_ref, k2_ref, vt_ref, hf_ref, hb_ref, ct_ref, n_ref, m_ref):
    c = SCAN_CHUNK
    hd = HEAD_DIM
    n_chunks = seq_len // c
    q2_ref[...] = _silu(_short_conv3(q_ref[...], cwq_ref[...]))
    k2_ref[...] = _silu(_short_conv3(k_ref[...], cwk_ref[...])) * (HEAD_DIM ** -0.5)
    for hh in range(HEADS):
        cols = slice(hh * hd, (hh + 1) * hd)
        for j in range(n_chunks):
            vt_ref[hh, :, j * c:(j + 1) * c] = v_ref[j * c:(j + 1) * c, cols].T
        for d in range(2):
            ct_ref[d, hh] = c0_ref[d, hh].T
    n_ref[...] = n0_ref[...]
    m_ref[...] = m0_ref[...]

    def body(n, carry):
        where, chains = [], []
        for d in range(2):
            sl = pl.ds(pl.multiple_of((n_chunks - 1 - n if d else n) * c, c), c)
            for hh in range(HEADS):
                cols = slice(hh * hd, (hh + 1) * hd)
                gr = gate_t_ref[hh, :, sl]
                where.append((d, hh, sl))
                chains.append((bool(d), q2_ref[sl, cols], k2_ref[sl, cols], vt_ref[hh, :, sl],
                               gr[2 * d:2 * d + 1, :], gr[2 * d + 1:2 * d + 2, :],
                               ct_ref[d, hh], n_ref[d, hh], m_ref[d, hh]))
        for (d, hh, sl), (ht, ct, nv, m_new) in zip(where, _mlstm_chunks(chains)):
            ct_ref[d, hh] = ct
            n_ref[d, hh] = nv
            m_ref[d, hh] = m_new
            (hb_ref if d else hf_ref)[hh, :, sl] = ht
        return carry

    lax.fori_loop(0, n_chunks, body, 0)
    for hh in range(HEADS):
        cols = slice(hh * hd, (hh + 1) * hd)
        for j in range(n_chunks):
            rows = slice(j * c, (j + 1) * c)
            h = (hf_ref[hh, :, rows] + hb_ref[hh, :, rows]).T
            h = h * lax.rsqrt(jnp.mean(h * h, axis=-1, keepdims=True) + RMS_EPS)
            o_ref[rows, cols] = h * _sigmoid(og_ref[rows, cols])
        for d in range(2):
            c_out_ref[d, hh] = ct_ref[d, hh].T
    n_out_ref[...] = n_ref[...]
    m_out_ref[...] = m_ref[...]


def _mlstm(p, gate_t_h, conv_w, c0, n0, m0, n_seq, seq_len, tok_offset):
    hd = HEAD_DIM
    gw = GROUP_W
    row0 = tok_offset // seq_len

    def col(part):
        return pl.BlockSpec((seq_len, gw), lambda b: (row0 + b, part))

    c_spec = pl.BlockSpec((None, 2, HEADS, hd, hd), lambda b: (b, 0, 0, 0, 0))
    n_spec = pl.BlockSpec((None, 2, HEADS, 1, hd), lambda b: (b, 0, 0, 0, 0))
    m_spec = pl.BlockSpec((None, 2, HEADS, 1, 1), lambda b: (b, 0, 0, 0, 0))
    return pl.pallas_call(
        functools.partial(_mlstm_kernel, seq_len),
        grid=(n_seq,),
        in_specs=[col(5), col(6), col(7), col(8),
                  pl.BlockSpec((HEADS, 4, seq_len), lambda b: (0, 0, row0 + b)),
                  pl.BlockSpec((3, gw), lambda b: (0, 0)),
                  pl.BlockSpec((3, gw), lambda b: (0, 1)),
                  c_spec, n_spec, m_spec],
        out_specs=[pl.BlockSpec((seq_len, gw), lambda b: (b, 0)), c_spec, n_spec, m_spec],
        out_shape=[jax.ShapeDtypeStruct((n_seq * seq_len, gw), F32),
                   jax.ShapeDtypeStruct((n_seq, 2, HEADS, hd, hd), F32),
                   jax.ShapeDtypeStruct((n_seq, 2, HEADS, 1, hd), F32),
                   jax.ShapeDtypeStruct((n_seq, 2, HEADS, 1, 1), F32)],
        scratch_shapes=[pltpu.VMEM((seq_len, gw), F32), pltpu.VMEM((seq_len, gw), F32),
                        pltpu.VMEM((HEADS, hd, seq_len), F32),
                        pltpu.VMEM((HEADS, hd, seq_len), F32), pltpu.VMEM((HEADS, hd, seq_len), F32),
                        pltpu.VMEM((2, HEADS, hd, hd), F32), pltpu.VMEM((2, HEADS, 1, hd), F32),
                        pltpu.VMEM((2, HEADS, 1, 1), F32)],
        compiler_params=_cparams("parallel"),
        name=f"mlstm_l{seq_len}",
    )(p, p, p, p, gate_t_h, conv_w, conv_w, c0, n0, m0)


def _dft_tables(seq_len):
    n = 2 * seq_len
    k = jnp.arange(seq_len, dtype=jnp.int32)[:, None]
    t = jnp.arange(seq_len, dtype=jnp.int32)[None, :]
    ang = ((k * t) % n).astype(F32) * (2.0 * np.pi / n)
    fc = jnp.cos(ang)
    fs = jnp.sin(ang)
    nyq = jnp.where(t % 2 == 0, 1.0, -1.0).astype(F32)
    fs = jnp.where(k == 0, nyq, fs)
    return jnp.concatenate([fc, fs], axis=0)


def _filter_kernel(seq_len, z_ref, w1_ref, b1_ref, w2_ref, b2_ref, w3f_ref, w3b_ref, b3f_ref, b3b_ref,
                   f0_ref, f1_ref, rf_ref, rb_ref, fhi_ref, flo_ref, kf_ref, a_ref):
    hp = lax.Precision.HIGHEST
    n = 2 * seq_len
    z = z_ref[...]

    @pl.when(jnp.logical_and(pl.program_id(0) == 0, pl.program_id(1) == 0))
    def _():
        a1 = jnp.sin(f0_ref[...] * (jnp.dot(z, w1_ref[...], precision=hp, preferred_element_type=F32)
                                    + b1_ref[...]))
        a_ref[...] = jnp.sin(f1_ref[...] * (jnp.dot(a1, w2_ref[...], precision=hp, preferred_element_type=F32)
                                            + b2_ref[...]))

    a = a_ref[...]
    t_norm = z[:, 0:1]
    hf = (jnp.dot(a, w3f_ref[...], precision=hp, preferred_element_type=F32) + b3f_ref[...]) \
        * jnp.exp(-t_norm * jnp.exp(rf_ref[...]))
    hb = (jnp.dot(a, w3b_ref[...], precision=hp, preferred_element_type=F32) + b3b_ref[...]) \
        * jnp.exp(-t_norm * jnp.exp(rb_ref[...]))
    inv = lax.rsqrt(jnp.sum(hf * hf, axis=0, keepdims=True) + jnp.sum(hb * hb, axis=0, keepdims=True))
    hf = hf * inv
    r = lax.broadcasted_iota(jnp.int32, (seq_len, 1), 0)
    hb = jnp.where(r == 0, 0.0, hb * inv)
    sh, sm, sl = _split3(hf + hb)
    dh, dm, dl = _split3(hf - hb)
    fhi = fhi_ref[...]
    flo = flo_ref[...]
    kc = _dot(fhi[:seq_len], sh) + _dot(fhi[:seq_len], sm) + _dot(fhi[:seq_len], sl) \
        + _dot(flo[:seq_len], sh) + _dot(flo[:seq_len], sm)
    ks = _dot(fhi[seq_len:], dh) + _dot(fhi[seq_len:], dm) + _dot(fhi[seq_len:], dl) \
        + _dot(flo[seq_len:], dh) + _dot(flo[seq_len:], dm)
    sign = jnp.where(r % 2 == 0, 1.0, -1.0)
    k_nyq = jnp.sum(sign * (hf + hb), axis=0, keepdims=True)
    ks = jnp.where(r == 0, k_nyq, ks)
    scale = jnp.where(r == 0, 1.0 / n, 2.0 / n)
    kf_ref[0:seq_len, :] = kc * scale
    kf_ref[seq_len:n, :] = ks * scale


def _hyena_filters(seq_len, w1, b1, w2, b2, w3, b3, freq, log_rate, f_tab):
    d = D_MODEL
    cb = 256
    t = jnp.arange(seq_len, dtype=F32)
    t_norm = t / (seq_len - 1)
    bands = jnp.linspace(1e-4, HY_BANDS - 1, HY_BANDS, dtype=F32)
    ang = (2.0 * np.pi / seq_len) * t[:, None] * bands[None, :]
    z = jnp.concatenate([t_norm[:, None], jnp.cos(ang), jnp.sin(ang)], axis=-1)
    kpad = 128 - HY_EMB
    z = jnp.pad(z, ((0, 0), (0, kpad)))
    w1p = jnp.pad(w1, ((0, kpad), (0, 0)))
    f_hi = f_tab.astype(BF16)
    f_lo = (f_tab - f_hi.astype(F32)).astype(BF16)
    n_cb = d // cb
    hh = HY_HIDDEN
    row = lambda a: a.reshape(1, -1)
    const = lambda shape: pl.BlockSpec(shape, lambda o, j: (0,) * len(shape))
    fwd = lambda rows: pl.BlockSpec((rows, cb), lambda o, j: (0, o * n_cb + j))
    bwd = lambda rows: pl.BlockSpec((rows, cb), lambda o, j: (0, (HY_ORDER + o) * n_cb + j))
    return pl.pallas_call(
        functools.partial(_filter_kernel, seq_len),
        grid=(HY_ORDER, n_cb),
        in_specs=[const((seq_len, 128)), const((128, hh)), const((1, hh)), const((hh, hh)), const((1, hh)),
                  fwd(hh), bwd(hh), fwd(1), bwd(1),
                  const((1, hh)), const((1, hh)), fwd(1), bwd(1),
                  const((2 * seq_len, seq_len)), const((2 * seq_len, seq_len))],
        out_specs=pl.BlockSpec((None, 2 * seq_len, cb), lambda o, j: (o, 0, j)),
        out_shape=jax.ShapeDtypeStruct((HY_ORDER, 2 * seq_len, d), F32),
        scratch_shapes=[pltpu.VMEM((seq_len, hh), F32)],
        compiler_params=_cparams("arbitrary", "arbitrary"),
        name=f"hyena_filter_l{seq_len}",
    )(z, w1p, row(b1), w2, row(b2), w3, w3, row(b3), row(b3),
      row(freq[0]), row(freq[1]), row(log_rate), row(log_rate), f_hi, f_lo)


def _hyena_kernel(seq_len, seqs, v_ref, x1_ref, x2_ref, cwv_ref, cw1_ref, cw2_ref, bias_ref, kf_ref,
                  f_ref, ft_ref, o_ref, z_ref, zb_ref, y_ref):
    kc = min(HY_FREQ_CHUNK, seq_len)
    n_k = seq_len // kc
    r = lax.broadcasted_iota(jnp.int32, (kc, 1), 0)
    gate_refs = ((x1_ref, cw1_ref), (x2_ref, cw2_ref))
    for s in range(seqs):
        rows = slice(s * seq_len, (s + 1) * seq_len)
        z_ref[s] = _short_conv3(v_ref[rows, :], cwv_ref[...])
    for o in range(HY_ORDER):
        for s in range(seqs):
            zb_ref[s] = z_ref[s].astype(BF16)
            y_ref[s] = jnp.zeros(y_ref.shape[1:], F32)

        def freq_chunk(j, carry):
            r0 = pl.multiple_of(j * kc, kc)
            k_cos = kf_ref[o, pl.ds(r0, kc), :]
            k_sin = kf_ref[o, pl.ds(seq_len + r0, kc), :]
            real_row = jnp.logical_and(r == 0, j == 0)
            for s in range(seqs):
                a = _dot(f_ref[pl.ds(r0, kc), :], zb_ref[s])
                bm = _dot(f_ref[pl.ds(seq_len + r0, kc), :], zb_ref[s])
                yc = a * k_cos - jnp.where(real_row, 0.0, bm * k_sin)
                ys = jnp.where(real_row, bm * k_sin, a * k_sin + bm * k_cos)
                y_ref[s] += _dot(ft_ref[j], yc.astype(BF16)) + _dot(ft_ref[n_k + j], ys.astype(BF16))
            return carry

        lax.fori_loop(0, n_k, freq_chunk, 0)
        x_ref, cw_ref = gate_refs[o]
        for s in range(seqs):
            rows = slice(s * seq_len, (s + 1) * seq_len)
            gate = _short_conv3(x_ref[rows, :], cw_ref[...])
            z_ref[s] = gate * (y_ref[s] + z_ref[s] * bias_ref[o:o + 1, :])
    for s in range(seqs):
        o_ref[s * seq_len:(s + 1) * seq_len, :] = z_ref[s]


def _hyena(u, conv_w, bias, kf, f_tab, n_seq, seq_len, tok_offset):
    d = D_MODEL
    cb = 256
    n_cb = d // cb
    seqs = max(1, HY_STEP_ROWS // seq_len)
    rows = seqs * seq_len
    row0 = tok_offset // rows
    kc = min(HY_FREQ_CHUNK, seq_len)
    n_k = seq_len // kc
    f_bf = f_tab.astype(BF16)
    ft = f_bf.T.reshape(seq_len, 2 * n_k, kc).transpose(1, 0, 2)

    def part(k):
        return pl.BlockSpec((rows, cb), lambda j, b: (row0 + b, k * n_cb + j))

    def cw(k):
        return pl.BlockSpec((3, cb), lambda j, b: (0, k * n_cb + j))

    return pl.pallas_call(
        functools.partial(_hyena_kernel, seq_len, seqs),
        grid=(n_cb, n_seq // seqs),
        in_specs=[part(0), part(1), part(2), cw(0), cw(1), cw(2),
                  pl.BlockSpec((HY_ORDER, cb), lambda j, b: (0, j)),
                  pl.BlockSpec((HY_ORDER, 2 * seq_len, cb), lambda j, b: (0, 0, j)),
                  pl.BlockSpec((2 * seq_len, seq_len), lambda j, b: (0, 0)),
                  pl.BlockSpec((2 * n_k, seq_len, kc), lambda j, b: (0, 0, 0))],
        out_specs=pl.BlockSpec((rows, cb), lambda j, b: (b, j)),
        out_shape=jax.ShapeDtypeStruct((n_seq * seq_len, d), F32),
        scratch_shapes=[pltpu.VMEM((seqs, seq_len, cb), F32), pltpu.VMEM((seqs, seq_len, cb), BF16),
                        pltpu.VMEM((seqs, seq_len, cb), F32)],
        compiler_params=_cparams("parallel", "parallel"),
        name=f"hyena_l{seq_len}",
    )(u, u, u, conv_w, conv_w, conv_w, bias, kf, f_bf, ft)


def _mix_out_kernel(n_parts, *refs):
    o_refs = refs[:2 * n_parts]
    w_refs = refs[2 * n_parts:3 * n_parts]
    x_ref, mod_ref, g_ref, wr_hi_ref, wr_lo_ref, br_ref = refs[3 * n_parts:3 * n_parts + 6]
    x1_ref, h_ref, eid_ref, wts_ref, rank_ref, cnt_ref, run_ref = refs[3 * n_parts + 6:]
    i = pl.program_id(0)
    t = TOK_TILE
    ne = N_EXPERTS

    from_prompt = i < N_PROMPT // TOK_TILE
    y = None
    for j in range(n_parts):
        o = jnp.where(from_prompt, o_refs[2 * j][...], o_refs[2 * j + 1][...])
        yj = _dot(o.astype(BF16), w_refs[j][...])
        y = yj if y is None else y + yj
    x1 = x_ref[...] + mod_ref[2:3, :] * y
    x1_ref[...] = x1
    h = _norm_mod(x1, g_ref[...], mod_ref[4:5, :], mod_ref[3:4, :])
    h_ref[...] = _pack_bf16_halves(h)
    h_hi = h.astype(BF16)
    h_lo = (h - h_hi.astype(F32)).astype(BF16)
    logits = (_dot(h_hi, wr_hi_ref[...]) + _dot(h_hi, wr_lo_ref[...]) + _dot(h_lo, wr_hi_ref[...])).T[:ne] \
        + br_ref[...]

    @pl.when(i == 0)
    def _():
        run_ref[...] = jnp.zeros_like(run_ref)

    e_iota = lax.broadcasted_iota(jnp.int32, (ne, t), 0)
    vals, eids, onehots = [], [], []
    for _k in range(TOP_K):
        m = jnp.max(logits, axis=0, keepdims=True)
        eid = jnp.min(jnp.where(logits == m, e_iota, ne), axis=0, keepdims=True)
        sel = e_iota == eid
        logits = jnp.where(sel, -jnp.inf, logits)
        onehots.append(jnp.where(sel, 1.0, 0.0))
        vals.append(m)
        eids.append(eid)
    r2 = lax.broadcasted_iota(jnp.int32, (t, t), 0)
    c2 = lax.broadcasted_iota(jnp.int32, (t, t), 1)
    before = jnp.where(r2 < c2, 1.0, 0.0).astype(BF16)
    earlier = _dot(jnp.concatenate(onehots, axis=0).astype(BF16), before)
    running = run_ref[...]
    ranks = []
    for k, onehot in enumerate(onehots):
        ranks.append(jnp.sum(onehot * (running + earlier[k * ne:(k + 1) * ne]), axis=0, keepdims=True))
        running = running + jnp.sum(onehot, axis=1, keepdims=True)
    run_ref[...] = running
    cnt_ref[...] = running
    v = jnp.concatenate(vals, axis=0)
    ex = jnp.exp(v - v[0:1])
    wts_ref[...] = ex / jnp.sum(ex, axis=0, keepdims=True)
    eid_ref[...] = jnp.concatenate(eids, axis=0)
    rank_ref[...] = jnp.concatenate(ranks, axis=0).astype(jnp.int32)


def _mix_out(parts, x, mod_l, norm_g, w_router, b_router):
    d = D_MODEL
    t = TOK_TILE
    ne = N_EXPERTS
    n_parts = len(parts)
    n_prompt_tiles = N_PROMPT // t
    wr = jnp.pad(w_router, ((0, 0), (0, LANES - ne)))
    wr_hi = wr.astype(BF16)
    wr_lo = (wr - wr_hi.astype(F32)).astype(BF16)
    in_specs = []
    for o_p, _, _ in parts:
        in_specs.append(pl.BlockSpec((t, o_p.shape[1]), lambda i: (jnp.minimum(i, n_prompt_tiles - 1), 0)))
        in_specs.append(pl.BlockSpec((t, o_p.shape[1]), lambda i: (jnp.maximum(i - n_prompt_tiles, 0), 0)))
    in_specs += [pl.BlockSpec(w.shape, lambda i: (0, 0)) for _, _, w in parts]
    in_specs += [
        pl.BlockSpec((t, d), lambda i: (i, 0)),
        pl.BlockSpec((None, N_MOD, d), lambda i: (_tile_cond_row(i), 0, 0)),
        pl.BlockSpec((1, d), lambda i: (0, 0)),
        pl.BlockSpec((d, LANES), lambda i: (0, 0)),
        pl.BlockSpec((d, LANES), lambda i: (0, 0)),
        pl.BlockSpec((ne, 1), lambda i: (0, 0)),
    ]
    tok_major = pl.BlockSpec((TOP_K, t), lambda i: (0, i))
    return pl.pallas_call(
        functools.partial(_mix_out_kernel, n_parts),
        grid=(N_TOK // t,),
        in_specs=in_specs,
        out_specs=[pl.BlockSpec((t, d), lambda i: (i, 0)), pl.BlockSpec((t, d // 2), lambda i: (i, 0)),
                   tok_major, tok_major, tok_major, pl.BlockSpec((ne, 1), lambda i: (0, 0))],
        out_shape=[jax.ShapeDtypeStruct((N_TOK, d), F32), jax.ShapeDtypeStruct((N_TOK, d // 2), jnp.uint32),
                   jax.ShapeDtypeStruct((TOP_K, N_TOK), jnp.int32), jax.ShapeDtypeStruct((TOP_K, N_TOK), F32),
                   jax.ShapeDtypeStruct((TOP_K, N_TOK), jnp.int32), jax.ShapeDtypeStruct((ne, 1), F32)],
        scratch_shapes=[pltpu.VMEM((ne, 1), F32)],
        compiler_params=_cparams("arbitrary"),
        name="mix_out_router",
    )(*[o for part in parts for o in part[:2]], *[w.astype(BF16) for _, _, w in parts], x, mod_l,
      norm_g.reshape(1, d), wr_hi, wr_lo, b_router.reshape(ne, 1))


def _sc_row_gather(src, idx):
    n = idx.shape[0]
    width = src.shape[1]
    step_rows = min(SC_MAX_INDICES, SC_STEP_BYTES // (width * 4))
    sc = plsc.get_sparse_core_info()
    n_workers = sc.num_cores * sc.num_subcores
    per_worker = n // n_workers
    if n % n_workers or per_worker % step_rows:
        raise ValueError("row count must be whole SparseCore steps on every subcore")
    mesh = plsc.VectorSubcoreMesh(core_axis_name="core", subcore_axis_name="subcore")

    @functools.partial(pl.kernel, out_type=jax.ShapeDtypeStruct((n, width), src.dtype), mesh=mesh,
                       scratch_types=[pltpu.VMEM((per_worker,), jnp.int32),
                                      pltpu.VMEM((step_rows, width), src.dtype)],
                       name="sc_row_gather")
    def gather(src_hbm, idx_hbm, dst_hbm, idx_vmem, rows_vmem):
        worker = lax.axis_index("subcore") * sc.num_cores + lax.axis_index("core")
        base = worker * per_worker
        pltpu.sync_copy(idx_hbm.at[pl.ds(base, per_worker)], idx_vmem)

        @pl.loop(0, per_worker // step_rows)
        def _(c):
            pltpu.sync_copy(src_hbm.at[idx_vmem.at[pl.ds(c * step_rows, step_rows)]], rows_vmem)
            pltpu.sync_copy(rows_vmem, dst_hbm.at[pl.ds(base + c * step_rows, step_rows)])

    return gather(src, idx)


def _sc_row_scatter(src, row_of_slot):
    n = row_of_slot.shape[0]
    n_src, width = src.shape
    step_rows = min(SC_MAX_INDICES, SC_STEP_BYTES // (width * 4))
    sc = plsc.get_sparse_core_info()
    n_workers = sc.num_cores * sc.num_subcores
    per_worker = n // n_workers
    steps = per_worker // step_rows
    if n % n_workers or per_worker % step_rows or n_src % step_rows:
        raise ValueError("row counts must be whole SparseCore steps on every subcore")
    mesh = plsc.VectorSubcoreMesh(core_axis_name="core", subcore_axis_name="subcore")

    @functools.partial(pl.kernel, out_type=jax.ShapeDtypeStruct((n, width), src.dtype), mesh=mesh,
                       scratch_types=[pltpu.VMEM((steps, step_rows), jnp.int32),
                                      pltpu.VMEM((step_rows, width), src.dtype)],
                       name="sc_row_scatter")
    def scatter(src_hbm, idx_hbm, dst_hbm, idx_vmem, rows_vmem):
        worker = lax.axis_index("subcore") * sc.num_cores + lax.axis_index("core")
        pltpu.sync_copy(idx_hbm.at[worker], idx_vmem)

        @pl.loop(0, steps)
        def _(c):
            src0 = lax.rem(worker * per_worker + c * step_rows, n_src)
            pltpu.sync_copy(src_hbm.at[pl.ds(src0, step_rows)], rows_vmem)
            pltpu.sync_copy(rows_vmem, dst_hbm.at[idx_vmem.at[c]])

    return scatter(src, row_of_slot.reshape(n_workers, steps, step_rows))


def _pack_bf16_halves(x):
    w = x.shape[1] // 2
    bits = pltpu.bitcast(x.astype(BF16).astype(F32), jnp.uint32)
    return bits[:, :w] | (bits[:, w:] >> 16)


def _unpack_bf16_halves(p):
    hi = pltpu.bitcast(p & jnp.uint32(0xFFFF0000), F32).astype(BF16)
    lo = pltpu.bitcast(p << 16, F32).astype(BF16)
    return hi, lo


def _experts_kernel(layer, te_ref, first_ref, slot_ref, next_ref, nv_ref,
                    x_ref, wgu_hbm, bg_ref, bu_ref, wd_hbm, bd_ref, sel_ref,
                    y_ref, wgu_buf, wd_buf, wg_ref, wu_ref, wdb_ref, sem):
    i = pl.program_id(0)
    valid = i < nv_ref[0]
    half = DEINT_COLS // 2
    k_half = x_ref.shape[1]

    def fetch(expert, slot):
        return (pltpu.make_async_copy(wgu_hbm.at[layer, expert], wgu_buf.at[slot], sem.at[slot, 0]),
                pltpu.make_async_copy(wd_hbm.at[layer, expert], wd_buf.at[slot], sem.at[slot, 1]))

    @pl.when(i == 0)
    def _():
        for cp in fetch(te_ref[0], 0):
            cp.start()

    @pl.when(jnp.logical_and(valid, first_ref[i] == 1))
    def _():
        slot = slot_ref[i]
        for cp in fetch(te_ref[i], slot):
            cp.wait()

        @pl.when(next_ref[i] >= 0)
        def _():
            for cp in fetch(next_ref[i], 1 - slot):
                cp.start()

        for c in range(wgu_buf.shape[2] // DEINT_COLS):
            w = wgu_buf[slot, :, c * DEINT_COLS:(c + 1) * DEINT_COLS].astype(BF16)
            split = _dot(w, sel_ref[...]).astype(BF16)
            wg_ref[:, c * half:(c + 1) * half] = split[:, :half]
            wu_ref[:, c * half:(c + 1) * half] = split[:, half:]
        wdb_ref[...] = wd_buf[slot].astype(BF16)

    @pl.when(valid)
    def _():
        x_l, x_r = _unpack_bf16_halves(x_ref[...])
        gl = _dot(x_l, wg_ref[:k_half, :]) + _dot(x_r, wg_ref[k_half:, :]) + bg_ref[...]
        up = _dot(x_l, wu_ref[:k_half, :]) + _dot(x_r, wu_ref[k_half:, :]) + bu_ref[...]
        gl = jnp.minimum(gl, SWIGLU_LIMIT)
        up = jnp.clip(up, -SWIGLU_LIMIT, SWIGLU_LIMIT)
        act = (up + 1.0) * gl * _sigmoid(SWIGLU_ALPHA * gl)
        y_ref[...] = _pack_bf16_halves(_dot(act.astype(BF16), wdb_ref[...]) + bd_ref[...])

    @pl.when(jnp.logical_not(valid))
    def _():
        y_ref[...] = jnp.zeros_like(y_ref)


def _experts(xs, tile_expert, tile_first, n_valid, layer, w_gu, b_gate, b_up, w_down, b_down):
    d = D_MODEL
    tm = MOE_TILE
    n_tiles = MOE_ROWS // tm
    ff = w_down.shape[2]
    half = DEINT_COLS // 2
    r = jnp.arange(DEINT_COLS)[:, None]
    c = jnp.arange(DEINT_COLS)[None, :]
    sel = (r == jnp.where(c < half, 2 * c, 2 * (c - half) + 1)).astype(BF16)
    group = jnp.cumsum(tile_first) - 1
    tile_slot = (group % 2).astype(jnp.int32)
    is_last_group = group == group[-1]
    following = jnp.concatenate([tile_expert[1:], tile_expert[-1:]])
    idx = jnp.arange(n_tiles, dtype=jnp.int32)
    group_end = jnp.max(jnp.where(group[None, :] == group[:, None], idx[None, :], -1), axis=1)
    tile_next = jnp.where(is_last_group, -1, following[group_end]).astype(jnp.int32)
    wspec = lambda k, n: pl.BlockSpec((None, None, k, n), lambda i, *_: (layer, _[0][i], 0, 0))
    grid_spec = pltpu.PrefetchScalarGridSpec(
        num_scalar_prefetch=5,
        grid=(n_tiles,),
        in_specs=[pl.BlockSpec((tm, d // 2), lambda i, *_: (i, 0)),
                  pl.BlockSpec(memory_space=pl.ANY), wspec(1, ff), wspec(1, ff),
                  pl.BlockSpec(memory_space=pl.ANY), wspec(1, d),
                  pl.BlockSpec((DEINT_COLS, DEINT_COLS), lambda i, *_: (0, 0))],
        out_specs=pl.BlockSpec((tm, d // 2), lambda i, *_: (i, 0)),
        scratch_shapes=[pltpu.VMEM((2, d, 2 * ff), F32), pltpu.VMEM((2, ff, d), F32),
                        pltpu.VMEM((d, ff), BF16), pltpu.VMEM((d, ff), BF16), pltpu.VMEM((ff, d), BF16),
                        pltpu.SemaphoreType.DMA((2, 2))],
    )
    return pl.pallas_call(
        functools.partial(_experts_kernel, layer),
        grid_spec=grid_spec,
        out_shape=jax.ShapeDtypeStruct((MOE_ROWS, d // 2), jnp.uint32),
        compiler_params=_cparams("arbitrary"),
        name="experts",
    )(tile_expert, tile_first, tile_slot, tile_next, n_valid, xs, w_gu, b_gate, b_up, w_down, b_down, sel)


def _combine_kernel(final, x_ref, g_ref, w_ref, mod_ref, fg_ref, *o_refs):
    y_l = y_r = None
    for k in range(TOP_K):
        g_l, g_r = _unpack_bf16_halves(g_ref[k])
        wk = w_ref[:, k:k + 1]
        y_l = wk * g_l.astype(F32) if y_l is None else y_l + wk * g_l.astype(F32)
        y_r = wk * g_r.astype(F32) if y_r is None else y_r + wk * g_r.astype(F32)
    x = x_ref[...] + mod_ref[5:6, :] * jnp.concatenate([y_l, y_r], axis=1)
    if not final:
        o_refs[0][...] = x
        return
    x = x * lax.rsqrt(jnp.mean(x * x, axis=-1, keepdims=True) + RMS_EPS) * fg_ref[...]
    from_prompt = pl.program_id(0) < N_PROMPT // TOK_TILE

    @pl.when(from_prompt)
    def _():
        o_refs[0][...] = x

    @pl.when(jnp.logical_not(from_prompt))
    def _():
        o_refs[1][...] = x


def _combine(x1, gathered, wts, mod_l, final_g, final):
    d = D_MODEL
    t = TOK_TILE
    n_prompt_tiles = N_PROMPT // t
    if final:
        out_specs = [pl.BlockSpec((t, d), lambda i: (jnp.minimum(i, n_prompt_tiles - 1), 0)),
                     pl.BlockSpec((t, d), lambda i: (jnp.maximum(i - n_prompt_tiles, 0), 0))]
        out_shape = [jax.ShapeDtypeStruct((N_PROMPT, d), F32), jax.ShapeDtypeStruct((N_SAMPLE, d), F32)]
    else:
        out_specs = pl.BlockSpec((t, d), lambda i: (i, 0))
        out_shape = jax.ShapeDtypeStruct((N_TOK, d), F32)
    return pl.pallas_call(
        functools.partial(_combine_kernel, final),
        grid=(N_TOK // t,),
        in_specs=[pl.BlockSpec((t, d), lambda i: (i, 0)),
                  pl.BlockSpec((TOP_K, t, d // 2), lambda i: (0, i, 0)),
                  pl.BlockSpec((t, TOP_K), lambda i: (i, 0)),
                  pl.BlockSpec((None, N_MOD, d), lambda i: (_tile_cond_row(i), 0, 0)),
                  pl.BlockSpec((1, d), lambda i: (0, 0))],
        out_specs=out_specs,
        out_shape=out_shape,
        compiler_params=_cparams("arbitrary"),
        name="moe_combine",
    )(x1, gathered, wts, mod_l, final_g.reshape(1, d))


def _moe(x1, h, eid, wts, rank, counts, mod_l, layer, w_gu, b_gu, w_down, b_down, final_g, final):
    d = D_MODEL
    tm = MOE_TILE
    n_tiles = MOE_ROWS // tm
    cnt = counts.reshape(N_EXPERTS).astype(jnp.int32)
    gsz = ((cnt + tm - 1) // tm) * tm
    ends = jnp.cumsum(gsz)
    offs = ends - gsz
    e_ids = jnp.arange(N_EXPERTS, dtype=jnp.int32)
    pos = jnp.sum(jnp.where(eid[..., None] == e_ids, offs, 0), axis=-1) + rank
    tile_start = jnp.arange(n_tiles, dtype=jnp.int32) * tm
    tile_expert = jnp.minimum(jnp.sum((ends[None, :] <= tile_start[:, None]).astype(jnp.int32), axis=1),
                              N_EXPERTS - 1)
    n_valid = (ends[-1:] // tm).astype(jnp.int32)
    last_valid = jnp.maximum(n_valid[0] - 1, 0)
    tile_expert = jnp.where(jnp.arange(n_tiles) < n_valid[0], tile_expert, tile_expert[last_valid])
    tile_first = jnp.concatenate([jnp.ones((1,), jnp.int32),
                                  (tile_expert[1:] != tile_expert[:-1]).astype(jnp.int32)])
    j = jnp.arange(tm, dtype=jnp.int32)[None, :]
    pad_used = j < (gsz - cnt)[:, None]
    n_unused_before = jnp.cumsum((~pad_used).reshape(-1).astype(jnp.int32)) - 1
    pad_pos = jnp.where(pad_used, (offs + cnt)[:, None] + j,
                        ends[-1] + n_unused_before.reshape(N_EXPERTS, tm))
    row_of_slot = jnp.concatenate([pos.reshape(-1), pad_pos.reshape(-1)])

    xs = _sc_row_scatter(h, row_of_slot)
    ys = _experts(xs, tile_expert, tile_first, n_valid, layer, w_gu,
                  b_gu[:, :, None, 0::2], b_gu[:, :, None, 1::2], w_down, b_down[:, :, None, :])
    gathered = _sc_row_gather(ys, pos.reshape(-1))
    return _combine(x1, gathered.reshape(TOP_K, N_TOK, d // 2), wts.T, mod_l, final_g, final)


def _grid_positions(n_tok, d):
    rows = n_tok // GRID_W
    r, col = jnp.meshgrid(jnp.arange(rows, dtype=F32), jnp.arange(GRID_W, dtype=F32), indexing='ij')
    r = r.reshape(-1)
    col = col.reshape(-1)
    quarter = d // 4
    inv = 1.0 / (10000.0 ** (jnp.arange(quarter, dtype=F32) / quarter))
    ar = r[:, None] * inv[None]
    ac = col[:, None] * inv[None]
    return jnp.concatenate([jnp.sin(ar), jnp.cos(ar), jnp.sin(ac), jnp.cos(ac)], axis=-1)


def kernel(x_prompt, x_sample, state_hgrn, state_mlstm_c, state_mlstm_n, state_mlstm_m, c, c_ctx,
           norm_g, final_g, w_mod, b_mod, ev_w_in, ev_gate_b, ev_conv, hg_lb, ev_w_out,
           hy_w_in, hy_conv, hy_w1, hy_b1, hy_w2, hy_b2, hy_w3, hy_b3, hy_freq, hy_log_rate, hy_bias, hy_w_out,
           w_router, b_router, w_gu, b_gu, w_down, b_down):
    d = D_MODEL
    hd = HEAD_DIM
    cond = jnp.concatenate([c_ctx[None], c, jnp.zeros((N_COND - 1 - DEC_BATCH, d), F32)], axis=0)
    mod = _modulation(cond, w_mod, b_mod)
    pos_tab = jnp.concatenate([jnp.zeros((TOK_TILE, d), F32), _grid_positions(DEC_SEQ, d)], axis=0)

    groups = ((BATCH, SEQ, 0), (DEC_BATCH, DEC_SEQ, N_PROMPT))
    new_states = None
    for l in range(DEPTH):
        if l % 2 == 0:
            e = l // 2
            if l == 0:
                x, p, gate_t = _proj_even(x_prompt.reshape(N_PROMPT, d), x_sample.reshape(N_SAMPLE, d),
                                          pos_tab, mod[l], norm_g[l, 0], ev_w_in[e], ev_gate_b[e])
            else:
                raise NotImplementedError("only the first layer adds grid positions")
            gate_t_h = gate_t.reshape(4, HEADS, N_TOK).transpose(1, 0, 2)
            o_hg, o_ml = [], []
            for gi, (n_seq, seq_len, off) in enumerate(groups):
                if gi == 0:
                    s0 = jnp.zeros((n_seq, 2, HEADS, hd, hd), F32)
                    c0 = jnp.zeros((n_seq, 2, HEADS, hd, hd), F32)
                    n0 = jnp.zeros((n_seq, 2, HEADS, 1, hd), F32)
                    m0 = jnp.zeros((n_seq, 2, HEADS, 1, 1), F32)
                else:
                    s0 = state_hgrn[:, e]
                    c0 = state_mlstm_c[:, e]
                    n0 = state_mlstm_n[:, e].reshape(n_seq, 2, HEADS, 1, hd)
                    m0 = state_mlstm_m[:, e].reshape(n_seq, 2, HEADS, 1, 1)
                og, s_fin = _hgrn(p, hg_lb, l, s0, n_seq, seq_len, off)
                om, c_fin, n_fin, m_fin = _mlstm(p, gate_t_h, ev_conv[e], c0, n0, m0, n_seq, seq_len, off)
                o_hg.append(og)
                o_ml.append(om)
                if gi == 0:
                    new_states = (s_fin[:, None], c_fin[:, None],
                                  n_fin.reshape(n_seq, 1, 2, HEADS, hd), m_fin.reshape(n_seq, 1, 2, HEADS))
            parts = [(*o_hg, ev_w_out[e][:GROUP_W]), (*o_ml, ev_w_out[e][GROUP_W:])]
        else:
            o = l // 2
            u = _proj_odd(x, mod[l], norm_g[l, 0], hy_w_in[o])
            zs = []
            for n_seq, seq_len, off in groups:
                f_tab = _dft_tables(seq_len)
                kf = _hyena_filters(seq_len, hy_w1[o], hy_b1[o], hy_w2[o], hy_b2[o], hy_w3[o], hy_b3[o],
                                    hy_freq[o], hy_log_rate[o], f_tab)
                zs.append(_hyena(u, hy_conv[o], hy_bias[o], kf, f_tab, n_seq, seq_len, off))
            parts = [(*zs, hy_w_out[o])]
        x1, h, eid, wts, rank, counts = _mix_out(parts, x, mod[l], norm_g[l, 1], w_router[l], b_router[l])
        x = _moe(x1, h, eid, wts, rank, counts, mod[l], l, w_gu, b_gu, w_down, b_down,
                 final_g, final=(l == DEPTH - 1))

    y_prompt, y_sample = x
    return (y_prompt.reshape(BATCH, SEQ, d), y_sample.reshape(DEC_BATCH, DEC_SEQ, d)) + new_states
```

```python
import functools

import numpy as np
import jax
import jax.numpy as jnp
from jax import lax
from jax.experimental import pallas as pl
from jax.experimental.pallas import tpu as pltpu
from jax.experimental.pallas import tpu_sc as plsc

F32 = jnp.float32
BF16 = jnp.bfloat16

D_MODEL = 1024
BATCH = 32
SEQ = 256
DEPTH = 2
DEC_BATCH = 8
DEC_SEQ = 1024
GRID_W = 64
RMS_EPS = 1e-6
N_MOD = 6
LOG2_E = 1.4426950408889634
LANES = 128

HEADS = 4
HEAD_DIM = 128
GROUP_W = HEADS * HEAD_DIM
N_GATES = 4 * HEADS
EVEN_MAIN = 9 * GROUP_W

HY_ORDER = 2
HY_BANDS = 16
HY_EMB = 1 + 2 * HY_BANDS
HY_HIDDEN = 64
HY_FREQ_CHUNK = 512
HY_STEP_ROWS = 1024

N_EXPERTS = 32
TOP_K = 4
SWIGLU_LIMIT = 7.0
SWIGLU_ALPHA = 1.702

N_PROMPT = BATCH * SEQ
N_SAMPLE = DEC_BATCH * DEC_SEQ
N_TOK = N_PROMPT + N_SAMPLE
N_COND = 16

TOK_TILE = 256
SCAN_CHUNK = 128
SUB = 16
HALF = 8
HG_STEP_HEADS = 2
MOE_TILE = 256
MOE_ROWS = N_TOK * TOP_K + N_EXPERTS * MOE_TILE
DEINT_COLS = 256
SC_STEP_BYTES = 256 * 1024
SC_MAX_INDICES = 128

VMEM_LIMIT = 56 * 1024 * 1024


def _cparams(*sem):
    return pltpu.CompilerParams(dimension_semantics=sem, vmem_limit_bytes=VMEM_LIMIT)


def _split3(x):
    hi = x.astype(BF16)
    r = x - hi.astype(F32)
    mid = r.astype(BF16)
    lo = (r - mid.astype(F32)).astype(BF16)
    return hi, mid, lo


def _dot(a, b):
    return jnp.dot(a, b, preferred_element_type=F32)


def _dot_nt(a, b):
    return lax.dot_general(a, b, (((1,), (1,)), ((), ())), preferred_element_type=F32)


def _dot_tn(a, b):
    return lax.dot_general(a, b, (((0,), (0,)), ((), ())), preferred_element_type=F32)


def _dot_w3(a_exact_bf16, x):
    hi, mid, lo = _split3(x)
    return _dot(a_exact_bf16, hi) + _dot(a_exact_bf16, mid) + _dot(a_exact_bf16, lo)


def _sigmoid(x):
    return 1.0 / (1.0 + jnp.exp(-x))


def _silu(x):
    return x * _sigmoid(x)


def _log_sigmoid(x):
    return jnp.minimum(x, 0.0) - jnp.log(1.0 + jnp.exp(-jnp.abs(x)))


def _tile_cond_row(i):
    n_prompt_tiles = N_PROMPT // TOK_TILE
    tiles_per_seq = DEC_SEQ // TOK_TILE
    return jnp.where(i < n_prompt_tiles, 0, 1 + (i - n_prompt_tiles) // tiles_per_seq)


def _mod_kernel(cond_ref, w_ref, b_ref, o_ref):
    a = _silu(cond_ref[...]).astype(BF16)
    o_ref[...] = _dot(a, w_ref[...].astype(BF16)) + b_ref[...]


def _modulation(cond, w_mod, b_mod):
    d = D_MODEL
    out = pl.pallas_call(
        _mod_kernel,
        grid=(DEPTH, N_MOD),
        in_specs=[
            pl.BlockSpec((N_COND, d), lambda l, j: (0, 0)),
            pl.BlockSpec((None, d, d), lambda l, j: (l, 0, j)),
            pl.BlockSpec((None, 1, d), lambda l, j: (l, 0, j)),
        ],
        out_specs=pl.BlockSpec((None, None, N_COND, d), lambda l, j: (l, j, 0, 0)),
        out_shape=jax.ShapeDtypeStruct((DEPTH, N_MOD, N_COND, d), F32),
        compiler_params=_cparams("parallel", "parallel"),
        name="modulation",
    )(cond, w_mod, b_mod.reshape(DEPTH, 1, N_MOD * d))
    return out.transpose(0, 2, 1, 3)


def _norm_mod(x, g_row, scale_row, shift_row):
    ms = jnp.mean(x * x, axis=-1, keepdims=True)
    y = x * lax.rsqrt(ms + RMS_EPS) * g_row
    return y * (1.0 + scale_row) + shift_row


def _proj_even_kernel(xp_ref, xs_ref, pos_ref, mod_ref, g_ref, w_ref, wgt_ref, gbt_ref,
                      xres_ref, p_ref, gate_t_ref):
    from_prompt = pl.program_id(0) < N_PROMPT // TOK_TILE
    x = jnp.where(from_prompt, xp_ref[...], xs_ref[...]) + pos_ref[...]
    xres_ref[...] = x
    h = _norm_mod(x, g_ref[...], mod_ref[1:2, :], mod_ref[0:1, :]).astype(BF16)
    p_ref[...] = _dot(h, w_ref[...])
    gate_t_ref[...] = _dot_nt(wgt_ref[...], h) + gbt_ref[...]


def _proj_even(x_prompt, x_sample, pos_tab, mod_l, norm_g, w_in, gate_b):
    d = D_MODEL
    n_tiles = N_TOK // TOK_TILE
    n_prompt_tiles = N_PROMPT // TOK_TILE
    tiles_per_seq = DEC_SEQ // TOK_TILE
    w_main = w_in[:, :EVEN_MAIN].astype(BF16)
    w_gate = w_in[:, EVEN_MAIN:].astype(BF16)

    def pos_map(i):
        return (jnp.where(i < n_prompt_tiles, 0, 1 + (i - n_prompt_tiles) % tiles_per_seq), 0)

    return pl.pallas_call(
        _proj_even_kernel,
        grid=(n_tiles,),
        in_specs=[
            pl.BlockSpec((TOK_TILE, d), lambda i: (jnp.minimum(i, n_prompt_tiles - 1), 0)),
            pl.BlockSpec((TOK_TILE, d), lambda i: (jnp.maximum(i - n_prompt_tiles, 0), 0)),
            pl.BlockSpec((TOK_TILE, d), pos_map),
            pl.BlockSpec((None, N_MOD, d), lambda i: (_tile_cond_row(i), 0, 0)),
            pl.BlockSpec((1, d), lambda i: (0, 0)),
            pl.BlockSpec((d, EVEN_MAIN), lambda i: (0, 0)),
            pl.BlockSpec((N_GATES, d), lambda i: (0, 0)),
            pl.BlockSpec((N_GATES, 1), lambda i: (0, 0)),
        ],
        out_specs=[
            pl.BlockSpec((TOK_TILE, d), lambda i: (i, 0)),
            pl.BlockSpec((TOK_TILE, EVEN_MAIN), lambda i: (i, 0)),
            pl.BlockSpec((N_GATES, TOK_TILE), lambda i: (0, i)),
        ],
        out_shape=[
            jax.ShapeDtypeStruct((N_TOK, d), F32),
            jax.ShapeDtypeStruct((N_TOK, EVEN_MAIN), F32),
            jax.ShapeDtypeStruct((N_GATES, N_TOK), F32),
        ],
        compiler_params=_cparams("parallel"),
        name="proj_even",
    )(x_prompt, x_sample, pos_tab, mod_l, norm_g.reshape(1, d), w_main, w_gate.T, gate_b.reshape(N_GATES, 1))


def _proj_odd_kernel(x_ref, mod_ref, g_ref, w_ref, p_ref):
    h = _norm_mod(x_ref[...], g_ref[...], mod_ref[1:2, :], mod_ref[0:1, :]).astype(BF16)
    p_ref[...] = _dot(h, w_ref[...])


def _proj_odd(x, mod_l, norm_g, w_in):
    d = D_MODEL
    width = w_in.shape[1]
    return pl.pallas_call(
        _proj_odd_kernel,
        grid=(N_TOK // TOK_TILE,),
        in_specs=[
            pl.BlockSpec((TOK_TILE, d), lambda i: (i, 0)),
            pl.BlockSpec((None, N_MOD, d), lambda i: (_tile_cond_row(i), 0, 0)),
            pl.BlockSpec((1, d), lambda i: (0, 0)),
            pl.BlockSpec((d, width), lambda i: (0, 0)),
        ],
        out_specs=pl.BlockSpec((TOK_TILE, width), lambda i: (i, 0)),
        out_shape=jax.ShapeDtypeStruct((N_TOK, width), F32),
        compiler_params=_cparams("parallel"),
        name="proj_odd",
    )(x, mod_l, norm_g.reshape(1, d), w_in.astype(BF16))


def _hgrn_chunks(chains):
    c = SCAN_CHUNK
    n = range(len(chains))
    rev, q, k, v, lf, st = zip(*chains)
    row = lax.broadcasted_iota(jnp.int32, (c, c), 0)
    col = lax.broadcasted_iota(jnp.int32, (c, c), 1)
    tris = {False: col <= row, True: col >= row}
    tris_b = {r: jnp.where(t, 1.0, 0.0).astype(BF16) for r, t in tris.items()}
    b = [_dot_w3(tris_b[rev[j]], lf[j]) for j in n]
    b2 = [b[j] * LOG2_E for j in n]
    lane_half = lax.broadcasted_iota(jnp.int32, (HALF, c), 1)
    chunk_row = lax.broadcasted_iota(jnp.int32, (c, 1), 0)
    rows = [[] for _ in n]
    for i in range(c // SUB):
        lo, hi = i * SUB, (i + 1) * SUB
        a_row = []
        for j in n:
            if rev[j]:
                has_off, edge, outside = hi < c, hi, chunk_row >= hi
            else:
                has_off, edge, outside = lo > 0, lo - 1, chunk_row < lo
            if has_off:
                beta = b[j][edge:edge + 1]
                qs = q[j][lo:hi] * jnp.exp(b[j][lo:hi] - beta)
                ks = k[j] * jnp.exp(jnp.where(outside, beta - b[j], -jnp.inf))
                a_row.append(_dot_nt(qs.astype(BF16), ks.astype(BF16)))
            else:
                a_row.append(jnp.zeros((SUB, c), F32))
        for half in range(SUB // HALF):
            h0 = lo + half * HALF
            piece = [a_row[j][half * HALF:(half + 1) * HALF] for j in n]
            for s in range(HALF):
                for j in n:
                    bh2 = b2[j][h0:h0 + HALF]
                    a_col = jnp.sum(jnp.exp2(bh2 - bh2[s:s + 1]) * q[j][h0:h0 + HALF] * k[j][h0 + s:h0 + s + 1],
                                    axis=-1, keepdims=True)
                    piece[j] = jnp.where(lane_half == h0 + s, a_col, piece[j])
            for j in n:
                rows[j].append(piece[j])
    second_half = (chunk_row % SUB) >= HALF
    same_block = (row // SUB) == (col // SUB)
    out = []
    for j in n:
        meet = HALF if rev[j] else HALF - 1
        beta = jnp.concatenate([jnp.broadcast_to(b[j][lo + meet:lo + meet + 1], (SUB, b[j].shape[1]))
                                for lo in range(0, c, SUB)], axis=0)
        t_side = jnp.logical_not(second_half) if rev[j] else second_half
        qs = q[j] * jnp.exp(jnp.where(t_side, b[j] - beta, -jnp.inf))
        ks = k[j] * jnp.exp(jnp.where(t_side, -jnp.inf, beta - b[j]))
        cross = jnp.where(same_block, _dot_nt(qs.astype(BF16), ks.astype(BF16)), 0.0)
        attn = jnp.where(tris[rev[j]], jnp.concatenate(rows[j], axis=0) + cross, 0.0)
        o = _dot(attn.astype(BF16), v[j].astype(BF16)) \
            + _dot_nt((q[j] * jnp.exp(b[j])).astype(BF16), st[j].astype(BF16))
        b_exit = b[j][0:1] if rev[j] else b[j][c - 1:c]
        k_out = k[j] * jnp.exp(b_exit - b[j])
        st_new = jnp.exp(b_exit) * st[j] + _dot_tn(v[j].astype(BF16), k_out.astype(BF16))
        out.append((o, st_new))
    return out


def _hgrn_kernel(seq_len, layer, q_ref, i_ref, g_ref, ff_ref, fb_ref, lb_ref, s0_ref,
                 o_ref, s_out_ref, of_ref, ob_ref, st_ref):
    c = SCAN_CHUNK
    hd = HEAD_DIM
    n_chunks = seq_len // c
    lbp = lb_ref[...]
    e = jnp.exp(lbp - jnp.max(lbp, axis=0, keepdims=True))
    lb = jnp.sum(e[0:layer + 1], axis=0, keepdims=True) / jnp.sum(e, axis=0, keepdims=True)

    for d in range(2):
        for hh in range(HG_STEP_HEADS):
            st_ref[d, hh] = s0_ref[d, hh].T

    def body(n, carry):
        where, chains = [], []
        for d in range(2):
            sl = pl.ds(pl.multiple_of((n_chunks - 1 - n if d else n) * c, c), c)
            for hh in range(HG_STEP_HEADS):
                cols = slice(hh * hd, (hh + 1) * hd)
                f = lb[:, cols] + (1.0 - lb[:, cols]) * _sigmoid((fb_ref if d else ff_ref)[sl, cols])
                where.append((d, hh, sl, cols))
                chains.append((bool(d), q_ref[sl, cols], 1.0 - f, i_ref[sl, cols], jnp.log(f), st_ref[d, hh]))
        for (d, hh, sl, cols), (o, st_new) in zip(where, _hgrn_chunks(chains)):
            st_ref[d, hh] = st_new
            (ob_ref if d else of_ref)[sl, cols] = o
        return carry

    lax.fori_loop(0, n_chunks, body, 0)
    for hh in range(HG_STEP_HEADS):
        cols = slice(hh * hd, (hh + 1) * hd)
        o = of_ref[:, cols] + ob_ref[:, cols]
        o = o * lax.rsqrt(jnp.mean(o * o, axis=-1, keepdims=True) + RMS_EPS)
        o_ref[:, cols] = o * _silu(g_ref[:, cols])
        for d in range(2):
            s_out_ref[d, hh] = st_ref[d, hh].T


def _hgrn(p, hg_lb, layer, s0, n_seq, seq_len, tok_offset):
    hd = HEAD_DIM
    sh = HG_STEP_HEADS
    steps_per_seq = HEADS // sh
    row0 = tok_offset // seq_len

    def col(part):
        return pl.BlockSpec((seq_len, sh * hd), lambda b, h: (row0 + b, part * steps_per_seq + h))

    state_spec = pl.BlockSpec((None, 2, sh, hd, hd), lambda b, h: (b, 0, h, 0, 0))
    return pl.pallas_call(
        functools.partial(_hgrn_kernel, seq_len, layer),
        grid=(n_seq, steps_per_seq),
        in_specs=[col(0), col(1), col(2), col(3), col(4),
                  pl.BlockSpec((DEPTH + 1, sh * hd), lambda b, h: (0, h)),
                  state_spec],
        out_specs=[pl.BlockSpec((seq_len, sh * hd), lambda b, h: (b, h)), state_spec],
        out_shape=[jax.ShapeDtypeStruct((n_seq * seq_len, GROUP_W), F32),
                   jax.ShapeDtypeStruct((n_seq, 2, HEADS, hd, hd), F32)],
        scratch_shapes=[pltpu.VMEM((seq_len, sh * hd), F32), pltpu.VMEM((seq_len, sh * hd), F32),
                        pltpu.VMEM((2, sh, hd, hd), F32)],
        compiler_params=_cparams("parallel", "parallel"),
        name=f"hgrn_l{seq_len}",
    )(p, p, p, p, p, hg_lb, s0)


def _short_conv3(x, w):
    n = x.shape[0]
    r = lax.broadcasted_iota(jnp.int32, (n, 1), 0)
    prev = jnp.where(r == 0, 0.0, pltpu.roll(x, 1, 0))
    nxt = jnp.where(r == n - 1, 0.0, pltpu.roll(x, n - 1, 0))
    return prev * w[0:1] + x * w[1:2] + nxt * w[2:3]


def _split2(x):
    hi = x.astype(BF16)
    return hi, (x - hi.astype(F32)).astype(BF16)


def _mlstm_chunks(chains):
    c = SCAN_CHUNK
    row = lax.broadcasted_iota(jnp.int32, (c, c), 0)
    col = lax.broadcasted_iota(jnp.int32, (c, c), 1)
    eye_b = jnp.where(row == col, 1.0, 0.0).astype(BF16)
    tris = {False: row <= col, True: row >= col}
    tris_b = {r: jnp.where(t, 1.0, 0.0).astype(BF16) for r, t in tris.items()}
    n = range(len(chains))
    rev, q, k, vt, ig, fg, ct, nv, m_prev = zip(*chains)

    def each(fn):
        return [fn(i) for i in n]

    def dot3(parts, rhs, nt=False):
        d = _dot_nt if nt else _dot
        return d(parts[0], rhs) + d(parts[1], rhs) + d(parts[2], rhs)

    lf = each(lambda i: _split3(jnp.broadcast_to(_log_sigmoid(fg[i]), (8, c))))
    b = each(lambda i: dot3(lf[i], tris_b[rev[i]])[0:1])
    us = each(lambda i: _split3(jnp.broadcast_to(ig[i] - b[i], (c, c))))
    u = each(lambda i: _dot_nt(eye_b, us[i][0]) + _dot_nt(eye_b, us[i][1]) + _dot_nt(eye_b, us[i][2]))
    dmat = each(lambda i: jnp.where(tris[rev[i]], b[i] + u[i], -jnp.inf))
    m_t = each(lambda i: jnp.maximum(b[i] + m_prev[i], jnp.max(dmat[i], axis=0, keepdims=True)))
    qb = each(lambda i: q[i].astype(BF16))
    kb = each(lambda i: k[i].astype(BF16))
    kq = each(lambda i: _dot_nt(kb[i], qb[i]))
    p = each(lambda i: jnp.exp(dmat[i] - m_t[i]) * kq[i])
    inter = each(lambda i: jnp.exp(b[i] + m_prev[i] - m_t[i]))
    ns = each(lambda i: _split2(jnp.broadcast_to(nv[i], (8, nv[i].shape[1]))))
    qs = each(lambda i: _split2(q[i]))
    qn = each(lambda i: (_dot_nt(ns[i][0], qs[i][0]) + _dot_nt(ns[i][1], qs[i][0])
                         + _dot_nt(ns[i][0], qs[i][1]))[0:1])
    den = each(lambda i: inter[i] * qn[i] + jnp.sum(p[i], axis=0, keepdims=True))
    scale = each(lambda i: 1.0 / jnp.maximum(jnp.abs(den[i]), jnp.exp(-m_t[i])))
    cq = each(lambda i: _dot_nt(ct[i].astype(BF16), qb[i]))
    vp = each(lambda i: _dot(vt[i].astype(BF16), p[i].astype(BF16)))
    ht = each(lambda i: (inter[i] * cq[i] + vp[i]) * scale[i])
    last = each(lambda i: 0 if rev[i] else c - 1)
    m_new = each(lambda i: m_t[i][:, last[i]:last[i] + 1])
    b_exit = each(lambda i: b[i][:, last[i]:last[i] + 1])
    w = each(lambda i: jnp.exp(b_exit[i] - b[i] + ig[i] - m_new[i]))
    dec = each(lambda i: jnp.exp(b_exit[i] + m_prev[i] - m_new[i]))
    vk = each(lambda i: _dot((vt[i] * w[i]).astype(BF16), kb[i]))
    ct_new = each(lambda i: dec[i] * ct[i] + vk[i])
    ws = each(lambda i: _split2(jnp.broadcast_to(w[i], (8, c))))
    ks = each(lambda i: _split2(k[i]))
    wk = each(lambda i: (_dot(ws[i][0], ks[i][0]) + _dot(ws[i][1], ks[i][0]) + _dot(ws[i][0], ks[i][1]))[0:1])
    nv_new = each(lambda i: dec[i] * nv[i] + wk[i])
    return [(ht[i], ct_new[i], nv_new[i], m_new[i]) for i in n]


def _mlstm_kernel(seq_len, q_ref, k_ref, v_ref, og_ref, gate_t_ref, cwq_ref, cwk_ref,
                  c0_ref, n0_ref, m0_ref,
                  o_ref, c_out_ref, n_out_ref, m_out_ref,
                  q2_ref, k2_ref, vt_ref, hf_ref, hb_ref, ct_ref, n_ref, m_ref):
    c = SCAN_CHUNK
    hd = HEAD_DIM
    n_chunks = seq_len // c
    q2_ref[...] = _silu(_short_conv3(q_ref[...], cwq_ref[...]))
    k2_ref[...] = _silu(_short_conv3(k_ref[...], cwk_ref[...])) * (HEAD_DIM ** -0.5)
    for hh in range(HEADS):
        cols = slice(hh * hd, (hh + 1) * hd)
        for j in range(n_chunks):
            vt_ref[hh, :, j * c:(j + 1) * c] = v_ref[j * c:(j + 1) * c, cols].T
        for d in range(2):
            ct_ref[d, hh] = c0_ref[d, hh].T
    n_ref[...] = n0_ref[...]
    m_ref[...] = m0_ref[...]

    def body(n, carry):
        where, chains = [], []
        for d in range(2):
            sl = pl.ds(pl.multiple_of((n_chunks - 1 - n if d else n) * c, c), c)
            for hh in range(HEADS):
                cols = slice(hh * hd, (hh + 1) * hd)
                gr = gate_t_ref[hh, :, sl]
                where.append((d, hh, sl))
                chains.append((bool(d), q2_ref[sl, cols], k2_ref[sl, cols], vt_ref[hh, :, sl],
                               gr[2 * d:2 * d + 1, :], gr[2 * d + 1:2 * d + 2, :],
                               ct_ref[d, hh], n_ref[d, hh], m_ref[d, hh]))
        for (d, hh, sl), (ht, ct, nv, m_new) in zip(where, _mlstm_chunks(chains)):
            ct_ref[d, hh] = ct
            n_ref[d, hh] = nv
            m_ref[d, hh] = m_new
            (hb_ref if d else hf_ref)[hh, :, sl] = ht
        return carry

    lax.fori_loop(0, n_chunks, body, 0)
    for hh in range(HEADS):
        cols = slice(hh * hd, (hh + 1) * hd)
        for j in range(n_chunks):
            rows = slice(j * c, (j + 1) * c)
            h = (hf_ref[hh, :, rows] + hb_ref[hh, :, rows]).T
            h = h * lax.rsqrt(jnp.mean(h * h, axis=-1, keepdims=True) + RMS_EPS)
            o_ref[rows, cols] = h * _sigmoid(og_ref[rows, cols])
        for d in range(2):
            c_out_ref[d, hh] = ct_ref[d, hh].T
    n_out_ref[...] = n_ref[...]
    m_out_ref[...] = m_ref[...]


def _mlstm(p, gate_t_h, conv_w, c0, n0, m0, n_seq, seq_len, tok_offset):
    hd = HEAD_DIM
    gw = GROUP_W
    row0 = tok_offset // seq_len

    def col(part):
        return pl.BlockSpec((seq_len, gw), lambda b: (row0 + b, part))

    c_spec = pl.BlockSpec((None, 2, HEADS, hd, hd), lambda b: (b, 0, 0, 0, 0))
    n_spec = pl.BlockSpec((None, 2, HEADS, 1, hd), lambda b: (b, 0, 0, 0, 0))
    m_spec = pl.BlockSpec((None, 2, HEADS, 1, 1), lambda b: (b, 0, 0, 0, 0))
    return pl.pallas_call(
        functools.partial(_mlstm_kernel, seq_len),
        grid=(n_seq,),
        in_specs=[col(5), col(6), col(7), col(8),
                  pl.BlockSpec((HEADS, 4, seq_len), lambda b: (0, 0, row0 + b)),
                  pl.BlockSpec((3, gw), lambda b: (0, 0)),
                  pl.BlockSpec((3, gw), lambda b: (0, 1)),
                  c_spec, n_spec, m_spec],
        out_specs=[pl.BlockSpec((seq_len, gw), lambda b: (b, 0)), c_spec, n_spec, m_spec],
        out_shape=[jax.ShapeDtypeStruct((n_seq * seq_len, gw), F32),
                   jax.ShapeDtypeStruct((n_seq, 2, HEADS, hd, hd), F32),
                   jax.ShapeDtypeStruct((n_seq, 2, HEADS, 1, hd), F32),
                   jax.ShapeDtypeStruct((n_seq, 2, HEADS, 1, 1), F32)],
        scratch_shapes=[pltpu.VMEM((seq_len, gw), F32), pltpu.VMEM((seq_len, gw), F32),
                        pltpu.VMEM((HEADS, hd, seq_len), F32),
                        pltpu.VMEM((HEADS, hd, seq_len), F32), pltpu.VMEM((HEADS, hd, seq_len), F32),
                        pltpu.VMEM((2, HEADS, hd, hd), F32), pltpu.VMEM((2, HEADS, 1, hd), F32),
                        pltpu.VMEM((2, HEADS, 1, 1), F32)],
        compiler_params=_cparams("parallel"),
        name=f"mlstm_l{seq_len}",
    )(p, p, p, p, gate_t_h, conv_w, conv_w, c0, n0, m0)


def _dft_tables(seq_len):
    n = 2 * seq_len
    k = jnp.arange(seq_len, dtype=jnp.int32)[:, None]
    t = jnp.arange(seq_len, dtype=jnp.int32)[None, :]
    ang = ((k * t) % n).astype(F32) * (2.0 * np.pi / n)
    fc = jnp.cos(ang)
    fs = jnp.sin(ang)
    nyq = jnp.where(t % 2 == 0, 1.0, -1.0).astype(F32)
    fs = jnp.where(k == 0, nyq, fs)
    return jnp.concatenate([fc, fs], axis=0)


def _filter_kernel(seq_len, z_ref, w1_ref, b1_ref, w2_ref, b2_ref, w3f_ref, w3b_ref, b3f_ref, b3b_ref,
                   f0_ref, f1_ref, rf_ref, rb_ref, fhi_ref, flo_ref, kf_ref, a_ref):
    hp = lax.Precision.HIGHEST
    n = 2 * seq_len
    z = z_ref[...]

    @pl.when(jnp.logical_and(pl.program_id(0) == 0, pl.program_id(1) == 0))
    def _():
        a1 = jnp.sin(f0_ref[...] * (jnp.dot(z, w1_ref[...], precision=hp, preferred_element_type=F32)
                                    + b1_ref[...]))
        a_ref[...] = jnp.sin(f1_ref[...] * (jnp.dot(a1, w2_ref[...], precision=hp, preferred_element_type=F32)
                                            + b2_ref[...]))

    a = a_ref[...]
    t_norm = z[:, 0:1]
    hf = (jnp.dot(a, w3f_ref[...], precision=hp, preferred_element_type=F32) + b3f_ref[...]) \
        * jnp.exp(-t_norm * jnp.exp(rf_ref[...]))
    hb = (jnp.dot(a, w3b_ref[...], precision=hp, preferred_element_type=F32) + b3b_ref[...]) \
        * jnp.exp(-t_norm * jnp.exp(rb_ref[...]))
    inv = lax.rsqrt(jnp.sum(hf * hf, axis=0, keepdims=True) + jnp.sum(hb * hb, axis=0, keepdims=True))
    hf = hf * inv
    r = lax.broadcasted_iota(jnp.int32, (seq_len, 1), 0)
    hb = jnp.where(r == 0, 0.0, hb * inv)
    sh, sm, sl = _split3(hf + hb)
    dh, dm, dl = _split3(hf - hb)
    fhi = fhi_ref[...]
    flo = flo_ref[...]
    kc = _dot(fhi[:seq_len], sh) + _dot(fhi[:seq_len], sm) + _dot(fhi[:seq_len], sl) \
        + _dot(flo[:seq_len], sh) + _dot(flo[:seq_len], sm)
    ks = _dot(fhi[seq_len:], dh) + _dot(fhi[seq_len:], dm) + _dot(fhi[seq_len:], dl) \
        + _dot(flo[seq_len:], dh) + _dot(flo[seq_len:], dm)
    sign = jnp.where(r % 2 == 0, 1.0, -1.0)
    k_nyq = jnp.sum(sign * (hf + hb), axis=0, keepdims=True)
    ks = jnp.where(r == 0, k_nyq, ks)
    scale = jnp.where(r == 0, 1.0 / n, 2.0 / n)
    kf_ref[0:seq_len, :] = kc * scale
    kf_ref[seq_len:n, :] = ks * scale


def _hyena_filters(seq_len, w1, b1, w2, b2, w3, b3, freq, log_rate, f_tab):
    d = D_MODEL
    cb = 256
    t = jnp.arange(seq_len, dtype=F32)
    t_norm = t / (seq_len - 1)
    bands = jnp.linspace(1e-4, HY_BANDS - 1, HY_BANDS, dtype=F32)
    ang = (2.0 * np.pi / seq_len) * t[:, None] * bands[None, :]
    z = jnp.concatenate([t_norm[:, None], jnp.cos(ang), jnp.sin(ang)], axis=-1)
    kpad = 128 - HY_EMB
    z = jnp.pad(z, ((0, 0), (0, kpad)))
    w1p = jnp.pad(w1, ((0, kpad), (0, 0)))
    f_hi = f_tab.astype(BF16)
    f_lo = (f_tab - f_hi.astype(F32)).astype(BF16)
    n_cb = d // cb
    hh = HY_HIDDEN
    row = lambda a: a.reshape(1, -1)
    const = lambda shape: pl.BlockSpec(shape, lambda o, j: (0,) * len(shape))
    fwd = lambda rows: pl.BlockSpec((rows, cb), lambda o, j: (0, o * n_cb + j))
    bwd = lambda rows: pl.BlockSpec((rows, cb), lambda o, j: (0, (HY_ORDER + o) * n_cb + j))
    return pl.pallas_call(
        functools.partial(_filter_kernel, seq_len),
        grid=(HY_ORDER, n_cb),
        in_specs=[const((seq_len, 128)), const((128, hh)), const((1, hh)), const((hh, hh)), const((1, hh)),
                  fwd(hh), bwd(hh), fwd(1), bwd(1),
                  const((1, hh)), const((1, hh)), fwd(1), bwd(1),
                  const((2 * seq_len, seq_len)), const((2 * seq_len, seq_len))],
        out_specs=pl.BlockSpec((None, 2 * seq_len, cb), lambda o, j: (o, 0, j)),
        out_shape=jax.ShapeDtypeStruct((HY_ORDER, 2 * seq_len, d), F32),
        scratch_shapes=[pltpu.VMEM((seq_len, hh), F32)],
        compiler_params=_cparams("arbitrary", "arbitrary"),
        name=f"hyena_filter_l{seq_len}",
    )(z, w1p, row(b1), w2, row(b2), w3, w3, row(b3), row(b3),
      row(freq[0]), row(freq[1]), row(log_rate), row(log_rate), f_hi, f_lo)


def _hyena_kernel(seq_len, seqs, v_ref, x1_ref, x2_ref, cwv_ref, cw1_ref, cw2_ref, bias_ref, kf_ref,
                  f_ref, ft_ref, o_ref, z_ref, zb_ref, y_ref):
    kc = min(HY_FREQ_CHUNK, seq_len)
    n_k = seq_len // kc
    r = lax.broadcasted_iota(jnp.int32, (kc, 1), 0)
    gate_refs = ((x1_ref, cw1_ref), (x2_ref, cw2_ref))
    for s in range(seqs):
        rows = slice(s * seq_len, (s + 1) * seq_len)
        z_ref[s] = _short_conv3(v_ref[rows, :], cwv_ref[...])
    for o in range(HY_ORDER):
        for s in range(seqs):
            zb_ref[s] = z_ref[s].astype(BF16)
            y_ref[s] = jnp.zeros(y_ref.shape[1:], F32)

        def freq_chunk(j, carry):
            r0 = pl.multiple_of(j * kc, kc)
            k_cos = kf_ref[o, pl.ds(r0, kc), :]
            k_sin = kf_ref[o, pl.ds(seq_len + r0, kc), :]
            real_row = jnp.logical_and(r == 0, j == 0)
            for s in range(seqs):
                a = _dot(f_ref[pl.ds(r0, kc), :], zb_ref[s])
                bm = _dot(f_ref[pl.ds(seq_len + r0, kc), :], zb_ref[s])
                yc = a * k_cos - jnp.where(real_row, 0.0, bm * k_sin)
                ys = jnp.where(real_row, bm * k_sin, a * k_sin + bm * k_cos)
                y_ref[s] += _dot(ft_ref[j], yc.astype(BF16)) + _dot(ft_ref[n_k + j], ys.astype(BF16))
            return carry

        lax.fori_loop(0, n_k, freq_chunk, 0)
        x_ref, cw_ref = gate_refs[o]
        for s in range(seqs):
            rows = slice(s * seq_len, (s + 1) * seq_len)
            gate = _short_conv3(x_ref[rows, :], cw_ref[...])
            z_ref[s] = gate * (y_ref[s] + z_ref[s] * bias_ref[o:o + 1, :])
    for s in range(seqs):
        o_ref[s * seq_len:(s + 1) * seq_len, :] = z_ref[s]


def _hyena(u, conv_w, bias, kf, f_tab, n_seq, seq_len, tok_offset):
    d = D_MODEL
    cb = 256
    n_cb = d // cb
    seqs = max(1, HY_STEP_ROWS // seq_len)
    rows = seqs * seq_len
    row0 = tok_offset // rows
    kc = min(HY_FREQ_CHUNK, seq_len)
    n_k = seq_len // kc
    f_bf = f_tab.astype(BF16)
    ft = f_bf.T.reshape(seq_len, 2 * n_k, kc).transpose(1, 0, 2)

    def part(k):
        return pl.BlockSpec((rows, cb), lambda j, b: (row0 + b, k * n_cb + j))

    def cw(k):
        return pl.BlockSpec((3, cb), lambda j, b: (0, k * n_cb + j))

    return pl.pallas_call(
        functools.partial(_hyena_kernel, seq_len, seqs),
        grid=(n_cb, n_seq // seqs),
        in_specs=[part(0), part(1), part(2), cw(0), cw(1), cw(2),
                  pl.BlockSpec((HY_ORDER, cb), lambda j, b: (0, j)),
                  pl.BlockSpec((HY_ORDER, 2 * seq_len, cb), lambda j, b: (0, 0, j)),
                  pl.BlockSpec((2 * seq_len, seq_len), lambda j, b: (0, 0)),
                  pl.BlockSpec((2 * n_k, seq_len, kc), lambda j, b: (0, 0, 0))],
        out_specs=pl.BlockSpec((rows, cb), lambda j, b: (b, j)),
        out_shape=jax.ShapeDtypeStruct((n_seq * seq_len, d), F32),
        scratch_shapes=[pltpu.VMEM((seqs, seq_len, cb), F32), pltpu.VMEM((seqs, seq_len, cb), BF16),
                        pltpu.VMEM((seqs, seq_len, cb), F32)],
        compiler_params=_cparams("parallel", "parallel"),
        name=f"hyena_l{seq_len}",
    )(u, u, u, conv_w, conv_w, conv_w, bias, kf, f_bf, ft)


def _mix_out_kernel(n_parts, *refs):
    o_refs = refs[:2 * n_parts]
    w_refs = refs[2 * n_parts:3 * n_parts]
    x_ref, mod_ref, g_ref, wr_both_ref, br_ref = refs[3 * n_parts:3 * n_parts + 5]
    x1_ref, h_ref, eid_ref, wts_ref, rank_ref, cnt_ref, run_ref = refs[3 * n_parts + 5:]
    i = pl.program_id(0)
    t = TOK_TILE
    ne = N_EXPERTS

    from_prompt = i < N_PROMPT // TOK_TILE
    y = None
    for j in range(n_parts):
        o = jnp.where(from_prompt, o_refs[2 * j][...], o_refs[2 * j + 1][...])
        yj = _dot(o.astype(BF16), w_refs[j][...])
        y = yj if y is None else y + yj
    x1 = x_ref[...] + mod_ref[2:3, :] * y
    x1_ref[...] = x1
    h = _norm_mod(x1, g_ref[...], mod_ref[4:5, :], mod_ref[3:4, :])
    h_ref[...] = _pack_bf16_halves(h)
    h_hi = h.astype(BF16)
    h_lo = (h - h_hi.astype(F32)).astype(BF16)
    hi_terms = _dot(h_hi, wr_both_ref[...])
    logits = (hi_terms[:, :LANES] + hi_terms[:, LANES:] + _dot(h_lo, wr_both_ref[:, :LANES])).T[:ne] \
        + br_ref[...]

    @pl.when(i == 0)
    def _():
        run_ref[...] = jnp.zeros_like(run_ref)

    e_iota = lax.broadcasted_iota(jnp.int32, (ne, t), 0)
    vals, eids, onehots = [], [], []
    for _k in range(TOP_K):
        m = jnp.max(logits, axis=0, keepdims=True)
        eid = jnp.min(jnp.where(logits == m, e_iota, ne), axis=0, keepdims=True)
        sel = e_iota == eid
        logits = jnp.where(sel, -jnp.inf, logits)
        onehots.append(jnp.where(sel, 1.0, 0.0))
        vals.append(m)
        eids.append(eid)
    r2 = lax.broadcasted_iota(jnp.int32, (t, t), 0)
    c2 = lax.broadcasted_iota(jnp.int32, (t, t), 1)
    before = jnp.where(r2 < c2, 1.0, 0.0).astype(BF16)
    earlier = _dot(jnp.concatenate(onehots, axis=0).astype(BF16), before)
    running = run_ref[...]
    ranks = []
    for k, onehot in enumerate(onehots):
        ranks.append(jnp.sum(onehot * (running + earlier[k * ne:(k + 1) * ne]), axis=0, keepdims=True))
        running = running + jnp.sum(onehot, axis=1, keepdims=True)
    run_ref[...] = running
    cnt_ref[...] = running
    v = jnp.concatenate(vals, axis=0)
    ex = jnp.exp(v - v[0:1])
    wts_ref[...] = ex / jnp.sum(ex, axis=0, keepdims=True)
    eid_ref[...] = jnp.concatenate(eids, axis=0)
    rank_ref[...] = jnp.concatenate(ranks, axis=0).astype(jnp.int32)


def _mix_out(parts, x, mod_l, norm_g, w_router, b_router):
    d = D_MODEL
    t = TOK_TILE
    ne = N_EXPERTS
    n_parts = len(parts)
    n_prompt_tiles = N_PROMPT // t
    wr = jnp.pad(w_router, ((0, 0), (0, LANES - ne)))
    wr_hi = wr.astype(BF16)
    wr_both = jnp.concatenate([wr_hi, (wr - wr_hi.astype(F32)).astype(BF16)], axis=1)
    in_specs = []
    for o_p, _, _ in parts:
        in_specs.append(pl.BlockSpec((t, o_p.shape[1]), lambda i: (jnp.minimum(i, n_prompt_tiles - 1), 0)))
        in_specs.append(pl.BlockSpec((t, o_p.shape[1]), lambda i: (jnp.maximum(i - n_prompt_tiles, 0), 0)))
    in_specs += [pl.BlockSpec(w.shape, lambda i: (0, 0)) for _, _, w in parts]
    in_specs += [
        pl.BlockSpec((t, d), lambda i: (i, 0)),
        pl.BlockSpec((None, N_MOD, d), lambda i: (_tile_cond_row(i), 0, 0)),
        pl.BlockSpec((1, d), lambda i: (0, 0)),
        pl.BlockSpec((d, 2 * LANES), lambda i: (0, 0)),
        pl.BlockSpec((ne, 1), lambda i: (0, 0)),
    ]
    tok_major = pl.BlockSpec((TOP_K, t), lambda i: (0, i))
    return pl.pallas_call(
        functools.partial(_mix_out_kernel, n_parts),
        grid=(N_TOK // t,),
        in_specs=in_specs,
        out_specs=[pl.BlockSpec((t, d), lambda i: (i, 0)), pl.BlockSpec((t, d // 2), lambda i: (i, 0)),
                   tok_major, tok_major, tok_major, pl.BlockSpec((ne, 1), lambda i: (0, 0))],
        out_shape=[jax.ShapeDtypeStruct((N_TOK, d), F32), jax.ShapeDtypeStruct((N_TOK, d // 2), jnp.uint32),
                   jax.ShapeDtypeStruct((TOP_K, N_TOK), jnp.int32), jax.ShapeDtypeStruct((TOP_K, N_TOK), F32),
                   jax.ShapeDtypeStruct((TOP_K, N_TOK), jnp.int32), jax.ShapeDtypeStruct((ne, 1), F32)],
        scratch_shapes=[pltpu.VMEM((ne, 1), F32)],
        compiler_params=_cparams("arbitrary"),
        name="mix_out_router",
    )(*[o for part in parts for o in part[:2]], *[w.astype(BF16) for _, _, w in parts], x, mod_l,
      norm_g.reshape(1, d), wr_both, b_router.reshape(ne, 1))


def _sc_row_gather(src, idx):
    n = idx.shape[0]
    width = src.shape[1]
    step_rows = min(SC_MAX_INDICES, SC_STEP_BYTES // (width * 4))
    sc = plsc.get_sparse_core_info()
    n_workers = sc.num_cores * sc.num_subcores
    per_worker = n // n_workers
    if n % n_workers or per_worker % step_rows:
        raise ValueError("row count must be whole SparseCore steps on every subcore")
    mesh = plsc.VectorSubcoreMesh(core_axis_name="core", subcore_axis_name="subcore")

    @functools.partial(pl.kernel, out_type=jax.ShapeDtypeStruct((n, width), src.dtype), mesh=mesh,
                       scratch_types=[pltpu.VMEM((per_worker,), jnp.int32),
                                      pltpu.VMEM((step_rows, width), src.dtype)],
                       name="sc_row_gather")
    def gather(src_hbm, idx_hbm, dst_hbm, idx_vmem, rows_vmem):
        worker = lax.axis_index("subcore") * sc.num_cores + lax.axis_index("core")
        base = worker * per_worker
        pltpu.sync_copy(idx_hbm.at[pl.ds(base, per_worker)], idx_vmem)

        @pl.loop(0, per_worker // step_rows)
        def _(c):
            pltpu.sync_copy(src_hbm.at[idx_vmem.at[pl.ds(c * step_rows, step_rows)]], rows_vmem)
            pltpu.sync_copy(rows_vmem, dst_hbm.at[pl.ds(base + c * step_rows, step_rows)])

    return gather(src, idx)


def _sc_row_scatter(src, row_of_slot):
    n = row_of_slot.shape[0]
    n_src, width = src.shape
    step_rows = min(SC_MAX_INDICES, SC_STEP_BYTES // (width * 4))
    sc = plsc.get_sparse_core_info()
    n_workers = sc.num_cores * sc.num_subcores
    per_worker = n // n_workers
    steps = per_worker // step_rows
    if n % n_workers or per_worker % step_rows or n_src % step_rows:
        raise ValueError("row counts must be whole SparseCore steps on every subcore")
    mesh = plsc.VectorSubcoreMesh(core_axis_name="core", subcore_axis_name="subcore")

    @functools.partial(pl.kernel, out_type=jax.ShapeDtypeStruct((n, width), src.dtype), mesh=mesh,
                       scratch_types=[pltpu.VMEM((steps, step_rows), jnp.int32),
                                      pltpu.VMEM((step_rows, width), src.dtype)],
                       name="sc_row_scatter")
    def scatter(src_hbm, idx_hbm, dst_hbm, idx_vmem, rows_vmem):
        worker = lax.axis_index("subcore") * sc.num_cores + lax.axis_index("core")
        pltpu.sync_copy(idx_hbm.at[worker], idx_vmem)

        @pl.loop(0, steps)
        def _(c):
            src0 = lax.rem(worker * per_worker + c * step_rows, n_src)
            pltpu.sync_copy(src_hbm.at[pl.ds(src0, step_rows)], rows_vmem)
            pltpu.sync_copy(rows_vmem, dst_hbm.at[idx_vmem.at[c]])

    return scatter(src, row_of_slot.reshape(n_workers, steps, step_rows))


def _pack_bf16_halves(x):
    w = x.shape[1] // 2
    bits = pltpu.bitcast(x.astype(BF16).astype(F32), jnp.uint32)
    return bits[:, :w] | (bits[:, w:] >> 16)


def _unpack_bf16_halves(p):
    hi = pltpu.bitcast(p & jnp.uint32(0xFFFF0000), F32).astype(BF16)
    lo = pltpu.bitcast(p << 16, F32).astype(BF16)
    return hi, lo


def _experts_kernel(layer, te_ref, first_ref, slot_ref, next_ref, nv_ref,
                    x_ref, wgu_hbm, bg_ref, bu_ref, wd_hbm, bd_ref, sel_ref,
                    y_ref, wgu_buf, wd_buf, wg_ref, wu_ref, wdb_ref, sem):
    i = pl.program_id(0)
    valid = i < nv_ref[0]
    half = DEINT_COLS // 2
    k_half = x_ref.shape[1]

    def fetch(expert, slot):
        return (pltpu.make_async_copy(wgu_hbm.at[layer, expert], wgu_buf.at[slot], sem.at[slot, 0]),
                pltpu.make_async_copy(wd_hbm.at[layer, expert], wd_buf.at[slot], sem.at[slot, 1]))

    @pl.when(i == 0)
    def _():
        for cp in fetch(te_ref[0], 0):
            cp.start()

    @pl.when(jnp.logical_and(valid, first_ref[i] == 1))
    def _():
        slot = slot_ref[i]
        for cp in fetch(te_ref[i], slot):
            cp.wait()

        @pl.when(next_ref[i] >= 0)
        def _():
            for cp in fetch(next_ref[i], 1 - slot):
                cp.start(priority=1)

        for c in range(wgu_buf.shape[2] // DEINT_COLS):
            w = wgu_buf[slot, :, c * DEINT_COLS:(c + 1) * DEINT_COLS].astype(BF16)
            split = _dot(w, sel_ref[...]).astype(BF16)
            wg_ref[:, c * half:(c + 1) * half] = split[:, :half]
            wu_ref[:, c * half:(c + 1) * half] = split[:, half:]
        wdb_ref[...] = wd_buf[slot].astype(BF16)

    @pl.when(valid)
    def _():
        x_l, x_r = _unpack_bf16_halves(x_ref[...])
        gl = _dot(x_l, wg_ref[:k_half, :]) + _dot(x_r, wg_ref[k_half:, :]) + bg_ref[...]
        up = _dot(x_l, wu_ref[:k_half, :]) + _dot(x_r, wu_ref[k_half:, :]) + bu_ref[...]
        gl = jnp.minimum(gl, SWIGLU_LIMIT)
        up = jnp.clip(up, -SWIGLU_LIMIT, SWIGLU_LIMIT)
        act = (up + 1.0) * gl * _sigmoid(SWIGLU_ALPHA * gl)
        y_ref[...] = _pack_bf16_halves(_dot(act.astype(BF16), wdb_ref[...]) + bd_ref[...])

    @pl.when(jnp.logical_not(valid))
    def _():
        y_ref[...] = jnp.zeros_like(y_ref)


def _experts(xs, tile_expert, tile_first, n_valid, layer, w_gu, b_gate, b_up, w_down, b_down):
    d = D_MODEL
    tm = MOE_TILE
    n_tiles = MOE_ROWS // tm
    ff = w_down.shape[2]
    half = DEINT_COLS // 2
    r = jnp.arange(DEINT_COLS)[:, None]
    c = jnp.arange(DEINT_COLS)[None, :]
    sel = (r == jnp.where(c < half, 2 * c, 2 * (c - half) + 1)).astype(BF16)
    group = jnp.cumsum(tile_first) - 1
    tile_slot = (group % 2).astype(jnp.int32)
    is_last_group = group == group[-1]
    following = jnp.concatenate([tile_expert[1:], tile_expert[-1:]])
    idx = jnp.arange(n_tiles, dtype=jnp.int32)
    group_end = jnp.max(jnp.where(group[None, :] == group[:, None], idx[None, :], -1), axis=1)
    tile_next = jnp.where(is_last_group, -1, following[group_end]).astype(jnp.int32)
    wspec = lambda k, n: pl.BlockSpec((None, None, k, n), lambda i, *_: (layer, _[0][i], 0, 0))
    grid_spec = pltpu.PrefetchScalarGridSpec(
        num_scalar_prefetch=5,
        grid=(n_tiles,),
        in_specs=[pl.BlockSpec((tm, d // 2), lambda i, *_: (i, 0)),
                  pl.BlockSpec(memory_space=pl.ANY), wspec(1, ff), wspec(1, ff),
                  pl.BlockSpec(memory_space=pl.ANY), wspec(1, d),
                  pl.BlockSpec((DEINT_COLS, DEINT_COLS), lambda i, *_: (0, 0))],
        out_specs=pl.BlockSpec((tm, d // 2), lambda i, *_: (i, 0)),
        scratch_shapes=[pltpu.VMEM((2, d, 2 * ff), F32), pltpu.VMEM((2, ff, d), F32),
                        pltpu.VMEM((d, ff), BF16), pltpu.VMEM((d, ff), BF16), pltpu.VMEM((ff, d), BF16),
                        pltpu.SemaphoreType.DMA((2, 2))],
    )
    return pl.pallas_call(
        functools.partial(_experts_kernel, layer),
        grid_spec=grid_spec,
        out_shape=jax.ShapeDtypeStruct((MOE_ROWS, d // 2), jnp.uint32),
        compiler_params=_cparams("arbitrary"),
        name="experts",
    )(tile_expert, tile_first, tile_slot, tile_next, n_valid, xs, w_gu, b_gate, b_up, w_down, b_down, sel)


def _combine_kernel(final, x_ref, g_ref, w_ref, mod_ref, fg_ref, *o_refs):
    y_l = y_r = None
    for k in range(TOP_K):
        g_l, g_r = _unpack_bf16_halves(g_ref[k])
        wk = w_ref[:, k:k + 1]
        y_l = wk * g_l.astype(F32) if y_l is None else y_l + wk * g_l.astype(F32)
        y_r = wk * g_r.astype(F32) if y_r is None else y_r + wk * g_r.astype(F32)
    x = x_ref[...] + mod_ref[5:6, :] * jnp.concatenate([y_l, y_r], axis=1)
    if not final:
        o_refs[0][...] = x
        return
    x = x * lax.rsqrt(jnp.mean(x * x, axis=-1, keepdims=True) + RMS_EPS) * fg_ref[...]
    from_prompt = pl.program_id(0) < N_PROMPT // TOK_TILE

    @pl.when(from_prompt)
    def _():
        o_refs[0][...] = x

    @pl.when(jnp.logical_not(from_prompt))
    def _():
        o_refs[1][...] = x


def _combine(x1, gathered, wts, mod_l, final_g, final):
    d = D_MODEL
    t = TOK_TILE
    n_prompt_tiles = N_PROMPT // t
    if final:
        out_specs = [pl.BlockSpec((t, d), lambda i: (jnp.minimum(i, n_prompt_tiles - 1), 0)),
                     pl.BlockSpec((t, d), lambda i: (jnp.maximum(i - n_prompt_tiles, 0), 0))]
        out_shape = [jax.ShapeDtypeStruct((N_PROMPT, d), F32), jax.ShapeDtypeStruct((N_SAMPLE, d), F32)]
    else:
        out_specs = pl.BlockSpec((t, d), lambda i: (i, 0))
        out_shape = jax.ShapeDtypeStruct((N_TOK, d), F32)
    return pl.pallas_call(
        functools.partial(_combine_kernel, final),
        grid=(N_TOK // t,),
        in_specs=[pl.BlockSpec((t, d), lambda i: (i, 0)),
                  pl.BlockSpec((TOP_K, t, d // 2), lambda i: (0, i, 0)),
                  pl.BlockSpec((t, TOP_K), lambda i: (i, 0)),
                  pl.BlockSpec((None, N_MOD, d), lambda i: (_tile_cond_row(i), 0, 0)),
                  pl.BlockSpec((1, d), lambda i: (0, 0))],
        out_specs=out_specs,
        out_shape=out_shape,
        compiler_params=_cparams("arbitrary"),
        name="moe_combine",
    )(x1, gathered, wts, mod_l, final_g.reshape(1, d))


def _moe(x1, h, eid, wts, rank, counts, mod_l, layer, w_gu, b_gu, w_down, b_down, final_g, final):
    d = D_MODEL
    tm = MOE_TILE
    n_tiles = MOE_ROWS // tm
    cnt = counts.reshape(N_EXPERTS).astype(jnp.int32)
    gsz = ((cnt + tm - 1) // tm) * tm
    ends = jnp.cumsum(gsz)
    offs = ends - gsz
    e_ids = jnp.arange(N_EXPERTS, dtype=jnp.int32)
    pos = jnp.sum(jnp.where(eid[..., None] == e_ids, offs, 0), axis=-1) + rank
    tile_start = jnp.arange(n_tiles, dtype=jnp.int32) * tm
    tile_expert = jnp.minimum(jnp.sum((ends[None, :] <= tile_start[:, None]).astype(jnp.int32), axis=1),
                              N_EXPERTS - 1)
    n_valid = (ends[-1:] // tm).astype(jnp.int32)
    last_valid = jnp.maximum(n_valid[0] - 1, 0)
    tile_expert = jnp.where(jnp.arange(n_tiles) < n_valid[0], tile_expert, tile_expert[last_valid])
    tile_first = jnp.concatenate([jnp.ones((1,), jnp.int32),
                                  (tile_expert[1:] != tile_expert[:-1]).astype(jnp.int32)])
    j = jnp.arange(tm, dtype=jnp.int32)[None, :]
    pad_used = j < (gsz - cnt)[:, None]
    n_unused_before = jnp.cumsum((~pad_used).reshape(-1).astype(jnp.int32)) - 1
    pad_pos = jnp.where(pad_used, (offs + cnt)[:, None] + j,
                        ends[-1] + n_unused_before.reshape(N_EXPERTS, tm))
    row_of_slot = jnp.concatenate([pos.reshape(-1), pad_pos.reshape(-1)])

    xs = _sc_row_scatter(h, row_of_slot)
    ys = _experts(xs, tile_expert, tile_first, n_valid, layer, w_gu,
                  b_gu[:, :, None, 0::2], b_gu[:, :, None, 1::2], w_down, b_down[:, :, None, :])
    gathered = _sc_row_gather(ys, pos.reshape(-1))
    return _combine(x1, gathered.reshape(TOP_K, N_TOK, d // 2), wts.T, mod_l, final_g, final)


def _grid_positions(n_tok, d):
    rows = n_tok // GRID_W
    r, col = jnp.meshgrid(jnp.arange(rows, dtype=F32), jnp.arange(GRID_W, dtype=F32), indexing='ij')
    r = r.reshape(-1)
    col = col.reshape(-1)
    quarter = d // 4
    inv = 1.0 / (10000.0 ** (jnp.arange(quarter, dtype=F32) / quarter))
    ar = r[:, None] * inv[None]
    ac = col[:, None] * inv[None]
    return jnp.concatenate([jnp.sin(ar), jnp.cos(ar), jnp.sin(ac), jnp.cos(ac)], axis=-1)


def kernel(x_prompt, x_sample, state_hgrn, state_mlstm_c, state_mlstm_n, state_mlstm_m, c, c_ctx,
           norm_g, final_g, w_mod, b_mod, ev_w_in, ev_gate_b, ev_conv, hg_lb, ev_w_out,
           hy_w_in, hy_conv, hy_w1, hy_b1, hy_w2, hy_b2, hy_w3, hy_b3, hy_freq, hy_log_rate, hy_bias, hy_w_out,
           w_router, b_router, w_gu, b_gu, w_down, b_down):
    d = D_MODEL
    hd = HEAD_DIM
    cond = jnp.concatenate([c_ctx[None], c, jnp.zeros((N_COND - 1 - DEC_BATCH, d), F32)], axis=0)
    mod = _modulation(cond, w_mod, b_mod)
    pos_tab = jnp.concatenate([jnp.zeros((TOK_TILE, d), F32), _grid_positions(DEC_SEQ, d)], axis=0)

    groups = ((BATCH, SEQ, 0), (DEC_BATCH, DEC_SEQ, N_PROMPT))
    new_states = None
    for l in range(DEPTH):
        if l % 2 == 0:
            e = l // 2
            if l == 0:
                x, p, gate_t = _proj_even(x_prompt.reshape(N_PROMPT, d), x_sample.reshape(N_SAMPLE, d),
                                          pos_tab, mod[l], norm_g[l, 0], ev_w_in[e], ev_gate_b[e])
            else:
                raise NotImplementedError("only the first layer adds grid positions")
            gate_t_h = gate_t.reshape(4, HEADS, N_TOK).transpose(1, 0, 2)
            o_hg, o_ml = [], []
            for gi, (n_seq, seq_len, off) in enumerate(groups):
                if gi == 0:
                    s0 = jnp.zeros((n_seq, 2, HEADS, hd, hd), F32)
                    c0 = jnp.zeros((n_seq, 2, HEADS, hd, hd), F32)
                    n0 = jnp.zeros((n_seq, 2, HEADS, 1, hd), F32)
                    m0 = jnp.zeros((n_seq, 2, HEADS, 1, 1), F32)
                else:
                    s0 = state_hgrn[:, e]
                    c0 = state_mlstm_c[:, e]
                    n0 = state_mlstm_n[:, e].reshape(n_seq, 2, HEADS, 1, hd)
                    m0 = state_mlstm_m[:, e].reshape(n_seq, 2, HEADS, 1, 1)
                og, s_fin = _hgrn(p, hg_lb, l, s0, n_seq, seq_len, off)
                om, c_fin, n_fin, m_fin = _mlstm(p, gate_t_h, ev_conv[e], c0, n0, m0, n_seq, seq_len, off)
                o_hg.append(og)
                o_ml.append(om)
                if gi == 0:
                    new_states = (s_fin[:, None], c_fin[:, None],
                                  n_fin.reshape(n_seq, 1, 2, HEADS, hd), m_fin.reshape(n_seq, 1, 2, HEADS))
            parts = [(*o_hg, ev_w_out[e][:GROUP_W]), (*o_ml, ev_w_out[e][GROUP_W:])]
        else:
            o = l // 2
            u = _proj_odd(x, mod[l], norm_g[l, 0], hy_w_in[o])
            zs = []
            for n_seq, seq_len, off in groups:
                f_tab = _dft_tables(seq_len)
                kf = _hyena_filters(seq_len, hy_w1[o], hy_b1[o], hy_w2[o], hy_b2[o], hy_w3[o], hy_b3[o],
                                    hy_freq[o], hy_log_rate[o], f_tab)
                zs.append(_hyena(u, hy_conv[o], hy_bias[o], kf, f_tab, n_seq, seq_len, off))
            parts = [(*zs, hy_w_out[o])]
        x1, h, eid, wts, rank, counts = _mix_out(parts, x, mod[l], norm_g[l, 1], w_router[l], b_router[l])
        x = _moe(x1, h, eid, wts, rank, counts, mod[l], l, w_gu, b_gu, w_down, b_down,
                 final_g, final=(l == DEPTH - 1))

    y_prompt, y_sample = x
    return (y_prompt.reshape(BATCH, SEQ, d), y_sample.reshape(DEC_BATCH, DEC_SEQ, d)) + new_states
```

```python
import functools

import numpy as np
import jax
import jax.numpy as jnp
from jax import lax
from jax.experimental import pallas as pl
from jax.experimental.pallas import tpu as pltpu
from jax.experimental.pallas import tpu_sc as plsc

F32 = jnp.float32
BF16 = jnp.bfloat16

D_MODEL = 1024
BATCH = 32
SEQ = 256
DEPTH = 2
DEC_BATCH = 8
DEC_SEQ = 1024
GRID_W = 64
RMS_EPS = 1e-6
N_MOD = 6
LOG2_E = 1.4426950408889634
LANES = 128

HEADS = 4
HEAD_DIM = 128
GROUP_W = HEADS * HEAD_DIM
N_GATES = 4 * HEADS
EVEN_MAIN = 9 * GROUP_W

HY_ORDER = 2
HY_BANDS = 16
HY_EMB = 1 + 2 * HY_BANDS
HY_HIDDEN = 64
HY_FREQ_CHUNK = 512
HY_STEP_ROWS = 1024

N_EXPERTS = 32
TOP_K = 4
SWIGLU_LIMIT = 7.0
SWIGLU_ALPHA = 1.702

N_PROMPT = BATCH * SEQ
N_SAMPLE = DEC_BATCH * DEC_SEQ
N_TOK = N_PROMPT + N_SAMPLE
N_COND = 16

TOK_TILE = 256
SCAN_CHUNK = 128
SUB = 16
HALF = 8
HG_STEP_HEADS = 4
MOE_TILE = 512
MOE_ROWS = N_TOK * TOP_K + N_EXPERTS * MOE_TILE
DEINT_COLS = 256
SC_STEP_BYTES = 256 * 1024
SC_MAX_INDICES = 128

VMEM_LIMIT = 56 * 1024 * 1024


def _cparams(*sem):
    return pltpu.CompilerParams(dimension_semantics=sem, vmem_limit_bytes=VMEM_LIMIT)


def _split3(x):
    hi = x.astype(BF16)
    r = x - hi.astype(F32)
    mid = r.astype(BF16)
    lo = (r - mid.astype(F32)).astype(BF16)
    return hi, mid, lo


def _dot(a, b):
    return jnp.dot(a, b, preferred_element_type=F32)


def _dot_nt(a, b):
    return lax.dot_general(a, b, (((1,), (1,)), ((), ())), preferred_element_type=F32)


def _dot_tn(a, b):
    return lax.dot_general(a, b, (((0,), (0,)), ((), ())), preferred_element_type=F32)


def _dot_w3(a_exact_bf16, x):
    hi, mid, lo = _split3(x)
    return _dot(a_exact_bf16, hi) + _dot(a_exact_bf16, mid) + _dot(a_exact_bf16, lo)


def _sigmoid(x):
    return 1.0 / (1.0 + jnp.exp(-x))


def _silu(x):
    return x * _sigmoid(x)


def _log_sigmoid(x):
    return jnp.minimum(x, 0.0) - jnp.log(1.0 + jnp.exp(-jnp.abs(x)))


def _tile_cond_row(i):
    n_prompt_tiles = N_PROMPT // TOK_TILE
    tiles_per_seq = DEC_SEQ // TOK_TILE
    return jnp.where(i < n_prompt_tiles, 0, 1 + (i - n_prompt_tiles) // tiles_per_seq)


def _mod_kernel(cond_ref, w_ref, b_ref, o_ref):
    a = _silu(cond_ref[...]).astype(BF16)
    o_ref[...] = _dot(a, w_ref[...].astype(BF16)) + b_ref[...]


def _modulation(cond, w_mod, b_mod):
    d = D_MODEL
    out = pl.pallas_call(
        _mod_kernel,
        grid=(DEPTH, N_MOD),
        in_specs=[
            pl.BlockSpec((N_COND, d), lambda l, j: (0, 0)),
            pl.BlockSpec((None, d, d), lambda l, j: (l, 0, j)),
            pl.BlockSpec((None, 1, d), lambda l, j: (l, 0, j)),
        ],
        out_specs=pl.BlockSpec((None, None, N_COND, d), lambda l, j: (l, j, 0, 0)),
        out_shape=jax.ShapeDtypeStruct((DEPTH, N_MOD, N_COND, d), F32),
        compiler_params=_cparams("parallel", "parallel"),
        name="modulation",
    )(cond, w_mod, b_mod.reshape(DEPTH, 1, N_MOD * d))
    return out.transpose(0, 2, 1, 3)


def _norm_mod(x, g_row, scale_row, shift_row):
    ms = jnp.mean(x * x, axis=-1, keepdims=True)
    y = x * lax.rsqrt(ms + RMS_EPS) * g_row
    return y * (1.0 + scale_row) + shift_row


def _proj_even_kernel(xp_ref, xs_ref, pos_ref, mod_ref, g_ref, w_ref, wgt_ref, gbt_ref,
                      xres_ref, p_ref, gate_t_ref):
    from_prompt = pl.program_id(0) < N_PROMPT // TOK_TILE
    x = jnp.where(from_prompt, xp_ref[...], xs_ref[...]) + pos_ref[...]
    xres_ref[...] = x
    h = _norm_mod(x, g_ref[...], mod_ref[1:2, :], mod_ref[0:1, :]).astype(BF16)
    p_ref[...] = _dot(h, w_ref[...])
    gate_t_ref[...] = _dot_nt(wgt_ref[...], h) + gbt_ref[...]


def _proj_even(x_prompt, x_sample, pos_tab, mod_l, norm_g, w_in, gate_b):
    d = D_MODEL
    n_tiles = N_TOK // TOK_TILE
    n_prompt_tiles = N_PROMPT // TOK_TILE
    tiles_per_seq = DEC_SEQ // TOK_TILE
    w_main = w_in[:, :EVEN_MAIN].astype(BF16)
    w_gate = w_in[:, EVEN_MAIN:].astype(BF16)

    def pos_map(i):
        return (jnp.where(i < n_prompt_tiles, 0, 1 + (i - n_prompt_tiles) % tiles_per_seq), 0)

    return pl.pallas_call(
        _proj_even_kernel,
        grid=(n_tiles,),
        in_specs=[
            pl.BlockSpec((TOK_TILE, d), lambda i: (jnp.minimum(i, n_prompt_tiles - 1), 0)),
            pl.BlockSpec((TOK_TILE, d), lambda i: (jnp.maximum(i - n_prompt_tiles, 0), 0)),
            pl.BlockSpec((TOK_TILE, d), pos_map),
            pl.BlockSpec((None, N_MOD, d), lambda i: (_tile_cond_row(i), 0, 0)),
            pl.BlockSpec((1, d), lambda i: (0, 0)),
            pl.BlockSpec((d, EVEN_MAIN), lambda i: (0, 0)),
            pl.BlockSpec((N_GATES, d), lambda i: (0, 0)),
            pl.BlockSpec((N_GATES, 1), lambda i: (0, 0)),
        ],
        out_specs=[
            pl.BlockSpec((TOK_TILE, d), lambda i: (i, 0)),
            pl.BlockSpec((TOK_TILE, EVEN_MAIN), lambda i: (i, 0)),
            pl.BlockSpec((N_GATES, TOK_TILE), lambda i: (0, i)),
        ],
        out_shape=[
            jax.ShapeDtypeStruct((N_TOK, d), F32),
            jax.ShapeDtypeStruct((N_TOK, EVEN_MAIN), F32),
            jax.ShapeDtypeStruct((N_GATES, N_TOK), F32),
        ],
        compiler_params=_cparams("parallel"),
        name="proj_even",
    )(x_prompt, x_sample, pos_tab, mod_l, norm_g.reshape(1, d), w_main, w_gate.T, gate_b.reshape(N_GATES, 1))


def _proj_odd_kernel(x_ref, mod_ref, g_ref, w_ref, p_ref):
    h = _norm_mod(x_ref[...], g_ref[...], mod_ref[1:2, :], mod_ref[0:1, :]).astype(BF16)
    p_ref[...] = _dot(h, w_ref[...])


def _proj_odd(x, mod_l, norm_g, w_in):
    d = D_MODEL
    width = w_in.shape[1]
    return pl.pallas_call(
        _proj_odd_kernel,
        grid=(N_TOK // TOK_TILE,),
        in_specs=[
            pl.BlockSpec((TOK_TILE, d), lambda i: (i, 0)),
            pl.BlockSpec((None, N_MOD, d), lambda i: (_tile_cond_row(i), 0, 0)),
            pl.BlockSpec((1, d), lambda i: (0, 0)),
            pl.BlockSpec((d, width), lambda i: (0, 0)),
        ],
        out_specs=pl.BlockSpec((TOK_TILE, width), lambda i: (i, 0)),
        out_shape=jax.ShapeDtypeStruct((N_TOK, width), F32),
        compiler_params=_cparams("parallel"),
        name="proj_odd",
    )(x, mod_l, norm_g.reshape(1, d), w_in.astype(BF16))


def _hgrn_chunks(chains):
    c = SCAN_CHUNK
    n = range(len(chains))
    rev, q, k, v, lf, st = zip(*chains)
    row = lax.broadcasted_iota(jnp.int32, (c, c), 0)
    col = lax.broadcasted_iota(jnp.int32, (c, c), 1)
    tris = {False: col <= row, True: col >= row}
    tris_b = {r: jnp.where(t, 1.0, 0.0).astype(BF16) for r, t in tris.items()}
    b = [_dot_w3(tris_b[rev[j]], lf[j]) for j in n]
    b2 = [b[j] * LOG2_E for j in n]
    lane_half = lax.broadcasted_iota(jnp.int32, (HALF, c), 1)
    chunk_row = lax.broadcasted_iota(jnp.int32, (c, 1), 0)
    rows = [[] for _ in n]
    for i in range(c // SUB):
        lo, hi = i * SUB, (i + 1) * SUB
        a_row = []
        for j in n:
            if rev[j]:
                has_off, edge, outside = hi < c, hi, chunk_row >= hi
            else:
                has_off, edge, outside = lo > 0, lo - 1, chunk_row < lo
            if has_off:
                beta = b[j][edge:edge + 1]
                qs = q[j][lo:hi] * jnp.exp(b[j][lo:hi] - beta)
                ks = k[j] * jnp.exp(jnp.where(outside, beta - b[j], -jnp.inf))
                a_row.append(_dot_nt(qs.astype(BF16), ks.astype(BF16)))
            else:
                a_row.append(jnp.zeros((SUB, c), F32))
        for half in range(SUB // HALF):
            h0 = lo + half * HALF
            piece = [a_row[j][half * HALF:(half + 1) * HALF] for j in n]
            for s in range(HALF):
                for j in n:
                    bh2 = b2[j][h0:h0 + HALF]
                    a_col = jnp.sum(jnp.exp2(bh2 - bh2[s:s + 1]) * q[j][h0:h0 + HALF] * k[j][h0 + s:h0 + s + 1],
                                    axis=-1, keepdims=True)
                    piece[j] = jnp.where(lane_half == h0 + s, a_col, piece[j])
            for j in n:
                rows[j].append(piece[j])
    second_half = (chunk_row % SUB) >= HALF
    same_block = (row // SUB) == (col // SUB)
    out = []
    for j in n:
        meet = HALF if rev[j] else HALF - 1
        beta = jnp.concatenate([jnp.broadcast_to(b[j][lo + meet:lo + meet + 1], (SUB, b[j].shape[1]))
                                for lo in range(0, c, SUB)], axis=0)
        t_side = jnp.logical_not(second_half) if rev[j] else second_half
        qs = q[j] * jnp.exp(jnp.where(t_side, b[j] - beta, -jnp.inf))
        ks = k[j] * jnp.exp(jnp.where(t_side, -jnp.inf, beta - b[j]))
        cross = jnp.where(same_block, _dot_nt(qs.astype(BF16), ks.astype(BF16)), 0.0)
        attn = jnp.where(tris[rev[j]], jnp.concatenate(rows[j], axis=0) + cross, 0.0)
        o = _dot(attn.astype(BF16), v[j].astype(BF16)) \
            + _dot_nt((q[j] * jnp.exp(b[j])).astype(BF16), st[j].astype(BF16))
        b_exit = b[j][0:1] if rev[j] else b[j][c - 1:c]
        k_out = k[j] * jnp.exp(b_exit - b[j])
        st_new = jnp.exp(b_exit) * st[j] + _dot_tn(v[j].astype(BF16), k_out.astype(BF16))
        out.append((o, st_new))
    return out


def _hgrn_kernel(seq_len, layer, q_ref, i_ref, g_ref, ff_ref, fb_ref, lb_ref, s0_ref,
                 o_ref, s_out_ref, of_ref, ob_ref, st_ref):
    c = SCAN_CHUNK
    hd = HEAD_DIM
    n_chunks = seq_len // c
    lbp = lb_ref[...]
    e = jnp.exp(lbp - jnp.max(lbp, axis=0, keepdims=True))
    lb = jnp.sum(e[0:layer + 1], axis=0, keepdims=True) / jnp.sum(e, axis=0, keepdims=True)

    for d in range(2):
        for hh in range(HG_STEP_HEADS):
            st_ref[d, hh] = s0_ref[d, hh].T

    def body(n, carry):
        where, chains = [], []
        for d in range(2):
            sl = pl.ds(pl.multiple_of((n_chunks - 1 - n if d else n) * c, c), c)
            for hh in range(HG_STEP_HEADS):
                cols = slice(hh * hd, (hh + 1) * hd)
                f = lb[:, cols] + (1.0 - lb[:, cols]) * _sigmoid((fb_ref if d else ff_ref)[sl, cols])
                where.append((d, hh, sl, cols))
                chains.append((bool(d), q_ref[sl, cols], 1.0 - f, i_ref[sl, cols], jnp.log(f), st_ref[d, hh]))
        for (d, hh, sl, cols), (o, st_new) in zip(where, _hgrn_chunks(chains)):
            st_ref[d, hh] = st_new
            (ob_ref if d else of_ref)[sl, cols] = o
        return carry

    lax.fori_loop(0, n_chunks, body, 0)
    for hh in range(HG_STEP_HEADS):
        cols = slice(hh * hd, (hh + 1) * hd)
        o = of_ref[:, cols] + ob_ref[:, cols]
        o = o * lax.rsqrt(jnp.mean(o * o, axis=-1, keepdims=True) + RMS_EPS)
        o_ref[:, cols] = o * _silu(g_ref[:, cols])
        for d in range(2):
            s_out_ref[d, hh] = st_ref[d, hh].T


def _hgrn(p, hg_lb, layer, s0, n_seq, seq_len, tok_offset):
    hd = HEAD_DIM
    sh = HG_STEP_HEADS
    steps_per_seq = HEADS // sh
    row0 = tok_offset // seq_len

    def col(part):
        return pl.BlockSpec((seq_len, sh * hd), lambda b, h: (row0 + b, part * steps_per_seq + h))

    state_spec = pl.BlockSpec((None, 2, sh, hd, hd), lambda b, h: (b, 0, h, 0, 0))
    return pl.pallas_call(
        functools.partial(_hgrn_kernel, seq_len, layer),
        grid=(n_seq, steps_per_seq),
        in_specs=[col(0), col(1), col(2), col(3), col(4),
                  pl.BlockSpec((DEPTH + 1, sh * hd), lambda b, h: (0, h)),
                  state_spec],
        out_specs=[pl.BlockSpec((seq_len, sh * hd), lambda b, h: (b, h)), state_spec],
        out_shape=[jax.ShapeDtypeStruct((n_seq * seq_len, GROUP_W), F32),
                   jax.ShapeDtypeStruct((n_seq, 2, HEADS, hd, hd), F32)],
        scratch_shapes=[pltpu.VMEM((seq_len, sh * hd), F32), pltpu.VMEM((seq_len, sh * hd), F32),
                        pltpu.VMEM((2, sh, hd, hd), F32)],
        compiler_params=_cparams("parallel", "parallel"),
        name=f"hgrn_l{seq_len}",
    )(p, p, p, p, p, hg_lb, s0)


def _short_conv3(x, w):
    n = x.shape[0]
    r = lax.broadcasted_iota(jnp.int32, (n, 1), 0)
    prev = jnp.where(r == 0, 0.0, pltpu.roll(x, 1, 0))
    nxt = jnp.where(r == n - 1, 0.0, pltpu.roll(x, n - 1, 0))
    return prev * w[0:1] + x * w[1:2] + nxt * w[2:3]


def _split2(x):
    hi = x.astype(BF16)
    return hi, (x - hi.astype(F32)).astype(BF16)


def _mlstm_chunks(chains):
    c = SCAN_CHUNK
    row = lax.broadcasted_iota(jnp.int32, (c, c), 0)
    col = lax.broadcasted_iota(jnp.int32, (c, c), 1)
    eye_b = jnp.where(row == col, 1.0, 0.0).astype(BF16)
    tris = {False: row <= col, True: row >= col}
    tris_b = {r: jnp.where(t, 1.0, 0.0).astype(BF16) for r, t in tris.items()}
    n = range(len(chains))
    rev, q, k, vt, ig, fg, ct, nv, m_prev = zip(*chains)

    def each(fn):
        return [fn(i) for i in n]

    def dot3(parts, rhs, nt=False):
        d = _dot_nt if nt else _dot
        return d(parts[0], rhs) + d(parts[1], rhs) + d(parts[2], rhs)

    lf = each(lambda i: _split3(jnp.broadcast_to(_log_sigmoid(fg[i]), (8, c))))
    b = each(lambda i: dot3(lf[i], tris_b[rev[i]])[0:1])
    us = each(lambda i: _split3(jnp.broadcast_to(ig[i] - b[i], (c, c))))
    u = each(lambda i: _dot_nt(eye_b, us[i][0]) + _dot_nt(eye_b, us[i][1]) + _dot_nt(eye_b, us[i][2]))
    dmat = each(lambda i: jnp.where(tris[rev[i]], b[i] + u[i], -jnp.inf))
    m_t = each(lambda i: jnp.maximum(b[i] + m_prev[i], jnp.max(dmat[i], axis=0, keepdims=True)))
    qb = each(lambda i: q[i].astype(BF16))
    kb = each(lambda i: k[i].astype(BF16))
    kq = each(lambda i: _dot_nt(kb[i], qb[i]))
    p = each(lambda i: jnp.exp(dmat[i] - m_t[i]) * kq[i])
    inter = each(lambda i: jnp.exp(b[i] + m_prev[i] - m_t[i]))
    ns = each(lambda i: _split2(jnp.broadcast_to(nv[i], (8, nv[i].shape[1]))))
    qs = each(lambda i: _split2(q[i]))
    qn = each(lambda i: (_dot_nt(ns[i][0], qs[i][0]) + _dot_nt(ns[i][1], qs[i][0])
                         + _dot_nt(ns[i][0], qs[i][1]))[0:1])
    den = each(lambda i: inter[i] * qn[i] + jnp.sum(p[i], axis=0, keepdims=True))
    scale = each(lambda i: 1.0 / jnp.maximum(jnp.abs(den[i]), jnp.exp(-m_t[i])))
    cq = each(lambda i: _dot_nt(ct[i].astype(BF16), qb[i]))
    vp = each(lambda i: _dot(vt[i].astype(BF16), p[i].astype(BF16)))
    ht = each(lambda i: (inter[i] * cq[i] + vp[i]) * scale[i])
    last = each(lambda i: 0 if rev[i] else c - 1)
    m_new = each(lambda i: m_t[i][:, last[i]:last[i] + 1])
    b_exit = each(lambda i: b[i][:, last[i]:last[i] + 1])
    w = each(lambda i: jnp.exp(b_exit[i] - b[i] + ig[i] - m_new[i]))
    dec = each(lambda i: jnp.exp(b_exit[i] + m_prev[i] - m_new[i]))
    vk = each(lambda i: _dot((vt[i] * w[i]).astype(BF16), kb[i]))
    ct_new = each(lambda i: dec[i] * ct[i] + vk[i])
    ws = each(lambda i: _split2(jnp.broadcast_to(w[i], (8, c))))
    ks = each(lambda i: _split2(k[i]))
    wk = each(lambda i: (_dot(ws[i][0], ks[i][0]) + _dot(ws[i][1], ks[i][0]) + _dot(ws[i][0], ks[i][1]))[0:1])
    nv_new = each(lambda i: dec[i] * nv[i] + wk[i])
    return [(ht[i], ct_new[i], nv_new[i], m_new[i]) for i in n]


def _mlstm_kernel(seq_len, q_ref, k_ref, v_ref, og_ref, gate_t_ref, cwq_ref, cwk_ref,
                  c0_ref, n0_ref, m0_ref,
                  o_ref, c_out_ref, n_out_ref, m_out_ref,
                  q2_ref, k2_ref, vt_ref, hf_ref, hb_ref, ct_ref, n_ref, m_ref):
    c = SCAN_CHUNK
    hd = HEAD_DIM
    n_chunks = seq_len // c
    q2_ref[...] = _silu(_short_conv3(q_ref[...], cwq_ref[...]))
    k2_ref[...] = _silu(_short_conv3(k_ref[...], cwk_ref[...])) * (HEAD_DIM ** -0.5)
    for hh in range(HEADS):
        cols = slice(hh * hd, (hh + 1) * hd)
        for j in range(n_chunks):
            vt_ref[hh, :, j * c:(j + 1) * c] = v_ref[j * c:(j + 1) * c, cols].T
        for d in range(2):
            ct_ref[d, hh] = c0_ref[d, hh].T
    n_ref[...] = n0_ref[...]
    m_ref[...] = m0_ref[...]

    def body(n, carry):
        where, chains = [], []
        for d in range(2):
            sl = pl.ds(pl.multiple_of((n_chunks - 1 - n if d else n) * c, c), c)
            for hh in range(HEADS):
                cols = slice(hh * hd, (hh + 1) * hd)
                gr = gate_t_ref[hh, :, sl]
                where.append((d, hh, sl))
                chains.append((bool(d), q2_ref[sl, cols], k2_ref[sl, cols], vt_ref[hh, :, sl],
                               gr[2 * d:2 * d + 1, :], gr[2 * d + 1:2 * d + 2, :],
                               ct_ref[d, hh], n_ref[d, hh], m_ref[d, hh]))
        for (d, hh, sl), (ht, ct, nv, m_new) in zip(where, _mlstm_chunks(chains)):
            ct_ref[d, hh] = ct
            n_ref[d, hh] = nv
            m_ref[d, hh] = m_new
            (hb_ref if d else hf_ref)[hh, :, sl] = ht
        return carry

    lax.fori_loop(0, n_chunks, body, 0)
    for hh in range(HEADS):
        cols = slice(hh * hd, (hh + 1) * hd)
        for j in range(n_chunks):
            rows = slice(j * c, (j + 1) * c)
            h = (hf_ref[hh, :, rows] + hb_ref[hh, :, rows]).T
            h = h * lax.rsqrt(jnp.mean(h * h, axis=-1, keepdims=True) + RMS_EPS)
            o_ref[rows, cols] = h * _sigmoid(og_ref[rows, cols])
        for d in range(2):
            c_out_ref[d, hh] = ct_ref[d, hh].T
    n_out_ref[...] = n_ref[...]
    m_out_ref[...] = m_ref[...]


def _mlstm(p, gate_t_h, conv_w, c0, n0, m0, n_seq, seq_len, tok_offset):
    hd = HEAD_DIM
    gw = GROUP_W
    row0 = tok_offset // seq_len

    def col(part):
        return pl.BlockSpec((seq_len, gw), lambda b: (row0 + b, part))

    c_spec = pl.BlockSpec((None, 2, HEADS, hd, hd), lambda b: (b, 0, 0, 0, 0))
    n_spec = pl.BlockSpec((None, 2, HEADS, 1, hd), lambda b: (b, 0, 0, 0, 0))
    m_spec = pl.BlockSpec((None, 2, HEADS, 1, 1), lambda b: (b, 0, 0, 0, 0))
    return pl.pallas_call(
        functools.partial(_mlstm_kernel, seq_len),
        grid=(n_seq,),
        in_specs=[col(5), col(6), col(7), col(8),
                  pl.BlockSpec((HEADS, 4, seq_len), lambda b: (0, 0, row0 + b)),
                  pl.BlockSpec((3, gw), lambda b: (0, 0)),
                  pl.BlockSpec((3, gw), lambda b: (0, 1)),
                  c_spec, n_spec, m_spec],
        out_specs=[pl.BlockSpec((seq_len, gw), lambda b: (b, 0)), c_spec, n_spec, m_spec],
        out_shape=[jax.ShapeDtypeStruct((n_seq * seq_len, gw), F32),
                   jax.ShapeDtypeStruct((n_seq, 2, HEADS, hd, hd), F32),
                   jax.ShapeDtypeStruct((n_seq, 2, HEADS, 1, hd), F32),
                   jax.ShapeDtypeStruct((n_seq, 2, HEADS, 1, 1), F32)],
        scratch_shapes=[pltpu.VMEM((seq_len, gw), F32), pltpu.VMEM((seq_len, gw), F32),
                        pltpu.VMEM((HEADS, hd, seq_len), F32),
                        pltpu.VMEM((HEADS, hd, seq_len), F32), pltpu.VMEM((HEADS, hd, seq_len), F32),
                        pltpu.VMEM((2, HEADS, hd, hd), F32), pltpu.VMEM((2, HEADS, 1, hd), F32),
                        pltpu.VMEM((2, HEADS, 1, 1), F32)],
        compiler_params=_cparams("parallel"),
        name=f"mlstm_l{seq_len}",
    )(p, p, p, p, gate_t_h, conv_w, conv_w, c0, n0, m0)


def _dft_tables(seq_len):
    n = 2 * seq_len
    k = jnp.arange(seq_len, dtype=jnp.int32)[:, None]
    t = jnp.arange(seq_len, dtype=jnp.int32)[None, :]
    ang = ((k * t) % n).astype(F32) * (2.0 * np.pi / n)
    fc = jnp.cos(ang)
    fs = jnp.sin(ang)
    nyq = jnp.where(t % 2 == 0, 1.0, -1.0).astype(F32)
    fs = jnp.where(k == 0, nyq, fs)
    return jnp.concatenate([fc, fs], axis=0)


def _filter_kernel(seq_len, z_ref, w1_ref, b1_ref, w2_ref, b2_ref, w3f_ref, w3b_ref, b3f_ref, b3b_ref,
                   f0_ref, f1_ref, rf_ref, rb_ref, fhi_ref, flo_ref, kf_ref, a_ref):
    hp = lax.Precision.HIGHEST
    n = 2 * seq_len
    z = z_ref[...]

    @pl.when(jnp.logical_and(pl.program_id(0) == 0, pl.program_id(1) == 0))
    def _():
        a1 = jnp.sin(f0_ref[...] * (jnp.dot(z, w1_ref[...], precision=hp, preferred_element_type=F32)
                                    + b1_ref[...]))
        a_ref[...] = jnp.sin(f1_ref[...] * (jnp.dot(a1, w2_ref[...], precision=hp, preferred_element_type=F32)
                                            + b2_ref[...]))

    a = a_ref[...]
    t_norm = z[:, 0:1]
    hf = (jnp.dot(a, w3f_ref[...], precision=hp, preferred_element_type=F32) + b3f_ref[...]) \
        * jnp.exp(-t_norm * jnp.exp(rf_ref[...]))
    hb = (jnp.dot(a, w3b_ref[...], precision=hp, preferred_element_type=F32) + b3b_ref[...]) \
        * jnp.exp(-t_norm * jnp.exp(rb_ref[...]))
    inv = lax.rsqrt(jnp.sum(hf * hf, axis=0, keepdims=True) + jnp.sum(hb * hb, axis=0, keepdims=True))
    hf = hf * inv
    r = lax.broadcasted_iota(jnp.int32, (seq_len, 1), 0)
    hb = jnp.where(r == 0, 0.0, hb * inv)
    sh, sm, sl = _split3(hf + hb)
    dh, dm, dl = _split3(hf - hb)
    fhi = fhi_ref[...]
    flo = flo_ref[...]
    kc = _dot(fhi[:seq_len], sh) + _dot(fhi[:seq_len], sm) + _dot(fhi[:seq_len], sl) \
        + _dot(flo[:seq_len], sh) + _dot(flo[:seq_len], sm)
    ks = _dot(fhi[seq_len:], dh) + _dot(fhi[seq_len:], dm) + _dot(fhi[seq_len:], dl) \
        + _dot(flo[seq_len:], dh) + _dot(flo[seq_len:], dm)
    sign = jnp.where(r % 2 == 0, 1.0, -1.0)
    k_nyq = jnp.sum(sign * (hf + hb), axis=0, keepdims=True)
    ks = jnp.where(r == 0, k_nyq, ks)
    scale = jnp.where(r == 0, 1.0 / n, 2.0 / n)
    kf_ref[0:seq_len, :] = kc * scale
    kf_ref[seq_len:n, :] = ks * scale


def _hyena_filters(seq_len, w1, b1, w2, b2, w3, b3, freq, log_rate, f_tab):
    d = D_MODEL
    cb = 256
    t = jnp.arange(seq_len, dtype=F32)
    t_norm = t / (seq_len - 1)
    bands = jnp.linspace(1e-4, HY_BANDS - 1, HY_BANDS, dtype=F32)
    ang = (2.0 * np.pi / seq_len) * t[:, None] * bands[None, :]
    z = jnp.concatenate([t_norm[:, None], jnp.cos(ang), jnp.sin(ang)], axis=-1)
    kpad = 128 - HY_EMB
    z = jnp.pad(z, ((0, 0), (0, kpad)))
    w1p = jnp.pad(w1, ((0, kpad), (0, 0)))
    f_hi = f_tab.astype(BF16)
    f_lo = (f_tab - f_hi.astype(F32)).astype(BF16)
    n_cb = d // cb
    hh = HY_HIDDEN
    row = lambda a: a.reshape(1, -1)
    const = lambda shape: pl.BlockSpec(shape, lambda o, j: (0,) * len(shape))
    fwd = lambda rows: pl.BlockSpec((rows, cb), lambda o, j: (0, o * n_cb + j))
    bwd = lambda rows: pl.BlockSpec((rows, cb), lambda o, j: (0, (HY_ORDER + o) * n_cb + j))
    return pl.pallas_call(
        functools.partial(_filter_kernel, seq_len),
        grid=(HY_ORDER, n_cb),
        in_specs=[const((seq_len, 128)), const((128, hh)), const((1, hh)), const((hh, hh)), const((1, hh)),
                  fwd(hh), bwd(hh), fwd(1), bwd(1),
                  const((1, hh)), const((1, hh)), fwd(1), bwd(1),
                  const((2 * seq_len, seq_len)), const((2 * seq_len, seq_len))],
        out_specs=pl.BlockSpec((None, 2 * seq_len, cb), lambda o, j: (o, 0, j)),
        out_shape=jax.ShapeDtypeStruct((HY_ORDER, 2 * seq_len, d), F32),
        scratch_shapes=[pltpu.VMEM((seq_len, hh), F32)],
        compiler_params=_cparams("arbitrary", "arbitrary"),
        name=f"hyena_filter_l{seq_len}",
    )(z, w1p, row(b1), w2, row(b2), w3, w3, row(b3), row(b3),
      row(freq[0]), row(freq[1]), row(log_rate), row(log_rate), f_hi, f_lo)


def _hyena_kernel(seq_len, seqs, v_ref, x1_ref, x2_ref, cwv_ref, cw1_ref, cw2_ref, bias_ref, kf_ref,
                  f_ref, ft_ref, o_ref, z_ref, zb_ref, y_ref):
    kc = min(HY_FREQ_CHUNK, seq_len)
    n_k = seq_len // kc
    r = lax.broadcasted_iota(jnp.int32, (kc, 1), 0)
    gate_refs = ((x1_ref, cw1_ref), (x2_ref, cw2_ref))
    for s in range(seqs):
        rows = slice(s * seq_len, (s + 1) * seq_len)
        z_ref[s] = _short_conv3(v_ref[rows, :], cwv_ref[...])
    for o in range(HY_ORDER):
        for s in range(seqs):
            zb_ref[s] = z_ref[s].astype(BF16)
            y_ref[s] = jnp.zeros(y_ref.shape[1:], F32)

        def freq_chunk(j, carry):
            r0 = pl.multiple_of(j * kc, kc)
            k_cos = kf_ref[o, pl.ds(r0, kc), :]
            k_sin = kf_ref[o, pl.ds(seq_len + r0, kc), :]
            real_row = jnp.logical_and(r == 0, j == 0)
            for s in range(seqs):
                a = _dot(f_ref[pl.ds(r0, kc), :], zb_ref[s])
                bm = _dot(f_ref[pl.ds(seq_len + r0, kc), :], zb_ref[s])
                yc = a * k_cos - jnp.where(real_row, 0.0, bm * k_sin)
                ys = jnp.where(real_row, bm * k_sin, a * k_sin + bm * k_cos)
                y_ref[s] += _dot(ft_ref[j], yc.astype(BF16)) + _dot(ft_ref[n_k + j], ys.astype(BF16))
            return carry

        lax.fori_loop(0, n_k, freq_chunk, 0)
        x_ref, cw_ref = gate_refs[o]
        for s in range(seqs):
            rows = slice(s * seq_len, (s + 1) * seq_len)
            gate = _short_conv3(x_ref[rows, :], cw_ref[...])
            z_ref[s] = gate * (y_ref[s] + z_ref[s] * bias_ref[o:o + 1, :])
    for s in range(seqs):
        o_ref[s * seq_len:(s + 1) * seq_len, :] = z_ref[s]


def _hyena(u, conv_w, bias, kf, f_tab, n_seq, seq_len, tok_offset):
    d = D_MODEL
    cb = 256
    n_cb = d // cb
    seqs = max(1, HY_STEP_ROWS // seq_len)
    rows = seqs * seq_len
    row0 = tok_offset // rows
    kc = min(HY_FREQ_CHUNK, seq_len)
    n_k = seq_len // kc
    f_bf = f_tab.astype(BF16)
    ft = f_bf.T.reshape(seq_len, 2 * n_k, kc).transpose(1, 0, 2)

    def part(k):
        return pl.BlockSpec((rows, cb), lambda j, b: (row0 + b, k * n_cb + j))

    def cw(k):
        return pl.BlockSpec((3, cb), lambda j, b: (0, k * n_cb + j))

    return pl.pallas_call(
        functools.partial(_hyena_kernel, seq_len, seqs),
        grid=(n_cb, n_seq // seqs),
        in_specs=[part(0), part(1), part(2), cw(0), cw(1), cw(2),
                  pl.BlockSpec((HY_ORDER, cb), lambda j, b: (0, j)),
                  pl.BlockSpec((HY_ORDER, 2 * seq_len, cb), lambda j, b: (0, 0, j)),
                  pl.BlockSpec((2 * seq_len, seq_len), lambda j, b: (0, 0)),
                  pl.BlockSpec((2 * n_k, seq_len, kc), lambda j, b: (0, 0, 0))],
        out_specs=pl.BlockSpec((rows, cb), lambda j, b: (b, j)),
        out_shape=jax.ShapeDtypeStruct((n_seq * seq_len, d), F32),
        scratch_shapes=[pltpu.VMEM((seqs, seq_len, cb), F32), pltpu.VMEM((seqs, seq_len, cb), BF16),
                        pltpu.VMEM((seqs, seq_len, cb), F32)],
        compiler_params=_cparams("parallel", "parallel"),
        name=f"hyena_l{seq_len}",
    )(u, u, u, conv_w, conv_w, conv_w, bias, kf, f_bf, ft)


def _mix_out_kernel(n_parts, *refs):
    o_refs = refs[:2 * n_parts]
    w_refs = refs[2 * n_parts:3 * n_parts]
    x_ref, mod_ref, g_ref, wr_both_ref, br_ref = refs[3 * n_parts:3 * n_parts + 5]
    x1_ref, h_ref, eid_ref, wts_ref, rank_ref, cnt_ref, run_ref = refs[3 * n_parts + 5:]
    i = pl.program_id(0)
    t = TOK_TILE
    ne = N_EXPERTS

    from_prompt = i < N_PROMPT // TOK_TILE
    y = None
    for j in range(n_parts):
        o = jnp.where(from_prompt, o_refs[2 * j][...], o_refs[2 * j + 1][...])
        yj = _dot(o.astype(BF16), w_refs[j][...])
        y = yj if y is None else y + yj
    x1 = x_ref[...] + mod_ref[2:3, :] * y
    x1_ref[...] = x1
    h = _norm_mod(x1, g_ref[...], mod_ref[4:5, :], mod_ref[3:4, :])
    h_ref[...] = _pack_bf16_halves(h)
    h_hi = h.astype(BF16)
    h_lo = (h - h_hi.astype(F32)).astype(BF16)
    hi_terms = _dot(h_hi, wr_both_ref[...])
    logits = (hi_terms[:, :LANES] + hi_terms[:, LANES:] + _dot(h_lo, wr_both_ref[:, :LANES])).T[:ne] \
        + br_ref[...]

    @pl.when(i == 0)
    def _():
        run_ref[...] = jnp.zeros_like(run_ref)

    e_iota = lax.broadcasted_iota(jnp.int32, (ne, t), 0)
    vals, eids, onehots = [], [], []
    for _k in range(TOP_K):
        m = jnp.max(logits, axis=0, keepdims=True)
        eid = jnp.min(jnp.where(logits == m, e_iota, ne), axis=0, keepdims=True)
        sel = e_iota == eid
        logits = jnp.where(sel, -jnp.inf, logits)
        onehots.append(jnp.where(sel, 1.0, 0.0))
        vals.append(m)
        eids.append(eid)
    r2 = lax.broadcasted_iota(jnp.int32, (t, t), 0)
    c2 = lax.broadcasted_iota(jnp.int32, (t, t), 1)
    before = jnp.where(r2 < c2, 1.0, 0.0).astype(BF16)
    earlier = _dot(jnp.concatenate(onehots, axis=0).astype(BF16), before)
    running = run_ref[...]
    ranks = []
    for k, onehot in enumerate(onehots):
        ranks.append(jnp.sum(onehot * (running + earlier[k * ne:(k + 1) * ne]), axis=0, keepdims=True))
        running = running + jnp.sum(onehot, axis=1, keepdims=True)
    run_ref[...] = running
    cnt_ref[...] = running
    v = jnp.concatenate(vals, axis=0)
    ex = jnp.exp(v - v[0:1])
    wts_ref[...] = ex / jnp.sum(ex, axis=0, keepdims=True)
    eid_ref[...] = jnp.concatenate(eids, axis=0)
    rank_ref[...] = jnp.concatenate(ranks, axis=0).astype(jnp.int32)


def _mix_out(parts, x, mod_l, norm_g, w_router, b_router):
    d = D_MODEL
    t = TOK_TILE
    ne = N_EXPERTS
    n_parts = len(parts)
    n_prompt_tiles = N_PROMPT // t
    wr = jnp.pad(w_router, ((0, 0), (0, LANES - ne)))
    wr_hi = wr.astype(BF16)
    wr_both = jnp.concatenate([wr_hi, (wr - wr_hi.astype(F32)).astype(BF16)], axis=1)
    in_specs = []
    for o_p, _, _ in parts:
        in_specs.append(pl.BlockSpec((t, o_p.shape[1]), lambda i: (jnp.minimum(i, n_prompt_tiles - 1), 0)))
        in_specs.append(pl.BlockSpec((t, o_p.shape[1]), lambda i: (jnp.maximum(i - n_prompt_tiles, 0), 0)))
    in_specs += [pl.BlockSpec(w.shape, lambda i: (0, 0)) for _, _, w in parts]
    in_specs += [
        pl.BlockSpec((t, d), lambda i: (i, 0)),
        pl.BlockSpec((None, N_MOD, d), lambda i: (_tile_cond_row(i), 0, 0)),
        pl.BlockSpec((1, d), lambda i: (0, 0)),
        pl.BlockSpec((d, 2 * LANES), lambda i: (0, 0)),
        pl.BlockSpec((ne, 1), lambda i: (0, 0)),
    ]
    tok_major = pl.BlockSpec((TOP_K, t), lambda i: (0, i))
    return pl.pallas_call(
        functools.partial(_mix_out_kernel, n_parts),
        grid=(N_TOK // t,),
        in_specs=in_specs,
        out_specs=[pl.BlockSpec((t, d), lambda i: (i, 0)), pl.BlockSpec((t, d // 2), lambda i: (i, 0)),
                   tok_major, tok_major, tok_major, pl.BlockSpec((ne, 1), lambda i: (0, 0))],
        out_shape=[jax.ShapeDtypeStruct((N_TOK, d), F32), jax.ShapeDtypeStruct((N_TOK, d // 2), jnp.uint32),
                   jax.ShapeDtypeStruct((TOP_K, N_TOK), jnp.int32), jax.ShapeDtypeStruct((TOP_K, N_TOK), F32),
                   jax.ShapeDtypeStruct((TOP_K, N_TOK), jnp.int32), jax.ShapeDtypeStruct((ne, 1), F32)],
        scratch_shapes=[pltpu.VMEM((ne, 1), F32)],
        compiler_params=_cparams("arbitrary"),
        name="mix_out_router",
    )(*[o for part in parts for o in part[:2]], *[w.astype(BF16) for _, _, w in parts], x, mod_l,
      norm_g.reshape(1, d), wr_both, b_router.reshape(ne, 1))


def _sc_row_gather(src, idx):
    n = idx.shape[0]
    width = src.shape[1]
    step_rows = min(SC_MAX_INDICES, SC_STEP_BYTES // (width * 4))
    sc = plsc.get_sparse_core_info()
    n_workers = sc.num_cores * sc.num_subcores
    per_worker = n // n_workers
    if n % n_workers or per_worker % step_rows:
        raise ValueError("row count must be whole SparseCore steps on every subcore")
    mesh = plsc.VectorSubcoreMesh(core_axis_name="core", subcore_axis_name="subcore")

    @functools.partial(pl.kernel, out_type=jax.ShapeDtypeStruct((n, width), src.dtype), mesh=mesh,
                       scratch_types=[pltpu.VMEM((per_worker,), jnp.int32),
                                      pltpu.VMEM((step_rows, width), src.dtype)],
                       name="sc_row_gather")
    def gather(src_hbm, idx_hbm, dst_hbm, idx_vmem, rows_vmem):
        worker = lax.axis_index("subcore") * sc.num_cores + lax.axis_index("core")
        base = worker * per_worker
        pltpu.sync_copy(idx_hbm.at[pl.ds(base, per_worker)], idx_vmem)

        @pl.loop(0, per_worker // step_rows)
        def _(c):
            pltpu.sync_copy(src_hbm.at[idx_vmem.at[pl.ds(c * step_rows, step_rows)]], rows_vmem)
            pltpu.sync_copy(rows_vmem, dst_hbm.at[pl.ds(base + c * step_rows, step_rows)])

    return gather(src, idx)


def _sc_row_scatter(src, row_of_slot):
    n = row_of_slot.shape[0]
    n_src, width = src.shape
    step_rows = min(SC_MAX_INDICES, SC_STEP_BYTES // (width * 4))
    sc = plsc.get_sparse_core_info()
    n_workers = sc.num_cores * sc.num_subcores
    per_worker = n // n_workers
    steps = per_worker // step_rows
    if n % n_workers or per_worker % step_rows or n_src % step_rows:
        raise ValueError("row counts must be whole SparseCore steps on every subcore")
    mesh = plsc.VectorSubcoreMesh(core_axis_name="core", subcore_axis_name="subcore")

    @functools.partial(pl.kernel, out_type=jax.ShapeDtypeStruct((n, width), src.dtype), mesh=mesh,
                       scratch_types=[pltpu.VMEM((steps, step_rows), jnp.int32),
                                      pltpu.VMEM((step_rows, width), src.dtype)],
                       name="sc_row_scatter")
    def scatter(src_hbm, idx_hbm, dst_hbm, idx_vmem, rows_vmem):
        worker = lax.axis_index("subcore") * sc.num_cores + lax.axis_index("core")
        pltpu.sync_copy(idx_hbm.at[worker], idx_vmem)

        @pl.loop(0, steps)
        def _(c):
            src0 = lax.rem(worker * per_worker + c * step_rows, n_src)
            pltpu.sync_copy(src_hbm.at[pl.ds(src0, step_rows)], rows_vmem)
            pltpu.sync_copy(rows_vmem, dst_hbm.at[idx_vmem.at[c]])

    return scatter(src, row_of_slot.reshape(n_workers, steps, step_rows))


def _pack_bf16_halves(x):
    w = x.shape[1] // 2
    bits = pltpu.bitcast(x.astype(BF16).astype(F32), jnp.uint32)
    return bits[:, :w] | (bits[:, w:] >> 16)


def _unpack_bf16_halves(p):
    hi = pltpu.bitcast(p & jnp.uint32(0xFFFF0000), F32).astype(BF16)
    lo = pltpu.bitcast(p << 16, F32).astype(BF16)
    return hi, lo


def _experts_kernel(layer, te_ref, first_ref, slot_ref, next_ref, nv_ref,
                    x_ref, wgu_hbm, bg_ref, bu_ref, wd_hbm, bd_ref, sel_ref,
                    y_ref, wgu_buf, wd_buf, wg_ref, wu_ref, wdb_ref, sem):
    i = pl.program_id(0)
    valid = i < nv_ref[0]
    half = DEINT_COLS // 2
    k_half = x_ref.shape[1]

    def fetch(expert, slot):
        return (pltpu.make_async_copy(wgu_hbm.at[layer, expert], wgu_buf.at[slot], sem.at[slot, 0]),
                pltpu.make_async_copy(wd_hbm.at[layer, expert], wd_buf.at[slot], sem.at[slot, 1]))

    @pl.when(i == 0)
    def _():
        for cp in fetch(te_ref[0], 0):
            cp.start()

    @pl.when(jnp.logical_and(valid, first_ref[i] == 1))
    def _():
        slot = slot_ref[i]
        for cp in fetch(te_ref[i], slot):
            cp.wait()

        @pl.when(next_ref[i] >= 0)
        def _():
            for cp in fetch(next_ref[i], 1 - slot):
                cp.start(priority=1)

        for c in range(wgu_buf.shape[2] // DEINT_COLS):
            w = wgu_buf[slot, :, c * DEINT_COLS:(c + 1) * DEINT_COLS].astype(BF16)
            split = _dot(w, sel_ref[...]).astype(BF16)
            wg_ref[:, c * half:(c + 1) * half] = split[:, :half]
            wu_ref[:, c * half:(c + 1) * half] = split[:, half:]
        wdb_ref[...] = wd_buf[slot].astype(BF16)

    @pl.when(valid)
    def _():
        x_l, x_r = _unpack_bf16_halves(x_ref[...])
        gl = _dot(x_l, wg_ref[:k_half, :]) + _dot(x_r, wg_ref[k_half:, :]) + bg_ref[...]
        up = _dot(x_l, wu_ref[:k_half, :]) + _dot(x_r, wu_ref[k_half:, :]) + bu_ref[...]
        gl = jnp.minimum(gl, SWIGLU_LIMIT)
        up = jnp.clip(up, -SWIGLU_LIMIT, SWIGLU_LIMIT)
        act = (up + 1.0) * gl * _sigmoid(SWIGLU_ALPHA * gl)
        y_ref[...] = _pack_bf16_halves(_dot(act.astype(BF16), wdb_ref[...]) + bd_ref[...])

    @pl.when(jnp.logical_not(valid))
    def _():
        y_ref[...] = jnp.zeros_like(y_ref)


def _experts(xs, tile_expert, tile_first, n_valid, layer, w_gu, b_gate, b_up, w_down, b_down):
    d = D_MODEL
    tm = MOE_TILE
    n_tiles = MOE_ROWS // tm
    ff = w_down.shape[2]
    half = DEINT_COLS // 2
    r = jnp.arange(DEINT_COLS)[:, None]
    c = jnp.arange(DEINT_COLS)[None, :]
    sel = (r == jnp.where(c < half, 2 * c, 2 * (c - half) + 1)).astype(BF16)
    group = jnp.cumsum(tile_first) - 1
    tile_slot = (group % 2).astype(jnp.int32)
    is_last_group = group == group[-1]
    following = jnp.concatenate([tile_expert[1:], tile_expert[-1:]])
    idx = jnp.arange(n_tiles, dtype=jnp.int32)
    group_end = jnp.max(jnp.where(group[None, :] == group[:, None], idx[None, :], -1), axis=1)
    tile_next = jnp.where(is_last_group, -1, following[group_end]).astype(jnp.int32)
    wspec = lambda k, n: pl.BlockSpec((None, None, k, n), lambda i, *_: (layer, _[0][i], 0, 0))
    grid_spec = pltpu.PrefetchScalarGridSpec(
        num_scalar_prefetch=5,
        grid=(n_tiles,),
        in_specs=[pl.BlockSpec((tm, d // 2), lambda i, *_: (i, 0)),
                  pl.BlockSpec(memory_space=pl.ANY), wspec(1, ff), wspec(1, ff),
                  pl.BlockSpec(memory_space=pl.ANY), wspec(1, d),
                  pl.BlockSpec((DEINT_COLS, DEINT_COLS), lambda i, *_: (0, 0))],
        out_specs=pl.BlockSpec((tm, d // 2), lambda i, *_: (i, 0)),
        scratch_shapes=[pltpu.VMEM((2, d, 2 * ff), F32), pltpu.VMEM((2, ff, d), F32),
                        pltpu.VMEM((d, ff), BF16), pltpu.VMEM((d, ff), BF16), pltpu.VMEM((ff, d), BF16),
                        pltpu.SemaphoreType.DMA((2, 2))],
    )
    return pl.pallas_call(
        functools.partial(_experts_kernel, layer),
        grid_spec=grid_spec,
        out_shape=jax.ShapeDtypeStruct((MOE_ROWS, d // 2), jnp.uint32),
        compiler_params=_cparams("arbitrary"),
        name="experts",
    )(tile_expert, tile_first, tile_slot, tile_next, n_valid, xs, w_gu, b_gate, b_up, w_down, b_down, sel)


def _combine_kernel(final, x_ref, g_ref, w_ref, mod_ref, fg_ref, *o_refs):
    y_l = y_r = None
    for k in range(TOP_K):
        g_l, g_r = _unpack_bf16_halves(g_ref[k])
        wk = w_ref[:, k:k + 1]
        y_l = wk * g_l.astype(F32) if y_l is None else y_l + wk * g_l.astype(F32)
        y_r = wk * g_r.astype(F32) if y_r is None else y_r + wk * g_r.astype(F32)
    x = x_ref[...] + mod_ref[5:6, :] * jnp.concatenate([y_l, y_r], axis=1)
    if not final:
        o_refs[0][...] = x
        return
    x = x * lax.rsqrt(jnp.mean(x * x, axis=-1, keepdims=True) + RMS_EPS) * fg_ref[...]
    from_prompt = pl.program_id(0) < N_PROMPT // TOK_TILE

    @pl.when(from_prompt)
    def _():
        o_refs[0][...] = x

    @pl.when(jnp.logical_not(from_prompt))
    def _():
        o_refs[1][...] = x


def _combine(x1, gathered, wts, mod_l, final_g, final):
    d = D_MODEL
    t = TOK_TILE
    n_prompt_tiles = N_PROMPT // t
    if final:
        out_specs = [pl.BlockSpec((t, d), lambda i: (jnp.minimum(i, n_prompt_tiles - 1), 0)),
                     pl.BlockSpec((t, d), lambda i: (jnp.maximum(i - n_prompt_tiles, 0), 0))]
        out_shape = [jax.ShapeDtypeStruct((N_PROMPT, d), F32), jax.ShapeDtypeStruct((N_SAMPLE, d), F32)]
    else:
        out_specs = pl.BlockSpec((t, d), lambda i: (i, 0))
        out_shape = jax.ShapeDtypeStruct((N_TOK, d), F32)
    return pl.pallas_call(
        functools.partial(_combine_kernel, final),
        grid=(N_TOK // t,),
        in_specs=[pl.BlockSpec((t, d), lambda i: (i, 0)),
                  pl.BlockSpec((TOP_K, t, d // 2), lambda i: (0, i, 0)),
                  pl.BlockSpec((t, TOP_K), lambda i: (i, 0)),
                  pl.BlockSpec((None, N_MOD, d), lambda i: (_tile_cond_row(i), 0, 0)),
                  pl.BlockSpec((1, d), lambda i: (0, 0))],
        out_specs=out_specs,
        out_shape=out_shape,
        compiler_params=_cparams("arbitrary"),
        name="moe_combine",
    )(x1, gathered, wts, mod_l, final_g.reshape(1, d))


def _moe(x1, h, eid, wts, rank, counts, mod_l, layer, w_gu, b_gu, w_down, b_down, final_g, final):
    d = D_MODEL
    tm = MOE_TILE
    n_tiles = MOE_ROWS // tm
    cnt = counts.reshape(N_EXPERTS).astype(jnp.int32)
    gsz = ((cnt + tm - 1) // tm) * tm
    ends = jnp.cumsum(gsz)
    offs = ends - gsz
    e_ids = jnp.arange(N_EXPERTS, dtype=jnp.int32)
    pos = jnp.sum(jnp.where(eid[..., None] == e_ids, offs, 0), axis=-1) + rank
    tile_start = jnp.arange(n_tiles, dtype=jnp.int32) * tm
    tile_expert = jnp.minimum(jnp.sum((ends[None, :] <= tile_start[:, None]).astype(jnp.int32), axis=1),
                              N_EXPERTS - 1)
    n_valid = (ends[-1:] // tm).astype(jnp.int32)
    last_valid = jnp.maximum(n_valid[0] - 1, 0)
    tile_expert = jnp.where(jnp.arange(n_tiles) < n_valid[0], tile_expert, tile_expert[last_valid])
    tile_first = jnp.concatenate([jnp.ones((1,), jnp.int32),
                                  (tile_expert[1:] != tile_expert[:-1]).astype(jnp.int32)])
    j = jnp.arange(tm, dtype=jnp.int32)[None, :]
    pad_used = j < (gsz - cnt)[:, None]
    n_unused_before = jnp.cumsum((~pad_used).reshape(-1).astype(jnp.int32)) - 1
    pad_pos = jnp.where(pad_used, (offs + cnt)[:, None] + j,
                        ends[-1] + n_unused_before.reshape(N_EXPERTS, tm))
    row_of_slot = jnp.concatenate([pos.reshape(-1), pad_pos.reshape(-1)])

    xs = _sc_row_scatter(h, row_of_slot)
    ys = _experts(xs, tile_expert, tile_first, n_valid, layer, w_gu,
                  b_gu[:, :, None, 0::2], b_gu[:, :, None, 1::2], w_down, b_down[:, :, None, :])
    gathered = _sc_row_gather(ys, pos.reshape(-1))
    return _combine(x1, gathered.reshape(TOP_K, N_TOK, d // 2), wts.T, mod_l, final_g, final)


def _grid_positions(n_tok, d):
    rows = n_tok // GRID_W
    r, col = jnp.meshgrid(jnp.arange(rows, dtype=F32), jnp.arange(GRID_W, dtype=F32), indexing='ij')
    r = r.reshape(-1)
    col = col.reshape(-1)
    quarter = d // 4
    inv = 1.0 / (10000.0 ** (jnp.arange(quarter, dtype=F32) / quarter))
    ar = r[:, None] * inv[None]
    ac = col[:, None] * inv[None]
    return jnp.concatenate([jnp.sin(ar), jnp.cos(ar), jnp.sin(ac), jnp.cos(ac)], axis=-1)


def kernel(x_prompt, x_sample, state_hgrn, state_mlstm_c, state_mlstm_n, state_mlstm_m, c, c_ctx,
           norm_g, final_g, w_mod, b_mod, ev_w_in, ev_gate_b, ev_conv, hg_lb, ev_w_out,
           hy_w_in, hy_conv, hy_w1, hy_b1, hy_w2, hy_b2, hy_w3, hy_b3, hy_freq, hy_log_rate, hy_bias, hy_w_out,
           w_router, b_router, w_gu, b_gu, w_down, b_down):
    d = D_MODEL
    hd = HEAD_DIM
    cond = jnp.concatenate([c_ctx[None], c, jnp.zeros((N_COND - 1 - DEC_BATCH, d), F32)], axis=0)
    mod = _modulation(cond, w_mod, b_mod)
    pos_tab = jnp.concatenate([jnp.zeros((TOK_TILE, d), F32), _grid_positions(DEC_SEQ, d)], axis=0)

    groups = ((BATCH, SEQ, 0), (DEC_BATCH, DEC_SEQ, N_PROMPT))
    new_states = None
    for l in range(DEPTH):
        if l % 2 == 0:
            e = l // 2
            if l == 0:
                x, p, gate_t = _proj_even(x_prompt.reshape(N_PROMPT, d), x_sample.reshape(N_SAMPLE, d),
                                          pos_tab, mod[l], norm_g[l, 0], ev_w_in[e], ev_gate_b[e])
            else:
                raise NotImplementedError("only the first layer adds grid positions")
            gate_t_h = gate_t.reshape(4, HEADS, N_TOK).transpose(1, 0, 2)
            o_hg, o_ml = [], []
            for gi, (n_seq, seq_len, off) in enumerate(groups):
                if gi == 0:
                    s0 = jnp.zeros((n_seq, 2, HEADS, hd, hd), F32)
                    c0 = jnp.zeros((n_seq, 2, HEADS, hd, hd), F32)
                    n0 = jnp.zeros((n_seq, 2, HEADS, 1, hd), F32)
                    m0 = jnp.zeros((n_seq, 2, HEADS, 1, 1), F32)
                else:
                    s0 = state_hgrn[:, e]
                    c0 = state_mlstm_c[:, e]
                    n0 = state_mlstm_n[:, e].reshape(n_seq, 2, HEADS, 1, hd)
                    m0 = state_mlstm_m[:, e].reshape(n_seq, 2, HEADS, 1, 1)
                og, s_fin = _hgrn(p, hg_lb, l, s0, n_seq, seq_len, off)
                om, c_fin, n_fin, m_fin = _mlstm(p, gate_t_h, ev_conv[e], c0, n0, m0, n_seq, seq_len, off)
                o_hg.append(og)
                o_ml.append(om)
                if gi == 0:
                    new_states = (s_fin[:, None], c_fin[:, None],
                                  n_fin.reshape(n_seq, 1, 2, HEADS, hd), m_fin.reshape(n_seq, 1, 2, HEADS))
            parts = [(*o_hg, ev_w_out[e][:GROUP_W]), (*o_ml, ev_w_out[e][GROUP_W:])]
        else:
            o = l // 2
            u = _proj_odd(x, mod[l], norm_g[l, 0], hy_w_in[o])
            zs = []
            for n_seq, seq_len, off in groups:
                f_tab = _dft_tables(seq_len)
                kf = _hyena_filters(seq_len, hy_w1[o], hy_b1[o], hy_w2[o], hy_b2[o], hy_w3[o], hy_b3[o],
                                    hy_freq[o], hy_log_rate[o], f_tab)
                zs.append(_hyena(u, hy_conv[o], hy_bias[o], kf, f_tab, n_seq, seq_len, off))
            parts = [(*zs, hy_w_out[o])]
        x1, h, eid, wts, rank, counts = _mix_out(parts, x, mod[l], norm_g[l, 1], w_router[l], b_router[l])
        x = _moe(x1, h, eid, wts, rank, counts, mod[l], l, w_gu, b_gu, w_down, b_down,
                 final_g, final=(l == DEPTH - 1))

    y_prompt, y_sample = x
    return (y_prompt.reshape(BATCH, SEQ, d), y_sample.reshape(DEC_BATCH, DEC_SEQ, d)) + new_states
```

```python
import functools

import numpy as np
import jax
import jax.numpy as jnp
from jax import lax
from jax.experimental import pallas as pl
from jax.experimental.pallas import tpu as pltpu
from jax.experimental.pallas import tpu_sc as plsc

F32 = jnp.float32
BF16 = jnp.bfloat16

D_MODEL = 1024
BATCH = 32
SEQ = 256
DEPTH = 2
DEC_BATCH = 8
DEC_SEQ = 1024
GRID_W = 64
RMS_EPS = 1e-6
N_MOD = 6
LOG2_E = 1.4426950408889634
LANES = 128

HEADS = 4
HEAD_DIM = 128
GROUP_W = HEADS * HEAD_DIM
N_GATES = 4 * HEADS
EVEN_MAIN = 9 * GROUP_W

HY_ORDER = 2
HY_BANDS = 16
HY_EMB = 1 + 2 * HY_BANDS
HY_HIDDEN = 64
HY_FREQ_CHUNK = 512
HY_STEP_ROWS = 1024

N_EXPERTS = 32
TOP_K = 4
SWIGLU_LIMIT = 7.0
SWIGLU_ALPHA = 1.702

N_PROMPT = BATCH * SEQ
N_SAMPLE = DEC_BATCH * DEC_SEQ
N_TOK = N_PROMPT + N_SAMPLE
N_COND = 16

TOK_TILE = 256
SCAN_CHUNK = 128
SUB = 16
HALF = 8
HG_STEP_HEADS = 4
MOE_TILE = 512
MOE_ROWS = N_TOK * TOP_K + N_EXPERTS * MOE_TILE
DEINT_COLS = 256
SC_STEP_BYTES = 256 * 1024
SC_MAX_INDICES = 128

VMEM_LIMIT = 56 * 1024 * 1024


def _cparams(*sem):
    return pltpu.CompilerParams(dimension_semantics=sem, vmem_limit_bytes=VMEM_LIMIT)


def _split3(x):
    hi = x.astype(BF16)
    r = x - hi.astype(F32)
    mid = r.astype(BF16)
    lo = (r - mid.astype(F32)).astype(BF16)
    return hi, mid, lo


def _dot(a, b):
    return jnp.dot(a, b, preferred_element_type=F32)


def _dot_nt(a, b):
    return lax.dot_general(a, b, (((1,), (1,)), ((), ())), preferred_element_type=F32)


def _dot_tn(a, b):
    return lax.dot_general(a, b, (((0,), (0,)), ((), ())), preferred_element_type=F32)


def _dot_w3(a_exact_bf16, x):
    hi, mid, lo = _split3(x)
    return _dot(a_exact_bf16, hi) + _dot(a_exact_bf16, mid) + _dot(a_exact_bf16, lo)


def _sigmoid(x):
    return 1.0 / (1.0 + jnp.exp(-x))


def _silu(x):
    return x * _sigmoid(x)


def _log_sigmoid(x):
    return jnp.minimum(x, 0.0) - jnp.log(1.0 + jnp.exp(-jnp.abs(x)))


def _tile_cond_row(i):
    n_prompt_tiles = N_PROMPT // TOK_TILE
    tiles_per_seq = DEC_SEQ // TOK_TILE
    return jnp.where(i < n_prompt_tiles, 0, 1 + (i - n_prompt_tiles) // tiles_per_seq)


def _mod_kernel(cond_ref, w_ref, b_ref, o_ref):
    a = _silu(cond_ref[...]).astype(BF16)
    o_ref[...] = _dot(a, w_ref[...].astype(BF16)) + b_ref[...]


def _modulation(cond, w_mod, b_mod):
    d = D_MODEL
    out = pl.pallas_call(
        _mod_kernel,
        grid=(DEPTH, N_MOD),
        in_specs=[
            pl.BlockSpec((N_COND, d), lambda l, j: (0, 0)),
            pl.BlockSpec((None, d, d), lambda l, j: (l, 0, j)),
            pl.BlockSpec((None, 1, d), lambda l, j: (l, 0, j)),
        ],
        out_specs=pl.BlockSpec((None, None, N_COND, d), lambda l, j: (l, j, 0, 0)),
        out_shape=jax.ShapeDtypeStruct((DEPTH, N_MOD, N_COND, d), F32),
        compiler_params=_cparams("parallel", "parallel"),
        name="modulation",
    )(cond, w_mod, b_mod.reshape(DEPTH, 1, N_MOD * d))
    return out.transpose(0, 2, 1, 3)


def _norm_mod(x, g_row, scale_row, shift_row):
    ms = jnp.mean(x * x, axis=-1, keepdims=True)
    y = x * lax.rsqrt(ms + RMS_EPS) * g_row
    return y * (1.0 + scale_row) + shift_row


def _proj_even_kernel(xp_ref, xs_ref, pos_ref, mod_ref, g_ref, w_ref, wgt_ref, gbt_ref,
                      xres_ref, p_ref, gate_t_ref):
    from_prompt = pl.program_id(0) < N_PROMPT // TOK_TILE
    x = jnp.where(from_prompt, xp_ref[...], xs_ref[...]) + pos_ref[...]
    xres_ref[...] = x
    h = _norm_mod(x, g_ref[...], mod_ref[1:2, :], mod_ref[0:1, :]).astype(BF16)
    p_ref[...] = _dot(h, w_ref[...])
    gate_t_ref[...] = _dot_nt(wgt_ref[...], h) + gbt_ref[...]


def _proj_even(x_prompt, x_sample, pos_tab, mod_l, norm_g, w_in, gate_b):
    d = D_MODEL
    n_tiles = N_TOK // TOK_TILE
    n_prompt_tiles = N_PROMPT // TOK_TILE
    tiles_per_seq = DEC_SEQ // TOK_TILE
    w_main = w_in[:, :EVEN_MAIN].astype(BF16)
    w_gate = w_in[:, EVEN_MAIN:].astype(BF16)

    def pos_map(i):
        return (jnp.where(i < n_prompt_tiles, 0, 1 + (i - n_prompt_tiles) % tiles_per_seq), 0)

    return pl.pallas_call(
        _proj_even_kernel,
        grid=(n_tiles,),
        in_specs=[
            pl.BlockSpec((TOK_TILE, d), lambda i: (jnp.minimum(i, n_prompt_tiles - 1), 0)),
            pl.BlockSpec((TOK_TILE, d), lambda i: (jnp.maximum(i - n_prompt_tiles, 0), 0)),
            pl.BlockSpec((TOK_TILE, d), pos_map),
            pl.BlockSpec((None, N_MOD, d), lambda i: (_tile_cond_row(i), 0, 0)),
            pl.BlockSpec((1, d), lambda i: (0, 0)),
            pl.BlockSpec((d, EVEN_MAIN), lambda i: (0, 0)),
            pl.BlockSpec((N_GATES, d), lambda i: (0, 0)),
            pl.BlockSpec((N_GATES, 1), lambda i: (0, 0)),
        ],
        out_specs=[
            pl.BlockSpec((TOK_TILE, d), lambda i: (i, 0)),
            pl.BlockSpec((TOK_TILE, EVEN_MAIN), lambda i: (i, 0)),
            pl.BlockSpec((N_GATES, TOK_TILE), lambda i: (0, i)),
        ],
        out_shape=[
            jax.ShapeDtypeStruct((N_TOK, d), F32),
            jax.ShapeDtypeStruct((N_TOK, EVEN_MAIN), F32),
            jax.ShapeDtypeStruct((N_GATES, N_TOK), F32),
        ],
        compiler_params=_cparams("parallel"),
        name="proj_even",
    )(x_prompt, x_sample, pos_tab, mod_l, norm_g.reshape(1, d), w_main, w_gate.T, gate_b.reshape(N_GATES, 1))


def _proj_odd_kernel(x_ref, mod_ref, g_ref, w_ref, p_ref):
    h = _norm_mod(x_ref[...], g_ref[...], mod_ref[1:2, :], mod_ref[0:1, :]).astype(BF16)
    p_ref[...] = _dot(h, w_ref[...])


def _proj_odd(x, mod_l, norm_g, w_in):
    d = D_MODEL
    width = w_in.shape[1]
    return pl.pallas_call(
        _proj_odd_kernel,
        grid=(N_TOK // TOK_TILE,),
        in_specs=[
            pl.BlockSpec((TOK_TILE, d), lambda i: (i, 0)),
            pl.BlockSpec((None, N_MOD, d), lambda i: (_tile_cond_row(i), 0, 0)),
            pl.BlockSpec((1, d), lambda i: (0, 0)),
            pl.BlockSpec((d, width), lambda i: (0, 0)),
        ],
        out_specs=pl.BlockSpec((TOK_TILE, width), lambda i: (i, 0)),
        out_shape=jax.ShapeDtypeStruct((N_TOK, width), F32),
        compiler_params=_cparams("parallel"),
        name="proj_odd",
    )(x, mod_l, norm_g.reshape(1, d), w_in.astype(BF16))


def _hgrn_chunks(chains):
    c = SCAN_CHUNK
    n = range(len(chains))
    rev, q, k, v, lf, st = zip(*chains)
    row = lax.broadcasted_iota(jnp.int32, (c, c), 0)
    col = lax.broadcasted_iota(jnp.int32, (c, c), 1)
    tris = {False: col <= row, True: col >= row}
    tris_b = {r: jnp.where(t, 1.0, 0.0).astype(BF16) for r, t in tris.items()}
    b = [_dot_w3(tris_b[rev[j]], lf[j]) for j in n]
    b2 = [b[j] * LOG2_E for j in n]
    lane_half = lax.broadcasted_iota(jnp.int32, (HALF, c), 1)
    chunk_row = lax.broadcasted_iota(jnp.int32, (c, 1), 0)
    rows = [[] for _ in n]
    for i in range(c // SUB):
        lo, hi = i * SUB, (i + 1) * SUB
        a_row = []
        for j in n:
            if rev[j]:
                has_off, edge, outside = hi < c, hi, chunk_row >= hi
            else:
                has_off, edge, outside = lo > 0, lo - 1, chunk_row < lo
            if has_off:
                beta = b[j][edge:edge + 1]
                qs = q[j][lo:hi] * jnp.exp(b[j][lo:hi] - beta)
                ks = k[j] * jnp.exp(jnp.where(outside, beta - b[j], -jnp.inf))
                a_row.append(_dot_nt(qs.astype(BF16), ks.astype(BF16)))
            else:
                a_row.append(jnp.zeros((SUB, c), F32))
        for half in range(SUB // HALF):
            h0 = lo + half * HALF
            piece = [a_row[j][half * HALF:(half + 1) * HALF] for j in n]
            for s in range(HALF):
                for j in n:
                    bh2 = b2[j][h0:h0 + HALF]
                    a_col = jnp.sum(jnp.exp2(bh2 - bh2[s:s + 1]) * q[j][h0:h0 + HALF] * k[j][h0 + s:h0 + s + 1],
                                    axis=-1, keepdims=True)
                    piece[j] = jnp.where(lane_half == h0 + s, a_col, piece[j])
            for j in n:
                rows[j].append(piece[j])
    second_half = (chunk_row % SUB) >= HALF
    same_block = (row // SUB) == (col // SUB)
    out = []
    for j in n:
        meet = HALF if rev[j] else HALF - 1
        beta = jnp.concatenate([jnp.broadcast_to(b[j][lo + meet:lo + meet + 1], (SUB, b[j].shape[1]))
                                for lo in range(0, c, SUB)], axis=0)
        t_side = jnp.logical_not(second_half) if rev[j] else second_half
        qs = q[j] * jnp.exp(jnp.where(t_side, b[j] - beta, -jnp.inf))
        ks = k[j] * jnp.exp(jnp.where(t_side, -jnp.inf, beta - b[j]))
        cross = jnp.where(same_block, _dot_nt(qs.astype(BF16), ks.astype(BF16)), 0.0)
        attn = jnp.where(tris[rev[j]], jnp.concatenate(rows[j], axis=0) + cross, 0.0)
        o = _dot(attn.astype(BF16), v[j].astype(BF16)) \
            + _dot_nt((q[j] * jnp.exp(b[j])).astype(BF16), st[j].astype(BF16))
        b_exit = b[j][0:1] if rev[j] else b[j][c - 1:c]
        k_out = k[j] * jnp.exp(b_exit - b[j])
        st_new = jnp.exp(b_exit) * st[j] + _dot_tn(v[j].astype(BF16), k_out.astype(BF16))
        out.append((o, st_new))
    return out


def _hgrn_kernel(seq_len, layer, q_ref, i_ref, g_ref, ff_ref, fb_ref, lb_ref, s0_ref,
                 o_ref, s_out_ref, of_ref, ob_ref, st_ref):
    c = SCAN_CHUNK
    hd = HEAD_DIM
    n_chunks = seq_len // c
    lbp = lb_ref[...]
    e = jnp.exp(lbp - jnp.max(lbp, axis=0, keepdims=True))
    lb = jnp.sum(e[0:layer + 1], axis=0, keepdims=True) / jnp.sum(e, axis=0, keepdims=True)

    for d in range(2):
        for hh in range(HG_STEP_HEADS):
            st_ref[d, hh] = s0_ref[d, hh].T

    def body(n, carry):
        where, chains = [], []
        for d in range(2):
            sl = pl.ds(pl.multiple_of((n_chunks - 1 - n if d else n) * c, c), c)
            for hh in range(HG_STEP_HEADS):
                cols = slice(hh * hd, (hh + 1) * hd)
                f = lb[:, cols] + (1.0 - lb[:, cols]) * _sigmoid((fb_ref if d else ff_ref)[sl, cols])
                where.append((d, hh, sl, cols))
                chains.append((bool(d), q_ref[sl, cols], 1.0 - f, i_ref[sl, cols], jnp.log(f), st_ref[d, hh]))
        for (d, hh, sl, cols), (o, st_new) in zip(where, _hgrn_chunks(chains)):
            st_ref[d, hh] = st_new
            (ob_ref if d else of_ref)[sl, cols] = o
        return carry

    lax.fori_loop(0, n_chunks, body, 0)
    for hh in range(HG_STEP_HEADS):
        cols = slice(hh * hd, (hh + 1) * hd)
        o = of_ref[:, cols] + ob_ref[:, cols]
        o = o * lax.rsqrt(jnp.mean(o * o, axis=-1, keepdims=True) + RMS_EPS)
        o_ref[:, cols] = o * _silu(g_ref[:, cols])
        for d in range(2):
            s_out_ref[d, hh] = st_ref[d, hh].T


def _hgrn(p, hg_lb, layer, s0, n_seq, seq_len, tok_offset):
    hd = HEAD_DIM
    sh = HG_STEP_HEADS
    steps_per_seq = HEADS // sh
    row0 = tok_offset // seq_len

    def col(part):
        return pl.BlockSpec((seq_len, sh * hd), lambda b, h: (row0 + b, part * steps_per_seq + h))

    state_spec = pl.BlockSpec((None, 2, sh, hd, hd), lambda b, h: (b, 0, h, 0, 0))
    return pl.pallas_call(
        functools.partial(_hgrn_kernel, seq_len, layer),
        grid=(n_seq, steps_per_seq),
        in_specs=[col(0), col(1), col(2), col(3), col(4),
                  pl.BlockSpec((DEPTH + 1, sh * hd), lambda b, h: (0, h)),
                  state_spec],
        out_specs=[pl.BlockSpec((seq_len, sh * hd), lambda b, h: (b, h)), state_spec],
        out_shape=[jax.ShapeDtypeStruct((n_seq * seq_len, GROUP_W), F32),
                   jax.ShapeDtypeStruct((n_seq, 2, HEADS, hd, hd), F32)],
        scratch_shapes=[pltpu.VMEM((seq_len, sh * hd), F32), pltpu.VMEM((seq_len, sh * hd), F32),
                        pltpu.VMEM((2, sh, hd, hd), F32)],
        compiler_params=_cparams("parallel", "parallel"),
        name=f"hgrn_l{seq_len}",
    )(p, p, p, p, p, hg_lb, s0)


def _short_conv3(x, w):
    n = x.shape[0]
    r = lax.broadcasted_iota(jnp.int32, (n, 1), 0)
    prev = jnp.where(r == 0, 0.0, pltpu.roll(x, 1, 0))
    nxt = jnp.where(r == n - 1, 0.0, pltpu.roll(x, n - 1, 0))
    return prev * w[0:1] + x * w[1:2] + nxt * w[2:3]


def _conv3_silu_tiles(dst_ref, src_ref, w_ref, scale):
    n, width = src_ref.shape
    c = SCAN_CHUNK
    r = lax.broadcasted_iota(jnp.int32, (c, 1), 0)
    for j in range(n // c):
        r0 = j * c
        for col in range(0, width, LANES):
            cols = slice(col, col + LANES)
            cur = src_ref[r0:r0 + c, cols]
            prev = jnp.where(r == 0, 0.0, pltpu.roll(cur, 1, 0)) if j == 0 else src_ref[r0 - 1:r0 + c - 1, cols]
            nxt = (jnp.where(r == c - 1, 0.0, pltpu.roll(cur, c - 1, 0)) if r0 + c == n
                   else src_ref[r0 + 1:r0 + c + 1, cols])
            w = w_ref[:, cols]
            y = _silu(prev * w[0:1] + cur * w[1:2] + nxt * w[2:3])
            dst_ref[r0:r0 + c, cols] = y if scale == 1.0 else y * scale


def _split2(x):
    hi = x.astype(BF16)
    return hi, (x - hi.astype(F32)).astype(BF16)


def _mlstm_chunks(chains):
    c = SCAN_CHUNK
    row = lax.broadcasted_iota(jnp.int32, (c, c), 0)
    col = lax.broadcasted_iota(jnp.int32, (c, c), 1)
    eye_b = jnp.where(row == col, 1.0, 0.0).astype(BF16)
    tris = {False: row <= col, True: row >= col}
    tris_b = {r: jnp.where(t, 1.0, 0.0).astype(BF16) for r, t in tris.items()}
    n = range(len(chains))
    rev, q, k, vt, ig, fg, ct, nv, m_prev = zip(*chains)

    def each(fn):
        return [fn(i) for i in n]

    def dot3(parts, rhs, nt=False):
        d = _dot_nt if nt else _dot
        return d(parts[0], rhs) + d(parts[1], rhs) + d(parts[2], rhs)

    lf = each(lambda i: _split3(jnp.broadcast_to(_log_sigmoid(fg[i]), (8, c))))
    b = each(lambda i: dot3(lf[i], tris_b[rev[i]])[0:1])
    us = each(lambda i: _split3(jnp.broadcast_to(ig[i] - b[i], (c, c))))
    u = each(lambda i: _dot_nt(eye_b, us[i][0]) + _dot_nt(eye_b, us[i][1]) + _dot_nt(eye_b, us[i][2]))
    dmat = each(lambda i: jnp.where(tris[rev[i]], b[i] + u[i], -jnp.inf))
    m_t = each(lambda i: jnp.maximum(b[i] + m_prev[i], jnp.max(dmat[i], axis=0, keepdims=True)))
    qb = each(lambda i: q[i].astype(BF16))
    kb = each(lambda i: k[i].astype(BF16))
    kq = each(lambda i: _dot_nt(kb[i], qb[i]))
    p = each(lambda i: jnp.exp(dmat[i] - m_t[i]) * kq[i])
    inter = each(lambda i: jnp.exp(b[i] + m_prev[i] - m_t[i]))
    ns = each(lambda i: _split2(jnp.broadcast_to(nv[i], (8, nv[i].shape[1]))))
    qs = each(lambda i: _split2(q[i]))
    qn = each(lambda i: (_dot_nt(ns[i][0], qs[i][0]) + _dot_nt(ns[i][1], qs[i][0])
                         + _dot_nt(ns[i][0], qs[i][1]))[0:1])
    den = each(lambda i: inter[i] * qn[i] + jnp.sum(p[i], axis=0, keepdims=True))
    scale = each(lambda i: 1.0 / jnp.maximum(jnp.abs(den[i]), jnp.exp(-m_t[i])))
    cq = each(lambda i: _dot_nt(ct[i].astype(BF16), qb[i]))
    vp = each(lambda i: _dot(vt[i].astype(BF16), p[i].astype(BF16)))
    ht = each(lambda i: (inter[i] * cq[i] + vp[i]) * scale[i])
    last = each(lambda i: 0 if rev[i] else c - 1)
    m_new = each(lambda i: m_t[i][:, last[i]:last[i] + 1])
    b_exit = each(lambda i: b[i][:, last[i]:last[i] + 1])
    w = each(lambda i: jnp.exp(b_exit[i] - b[i] + ig[i] - m_new[i]))
    dec = each(lambda i: jnp.exp(b_exit[i] + m_prev[i] - m_new[i]))
    vk = each(lambda i: _dot((vt[i] * w[i]).astype(BF16), kb[i]))
    ct_new = each(lambda i: dec[i] * ct[i] + vk[i])
    ws = each(lambda i: _split2(jnp.broadcast_to(w[i], (8, c))))
    ks = each(lambda i: _split2(k[i]))
    wk = each(lambda i: (_dot(ws[i][0], ks[i][0]) + _dot(ws[i][1], ks[i][0]) + _dot(ws[i][0], ks[i][1]))[0:1])
    nv_new = each(lambda i: dec[i] * nv[i] + wk[i])
    return [(ht[i], ct_new[i], nv_new[i], m_new[i]) for i in n]


def _mlstm_kernel(seq_len, q_ref, k_ref, v_ref, og_ref, gate_t_ref, cwq_ref, cwk_ref,
                  c0_ref, n0_ref, m0_ref,
                  o_ref, c_out_ref, n_out_ref, m_out_ref,
                  q2_ref, k2_ref, vt_ref, hf_ref, hb_ref, ct_ref, n_ref, m_ref):
    c = SCAN_CHUNK
    hd = HEAD_DIM
    n_chunks = seq_len // c
    _conv3_silu_tiles(q2_ref, q_ref, cwq_ref, 1.0)
    _conv3_silu_tiles(k2_ref, k_ref, cwk_ref, HEAD_DIM ** -0.5)
    for hh in range(HEADS):
        cols = slice(hh * hd, (hh + 1) * hd)
        for j in range(n_chunks):
            vt_ref[hh, :, j * c:(j + 1) * c] = v_ref[j * c:(j + 1) * c, cols].T
        for d in range(2):
            ct_ref[d, hh] = c0_ref[d, hh].T
    n_ref[...] = n0_ref[...]
    m_ref[...] = m0_ref[...]

    def body(n, carry):
        where, chains = [], []
        for d in range(2):
            sl = pl.ds(pl.multiple_of((n_chunks - 1 - n if d else n) * c, c), c)
            for hh in range(HEADS):
                cols = slice(hh * hd, (hh + 1) * hd)
                gr = gate_t_ref[hh, :, sl]
                where.append((d, hh, sl))
                chains.append((bool(d), q2_ref[sl, cols], k2_ref[sl, cols], vt_ref[hh, :, sl],
                               gr[2 * d:2 * d + 1, :], gr[2 * d + 1:2 * d + 2, :],
                               ct_ref[d, hh], n_ref[d, hh], m_ref[d, hh]))
        for (d, hh, sl), (ht, ct, nv, m_new) in zip(where, _mlstm_chunks(chains)):
            ct_ref[d, hh] = ct
            n_ref[d, hh] = nv
            m_ref[d, hh] = m_new
            (hb_ref if d else hf_ref)[hh, :, sl] = ht
        return carry

    lax.fori_loop(0, n_chunks, body, 0)
    for hh in range(HEADS):
        cols = slice(hh * hd, (hh + 1) * hd)
        for j in range(n_chunks):
            rows = slice(j * c, (j + 1) * c)
            h = (hf_ref[hh, :, rows] + hb_ref[hh, :, rows]).T
            h = h * lax.rsqrt(jnp.mean(h * h, axis=-1, keepdims=True) + RMS_EPS)
            o_ref[rows, cols] = h * _sigmoid(og_ref[rows, cols])
        for d in range(2):
            c_out_ref[d, hh] = ct_ref[d, hh].T
    n_out_ref[...] = n_ref[...]
    m_out_ref[...] = m_ref[...]


def _mlstm(p, gate_t_h, conv_w, c0, n0, m0, n_seq, seq_len, tok_offset):
    hd = HEAD_DIM
    gw = GROUP_W
    row0 = tok_offset // seq_len

    def col(part):
        return pl.BlockSpec((seq_len, gw), lambda b: (row0 + b, part))

    c_spec = pl.BlockSpec((None, 2, HEADS, hd, hd), lambda b: (b, 0, 0, 0, 0))
    n_spec = pl.BlockSpec((None, 2, HEADS, 1, hd), lambda b: (b, 0, 0, 0, 0))
    m_spec = pl.BlockSpec((None, 2, HEADS, 1, 1), lambda b: (b, 0, 0, 0, 0))
    return pl.pallas_call(
        functools.partial(_mlstm_kernel, seq_len),
        grid=(n_seq,),
        in_specs=[col(5), col(6), col(7), col(8),
                  pl.BlockSpec((HEADS, 4, seq_len), lambda b: (0, 0, row0 + b)),
                  pl.BlockSpec((3, gw), lambda b: (0, 0)),
                  pl.BlockSpec((3, gw), lambda b: (0, 1)),
                  c_spec, n_spec, m_spec],
        out_specs=[pl.BlockSpec((seq_len, gw), lambda b: (b, 0)), c_spec, n_spec, m_spec],
        out_shape=[jax.ShapeDtypeStruct((n_seq * seq_len, gw), F32),
                   jax.ShapeDtypeStruct((n_seq, 2, HEADS, hd, hd), F32),
                   jax.ShapeDtypeStruct((n_seq, 2, HEADS, 1, hd), F32),
                   jax.ShapeDtypeStruct((n_seq, 2, HEADS, 1, 1), F32)],
        scratch_shapes=[pltpu.VMEM((seq_len, gw), F32), pltpu.VMEM((seq_len, gw), F32),
                        pltpu.VMEM((HEADS, hd, seq_len), F32),
                        pltpu.VMEM((HEADS, hd, seq_len), F32), pltpu.VMEM((HEADS, hd, seq_len), F32),
                        pltpu.VMEM((2, HEADS, hd, hd), F32), pltpu.VMEM((2, HEADS, 1, hd), F32),
                        pltpu.VMEM((2, HEADS, 1, 1), F32)],
        compiler_params=_cparams("parallel"),
        name=f"mlstm_l{seq_len}",
    )(p, p, p, p, gate_t_h, conv_w, conv_w, c0, n0, m0)


def _dft_tables(seq_len):
    n = 2 * seq_len
    k = jnp.arange(seq_len, dtype=jnp.int32)[:, None]
    t = jnp.arange(seq_len, dtype=jnp.int32)[None, :]
    ang = ((k * t) % n).astype(F32) * (2.0 * np.pi / n)
    fc = jnp.cos(ang)
    fs = jnp.sin(ang)
    nyq = jnp.where(t % 2 == 0, 1.0, -1.0).astype(F32)
    fs = jnp.where(k == 0, nyq, fs)
    return jnp.concatenate([fc, fs], axis=0)


def _filter_kernel(seq_len, z_ref, w1_ref, b1_ref, w2_ref, b2_ref, w3f_ref, w3b_ref, b3f_ref, b3b_ref,
                   f0_ref, f1_ref, rf_ref, rb_ref, fhi_ref, flo_ref, kf_ref, a_ref):
    hp = lax.Precision.HIGHEST
    n = 2 * seq_len
    z = z_ref[...]

    @pl.when(jnp.logical_and(pl.program_id(0) == 0, pl.program_id(1) == 0))
    def _():
        a1 = jnp.sin(f0_ref[...] * (jnp.dot(z, w1_ref[...], precision=hp, preferred_element_type=F32)
                                    + b1_ref[...]))
        a_ref[...] = jnp.sin(f1_ref[...] * (jnp.dot(a1, w2_ref[...], precision=hp, preferred_element_type=F32)
                                            + b2_ref[...]))

    a = a_ref[...]
    t_norm = z[:, 0:1]
    hf = (jnp.dot(a, w3f_ref[...], precision=hp, preferred_element_type=F32) + b3f_ref[...]) \
        * jnp.exp(-t_norm * jnp.exp(rf_ref[...]))
    hb = (jnp.dot(a, w3b_ref[...], precision=hp, preferred_element_type=F32) + b3b_ref[...]) \
        * jnp.exp(-t_norm * jnp.exp(rb_ref[...]))
    inv = lax.rsqrt(jnp.sum(hf * hf, axis=0, keepdims=True) + jnp.sum(hb * hb, axis=0, keepdims=True))
    hf = hf * inv
    r = lax.broadcasted_iota(jnp.int32, (seq_len, 1), 0)
    hb = jnp.where(r == 0, 0.0, hb * inv)
    sh, sl = _split2(hf + hb)
    dh, dl = _split2(hf - hb)
    fhi = fhi_ref[...]
    flo = flo_ref[...]
    kc = _dot(fhi[:seq_len], sh) + _dot(fhi[:seq_len], sl) + _dot(flo[:seq_len], sh)
    ks = _dot(fhi[seq_len:], dh) + _dot(fhi[seq_len:], dl) + _dot(flo[seq_len:], dh)
    sign = jnp.where(r % 2 == 0, 1.0, -1.0)
    k_nyq = jnp.sum(sign * (hf + hb), axis=0, keepdims=True)
    ks = jnp.where(r == 0, k_nyq, ks)
    scale = jnp.where(r == 0, 1.0 / n, 2.0 / n)
    kf_ref[0:seq_len, :] = kc * scale
    kf_ref[seq_len:n, :] = ks * scale


def _hyena_filters(seq_len, w1, b1, w2, b2, w3, b3, freq, log_rate, f_tab):
    d = D_MODEL
    cb = 256
    t = jnp.arange(seq_len, dtype=F32)
    t_norm = t / (seq_len - 1)
    bands = jnp.linspace(1e-4, HY_BANDS - 1, HY_BANDS, dtype=F32)
    ang = (2.0 * np.pi / seq_len) * t[:, None] * bands[None, :]
    z = jnp.concatenate([t_norm[:, None], jnp.cos(ang), jnp.sin(ang)], axis=-1)
    kpad = 128 - HY_EMB
    z = jnp.pad(z, ((0, 0), (0, kpad)))
    w1p = jnp.pad(w1, ((0, kpad), (0, 0)))
    f_hi = f_tab.astype(BF16)
    f_lo = (f_tab - f_hi.astype(F32)).astype(BF16)
    n_cb = d // cb
    hh = HY_HIDDEN
    row = lambda a: a.reshape(1, -1)
    const = lambda shape: pl.BlockSpec(shape, lambda o, j: (0,) * len(shape))
    fwd = lambda rows: pl.BlockSpec((rows, cb), lambda o, j: (0, o * n_cb + j))
    bwd = lambda rows: pl.BlockSpec((rows, cb), lambda o, j: (0, (HY_ORDER + o) * n_cb + j))
    return pl.pallas_call(
        functools.partial(_filter_kernel, seq_len),
        grid=(HY_ORDER, n_cb),
        in_specs=[const((seq_len, 128)), const((128, hh)), const((1, hh)), const((hh, hh)), const((1, hh)),
                  fwd(hh), bwd(hh), fwd(1), bwd(1),
                  const((1, hh)), const((1, hh)), fwd(1), bwd(1),
                  const((2 * seq_len, seq_len)), const((2 * seq_len, seq_len))],
        out_specs=pl.BlockSpec((None, 2 * seq_len, cb), lambda o, j: (o, 0, j)),
        out_shape=jax.ShapeDtypeStruct((HY_ORDER, 2 * seq_len, d), F32),
        scratch_shapes=[pltpu.VMEM((seq_len, hh), F32)],
        compiler_params=_cparams("arbitrary", "arbitrary"),
        name=f"hyena_filter_l{seq_len}",
    )(z, w1p, row(b1), w2, row(b2), w3, w3, row(b3), row(b3),
      row(freq[0]), row(freq[1]), row(log_rate), row(log_rate), f_hi, f_lo)


def _hyena_kernel(seq_len, seqs, v_ref, x1_ref, x2_ref, cwv_ref, cw1_ref, cw2_ref, bias_ref, kf_ref,
                  f_ref, ft_ref, o_ref, z_ref, zb_ref, y_ref):
    kc = min(HY_FREQ_CHUNK, seq_len)
    n_k = seq_len // kc
    r = lax.broadcasted_iota(jnp.int32, (kc, 1), 0)
    gate_refs = ((x1_ref, cw1_ref), (x2_ref, cw2_ref))
    for s in range(seqs):
        rows = slice(s * seq_len, (s + 1) * seq_len)
        z_ref[s] = _short_conv3(v_ref[rows, :], cwv_ref[...])
    for o in range(HY_ORDER):
        for s in range(seqs):
            zb_ref[s] = z_ref[s].astype(BF16)
            y_ref[s] = jnp.zeros(y_ref.shape[1:], F32)

        def freq_chunk(j, carry):
            r0 = pl.multiple_of(j * kc, kc)
            k_cos = kf_ref[o, pl.ds(r0, kc), :]
            k_sin = kf_ref[o, pl.ds(seq_len + r0, kc), :]
            real_row = jnp.logical_and(r == 0, j == 0)
            for s in range(seqs):
                a = _dot(f_ref[pl.ds(r0, kc), :], zb_ref[s])
                bm = _dot(f_ref[pl.ds(seq_len + r0, kc), :], zb_ref[s])
                yc = a * k_cos - jnp.where(real_row, 0.0, bm * k_sin)
                ys = jnp.where(real_row, bm * k_sin, a * k_sin + bm * k_cos)
                y_ref[s] += _dot(ft_ref[j], yc.astype(BF16)) + _dot(ft_ref[n_k + j], ys.astype(BF16))
            return carry

        lax.fori_loop(0, n_k, freq_chunk, 0)
        x_ref, cw_ref = gate_refs[o]
        for s in range(seqs):
            rows = slice(s * seq_len, (s + 1) * seq_len)
            gate = _short_conv3(x_ref[rows, :], cw_ref[...])
            z_ref[s] = gate * (y_ref[s] + z_ref[s] * bias_ref[o:o + 1, :])
    for s in range(seqs):
        o_ref[s * seq_len:(s + 1) * seq_len, :] = z_ref[s]


def _hyena(u, conv_w, bias, kf, f_tab, n_seq, seq_len, tok_offset):
    d = D_MODEL
    cb = 256
    n_cb = d // cb
    seqs = max(1, HY_STEP_ROWS // seq_len)
    rows = seqs * seq_len
    row0 = tok_offset // rows
    kc = min(HY_FREQ_CHUNK, seq_len)
    n_k = seq_len // kc
    f_bf = f_tab.astype(BF16)
    ft = f_bf.T.reshape(seq_len, 2 * n_k, kc).transpose(1, 0, 2)

    def part(k):
        return pl.BlockSpec((rows, cb), lambda j, b: (row0 + b, k * n_cb + j))

    def cw(k):
        return pl.BlockSpec((3, cb), lambda j, b: (0, k * n_cb + j))

    return pl.pallas_call(
        functools.partial(_hyena_kernel, seq_len, seqs),
        grid=(n_cb, n_seq // seqs),
        in_specs=[part(0), part(1), part(2), cw(0), cw(1), cw(2),
                  pl.BlockSpec((HY_ORDER, cb), lambda j, b: (0, j)),
                  pl.BlockSpec((HY_ORDER, 2 * seq_len, cb), lambda j, b: (0, 0, j)),
                  pl.BlockSpec((2 * seq_len, seq_len), lambda j, b: (0, 0)),
                  pl.BlockSpec((2 * n_k, seq_len, kc), lambda j, b: (0, 0, 0))],
        out_specs=pl.BlockSpec((rows, cb), lambda j, b: (b, j)),
        out_shape=jax.ShapeDtypeStruct((n_seq * seq_len, d), F32),
        scratch_shapes=[pltpu.VMEM((seqs, seq_len, cb), F32), pltpu.VMEM((seqs, seq_len, cb), BF16),
                        pltpu.VMEM((seqs, seq_len, cb), F32)],
        compiler_params=_cparams("parallel", "parallel"),
        name=f"hyena_l{seq_len}",
    )(u, u, u, conv_w, conv_w, conv_w, bias, kf, f_bf, ft)


def _mix_out_kernel(n_parts, *refs):
    o_refs = refs[:2 * n_parts]
    w_refs = refs[2 * n_parts:3 * n_parts]
    x_ref, mod_ref, g_ref, wr_both_ref, br_ref = refs[3 * n_parts:3 * n_parts + 5]
    x1_ref, h_ref, eid_ref, wts_ref, rank_ref, cnt_ref, run_ref = refs[3 * n_parts + 5:]
    i = pl.program_id(0)
    t = TOK_TILE
    ne = N_EXPERTS

    from_prompt = i < N_PROMPT // TOK_TILE
    y = None
    for j in range(n_parts):
        o = jnp.where(from_prompt, o_refs[2 * j][...], o_refs[2 * j + 1][...])
        yj = _dot(o.astype(BF16), w_refs[j][...])
        y = yj if y is None else y + yj
    x1 = x_ref[...] + mod_ref[2:3, :] * y
    x1_ref[...] = x1
    h = _norm_mod(x1, g_ref[...], mod_ref[4:5, :], mod_ref[3:4, :])
    h_ref[...] = _pack_bf16_halves(h)
    h_hi = h.astype(BF16)
    h_lo = (h - h_hi.astype(F32)).astype(BF16)
    hi_terms = _dot(h_hi, wr_both_ref[...])
    logits = (hi_terms[:, :LANES] + hi_terms[:, LANES:] + _dot(h_lo, wr_both_ref[:, :LANES])).T[:ne] \
        + br_ref[...]

    @pl.when(i == 0)
    def _():
        run_ref[...] = jnp.zeros_like(run_ref)

    e_iota = lax.broadcasted_iota(jnp.int32, (ne, t), 0)
    vals, eids, onehots = [], [], []
    for _k in range(TOP_K):
        m = jnp.max(logits, axis=0, keepdims=True)
        eid = jnp.min(jnp.where(logits == m, e_iota, ne), axis=0, keepdims=True)
        sel = e_iota == eid
        logits = jnp.where(sel, -jnp.inf, logits)
        onehots.append(jnp.where(sel, 1.0, 0.0))
        vals.append(m)
        eids.append(eid)
    r2 = lax.broadcasted_iota(jnp.int32, (t, t), 0)
    c2 = lax.broadcasted_iota(jnp.int32, (t, t), 1)
    before = jnp.where(r2 < c2, 1.0, 0.0).astype(BF16)
    earlier = _dot(jnp.concatenate(onehots, axis=0).astype(BF16), before)
    running = run_ref[...]
    ranks = []
    for k, onehot in enumerate(onehots):
        ranks.append(jnp.sum(onehot * (running + earlier[k * ne:(k + 1) * ne]), axis=0, keepdims=True))
        running = running + jnp.sum(onehot, axis=1, keepdims=True)
    run_ref[...] = running
    cnt_ref[...] = running
    v = jnp.concatenate(vals, axis=0)
    ex = jnp.exp(v - v[0:1])
    wts_ref[...] = ex / jnp.sum(ex, axis=0, keepdims=True)
    eid_ref[...] = jnp.concatenate(eids, axis=0)
    rank_ref[...] = jnp.concatenate(ranks, axis=0).astype(jnp.int32)


def _mix_out(parts, x, mod_l, norm_g, w_router, b_router):
    d = D_MODEL
    t = TOK_TILE
    ne = N_EXPERTS
    n_parts = len(parts)
    n_prompt_tiles = N_PROMPT // t
    wr = jnp.pad(w_router, ((0, 0), (0, LANES - ne)))
    wr_hi = wr.astype(BF16)
    wr_both = jnp.concatenate([wr_hi, (wr - wr_hi.astype(F32)).astype(BF16)], axis=1)
    in_specs = []
    for o_p, _, _ in parts:
        in_specs.append(pl.BlockSpec((t, o_p.shape[1]), lambda i: (jnp.minimum(i, n_prompt_tiles - 1), 0)))
        in_specs.append(pl.BlockSpec((t, o_p.shape[1]), lambda i: (jnp.maximum(i - n_prompt_tiles, 0), 0)))
    in_specs += [pl.BlockSpec(w.shape, lambda i: (0, 0)) for _, _, w in parts]
    in_specs += [
        pl.BlockSpec((t, d), lambda i: (i, 0)),
        pl.BlockSpec((None, N_MOD, d), lambda i: (_tile_cond_row(i), 0, 0)),
        pl.BlockSpec((1, d), lambda i: (0, 0)),
        pl.BlockSpec((d, 2 * LANES), lambda i: (0, 0)),
        pl.BlockSpec((ne, 1), lambda i: (0, 0)),
    ]
    tok_major = pl.BlockSpec((TOP_K, t), lambda i: (0, i))
    return pl.pallas_call(
        functools.partial(_mix_out_kernel, n_parts),
        grid=(N_TOK // t,),
        in_specs=in_specs,
        out_specs=[pl.BlockSpec((t, d), lambda i: (i, 0)), pl.BlockSpec((t, d // 2), lambda i: (i, 0)),
                   tok_major, tok_major, tok_major, pl.BlockSpec((ne, 1), lambda i: (0, 0))],
        out_shape=[jax.ShapeDtypeStruct((N_TOK, d), F32), jax.ShapeDtypeStruct((N_TOK, d // 2), jnp.uint32),
                   jax.ShapeDtypeStruct((TOP_K, N_TOK), jnp.int32), jax.ShapeDtypeStruct((TOP_K, N_TOK), F32),
                   jax.ShapeDtypeStruct((TOP_K, N_TOK), jnp.int32), jax.ShapeDtypeStruct((ne, 1), F32)],
        scratch_shapes=[pltpu.VMEM((ne, 1), F32)],
        compiler_params=_cparams("arbitrary"),
        name="mix_out_router",
    )(*[o for part in parts for o in part[:2]], *[w.astype(BF16) for _, _, w in parts], x, mod_l,
      norm_g.reshape(1, d), wr_both, b_router.reshape(ne, 1))


def _sc_row_gather(src, idx):
    n = idx.shape[0]
    width = src.shape[1]
    step_rows = min(SC_MAX_INDICES, SC_STEP_BYTES // (width * 4))
    sc = plsc.get_sparse_core_info()
    n_workers = sc.num_cores * sc.num_subcores
    per_worker = n // n_workers
    if n % n_workers or per_worker % step_rows:
        raise ValueError("row count must be whole SparseCore steps on every subcore")
    mesh = plsc.VectorSubcoreMesh(core_axis_name="core", subcore_axis_name="subcore")

    @functools.partial(pl.kernel, out_type=jax.ShapeDtypeStruct((n, width), src.dtype), mesh=mesh,
                       scratch_types=[pltpu.VMEM((per_worker,), jnp.int32),
                                      pltpu.VMEM((step_rows, width), src.dtype)],
                       name="sc_row_gather")
    def gather(src_hbm, idx_hbm, dst_hbm, idx_vmem, rows_vmem):
        worker = lax.axis_index("subcore") * sc.num_cores + lax.axis_index("core")
        base = worker * per_worker
        pltpu.sync_copy(idx_hbm.at[pl.ds(base, per_worker)], idx_vmem)

        @pl.loop(0, per_worker // step_rows)
        def _(c):
            pltpu.sync_copy(src_hbm.at[idx_vmem.at[pl.ds(c * step_rows, step_rows)]], rows_vmem)
            pltpu.sync_copy(rows_vmem, dst_hbm.at[pl.ds(base + c * step_rows, step_rows)])

    return gather(src, idx)


def _sc_row_scatter(src, row_of_slot):
    n = row_of_slot.shape[0]
    n_src, width = src.shape
    step_rows = min(SC_MAX_INDICES, SC_STEP_BYTES // (width * 4))
    sc = plsc.get_sparse_core_info()
    n_workers = sc.num_cores * sc.num_subcores
    per_worker = n // n_workers
    steps = per_worker // step_rows
    if n % n_workers or per_worker % step_rows or n_src % step_rows:
        raise ValueError("row counts must be whole SparseCore steps on every subcore")
    mesh = plsc.VectorSubcoreMesh(core_axis_name="core", subcore_axis_name="subcore")

    @functools.partial(pl.kernel, out_type=jax.ShapeDtypeStruct((n, width), src.dtype), mesh=mesh,
                       scratch_types=[pltpu.VMEM((steps, step_rows), jnp.int32),
                                      pltpu.VMEM((step_rows, width), src.dtype)],
                       name="sc_row_scatter")
    def scatter(src_hbm, idx_hbm, dst_hbm, idx_vmem, rows_vmem):
        worker = lax.axis_index("subcore") * sc.num_cores + lax.axis_index("core")
        pltpu.sync_copy(idx_hbm.at[worker], idx_vmem)

        @pl.loop(0, steps)
        def _(c):
            src0 = lax.rem(worker * per_worker + c * step_rows, n_src)
            pltpu.sync_copy(src_hbm.at[pl.ds(src0, step_rows)], rows_vmem)
            pltpu.sync_copy(rows_vmem, dst_hbm.at[idx_vmem.at[c]])

    return scatter(src, row_of_slot.reshape(n_workers, steps, step_rows))


def _pack_bf16_halves(x):
    w = x.shape[1] // 2
    bits = pltpu.bitcast(x.astype(BF16).astype(F32), jnp.uint32)
    return bits[:, :w] | (bits[:, w:] >> 16)


def _unpack_bf16_halves(p):
    hi = pltpu.bitcast(p & jnp.uint32(0xFFFF0000), F32).astype(BF16)
    lo = pltpu.bitcast(p << 16, F32).astype(BF16)
    return hi, lo


def _experts_kernel(layer, te_ref, first_ref, slot_ref, next_ref, nv_ref,
                    x_ref, wgu_hbm, bg_ref, bu_ref, wd_hbm, bd_ref, sel_ref,
                    y_ref, wgu_buf, wd_buf, wg_ref, wu_ref, wdb_ref, sem):
    i = pl.program_id(0)
    valid = i < nv_ref[0]
    half = DEINT_COLS // 2
    k_half = x_ref.shape[1]

    def fetch(expert, slot):
        return (pltpu.make_async_copy(wgu_hbm.at[layer, expert], wgu_buf.at[slot], sem.at[slot, 0]),
                pltpu.make_async_copy(wd_hbm.at[layer, expert], wd_buf.at[slot], sem.at[slot, 1]))

    @pl.when(i == 0)
    def _():
        for cp in fetch(te_ref[0], 0):
            cp.start()

    @pl.when(jnp.logical_and(valid, first_ref[i] == 1))
    def _():
        slot = slot_ref[i]
        for cp in fetch(te_ref[i], slot):
            cp.wait()

        @pl.when(next_ref[i] >= 0)
        def _():
            for cp in fetch(next_ref[i], 1 - slot):
                cp.start(priority=1)

        for c in range(wgu_buf.shape[2] // DEINT_COLS):
            w = wgu_buf[slot, :, c * DEINT_COLS:(c + 1) * DEINT_COLS].astype(BF16)
            split = _dot(w, sel_ref[...]).astype(BF16)
            wg_ref[:, c * half:(c + 1) * half] = split[:, :half]
            wu_ref[:, c * half:(c + 1) * half] = split[:, half:]
        wdb_ref[...] = wd_buf[slot].astype(BF16)

    @pl.when(valid)
    def _():
        x_l, x_r = _unpack_bf16_halves(x_ref[...])
        gl = _dot(x_l, wg_ref[:k_half, :]) + _dot(x_r, wg_ref[k_half:, :]) + bg_ref[...]
        up = _dot(x_l, wu_ref[:k_half, :]) + _dot(x_r, wu_ref[k_half:, :]) + bu_ref[...]
        gl = jnp.minimum(gl, SWIGLU_LIMIT)
        up = jnp.clip(up, -SWIGLU_LIMIT, SWIGLU_LIMIT)
        act = (up + 1.0) * gl * _sigmoid(SWIGLU_ALPHA * gl)
        y_ref[...] = _pack_bf16_halves(_dot(act.astype(BF16), wdb_ref[...]) + bd_ref[...])

    @pl.when(jnp.logical_not(valid))
    def _():
        y_ref[...] = jnp.zeros_like(y_ref)


def _experts(xs, tile_expert, tile_first, n_valid, layer, w_gu, b_gate, b_up, w_down, b_down):
    d = D_MODEL
    tm = MOE_TILE
    n_tiles = MOE_ROWS // tm
    ff = w_down.shape[2]
    half = DEINT_COLS // 2
    r = jnp.arange(DEINT_COLS)[:, None]
    c = jnp.arange(DEINT_COLS)[None, :]
    sel = (r == jnp.where(c < half, 2 * c, 2 * (c - half) + 1)).astype(BF16)
    group = jnp.cumsum(tile_first) - 1
    tile_slot = (group % 2).astype(jnp.int32)
    is_last_group = group == group[-1]
    following = jnp.concatenate([tile_expert[1:], tile_expert[-1:]])
    idx = jnp.arange(n_tiles, dtype=jnp.int32)
    group_end = jnp.max(jnp.where(group[None, :] == group[:, None], idx[None, :], -1), axis=1)
    tile_next = jnp.where(is_last_group, -1, following[group_end]).astype(jnp.int32)
    wspec = lambda k, n: pl.BlockSpec((None, None, k, n), lambda i, *_: (layer, _[0][i], 0, 0))
    grid_spec = pltpu.PrefetchScalarGridSpec(
        num_scalar_prefetch=5,
        grid=(n_tiles,),
        in_specs=[pl.BlockSpec((tm, d // 2), lambda i, *_: (i, 0)),
                  pl.BlockSpec(memory_space=pl.ANY), wspec(1, ff), wspec(1, ff),
                  pl.BlockSpec(memory_space=pl.ANY), wspec(1, d),
                  pl.BlockSpec((DEINT_COLS, DEINT_COLS), lambda i, *_: (0, 0))],
        out_specs=pl.BlockSpec((tm, d // 2), lambda i, *_: (i, 0)),
        scratch_shapes=[pltpu.VMEM((2, d, 2 * ff), F32), pltpu.VMEM((2, ff, d), F32),
                        pltpu.VMEM((d, ff), BF16), pltpu.VMEM((d, ff), BF16), pltpu.VMEM((ff, d), BF16),
                        pltpu.SemaphoreType.DMA((2, 2))],
    )
    return pl.pallas_call(
        functools.partial(_experts_kernel, layer),
        grid_spec=grid_spec,
        out_shape=jax.ShapeDtypeStruct((MOE_ROWS, d // 2), jnp.uint32),
        compiler_params=_cparams("arbitrary"),
        name="experts",
    )(tile_expert, tile_first, tile_slot, tile_next, n_valid, xs, w_gu, b_gate, b_up, w_down, b_down, sel)


def _combine_kernel(final, x_ref, g_ref, w_ref, mod_ref, fg_ref, *o_refs):
    y_l = y_r = None
    for k in range(TOP_K):
        g_l, g_r = _unpack_bf16_halves(g_ref[k])
        wk = w_ref[:, k:k + 1]
        y_l = wk * g_l.astype(F32) if y_l is None else y_l + wk * g_l.astype(F32)
        y_r = wk * g_r.astype(F32) if y_r is None else y_r + wk * g_r.astype(F32)
    x = x_ref[...] + mod_ref[5:6, :] * jnp.concatenate([y_l, y_r], axis=1)
    if not final:
        o_refs[0][...] = x
        return
    x = x * lax.rsqrt(jnp.mean(x * x, axis=-1, keepdims=True) + RMS_EPS) * fg_ref[...]
    from_prompt = pl.program_id(0) < N_PROMPT // TOK_TILE

    @pl.when(from_prompt)
    def _():
        o_refs[0][...] = x

    @pl.when(jnp.logical_not(from_prompt))
    def _():
        o_refs[1][...] = x


def _combine(x1, gathered, wts, mod_l, final_g, final):
    d = D_MODEL
    t = TOK_TILE
    n_prompt_tiles = N_PROMPT // t
    if final:
        out_specs = [pl.BlockSpec((t, d), lambda i: (jnp.minimum(i, n_prompt_tiles - 1), 0)),
                     pl.BlockSpec((t, d), lambda i: (jnp.maximum(i - n_prompt_tiles, 0), 0))]
        out_shape = [jax.ShapeDtypeStruct((N_PROMPT, d), F32), jax.ShapeDtypeStruct((N_SAMPLE, d), F32)]
    else:
        out_specs = pl.BlockSpec((t, d), lambda i: (i, 0))
        out_shape = jax.ShapeDtypeStruct((N_TOK, d), F32)
    return pl.pallas_call(
        functools.partial(_combine_kernel, final),
        grid=(N_TOK // t,),
        in_specs=[pl.BlockSpec((t, d), lambda i: (i, 0)),
                  pl.BlockSpec((TOP_K, t, d // 2), lambda i: (0, i, 0)),
                  pl.BlockSpec((t, TOP_K), lambda i: (i, 0)),
                  pl.BlockSpec((None, N_MOD, d), lambda i: (_tile_cond_row(i), 0, 0)),
                  pl.BlockSpec((1, d), lambda i: (0, 0))],
        out_specs=out_specs,
        out_shape=out_shape,
        compiler_params=_cparams("arbitrary"),
        name="moe_combine",
    )(x1, gathered, wts, mod_l, final_g.reshape(1, d))


def _moe(x1, h, eid, wts, rank, counts, mod_l, layer, w_gu, b_gu, w_down, b_down, final_g, final):
    d = D_MODEL
    tm = MOE_TILE
    n_tiles = MOE_ROWS // tm
    cnt = counts.reshape(N_EXPERTS).astype(jnp.int32)
    gsz = ((cnt + tm - 1) // tm) * tm
    ends = jnp.cumsum(gsz)
    offs = ends - gsz
    e_ids = jnp.arange(N_EXPERTS, dtype=jnp.int32)
    pos = jnp.sum(jnp.where(eid[..., None] == e_ids, offs, 0), axis=-1) + rank
    tile_start = jnp.arange(n_tiles, dtype=jnp.int32) * tm
    tile_expert = jnp.minimum(jnp.sum((ends[None, :] <= tile_start[:, None]).astype(jnp.int32), axis=1),
                              N_EXPERTS - 1)
    n_valid = (ends[-1:] // tm).astype(jnp.int32)
    last_valid = jnp.maximum(n_valid[0] - 1, 0)
    tile_expert = jnp.where(jnp.arange(n_tiles) < n_valid[0], tile_expert, tile_expert[last_valid])
    tile_first = jnp.concatenate([jnp.ones((1,), jnp.int32),
                                  (tile_expert[1:] != tile_expert[:-1]).astype(jnp.int32)])
    j = jnp.arange(tm, dtype=jnp.int32)[None, :]
    pad_used = j < (gsz - cnt)[:, None]
    n_unused_before = jnp.cumsum((~pad_used).reshape(-1).astype(jnp.int32)) - 1
    pad_pos = jnp.where(pad_used, (offs + cnt)[:, None] + j,
                        ends[-1] + n_unused_before.reshape(N_EXPERTS, tm))
    row_of_slot = jnp.concatenate([pos.reshape(-1), pad_pos.reshape(-1)])

    xs = _sc_row_scatter(h, row_of_slot)
    ys = _experts(xs, tile_expert, tile_first, n_valid, layer, w_gu,
                  b_gu[:, :, None, 0::2], b_gu[:, :, None, 1::2], w_down, b_down[:, :, None, :])
    gathered = _sc_row_gather(ys, pos.reshape(-1))
    return _combine(x1, gathered.reshape(TOP_K, N_TOK, d // 2), wts.T, mod_l, final_g, final)


def _grid_positions(n_tok, d):
    rows = n_tok // GRID_W
    r, col = jnp.meshgrid(jnp.arange(rows, dtype=F32), jnp.arange(GRID_W, dtype=F32), indexing='ij')
    r = r.reshape(-1)
    col = col.reshape(-1)
    quarter = d // 4
    inv = 1.0 / (10000.0 ** (jnp.arange(quarter, dtype=F32) / quarter))
    ar = r[:, None] * inv[None]
    ac = col[:, None] * inv[None]
    return jnp.concatenate([jnp.sin(ar), jnp.cos(ar), jnp.sin(ac), jnp.cos(ac)], axis=-1)


def kernel(x_prompt, x_sample, state_hgrn, state_mlstm_c, state_mlstm_n, state_mlstm_m, c, c_ctx,
           norm_g, final_g, w_mod, b_mod, ev_w_in, ev_gate_b, ev_conv, hg_lb, ev_w_out,
           hy_w_in, hy_conv, hy_w1, hy_b1, hy_w2, hy_b2, hy_w3, hy_b3, hy_freq, hy_log_rate, hy_bias, hy_w_out,
           w_router, b_router, w_gu, b_gu, w_down, b_down):
    d = D_MODEL
    hd = HEAD_DIM
    cond = jnp.concatenate([c_ctx[None], c, jnp.zeros((N_COND - 1 - DEC_BATCH, d), F32)], axis=0)
    mod = _modulation(cond, w_mod, b_mod)
    pos_tab = jnp.concatenate([jnp.zeros((TOK_TILE, d), F32), _grid_positions(DEC_SEQ, d)], axis=0)

    groups = ((BATCH, SEQ, 0), (DEC_BATCH, DEC_SEQ, N_PROMPT))
    new_states = None
    for l in range(DEPTH):
        if l % 2 == 0:
            e = l // 2
            if l == 0:
                x, p, gate_t = _proj_even(x_prompt.reshape(N_PROMPT, d), x_sample.reshape(N_SAMPLE, d),
                                          pos_tab, mod[l], norm_g[l, 0], ev_w_in[e], ev_gate_b[e])
            else:
                raise NotImplementedError("only the first layer adds grid positions")
            gate_t_h = gate_t.reshape(4, HEADS, N_TOK).transpose(1, 0, 2)
            o_hg, o_ml = [], []
            for gi, (n_seq, seq_len, off) in enumerate(groups):
                if gi == 0:
                    s0 = jnp.zeros((n_seq, 2, HEADS, hd, hd), F32)
                    c0 = jnp.zeros((n_seq, 2, HEADS, hd, hd), F32)
                    n0 = jnp.zeros((n_seq, 2, HEADS, 1, hd), F32)
                    m0 = jnp.zeros((n_seq, 2, HEADS, 1, 1), F32)
                else:
                    s0 = state_hgrn[:, e]
                    c0 = state_mlstm_c[:, e]
                    n0 = state_mlstm_n[:, e].reshape(n_seq, 2, HEADS, 1, hd)
                    m0 = state_mlstm_m[:, e].reshape(n_seq, 2, HEADS, 1, 1)
                og, s_fin = _hgrn(p, hg_lb, l, s0, n_seq, seq_len, off)
                om, c_fin, n_fin, m_fin = _mlstm(p, gate_t_h, ev_conv[e], c0, n0, m0, n_seq, seq_len, off)
                o_hg.append(og)
                o_ml.append(om)
                if gi == 0:
                    new_states = (s_fin[:, None], c_fin[:, None],
                                  n_fin.reshape(n_seq, 1, 2, HEADS, hd), m_fin.reshape(n_seq, 1, 2, HEADS))
            parts = [(*o_hg, ev_w_out[e][:GROUP_W]), (*o_ml, ev_w_out[e][GROUP_W:])]
        else:
            o = l // 2
            u = _proj_odd(x, mod[l], norm_g[l, 0], hy_w_in[o])
            zs = []
            for n_seq, seq_len, off in groups:
                f_tab = _dft_tables(seq_len)
                kf = _hyena_filters(seq_len, hy_w1[o], hy_b1[o], hy_w2[o], hy_b2[o], hy_w3[o], hy_b3[o],
                                    hy_freq[o], hy_log_rate[o], f_tab)
                zs.append(_hyena(u, hy_conv[o], hy_bias[o], kf, f_tab, n_seq, seq_len, off))
            parts = [(*zs, hy_w_out[o])]
        x1, h, eid, wts, rank, counts = _mix_out(parts, x, mod[l], norm_g[l, 1], w_router[l], b_router[l])
        x = _moe(x1, h, eid, wts, rank, counts, mod[l], l, w_gu, b_gu, w_down, b_down,
                 final_g, final=(l == DEPTH - 1))

    y_prompt, y_sample = x
    return (y_prompt.reshape(BATCH, SEQ, d), y_sample.reshape(DEC_BATCH, DEC_SEQ, d)) + new_states
```

```python
import functools

import numpy as np
import jax
import jax.numpy as jnp
from jax import lax
from jax.experimental import pallas as pl
from jax.experimental.pallas import tpu as pltpu
from jax.experimental.pallas import tpu_sc as plsc

F32 = jnp.float32
BF16 = jnp.bfloat16

D_MODEL = 1024
BATCH = 32
SEQ = 256
DEPTH = 2
DEC_BATCH = 8
DEC_SEQ = 1024
GRID_W = 64
RMS_EPS = 1e-6
N_MOD = 6
LOG2_E = 1.4426950408889634
LANES = 128

HEADS = 4
HEAD_DIM = 128
GROUP_W = HEADS * HEAD_DIM
N_GATES = 4 * HEADS
EVEN_MAIN = 9 * GROUP_W

HY_ORDER = 2
HY_BANDS = 16
HY_EMB = 1 + 2 * HY_BANDS
HY_HIDDEN = 64
HY_FREQ_CHUNK = 512
HY_STEP_ROWS = 1024

N_EXPERTS = 32
TOP_K = 4
SWIGLU_LIMIT = 7.0
SWIGLU_ALPHA = 1.702

N_PROMPT = BATCH * SEQ
N_SAMPLE = DEC_BATCH * DEC_SEQ
N_TOK = N_PROMPT + N_SAMPLE
N_COND = 16

TOK_TILE = 256
SCAN_CHUNK = 128
SUB = 16
HALF = 8
HG_STEP_HEADS = 4
MOE_TILE = 512
MOE_ROWS = N_TOK * TOP_K + N_EXPERTS * MOE_TILE
DEINT_COLS = 256
SC_STEP_BYTES = 256 * 1024
SC_MAX_INDICES = 128

VMEM_LIMIT = 56 * 1024 * 1024


def _cparams(*sem):
    return pltpu.CompilerParams(dimension_semantics=sem, vmem_limit_bytes=VMEM_LIMIT)


def _split3(x):
    hi = x.astype(BF16)
    r = x - hi.astype(F32)
    mid = r.astype(BF16)
    lo = (r - mid.astype(F32)).astype(BF16)
    return hi, mid, lo


def _dot(a, b):
    return jnp.dot(a, b, preferred_element_type=F32)


def _dot_nt(a, b):
    return lax.dot_general(a, b, (((1,), (1,)), ((), ())), preferred_element_type=F32)


def _dot_tn(a, b):
    return lax.dot_general(a, b, (((0,), (0,)), ((), ())), preferred_element_type=F32)


def _dot_w3(a_exact_bf16, x):
    hi, mid, lo = _split3(x)
    return _dot(a_exact_bf16, hi) + _dot(a_exact_bf16, mid) + _dot(a_exact_bf16, lo)


def _sigmoid(x):
    return 1.0 / (1.0 + jnp.exp(-x))


def _silu(x):
    return x * _sigmoid(x)


def _log_sigmoid(x):
    return jnp.minimum(x, 0.0) - jnp.log(1.0 + jnp.exp(-jnp.abs(x)))


def _tile_cond_row(i):
    n_prompt_tiles = N_PROMPT // TOK_TILE
    tiles_per_seq = DEC_SEQ // TOK_TILE
    return jnp.where(i < n_prompt_tiles, 0, 1 + (i - n_prompt_tiles) // tiles_per_seq)


def _mod_kernel(cond_ref, w_ref, b_ref, o_ref):
    a = _silu(cond_ref[...]).astype(BF16)
    o_ref[...] = _dot(a, w_ref[...].astype(BF16)) + b_ref[...]


def _modulation(cond, w_mod, b_mod):
    d = D_MODEL
    out = pl.pallas_call(
        _mod_kernel,
        grid=(DEPTH, N_MOD),
        in_specs=[
            pl.BlockSpec((N_COND, d), lambda l, j: (0, 0)),
            pl.BlockSpec((None, d, d), lambda l, j: (l, 0, j)),
            pl.BlockSpec((None, 1, d), lambda l, j: (l, 0, j)),
        ],
        out_specs=pl.BlockSpec((None, None, N_COND, d), lambda l, j: (l, j, 0, 0)),
        out_shape=jax.ShapeDtypeStruct((DEPTH, N_MOD, N_COND, d), F32),
        compiler_params=_cparams("parallel", "parallel"),
        name="modulation",
    )(cond, w_mod, b_mod.reshape(DEPTH, 1, N_MOD * d))
    return out.transpose(0, 2, 1, 3)


def _norm_mod(x, g_row, scale_row, shift_row):
    ms = jnp.mean(x * x, axis=-1, keepdims=True)
    y = x * lax.rsqrt(ms + RMS_EPS) * g_row
    return y * (1.0 + scale_row) + shift_row


def _proj_even_kernel(xp_ref, xs_ref, pos_ref, mod_ref, g_ref, w_ref, wgt_ref, gbt_ref,
                      xres_ref, p_ref, gate_t_ref):
    from_prompt = pl.program_id(0) < N_PROMPT // TOK_TILE
    x = jnp.where(from_prompt, xp_ref[...], xs_ref[...]) + pos_ref[...]
    xres_ref[...] = x
    h = _norm_mod(x, g_ref[...], mod_ref[1:2, :], mod_ref[0:1, :]).astype(BF16)
    p_ref[...] = _dot(h, w_ref[...])
    gate_t_ref[...] = _dot_nt(wgt_ref[...], h) + gbt_ref[...]


def _proj_even(x_prompt, x_sample, pos_tab, mod_l, norm_g, w_in, gate_b):
    d = D_MODEL
    n_tiles = N_TOK // TOK_TILE
    n_prompt_tiles = N_PROMPT // TOK_TILE
    tiles_per_seq = DEC_SEQ // TOK_TILE
    w_main = w_in[:, :EVEN_MAIN].astype(BF16)
    w_gate = w_in[:, EVEN_MAIN:].astype(BF16)

    def pos_map(i):
        return (jnp.where(i < n_prompt_tiles, 0, 1 + (i - n_prompt_tiles) % tiles_per_seq), 0)

    return pl.pallas_call(
        _proj_even_kernel,
        grid=(n_tiles,),
        in_specs=[
            pl.BlockSpec((TOK_TILE, d), lambda i: (jnp.minimum(i, n_prompt_tiles - 1), 0)),
            pl.BlockSpec((TOK_TILE, d), lambda i: (jnp.maximum(i - n_prompt_tiles, 0), 0)),
            pl.BlockSpec((TOK_TILE, d), pos_map),
            pl.BlockSpec((None, N_MOD, d), lambda i: (_tile_cond_row(i), 0, 0)),
            pl.BlockSpec((1, d), lambda i: (0, 0)),
            pl.BlockSpec((d, EVEN_MAIN), lambda i: (0, 0)),
            pl.BlockSpec((N_GATES, d), lambda i: (0, 0)),
            pl.BlockSpec((N_GATES, 1), lambda i: (0, 0)),
        ],
        out_specs=[
            pl.BlockSpec((TOK_TILE, d), lambda i: (i, 0)),
            pl.BlockSpec((TOK_TILE, EVEN_MAIN), lambda i: (i, 0)),
            pl.BlockSpec((N_GATES, TOK_TILE), lambda i: (0, i)),
        ],
        out_shape=[
            jax.ShapeDtypeStruct((N_TOK, d), F32),
            jax.ShapeDtypeStruct((N_TOK, EVEN_MAIN), F32),
            jax.ShapeDtypeStruct((N_GATES, N_TOK), F32),
        ],
        compiler_params=_cparams("parallel"),
        name="proj_even",
    )(x_prompt, x_sample, pos_tab, mod_l, norm_g.reshape(1, d), w_main, w_gate.T, gate_b.reshape(N_GATES, 1))


def _proj_odd_kernel(x_ref, mod_ref, g_ref, w_ref, p_ref):
    h = _norm_mod(x_ref[...], g_ref[...], mod_ref[1:2, :], mod_ref[0:1, :]).astype(BF16)
    p_ref[...] = _dot(h, w_ref[...])


def _proj_odd(x, mod_l, norm_g, w_in):
    d = D_MODEL
    width = w_in.shape[1]
    return pl.pallas_call(
        _proj_odd_kernel,
        grid=(N_TOK // TOK_TILE,),
        in_specs=[
            pl.BlockSpec((TOK_TILE, d), lambda i: (i, 0)),
            pl.BlockSpec((None, N_MOD, d), lambda i: (_tile_cond_row(i), 0, 0)),
            pl.BlockSpec((1, d), lambda i: (0, 0)),
            pl.BlockSpec((d, width), lambda i: (0, 0)),
        ],
        out_specs=pl.BlockSpec((TOK_TILE, width), lambda i: (i, 0)),
        out_shape=jax.ShapeDtypeStruct((N_TOK, width), F32),
        compiler_params=_cparams("parallel"),
        name="proj_odd",
    )(x, mod_l, norm_g.reshape(1, d), w_in.astype(BF16))


def _hgrn_chunks(chains):
    c = SCAN_CHUNK
    n = range(len(chains))
    rev, q, k, v, lf, st = zip(*chains)
    row = lax.broadcasted_iota(jnp.int32, (c, c), 0)
    col = lax.broadcasted_iota(jnp.int32, (c, c), 1)
    tris = {False: col <= row, True: col >= row}
    tris_b = {r: jnp.where(t, 1.0, 0.0).astype(BF16) for r, t in tris.items()}
    b = [_dot_w3(tris_b[rev[j]], lf[j]) for j in n]
    b2 = [b[j] * LOG2_E for j in n]
    lane_half = lax.broadcasted_iota(jnp.int32, (HALF, c), 1)
    chunk_row = lax.broadcasted_iota(jnp.int32, (c, 1), 0)
    rows = [[] for _ in n]
    for i in range(c // SUB):
        lo, hi = i * SUB, (i + 1) * SUB
        a_row = []
        for j in n:
            if rev[j]:
                has_off, edge, outside = hi < c, hi, chunk_row >= hi
            else:
                has_off, edge, outside = lo > 0, lo - 1, chunk_row < lo
            if has_off:
                beta = b[j][edge:edge + 1]
                qs = q[j][lo:hi] * jnp.exp(b[j][lo:hi] - beta)
                ks = k[j] * jnp.exp(jnp.where(outside, beta - b[j], -jnp.inf))
                a_row.append(_dot_nt(qs.astype(BF16), ks.astype(BF16)))
            else:
                a_row.append(jnp.zeros((SUB, c), F32))
        for half in range(SUB // HALF):
            h0 = lo + half * HALF
            piece = [a_row[j][half * HALF:(half + 1) * HALF] for j in n]
            for s in range(HALF):
                for j in n:
                    bh2 = b2[j][h0:h0 + HALF]
                    a_col = jnp.sum(jnp.exp2(bh2 - bh2[s:s + 1]) * q[j][h0:h0 + HALF] * k[j][h0 + s:h0 + s + 1],
                                    axis=-1, keepdims=True)
                    piece[j] = jnp.where(lane_half == h0 + s, a_col, piece[j])
            for j in n:
                rows[j].append(piece[j])
    second_half = (chunk_row % SUB) >= HALF
    same_block = (row // SUB) == (col // SUB)
    out = []
    for j in n:
        meet = HALF if rev[j] else HALF - 1
        beta = jnp.concatenate([jnp.broadcast_to(b[j][lo + meet:lo + meet + 1], (SUB, b[j].shape[1]))
                                for lo in range(0, c, SUB)], axis=0)
        t_side = jnp.logical_not(second_half) if rev[j] else second_half
        qs = q[j] * jnp.exp(jnp.where(t_side, b[j] - beta, -jnp.inf))
        ks = k[j] * jnp.exp(jnp.where(t_side, -jnp.inf, beta - b[j]))
        cross = jnp.where(same_block, _dot_nt(qs.astype(BF16), ks.astype(BF16)), 0.0)
        attn = jnp.where(tris[rev[j]], jnp.concatenate(rows[j], axis=0) + cross, 0.0)
        o = _dot(attn.astype(BF16), v[j].astype(BF16)) \
            + _dot_nt((q[j] * jnp.exp(b[j])).astype(BF16), st[j].astype(BF16))
        b_exit = b[j][0:1] if rev[j] else b[j][c - 1:c]
        k_out = k[j] * jnp.exp(b_exit - b[j])
        st_new = jnp.exp(b_exit) * st[j] + _dot_tn(v[j].astype(BF16), k_out.astype(BF16))
        out.append((o, st_new))
    return out


def _hgrn_kernel(seq_len, layer, q_ref, i_ref, g_ref, ff_ref, fb_ref, lb_ref, s0_ref,
                 o_ref, s_out_ref, of_ref, ob_ref, st_ref):
    c = SCAN_CHUNK
    hd = HEAD_DIM
    n_chunks = seq_len // c
    lbp = lb_ref[...]
    e = jnp.exp(lbp - jnp.max(lbp, axis=0, keepdims=True))
    lb = jnp.sum(e[0:layer + 1], axis=0, keepdims=True) / jnp.sum(e, axis=0, keepdims=True)

    for d in range(2):
        for hh in range(HG_STEP_HEADS):
            st_ref[d, hh] = s0_ref[d, hh].T

    def body(n, carry):
        where, chains = [], []
        for d in range(2):
            sl = pl.ds(pl.multiple_of((n_chunks - 1 - n if d else n) * c, c), c)
            for hh in range(HG_STEP_HEADS):
                cols = slice(hh * hd, (hh + 1) * hd)
                f = lb[:, cols] + (1.0 - lb[:, cols]) * _sigmoid((fb_ref if d else ff_ref)[sl, cols])
                where.append((d, hh, sl, cols))
                chains.append((bool(d), q_ref[sl, cols], 1.0 - f, i_ref[sl, cols], jnp.log(f), st_ref[d, hh]))
        for (d, hh, sl, cols), (o, st_new) in zip(where, _hgrn_chunks(chains)):
            st_ref[d, hh] = st_new
            (ob_ref if d else of_ref)[sl, cols] = o
        return carry

    lax.fori_loop(0, n_chunks, body, 0)
    for hh in range(HG_STEP_HEADS):
        cols = slice(hh * hd, (hh + 1) * hd)
        o = of_ref[:, cols] + ob_ref[:, cols]
        o = o * lax.rsqrt(jnp.mean(o * o, axis=-1, keepdims=True) + RMS_EPS)
        o_ref[:, cols] = o * _silu(g_ref[:, cols])
        for d in range(2):
            s_out_ref[d, hh] = st_ref[d, hh].T


def _hgrn(p, hg_lb, layer, s0, n_seq, seq_len, tok_offset):
    hd = HEAD_DIM
    sh = HG_STEP_HEADS
    steps_per_seq = HEADS // sh
    row0 = tok_offset // seq_len

    def col(part):
        return pl.BlockSpec((seq_len, sh * hd), lambda b, h: (row0 + b, part * steps_per_seq + h))

    state_spec = pl.BlockSpec((None, 2, sh, hd, hd), lambda b, h: (b, 0, h, 0, 0))
    return pl.pallas_call(
        functools.partial(_hgrn_kernel, seq_len, layer),
        grid=(n_seq, steps_per_seq),
        in_specs=[col(0), col(1), col(2), col(3), col(4),
                  pl.BlockSpec((DEPTH + 1, sh * hd), lambda b, h: (0, h)),
                  state_spec],
        out_specs=[pl.BlockSpec((seq_len, sh * hd), lambda b, h: (b, h)), state_spec],
        out_shape=[jax.ShapeDtypeStruct((n_seq * seq_len, GROUP_W), F32),
                   jax.ShapeDtypeStruct((n_seq, 2, HEADS, hd, hd), F32)],
        scratch_shapes=[pltpu.VMEM((seq_len, sh * hd), F32), pltpu.VMEM((seq_len, sh * hd), F32),
                        pltpu.VMEM((2, sh, hd, hd), F32)],
        compiler_params=_cparams("parallel", "parallel"),
        name=f"hgrn_l{seq_len}",
    )(p, p, p, p, p, hg_lb, s0)


def _short_conv3(x, w):
    n = x.shape[0]
    r = lax.broadcasted_iota(jnp.int32, (n, 1), 0)
    prev = jnp.where(r == 0, 0.0, pltpu.roll(x, 1, 0))
    nxt = jnp.where(r == n - 1, 0.0, pltpu.roll(x, n - 1, 0))
    return prev * w[0:1] + x * w[1:2] + nxt * w[2:3]


def _conv3_silu_tiles(dst_ref, src_ref, w_ref, scale):
    n, width = src_ref.shape
    c = SCAN_CHUNK
    r = lax.broadcasted_iota(jnp.int32, (c, 1), 0)
    for j in range(n // c):
        r0 = j * c
        for col in range(0, width, LANES):
            cols = slice(col, col + LANES)
            cur = src_ref[r0:r0 + c, cols]
            prev = jnp.where(r == 0, 0.0, pltpu.roll(cur, 1, 0)) if j == 0 else src_ref[r0 - 1:r0 + c - 1, cols]
            nxt = (jnp.where(r == c - 1, 0.0, pltpu.roll(cur, c - 1, 0)) if r0 + c == n
                   else src_ref[r0 + 1:r0 + c + 1, cols])
            w = w_ref[:, cols]
            y = _silu(prev * w[0:1] + cur * w[1:2] + nxt * w[2:3])
            dst_ref[r0:r0 + c, cols] = y if scale == 1.0 else y * scale


def _split2(x):
    hi = x.astype(BF16)
    return hi, (x - hi.astype(F32)).astype(BF16)


def _mlstm_chunks(chains):
    c = SCAN_CHUNK
    row = lax.broadcasted_iota(jnp.int32, (c, c), 0)
    col = lax.broadcasted_iota(jnp.int32, (c, c), 1)
    eye_b = jnp.where(row == col, 1.0, 0.0).astype(BF16)
    tris = {False: row <= col, True: row >= col}
    tris_b = {r: jnp.where(t, 1.0, 0.0).astype(BF16) for r, t in tris.items()}
    n = range(len(chains))
    rev, q, k, vt, ig, fg, ct, nv, m_prev = zip(*chains)

    def each(fn):
        return [fn(i) for i in n]

    def dot3(parts, rhs, nt=False):
        d = _dot_nt if nt else _dot
        return d(parts[0], rhs) + d(parts[1], rhs) + d(parts[2], rhs)

    lf = each(lambda i: _split3(jnp.broadcast_to(_log_sigmoid(fg[i]), (8, c))))
    b = each(lambda i: dot3(lf[i], tris_b[rev[i]])[0:1])
    us = each(lambda i: _split3(jnp.broadcast_to(ig[i] - b[i], (c, c))))
    u = each(lambda i: _dot_nt(eye_b, us[i][0]) + _dot_nt(eye_b, us[i][1]) + _dot_nt(eye_b, us[i][2]))
    dmat = each(lambda i: jnp.where(tris[rev[i]], b[i] + u[i], -jnp.inf))
    m_t = each(lambda i: jnp.maximum(b[i] + m_prev[i], jnp.max(dmat[i], axis=0, keepdims=True)))
    qb = each(lambda i: q[i].astype(BF16))
    kb = each(lambda i: k[i].astype(BF16))
    kq = each(lambda i: _dot_nt(kb[i], qb[i]))
    p = each(lambda i: jnp.exp(dmat[i] - m_t[i]) * kq[i])
    inter = each(lambda i: jnp.exp(b[i] + m_prev[i] - m_t[i]))
    ns = each(lambda i: _split2(jnp.broadcast_to(nv[i], (8, nv[i].shape[1]))))
    qs = each(lambda i: _split2(q[i]))
    qn = each(lambda i: (_dot_nt(ns[i][0], qs[i][0]) + _dot_nt(ns[i][1], qs[i][0])
                         + _dot_nt(ns[i][0], qs[i][1]))[0:1])
    den = each(lambda i: inter[i] * qn[i] + jnp.sum(p[i], axis=0, keepdims=True))
    scale = each(lambda i: 1.0 / jnp.maximum(jnp.abs(den[i]), jnp.exp(-m_t[i])))
    cq = each(lambda i: _dot_nt(ct[i].astype(BF16), qb[i]))
    vp = each(lambda i: _dot(vt[i].astype(BF16), p[i].astype(BF16)))
    ht = each(lambda i: (inter[i] * cq[i] + vp[i]) * scale[i])
    last = each(lambda i: 0 if rev[i] else c - 1)
    m_new = each(lambda i: m_t[i][:, last[i]:last[i] + 1])
    b_exit = each(lambda i: b[i][:, last[i]:last[i] + 1])
    w = each(lambda i: jnp.exp(b_exit[i] - b[i] + ig[i] - m_new[i]))
    dec = each(lambda i: jnp.exp(b_exit[i] + m_prev[i] - m_new[i]))
    vk = each(lambda i: _dot((vt[i] * w[i]).astype(BF16), kb[i]))
    ct_new = each(lambda i: dec[i] * ct[i] + vk[i])
    ws = each(lambda i: _split2(jnp.broadcast_to(w[i], (8, c))))
    ks = each(lambda i: _split2(k[i]))
    wk = each(lambda i: (_dot(ws[i][0], ks[i][0]) + _dot(ws[i][1], ks[i][0]) + _dot(ws[i][0], ks[i][1]))[0:1])
    nv_new = each(lambda i: dec[i] * nv[i] + wk[i])
    return [(ht[i], ct_new[i], nv_new[i], m_new[i]) for i in n]


def _mlstm_kernel(seq_len, q_ref, k_ref, v_ref, og_ref, gate_t_ref, cwq_ref, cwk_ref,
                  c0_ref, n0_ref, m0_ref,
                  o_ref, c_out_ref, n_out_ref, m_out_ref,
                  q2_ref, k2_ref, vt_ref, hf_ref, hb_ref, ct_ref, n_ref, m_ref):
    c = SCAN_CHUNK
    hd = HEAD_DIM
    n_chunks = seq_len // c
    _conv3_silu_tiles(q2_ref, q_ref, cwq_ref, 1.0)
    _conv3_silu_tiles(k2_ref, k_ref, cwk_ref, HEAD_DIM ** -0.5)
    for hh in range(HEADS):
        cols = slice(hh * hd, (hh + 1) * hd)
        for j in range(n_chunks):
            vt_ref[hh, :, j * c:(j + 1) * c] = v_ref[j * c:(j + 1) * c, cols].T
        for d in range(2):
            ct_ref[d, hh] = c0_ref[d, hh].T
    n_ref[...] = n0_ref[...]
    m_ref[...] = m0_ref[...]

    def body(n, carry):
        where, chains = [], []
        for d in range(2):
            sl = pl.ds(pl.multiple_of((n_chunks - 1 - n if d else n) * c, c), c)
            for hh in range(HEADS):
                cols = slice(hh * hd, (hh + 1) * hd)
                gr = gate_t_ref[hh, :, sl]
                where.append((d, hh, sl))
                chains.append((bool(d), q2_ref[sl, cols], k2_ref[sl, cols], vt_ref[hh, :, sl],
                               gr[2 * d:2 * d + 1, :], gr[2 * d + 1:2 * d + 2, :],
                               ct_ref[d, hh], n_ref[d, hh], m_ref[d, hh]))
        for (d, hh, sl), (ht, ct, nv, m_new) in zip(where, _mlstm_chunks(chains)):
            ct_ref[d, hh] = ct
            n_ref[d, hh] = nv
            m_ref[d, hh] = m_new
            (hb_ref if d else hf_ref)[hh, :, sl] = ht
        return carry

    lax.fori_loop(0, n_chunks, body, 0)
    for hh in range(HEADS):
        cols = slice(hh * hd, (hh + 1) * hd)
        for j in range(n_chunks):
            rows = slice(j * c, (j + 1) * c)
            h = (hf_ref[hh, :, rows] + hb_ref[hh, :, rows]).T
            h = h * lax.rsqrt(jnp.mean(h * h, axis=-1, keepdims=True) + RMS_EPS)
            o_ref[rows, cols] = h * _sigmoid(og_ref[rows, cols])
        for d in range(2):
            c_out_ref[d, hh] = ct_ref[d, hh].T
    n_out_ref[...] = n_ref[...]
    m_out_ref[...] = m_ref[...]


def _mlstm(p, gate_t_h, conv_w, c0, n0, m0, n_seq, seq_len, tok_offset):
    hd = HEAD_DIM
    gw = GROUP_W
    row0 = tok_offset // seq_len

    def col(part):
        return pl.BlockSpec((seq_len, gw), lambda b: (row0 + b, part))

    c_spec = pl.BlockSpec((None, 2, HEADS, hd, hd), lambda b: (b, 0, 0, 0, 0))
    n_spec = pl.BlockSpec((None, 2, HEADS, 1, hd), lambda b: (b, 0, 0, 0, 0))
    m_spec = pl.BlockSpec((None, 2, HEADS, 1, 1), lambda b: (b, 0, 0, 0, 0))
    return pl.pallas_call(
        functools.partial(_mlstm_kernel, seq_len),
        grid=(n_seq,),
        in_specs=[col(5), col(6), col(7), col(8),
                  pl.BlockSpec((HEADS, 4, seq_len), lambda b: (0, 0, row0 + b)),
                  pl.BlockSpec((3, gw), lambda b: (0, 0)),
                  pl.BlockSpec((3, gw), lambda b: (0, 1)),
                  c_spec, n_spec, m_spec],
        out_specs=[pl.BlockSpec((seq_len, gw), lambda b: (b, 0)), c_spec, n_spec, m_spec],
        out_shape=[jax.ShapeDtypeStruct((n_seq * seq_len, gw), F32),
                   jax.ShapeDtypeStruct((n_seq, 2, HEADS, hd, hd), F32),
                   jax.ShapeDtypeStruct((n_seq, 2, HEADS, 1, hd), F32),
                   jax.ShapeDtypeStruct((n_seq, 2, HEADS, 1, 1), F32)],
        scratch_shapes=[pltpu.VMEM((seq_len, gw), F32), pltpu.VMEM((seq_len, gw), F32),
                        pltpu.VMEM((HEADS, hd, seq_len), F32),
                        pltpu.VMEM((HEADS, hd, seq_len), F32), pltpu.VMEM((HEADS, hd, seq_len), F32),
                        pltpu.VMEM((2, HEADS, hd, hd), F32), pltpu.VMEM((2, HEADS, 1, hd), F32),
                        pltpu.VMEM((2, HEADS, 1, 1), F32)],
        compiler_params=_cparams("parallel"),
        name=f"mlstm_l{seq_len}",
    )(p, p, p, p, gate_t_h, conv_w, conv_w, c0, n0, m0)


def _dft_tables(seq_len):
    n = 2 * seq_len
    k = jnp.arange(seq_len, dtype=jnp.int32)[:, None]
    t = jnp.arange(seq_len, dtype=jnp.int32)[None, :]
    ang = ((k * t) % n).astype(F32) * (2.0 * np.pi / n)
    fc = jnp.cos(ang)
    fs = jnp.sin(ang)
    nyq = jnp.where(t % 2 == 0, 1.0, -1.0).astype(F32)
    fs = jnp.where(k == 0, nyq, fs)
    return jnp.concatenate([fc, fs], axis=0)


def _filter_kernel(seq_len, z_ref, w1_ref, b1_ref, w2_ref, b2_ref, w3f_ref, w3b_ref, b3f_ref, b3b_ref,
                   f0_ref, f1_ref, rf_ref, rb_ref, fhi_ref, flo_ref, kf_ref, a_ref):
    hp = lax.Precision.HIGHEST
    n = 2 * seq_len
    z = z_ref[...]

    @pl.when(jnp.logical_and(pl.program_id(0) == 0, pl.program_id(1) == 0))
    def _():
        a1 = jnp.sin(f0_ref[...] * (jnp.dot(z, w1_ref[...], precision=hp, preferred_element_type=F32)
                                    + b1_ref[...]))
        a_ref[...] = jnp.sin(f1_ref[...] * (jnp.dot(a1, w2_ref[...], precision=hp, preferred_element_type=F32)
                                            + b2_ref[...]))

    a = a_ref[...]
    t_norm = z[:, 0:1]
    hf = (jnp.dot(a, w3f_ref[...], precision=hp, preferred_element_type=F32) + b3f_ref[...]) \
        * jnp.exp(-t_norm * jnp.exp(rf_ref[...]))
    hb = (jnp.dot(a, w3b_ref[...], precision=hp, preferred_element_type=F32) + b3b_ref[...]) \
        * jnp.exp(-t_norm * jnp.exp(rb_ref[...]))
    inv = lax.rsqrt(jnp.sum(hf * hf, axis=0, keepdims=True) + jnp.sum(hb * hb, axis=0, keepdims=True))
    hf = hf * inv
    r = lax.broadcasted_iota(jnp.int32, (seq_len, 1), 0)
    hb = jnp.where(r == 0, 0.0, hb * inv)
    sh, sl = _split2(hf + hb)
    dh, dl = _split2(hf - hb)
    fhi = fhi_ref[...]
    flo = flo_ref[...]
    kc = _dot(fhi[:seq_len], sh) + _dot(fhi[:seq_len], sl) + _dot(flo[:seq_len], sh)
    ks = _dot(fhi[seq_len:], dh) + _dot(fhi[seq_len:], dl) + _dot(flo[seq_len:], dh)
    sign = jnp.where(r % 2 == 0, 1.0, -1.0)
    k_nyq = jnp.sum(sign * (hf + hb), axis=0, keepdims=True)
    ks = jnp.where(r == 0, k_nyq, ks)
    scale = jnp.where(r == 0, 1.0 / n, 2.0 / n)
    kf_ref[0:seq_len, :] = kc * scale
    kf_ref[seq_len:n, :] = ks * scale


def _hyena_filters(seq_len, w1, b1, w2, b2, w3, b3, freq, log_rate, f_tab):
    d = D_MODEL
    cb = 256
    t = jnp.arange(seq_len, dtype=F32)
    t_norm = t / (seq_len - 1)
    bands = jnp.linspace(1e-4, HY_BANDS - 1, HY_BANDS, dtype=F32)
    ang = (2.0 * np.pi / seq_len) * t[:, None] * bands[None, :]
    z = jnp.concatenate([t_norm[:, None], jnp.cos(ang), jnp.sin(ang)], axis=-1)
    kpad = 128 - HY_EMB
    z = jnp.pad(z, ((0, 0), (0, kpad)))
    w1p = jnp.pad(w1, ((0, kpad), (0, 0)))
    f_hi = f_tab.astype(BF16)
    f_lo = (f_tab - f_hi.astype(F32)).astype(BF16)
    n_cb = d // cb
    hh = HY_HIDDEN
    row = lambda a: a.reshape(1, -1)
    const = lambda shape: pl.BlockSpec(shape, lambda o, j: (0,) * len(shape))
    fwd = lambda rows: pl.BlockSpec((rows, cb), lambda o, j: (0, o * n_cb + j))
    bwd = lambda rows: pl.BlockSpec((rows, cb), lambda o, j: (0, (HY_ORDER + o) * n_cb + j))
    return pl.pallas_call(
        functools.partial(_filter_kernel, seq_len),
        grid=(HY_ORDER, n_cb),
        in_specs=[const((seq_len, 128)), const((128, hh)), const((1, hh)), const((hh, hh)), const((1, hh)),
                  fwd(hh), bwd(hh), fwd(1), bwd(1),
                  const((1, hh)), const((1, hh)), fwd(1), bwd(1),
                  const((2 * seq_len, seq_len)), const((2 * seq_len, seq_len))],
        out_specs=pl.BlockSpec((None, 2 * seq_len, cb), lambda o, j: (o, 0, j)),
        out_shape=jax.ShapeDtypeStruct((HY_ORDER, 2 * seq_len, d), F32),
        scratch_shapes=[pltpu.VMEM((seq_len, hh), F32)],
        compiler_params=_cparams("arbitrary", "arbitrary"),
        name=f"hyena_filter_l{seq_len}",
    )(z, w1p, row(b1), w2, row(b2), w3, w3, row(b3), row(b3),
      row(freq[0]), row(freq[1]), row(log_rate), row(log_rate), f_hi, f_lo)


def _hyena_kernel(seq_len, seqs, v_ref, x1_ref, x2_ref, cwv_ref, cw1_ref, cw2_ref, bias_ref, kf_ref,
                  f_ref, ft_ref, o_ref, z_ref, zb_ref, y_ref):
    kc = min(HY_FREQ_CHUNK, seq_len)
    n_k = seq_len // kc
    r = lax.broadcasted_iota(jnp.int32, (kc, 1), 0)
    gate_refs = ((x1_ref, cw1_ref), (x2_ref, cw2_ref))
    for s in range(seqs):
        rows = slice(s * seq_len, (s + 1) * seq_len)
        z_ref[s] = _short_conv3(v_ref[rows, :], cwv_ref[...])
    for o in range(HY_ORDER):
        for s in range(seqs):
            zb_ref[s] = z_ref[s].astype(BF16)
            y_ref[s] = jnp.zeros(y_ref.shape[1:], F32)

        def freq_chunk(j, carry):
            r0 = pl.multiple_of(j * kc, kc)
            k_cos = kf_ref[o, pl.ds(r0, kc), :]
            k_sin = kf_ref[o, pl.ds(seq_len + r0, kc), :]
            real_row = jnp.logical_and(r == 0, j == 0)
            for s in range(seqs):
                a = _dot(f_ref[pl.ds(r0, kc), :], zb_ref[s])
                bm = _dot(f_ref[pl.ds(seq_len + r0, kc), :], zb_ref[s])
                yc = a * k_cos - jnp.where(real_row, 0.0, bm * k_sin)
                ys = jnp.where(real_row, bm * k_sin, a * k_sin + bm * k_cos)
                y_ref[s] += _dot(ft_ref[j], yc.astype(BF16)) + _dot(ft_ref[n_k + j], ys.astype(BF16))
            return carry

        lax.fori_loop(0, n_k, freq_chunk, 0)
        x_ref, cw_ref = gate_refs[o]
        for s in range(seqs):
            rows = slice(s * seq_len, (s + 1) * seq_len)
            gate = _short_conv3(x_ref[rows, :], cw_ref[...])
            z_ref[s] = gate * (y_ref[s] + z_ref[s] * bias_ref[o:o + 1, :])
    for s in range(seqs):
        o_ref[s * seq_len:(s + 1) * seq_len, :] = z_ref[s]


def _hyena(u, conv_w, bias, kf, f_tab, n_seq, seq_len, tok_offset):
    d = D_MODEL
    cb = 256
    n_cb = d // cb
    seqs = max(1, HY_STEP_ROWS // seq_len)
    rows = seqs * seq_len
    row0 = tok_offset // rows
    kc = min(HY_FREQ_CHUNK, seq_len)
    n_k = seq_len // kc
    f_bf = f_tab.astype(BF16)
    ft = f_bf.T.reshape(seq_len, 2 * n_k, kc).transpose(1, 0, 2)

    def part(k):
        return pl.BlockSpec((rows, cb), lambda j, b: (row0 + b, k * n_cb + j))

    def cw(k):
        return pl.BlockSpec((3, cb), lambda j, b: (0, k * n_cb + j))

    return pl.pallas_call(
        functools.partial(_hyena_kernel, seq_len, seqs),
        grid=(n_cb, n_seq // seqs),
        in_specs=[part(0), part(1), part(2), cw(0), cw(1), cw(2),
                  pl.BlockSpec((HY_ORDER, cb), lambda j, b: (0, j)),
                  pl.BlockSpec((HY_ORDER, 2 * seq_len, cb), lambda j, b: (0, 0, j)),
                  pl.BlockSpec((2 * seq_len, seq_len), lambda j, b: (0, 0)),
                  pl.BlockSpec((2 * n_k, seq_len, kc), lambda j, b: (0, 0, 0))],
        out_specs=pl.BlockSpec((rows, cb), lambda j, b: (b, j)),
        out_shape=jax.ShapeDtypeStruct((n_seq * seq_len, d), F32),
        scratch_shapes=[pltpu.VMEM((seqs, seq_len, cb), F32), pltpu.VMEM((seqs, seq_len, cb), BF16),
                        pltpu.VMEM((seqs, seq_len, cb), F32)],
        compiler_params=_cparams("parallel", "parallel"),
        name=f"hyena_l{seq_len}",
    )(u, u, u, conv_w, conv_w, conv_w, bias, kf, f_bf, ft)


def _mix_out_kernel(n_parts, *refs):
    o_refs = refs[:2 * n_parts]
    w_refs = refs[2 * n_parts:3 * n_parts]
    x_ref, mod_ref, g_ref, wr_both_ref, br_ref = refs[3 * n_parts:3 * n_parts + 5]
    x1_ref, h_ref, eid_ref, wts_ref, rank_ref, cnt_ref, run_ref = refs[3 * n_parts + 5:]
    i = pl.program_id(0)
    t = TOK_TILE
    ne = N_EXPERTS

    from_prompt = i < N_PROMPT // TOK_TILE
    y = None
    for j in range(n_parts):
        o = jnp.where(from_prompt, o_refs[2 * j][...], o_refs[2 * j + 1][...])
        yj = _dot(o.astype(BF16), w_refs[j][...])
        y = yj if y is None else y + yj
    x1 = x_ref[...] + mod_ref[2:3, :] * y
    x1_ref[...] = x1
    h = _norm_mod(x1, g_ref[...], mod_ref[4:5, :], mod_ref[3:4, :])
    h_ref[...] = _pack_bf16_halves(h)
    h_hi = h.astype(BF16)
    h_lo = (h - h_hi.astype(F32)).astype(BF16)
    hi_terms = _dot(h_hi, wr_both_ref[...])
    logits = (hi_terms[:, :LANES] + hi_terms[:, LANES:] + _dot(h_lo, wr_both_ref[:, :LANES])).T[:ne] \
        + br_ref[...]

    @pl.when(i == 0)
    def _():
        run_ref[...] = jnp.zeros_like(run_ref)

    e_iota = lax.broadcasted_iota(jnp.int32, (ne, t), 0)
    vals, eids, onehots = [], [], []
    for _k in range(TOP_K):
        m = jnp.max(logits, axis=0, keepdims=True)
        eid = jnp.min(jnp.where(logits == m, e_iota, ne), axis=0, keepdims=True)
        sel = e_iota == eid
        logits = jnp.where(sel, -jnp.inf, logits)
        onehots.append(jnp.where(sel, 1.0, 0.0))
        vals.append(m)
        eids.append(eid)
    r2 = lax.broadcasted_iota(jnp.int32, (t, t), 0)
    c2 = lax.broadcasted_iota(jnp.int32, (t, t), 1)
    before = jnp.where(r2 < c2, 1.0, 0.0).astype(BF16)
    earlier = _dot(jnp.concatenate(onehots, axis=0).astype(BF16), before)
    running = run_ref[...]
    ranks = []
    for k, onehot in enumerate(onehots):
        ranks.append(jnp.sum(onehot * (running + earlier[k * ne:(k + 1) * ne]), axis=0, keepdims=True))
        running = running + jnp.sum(onehot, axis=1, keepdims=True)
    run_ref[...] = running
    cnt_ref[...] = running
    v = jnp.concatenate(vals, axis=0)
    ex = jnp.exp(v - v[0:1])
    wts_ref[...] = ex / jnp.sum(ex, axis=0, keepdims=True)
    eid_ref[...] = jnp.concatenate(eids, axis=0)
    rank_ref[...] = jnp.concatenate(ranks, axis=0).astype(jnp.int32)


def _mix_out(parts, x, mod_l, norm_g, w_router, b_router):
    d = D_MODEL
    t = TOK_TILE
    ne = N_EXPERTS
    n_parts = len(parts)
    n_prompt_tiles = N_PROMPT // t
    wr = jnp.pad(w_router, ((0, 0), (0, LANES - ne)))
    wr_hi = wr.astype(BF16)
    wr_both = jnp.concatenate([wr_hi, (wr - wr_hi.astype(F32)).astype(BF16)], axis=1)
    in_specs = []
    for o_p, _, _ in parts:
        in_specs.append(pl.BlockSpec((t, o_p.shape[1]), lambda i: (jnp.minimum(i, n_prompt_tiles - 1), 0)))
        in_specs.append(pl.BlockSpec((t, o_p.shape[1]), lambda i: (jnp.maximum(i - n_prompt_tiles, 0), 0)))
    in_specs += [pl.BlockSpec(w.shape, lambda i: (0, 0)) for _, _, w in parts]
    in_specs += [
        pl.BlockSpec((t, d), lambda i: (i, 0)),
        pl.BlockSpec((None, N_MOD, d), lambda i: (_tile_cond_row(i), 0, 0)),
        pl.BlockSpec((1, d), lambda i: (0, 0)),
        pl.BlockSpec((d, 2 * LANES), lambda i: (0, 0)),
        pl.BlockSpec((ne, 1), lambda i: (0, 0)),
    ]
    tok_major = pl.BlockSpec((TOP_K, t), lambda i: (0, i))
    return pl.pallas_call(
        functools.partial(_mix_out_kernel, n_parts),
        grid=(N_TOK // t,),
        in_specs=in_specs,
        out_specs=[pl.BlockSpec((t, d), lambda i: (i, 0)), pl.BlockSpec((t, d // 2), lambda i: (i, 0)),
                   tok_major, tok_major, tok_major, pl.BlockSpec((ne, 1), lambda i: (0, 0))],
        out_shape=[jax.ShapeDtypeStruct((N_TOK, d), F32), jax.ShapeDtypeStruct((N_TOK, d // 2), jnp.uint32),
                   jax.ShapeDtypeStruct((TOP_K, N_TOK), jnp.int32), jax.ShapeDtypeStruct((TOP_K, N_TOK), F32),
                   jax.ShapeDtypeStruct((TOP_K, N_TOK), jnp.int32), jax.ShapeDtypeStruct((ne, 1), F32)],
        scratch_shapes=[pltpu.VMEM((ne, 1), F32)],
        compiler_params=_cparams("arbitrary"),
        name="mix_out_router",
    )(*[o for part in parts for o in part[:2]], *[w.astype(BF16) for _, _, w in parts], x, mod_l,
      norm_g.reshape(1, d), wr_both, b_router.reshape(ne, 1))


def _sc_row_gather(src, idx):
    n = idx.shape[0]
    width = src.shape[1]
    step_rows = min(SC_MAX_INDICES, SC_STEP_BYTES // (width * 4))
    sc = plsc.get_sparse_core_info()
    n_workers = sc.num_cores * sc.num_subcores
    per_worker = n // n_workers
    if n % n_workers or per_worker % step_rows:
        raise ValueError("row count must be whole SparseCore steps on every subcore")
    mesh = plsc.VectorSubcoreMesh(core_axis_name="core", subcore_axis_name="subcore")

    @functools.partial(pl.kernel, out_type=jax.ShapeDtypeStruct((n, width), src.dtype), mesh=mesh,
                       scratch_types=[pltpu.VMEM((per_worker,), jnp.int32),
                                      pltpu.VMEM((step_rows, width), src.dtype)],
                       name="sc_row_gather")
    def gather(src_hbm, idx_hbm, dst_hbm, idx_vmem, rows_vmem):
        worker = lax.axis_index("subcore") * sc.num_cores + lax.axis_index("core")
        base = worker * per_worker
        pltpu.sync_copy(idx_hbm.at[pl.ds(base, per_worker)], idx_vmem)

        @pl.loop(0, per_worker // step_rows)
        def _(c):
            pltpu.sync_copy(src_hbm.at[idx_vmem.at[pl.ds(c * step_rows, step_rows)]], rows_vmem)
            pltpu.sync_copy(rows_vmem, dst_hbm.at[pl.ds(base + c * step_rows, step_rows)])

    return gather(src, idx)


def _sc_row_scatter(src, row_of_slot):
    n = row_of_slot.shape[0]
    n_src, width = src.shape
    step_rows = min(SC_MAX_INDICES, SC_STEP_BYTES // (width * 4))
    sc = plsc.get_sparse_core_info()
    n_workers = sc.num_cores * sc.num_subcores
    per_worker = n // n_workers
    steps = per_worker // step_rows
    if n % n_workers or per_worker % step_rows or n_src % step_rows:
        raise ValueError("row counts must be whole SparseCore steps on every subcore")
    mesh = plsc.VectorSubcoreMesh(core_axis_name="core", subcore_axis_name="subcore")

    @functools.partial(pl.kernel, out_type=jax.ShapeDtypeStruct((n, width), src.dtype), mesh=mesh,
                       scratch_types=[pltpu.VMEM((steps, step_rows), jnp.int32),
                                      pltpu.VMEM((step_rows, width), src.dtype)],
                       name="sc_row_scatter")
    def scatter(src_hbm, idx_hbm, dst_hbm, idx_vmem, rows_vmem):
        worker = lax.axis_index("subcore") * sc.num_cores + lax.axis_index("core")
        pltpu.sync_copy(idx_hbm.at[worker], idx_vmem)

        @pl.loop(0, steps)
        def _(c):
            src0 = lax.rem(worker * per_worker + c * step_rows, n_src)
            pltpu.sync_copy(src_hbm.at[pl.ds(src0, step_rows)], rows_vmem)
            pltpu.sync_copy(rows_vmem, dst_hbm.at[idx_vmem.at[c]])

    return scatter(src, row_of_slot.reshape(n_workers, steps, step_rows))


def _pack_bf16_halves(x):
    w = x.shape[1] // 2
    bits = pltpu.bitcast(x.astype(BF16).astype(F32), jnp.uint32)
    return bits[:, :w] | (bits[:, w:] >> 16)


def _unpack_bf16_halves(p):
    hi = pltpu.bitcast(p & jnp.uint32(0xFFFF0000), F32).astype(BF16)
    lo = pltpu.bitcast(p << 16, F32).astype(BF16)
    return hi, lo


def _experts_kernel(layer, te_ref, first_ref, slot_ref, next_ref, rows_ref, nv_ref,
                    x_ref, wgu_hbm, bg_ref, bu_ref, wd_hbm, bd_ref, sel_ref,
                    y_ref, wgu_buf, wd_buf, wg_ref, wu_ref, wdb_ref, sem):
    i = pl.program_id(0)
    valid = i < nv_ref[0]
    half = DEINT_COLS // 2
    k_half = x_ref.shape[1]

    def fetch(expert, slot):
        return (pltpu.make_async_copy(wgu_hbm.at[layer, expert], wgu_buf.at[slot], sem.at[slot, 0]),
                pltpu.make_async_copy(wd_hbm.at[layer, expert], wd_buf.at[slot], sem.at[slot, 1]))

    @pl.when(i == 0)
    def _():
        for cp in fetch(te_ref[0], 0):
            cp.start()

    @pl.when(jnp.logical_and(valid, first_ref[i] == 1))
    def _():
        slot = slot_ref[i]
        for cp in fetch(te_ref[i], slot):
            cp.wait()

        @pl.when(next_ref[i] >= 0)
        def _():
            for cp in fetch(next_ref[i], 1 - slot):
                cp.start(priority=1)

        for c in range(wgu_buf.shape[2] // DEINT_COLS):
            w = wgu_buf[slot, :, c * DEINT_COLS:(c + 1) * DEINT_COLS].astype(BF16)
            split = _dot(w, sel_ref[...]).astype(BF16)
            wg_ref[:, c * half:(c + 1) * half] = split[:, :half]
            wu_ref[:, c * half:(c + 1) * half] = split[:, half:]
        wdb_ref[...] = wd_buf[slot].astype(BF16)

    def ffn(rows):
        x_l, x_r = _unpack_bf16_halves(x_ref[:rows, :])
        gl = _dot(x_l, wg_ref[:k_half, :]) + _dot(x_r, wg_ref[k_half:, :]) + bg_ref[...]
        up = _dot(x_l, wu_ref[:k_half, :]) + _dot(x_r, wu_ref[k_half:, :]) + bu_ref[...]
        gl = jnp.minimum(gl, SWIGLU_LIMIT)
        up = jnp.clip(up, -SWIGLU_LIMIT, SWIGLU_LIMIT)
        act = (up + 1.0) * gl * _sigmoid(SWIGLU_ALPHA * gl)
        y_ref[:rows, :] = _pack_bf16_halves(_dot(act.astype(BF16), wdb_ref[...]) + bd_ref[...])

    tm = x_ref.shape[0]
    few = rows_ref[i] <= tm // 2

    @pl.when(jnp.logical_and(valid, jnp.logical_not(few)))
    def _():
        ffn(tm)

    @pl.when(jnp.logical_and(valid, few))
    def _():
        ffn(tm // 2)
        y_ref[tm // 2:, :] = jnp.zeros((tm - tm // 2, y_ref.shape[1]), y_ref.dtype)

    @pl.when(jnp.logical_not(valid))
    def _():
        y_ref[...] = jnp.zeros_like(y_ref)


def _experts(xs, tile_expert, tile_first, tile_rows, n_valid, layer, w_gu, b_gate, b_up, w_down, b_down):
    d = D_MODEL
    tm = MOE_TILE
    n_tiles = MOE_ROWS // tm
    ff = w_down.shape[2]
    half = DEINT_COLS // 2
    r = jnp.arange(DEINT_COLS)[:, None]
    c = jnp.arange(DEINT_COLS)[None, :]
    sel = (r == jnp.where(c < half, 2 * c, 2 * (c - half) + 1)).astype(BF16)
    group = jnp.cumsum(tile_first) - 1
    tile_slot = (group % 2).astype(jnp.int32)
    is_last_group = group == group[-1]
    following = jnp.concatenate([tile_expert[1:], tile_expert[-1:]])
    idx = jnp.arange(n_tiles, dtype=jnp.int32)
    group_end = jnp.max(jnp.where(group[None, :] == group[:, None], idx[None, :], -1), axis=1)
    tile_next = jnp.where(is_last_group, -1, following[group_end]).astype(jnp.int32)
    wspec = lambda k, n: pl.BlockSpec((None, None, k, n), lambda i, *_: (layer, _[0][i], 0, 0))
    grid_spec = pltpu.PrefetchScalarGridSpec(
        num_scalar_prefetch=6,
        grid=(n_tiles,),
        in_specs=[pl.BlockSpec((tm, d // 2), lambda i, *_: (i, 0)),
                  pl.BlockSpec(memory_space=pl.ANY), wspec(1, ff), wspec(1, ff),
                  pl.BlockSpec(memory_space=pl.ANY), wspec(1, d),
                  pl.BlockSpec((DEINT_COLS, DEINT_COLS), lambda i, *_: (0, 0))],
        out_specs=pl.BlockSpec((tm, d // 2), lambda i, *_: (i, 0)),
        scratch_shapes=[pltpu.VMEM((2, d, 2 * ff), F32), pltpu.VMEM((2, ff, d), F32),
                        pltpu.VMEM((d, ff), BF16), pltpu.VMEM((d, ff), BF16), pltpu.VMEM((ff, d), BF16),
                        pltpu.SemaphoreType.DMA((2, 2))],
    )
    return pl.pallas_call(
        functools.partial(_experts_kernel, layer),
        grid_spec=grid_spec,
        out_shape=jax.ShapeDtypeStruct((MOE_ROWS, d // 2), jnp.uint32),
        compiler_params=_cparams("arbitrary"),
        name="experts",
    )(tile_expert, tile_first, tile_slot, tile_next, tile_rows, n_valid, xs, w_gu, b_gate, b_up, w_down,
      b_down, sel)


def _combine_kernel(final, x_ref, g_ref, w_ref, mod_ref, fg_ref, *o_refs):
    y_l = y_r = None
    for k in range(TOP_K):
        g_l, g_r = _unpack_bf16_halves(g_ref[k])
        wk = w_ref[:, k:k + 1]
        y_l = wk * g_l.astype(F32) if y_l is None else y_l + wk * g_l.astype(F32)
        y_r = wk * g_r.astype(F32) if y_r is None else y_r + wk * g_r.astype(F32)
    x = x_ref[...] + mod_ref[5:6, :] * jnp.concatenate([y_l, y_r], axis=1)
    if not final:
        o_refs[0][...] = x
        return
    x = x * lax.rsqrt(jnp.mean(x * x, axis=-1, keepdims=True) + RMS_EPS) * fg_ref[...]
    from_prompt = pl.program_id(0) < N_PROMPT // TOK_TILE

    @pl.when(from_prompt)
    def _():
        o_refs[0][...] = x

    @pl.when(jnp.logical_not(from_prompt))
    def _():
        o_refs[1][...] = x


def _combine(x1, gathered, wts, mod_l, final_g, final):
    d = D_MODEL
    t = TOK_TILE
    n_prompt_tiles = N_PROMPT // t
    if final:
        out_specs = [pl.BlockSpec((t, d), lambda i: (jnp.minimum(i, n_prompt_tiles - 1), 0)),
                     pl.BlockSpec((t, d), lambda i: (jnp.maximum(i - n_prompt_tiles, 0), 0))]
        out_shape = [jax.ShapeDtypeStruct((N_PROMPT, d), F32), jax.ShapeDtypeStruct((N_SAMPLE, d), F32)]
    else:
        out_specs = pl.BlockSpec((t, d), lambda i: (i, 0))
        out_shape = jax.ShapeDtypeStruct((N_TOK, d), F32)
    return pl.pallas_call(
        functools.partial(_combine_kernel, final),
        grid=(N_TOK // t,),
        in_specs=[pl.BlockSpec((t, d), lambda i: (i, 0)),
                  pl.BlockSpec((TOP_K, t, d // 2), lambda i: (0, i, 0)),
                  pl.BlockSpec((t, TOP_K), lambda i: (i, 0)),
                  pl.BlockSpec((None, N_MOD, d), lambda i: (_tile_cond_row(i), 0, 0)),
                  pl.BlockSpec((1, d), lambda i: (0, 0))],
        out_specs=out_specs,
        out_shape=out_shape,
        compiler_params=_cparams("arbitrary"),
        name="moe_combine",
    )(x1, gathered, wts, mod_l, final_g.reshape(1, d))


def _moe(x1, h, eid, wts, rank, counts, mod_l, layer, w_gu, b_gu, w_down, b_down, final_g, final):
    d = D_MODEL
    tm = MOE_TILE
    n_tiles = MOE_ROWS // tm
    cnt = counts.reshape(N_EXPERTS).astype(jnp.int32)
    gsz = ((cnt + tm - 1) // tm) * tm
    ends = jnp.cumsum(gsz)
    offs = ends - gsz
    e_ids = jnp.arange(N_EXPERTS, dtype=jnp.int32)
    pos = jnp.sum(jnp.where(eid[..., None] == e_ids, offs, 0), axis=-1) + rank
    tile_start = jnp.arange(n_tiles, dtype=jnp.int32) * tm
    tile_expert = jnp.minimum(jnp.sum((ends[None, :] <= tile_start[:, None]).astype(jnp.int32), axis=1),
                              N_EXPERTS - 1)
    n_valid = (ends[-1:] // tm).astype(jnp.int32)
    last_valid = jnp.maximum(n_valid[0] - 1, 0)
    tile_expert = jnp.where(jnp.arange(n_tiles) < n_valid[0], tile_expert, tile_expert[last_valid])
    tile_first = jnp.concatenate([jnp.ones((1,), jnp.int32),
                                  (tile_expert[1:] != tile_expert[:-1]).astype(jnp.int32)])
    j = jnp.arange(tm, dtype=jnp.int32)[None, :]
    pad_used = j < (gsz - cnt)[:, None]
    n_unused_before = jnp.cumsum((~pad_used).reshape(-1).astype(jnp.int32)) - 1
    pad_pos = jnp.where(pad_used, (offs + cnt)[:, None] + j,
                        ends[-1] + n_unused_before.reshape(N_EXPERTS, tm))
    row_of_slot = jnp.concatenate([pos.reshape(-1), pad_pos.reshape(-1)])

    xs = _sc_row_scatter(h, row_of_slot)
    real_end = (offs + cnt)[tile_expert]
    tile_rows = jnp.clip(real_end - tile_start, 0, tm).astype(jnp.int32)
    ys = _experts(xs, tile_expert, tile_first, tile_rows, n_valid, layer, w_gu,
                  b_gu[:, :, None, 0::2], b_gu[:, :, None, 1::2], w_down, b_down[:, :, None, :])
    gathered = _sc_row_gather(ys, pos.reshape(-1))
    return _combine(x1, gathered.reshape(TOP_K, N_TOK, d // 2), wts.T, mod_l, final_g, final)


def _grid_positions(n_tok, d):
    rows = n_tok // GRID_W
    r, col = jnp.meshgrid(jnp.arange(rows, dtype=F32), jnp.arange(GRID_W, dtype=F32), indexing='ij')
    r = r.reshape(-1)
    col = col.reshape(-1)
    quarter = d // 4
    inv = 1.0 / (10000.0 ** (jnp.arange(quarter, dtype=F32) / quarter))
    ar = r[:, None] * inv[None]
    ac = col[:, None] * inv[None]
    return jnp.concatenate([jnp.sin(ar), jnp.cos(ar), jnp.sin(ac), jnp.cos(ac)], axis=-1)


def kernel(x_prompt, x_sample, state_hgrn, state_mlstm_c, state_mlstm_n, state_mlstm_m, c, c_ctx,
           norm_g, final_g, w_mod, b_mod, ev_w_in, ev_gate_b, ev_conv, hg_lb, ev_w_out,
           hy_w_in, hy_conv, hy_w1, hy_b1, hy_w2, hy_b2, hy_w3, hy_b3, hy_freq, hy_log_rate, hy_bias, hy_w_out,
           w_router, b_router, w_gu, b_gu, w_down, b_down):
    d = D_MODEL
    hd = HEAD_DIM
    cond = jnp.concatenate([c_ctx[None], c, jnp.zeros((N_COND - 1 - DEC_BATCH, d), F32)], axis=0)
    mod = _modulation(cond, w_mod, b_mod)
    pos_tab = jnp.concatenate([jnp.zeros((TOK_TILE, d), F32), _grid_positions(DEC_SEQ, d)], axis=0)

    groups = ((BATCH, SEQ, 0), (DEC_BATCH, DEC_SEQ, N_PROMPT))
    new_states = None
    for l in range(DEPTH):
        if l % 2 == 0:
            e = l // 2
            if l == 0:
                x, p, gate_t = _proj_even(x_prompt.reshape(N_PROMPT, d), x_sample.reshape(N_SAMPLE, d),
                                          pos_tab, mod[l], norm_g[l, 0], ev_w_in[e], ev_gate_b[e])
            else:
                raise NotImplementedError("only the first layer adds grid positions")
            gate_t_h = gate_t.reshape(4, HEADS, N_TOK).transpose(1, 0, 2)
            o_hg, o_ml = [], []
            for gi, (n_seq, seq_len, off) in enumerate(groups):
                if gi == 0:
                    s0 = jnp.zeros((n_seq, 2, HEADS, hd, hd), F32)
                    c0 = jnp.zeros((n_seq, 2, HEADS, hd, hd), F32)
                    n0 = jnp.zeros((n_seq, 2, HEADS, 1, hd), F32)
                    m0 = jnp.zeros((n_seq, 2, HEADS, 1, 1), F32)
                else:
                    s0 = state_hgrn[:, e]
                    c0 = state_mlstm_c[:, e]
                    n0 = state_mlstm_n[:, e].reshape(n_seq, 2, HEADS, 1, hd)
                    m0 = state_mlstm_m[:, e].reshape(n_seq, 2, HEADS, 1, 1)
                og, s_fin = _hgrn(p, hg_lb, l, s0, n_seq, seq_len, off)
                om, c_fin, n_fin, m_fin = _mlstm(p, gate_t_h, ev_conv[e], c0, n0, m0, n_seq, seq_len, off)
                o_hg.append(og)
                o_ml.append(om)
                if gi == 0:
                    new_states = (s_fin[:, None], c_fin[:, None],
                                  n_fin.reshape(n_seq, 1, 2, HEADS, hd), m_fin.reshape(n_seq, 1, 2, HEADS))
            parts = [(*o_hg, ev_w_out[e][:GROUP_W]), (*o_ml, ev_w_out[e][GROUP_W:])]
        else:
            o = l // 2
            u = _proj_odd(x, mod[l], norm_g[l, 0], hy_w_in[o])
            zs = []
            for n_seq, seq_len, off in groups:
                f_tab = _dft_tables(seq_len)
                kf = _hyena_filters(seq_len, hy_w1[o], hy_b1[o], hy_w2[o], hy_b2[o], hy_w3[o], hy_b3[o],
                                    hy_freq[o], hy_log_rate[o], f_tab)
                zs.append(_hyena(u, hy_conv[o], hy_bias[o], kf, f_tab, n_seq, seq_len, off))
            parts = [(*zs, hy_w_out[o])]
        x1, h, eid, wts, rank, counts = _mix_out(parts, x, mod[l], norm_g[l, 1], w_router[l], b_router[l])
        x = _moe(x1, h, eid, wts, rank, counts, mod[l], l, w_gu, b_gu, w_down, b_down,
                 final_g, final=(l == DEPTH - 1))

    y_prompt, y_sample = x
    return (y_prompt.reshape(BATCH, SEQ, d), y_sample.reshape(DEC_BATCH, DEC_SEQ, d)) + new_states
```

```python
import functools

import numpy as np
import jax
import jax.numpy as jnp
from jax import lax
from jax.experimental import pallas as pl
from jax.experimental.pallas import tpu as pltpu
from jax.experimental.pallas import tpu_sc as plsc

F32 = jnp.float32
BF16 = jnp.bfloat16

D_MODEL = 1024
BATCH = 32
SEQ = 256
DEPTH = 2
DEC_BATCH = 8
DEC_SEQ = 1024
GRID_W = 64
RMS_EPS = 1e-6
N_MOD = 6
LOG2_E = 1.4426950408889634
LANES = 128

HEADS = 4
HEAD_DIM = 128
GROUP_W = HEADS * HEAD_DIM
N_GATES = 4 * HEADS
EVEN_MAIN = 9 * GROUP_W

HY_ORDER = 2
HY_BANDS = 16
HY_EMB = 1 + 2 * HY_BANDS
HY_HIDDEN = 64
HY_FREQ_CHUNK = 512
HY_STEP_ROWS = 1024

N_EXPERTS = 32
TOP_K = 4
SWIGLU_LIMIT = 7.0
SWIGLU_ALPHA = 1.702

N_PROMPT = BATCH * SEQ
N_SAMPLE = DEC_BATCH * DEC_SEQ
N_TOK = N_PROMPT + N_SAMPLE
N_COND = 16

TOK_TILE = 256
ROUTE_TILE = 512
SCAN_CHUNK = 128
SUB = 16
HALF = 8
HG_STEP_HEADS = 4
MOE_TILE = 512
MOE_ROWS = N_TOK * TOP_K + N_EXPERTS * MOE_TILE
DEINT_COLS = 256
SC_STEP_BYTES = 256 * 1024
SC_MAX_INDICES = 128

VMEM_LIMIT = 56 * 1024 * 1024


def _cparams(*sem):
    return pltpu.CompilerParams(dimension_semantics=sem, vmem_limit_bytes=VMEM_LIMIT)


def _split3(x):
    hi = x.astype(BF16)
    r = x - hi.astype(F32)
    mid = r.astype(BF16)
    lo = (r - mid.astype(F32)).astype(BF16)
    return hi, mid, lo


def _dot(a, b):
    return jnp.dot(a, b, preferred_element_type=F32)


def _dot_nt(a, b):
    return lax.dot_general(a, b, (((1,), (1,)), ((), ())), preferred_element_type=F32)


def _dot_tn(a, b):
    return lax.dot_general(a, b, (((0,), (0,)), ((), ())), preferred_element_type=F32)


def _dot_w3(a_exact_bf16, x):
    hi, mid, lo = _split3(x)
    return _dot(a_exact_bf16, hi) + _dot(a_exact_bf16, mid) + _dot(a_exact_bf16, lo)


def _sigmoid(x):
    return 1.0 / (1.0 + jnp.exp(-x))


def _silu(x):
    return x * _sigmoid(x)


def _log_sigmoid(x):
    return jnp.minimum(x, 0.0) - jnp.log(1.0 + jnp.exp(-jnp.abs(x)))


def _tile_cond_row(i, tile=TOK_TILE):
    n_prompt_tiles = N_PROMPT // tile
    tiles_per_seq = DEC_SEQ // tile
    return jnp.where(i < n_prompt_tiles, 0, 1 + (i - n_prompt_tiles) // tiles_per_seq)


def _mod_kernel(cond_ref, w_ref, b_ref, o_ref):
    a = _silu(cond_ref[...]).astype(BF16)
    o_ref[...] = _dot(a, w_ref[...].astype(BF16)) + b_ref[...]


def _modulation(cond, w_mod, b_mod):
    d = D_MODEL
    out = pl.pallas_call(
        _mod_kernel,
        grid=(DEPTH, N_MOD),
        in_specs=[
            pl.BlockSpec((N_COND, d), lambda l, j: (0, 0)),
            pl.BlockSpec((None, d, d), lambda l, j: (l, 0, j)),
            pl.BlockSpec((None, 1, d), lambda l, j: (l, 0, j)),
        ],
        out_specs=pl.BlockSpec((None, None, N_COND, d), lambda l, j: (l, j, 0, 0)),
        out_shape=jax.ShapeDtypeStruct((DEPTH, N_MOD, N_COND, d), F32),
        compiler_params=_cparams("parallel", "parallel"),
        name="modulation",
    )(cond, w_mod, b_mod.reshape(DEPTH, 1, N_MOD * d))
    return out.transpose(0, 2, 1, 3)


def _norm_mod(x, g_row, scale_row, shift_row):
    ms = jnp.mean(x * x, axis=-1, keepdims=True)
    y = x * lax.rsqrt(ms + RMS_EPS) * g_row
    return y * (1.0 + scale_row) + shift_row


def _proj_even_kernel(xp_ref, xs_ref, pos_ref, mod_ref, g_ref, w_ref, wgt_ref, gbt_ref,
                      xres_ref, p_ref, gate_t_ref):
    from_prompt = pl.program_id(0) < N_PROMPT // TOK_TILE
    x = jnp.where(from_prompt, xp_ref[...], xs_ref[...]) + pos_ref[...]
    xres_ref[...] = x
    h = _norm_mod(x, g_ref[...], mod_ref[1:2, :], mod_ref[0:1, :]).astype(BF16)
    p_ref[...] = _dot(h, w_ref[...])
    gate_t_ref[...] = _dot_nt(wgt_ref[...], h) + gbt_ref[...]


def _proj_even(x_prompt, x_sample, pos_tab, mod_l, norm_g, w_in, gate_b):
    d = D_MODEL
    n_tiles = N_TOK // TOK_TILE
    n_prompt_tiles = N_PROMPT // TOK_TILE
    tiles_per_seq = DEC_SEQ // TOK_TILE
    w_main = w_in[:, :EVEN_MAIN].astype(BF16)
    w_gate = w_in[:, EVEN_MAIN:].astype(BF16)

    def pos_map(i):
        return (jnp.where(i < n_prompt_tiles, 0, 1 + (i - n_prompt_tiles) % tiles_per_seq), 0)

    return pl.pallas_call(
        _proj_even_kernel,
        grid=(n_tiles,),
        in_specs=[
            pl.BlockSpec((TOK_TILE, d), lambda i: (jnp.minimum(i, n_prompt_tiles - 1), 0)),
            pl.BlockSpec((TOK_TILE, d), lambda i: (jnp.maximum(i - n_prompt_tiles, 0), 0)),
            pl.BlockSpec((TOK_TILE, d), pos_map),
            pl.BlockSpec((None, N_MOD, d), lambda i: (_tile_cond_row(i), 0, 0)),
            pl.BlockSpec((1, d), lambda i: (0, 0)),
            pl.BlockSpec((d, EVEN_MAIN), lambda i: (0, 0)),
            pl.BlockSpec((N_GATES, d), lambda i: (0, 0)),
            pl.BlockSpec((N_GATES, 1), lambda i: (0, 0)),
        ],
        out_specs=[
            pl.BlockSpec((TOK_TILE, d), lambda i: (i, 0)),
            pl.BlockSpec((TOK_TILE, EVEN_MAIN), lambda i: (i, 0)),
            pl.BlockSpec((N_GATES, TOK_TILE), lambda i: (0, i)),
        ],
        out_shape=[
            jax.ShapeDtypeStruct((N_TOK, d), F32),
            jax.ShapeDtypeStruct((N_TOK, EVEN_MAIN), F32),
            jax.ShapeDtypeStruct((N_GATES, N_TOK), F32),
        ],
        compiler_params=_cparams("parallel"),
        name="proj_even",
    )(x_prompt, x_sample, pos_tab, mod_l, norm_g.reshape(1, d), w_main, w_gate.T, gate_b.reshape(N_GATES, 1))


def _proj_odd_kernel(x_ref, mod_ref, g_ref, w_ref, p_ref):
    h = _norm_mod(x_ref[...], g_ref[...], mod_ref[1:2, :], mod_ref[0:1, :]).astype(BF16)
    p_ref[...] = _dot(h, w_ref[...])


def _proj_odd(x, mod_l, norm_g, w_in):
    d = D_MODEL
    width = w_in.shape[1]
    return pl.pallas_call(
        _proj_odd_kernel,
        grid=(N_TOK // TOK_TILE,),
        in_specs=[
            pl.BlockSpec((TOK_TILE, d), lambda i: (i, 0)),
            pl.BlockSpec((None, N_MOD, d), lambda i: (_tile_cond_row(i), 0, 0)),
            pl.BlockSpec((1, d), lambda i: (0, 0)),
            pl.BlockSpec((d, width), lambda i: (0, 0)),
        ],
        out_specs=pl.BlockSpec((TOK_TILE, width), lambda i: (i, 0)),
        out_shape=jax.ShapeDtypeStruct((N_TOK, width), F32),
        compiler_params=_cparams("parallel"),
        name="proj_odd",
    )(x, mod_l, norm_g.reshape(1, d), w_in.astype(BF16))


def _hgrn_chunks(chains):
    c = SCAN_CHUNK
    n = range(len(chains))
    rev, q, k, v, lf, st = zip(*chains)
    row = lax.broadcasted_iota(jnp.int32, (c, c), 0)
    col = lax.broadcasted_iota(jnp.int32, (c, c), 1)
    tris = {False: col <= row, True: col >= row}
    tris_b = {r: jnp.where(t, 1.0, 0.0).astype(BF16) for r, t in tris.items()}
    b = [_dot_w3(tris_b[rev[j]], lf[j]) for j in n]
    b2 = [b[j] * LOG2_E for j in n]
    lane_half = lax.broadcasted_iota(jnp.int32, (HALF, c), 1)
    chunk_row = lax.broadcasted_iota(jnp.int32, (c, 1), 0)
    rows = [[] for _ in n]
    for i in range(c // SUB):
        lo, hi = i * SUB, (i + 1) * SUB
        a_row = []
        for j in n:
            if rev[j]:
                has_off, edge, outside = hi < c, hi, chunk_row >= hi
            else:
                has_off, edge, outside = lo > 0, lo - 1, chunk_row < lo
            if has_off:
                beta = b[j][edge:edge + 1]
                qs = q[j][lo:hi] * jnp.exp(b[j][lo:hi] - beta)
                ks = k[j] * jnp.exp(jnp.where(outside, beta - b[j], -jnp.inf))
                a_row.append(_dot_nt(qs.astype(BF16), ks.astype(BF16)))
            else:
                a_row.append(jnp.zeros((SUB, c), F32))
        for half in range(SUB // HALF):
            h0 = lo + half * HALF
            piece = [a_row[j][half * HALF:(half + 1) * HALF] for j in n]
            for s in range(HALF):
                for j in n:
                    bh2 = b2[j][h0:h0 + HALF]
                    a_col = jnp.sum(jnp.exp2(bh2 - bh2[s:s + 1]) * q[j][h0:h0 + HALF] * k[j][h0 + s:h0 + s + 1],
                                    axis=-1, keepdims=True)
                    piece[j] = jnp.where(lane_half == h0 + s, a_col, piece[j])
            for j in n:
                rows[j].append(piece[j])
    second_half = (chunk_row % SUB) >= HALF
    same_block = (row // SUB) == (col // SUB)
    out = []
    for j in n:
        meet = HALF if rev[j] else HALF - 1
        beta = jnp.concatenate([jnp.broadcast_to(b[j][lo + meet:lo + meet + 1], (SUB, b[j].shape[1]))
                                for lo in range(0, c, SUB)], axis=0)
        t_side = jnp.logical_not(second_half) if rev[j] else second_half
        qs = q[j] * jnp.exp(jnp.where(t_side, b[j] - beta, -jnp.inf))
        ks = k[j] * jnp.exp(jnp.where(t_side, -jnp.inf, beta - b[j]))
        cross = jnp.where(same_block, _dot_nt(qs.astype(BF16), ks.astype(BF16)), 0.0)
        attn = jnp.where(tris[rev[j]], jnp.concatenate(rows[j], axis=0) + cross, 0.0)
        o = _dot(attn.astype(BF16), v[j].astype(BF16)) \
            + _dot_nt((q[j] * jnp.exp(b[j])).astype(BF16), st[j].astype(BF16))
        b_exit = b[j][0:1] if rev[j] else b[j][c - 1:c]
        k_out = k[j] * jnp.exp(b_exit - b[j])
        st_new = jnp.exp(b_exit) * st[j] + _dot_tn(v[j].astype(BF16), k_out.astype(BF16))
        out.append((o, st_new))
    return out


def _hgrn_kernel(seq_len, layer, q_ref, i_ref, g_ref, ff_ref, fb_ref, lb_ref, s0_ref,
                 o_ref, s_out_ref, of_ref, ob_ref, st_ref):
    c = SCAN_CHUNK
    hd = HEAD_DIM
    n_chunks = seq_len // c
    lbp = lb_ref[...]
    e = jnp.exp(lbp - jnp.max(lbp, axis=0, keepdims=True))
    lb = jnp.sum(e[0:layer + 1], axis=0, keepdims=True) / jnp.sum(e, axis=0, keepdims=True)

    for d in range(2):
        for hh in range(HG_STEP_HEADS):
            st_ref[d, hh] = s0_ref[d, hh].T

    def body(n, carry):
        where, chains = [], []
        for d in range(2):
            sl = pl.ds(pl.multiple_of((n_chunks - 1 - n if d else n) * c, c), c)
            for hh in range(HG_STEP_HEADS):
                cols = slice(hh * hd, (hh + 1) * hd)
                f = lb[:, cols] + (1.0 - lb[:, cols]) * _sigmoid((fb_ref if d else ff_ref)[sl, cols])
                where.append((d, hh, sl, cols))
                chains.append((bool(d), q_ref[sl, cols], 1.0 - f, i_ref[sl, cols], jnp.log(f), st_ref[d, hh]))
        for (d, hh, sl, cols), (o, st_new) in zip(where, _hgrn_chunks(chains)):
            st_ref[d, hh] = st_new
            (ob_ref if d else of_ref)[sl, cols] = o
        return carry

    lax.fori_loop(0, n_chunks, body, 0)
    for hh in range(HG_STEP_HEADS):
        cols = slice(hh * hd, (hh + 1) * hd)
        o = of_ref[:, cols] + ob_ref[:, cols]
        o = o * lax.rsqrt(jnp.mean(o * o, axis=-1, keepdims=True) + RMS_EPS)
        o_ref[:, cols] = o * _silu(g_ref[:, cols])
        for d in range(2):
            s_out_ref[d, hh] = st_ref[d, hh].T


def _hgrn(p, hg_lb, layer, s0, n_seq, seq_len, tok_offset):
    hd = HEAD_DIM
    sh = HG_STEP_HEADS
    steps_per_seq = HEADS // sh
    row0 = tok_offset // seq_len

    def col(part):
        return pl.BlockSpec((seq_len, sh * hd), lambda b, h: (row0 + b, part * steps_per_seq + h))

    state_spec = pl.BlockSpec((None, 2, sh, hd, hd), lambda b, h: (b, 0, h, 0, 0))
    return pl.pallas_call(
        functools.partial(_hgrn_kernel, seq_len, layer),
        grid=(n_seq, steps_per_seq),
        in_specs=[col(0), col(1), col(2), col(3), col(4),
                  pl.BlockSpec((DEPTH + 1, sh * hd), lambda b, h: (0, h)),
                  state_spec],
        out_specs=[pl.BlockSpec((seq_len, sh * hd), lambda b, h: (b, h)), state_spec],
        out_shape=[jax.ShapeDtypeStruct((n_seq * seq_len, GROUP_W), F32),
                   jax.ShapeDtypeStruct((n_seq, 2, HEADS, hd, hd), F32)],
        scratch_shapes=[pltpu.VMEM((seq_len, sh * hd), F32), pltpu.VMEM((seq_len, sh * hd), F32),
                        pltpu.VMEM((2, sh, hd, hd), F32)],
        compiler_params=_cparams("parallel", "parallel"),
        name=f"hgrn_l{seq_len}",
    )(p, p, p, p, p, hg_lb, s0)


def _short_conv3(x, w):
    n = x.shape[0]
    r = lax.broadcasted_iota(jnp.int32, (n, 1), 0)
    prev = jnp.where(r == 0, 0.0, pltpu.roll(x, 1, 0))
    nxt = jnp.where(r == n - 1, 0.0, pltpu.roll(x, n - 1, 0))
    return prev * w[0:1] + x * w[1:2] + nxt * w[2:3]


def _conv3_silu_tiles(dst_ref, src_ref, w_ref, scale):
    n, width = src_ref.shape
    c = SCAN_CHUNK
    r = lax.broadcasted_iota(jnp.int32, (c, 1), 0)
    for j in range(n // c):
        r0 = j * c
        for col in range(0, width, LANES):
            cols = slice(col, col + LANES)
            cur = src_ref[r0:r0 + c, cols]
            prev = jnp.where(r == 0, 0.0, pltpu.roll(cur, 1, 0)) if j == 0 else src_ref[r0 - 1:r0 + c - 1, cols]
            nxt = (jnp.where(r == c - 1, 0.0, pltpu.roll(cur, c - 1, 0)) if r0 + c == n
                   else src_ref[r0 + 1:r0 + c + 1, cols])
            w = w_ref[:, cols]
            y = _silu(prev * w[0:1] + cur * w[1:2] + nxt * w[2:3])
            dst_ref[r0:r0 + c, cols] = y if scale == 1.0 else y * scale


def _split2(x):
    hi = x.astype(BF16)
    return hi, (x - hi.astype(F32)).astype(BF16)


def _mlstm_chunks(chains):
    c = SCAN_CHUNK
    row = lax.broadcasted_iota(jnp.int32, (c, c), 0)
    col = lax.broadcasted_iota(jnp.int32, (c, c), 1)
    eye_b = jnp.where(row == col, 1.0, 0.0).astype(BF16)
    tris = {False: row <= col, True: row >= col}
    tris_b = {r: jnp.where(t, 1.0, 0.0).astype(BF16) for r, t in tris.items()}
    n = range(len(chains))
    rev, q, k, vt, ig, fg, ct, nv, m_prev = zip(*chains)

    def each(fn):
        return [fn(i) for i in n]

    def dot3(parts, rhs, nt=False):
        d = _dot_nt if nt else _dot
        return d(parts[0], rhs) + d(parts[1], rhs) + d(parts[2], rhs)

    lf = each(lambda i: _split3(jnp.broadcast_to(_log_sigmoid(fg[i]), (8, c))))
    b = each(lambda i: dot3(lf[i], tris_b[rev[i]])[0:1])
    us = each(lambda i: _split3(jnp.broadcast_to(ig[i] - b[i], (c, c))))
    u = each(lambda i: _dot_nt(eye_b, us[i][0]) + _dot_nt(eye_b, us[i][1]) + _dot_nt(eye_b, us[i][2]))
    dmat = each(lambda i: jnp.where(tris[rev[i]], b[i] + u[i], -jnp.inf))
    m_t = each(lambda i: jnp.maximum(b[i] + m_prev[i], jnp.max(dmat[i], axis=0, keepdims=True)))
    qb = each(lambda i: q[i].astype(BF16))
    kb = each(lambda i: k[i].astype(BF16))
    kq = each(lambda i: _dot_nt(kb[i], qb[i]))
    p = each(lambda i: jnp.exp(dmat[i] - m_t[i]) * kq[i])
    inter = each(lambda i: jnp.exp(b[i] + m_prev[i] - m_t[i]))
    ns = each(lambda i: _split2(jnp.broadcast_to(nv[i], (8, nv[i].shape[1]))))
    qs = each(lambda i: _split2(q[i]))
    qn = each(lambda i: (_dot_nt(ns[i][0], qs[i][0]) + _dot_nt(ns[i][1], qs[i][0])
                         + _dot_nt(ns[i][0], qs[i][1]))[0:1])
    den = each(lambda i: inter[i] * qn[i] + jnp.sum(p[i], axis=0, keepdims=True))
    scale = each(lambda i: 1.0 / jnp.maximum(jnp.abs(den[i]), jnp.exp(-m_t[i])))
    cq = each(lambda i: _dot_nt(ct[i].astype(BF16), qb[i]))
    vp = each(lambda i: _dot(vt[i].astype(BF16), p[i].astype(BF16)))
    ht = each(lambda i: (inter[i] * cq[i] + vp[i]) * scale[i])
    last = each(lambda i: 0 if rev[i] else c - 1)
    m_new = each(lambda i: m_t[i][:, last[i]:last[i] + 1])
    b_exit = each(lambda i: b[i][:, last[i]:last[i] + 1])
    w = each(lambda i: jnp.exp(b_exit[i] - b[i] + ig[i] - m_new[i]))
    dec = each(lambda i: jnp.exp(b_exit[i] + m_prev[i] - m_new[i]))
    vk = each(lambda i: _dot((vt[i] * w[i]).astype(BF16), kb[i]))
    ct_new = each(lambda i: dec[i] * ct[i] + vk[i])
    ws = each(lambda i: _split2(jnp.broadcast_to(w[i], (8, c))))
    ks = each(lambda i: _split2(k[i]))
    wk = each(lambda i: (_dot(ws[i][0], ks[i][0]) + _dot(ws[i][1], ks[i][0]) + _dot(ws[i][0], ks[i][1]))[0:1])
    nv_new = each(lambda i: dec[i] * nv[i] + wk[i])
    return [(ht[i], ct_new[i], nv_new[i], m_new[i]) for i in n]


def _mlstm_kernel(seq_len, q_ref, k_ref, v_ref, og_ref, gate_t_ref, cwq_ref, cwk_ref,
                  c0_ref, n0_ref, m0_ref,
                  o_ref, c_out_ref, n_out_ref, m_out_ref,
                  q2_ref, k2_ref, vt_ref, hf_ref, hb_ref, ct_ref, n_ref, m_ref):
    c = SCAN_CHUNK
    hd = HEAD_DIM
    n_chunks = seq_len // c
    _conv3_silu_tiles(q2_ref, q_ref, cwq_ref, 1.0)
    _conv3_silu_tiles(k2_ref, k_ref, cwk_ref, HEAD_DIM ** -0.5)
    for hh in range(HEADS):
        cols = slice(hh * hd, (hh + 1) * hd)
        for j in range(n_chunks):
            vt_ref[hh, :, j * c:(j + 1) * c] = v_ref[j * c:(j + 1) * c, cols].T
        for d in range(2):
            ct_ref[d, hh] = c0_ref[d, hh].T
    n_ref[...] = n0_ref[...]
    m_ref[...] = m0_ref[...]

    def body(n, carry):
        where, chains = [], []
        for d in range(2):
            sl = pl.ds(pl.multiple_of((n_chunks - 1 - n if d else n) * c, c), c)
            for hh in range(HEADS):
                cols = slice(hh * hd, (hh + 1) * hd)
                gr = gate_t_ref[hh, :, sl]
                where.append((d, hh, sl))
                chains.append((bool(d), q2_ref[sl, cols], k2_ref[sl, cols], vt_ref[hh, :, sl],
                               gr[2 * d:2 * d + 1, :], gr[2 * d + 1:2 * d + 2, :],
                               ct_ref[d, hh], n_ref[d, hh], m_ref[d, hh]))
        for (d, hh, sl), (ht, ct, nv, m_new) in zip(where, _mlstm_chunks(chains)):
            ct_ref[d, hh] = ct
            n_ref[d, hh] = nv
            m_ref[d, hh] = m_new
            (hb_ref if d else hf_ref)[hh, :, sl] = ht
        return carry

    lax.fori_loop(0, n_chunks, body, 0)
    for hh in range(HEADS):
        cols = slice(hh * hd, (hh + 1) * hd)
        for j in range(n_chunks):
            rows = slice(j * c, (j + 1) * c)
            h = (hf_ref[hh, :, rows] + hb_ref[hh, :, rows]).T
            h = h * lax.rsqrt(jnp.mean(h * h, axis=-1, keepdims=True) + RMS_EPS)
            o_ref[rows, cols] = h * _sigmoid(og_ref[rows, cols])
        for d in range(2):
            c_out_ref[d, hh] = ct_ref[d, hh].T
    n_out_ref[...] = n_ref[...]
    m_out_ref[...] = m_ref[...]


def _mlstm(p, gate_t_h, conv_w, c0, n0, m0, n_seq, seq_len, tok_offset):
    hd = HEAD_DIM
    gw = GROUP_W
    row0 = tok_offset // seq_len

    def col(part):
        return pl.BlockSpec((seq_len, gw), lambda b: (row0 + b, part))

    c_spec = pl.BlockSpec((None, 2, HEADS, hd, hd), lambda b: (b, 0, 0, 0, 0))
    n_spec = pl.BlockSpec((None, 2, HEADS, 1, hd), lambda b: (b, 0, 0, 0, 0))
    m_spec = pl.BlockSpec((None, 2, HEADS, 1, 1), lambda b: (b, 0, 0, 0, 0))
    return pl.pallas_call(
        functools.partial(_mlstm_kernel, seq_len),
        grid=(n_seq,),
        in_specs=[col(5), col(6), col(7), col(8),
                  pl.BlockSpec((HEADS, 4, seq_len), lambda b: (0, 0, row0 + b)),
                  pl.BlockSpec((3, gw), lambda b: (0, 0)),
                  pl.BlockSpec((3, gw), lambda b: (0, 1)),
                  c_spec, n_spec, m_spec],
        out_specs=[pl.BlockSpec((seq_len, gw), lambda b: (b, 0)), c_spec, n_spec, m_spec],
        out_shape=[jax.ShapeDtypeStruct((n_seq * seq_len, gw), F32),
                   jax.ShapeDtypeStruct((n_seq, 2, HEADS, hd, hd), F32),
                   jax.ShapeDtypeStruct((n_seq, 2, HEADS, 1, hd), F32),
                   jax.ShapeDtypeStruct((n_seq, 2, HEADS, 1, 1), F32)],
        scratch_shapes=[pltpu.VMEM((seq_len, gw), F32), pltpu.VMEM((seq_len, gw), F32),
                        pltpu.VMEM((HEADS, hd, seq_len), F32),
                        pltpu.VMEM((HEADS, hd, seq_len), F32), pltpu.VMEM((HEADS, hd, seq_len), F32),
                        pltpu.VMEM((2, HEADS, hd, hd), F32), pltpu.VMEM((2, HEADS, 1, hd), F32),
                        pltpu.VMEM((2, HEADS, 1, 1), F32)],
        compiler_params=_cparams("parallel"),
        name=f"mlstm_l{seq_len}",
    )(p, p, p, p, gate_t_h, conv_w, conv_w, c0, n0, m0)


def _dft_tables(seq_len):
    n = 2 * seq_len
    k = jnp.arange(seq_len, dtype=jnp.int32)[:, None]
    t = jnp.arange(seq_len, dtype=jnp.int32)[None, :]
    ang = ((k * t) % n).astype(F32) * (2.0 * np.pi / n)
    fc = jnp.cos(ang)
    fs = jnp.sin(ang)
    nyq = jnp.where(t % 2 == 0, 1.0, -1.0).astype(F32)
    fs = jnp.where(k == 0, nyq, fs)
    return jnp.concatenate([fc, fs], axis=0)


def _filter_kernel(seq_len, z_ref, w1_ref, b1_ref, w2_ref, b2_ref, w3f_ref, w3b_ref, b3f_ref, b3b_ref,
                   f0_ref, f1_ref, rf_ref, rb_ref, fhi_ref, flo_ref, kf_ref, a_ref):
    hp = lax.Precision.HIGHEST
    n = 2 * seq_len
    z = z_ref[...]

    @pl.when(jnp.logical_and(pl.program_id(0) == 0, pl.program_id(1) == 0))
    def _():
        a1 = jnp.sin(f0_ref[...] * (jnp.dot(z, w1_ref[...], precision=hp, preferred_element_type=F32)
                                    + b1_ref[...]))
        a_ref[...] = jnp.sin(f1_ref[...] * (jnp.dot(a1, w2_ref[...], precision=hp, preferred_element_type=F32)
                                            + b2_ref[...]))

    a = a_ref[...]
    t_norm = z[:, 0:1]
    hf = (jnp.dot(a, w3f_ref[...], precision=hp, preferred_element_type=F32) + b3f_ref[...]) \
        * jnp.exp(-t_norm * jnp.exp(rf_ref[...]))
    hb = (jnp.dot(a, w3b_ref[...], precision=hp, preferred_element_type=F32) + b3b_ref[...]) \
        * jnp.exp(-t_norm * jnp.exp(rb_ref[...]))
    inv = lax.rsqrt(jnp.sum(hf * hf, axis=0, keepdims=True) + jnp.sum(hb * hb, axis=0, keepdims=True))
    hf = hf * inv
    r = lax.broadcasted_iota(jnp.int32, (seq_len, 1), 0)
    hb = jnp.where(r == 0, 0.0, hb * inv)
    sh, sl = _split2(hf + hb)
    dh, dl = _split2(hf - hb)
    fhi = fhi_ref[...]
    flo = flo_ref[...]
    kc = _dot(fhi[:seq_len], sh) + _dot(fhi[:seq_len], sl) + _dot(flo[:seq_len], sh)
    ks = _dot(fhi[seq_len:], dh) + _dot(fhi[seq_len:], dl) + _dot(flo[seq_len:], dh)
    sign = jnp.where(r % 2 == 0, 1.0, -1.0)
    k_nyq = jnp.sum(sign * (hf + hb), axis=0, keepdims=True)
    ks = jnp.where(r == 0, k_nyq, ks)
    scale = jnp.where(r == 0, 1.0 / n, 2.0 / n)
    kf_ref[0:seq_len, :] = kc * scale
    kf_ref[seq_len:n, :] = ks * scale


def _hyena_filters(seq_len, w1, b1, w2, b2, w3, b3, freq, log_rate, f_tab):
    d = D_MODEL
    cb = 256
    t = jnp.arange(seq_len, dtype=F32)
    t_norm = t / (seq_len - 1)
    bands = jnp.linspace(1e-4, HY_BANDS - 1, HY_BANDS, dtype=F32)
    ang = (2.0 * np.pi / seq_len) * t[:, None] * bands[None, :]
    z = jnp.concatenate([t_norm[:, None], jnp.cos(ang), jnp.sin(ang)], axis=-1)
    kpad = 128 - HY_EMB
    z = jnp.pad(z, ((0, 0), (0, kpad)))
    w1p = jnp.pad(w1, ((0, kpad), (0, 0)))
    f_hi = f_tab.astype(BF16)
    f_lo = (f_tab - f_hi.astype(F32)).astype(BF16)
    n_cb = d // cb
    hh = HY_HIDDEN
    row = lambda a: a.reshape(1, -1)
    const = lambda shape: pl.BlockSpec(shape, lambda o, j: (0,) * len(shape))
    fwd = lambda rows: pl.BlockSpec((rows, cb), lambda o, j: (0, o * n_cb + j))
    bwd = lambda rows: pl.BlockSpec((rows, cb), lambda o, j: (0, (HY_ORDER + o) * n_cb + j))
    return pl.pallas_call(
        functools.partial(_filter_kernel, seq_len),
        grid=(HY_ORDER, n_cb),
        in_specs=[const((seq_len, 128)), const((128, hh)), const((1, hh)), const((hh, hh)), const((1, hh)),
                  fwd(hh), bwd(hh), fwd(1), bwd(1),
                  const((1, hh)), const((1, hh)), fwd(1), bwd(1),
                  const((2 * seq_len, seq_len)), const((2 * seq_len, seq_len))],
        out_specs=pl.BlockSpec((None, 2 * seq_len, cb), lambda o, j: (o, 0, j)),
        out_shape=jax.ShapeDtypeStruct((HY_ORDER, 2 * seq_len, d), F32),
        scratch_shapes=[pltpu.VMEM((seq_len, hh), F32)],
        compiler_params=_cparams("arbitrary", "arbitrary"),
        name=f"hyena_filter_l{seq_len}",
    )(z, w1p, row(b1), w2, row(b2), w3, w3, row(b3), row(b3),
      row(freq[0]), row(freq[1]), row(log_rate), row(log_rate), f_hi, f_lo)


def _hyena_kernel(seq_len, seqs, v_ref, x1_ref, x2_ref, cwv_ref, cw1_ref, cw2_ref, bias_ref, kf_ref,
                  f_ref, ft_ref, o_ref, z_ref, zb_ref, y_ref):
    kc = min(HY_FREQ_CHUNK, seq_len)
    n_k = seq_len // kc
    r = lax.broadcasted_iota(jnp.int32, (kc, 1), 0)
    gate_refs = ((x1_ref, cw1_ref), (x2_ref, cw2_ref))
    for s in range(seqs):
        rows = slice(s * seq_len, (s + 1) * seq_len)
        z_ref[s] = _short_conv3(v_ref[rows, :], cwv_ref[...])
    for o in range(HY_ORDER):
        for s in range(seqs):
            zb_ref[s] = z_ref[s].astype(BF16)
            y_ref[s] = jnp.zeros(y_ref.shape[1:], F32)

        def freq_chunk(j, carry):
            r0 = pl.multiple_of(j * kc, kc)
            k_cos = kf_ref[o, pl.ds(r0, kc), :]
            k_sin = kf_ref[o, pl.ds(seq_len + r0, kc), :]
            real_row = jnp.logical_and(r == 0, j == 0)
            for s in range(seqs):
                a = _dot(f_ref[pl.ds(r0, kc), :], zb_ref[s])
                bm = _dot(f_ref[pl.ds(seq_len + r0, kc), :], zb_ref[s])
                yc = a * k_cos - jnp.where(real_row, 0.0, bm * k_sin)
                ys = jnp.where(real_row, bm * k_sin, a * k_sin + bm * k_cos)
                y_ref[s] += _dot(ft_ref[j], yc.astype(BF16)) + _dot(ft_ref[n_k + j], ys.astype(BF16))
            return carry

        lax.fori_loop(0, n_k, freq_chunk, 0)
        x_ref, cw_ref = gate_refs[o]
        for s in range(seqs):
            rows = slice(s * seq_len, (s + 1) * seq_len)
            gate = _short_conv3(x_ref[rows, :], cw_ref[...])
            z_ref[s] = gate * (y_ref[s] + z_ref[s] * bias_ref[o:o + 1, :])
    for s in range(seqs):
        o_ref[s * seq_len:(s + 1) * seq_len, :] = z_ref[s]


def _hyena(u, conv_w, bias, kf, f_tab, n_seq, seq_len, tok_offset):
    d = D_MODEL
    cb = 256
    n_cb = d // cb
    seqs = max(1, HY_STEP_ROWS // seq_len)
    rows = seqs * seq_len
    row0 = tok_offset // rows
    kc = min(HY_FREQ_CHUNK, seq_len)
    n_k = seq_len // kc
    f_bf = f_tab.astype(BF16)
    ft = f_bf.T.reshape(seq_len, 2 * n_k, kc).transpose(1, 0, 2)

    def part(k):
        return pl.BlockSpec((rows, cb), lambda j, b: (row0 + b, k * n_cb + j))

    def cw(k):
        return pl.BlockSpec((3, cb), lambda j, b: (0, k * n_cb + j))

    return pl.pallas_call(
        functools.partial(_hyena_kernel, seq_len, seqs),
        grid=(n_cb, n_seq // seqs),
        in_specs=[part(0), part(1), part(2), cw(0), cw(1), cw(2),
                  pl.BlockSpec((HY_ORDER, cb), lambda j, b: (0, j)),
                  pl.BlockSpec((HY_ORDER, 2 * seq_len, cb), lambda j, b: (0, 0, j)),
                  pl.BlockSpec((2 * seq_len, seq_len), lambda j, b: (0, 0)),
                  pl.BlockSpec((2 * n_k, seq_len, kc), lambda j, b: (0, 0, 0))],
        out_specs=pl.BlockSpec((rows, cb), lambda j, b: (b, j)),
        out_shape=jax.ShapeDtypeStruct((n_seq * seq_len, d), F32),
        scratch_shapes=[pltpu.VMEM((seqs, seq_len, cb), F32), pltpu.VMEM((seqs, seq_len, cb), BF16),
                        pltpu.VMEM((seqs, seq_len, cb), F32)],
        compiler_params=_cparams("parallel", "parallel"),
        name=f"hyena_l{seq_len}",
    )(u, u, u, conv_w, conv_w, conv_w, bias, kf, f_bf, ft)


def _mix_out_kernel(n_parts, *refs):
    o_refs = refs[:2 * n_parts]
    w_refs = refs[2 * n_parts:3 * n_parts]
    x_ref, mod_ref, g_ref, wr_both_ref, br_ref = refs[3 * n_parts:3 * n_parts + 5]
    x1_ref, h_ref, eid_ref, wts_ref, rank_ref, cnt_ref, run_ref = refs[3 * n_parts + 5:]
    i = pl.program_id(0)
    t = ROUTE_TILE
    ne = N_EXPERTS

    from_prompt = i < N_PROMPT // ROUTE_TILE
    y = None
    for j in range(n_parts):
        o = jnp.where(from_prompt, o_refs[2 * j][...], o_refs[2 * j + 1][...])
        yj = _dot(o.astype(BF16), w_refs[j][...])
        y = yj if y is None else y + yj
    x1 = x_ref[...] + mod_ref[2:3, :] * y
    x1_ref[...] = x1
    h = _norm_mod(x1, g_ref[...], mod_ref[4:5, :], mod_ref[3:4, :])
    h_ref[...] = _pack_bf16_halves(h)
    h_hi = h.astype(BF16)
    h_lo = (h - h_hi.astype(F32)).astype(BF16)
    hi_terms = _dot(h_hi, wr_both_ref[...])
    logits = (hi_terms[:, :LANES] + hi_terms[:, LANES:] + _dot(h_lo, wr_both_ref[:, :LANES])).T[:ne] \
        + br_ref[...]

    @pl.when(i == 0)
    def _():
        run_ref[...] = jnp.zeros_like(run_ref)

    e_iota = lax.broadcasted_iota(jnp.int32, (ne, t), 0)
    vals, eids, onehots = [], [], []
    for _k in range(TOP_K):
        m = jnp.max(logits, axis=0, keepdims=True)
        eid = jnp.min(jnp.where(logits == m, e_iota, ne), axis=0, keepdims=True)
        sel = e_iota == eid
        logits = jnp.where(sel, -jnp.inf, logits)
        onehots.append(jnp.where(sel, 1.0, 0.0))
        vals.append(m)
        eids.append(eid)
    r2 = lax.broadcasted_iota(jnp.int32, (t, t), 0)
    c2 = lax.broadcasted_iota(jnp.int32, (t, t), 1)
    before = jnp.where(r2 < c2, 1.0, 0.0).astype(BF16)
    earlier = _dot(jnp.concatenate(onehots, axis=0).astype(BF16), before)
    running = run_ref[...]
    ranks = []
    for k, onehot in enumerate(onehots):
        ranks.append(jnp.sum(onehot * (running + earlier[k * ne:(k + 1) * ne]), axis=0, keepdims=True))
        running = running + jnp.sum(onehot, axis=1, keepdims=True)
    run_ref[...] = running
    cnt_ref[...] = running
    v = jnp.concatenate(vals, axis=0)
    ex = jnp.exp(v - v[0:1])
    wts_ref[...] = ex / jnp.sum(ex, axis=0, keepdims=True)
    eid_ref[...] = jnp.concatenate(eids, axis=0)
    rank_ref[...] = jnp.concatenate(ranks, axis=0).astype(jnp.int32)


def _mix_out(parts, x, mod_l, norm_g, w_router, b_router):
    d = D_MODEL
    t = ROUTE_TILE
    ne = N_EXPERTS
    n_parts = len(parts)
    n_prompt_tiles = N_PROMPT // t
    wr = jnp.pad(w_router, ((0, 0), (0, LANES - ne)))
    wr_hi = wr.astype(BF16)
    wr_both = jnp.concatenate([wr_hi, (wr - wr_hi.astype(F32)).astype(BF16)], axis=1)
    in_specs = []
    for o_p, _, _ in parts:
        in_specs.append(pl.BlockSpec((t, o_p.shape[1]), lambda i: (jnp.minimum(i, n_prompt_tiles - 1), 0)))
        in_specs.append(pl.BlockSpec((t, o_p.shape[1]), lambda i: (jnp.maximum(i - n_prompt_tiles, 0), 0)))
    in_specs += [pl.BlockSpec(w.shape, lambda i: (0, 0)) for _, _, w in parts]
    in_specs += [
        pl.BlockSpec((t, d), lambda i: (i, 0)),
        pl.BlockSpec((None, N_MOD, d), lambda i: (_tile_cond_row(i, t), 0, 0)),
        pl.BlockSpec((1, d), lambda i: (0, 0)),
        pl.BlockSpec((d, 2 * LANES), lambda i: (0, 0)),
        pl.BlockSpec((ne, 1), lambda i: (0, 0)),
    ]
    tok_major = pl.BlockSpec((TOP_K, t), lambda i: (0, i))
    return pl.pallas_call(
        functools.partial(_mix_out_kernel, n_parts),
        grid=(N_TOK // t,),
        in_specs=in_specs,
        out_specs=[pl.BlockSpec((t, d), lambda i: (i, 0)), pl.BlockSpec((t, d // 2), lambda i: (i, 0)),
                   tok_major, tok_major, tok_major, pl.BlockSpec((ne, 1), lambda i: (0, 0))],
        out_shape=[jax.ShapeDtypeStruct((N_TOK, d), F32), jax.ShapeDtypeStruct((N_TOK, d // 2), jnp.uint32),
                   jax.ShapeDtypeStruct((TOP_K, N_TOK), jnp.int32), jax.ShapeDtypeStruct((TOP_K, N_TOK), F32),
                   jax.ShapeDtypeStruct((TOP_K, N_TOK), jnp.int32), jax.ShapeDtypeStruct((ne, 1), F32)],
        scratch_shapes=[pltpu.VMEM((ne, 1), F32)],
        compiler_params=_cparams("arbitrary"),
        name="mix_out_router",
    )(*[o for part in parts for o in part[:2]], *[w.astype(BF16) for _, _, w in parts], x, mod_l,
      norm_g.reshape(1, d), wr_both, b_router.reshape(ne, 1))


def _sc_row_gather(src, idx):
    n = idx.shape[0]
    width = src.shape[1]
    step_rows = min(SC_MAX_INDICES, SC_STEP_BYTES // (width * 4))
    sc = plsc.get_sparse_core_info()
    n_workers = sc.num_cores * sc.num_subcores
    per_worker = n // n_workers
    if n % n_workers or per_worker % step_rows:
        raise ValueError("row count must be whole SparseCore steps on every subcore")
    mesh = plsc.VectorSubcoreMesh(core_axis_name="core", subcore_axis_name="subcore")

    @functools.partial(pl.kernel, out_type=jax.ShapeDtypeStruct((n, width), src.dtype), mesh=mesh,
                       scratch_types=[pltpu.VMEM((per_worker,), jnp.int32),
                                      pltpu.VMEM((step_rows, width), src.dtype)],
                       name="sc_row_gather")
    def gather(src_hbm, idx_hbm, dst_hbm, idx_vmem, rows_vmem):
        worker = lax.axis_index("subcore") * sc.num_cores + lax.axis_index("core")
        base = worker * per_worker
        pltpu.sync_copy(idx_hbm.at[pl.ds(base, per_worker)], idx_vmem)

        @pl.loop(0, per_worker // step_rows)
        def _(c):
            pltpu.sync_copy(src_hbm.at[idx_vmem.at[pl.ds(c * step_rows, step_rows)]], rows_vmem)
            pltpu.sync_copy(rows_vmem, dst_hbm.at[pl.ds(base + c * step_rows, step_rows)])

    return gather(src, idx)


def _sc_row_scatter(src, row_of_slot):
    n = row_of_slot.shape[0]
    n_src, width = src.shape
    step_rows = min(SC_MAX_INDICES, SC_STEP_BYTES // (width * 4))
    sc = plsc.get_sparse_core_info()
    n_workers = sc.num_cores * sc.num_subcores
    per_worker = n // n_workers
    steps = per_worker // step_rows
    if n % n_workers or per_worker % step_rows or n_src % step_rows:
        raise ValueError("row counts must be whole SparseCore steps on every subcore")
    mesh = plsc.VectorSubcoreMesh(core_axis_name="core", subcore_axis_name="subcore")

    @functools.partial(pl.kernel, out_type=jax.ShapeDtypeStruct((n, width), src.dtype), mesh=mesh,
                       scratch_types=[pltpu.VMEM((steps, step_rows), jnp.int32),
                                      pltpu.VMEM((step_rows, width), src.dtype)],
                       name="sc_row_scatter")
    def scatter(src_hbm, idx_hbm, dst_hbm, idx_vmem, rows_vmem):
        worker = lax.axis_index("subcore") * sc.num_cores + lax.axis_index("core")
        pltpu.sync_copy(idx_hbm.at[worker], idx_vmem)

        @pl.loop(0, steps)
        def _(c):
            src0 = lax.rem(worker * per_worker + c * step_rows, n_src)
            pltpu.sync_copy(src_hbm.at[pl.ds(src0, step_rows)], rows_vmem)
            pltpu.sync_copy(rows_vmem, dst_hbm.at[idx_vmem.at[c]])

    return scatter(src, row_of_slot.reshape(n_workers, steps, step_rows))


def _pack_bf16_halves(x):
    w = x.shape[1] // 2
    bits = pltpu.bitcast(x.astype(BF16).astype(F32), jnp.uint32)
    return bits[:, :w] | (bits[:, w:] >> 16)


def _unpack_bf16_halves(p):
    hi = pltpu.bitcast(p & jnp.uint32(0xFFFF0000), F32).astype(BF16)
    lo = pltpu.bitcast(p << 16, F32).astype(BF16)
    return hi, lo


def _experts_kernel(layer, te_ref, first_ref, slot_ref, next_ref, rows_ref, nv_ref,
                    x_ref, wgu_hbm, bg_ref, bu_ref, wd_hbm, bd_ref, sel_ref,
                    y_ref, wgu_buf, wd_buf, wg_ref, wu_ref, wdb_ref, sem):
    i = pl.program_id(0)
    valid = i < nv_ref[0]
    half = DEINT_COLS // 2
    k_half = x_ref.shape[1]

    def fetch(expert, slot):
        return (pltpu.make_async_copy(wgu_hbm.at[layer, expert], wgu_buf.at[slot], sem.at[slot, 0]),
                pltpu.make_async_copy(wd_hbm.at[layer, expert], wd_buf.at[slot], sem.at[slot, 1]))

    @pl.when(i == 0)
    def _():
        for cp in fetch(te_ref[0], 0):
            cp.start()

    @pl.when(jnp.logical_and(valid, first_ref[i] == 1))
    def _():
        slot = slot_ref[i]
        for cp in fetch(te_ref[i], slot):
            cp.wait()

        @pl.when(next_ref[i] >= 0)
        def _():
            for cp in fetch(next_ref[i], 1 - slot):
                cp.start(priority=1)

        for c in range(wgu_buf.shape[2] // DEINT_COLS):
            w = wgu_buf[slot, :, c * DEINT_COLS:(c + 1) * DEINT_COLS].astype(BF16)
            split = _dot(w, sel_ref[...]).astype(BF16)
            wg_ref[:, c * half:(c + 1) * half] = split[:, :half]
            wu_ref[:, c * half:(c + 1) * half] = split[:, half:]
        wdb_ref[...] = wd_buf[slot].astype(BF16)

    def ffn(rows):
        x_l, x_r = _unpack_bf16_halves(x_ref[:rows, :])
        gl = _dot(x_l, wg_ref[:k_half, :]) + _dot(x_r, wg_ref[k_half:, :]) + bg_ref[...]
        up = _dot(x_l, wu_ref[:k_half, :]) + _dot(x_r, wu_ref[k_half:, :]) + bu_ref[...]
        gl = jnp.minimum(gl, SWIGLU_LIMIT)
        up = jnp.clip(up, -SWIGLU_LIMIT, SWIGLU_LIMIT)
        act = (up + 1.0) * gl * _sigmoid(SWIGLU_ALPHA * gl)
        y_ref[:rows, :] = _pack_bf16_halves(_dot(act.astype(BF16), wdb_ref[...]) + bd_ref[...])

    tm = x_ref.shape[0]
    few = rows_ref[i] <= tm // 2

    @pl.when(jnp.logical_and(valid, jnp.logical_not(few)))
    def _():
        ffn(tm)

    @pl.when(jnp.logical_and(valid, few))
    def _():
        ffn(tm // 2)
        y_ref[tm // 2:, :] = jnp.zeros((tm - tm // 2, y_ref.shape[1]), y_ref.dtype)

    @pl.when(jnp.logical_not(valid))
    def _():
        y_ref[...] = jnp.zeros_like(y_ref)


def _experts(xs, tile_expert, tile_first, tile_rows, n_valid, layer, w_gu, b_gate, b_up, w_down, b_down):
    d = D_MODEL
    tm = MOE_TILE
    n_tiles = MOE_ROWS // tm
    ff = w_down.shape[2]
    half = DEINT_COLS // 2
    r = jnp.arange(DEINT_COLS)[:, None]
    c = jnp.arange(DEINT_COLS)[None, :]
    sel = (r == jnp.where(c < half, 2 * c, 2 * (c - half) + 1)).astype(BF16)
    group = jnp.cumsum(tile_first) - 1
    tile_slot = (group % 2).astype(jnp.int32)
    is_last_group = group == group[-1]
    following = jnp.concatenate([tile_expert[1:], tile_expert[-1:]])
    idx = jnp.arange(n_tiles, dtype=jnp.int32)
    group_end = jnp.max(jnp.where(group[None, :] == group[:, None], idx[None, :], -1), axis=1)
    tile_next = jnp.where(is_last_group, -1, following[group_end]).astype(jnp.int32)
    wspec = lambda k, n: pl.BlockSpec((None, None, k, n), lambda i, *_: (layer, _[0][i], 0, 0))
    grid_spec = pltpu.PrefetchScalarGridSpec(
        num_scalar_prefetch=6,
        grid=(n_tiles,),
        in_specs=[pl.BlockSpec((tm, d // 2), lambda i, *_: (i, 0)),
                  pl.BlockSpec(memory_space=pl.ANY), wspec(1, ff), wspec(1, ff),
                  pl.BlockSpec(memory_space=pl.ANY), wspec(1, d),
                  pl.BlockSpec((DEINT_COLS, DEINT_COLS), lambda i, *_: (0, 0))],
        out_specs=pl.BlockSpec((tm, d // 2), lambda i, *_: (i, 0)),
        scratch_shapes=[pltpu.VMEM((2, d, 2 * ff), F32), pltpu.VMEM((2, ff, d), F32),
                        pltpu.VMEM((d, ff), BF16), pltpu.VMEM((d, ff), BF16), pltpu.VMEM((ff, d), BF16),
                        pltpu.SemaphoreType.DMA((2, 2))],
    )
    return pl.pallas_call(
        functools.partial(_experts_kernel, layer),
        grid_spec=grid_spec,
        out_shape=jax.ShapeDtypeStruct((MOE_ROWS, d // 2), jnp.uint32),
        compiler_params=_cparams("arbitrary"),
        name="experts",
    )(tile_expert, tile_first, tile_slot, tile_next, tile_rows, n_valid, xs, w_gu, b_gate, b_up, w_down,
      b_down, sel)


def _combine_kernel(final, x_ref, g_ref, w_ref, mod_ref, fg_ref, *o_refs):
    y_l = y_r = None
    for k in range(TOP_K):
        g_l, g_r = _unpack_bf16_halves(g_ref[k])
        wk = w_ref[:, k:k + 1]
        y_l = wk * g_l.astype(F32) if y_l is None else y_l + wk * g_l.astype(F32)
        y_r = wk * g_r.astype(F32) if y_r is None else y_r + wk * g_r.astype(F32)
    x = x_ref[...] + mod_ref[5:6, :] * jnp.concatenate([y_l, y_r], axis=1)
    if not final:
        o_refs[0][...] = x
        return
    x = x * lax.rsqrt(jnp.mean(x * x, axis=-1, keepdims=True) + RMS_EPS) * fg_ref[...]
    from_prompt = pl.program_id(0) < N_PROMPT // ROUTE_TILE

    @pl.when(from_prompt)
    def _():
        o_refs[0][...] = x

    @pl.when(jnp.logical_not(from_prompt))
    def _():
        o_refs[1][...] = x


def _combine(x1, gathered, wts, mod_l, final_g, final):
    d = D_MODEL
    t = ROUTE_TILE
    n_prompt_tiles = N_PROMPT // t
    if final:
        out_specs = [pl.BlockSpec((t, d), lambda i: (jnp.minimum(i, n_prompt_tiles - 1), 0)),
                     pl.BlockSpec((t, d), lambda i: (jnp.maximum(i - n_prompt_tiles, 0), 0))]
        out_shape = [jax.ShapeDtypeStruct((N_PROMPT, d), F32), jax.ShapeDtypeStruct((N_SAMPLE, d), F32)]
    else:
        out_specs = pl.BlockSpec((t, d), lambda i: (i, 0))
        out_shape = jax.ShapeDtypeStruct((N_TOK, d), F32)
    return pl.pallas_call(
        functools.partial(_combine_kernel, final),
        grid=(N_TOK // t,),
        in_specs=[pl.BlockSpec((t, d), lambda i: (i, 0)),
                  pl.BlockSpec((TOP_K, t, d // 2), lambda i: (0, i, 0)),
                  pl.BlockSpec((t, TOP_K), lambda i: (i, 0)),
                  pl.BlockSpec((None, N_MOD, d), lambda i: (_tile_cond_row(i, t), 0, 0)),
                  pl.BlockSpec((1, d), lambda i: (0, 0))],
        out_specs=out_specs,
        out_shape=out_shape,
        compiler_params=_cparams("arbitrary"),
        name="moe_combine",
    )(x1, gathered, wts, mod_l, final_g.reshape(1, d))


def _moe(x1, h, eid, wts, rank, counts, mod_l, layer, w_gu, b_gu, w_down, b_down, final_g, final):
    d = D_MODEL
    tm = MOE_TILE
    n_tiles = MOE_ROWS // tm
    cnt = counts.reshape(N_EXPERTS).astype(jnp.int32)
    gsz = ((cnt + tm - 1) // tm) * tm
    ends = jnp.cumsum(gsz)
    offs = ends - gsz
    e_ids = jnp.arange(N_EXPERTS, dtype=jnp.int32)
    pos = jnp.sum(jnp.where(eid[..., None] == e_ids, offs, 0), axis=-1) + rank
    tile_start = jnp.arange(n_tiles, dtype=jnp.int32) * tm
    tile_expert = jnp.minimum(jnp.sum((ends[None, :] <= tile_start[:, None]).astype(jnp.int32), axis=1),
                              N_EXPERTS - 1)
    n_valid = (ends[-1:] // tm).astype(jnp.int32)
    last_valid = jnp.maximum(n_valid[0] - 1, 0)
    tile_expert = jnp.where(jnp.arange(n_tiles) < n_valid[0], tile_expert, tile_expert[last_valid])
    tile_first = jnp.concatenate([jnp.ones((1,), jnp.int32),
                                  (tile_expert[1:] != tile_expert[:-1]).astype(jnp.int32)])
    j = jnp.arange(tm, dtype=jnp.int32)[None, :]
    pad_used = j < (gsz - cnt)[:, None]
    n_unused_before = jnp.cumsum((~pad_used).reshape(-1).astype(jnp.int32)) - 1
    pad_pos = jnp.where(pad_used, (offs + cnt)[:, None] + j,
                        ends[-1] + n_unused_before.reshape(N_EXPERTS, tm))
    row_of_slot = jnp.concatenate([pos.reshape(-1), pad_pos.reshape(-1)])

    xs = _sc_row_scatter(h, row_of_slot)
    real_end = (offs + cnt)[tile_expert]
    tile_rows = jnp.clip(real_end - tile_start, 0, tm).astype(jnp.int32)
    ys = _experts(xs, tile_expert, tile_first, tile_rows, n_valid, layer, w_gu,
                  b_gu[:, :, None, 0::2], b_gu[:, :, None, 1::2], w_down, b_down[:, :, None, :])
    gathered = _sc_row_gather(ys, pos.reshape(-1))
    return _combine(x1, gathered.reshape(TOP_K, N_TOK, d // 2), wts.T, mod_l, final_g, final)


def _grid_positions(n_tok, d):
    rows = n_tok // GRID_W
    r, col = jnp.meshgrid(jnp.arange(rows, dtype=F32), jnp.arange(GRID_W, dtype=F32), indexing='ij')
    r = r.reshape(-1)
    col = col.reshape(-1)
    quarter = d // 4
    inv = 1.0 / (10000.0 ** (jnp.arange(quarter, dtype=F32) / quarter))
    ar = r[:, None] * inv[None]
    ac = col[:, None] * inv[None]
    return jnp.concatenate([jnp.sin(ar), jnp.cos(ar), jnp.sin(ac), jnp.cos(ac)], axis=-1)


def kernel(x_prompt, x_sample, state_hgrn, state_mlstm_c, state_mlstm_n, state_mlstm_m, c, c_ctx,
           norm_g, final_g, w_mod, b_mod, ev_w_in, ev_gate_b, ev_conv, hg_lb, ev_w_out,
           hy_w_in, hy_conv, hy_w1, hy_b1, hy_w2, hy_b2, hy_w3, hy_b3, hy_freq, hy_log_rate, hy_bias, hy_w_out,
           w_router, b_router, w_gu, b_gu, w_down, b_down):
    d = D_MODEL
    hd = HEAD_DIM
    cond = jnp.concatenate([c_ctx[None], c, jnp.zeros((N_COND - 1 - DEC_BATCH, d), F32)], axis=0)
    mod = _modulation(cond, w_mod, b_mod)
    pos_tab = jnp.concatenate([jnp.zeros((TOK_TILE, d), F32), _grid_positions(DEC_SEQ, d)], axis=0)

    groups = ((BATCH, SEQ, 0), (DEC_BATCH, DEC_SEQ, N_PROMPT))
    new_states = None
    for l in range(DEPTH):
        if l % 2 == 0:
            e = l // 2
            if l == 0:
                x, p, gate_t = _proj_even(x_prompt.reshape(N_PROMPT, d), x_sample.reshape(N_SAMPLE, d),
                                          pos_tab, mod[l], norm_g[l, 0], ev_w_in[e], ev_gate_b[e])
            else:
                raise NotImplementedError("only the first layer adds grid positions")
            gate_t_h = gate_t.reshape(4, HEADS, N_TOK).transpose(1, 0, 2)
            o_hg, o_ml = [], []
            for gi, (n_seq, seq_len, off) in enumerate(groups):
                if gi == 0:
                    s0 = jnp.zeros((n_seq, 2, HEADS, hd, hd), F32)
                    c0 = jnp.zeros((n_seq, 2, HEADS, hd, hd), F32)
                    n0 = jnp.zeros((n_seq, 2, HEADS, 1, hd), F32)
                    m0 = jnp.zeros((n_seq, 2, HEADS, 1, 1), F32)
                else:
                    s0 = state_hgrn[:, e]
                    c0 = state_mlstm_c[:, e]
                    n0 = state_mlstm_n[:, e].reshape(n_seq, 2, HEADS, 1, hd)
                    m0 = state_mlstm_m[:, e].reshape(n_seq, 2, HEADS, 1, 1)
                og, s_fin = _hgrn(p, hg_lb, l, s0, n_seq, seq_len, off)
                om, c_fin, n_fin, m_fin = _mlstm(p, gate_t_h, ev_conv[e], c0, n0, m0, n_seq, seq_len, off)
                o_hg.append(og)
                o_ml.append(om)
                if gi == 0:
                    new_states = (s_fin[:, None], c_fin[:, None],
                                  n_fin.reshape(n_seq, 1, 2, HEADS, hd), m_fin.reshape(n_seq, 1, 2, HEADS))
            parts = [(*o_hg, ev_w_out[e][:GROUP_W]), (*o_ml, ev_w_out[e][GROUP_W:])]
        else:
            o = l // 2
            u = _proj_odd(x, mod[l], norm_g[l, 0], hy_w_in[o])
            zs = []
            for n_seq, seq_len, off in groups:
                f_tab = _dft_tables(seq_len)
                kf = _hyena_filters(seq_len, hy_w1[o], hy_b1[o], hy_w2[o], hy_b2[o], hy_w3[o], hy_b3[o],
                                    hy_freq[o], hy_log_rate[o], f_tab)
                zs.append(_hyena(u, hy_conv[o], hy_bias[o], kf, f_tab, n_seq, seq_len, off))
            parts = [(*zs, hy_w_out[o])]
        x1, h, eid, wts, rank, counts = _mix_out(parts, x, mod[l], norm_g[l, 1], w_router[l], b_router[l])
        x = _moe(x1, h, eid, wts, rank, counts, mod[l], l, w_gu, b_gu, w_down, b_down,
                 final_g, final=(l == DEPTH - 1))

    y_prompt, y_sample = x
    return (y_prompt.reshape(BATCH, SEQ, d), y_sample.reshape(DEC_BATCH, DEC_SEQ, d)) + new_states
```

```python
import functools

import numpy as np
import jax
import jax.numpy as jnp
from jax import lax
from jax.experimental import pallas as pl
from jax.experimental.pallas import tpu as pltpu
from jax.experimental.pallas import tpu_sc as plsc

F32 = jnp.float32
BF16 = jnp.bfloat16

D_MODEL = 1024
BATCH = 32
SEQ = 256
DEPTH = 2
DEC_BATCH = 8
DEC_SEQ = 1024
GRID_W = 64
RMS_EPS = 1e-6
N_MOD = 6
LOG2_E = 1.4426950408889634
LANES = 128

HEADS = 4
HEAD_DIM = 128
GROUP_W = HEADS * HEAD_DIM
N_GATES = 4 * HEADS
EVEN_MAIN = 9 * GROUP_W

HY_ORDER = 2
HY_BANDS = 16
HY_EMB = 1 + 2 * HY_BANDS
HY_HIDDEN = 64
HY_FREQ_CHUNK = 512
HY_STEP_ROWS = 1024

N_EXPERTS = 32
TOP_K = 4
SWIGLU_LIMIT = 7.0
SWIGLU_ALPHA = 1.702

N_PROMPT = BATCH * SEQ
N_SAMPLE = DEC_BATCH * DEC_SEQ
N_TOK = N_PROMPT + N_SAMPLE
N_COND = 16

TOK_TILE = 512
ROUTE_TILE = 512
SCAN_CHUNK = 128
SUB = 16
HALF = 8
HG_STEP_HEADS = 4
MOE_TILE = 512
MOE_ROWS = N_TOK * TOP_K + N_EXPERTS * MOE_TILE
DEINT_COLS = 256
SC_STEP_BYTES = 256 * 1024
SC_MAX_INDICES = 128

VMEM_LIMIT = 56 * 1024 * 1024


def _cparams(*sem):
    return pltpu.CompilerParams(dimension_semantics=sem, vmem_limit_bytes=VMEM_LIMIT)


def _split3(x):
    hi = x.astype(BF16)
    r = x - hi.astype(F32)
    mid = r.astype(BF16)
    lo = (r - mid.astype(F32)).astype(BF16)
    return hi, mid, lo


def _dot(a, b):
    return jnp.dot(a, b, preferred_element_type=F32)


def _dot_nt(a, b):
    return lax.dot_general(a, b, (((1,), (1,)), ((), ())), preferred_element_type=F32)


def _dot_tn(a, b):
    return lax.dot_general(a, b, (((0,), (0,)), ((), ())), preferred_element_type=F32)


def _dot_w3(a_exact_bf16, x):
    hi, mid, lo = _split3(x)
    return _dot(a_exact_bf16, hi) + _dot(a_exact_bf16, mid) + _dot(a_exact_bf16, lo)


def _sigmoid(x):
    return 1.0 / (1.0 + jnp.exp(-x))


def _silu(x):
    return x * _sigmoid(x)


def _log_sigmoid(x):
    return jnp.minimum(x, 0.0) - jnp.log(1.0 + jnp.exp(-jnp.abs(x)))


def _tile_cond_row(i, tile=TOK_TILE):
    n_prompt_tiles = N_PROMPT // tile
    tiles_per_seq = DEC_SEQ // tile
    return jnp.where(i < n_prompt_tiles, 0, 1 + (i - n_prompt_tiles) // tiles_per_seq)


def _mod_kernel(cond_ref, w_ref, b_ref, o_ref):
    a = _silu(cond_ref[...]).astype(BF16)
    o_ref[...] = _dot(a, w_ref[...].astype(BF16)) + b_ref[...]


def _modulation(cond, w_mod, b_mod):
    d = D_MODEL
    out = pl.pallas_call(
        _mod_kernel,
        grid=(DEPTH, N_MOD),
        in_specs=[
            pl.BlockSpec((N_COND, d), lambda l, j: (0, 0)),
            pl.BlockSpec((None, d, d), lambda l, j: (l, 0, j)),
            pl.BlockSpec((None, 1, d), lambda l, j: (l, 0, j)),
        ],
        out_specs=pl.BlockSpec((None, None, N_COND, d), lambda l, j: (l, j, 0, 0)),
        out_shape=jax.ShapeDtypeStruct((DEPTH, N_MOD, N_COND, d), F32),
        compiler_params=_cparams("parallel", "parallel"),
        name="modulation",
    )(cond, w_mod, b_mod.reshape(DEPTH, 1, N_MOD * d))
    return out.transpose(0, 2, 1, 3)


def _norm_mod(x, g_row, scale_row, shift_row):
    ms = jnp.mean(x * x, axis=-1, keepdims=True)
    y = x * lax.rsqrt(ms + RMS_EPS) * g_row
    return y * (1.0 + scale_row) + shift_row


def _proj_even_kernel(xp_ref, xs_ref, pos_ref, mod_ref, g_ref, w_ref, wgt_ref, gbt_ref,
                      xres_ref, p_ref, gate_t_ref):
    from_prompt = pl.program_id(0) < N_PROMPT // TOK_TILE
    x = jnp.where(from_prompt, xp_ref[...], xs_ref[...]) + pos_ref[...]
    xres_ref[...] = x
    h = _norm_mod(x, g_ref[...], mod_ref[1:2, :], mod_ref[0:1, :]).astype(BF16)
    p_ref[...] = _dot(h, w_ref[...])
    gate_t_ref[...] = _dot_nt(wgt_ref[...], h) + gbt_ref[...]


def _proj_even(x_prompt, x_sample, pos_tab, mod_l, norm_g, w_in, gate_b):
    d = D_MODEL
    n_tiles = N_TOK // TOK_TILE
    n_prompt_tiles = N_PROMPT // TOK_TILE
    tiles_per_seq = DEC_SEQ // TOK_TILE
    w_main = w_in[:, :EVEN_MAIN].astype(BF16)
    w_gate = w_in[:, EVEN_MAIN:].astype(BF16)

    def pos_map(i):
        return (jnp.where(i < n_prompt_tiles, 0, 1 + (i - n_prompt_tiles) % tiles_per_seq), 0)

    return pl.pallas_call(
        _proj_even_kernel,
        grid=(n_tiles,),
        in_specs=[
            pl.BlockSpec((TOK_TILE, d), lambda i: (jnp.minimum(i, n_prompt_tiles - 1), 0)),
            pl.BlockSpec((TOK_TILE, d), lambda i: (jnp.maximum(i - n_prompt_tiles, 0), 0)),
            pl.BlockSpec((TOK_TILE, d), pos_map),
            pl.BlockSpec((None, N_MOD, d), lambda i: (_tile_cond_row(i), 0, 0)),
            pl.BlockSpec((1, d), lambda i: (0, 0)),
            pl.BlockSpec((d, EVEN_MAIN), lambda i: (0, 0), pipeline_mode=pl.Buffered(1)),
            pl.BlockSpec((N_GATES, d), lambda i: (0, 0)),
            pl.BlockSpec((N_GATES, 1), lambda i: (0, 0)),
        ],
        out_specs=[
            pl.BlockSpec((TOK_TILE, d), lambda i: (i, 0)),
            pl.BlockSpec((TOK_TILE, EVEN_MAIN), lambda i: (i, 0)),
            pl.BlockSpec((N_GATES, TOK_TILE), lambda i: (0, i)),
        ],
        out_shape=[
            jax.ShapeDtypeStruct((N_TOK, d), F32),
            jax.ShapeDtypeStruct((N_TOK, EVEN_MAIN), F32),
            jax.ShapeDtypeStruct((N_GATES, N_TOK), F32),
        ],
        compiler_params=_cparams("parallel"),
        name="proj_even",
    )(x_prompt, x_sample, pos_tab, mod_l, norm_g.reshape(1, d), w_main, w_gate.T, gate_b.reshape(N_GATES, 1))


def _proj_odd_kernel(x_ref, mod_ref, g_ref, w_ref, p_ref):
    h = _norm_mod(x_ref[...], g_ref[...], mod_ref[1:2, :], mod_ref[0:1, :]).astype(BF16)
    p_ref[...] = _dot(h, w_ref[...])


def _proj_odd(x, mod_l, norm_g, w_in):
    d = D_MODEL
    width = w_in.shape[1]
    return pl.pallas_call(
        _proj_odd_kernel,
        grid=(N_TOK // TOK_TILE,),
        in_specs=[
            pl.BlockSpec((TOK_TILE, d), lambda i: (i, 0)),
            pl.BlockSpec((None, N_MOD, d), lambda i: (_tile_cond_row(i), 0, 0)),
            pl.BlockSpec((1, d), lambda i: (0, 0)),
            pl.BlockSpec((d, width), lambda i: (0, 0), pipeline_mode=pl.Buffered(1)),
        ],
        out_specs=pl.BlockSpec((TOK_TILE, width), lambda i: (i, 0)),
        out_shape=jax.ShapeDtypeStruct((N_TOK, width), F32),
        compiler_params=_cparams("parallel"),
        name="proj_odd",
    )(x, mod_l, norm_g.reshape(1, d), w_in.astype(BF16))


def _hgrn_chunks(chains):
    c = SCAN_CHUNK
    n = range(len(chains))
    rev, q, k, v, lf, st = zip(*chains)
    row = lax.broadcasted_iota(jnp.int32, (c, c), 0)
    col = lax.broadcasted_iota(jnp.int32, (c, c), 1)
    tris = {False: col <= row, True: col >= row}
    tris_b = {r: jnp.where(t, 1.0, 0.0).astype(BF16) for r, t in tris.items()}
    b = [_dot_w3(tris_b[rev[j]], lf[j]) for j in n]
    b2 = [b[j] * LOG2_E for j in n]
    lane_half = lax.broadcasted_iota(jnp.int32, (HALF, c), 1)
    chunk_row = lax.broadcasted_iota(jnp.int32, (c, 1), 0)
    rows = [[] for _ in n]
    for i in range(c // SUB):
        lo, hi = i * SUB, (i + 1) * SUB
        a_row = []
        for j in n:
            if rev[j]:
                has_off, edge, outside = hi < c, hi, chunk_row >= hi
            else:
                has_off, edge, outside = lo > 0, lo - 1, chunk_row < lo
            if has_off:
                beta = b[j][edge:edge + 1]
                qs = q[j][lo:hi] * jnp.exp(b[j][lo:hi] - beta)
                ks = k[j] * jnp.exp(jnp.where(outside, beta - b[j], -jnp.inf))
                a_row.append(_dot_nt(qs.astype(BF16), ks.astype(BF16)))
            else:
                a_row.append(jnp.zeros((SUB, c), F32))
        for half in range(SUB // HALF):
            h0 = lo + half * HALF
            piece = [a_row[j][half * HALF:(half + 1) * HALF] for j in n]
            for s in range(HALF):
                for j in n:
                    bh2 = b2[j][h0:h0 + HALF]
                    a_col = jnp.sum(jnp.exp2(bh2 - bh2[s:s + 1]) * q[j][h0:h0 + HALF] * k[j][h0 + s:h0 + s + 1],
                                    axis=-1, keepdims=True)
                    piece[j] = jnp.where(lane_half == h0 + s, a_col, piece[j])
            for j in n:
                rows[j].append(piece[j])
    second_half = (chunk_row % SUB) >= HALF
    same_block = (row // SUB) == (col // SUB)
    out = []
    for j in n:
        meet = HALF if rev[j] else HALF - 1
        beta = jnp.concatenate([jnp.broadcast_to(b[j][lo + meet:lo + meet + 1], (SUB, b[j].shape[1]))
                                for lo in range(0, c, SUB)], axis=0)
        t_side = jnp.logical_not(second_half) if rev[j] else second_half
        qs = q[j] * jnp.exp(jnp.where(t_side, b[j] - beta, -jnp.inf))
        ks = k[j] * jnp.exp(jnp.where(t_side, -jnp.inf, beta - b[j]))
        cross = jnp.where(same_block, _dot_nt(qs.astype(BF16), ks.astype(BF16)), 0.0)
        attn = jnp.where(tris[rev[j]], jnp.concatenate(rows[j], axis=0) + cross, 0.0)
        o = _dot(attn.astype(BF16), v[j].astype(BF16)) \
            + _dot_nt((q[j] * jnp.exp(b[j])).astype(BF16), st[j].astype(BF16))
        b_exit = b[j][0:1] if rev[j] else b[j][c - 1:c]
        k_out = k[j] * jnp.exp(b_exit - b[j])
        st_new = jnp.exp(b_exit) * st[j] + _dot_tn(v[j].astype(BF16), k_out.astype(BF16))
        out.append((o, st_new))
    return out


def _hgrn_kernel(seq_len, layer, q_ref, i_ref, g_ref, ff_ref, fb_ref, lb_ref, s0_ref,
                 o_ref, s_out_ref, of_ref, ob_ref, st_ref):
    c = SCAN_CHUNK
    hd = HEAD_DIM
    n_chunks = seq_len // c
    lbp = lb_ref[...]
    e = jnp.exp(lbp - jnp.max(lbp, axis=0, keepdims=True))
    lb = jnp.sum(e[0:layer + 1], axis=0, keepdims=True) / jnp.sum(e, axis=0, keepdims=True)

    for d in range(2):
        for hh in range(HG_STEP_HEADS):
            st_ref[d, hh] = s0_ref[d, hh].T

    def body(n, carry):
        where, chains = [], []
        for d in range(2):
            sl = pl.ds(pl.multiple_of((n_chunks - 1 - n if d else n) * c, c), c)
            for hh in range(HG_STEP_HEADS):
                cols = slice(hh * hd, (hh + 1) * hd)
                f = lb[:, cols] + (1.0 - lb[:, cols]) * _sigmoid((fb_ref if d else ff_ref)[sl, cols])
                where.append((d, hh, sl, cols))
                chains.append((bool(d), q_ref[sl, cols], 1.0 - f, i_ref[sl, cols], jnp.log(f), st_ref[d, hh]))
        for (d, hh, sl, cols), (o, st_new) in zip(where, _hgrn_chunks(chains)):
            st_ref[d, hh] = st_new
            (ob_ref if d else of_ref)[sl, cols] = o
        return carry

    lax.fori_loop(0, n_chunks, body, 0)
    for hh in range(HG_STEP_HEADS):
        cols = slice(hh * hd, (hh + 1) * hd)
        o = of_ref[:, cols] + ob_ref[:, cols]
        o = o * lax.rsqrt(jnp.mean(o * o, axis=-1, keepdims=True) + RMS_EPS)
        o_ref[:, cols] = o * _silu(g_ref[:, cols])
        for d in range(2):
            s_out_ref[d, hh] = st_ref[d, hh].T


def _hgrn(p, hg_lb, layer, s0, n_seq, seq_len, tok_offset):
    hd = HEAD_DIM
    sh = HG_STEP_HEADS
    steps_per_seq = HEADS // sh
    row0 = tok_offset // seq_len

    def col(part):
        return pl.BlockSpec((seq_len, sh * hd), lambda b, h: (row0 + b, part * steps_per_seq + h))

    state_spec = pl.BlockSpec((None, 2, sh, hd, hd), lambda b, h: (b, 0, h, 0, 0))
    return pl.pallas_call(
        functools.partial(_hgrn_kernel, seq_len, layer),
        grid=(n_seq, steps_per_seq),
        in_specs=[col(0), col(1), col(2), col(3), col(4),
                  pl.BlockSpec((DEPTH + 1, sh * hd), lambda b, h: (0, h)),
                  state_spec],
        out_specs=[pl.BlockSpec((seq_len, sh * hd), lambda b, h: (b, h)), state_spec],
        out_shape=[jax.ShapeDtypeStruct((n_seq * seq_len, GROUP_W), F32),
                   jax.ShapeDtypeStruct((n_seq, 2, HEADS, hd, hd), F32)],
        scratch_shapes=[pltpu.VMEM((seq_len, sh * hd), F32), pltpu.VMEM((seq_len, sh * hd), F32),
                        pltpu.VMEM((2, sh, hd, hd), F32)],
        compiler_params=_cparams("parallel", "parallel"),
        name=f"hgrn_l{seq_len}",
    )(p, p, p, p, p, hg_lb, s0)


def _short_conv3(x, w):
    n = x.shape[0]
    r = lax.broadcasted_iota(jnp.int32, (n, 1), 0)
    prev = jnp.where(r == 0, 0.0, pltpu.roll(x, 1, 0))
    nxt = jnp.where(r == n - 1, 0.0, pltpu.roll(x, n - 1, 0))
    return prev * w[0:1] + x * w[1:2] + nxt * w[2:3]


def _conv3_silu_tiles(dst_ref, src_ref, w_ref, scale):
    n, width = src_ref.shape
    c = SCAN_CHUNK
    r = lax.broadcasted_iota(jnp.int32, (c, 1), 0)
    for j in range(n // c):
        r0 = j * c
        for col in range(0, width, LANES):
            cols = slice(col, col + LANES)
            cur = src_ref[r0:r0 + c, cols]
            prev = jnp.where(r == 0, 0.0, pltpu.roll(cur, 1, 0)) if j == 0 else src_ref[r0 - 1:r0 + c - 1, cols]
            nxt = (jnp.where(r == c - 1, 0.0, pltpu.roll(cur, c - 1, 0)) if r0 + c == n
                   else src_ref[r0 + 1:r0 + c + 1, cols])
            w = w_ref[:, cols]
            y = _silu(prev * w[0:1] + cur * w[1:2] + nxt * w[2:3])
            dst_ref[r0:r0 + c, cols] = y if scale == 1.0 else y * scale


def _split2(x):
    hi = x.astype(BF16)
    return hi, (x - hi.astype(F32)).astype(BF16)


def _mlstm_chunks(chains):
    c = SCAN_CHUNK
    row = lax.broadcasted_iota(jnp.int32, (c, c), 0)
    col = lax.broadcasted_iota(jnp.int32, (c, c), 1)
    eye_b = jnp.where(row == col, 1.0, 0.0).astype(BF16)
    tris = {False: row <= col, True: row >= col}
    tris_b = {r: jnp.where(t, 1.0, 0.0).astype(BF16) for r, t in tris.items()}
    n = range(len(chains))
    rev, q, k, vt, ig, fg, ct, nv, m_prev = zip(*chains)

    def each(fn):
        return [fn(i) for i in n]

    def dot3(parts, rhs, nt=False):
        d = _dot_nt if nt else _dot
        return d(parts[0], rhs) + d(parts[1], rhs) + d(parts[2], rhs)

    lf = each(lambda i: _split3(jnp.broadcast_to(_log_sigmoid(fg[i]), (8, c))))
    b = each(lambda i: dot3(lf[i], tris_b[rev[i]])[0:1])
    us = each(lambda i: _split3(jnp.broadcast_to(ig[i] - b[i], (c, c))))
    u = each(lambda i: _dot_nt(eye_b, us[i][0]) + _dot_nt(eye_b, us[i][1]) + _dot_nt(eye_b, us[i][2]))
    dmat = each(lambda i: jnp.where(tris[rev[i]], b[i] + u[i], -jnp.inf))
    m_t = each(lambda i: jnp.maximum(b[i] + m_prev[i], jnp.max(dmat[i], axis=0, keepdims=True)))
    qb = each(lambda i: q[i].astype(BF16))
    kb = each(lambda i: k[i].astype(BF16))
    kq = each(lambda i: _dot_nt(kb[i], qb[i]))
    p = each(lambda i: jnp.exp(dmat[i] - m_t[i]) * kq[i])
    inter = each(lambda i: jnp.exp(b[i] + m_prev[i] - m_t[i]))
    ns = each(lambda i: _split2(jnp.broadcast_to(nv[i], (8, nv[i].shape[1]))))
    qs = each(lambda i: _split2(q[i]))
    qn = each(lambda i: (_dot_nt(ns[i][0], qs[i][0]) + _dot_nt(ns[i][1], qs[i][0])
                         + _dot_nt(ns[i][0], qs[i][1]))[0:1])
    den = each(lambda i: inter[i] * qn[i] + jnp.sum(p[i], axis=0, keepdims=True))
    scale = each(lambda i: 1.0 / jnp.maximum(jnp.abs(den[i]), jnp.exp(-m_t[i])))
    cq = each(lambda i: _dot_nt(ct[i].astype(BF16), qb[i]))
    vp = each(lambda i: _dot(vt[i].astype(BF16), p[i].astype(BF16)))
    ht = each(lambda i: (inter[i] * cq[i] + vp[i]) * scale[i])
    last = each(lambda i: 0 if rev[i] else c - 1)
    m_new = each(lambda i: m_t[i][:, last[i]:last[i] + 1])
    b_exit = each(lambda i: b[i][:, last[i]:last[i] + 1])
    w = each(lambda i: jnp.exp(b_exit[i] - b[i] + ig[i] - m_new[i]))
    dec = each(lambda i: jnp.exp(b_exit[i] + m_prev[i] - m_new[i]))
    vk = each(lambda i: _dot((vt[i] * w[i]).astype(BF16), kb[i]))
    ct_new = each(lambda i: dec[i] * ct[i] + vk[i])
    ws = each(lambda i: _split2(jnp.broadcast_to(w[i], (8, c))))
    ks = each(lambda i: _split2(k[i]))
    wk = each(lambda i: (_dot(ws[i][0], ks[i][0]) + _dot(ws[i][1], ks[i][0]) + _dot(ws[i][0], ks[i][1]))[0:1])
    nv_new = each(lambda i: dec[i] * nv[i] + wk[i])
    return [(ht[i], ct_new[i], nv_new[i], m_new[i]) for i in n]


def _mlstm_kernel(seq_len, q_ref, k_ref, v_ref, og_ref, gate_t_ref, cwq_ref, cwk_ref,
                  c0_ref, n0_ref, m0_ref,
                  o_ref, c_out_ref, n_out_ref, m_out_ref,
                  q2_ref, k2_ref, vt_ref, hf_ref, hb_ref, ct_ref, n_ref, m_ref):
    c = SCAN_CHUNK
    hd = HEAD_DIM
    n_chunks = seq_len // c
    _conv3_silu_tiles(q2_ref, q_ref, cwq_ref, 1.0)
    _conv3_silu_tiles(k2_ref, k_ref, cwk_ref, HEAD_DIM ** -0.5)
    for hh in range(HEADS):
        cols = slice(hh * hd, (hh + 1) * hd)
        for j in range(n_chunks):
            vt_ref[hh, :, j * c:(j + 1) * c] = v_ref[j * c:(j + 1) * c, cols].T
        for d in range(2):
            ct_ref[d, hh] = c0_ref[d, hh].T
    n_ref[...] = n0_ref[...]
    m_ref[...] = m0_ref[...]

    def body(n, carry):
        where, chains = [], []
        for d in range(2):
            sl = pl.ds(pl.multiple_of((n_chunks - 1 - n if d else n) * c, c), c)
            for hh in range(HEADS):
                cols = slice(hh * hd, (hh + 1) * hd)
                gr = gate_t_ref[hh, :, sl]
                where.append((d, hh, sl))
                chains.append((bool(d), q2_ref[sl, cols], k2_ref[sl, cols], vt_ref[hh, :, sl],
                               gr[2 * d:2 * d + 1, :], gr[2 * d + 1:2 * d + 2, :],
                               ct_ref[d, hh], n_ref[d, hh], m_ref[d, hh]))
        for (d, hh, sl), (ht, ct, nv, m_new) in zip(where, _mlstm_chunks(chains)):
            ct_ref[d, hh] = ct
            n_ref[d, hh] = nv
            m_ref[d, hh] = m_new
            (hb_ref if d else hf_ref)[hh, :, sl] = ht
        return carry

    lax.fori_loop(0, n_chunks, body, 0)
    for hh in range(HEADS):
        cols = slice(hh * hd, (hh + 1) * hd)
        for j in range(n_chunks):
            rows = slice(j * c, (j + 1) * c)
            h = (hf_ref[hh, :, rows] + hb_ref[hh, :, rows]).T
            h = h * lax.rsqrt(jnp.mean(h * h, axis=-1, keepdims=True) + RMS_EPS)
            o_ref[rows, cols] = h * _sigmoid(og_ref[rows, cols])
        for d in range(2):
            c_out_ref[d, hh] = ct_ref[d, hh].T
    n_out_ref[...] = n_ref[...]
    m_out_ref[...] = m_ref[...]


def _mlstm(p, gate_t_h, conv_w, c0, n0, m0, n_seq, seq_len, tok_offset):
    hd = HEAD_DIM
    gw = GROUP_W
    row0 = tok_offset // seq_len

    def col(part):
        return pl.BlockSpec((seq_len, gw), lambda b: (row0 + b, part))

    c_spec = pl.BlockSpec((None, 2, HEADS, hd, hd), lambda b: (b, 0, 0, 0, 0))
    n_spec = pl.BlockSpec((None, 2, HEADS, 1, hd), lambda b: (b, 0, 0, 0, 0))
    m_spec = pl.BlockSpec((None, 2, HEADS, 1, 1), lambda b: (b, 0, 0, 0, 0))
    return pl.pallas_call(
        functools.partial(_mlstm_kernel, seq_len),
        grid=(n_seq,),
        in_specs=[col(5), col(6), col(7), col(8),
                  pl.BlockSpec((HEADS, 4, seq_len), lambda b: (0, 0, row0 + b)),
                  pl.BlockSpec((3, gw), lambda b: (0, 0)),
                  pl.BlockSpec((3, gw), lambda b: (0, 1)),
                  c_spec, n_spec, m_spec],
        out_specs=[pl.BlockSpec((seq_len, gw), lambda b: (b, 0)), c_spec, n_spec, m_spec],
        out_shape=[jax.ShapeDtypeStruct((n_seq * seq_len, gw), F32),
                   jax.ShapeDtypeStruct((n_seq, 2, HEADS, hd, hd), F32),
                   jax.ShapeDtypeStruct((n_seq, 2, HEADS, 1, hd), F32),
                   jax.ShapeDtypeStruct((n_seq, 2, HEADS, 1, 1), F32)],
        scratch_shapes=[pltpu.VMEM((seq_len, gw), F32), pltpu.VMEM((seq_len, gw), F32),
                        pltpu.VMEM((HEADS, hd, seq_len), F32),
                        pltpu.VMEM((HEADS, hd, seq_len), F32), pltpu.VMEM((HEADS, hd, seq_len), F32),
                        pltpu.VMEM((2, HEADS, hd, hd), F32), pltpu.VMEM((2, HEADS, 1, hd), F32),
                        pltpu.VMEM((2, HEADS, 1, 1), F32)],
        compiler_params=_cparams("parallel"),
        name=f"mlstm_l{seq_len}",
    )(p, p, p, p, gate_t_h, conv_w, conv_w, c0, n0, m0)


def _dft_tables(seq_len):
    n = 2 * seq_len
    k = jnp.arange(seq_len, dtype=jnp.int32)[:, None]
    t = jnp.arange(seq_len, dtype=jnp.int32)[None, :]
    ang = ((k * t) % n).astype(F32) * (2.0 * np.pi / n)
    fc = jnp.cos(ang)
    fs = jnp.sin(ang)
    nyq = jnp.where(t % 2 == 0, 1.0, -1.0).astype(F32)
    fs = jnp.where(k == 0, nyq, fs)
    return jnp.concatenate([fc, fs], axis=0)


def _filter_kernel(seq_len, z_ref, w1_ref, b1_ref, w2_ref, b2_ref, w3f_ref, w3b_ref, b3f_ref, b3b_ref,
                   f0_ref, f1_ref, rf_ref, rb_ref, fhi_ref, flo_ref, kf_ref, a_ref):
    hp = lax.Precision.HIGHEST
    n = 2 * seq_len
    z = z_ref[...]

    @pl.when(jnp.logical_and(pl.program_id(0) == 0, pl.program_id(1) == 0))
    def _():
        a1 = jnp.sin(f0_ref[...] * (jnp.dot(z, w1_ref[...], precision=hp, preferred_element_type=F32)
                                    + b1_ref[...]))
        a_ref[...] = jnp.sin(f1_ref[...] * (jnp.dot(a1, w2_ref[...], precision=hp, preferred_element_type=F32)
                                            + b2_ref[...]))

    a = a_ref[...]
    t_norm = z[:, 0:1]
    hf = (jnp.dot(a, w3f_ref[...], precision=hp, preferred_element_type=F32) + b3f_ref[...]) \
        * jnp.exp(-t_norm * jnp.exp(rf_ref[...]))
    hb = (jnp.dot(a, w3b_ref[...], precision=hp, preferred_element_type=F32) + b3b_ref[...]) \
        * jnp.exp(-t_norm * jnp.exp(rb_ref[...]))
    inv = lax.rsqrt(jnp.sum(hf * hf, axis=0, keepdims=True) + jnp.sum(hb * hb, axis=0, keepdims=True))
    hf = hf * inv
    r = lax.broadcasted_iota(jnp.int32, (seq_len, 1), 0)
    hb = jnp.where(r == 0, 0.0, hb * inv)
    sh, sl = _split2(hf + hb)
    dh, dl = _split2(hf - hb)
    fhi = fhi_ref[...]
    flo = flo_ref[...]
    kc = _dot(fhi[:seq_len], sh) + _dot(fhi[:seq_len], sl) + _dot(flo[:seq_len], sh)
    ks = _dot(fhi[seq_len:], dh) + _dot(fhi[seq_len:], dl) + _dot(flo[seq_len:], dh)
    sign = jnp.where(r % 2 == 0, 1.0, -1.0)
    k_nyq = jnp.sum(sign * (hf + hb), axis=0, keepdims=True)
    ks = jnp.where(r == 0, k_nyq, ks)
    scale = jnp.where(r == 0, 1.0 / n, 2.0 / n)
    kf_ref[0:seq_len, :] = kc * scale
    kf_ref[seq_len:n, :] = ks * scale


def _hyena_filters(seq_len, w1, b1, w2, b2, w3, b3, freq, log_rate, f_tab):
    d = D_MODEL
    cb = 256
    t = jnp.arange(seq_len, dtype=F32)
    t_norm = t / (seq_len - 1)
    bands = jnp.linspace(1e-4, HY_BANDS - 1, HY_BANDS, dtype=F32)
    ang = (2.0 * np.pi / seq_len) * t[:, None] * bands[None, :]
    z = jnp.concatenate([t_norm[:, None], jnp.cos(ang), jnp.sin(ang)], axis=-1)
    kpad = 128 - HY_EMB
    z = jnp.pad(z, ((0, 0), (0, kpad)))
    w1p = jnp.pad(w1, ((0, kpad), (0, 0)))
    f_hi = f_tab.astype(BF16)
    f_lo = (f_tab - f_hi.astype(F32)).astype(BF16)
    n_cb = d // cb
    hh = HY_HIDDEN
    row = lambda a: a.reshape(1, -1)
    const = lambda shape: pl.BlockSpec(shape, lambda o, j: (0,) * len(shape))
    fwd = lambda rows: pl.BlockSpec((rows, cb), lambda o, j: (0, o * n_cb + j))
    bwd = lambda rows: pl.BlockSpec((rows, cb), lambda o, j: (0, (HY_ORDER + o) * n_cb + j))
    return pl.pallas_call(
        functools.partial(_filter_kernel, seq_len),
        grid=(HY_ORDER, n_cb),
        in_specs=[const((seq_len, 128)), const((128, hh)), const((1, hh)), const((hh, hh)), const((1, hh)),
                  fwd(hh), bwd(hh), fwd(1), bwd(1),
                  const((1, hh)), const((1, hh)), fwd(1), bwd(1),
                  const((2 * seq_len, seq_len)), const((2 * seq_len, seq_len))],
        out_specs=pl.BlockSpec((None, 2 * seq_len, cb), lambda o, j: (o, 0, j)),
        out_shape=jax.ShapeDtypeStruct((HY_ORDER, 2 * seq_len, d), F32),
        scratch_shapes=[pltpu.VMEM((seq_len, hh), F32)],
        compiler_params=_cparams("arbitrary", "arbitrary"),
        name=f"hyena_filter_l{seq_len}",
    )(z, w1p, row(b1), w2, row(b2), w3, w3, row(b3), row(b3),
      row(freq[0]), row(freq[1]), row(log_rate), row(log_rate), f_hi, f_lo)


def _hyena_kernel(seq_len, seqs, v_ref, x1_ref, x2_ref, cwv_ref, cw1_ref, cw2_ref, bias_ref, kf_ref,
                  f_ref, ft_ref, o_ref, z_ref, zb_ref, y_ref):
    kc = min(HY_FREQ_CHUNK, seq_len)
    n_k = seq_len // kc
    r = lax.broadcasted_iota(jnp.int32, (kc, 1), 0)
    gate_refs = ((x1_ref, cw1_ref), (x2_ref, cw2_ref))
    for s in range(seqs):
        rows = slice(s * seq_len, (s + 1) * seq_len)
        z_ref[s] = _short_conv3(v_ref[rows, :], cwv_ref[...])
    for o in range(HY_ORDER):
        for s in range(seqs):
            zb_ref[s] = z_ref[s].astype(BF16)
            y_ref[s] = jnp.zeros(y_ref.shape[1:], F32)

        def freq_chunk(j, carry):
            r0 = pl.multiple_of(j * kc, kc)
            k_cos = kf_ref[o, pl.ds(r0, kc), :]
            k_sin = kf_ref[o, pl.ds(seq_len + r0, kc), :]
            real_row = jnp.logical_and(r == 0, j == 0)
            for s in range(seqs):
                a = _dot(f_ref[pl.ds(r0, kc), :], zb_ref[s])
                bm = _dot(f_ref[pl.ds(seq_len + r0, kc), :], zb_ref[s])
                yc = a * k_cos - jnp.where(real_row, 0.0, bm * k_sin)
                ys = jnp.where(real_row, bm * k_sin, a * k_sin + bm * k_cos)
                y_ref[s] += _dot(ft_ref[j], yc.astype(BF16)) + _dot(ft_ref[n_k + j], ys.astype(BF16))
            return carry

        lax.fori_loop(0, n_k, freq_chunk, 0)
        x_ref, cw_ref = gate_refs[o]
        for s in range(seqs):
            rows = slice(s * seq_len, (s + 1) * seq_len)
            gate = _short_conv3(x_ref[rows, :], cw_ref[...])
            z_ref[s] = gate * (y_ref[s] + z_ref[s] * bias_ref[o:o + 1, :])
    for s in range(seqs):
        o_ref[s * seq_len:(s + 1) * seq_len, :] = z_ref[s]


def _hyena(u, conv_w, bias, kf, f_tab, n_seq, seq_len, tok_offset):
    d = D_MODEL
    cb = 256
    n_cb = d // cb
    seqs = max(1, HY_STEP_ROWS // seq_len)
    rows = seqs * seq_len
    row0 = tok_offset // rows
    kc = min(HY_FREQ_CHUNK, seq_len)
    n_k = seq_len // kc
    f_bf = f_tab.astype(BF16)
    ft = f_bf.T.reshape(seq_len, 2 * n_k, kc).transpose(1, 0, 2)

    def part(k):
        return pl.BlockSpec((rows, cb), lambda j, b: (row0 + b, k * n_cb + j))

    def cw(k):
        return pl.BlockSpec((3, cb), lambda j, b: (0, k * n_cb + j))

    return pl.pallas_call(
        functools.partial(_hyena_kernel, seq_len, seqs),
        grid=(n_cb, n_seq // seqs),
        in_specs=[part(0), part(1), part(2), cw(0), cw(1), cw(2),
                  pl.BlockSpec((HY_ORDER, cb), lambda j, b: (0, j)),
                  pl.BlockSpec((HY_ORDER, 2 * seq_len, cb), lambda j, b: (0, 0, j)),
                  pl.BlockSpec((2 * seq_len, seq_len), lambda j, b: (0, 0)),
                  pl.BlockSpec((2 * n_k, seq_len, kc), lambda j, b: (0, 0, 0))],
        out_specs=pl.BlockSpec((rows, cb), lambda j, b: (b, j)),
        out_shape=jax.ShapeDtypeStruct((n_seq * seq_len, d), F32),
        scratch_shapes=[pltpu.VMEM((seqs, seq_len, cb), F32), pltpu.VMEM((seqs, seq_len, cb), BF16),
                        pltpu.VMEM((seqs, seq_len, cb), F32)],
        compiler_params=_cparams("parallel", "parallel"),
        name=f"hyena_l{seq_len}",
    )(u, u, u, conv_w, conv_w, conv_w, bias, kf, f_bf, ft)


def _mix_out_kernel(n_parts, *refs):
    o_refs = refs[:2 * n_parts]
    w_refs = refs[2 * n_parts:3 * n_parts]
    x_ref, mod_ref, g_ref, wr_both_ref, br_ref = refs[3 * n_parts:3 * n_parts + 5]
    x1_ref, h_ref, eid_ref, wts_ref, rank_ref, cnt_ref, run_ref = refs[3 * n_parts + 5:]
    i = pl.program_id(0)
    t = ROUTE_TILE
    ne = N_EXPERTS

    from_prompt = i < N_PROMPT // ROUTE_TILE
    y = None
    for j in range(n_parts):
        o = jnp.where(from_prompt, o_refs[2 * j][...], o_refs[2 * j + 1][...])
        yj = _dot(o.astype(BF16), w_refs[j][...])
        y = yj if y is None else y + yj
    x1 = x_ref[...] + mod_ref[2:3, :] * y
    x1_ref[...] = x1
    h = _norm_mod(x1, g_ref[...], mod_ref[4:5, :], mod_ref[3:4, :])
    h_ref[...] = _pack_bf16_halves(h)
    h_hi = h.astype(BF16)
    h_lo = (h - h_hi.astype(F32)).astype(BF16)
    hi_terms = _dot(h_hi, wr_both_ref[...])
    logits = (hi_terms[:, :LANES] + hi_terms[:, LANES:] + _dot(h_lo, wr_both_ref[:, :LANES])).T[:ne] \
        + br_ref[...]

    @pl.when(i == 0)
    def _():
        run_ref[...] = jnp.zeros_like(run_ref)

    e_iota = lax.broadcasted_iota(jnp.int32, (ne, t), 0)
    vals, eids, onehots = [], [], []
    for _k in range(TOP_K):
        m = jnp.max(logits, axis=0, keepdims=True)
        eid = jnp.min(jnp.where(logits == m, e_iota, ne), axis=0, keepdims=True)
        sel = e_iota == eid
        logits = jnp.where(sel, -jnp.inf, logits)
        onehots.append(jnp.where(sel, 1.0, 0.0))
        vals.append(m)
        eids.append(eid)
    r2 = lax.broadcasted_iota(jnp.int32, (t, t), 0)
    c2 = lax.broadcasted_iota(jnp.int32, (t, t), 1)
    before = jnp.where(r2 < c2, 1.0, 0.0).astype(BF16)
    earlier = _dot(jnp.concatenate(onehots, axis=0).astype(BF16), before)
    running = run_ref[...]
    ranks = []
    for k, onehot in enumerate(onehots):
        ranks.append(jnp.sum(onehot * (running + earlier[k * ne:(k + 1) * ne]), axis=0, keepdims=True))
        running = running + jnp.sum(onehot, axis=1, keepdims=True)
    run_ref[...] = running
    cnt_ref[...] = running
    v = jnp.concatenate(vals, axis=0)
    ex = jnp.exp(v - v[0:1])
    wts_ref[...] = ex / jnp.sum(ex, axis=0, keepdims=True)
    eid_ref[...] = jnp.concatenate(eids, axis=0)
    rank_ref[...] = jnp.concatenate(ranks, axis=0).astype(jnp.int32)


def _mix_out(parts, x, mod_l, norm_g, w_router, b_router):
    d = D_MODEL
    t = ROUTE_TILE
    ne = N_EXPERTS
    n_parts = len(parts)
    n_prompt_tiles = N_PROMPT // t
    wr = jnp.pad(w_router, ((0, 0), (0, LANES - ne)))
    wr_hi = wr.astype(BF16)
    wr_both = jnp.concatenate([wr_hi, (wr - wr_hi.astype(F32)).astype(BF16)], axis=1)
    in_specs = []
    for o_p, _, _ in parts:
        in_specs.append(pl.BlockSpec((t, o_p.shape[1]), lambda i: (jnp.minimum(i, n_prompt_tiles - 1), 0)))
        in_specs.append(pl.BlockSpec((t, o_p.shape[1]), lambda i: (jnp.maximum(i - n_prompt_tiles, 0), 0)))
    in_specs += [pl.BlockSpec(w.shape, lambda i: (0, 0)) for _, _, w in parts]
    in_specs += [
        pl.BlockSpec((t, d), lambda i: (i, 0)),
        pl.BlockSpec((None, N_MOD, d), lambda i: (_tile_cond_row(i, t), 0, 0)),
        pl.BlockSpec((1, d), lambda i: (0, 0)),
        pl.BlockSpec((d, 2 * LANES), lambda i: (0, 0)),
        pl.BlockSpec((ne, 1), lambda i: (0, 0)),
    ]
    tok_major = pl.BlockSpec((TOP_K, t), lambda i: (0, i))
    return pl.pallas_call(
        functools.partial(_mix_out_kernel, n_parts),
        grid=(N_TOK // t,),
        in_specs=in_specs,
        out_specs=[pl.BlockSpec((t, d), lambda i: (i, 0)), pl.BlockSpec((t, d // 2), lambda i: (i, 0)),
                   tok_major, tok_major, tok_major, pl.BlockSpec((ne, 1), lambda i: (0, 0))],
        out_shape=[jax.ShapeDtypeStruct((N_TOK, d), F32), jax.ShapeDtypeStruct((N_TOK, d // 2), jnp.uint32),
                   jax.ShapeDtypeStruct((TOP_K, N_TOK), jnp.int32), jax.ShapeDtypeStruct((TOP_K, N_TOK), F32),
                   jax.ShapeDtypeStruct((TOP_K, N_TOK), jnp.int32), jax.ShapeDtypeStruct((ne, 1), F32)],
        scratch_shapes=[pltpu.VMEM((ne, 1), F32)],
        compiler_params=_cparams("arbitrary"),
        name="mix_out_router",
    )(*[o for part in parts for o in part[:2]], *[w.astype(BF16) for _, _, w in parts], x, mod_l,
      norm_g.reshape(1, d), wr_both, b_router.reshape(ne, 1))


def _sc_row_gather(src, idx):
    n = idx.shape[0]
    width = src.shape[1]
    step_rows = min(SC_MAX_INDICES, SC_STEP_BYTES // (width * 4))
    sc = plsc.get_sparse_core_info()
    n_workers = sc.num_cores * sc.num_subcores
    per_worker = n // n_workers
    if n % n_workers or per_worker % step_rows:
        raise ValueError("row count must be whole SparseCore steps on every subcore")
    mesh = plsc.VectorSubcoreMesh(core_axis_name="core", subcore_axis_name="subcore")

    @functools.partial(pl.kernel, out_type=jax.ShapeDtypeStruct((n, width), src.dtype), mesh=mesh,
                       scratch_types=[pltpu.VMEM((per_worker,), jnp.int32),
                                      pltpu.VMEM((step_rows, width), src.dtype)],
                       name="sc_row_gather")
    def gather(src_hbm, idx_hbm, dst_hbm, idx_vmem, rows_vmem):
        worker = lax.axis_index("subcore") * sc.num_cores + lax.axis_index("core")
        base = worker * per_worker
        pltpu.sync_copy(idx_hbm.at[pl.ds(base, per_worker)], idx_vmem)

        @pl.loop(0, per_worker // step_rows)
        def _(c):
            pltpu.sync_copy(src_hbm.at[idx_vmem.at[pl.ds(c * step_rows, step_rows)]], rows_vmem)
            pltpu.sync_copy(rows_vmem, dst_hbm.at[pl.ds(base + c * step_rows, step_rows)])

    return gather(src, idx)


def _sc_row_scatter(src, row_of_slot):
    n = row_of_slot.shape[0]
    n_src, width = src.shape
    step_rows = min(SC_MAX_INDICES, SC_STEP_BYTES // (width * 4))
    sc = plsc.get_sparse_core_info()
    n_workers = sc.num_cores * sc.num_subcores
    per_worker = n // n_workers
    steps = per_worker // step_rows
    if n % n_workers or per_worker % step_rows or n_src % step_rows:
        raise ValueError("row counts must be whole SparseCore steps on every subcore")
    mesh = plsc.VectorSubcoreMesh(core_axis_name="core", subcore_axis_name="subcore")

    @functools.partial(pl.kernel, out_type=jax.ShapeDtypeStruct((n, width), src.dtype), mesh=mesh,
                       scratch_types=[pltpu.VMEM((steps, step_rows), jnp.int32),
                                      pltpu.VMEM((step_rows, width), src.dtype)],
                       name="sc_row_scatter")
    def scatter(src_hbm, idx_hbm, dst_hbm, idx_vmem, rows_vmem):
        worker = lax.axis_index("subcore") * sc.num_cores + lax.axis_index("core")
        pltpu.sync_copy(idx_hbm.at[worker], idx_vmem)

        @pl.loop(0, steps)
        def _(c):
            src0 = lax.rem(worker * per_worker + c * step_rows, n_src)
            pltpu.sync_copy(src_hbm.at[pl.ds(src0, step_rows)], rows_vmem)
            pltpu.sync_copy(rows_vmem, dst_hbm.at[idx_vmem.at[c]])

    return scatter(src, row_of_slot.reshape(n_workers, steps, step_rows))


def _pack_bf16_halves(x):
    w = x.shape[1] // 2
    bits = pltpu.bitcast(x.astype(BF16).astype(F32), jnp.uint32)
    return bits[:, :w] | (bits[:, w:] >> 16)


def _unpack_bf16_halves(p):
    hi = pltpu.bitcast(p & jnp.uint32(0xFFFF0000), F32).astype(BF16)
    lo = pltpu.bitcast(p << 16, F32).astype(BF16)
    return hi, lo


def _experts_kernel(layer, te_ref, first_ref, slot_ref, next_ref, rows_ref, nv_ref,
                    x_ref, wgu_hbm, bg_ref, bu_ref, wd_hbm, bd_ref, sel_ref,
                    y_ref, wgu_buf, wd_buf, wg_ref, wu_ref, wdb_ref, sem):
    i = pl.program_id(0)
    valid = i < nv_ref[0]
    half = DEINT_COLS // 2
    k_half = x_ref.shape[1]

    def fetch(expert, slot):
        return (pltpu.make_async_copy(wgu_hbm.at[layer, expert], wgu_buf.at[slot], sem.at[slot, 0]),
                pltpu.make_async_copy(wd_hbm.at[layer, expert], wd_buf.at[slot], sem.at[slot, 1]))

    @pl.when(i == 0)
    def _():
        for cp in fetch(te_ref[0], 0):
            cp.start()

    @pl.when(jnp.logical_and(valid, first_ref[i] == 1))
    def _():
        slot = slot_ref[i]
        for cp in fetch(te_ref[i], slot):
            cp.wait()

        @pl.when(next_ref[i] >= 0)
        def _():
            for cp in fetch(next_ref[i], 1 - slot):
                cp.start(priority=1)

        for c in range(wgu_buf.shape[2] // DEINT_COLS):
            w = wgu_buf[slot, :, c * DEINT_COLS:(c + 1) * DEINT_COLS].astype(BF16)
            split = _dot(w, sel_ref[...]).astype(BF16)
            wg_ref[:, c * half:(c + 1) * half] = split[:, :half]
            wu_ref[:, c * half:(c + 1) * half] = split[:, half:]
        wdb_ref[...] = wd_buf[slot].astype(BF16)

    def ffn(rows):
        x_l, x_r = _unpack_bf16_halves(x_ref[:rows, :])
        gl = _dot(x_l, wg_ref[:k_half, :]) + _dot(x_r, wg_ref[k_half:, :]) + bg_ref[...]
        up = _dot(x_l, wu_ref[:k_half, :]) + _dot(x_r, wu_ref[k_half:, :]) + bu_ref[...]
        gl = jnp.minimum(gl, SWIGLU_LIMIT)
        up = jnp.clip(up, -SWIGLU_LIMIT, SWIGLU_LIMIT)
        act = (up + 1.0) * gl * _sigmoid(SWIGLU_ALPHA * gl)
        y_ref[:rows, :] = _pack_bf16_halves(_dot(act.astype(BF16), wdb_ref[...]) + bd_ref[...])

    tm = x_ref.shape[0]
    few = rows_ref[i] <= tm // 2

    @pl.when(jnp.logical_and(valid, jnp.logical_not(few)))
    def _():
        ffn(tm)

    @pl.when(jnp.logical_and(valid, few))
    def _():
        ffn(tm // 2)
        y_ref[tm // 2:, :] = jnp.zeros((tm - tm // 2, y_ref.shape[1]), y_ref.dtype)

    @pl.when(jnp.logical_not(valid))
    def _():
        y_ref[...] = jnp.zeros_like(y_ref)


def _experts(xs, tile_expert, tile_first, tile_rows, n_valid, layer, w_gu, b_gate, b_up, w_down, b_down):
    d = D_MODEL
    tm = MOE_TILE
    n_tiles = MOE_ROWS // tm
    ff = w_down.shape[2]
    half = DEINT_COLS // 2
    r = jnp.arange(DEINT_COLS)[:, None]
    c = jnp.arange(DEINT_COLS)[None, :]
    sel = (r == jnp.where(c < half, 2 * c, 2 * (c - half) + 1)).astype(BF16)
    group = jnp.cumsum(tile_first) - 1
    tile_slot = (group % 2).astype(jnp.int32)
    is_last_group = group == group[-1]
    following = jnp.concatenate([tile_expert[1:], tile_expert[-1:]])
    idx = jnp.arange(n_tiles, dtype=jnp.int32)
    group_end = jnp.max(jnp.where(group[None, :] == group[:, None], idx[None, :], -1), axis=1)
    tile_next = jnp.where(is_last_group, -1, following[group_end]).astype(jnp.int32)
    wspec = lambda k, n: pl.BlockSpec((None, None, k, n), lambda i, *_: (layer, _[0][i], 0, 0))
    grid_spec = pltpu.PrefetchScalarGridSpec(
        num_scalar_prefetch=6,
        grid=(n_tiles,),
        in_specs=[pl.BlockSpec((tm, d // 2), lambda i, *_: (i, 0)),
                  pl.BlockSpec(memory_space=pl.ANY), wspec(1, ff), wspec(1, ff),
                  pl.BlockSpec(memory_space=pl.ANY), wspec(1, d),
                  pl.BlockSpec((DEINT_COLS, DEINT_COLS), lambda i, *_: (0, 0))],
        out_specs=pl.BlockSpec((tm, d // 2), lambda i, *_: (i, 0)),
        scratch_shapes=[pltpu.VMEM((2, d, 2 * ff), F32), pltpu.VMEM((2, ff, d), F32),
                        pltpu.VMEM((d, ff), BF16), pltpu.VMEM((d, ff), BF16), pltpu.VMEM((ff, d), BF16),
                        pltpu.SemaphoreType.DMA((2, 2))],
    )
    return pl.pallas_call(
        functools.partial(_experts_kernel, layer),
        grid_spec=grid_spec,
        out_shape=jax.ShapeDtypeStruct((MOE_ROWS, d // 2), jnp.uint32),
        compiler_params=_cparams("arbitrary"),
        name="experts",
    )(tile_expert, tile_first, tile_slot, tile_next, tile_rows, n_valid, xs, w_gu, b_gate, b_up, w_down,
      b_down, sel)


def _combine_kernel(final, x_ref, g_ref, w_ref, mod_ref, fg_ref, *o_refs):
    y_l = y_r = None
    for k in range(TOP_K):
        g_l, g_r = _unpack_bf16_halves(g_ref[k])
        wk = w_ref[:, k:k + 1]
        y_l = wk * g_l.astype(F32) if y_l is None else y_l + wk * g_l.astype(F32)
        y_r = wk * g_r.astype(F32) if y_r is None else y_r + wk * g_r.astype(F32)
    x = x_ref[...] + mod_ref[5:6, :] * jnp.concatenate([y_l, y_r], axis=1)
    if not final:
        o_refs[0][...] = x
        return
    x = x * lax.rsqrt(jnp.mean(x * x, axis=-1, keepdims=True) + RMS_EPS) * fg_ref[...]
    from_prompt = pl.program_id(0) < N_PROMPT // ROUTE_TILE

    @pl.when(from_prompt)
    def _():
        o_refs[0][...] = x

    @pl.when(jnp.logical_not(from_prompt))
    def _():
        o_refs[1][...] = x


def _combine(x1, gathered, wts, mod_l, final_g, final):
    d = D_MODEL
    t = ROUTE_TILE
    n_prompt_tiles = N_PROMPT // t
    if final:
        out_specs = [pl.BlockSpec((t, d), lambda i: (jnp.minimum(i, n_prompt_tiles - 1), 0)),
                     pl.BlockSpec((t, d), lambda i: (jnp.maximum(i - n_prompt_tiles, 0), 0))]
        out_shape = [jax.ShapeDtypeStruct((N_PROMPT, d), F32), jax.ShapeDtypeStruct((N_SAMPLE, d), F32)]
    else:
        out_specs = pl.BlockSpec((t, d), lambda i: (i, 0))
        out_shape = jax.ShapeDtypeStruct((N_TOK, d), F32)
    return pl.pallas_call(
        functools.partial(_combine_kernel, final),
        grid=(N_TOK // t,),
        in_specs=[pl.BlockSpec((t, d), lambda i: (i, 0)),
                  pl.BlockSpec((TOP_K, t, d // 2), lambda i: (0, i, 0)),
                  pl.BlockSpec((t, TOP_K), lambda i: (i, 0)),
                  pl.BlockSpec((None, N_MOD, d), lambda i: (_tile_cond_row(i, t), 0, 0)),
                  pl.BlockSpec((1, d), lambda i: (0, 0))],
        out_specs=out_specs,
        out_shape=out_shape,
        compiler_params=_cparams("arbitrary"),
        name="moe_combine",
    )(x1, gathered, wts, mod_l, final_g.reshape(1, d))


def _moe(x1, h, eid, wts, rank, counts, mod_l, layer, w_gu, b_gu, w_down, b_down, final_g, final):
    d = D_MODEL
    tm = MOE_TILE
    n_tiles = MOE_ROWS // tm
    cnt = counts.reshape(N_EXPERTS).astype(jnp.int32)
    gsz = ((cnt + tm - 1) // tm) * tm
    ends = jnp.cumsum(gsz)
    offs = ends - gsz
    e_ids = jnp.arange(N_EXPERTS, dtype=jnp.int32)
    pos = jnp.sum(jnp.where(eid[..., None] == e_ids, offs, 0), axis=-1) + rank
    tile_start = jnp.arange(n_tiles, dtype=jnp.int32) * tm
    tile_expert = jnp.minimum(jnp.sum((ends[None, :] <= tile_start[:, None]).astype(jnp.int32), axis=1),
                              N_EXPERTS - 1)
    n_valid = (ends[-1:] // tm).astype(jnp.int32)
    last_valid = jnp.maximum(n_valid[0] - 1, 0)
    tile_expert = jnp.where(jnp.arange(n_tiles) < n_valid[0], tile_expert, tile_expert[last_valid])
    tile_first = jnp.concatenate([jnp.ones((1,), jnp.int32),
                                  (tile_expert[1:] != tile_expert[:-1]).astype(jnp.int32)])
    j = jnp.arange(tm, dtype=jnp.int32)[None, :]
    pad_used = j < (gsz - cnt)[:, None]
    n_unused_before = jnp.cumsum((~pad_used).reshape(-1).astype(jnp.int32)) - 1
    pad_pos = jnp.where(pad_used, (offs + cnt)[:, None] + j,
                        ends[-1] + n_unused_before.reshape(N_EXPERTS, tm))
    row_of_slot = jnp.concatenate([pos.reshape(-1), pad_pos.reshape(-1)])

    xs = _sc_row_scatter(h, row_of_slot)
    real_end = (offs + cnt)[tile_expert]
    tile_rows = jnp.clip(real_end - tile_start, 0, tm).astype(jnp.int32)
    ys = _experts(xs, tile_expert, tile_first, tile_rows, n_valid, layer, w_gu,
                  b_gu[:, :, None, 0::2], b_gu[:, :, None, 1::2], w_down, b_down[:, :, None, :])
    gathered = _sc_row_gather(ys, pos.reshape(-1))
    return _combine(x1, gathered.reshape(TOP_K, N_TOK, d // 2), wts.T, mod_l, final_g, final)


def _grid_positions(n_tok, d):
    rows = n_tok // GRID_W
    r, col = jnp.meshgrid(jnp.arange(rows, dtype=F32), jnp.arange(GRID_W, dtype=F32), indexing='ij')
    r = r.reshape(-1)
    col = col.reshape(-1)
    quarter = d // 4
    inv = 1.0 / (10000.0 ** (jnp.arange(quarter, dtype=F32) / quarter))
    ar = r[:, None] * inv[None]
    ac = col[:, None] * inv[None]
    return jnp.concatenate([jnp.sin(ar), jnp.cos(ar), jnp.sin(ac), jnp.cos(ac)], axis=-1)


def kernel(x_prompt, x_sample, state_hgrn, state_mlstm_c, state_mlstm_n, state_mlstm_m, c, c_ctx,
           norm_g, final_g, w_mod, b_mod, ev_w_in, ev_gate_b, ev_conv, hg_lb, ev_w_out,
           hy_w_in, hy_conv, hy_w1, hy_b1, hy_w2, hy_b2, hy_w3, hy_b3, hy_freq, hy_log_rate, hy_bias, hy_w_out,
           w_router, b_router, w_gu, b_gu, w_down, b_down):
    d = D_MODEL
    hd = HEAD_DIM
    cond = jnp.concatenate([c_ctx[None], c, jnp.zeros((N_COND - 1 - DEC_BATCH, d), F32)], axis=0)
    mod = _modulation(cond, w_mod, b_mod)
    pos_tab = jnp.concatenate([jnp.zeros((TOK_TILE, d), F32), _grid_positions(DEC_SEQ, d)], axis=0)

    groups = ((BATCH, SEQ, 0), (DEC_BATCH, DEC_SEQ, N_PROMPT))
    new_states = None
    for l in range(DEPTH):
        if l % 2 == 0:
            e = l // 2
            if l == 0:
                x, p, gate_t = _proj_even(x_prompt.reshape(N_PROMPT, d), x_sample.reshape(N_SAMPLE, d),
                                          pos_tab, mod[l], norm_g[l, 0], ev_w_in[e], ev_gate_b[e])
            else:
                raise NotImplementedError("only the first layer adds grid positions")
            gate_t_h = gate_t.reshape(4, HEADS, N_TOK).transpose(1, 0, 2)
            o_hg, o_ml = [], []
            for gi, (n_seq, seq_len, off) in enumerate(groups):
                if gi == 0:
                    s0 = jnp.zeros((n_seq, 2, HEADS, hd, hd), F32)
                    c0 = jnp.zeros((n_seq, 2, HEADS, hd, hd), F32)
                    n0 = jnp.zeros((n_seq, 2, HEADS, 1, hd), F32)
                    m0 = jnp.zeros((n_seq, 2, HEADS, 1, 1), F32)
                else:
                    s0 = state_hgrn[:, e]
                    c0 = state_mlstm_c[:, e]
                    n0 = state_mlstm_n[:, e].reshape(n_seq, 2, HEADS, 1, hd)
                    m0 = state_mlstm_m[:, e].reshape(n_seq, 2, HEADS, 1, 1)
                og, s_fin = _hgrn(p, hg_lb, l, s0, n_seq, seq_len, off)
                om, c_fin, n_fin, m_fin = _mlstm(p, gate_t_h, ev_conv[e], c0, n0, m0, n_seq, seq_len, off)
                o_hg.append(og)
                o_ml.append(om)
                if gi == 0:
                    new_states = (s_fin[:, None], c_fin[:, None],
                                  n_fin.reshape(n_seq, 1, 2, HEADS, hd), m_fin.reshape(n_seq, 1, 2, HEADS))
            parts = [(*o_hg, ev_w_out[e][:GROUP_W]), (*o_ml, ev_w_out[e][GROUP_W:])]
        else:
            o = l // 2
            u = _proj_odd(x, mod[l], norm_g[l, 0], hy_w_in[o])
            zs = []
            for n_seq, seq_len, off in groups:
                f_tab = _dft_tables(seq_len)
                kf = _hyena_filters(seq_len, hy_w1[o], hy_b1[o], hy_w2[o], hy_b2[o], hy_w3[o], hy_b3[o],
                                    hy_freq[o], hy_log_rate[o], f_tab)
                zs.append(_hyena(u, hy_conv[o], hy_bias[o], kf, f_tab, n_seq, seq_len, off))
            parts = [(*zs, hy_w_out[o])]
        x1, h, eid, wts, rank, counts = _mix_out(parts, x, mod[l], norm_g[l, 1], w_router[l], b_router[l])
        x = _moe(x1, h, eid, wts, rank, counts, mod[l], l, w_gu, b_gu, w_down, b_down,
                 final_g, final=(l == DEPTH - 1))

    y_prompt, y_sample = x
    return (y_prompt.reshape(BATCH, SEQ, d), y_sample.reshape(DEC_BATCH, DEC_SEQ, d)) + new_states
```

```python
import functools

import numpy as np
import jax
import jax.numpy as jnp
from jax import lax
from jax.experimental import pallas as pl
from jax.experimental.pallas import tpu as pltpu
from jax.experimental.pallas import tpu_sc as plsc

F32 = jnp.float32
BF16 = jnp.bfloat16

D_MODEL = 1024
BATCH = 32
SEQ = 256
DEPTH = 2
DEC_BATCH = 8
DEC_SEQ = 1024
GRID_W = 64
RMS_EPS = 1e-6
N_MOD = 6
LOG2_E = 1.4426950408889634
LANES = 128

HEADS = 4
HEAD_DIM = 128
GROUP_W = HEADS * HEAD_DIM
N_GATES = 4 * HEADS
EVEN_MAIN = 9 * GROUP_W

HY_ORDER = 2
HY_BANDS = 16
HY_EMB = 1 + 2 * HY_BANDS
HY_HIDDEN = 64
HY_FREQ_CHUNK = 512
HY_STEP_ROWS = 1024

N_EXPERTS = 32
TOP_K = 4
SWIGLU_LIMIT = 7.0
SWIGLU_ALPHA = 1.702

N_PROMPT = BATCH * SEQ
N_SAMPLE = DEC_BATCH * DEC_SEQ
N_TOK = N_PROMPT + N_SAMPLE
N_COND = 16

TOK_TILE = 512
ROUTE_TILE = 512
SCAN_CHUNK = 128
SUB = 16
HALF = 8
HG_STEP_HEADS = 4
MOE_TILE = 512
MOE_ROWS = N_TOK * TOP_K + N_EXPERTS * MOE_TILE
DEINT_COLS = 256
SC_STEP_BYTES = 256 * 1024
SC_MAX_INDICES = 128

VMEM_LIMIT = 56 * 1024 * 1024


def _cparams(*sem):
    return pltpu.CompilerParams(dimension_semantics=sem, vmem_limit_bytes=VMEM_LIMIT)


def _split3(x):
    hi = x.astype(BF16)
    r = x - hi.astype(F32)
    mid = r.astype(BF16)
    lo = (r - mid.astype(F32)).astype(BF16)
    return hi, mid, lo


def _dot(a, b):
    return jnp.dot(a, b, preferred_element_type=F32)


def _dot_nt(a, b):
    return lax.dot_general(a, b, (((1,), (1,)), ((), ())), preferred_element_type=F32)


def _dot_tn(a, b):
    return lax.dot_general(a, b, (((0,), (0,)), ((), ())), preferred_element_type=F32)


def _dot_w3(a_exact_bf16, x):
    hi, mid, lo = _split3(x)
    return _dot(a_exact_bf16, hi) + _dot(a_exact_bf16, mid) + _dot(a_exact_bf16, lo)


def _sigmoid(x):
    return 1.0 / (1.0 + jnp.exp(-x))


def _silu(x):
    return x * _sigmoid(x)


def _log_sigmoid(x):
    return jnp.minimum(x, 0.0) - jnp.log(1.0 + jnp.exp(-jnp.abs(x)))


def _tile_cond_row(i, tile=TOK_TILE):
    n_prompt_tiles = N_PROMPT // tile
    tiles_per_seq = DEC_SEQ // tile
    return jnp.where(i < n_prompt_tiles, 0, 1 + (i - n_prompt_tiles) // tiles_per_seq)


def _mod_kernel(cond_ref, w_ref, b_ref, o_ref):
    a = _silu(cond_ref[...]).astype(BF16)
    o_ref[...] = _dot(a, w_ref[...].astype(BF16)) + b_ref[...]


def _modulation(cond, w_mod, b_mod):
    d = D_MODEL
    out = pl.pallas_call(
        _mod_kernel,
        grid=(DEPTH, N_MOD),
        in_specs=[
            pl.BlockSpec((N_COND, d), lambda l, j: (0, 0)),
            pl.BlockSpec((None, d, d), lambda l, j: (l, 0, j)),
            pl.BlockSpec((None, 1, d), lambda l, j: (l, 0, j)),
        ],
        out_specs=pl.BlockSpec((None, None, N_COND, d), lambda l, j: (l, j, 0, 0)),
        out_shape=jax.ShapeDtypeStruct((DEPTH, N_MOD, N_COND, d), F32),
        compiler_params=_cparams("parallel", "parallel"),
        name="modulation",
    )(cond, w_mod, b_mod.reshape(DEPTH, 1, N_MOD * d))
    return out.transpose(0, 2, 1, 3)


def _norm_mod(x, g_row, scale_row, shift_row):
    ms = jnp.mean(x * x, axis=-1, keepdims=True)
    y = x * lax.rsqrt(ms + RMS_EPS) * g_row
    return y * (1.0 + scale_row) + shift_row


def _proj_even_kernel(xp_ref, xs_ref, pos_ref, mod_ref, g_ref, w_ref, wgt_ref, gbt_ref,
                      xres_ref, p_ref, gate_t_ref):
    from_prompt = pl.program_id(0) < N_PROMPT // TOK_TILE
    x = jnp.where(from_prompt, xp_ref[...], xs_ref[...]) + pos_ref[...]
    xres_ref[...] = x
    h = _norm_mod(x, g_ref[...], mod_ref[1:2, :], mod_ref[0:1, :]).astype(BF16)
    p_ref[...] = _dot(h, w_ref[...])
    gate_t_ref[...] = _dot_nt(wgt_ref[...], h) + gbt_ref[...]


def _proj_even(x_prompt, x_sample, pos_tab, mod_l, norm_g, w_in, gate_b):
    d = D_MODEL
    n_tiles = N_TOK // TOK_TILE
    n_prompt_tiles = N_PROMPT // TOK_TILE
    tiles_per_seq = DEC_SEQ // TOK_TILE
    w_main = w_in[:, :EVEN_MAIN].astype(BF16)
    w_gate = w_in[:, EVEN_MAIN:].astype(BF16)

    def pos_map(i):
        return (jnp.where(i < n_prompt_tiles, 0, 1 + (i - n_prompt_tiles) % tiles_per_seq), 0)

    return pl.pallas_call(
        _proj_even_kernel,
        grid=(n_tiles,),
        in_specs=[
            pl.BlockSpec((TOK_TILE, d), lambda i: (jnp.minimum(i, n_prompt_tiles - 1), 0)),
            pl.BlockSpec((TOK_TILE, d), lambda i: (jnp.maximum(i - n_prompt_tiles, 0), 0)),
            pl.BlockSpec((TOK_TILE, d), pos_map),
            pl.BlockSpec((None, N_MOD, d), lambda i: (_tile_cond_row(i), 0, 0)),
            pl.BlockSpec((1, d), lambda i: (0, 0)),
            pl.BlockSpec((d, EVEN_MAIN), lambda i: (0, 0), pipeline_mode=pl.Buffered(1)),
            pl.BlockSpec((N_GATES, d), lambda i: (0, 0)),
            pl.BlockSpec((N_GATES, 1), lambda i: (0, 0)),
        ],
        out_specs=[
            pl.BlockSpec((TOK_TILE, d), lambda i: (i, 0)),
            pl.BlockSpec((TOK_TILE, EVEN_MAIN), lambda i: (i, 0)),
            pl.BlockSpec((N_GATES, TOK_TILE), lambda i: (0, i)),
        ],
        out_shape=[
            jax.ShapeDtypeStruct((N_TOK, d), F32),
            jax.ShapeDtypeStruct((N_TOK, EVEN_MAIN), F32),
            jax.ShapeDtypeStruct((N_GATES, N_TOK), F32),
        ],
        compiler_params=_cparams("parallel"),
        name="proj_even",
    )(x_prompt, x_sample, pos_tab, mod_l, norm_g.reshape(1, d), w_main, w_gate.T, gate_b.reshape(N_GATES, 1))


def _proj_odd_kernel(x_ref, mod_ref, g_ref, w_ref, p_ref):
    h = _norm_mod(x_ref[...], g_ref[...], mod_ref[1:2, :], mod_ref[0:1, :]).astype(BF16)
    p_ref[...] = _dot(h, w_ref[...])


def _proj_odd(x, mod_l, norm_g, w_in):
    d = D_MODEL
    width = w_in.shape[1]
    return pl.pallas_call(
        _proj_odd_kernel,
        grid=(N_TOK // TOK_TILE,),
        in_specs=[
            pl.BlockSpec((TOK_TILE, d), lambda i: (i, 0)),
            pl.BlockSpec((None, N_MOD, d), lambda i: (_tile_cond_row(i), 0, 0)),
            pl.BlockSpec((1, d), lambda i: (0, 0)),
            pl.BlockSpec((d, width), lambda i: (0, 0), pipeline_mode=pl.Buffered(1)),
        ],
        out_specs=pl.BlockSpec((TOK_TILE, width), lambda i: (i, 0)),
        out_shape=jax.ShapeDtypeStruct((N_TOK, width), F32),
        compiler_params=_cparams("parallel"),
        name="proj_odd",
    )(x, mod_l, norm_g.reshape(1, d), w_in.astype(BF16))


def _hgrn_chunks(chains):
    c = SCAN_CHUNK
    n = range(len(chains))
    rev, q, k, v, lf, st = zip(*chains)
    row = lax.broadcasted_iota(jnp.int32, (c, c), 0)
    col = lax.broadcasted_iota(jnp.int32, (c, c), 1)
    tris = {False: col <= row, True: col >= row}
    tris_b = {r: jnp.where(t, 1.0, 0.0).astype(BF16) for r, t in tris.items()}
    b = [_dot_w3(tris_b[rev[j]], lf[j]) for j in n]
    b2 = [b[j] * LOG2_E for j in n]
    lane_half = lax.broadcasted_iota(jnp.int32, (HALF, c), 1)
    chunk_row = lax.broadcasted_iota(jnp.int32, (c, 1), 0)
    rows = [[] for _ in n]
    for i in range(c // SUB):
        lo, hi = i * SUB, (i + 1) * SUB
        a_row = []
        for j in n:
            if rev[j]:
                has_off, edge, outside = hi < c, hi, chunk_row >= hi
            else:
                has_off, edge, outside = lo > 0, lo - 1, chunk_row < lo
            if has_off:
                beta = b[j][edge:edge + 1]
                qs = q[j][lo:hi] * jnp.exp(b[j][lo:hi] - beta)
                ks = k[j] * jnp.exp(jnp.where(outside, beta - b[j], -jnp.inf))
                a_row.append(_dot_nt(qs.astype(BF16), ks.astype(BF16)))
            else:
                a_row.append(jnp.zeros((SUB, c), F32))
        for half in range(SUB // HALF):
            h0 = lo + half * HALF
            piece = [a_row[j][half * HALF:(half + 1) * HALF] for j in n]
            for s in range(HALF):
                for j in n:
                    bh2 = b2[j][h0:h0 + HALF]
                    a_col = jnp.sum(jnp.exp2(bh2 - bh2[s:s + 1]) * q[j][h0:h0 + HALF] * k[j][h0 + s:h0 + s + 1],
                                    axis=-1, keepdims=True)
                    piece[j] = jnp.where(lane_half == h0 + s, a_col, piece[j])
            for j in n:
                rows[j].append(piece[j])
    second_half = (chunk_row % SUB) >= HALF
    same_block = (row // SUB) == (col // SUB)
    out = []
    for j in n:
        meet = HALF if rev[j] else HALF - 1
        beta = jnp.concatenate([jnp.broadcast_to(b[j][lo + meet:lo + meet + 1], (SUB, b[j].shape[1]))
                                for lo in range(0, c, SUB)], axis=0)
        t_side = jnp.logical_not(second_half) if rev[j] else second_half
        qs = q[j] * jnp.exp(jnp.where(t_side, b[j] - beta, -jnp.inf))
        ks = k[j] * jnp.exp(jnp.where(t_side, -jnp.inf, beta - b[j]))
        cross = jnp.where(same_block, _dot_nt(qs.astype(BF16), ks.astype(BF16)), 0.0)
        attn = jnp.where(tris[rev[j]], jnp.concatenate(rows[j], axis=0) + cross, 0.0)
        o = _dot(attn.astype(BF16), v[j].astype(BF16)) \
            + _dot_nt((q[j] * jnp.exp(b[j])).astype(BF16), st[j].astype(BF16))
        b_exit = b[j][0:1] if rev[j] else b[j][c - 1:c]
        k_out = k[j] * jnp.exp(b_exit - b[j])
        st_new = jnp.exp(b_exit) * st[j] + _dot_tn(v[j].astype(BF16), k_out.astype(BF16))
        out.append((o, st_new))
    return out


def _hgrn_kernel(seq_len, layer, q_ref, i_ref, g_ref, ff_ref, fb_ref, lb_ref, s0_ref,
                 o_ref, s_out_ref, of_ref, ob_ref, st_ref):
    c = SCAN_CHUNK
    hd = HEAD_DIM
    n_chunks = seq_len // c
    lbp = lb_ref[...]
    e = jnp.exp(lbp - jnp.max(lbp, axis=0, keepdims=True))
    lb = jnp.sum(e[0:layer + 1], axis=0, keepdims=True) / jnp.sum(e, axis=0, keepdims=True)

    for d in range(2):
        for hh in range(HG_STEP_HEADS):
            st_ref[d, hh] = s0_ref[d, hh].T

    def body(n, carry):
        where, chains = [], []
        for d in range(2):
            sl = pl.ds(pl.multiple_of((n_chunks - 1 - n if d else n) * c, c), c)
            for hh in range(HG_STEP_HEADS):
                cols = slice(hh * hd, (hh + 1) * hd)
                f = lb[:, cols] + (1.0 - lb[:, cols]) * _sigmoid((fb_ref if d else ff_ref)[sl, cols])
                where.append((d, hh, sl, cols))
                chains.append((bool(d), q_ref[sl, cols], 1.0 - f, i_ref[sl, cols], jnp.log(f), st_ref[d, hh]))
        for (d, hh, sl, cols), (o, st_new) in zip(where, _hgrn_chunks(chains)):
            st_ref[d, hh] = st_new
            (ob_ref if d else of_ref)[sl, cols] = o
        return carry

    lax.fori_loop(0, n_chunks, body, 0)
    for hh in range(HG_STEP_HEADS):
        cols = slice(hh * hd, (hh + 1) * hd)
        o = of_ref[:, cols] + ob_ref[:, cols]
        o = o * lax.rsqrt(jnp.mean(o * o, axis=-1, keepdims=True) + RMS_EPS)
        o_ref[:, cols] = o * _silu(g_ref[:, cols])
        for d in range(2):
            s_out_ref[d, hh] = st_ref[d, hh].T


def _hgrn(p, hg_lb, layer, s0, n_seq, seq_len, tok_offset):
    hd = HEAD_DIM
    sh = HG_STEP_HEADS
    steps_per_seq = HEADS // sh
    row0 = tok_offset // seq_len

    def col(part):
        return pl.BlockSpec((seq_len, sh * hd), lambda b, h: (row0 + b, part * steps_per_seq + h))

    state_spec = pl.BlockSpec((None, 2, sh, hd, hd), lambda b, h: (b, 0, h, 0, 0))
    return pl.pallas_call(
        functools.partial(_hgrn_kernel, seq_len, layer),
        grid=(n_seq, steps_per_seq),
        in_specs=[col(0), col(1), col(2), col(3), col(4),
                  pl.BlockSpec((DEPTH + 1, sh * hd), lambda b, h: (0, h)),
                  state_spec],
        out_specs=[pl.BlockSpec((seq_len, sh * hd), lambda b, h: (b, h)), state_spec],
        out_shape=[jax.ShapeDtypeStruct((n_seq * seq_len, GROUP_W), F32),
                   jax.ShapeDtypeStruct((n_seq, 2, HEADS, hd, hd), F32)],
        scratch_shapes=[pltpu.VMEM((seq_len, sh * hd), F32), pltpu.VMEM((seq_len, sh * hd), F32),
                        pltpu.VMEM((2, sh, hd, hd), F32)],
        compiler_params=_cparams("parallel", "parallel"),
        name=f"hgrn_l{seq_len}",
    )(p, p, p, p, p, hg_lb, s0)


def _short_conv3(x, w):
    n = x.shape[0]
    r = lax.broadcasted_iota(jnp.int32, (n, 1), 0)
    prev = jnp.where(r == 0, 0.0, pltpu.roll(x, 1, 0))
    nxt = jnp.where(r == n - 1, 0.0, pltpu.roll(x, n - 1, 0))
    return prev * w[0:1] + x * w[1:2] + nxt * w[2:3]


def _conv3_silu_tiles(dst_ref, src_ref, w_ref, scale):
    n, width = src_ref.shape
    c = SCAN_CHUNK
    r = lax.broadcasted_iota(jnp.int32, (c, 1), 0)
    for j in range(n // c):
        r0 = j * c
        for col in range(0, width, LANES):
            cols = slice(col, col + LANES)
            cur = src_ref[r0:r0 + c, cols]
            prev = jnp.where(r == 0, 0.0, pltpu.roll(cur, 1, 0)) if j == 0 else src_ref[r0 - 1:r0 + c - 1, cols]
            nxt = (jnp.where(r == c - 1, 0.0, pltpu.roll(cur, c - 1, 0)) if r0 + c == n
                   else src_ref[r0 + 1:r0 + c + 1, cols])
            w = w_ref[:, cols]
            y = _silu(prev * w[0:1] + cur * w[1:2] + nxt * w[2:3])
            dst_ref[r0:r0 + c, cols] = y if scale == 1.0 else y * scale


def _split2(x):
    hi = x.astype(BF16)
    return hi, (x - hi.astype(F32)).astype(BF16)


def _mlstm_chunks(chains):
    c = SCAN_CHUNK
    row = lax.broadcasted_iota(jnp.int32, (c, c), 0)
    col = lax.broadcasted_iota(jnp.int32, (c, c), 1)
    eye_b = jnp.where(row == col, 1.0, 0.0).astype(BF16)
    tris = {False: row <= col, True: row >= col}
    tris_b = {r: jnp.where(t, 1.0, 0.0).astype(BF16) for r, t in tris.items()}
    n = range(len(chains))
    rev, q, k, vt, ig, fg, ct, nv, m_prev = zip(*chains)

    def each(fn):
        return [fn(i) for i in n]

    def dot3(parts, rhs, nt=False):
        d = _dot_nt if nt else _dot
        return d(parts[0], rhs) + d(parts[1], rhs) + d(parts[2], rhs)

    lf = each(lambda i: _split3(jnp.broadcast_to(_log_sigmoid(fg[i]), (8, c))))
    b = each(lambda i: dot3(lf[i], tris_b[rev[i]])[0:1])
    us = each(lambda i: _split3(jnp.broadcast_to(ig[i] - b[i], (c, c))))
    u = each(lambda i: _dot_nt(eye_b, us[i][0]) + _dot_nt(eye_b, us[i][1]) + _dot_nt(eye_b, us[i][2]))
    dmat = each(lambda i: jnp.where(tris[rev[i]], b[i] + u[i], -jnp.inf))
    m_t = each(lambda i: jnp.maximum(b[i] + m_prev[i], jnp.max(dmat[i], axis=0, keepdims=True)))
    qb = each(lambda i: q[i].astype(BF16))
    kb = each(lambda i: k[i].astype(BF16))
    kq = each(lambda i: _dot_nt(kb[i], qb[i]))
    p = each(lambda i: jnp.exp(dmat[i] - m_t[i]) * kq[i])
    inter = each(lambda i: jnp.exp(b[i] + m_prev[i] - m_t[i]))
    ns = each(lambda i: _split2(jnp.broadcast_to(nv[i], (8, nv[i].shape[1]))))
    qs = each(lambda i: _split2(q[i]))
    qn = each(lambda i: (_dot_nt(ns[i][0], qs[i][0]) + _dot_nt(ns[i][1], qs[i][0])
                         + _dot_nt(ns[i][0], qs[i][1]))[0:1])
    den = each(lambda i: inter[i] * qn[i] + jnp.sum(p[i], axis=0, keepdims=True))
    scale = each(lambda i: 1.0 / jnp.maximum(jnp.abs(den[i]), jnp.exp(-m_t[i])))
    cq = each(lambda i: _dot_nt(ct[i].astype(BF16), qb[i]))
    vp = each(lambda i: _dot(vt[i].astype(BF16), p[i].astype(BF16)))
    ht = each(lambda i: (inter[i] * cq[i] + vp[i]) * scale[i])
    last = each(lambda i: 0 if rev[i] else c - 1)
    m_new = each(lambda i: m_t[i][:, last[i]:last[i] + 1])
    b_exit = each(lambda i: b[i][:, last[i]:last[i] + 1])
    w = each(lambda i: jnp.exp(b_exit[i] - b[i] + ig[i] - m_new[i]))
    dec = each(lambda i: jnp.exp(b_exit[i] + m_prev[i] - m_new[i]))
    vk = each(lambda i: _dot((vt[i] * w[i]).astype(BF16), kb[i]))
    ct_new = each(lambda i: dec[i] * ct[i] + vk[i])
    ws = each(lambda i: _split2(jnp.broadcast_to(w[i], (8, c))))
    ks = each(lambda i: _split2(k[i]))
    wk = each(lambda i: (_dot(ws[i][0], ks[i][0]) + _dot(ws[i][1], ks[i][0]) + _dot(ws[i][0], ks[i][1]))[0:1])
    nv_new = each(lambda i: dec[i] * nv[i] + wk[i])
    return [(ht[i], ct_new[i], nv_new[i], m_new[i]) for i in n]


def _mlstm_kernel(seq_len, q_ref, k_ref, v_ref, og_ref, gate_t_ref, cwq_ref, cwk_ref,
                  c0_ref, n0_ref, m0_ref,
                  o_ref, c_out_ref, n_out_ref, m_out_ref,
                  q2_ref, k2_ref, vt_ref, hf_ref, hb_ref, ct_ref, n_ref, m_ref):
    c = SCAN_CHUNK
    hd = HEAD_DIM
    n_chunks = seq_len // c
    _conv3_silu_tiles(q2_ref, q_ref, cwq_ref, 1.0)
    _conv3_silu_tiles(k2_ref, k_ref, cwk_ref, HEAD_DIM ** -0.5)
    for hh in range(HEADS):
        cols = slice(hh * hd, (hh + 1) * hd)
        for j in range(n_chunks):
            vt_ref[hh, :, j * c:(j + 1) * c] = v_ref[j * c:(j + 1) * c, cols].T
        for d in range(2):
            ct_ref[d, hh] = c0_ref[d, hh].T
    n_ref[...] = n0_ref[...]
    m_ref[...] = m0_ref[...]

    def body(n, carry):
        where, chains = [], []
        for d in range(2):
            sl = pl.ds(pl.multiple_of((n_chunks - 1 - n if d else n) * c, c), c)
            for hh in range(HEADS):
                cols = slice(hh * hd, (hh + 1) * hd)
                gr = gate_t_ref[hh, :, sl]
                where.append((d, hh, sl))
                chains.append((bool(d), q2_ref[sl, cols], k2_ref[sl, cols], vt_ref[hh, :, sl],
                               gr[2 * d:2 * d + 1, :], gr[2 * d + 1:2 * d + 2, :],
                               ct_ref[d, hh], n_ref[d, hh], m_ref[d, hh]))
        for (d, hh, sl), (ht, ct, nv, m_new) in zip(where, _mlstm_chunks(chains)):
            ct_ref[d, hh] = ct
            n_ref[d, hh] = nv
            m_ref[d, hh] = m_new
            (hb_ref if d else hf_ref)[hh, :, sl] = ht
        return carry

    lax.fori_loop(0, n_chunks, body, 0)
    for hh in range(HEADS):
        cols = slice(hh * hd, (hh + 1) * hd)
        for j in range(n_chunks):
            rows = slice(j * c, (j + 1) * c)
            h = (hf_ref[hh, :, rows] + hb_ref[hh, :, rows]).T
            h = h * lax.rsqrt(jnp.mean(h * h, axis=-1, keepdims=True) + RMS_EPS)
            o_ref[rows, cols] = h * _sigmoid(og_ref[rows, cols])
        for d in range(2):
            c_out_ref[d, hh] = ct_ref[d, hh].T
    n_out_ref[...] = n_ref[...]
    m_out_ref[...] = m_ref[...]


def _mlstm(p, gate_t_h, conv_w, c0, n0, m0, n_seq, seq_len, tok_offset):
    hd = HEAD_DIM
    gw = GROUP_W
    row0 = tok_offset // seq_len

    def col(part):
        return pl.BlockSpec((seq_len, gw), lambda b: (row0 + b, part))

    c_spec = pl.BlockSpec((None, 2, HEADS, hd, hd), lambda b: (b, 0, 0, 0, 0))
    n_spec = pl.BlockSpec((None, 2, HEADS, 1, hd), lambda b: (b, 0, 0, 0, 0))
    m_spec = pl.BlockSpec((None, 2, HEADS, 1, 1), lambda b: (b, 0, 0, 0, 0))
    return pl.pallas_call(
        functools.partial(_mlstm_kernel, seq_len),
        grid=(n_seq,),
        in_specs=[col(5), col(6), col(7), col(8),
                  pl.BlockSpec((HEADS, 4, seq_len), lambda b: (0, 0, row0 + b)),
                  pl.BlockSpec((3, gw), lambda b: (0, 0)),
                  pl.BlockSpec((3, gw), lambda b: (0, 1)),
                  c_spec, n_spec, m_spec],
        out_specs=[pl.BlockSpec((seq_len, gw), lambda b: (b, 0)), c_spec, n_spec, m_spec],
        out_shape=[jax.ShapeDtypeStruct((n_seq * seq_len, gw), F32),
                   jax.ShapeDtypeStruct((n_seq, 2, HEADS, hd, hd), F32),
                   jax.ShapeDtypeStruct((n_seq, 2, HEADS, 1, hd), F32),
                   jax.ShapeDtypeStruct((n_seq, 2, HEADS, 1, 1), F32)],
        scratch_shapes=[pltpu.VMEM((seq_len, gw), F32), pltpu.VMEM((seq_len, gw), F32),
                        pltpu.VMEM((HEADS, hd, seq_len), F32),
                        pltpu.VMEM((HEADS, hd, seq_len), F32), pltpu.VMEM((HEADS, hd, seq_len), F32),
                        pltpu.VMEM((2, HEADS, hd, hd), F32), pltpu.VMEM((2, HEADS, 1, hd), F32),
                        pltpu.VMEM((2, HEADS, 1, 1), F32)],
        compiler_params=_cparams("parallel"),
        name=f"mlstm_l{seq_len}",
    )(p, p, p, p, gate_t_h, conv_w, conv_w, c0, n0, m0)


def _dft_tables(seq_len):
    n = 2 * seq_len
    k = jnp.arange(seq_len, dtype=jnp.int32)[:, None]
    t = jnp.arange(seq_len, dtype=jnp.int32)[None, :]
    ang = ((k * t) % n).astype(F32) * (2.0 * np.pi / n)
    fc = jnp.cos(ang)
    fs = jnp.sin(ang)
    nyq = jnp.where(t % 2 == 0, 1.0, -1.0).astype(F32)
    fs = jnp.where(k == 0, nyq, fs)
    return jnp.concatenate([fc, fs], axis=0)


def _filter_kernel(seq_len, z_ref, w1_ref, b1_ref, w2_ref, b2_ref, w3f_ref, w3b_ref, b3f_ref, b3b_ref,
                   f0_ref, f1_ref, rf_ref, rb_ref, fhi_ref, flo_ref, kf_ref, a_ref):
    hp = lax.Precision.HIGHEST
    n = 2 * seq_len
    z = z_ref[...]

    @pl.when(jnp.logical_and(pl.program_id(0) == 0, pl.program_id(1) == 0))
    def _():
        a1 = jnp.sin(f0_ref[...] * (jnp.dot(z, w1_ref[...], precision=hp, preferred_element_type=F32)
                                    + b1_ref[...]))
        a_ref[...] = jnp.sin(f1_ref[...] * (jnp.dot(a1, w2_ref[...], precision=hp, preferred_element_type=F32)
                                            + b2_ref[...]))

    a = a_ref[...]
    t_norm = z[:, 0:1]
    hf = (jnp.dot(a, w3f_ref[...], precision=hp, preferred_element_type=F32) + b3f_ref[...]) \
        * jnp.exp(-t_norm * jnp.exp(rf_ref[...]))
    hb = (jnp.dot(a, w3b_ref[...], precision=hp, preferred_element_type=F32) + b3b_ref[...]) \
        * jnp.exp(-t_norm * jnp.exp(rb_ref[...]))
    inv = lax.rsqrt(jnp.sum(hf * hf, axis=0, keepdims=True) + jnp.sum(hb * hb, axis=0, keepdims=True))
    hf = hf * inv
    r = lax.broadcasted_iota(jnp.int32, (seq_len, 1), 0)
    hb = jnp.where(r == 0, 0.0, hb * inv)
    sh, sl = _split2(hf + hb)
    dh, dl = _split2(hf - hb)
    fhi = fhi_ref[...]
    flo = flo_ref[...]
    kc = _dot(fhi[:seq_len], sh) + _dot(fhi[:seq_len], sl) + _dot(flo[:seq_len], sh)
    ks = _dot(fhi[seq_len:], dh) + _dot(fhi[seq_len:], dl) + _dot(flo[seq_len:], dh)
    sign = jnp.where(r % 2 == 0, 1.0, -1.0)
    k_nyq = jnp.sum(sign * (hf + hb), axis=0, keepdims=True)
    ks = jnp.where(r == 0, k_nyq, ks)
    scale = jnp.where(r == 0, 1.0 / n, 2.0 / n)
    kf_ref[0:seq_len, :] = kc * scale
    kf_ref[seq_len:n, :] = ks * scale


def _hyena_filters(seq_len, w1, b1, w2, b2, w3, b3, freq, log_rate, f_tab):
    d = D_MODEL
    cb = 256
    t = jnp.arange(seq_len, dtype=F32)
    t_norm = t / (seq_len - 1)
    bands = jnp.linspace(1e-4, HY_BANDS - 1, HY_BANDS, dtype=F32)
    ang = (2.0 * np.pi / seq_len) * t[:, None] * bands[None, :]
    z = jnp.concatenate([t_norm[:, None], jnp.cos(ang), jnp.sin(ang)], axis=-1)
    kpad = 128 - HY_EMB
    z = jnp.pad(z, ((0, 0), (0, kpad)))
    w1p = jnp.pad(w1, ((0, kpad), (0, 0)))
    f_hi = f_tab.astype(BF16)
    f_lo = (f_tab - f_hi.astype(F32)).astype(BF16)
    n_cb = d // cb
    hh = HY_HIDDEN
    row = lambda a: a.reshape(1, -1)
    const = lambda shape: pl.BlockSpec(shape, lambda o, j: (0,) * len(shape))
    fwd = lambda rows: pl.BlockSpec((rows, cb), lambda o, j: (0, o * n_cb + j))
    bwd = lambda rows: pl.BlockSpec((rows, cb), lambda o, j: (0, (HY_ORDER + o) * n_cb + j))
    return pl.pallas_call(
        functools.partial(_filter_kernel, seq_len),
        grid=(HY_ORDER, n_cb),
        in_specs=[const((seq_len, 128)), const((128, hh)), const((1, hh)), const((hh, hh)), const((1, hh)),
                  fwd(hh), bwd(hh), fwd(1), bwd(1),
                  const((1, hh)), const((1, hh)), fwd(1), bwd(1),
                  const((2 * seq_len, seq_len)), const((2 * seq_len, seq_len))],
        out_specs=pl.BlockSpec((None, 2 * seq_len, cb), lambda o, j: (o, 0, j)),
        out_shape=jax.ShapeDtypeStruct((HY_ORDER, 2 * seq_len, d), F32),
        scratch_shapes=[pltpu.VMEM((seq_len, hh), F32)],
        compiler_params=_cparams("arbitrary", "arbitrary"),
        name=f"hyena_filter_l{seq_len}",
    )(z, w1p, row(b1), w2, row(b2), w3, w3, row(b3), row(b3),
      row(freq[0]), row(freq[1]), row(log_rate), row(log_rate), f_hi, f_lo)


def _hyena_kernel(seq_len, seqs, v_ref, x1_ref, x2_ref, cwv_ref, cw1_ref, cw2_ref, bias_ref, kf_ref,
                  f_ref, ft_ref, o_ref, z_ref, zb_ref, y_ref):
    kc = min(HY_FREQ_CHUNK, seq_len)
    n_k = seq_len // kc
    r = lax.broadcasted_iota(jnp.int32, (kc, 1), 0)
    gate_refs = ((x1_ref, cw1_ref), (x2_ref, cw2_ref))
    for s in range(seqs):
        rows = slice(s * seq_len, (s + 1) * seq_len)
        z_ref[s] = _short_conv3(v_ref[rows, :], cwv_ref[...])
    for o in range(HY_ORDER):
        for s in range(seqs):
            zb_ref[s] = z_ref[s].astype(BF16)
            y_ref[s] = jnp.zeros(y_ref.shape[1:], F32)

        def freq_chunk(j, carry):
            r0 = pl.multiple_of(j * kc, kc)
            k_cos = kf_ref[o, pl.ds(r0, kc), :]
            k_sin = kf_ref[o, pl.ds(seq_len + r0, kc), :]
            real_row = jnp.logical_and(r == 0, j == 0)
            for s in range(seqs):
                a = _dot(f_ref[pl.ds(r0, kc), :], zb_ref[s])
                bm = _dot(f_ref[pl.ds(seq_len + r0, kc), :], zb_ref[s])
                yc = a * k_cos - jnp.where(real_row, 0.0, bm * k_sin)
                ys = jnp.where(real_row, bm * k_sin, a * k_sin + bm * k_cos)
                y_ref[s] += _dot(ft_ref[j], yc.astype(BF16)) + _dot(ft_ref[n_k + j], ys.astype(BF16))
            return carry

        lax.fori_loop(0, n_k, freq_chunk, 0)
        x_ref, cw_ref = gate_refs[o]
        for s in range(seqs):
            rows = slice(s * seq_len, (s + 1) * seq_len)
            gate = _short_conv3(x_ref[rows, :], cw_ref[...])
            z_ref[s] = gate * (y_ref[s] + z_ref[s] * bias_ref[o:o + 1, :])
    for s in range(seqs):
        o_ref[s * seq_len:(s + 1) * seq_len, :] = z_ref[s]


def _hyena(u, conv_w, bias, kf, f_tab, n_seq, seq_len, tok_offset):
    d = D_MODEL
    cb = 256
    n_cb = d // cb
    seqs = max(1, HY_STEP_ROWS // seq_len)
    rows = seqs * seq_len
    row0 = tok_offset // rows
    kc = min(HY_FREQ_CHUNK, seq_len)
    n_k = seq_len // kc
    f_bf = f_tab.astype(BF16)
    ft = f_bf.T.reshape(seq_len, 2 * n_k, kc).transpose(1, 0, 2)

    def part(k):
        return pl.BlockSpec((rows, cb), lambda j, b: (row0 + b, k * n_cb + j))

    def cw(k):
        return pl.BlockSpec((3, cb), lambda j, b: (0, k * n_cb + j))

    return pl.pallas_call(
        functools.partial(_hyena_kernel, seq_len, seqs),
        grid=(n_cb, n_seq // seqs),
        in_specs=[part(0), part(1), part(2), cw(0), cw(1), cw(2),
                  pl.BlockSpec((HY_ORDER, cb), lambda j, b: (0, j)),
                  pl.BlockSpec((HY_ORDER, 2 * seq_len, cb), lambda j, b: (0, 0, j)),
                  pl.BlockSpec((2 * seq_len, seq_len), lambda j, b: (0, 0)),
                  pl.BlockSpec((2 * n_k, seq_len, kc), lambda j, b: (0, 0, 0))],
        out_specs=pl.BlockSpec((rows, cb), lambda j, b: (b, j)),
        out_shape=jax.ShapeDtypeStruct((n_seq * seq_len, d), F32),
        scratch_shapes=[pltpu.VMEM((seqs, seq_len, cb), F32), pltpu.VMEM((seqs, seq_len, cb), BF16),
                        pltpu.VMEM((seqs, seq_len, cb), F32)],
        compiler_params=_cparams("parallel", "parallel"),
        name=f"hyena_l{seq_len}",
    )(u, u, u, conv_w, conv_w, conv_w, bias, kf, f_bf, ft)


def _mix_out_kernel(n_parts, *refs):
    o_refs = refs[:2 * n_parts]
    w_refs = refs[2 * n_parts:3 * n_parts]
    x_ref, mod_ref, g_ref, wr_both_ref, br_ref = refs[3 * n_parts:3 * n_parts + 5]
    x1_ref, h_ref, eid_ref, wts_ref, rank_ref, cnt_ref, run_ref = refs[3 * n_parts + 5:]
    i = pl.program_id(0)
    t = ROUTE_TILE
    ne = N_EXPERTS

    from_prompt = i < N_PROMPT // ROUTE_TILE
    y = None
    for j in range(n_parts):
        o = jnp.where(from_prompt, o_refs[2 * j][...], o_refs[2 * j + 1][...])
        yj = _dot(o.astype(BF16), w_refs[j][...])
        y = yj if y is None else y + yj
    x1 = x_ref[...] + mod_ref[2:3, :] * y
    x1_ref[...] = x1
    h = _norm_mod(x1, g_ref[...], mod_ref[4:5, :], mod_ref[3:4, :])
    h_ref[...] = _pack_bf16_halves(h)
    h_hi = h.astype(BF16)
    h_lo = (h - h_hi.astype(F32)).astype(BF16)
    hi_terms = _dot(h_hi, wr_both_ref[...])
    logits = (hi_terms[:, :LANES] + hi_terms[:, LANES:] + _dot(h_lo, wr_both_ref[:, :LANES])).T[:ne] \
        + br_ref[...]

    @pl.when(i == 0)
    def _():
        run_ref[...] = jnp.zeros_like(run_ref)

    e_iota = lax.broadcasted_iota(jnp.int32, (ne, t), 0)
    vals, eids, onehots = [], [], []
    for _k in range(TOP_K):
        m = jnp.max(logits, axis=0, keepdims=True)
        eid = jnp.min(jnp.where(logits == m, e_iota, ne), axis=0, keepdims=True)
        sel = e_iota == eid
        logits = jnp.where(sel, -jnp.inf, logits)
        onehots.append(jnp.where(sel, 1.0, 0.0))
        vals.append(m)
        eids.append(eid)
    r2 = lax.broadcasted_iota(jnp.int32, (t, t), 0)
    c2 = lax.broadcasted_iota(jnp.int32, (t, t), 1)
    before = jnp.where(r2 < c2, 1.0, 0.0).astype(BF16)
    earlier = _dot(jnp.concatenate(onehots, axis=0).astype(BF16), before)
    running = run_ref[...]
    ranks = []
    for k, onehot in enumerate(onehots):
        ranks.append(jnp.sum(onehot * (running + earlier[k * ne:(k + 1) * ne]), axis=0, keepdims=True))
        running = running + jnp.sum(onehot, axis=1, keepdims=True)
    run_ref[...] = running
    cnt_ref[...] = running
    v = jnp.concatenate(vals, axis=0)
    ex = jnp.exp(v - v[0:1])
    wts_ref[...] = ex / jnp.sum(ex, axis=0, keepdims=True)
    eid_ref[...] = jnp.concatenate(eids, axis=0)
    rank_ref[...] = jnp.concatenate(ranks, axis=0).astype(jnp.int32)


def _mix_out(parts, x, mod_l, norm_g, w_router, b_router):
    d = D_MODEL
    t = ROUTE_TILE
    ne = N_EXPERTS
    n_parts = len(parts)
    n_prompt_tiles = N_PROMPT // t
    wr = jnp.pad(w_router, ((0, 0), (0, LANES - ne)))
    wr_hi = wr.astype(BF16)
    wr_both = jnp.concatenate([wr_hi, (wr - wr_hi.astype(F32)).astype(BF16)], axis=1)
    in_specs = []
    for o_p, _, _ in parts:
        in_specs.append(pl.BlockSpec((t, o_p.shape[1]), lambda i: (jnp.minimum(i, n_prompt_tiles - 1), 0)))
        in_specs.append(pl.BlockSpec((t, o_p.shape[1]), lambda i: (jnp.maximum(i - n_prompt_tiles, 0), 0)))
    in_specs += [pl.BlockSpec(w.shape, lambda i: (0, 0)) for _, _, w in parts]
    in_specs += [
        pl.BlockSpec((t, d), lambda i: (i, 0)),
        pl.BlockSpec((None, N_MOD, d), lambda i: (_tile_cond_row(i, t), 0, 0)),
        pl.BlockSpec((1, d), lambda i: (0, 0)),
        pl.BlockSpec((d, 2 * LANES), lambda i: (0, 0)),
        pl.BlockSpec((ne, 1), lambda i: (0, 0)),
    ]
    tok_major = pl.BlockSpec((TOP_K, t), lambda i: (0, i))
    return pl.pallas_call(
        functools.partial(_mix_out_kernel, n_parts),
        grid=(N_TOK // t,),
        in_specs=in_specs,
        out_specs=[pl.BlockSpec((t, d), lambda i: (i, 0)), pl.BlockSpec((t, d // 2), lambda i: (i, 0)),
                   tok_major, tok_major, tok_major, pl.BlockSpec((ne, 1), lambda i: (0, 0))],
        out_shape=[jax.ShapeDtypeStruct((N_TOK, d), F32), jax.ShapeDtypeStruct((N_TOK, d // 2), jnp.uint32),
                   jax.ShapeDtypeStruct((TOP_K, N_TOK), jnp.int32), jax.ShapeDtypeStruct((TOP_K, N_TOK), F32),
                   jax.ShapeDtypeStruct((TOP_K, N_TOK), jnp.int32), jax.ShapeDtypeStruct((ne, 1), F32)],
        scratch_shapes=[pltpu.VMEM((ne, 1), F32)],
        compiler_params=_cparams("arbitrary"),
        name="mix_out_router",
    )(*[o for part in parts for o in part[:2]], *[w.astype(BF16) for _, _, w in parts], x, mod_l,
      norm_g.reshape(1, d), wr_both, b_router.reshape(ne, 1))


def _sc_row_gather(src, idx):
    n = idx.shape[0]
    width = src.shape[1]
    step_rows = min(SC_MAX_INDICES, SC_STEP_BYTES // (width * 4))
    sc = plsc.get_sparse_core_info()
    n_workers = sc.num_cores * sc.num_subcores
    per_worker = n // n_workers
    if n % n_workers or per_worker % step_rows:
        raise ValueError("row count must be whole SparseCore steps on every subcore")
    mesh = plsc.VectorSubcoreMesh(core_axis_name="core", subcore_axis_name="subcore")

    @functools.partial(pl.kernel, out_type=jax.ShapeDtypeStruct((n, width), src.dtype), mesh=mesh,
                       scratch_types=[pltpu.VMEM((per_worker,), jnp.int32),
                                      pltpu.VMEM((step_rows, width), src.dtype)],
                       name="sc_row_gather")
    def gather(src_hbm, idx_hbm, dst_hbm, idx_vmem, rows_vmem):
        worker = lax.axis_index("subcore") * sc.num_cores + lax.axis_index("core")
        base = worker * per_worker
        pltpu.sync_copy(idx_hbm.at[pl.ds(base, per_worker)], idx_vmem)

        @pl.loop(0, per_worker // step_rows)
        def _(c):
            pltpu.sync_copy(src_hbm.at[idx_vmem.at[pl.ds(c * step_rows, step_rows)]], rows_vmem)
            pltpu.sync_copy(rows_vmem, dst_hbm.at[pl.ds(base + c * step_rows, step_rows)])

    return gather(src, idx)


def _sc_row_scatter(src, rows_of_src):
    copies, n_src = rows_of_src.shape
    width = src.shape[1]
    step_rows = min(SC_MAX_INDICES, SC_STEP_BYTES // (width * 4))
    sc = plsc.get_sparse_core_info()
    n_workers = sc.num_cores * sc.num_subcores
    per_worker = n_src // n_workers
    steps = per_worker // step_rows
    if src.shape[0] != n_src or n_src % n_workers or per_worker % step_rows:
        raise ValueError("row counts must be whole SparseCore steps on every subcore")
    mesh = plsc.VectorSubcoreMesh(core_axis_name="core", subcore_axis_name="subcore")
    idx = rows_of_src.reshape(copies, n_workers, steps, step_rows).transpose(1, 2, 0, 3)

    @functools.partial(pl.kernel, out_type=jax.ShapeDtypeStruct((copies * n_src, width), src.dtype),
                       mesh=mesh,
                       scratch_types=[pltpu.VMEM((steps, copies, step_rows), jnp.int32),
                                      pltpu.VMEM((step_rows, width), src.dtype)],
                       name="sc_row_scatter")
    def scatter(src_hbm, idx_hbm, dst_hbm, idx_vmem, rows_vmem):
        worker = lax.axis_index("subcore") * sc.num_cores + lax.axis_index("core")
        pltpu.sync_copy(idx_hbm.at[worker], idx_vmem)

        @pl.loop(0, steps)
        def _(c):
            pltpu.sync_copy(src_hbm.at[pl.ds(worker * per_worker + c * step_rows, step_rows)], rows_vmem)
            for j in range(copies):
                pltpu.sync_copy(rows_vmem, dst_hbm.at[idx_vmem.at[c, j]])

    return scatter(src, idx)


def _pack_bf16_halves(x):
    w = x.shape[1] // 2
    bits = pltpu.bitcast(x.astype(BF16).astype(F32), jnp.uint32)
    return bits[:, :w] | (bits[:, w:] >> 16)


def _unpack_bf16_halves(p):
    hi = pltpu.bitcast(p & jnp.uint32(0xFFFF0000), F32).astype(BF16)
    lo = pltpu.bitcast(p << 16, F32).astype(BF16)
    return hi, lo


def _experts_kernel(layer, te_ref, first_ref, slot_ref, next_ref, rows_ref, nv_ref,
                    x_ref, wgu_hbm, bg_ref, bu_ref, wd_hbm, bd_ref, sel_ref,
                    y_ref, wgu_buf, wd_buf, wg_ref, wu_ref, wdb_ref, sem):
    i = pl.program_id(0)
    valid = i < nv_ref[0]
    half = DEINT_COLS // 2
    k_half = x_ref.shape[1]

    def fetch(expert, slot):
        return (pltpu.make_async_copy(wgu_hbm.at[layer, expert], wgu_buf.at[slot], sem.at[slot, 0]),
                pltpu.make_async_copy(wd_hbm.at[layer, expert], wd_buf.at[slot], sem.at[slot, 1]))

    @pl.when(i == 0)
    def _():
        for cp in fetch(te_ref[0], 0):
            cp.start()

    @pl.when(jnp.logical_and(valid, first_ref[i] == 1))
    def _():
        slot = slot_ref[i]
        for cp in fetch(te_ref[i], slot):
            cp.wait()

        @pl.when(next_ref[i] >= 0)
        def _():
            for cp in fetch(next_ref[i], 1 - slot):
                cp.start(priority=1)

        for c in range(wgu_buf.shape[2] // DEINT_COLS):
            w = wgu_buf[slot, :, c * DEINT_COLS:(c + 1) * DEINT_COLS].astype(BF16)
            split = _dot(w, sel_ref[...]).astype(BF16)
            wg_ref[:, c * half:(c + 1) * half] = split[:, :half]
            wu_ref[:, c * half:(c + 1) * half] = split[:, half:]
        wdb_ref[...] = wd_buf[slot].astype(BF16)

    def ffn(rows):
        x_l, x_r = _unpack_bf16_halves(x_ref[:rows, :])
        gl = _dot(x_l, wg_ref[:k_half, :]) + _dot(x_r, wg_ref[k_half:, :]) + bg_ref[...]
        up = _dot(x_l, wu_ref[:k_half, :]) + _dot(x_r, wu_ref[k_half:, :]) + bu_ref[...]
        gl = jnp.minimum(gl, SWIGLU_LIMIT)
        up = jnp.clip(up, -SWIGLU_LIMIT, SWIGLU_LIMIT)
        act = (up + 1.0) * gl * _sigmoid(SWIGLU_ALPHA * gl)
        y_ref[:rows, :] = _pack_bf16_halves(_dot(act.astype(BF16), wdb_ref[...]) + bd_ref[...])

    tm = x_ref.shape[0]
    few = rows_ref[i] <= tm // 2

    @pl.when(jnp.logical_and(valid, jnp.logical_not(few)))
    def _():
        ffn(tm)

    @pl.when(jnp.logical_and(valid, few))
    def _():
        ffn(tm // 2)
        y_ref[tm // 2:, :] = jnp.zeros((tm - tm // 2, y_ref.shape[1]), y_ref.dtype)

    @pl.when(jnp.logical_not(valid))
    def _():
        y_ref[...] = jnp.zeros_like(y_ref)


def _experts(xs, tile_expert, tile_first, tile_rows, n_valid, layer, w_gu, b_gate, b_up, w_down, b_down):
    d = D_MODEL
    tm = MOE_TILE
    n_tiles = MOE_ROWS // tm
    ff = w_down.shape[2]
    half = DEINT_COLS // 2
    r = jnp.arange(DEINT_COLS)[:, None]
    c = jnp.arange(DEINT_COLS)[None, :]
    sel = (r == jnp.where(c < half, 2 * c, 2 * (c - half) + 1)).astype(BF16)
    group = jnp.cumsum(tile_first) - 1
    tile_slot = (group % 2).astype(jnp.int32)
    is_last_group = group == group[-1]
    following = jnp.concatenate([tile_expert[1:], tile_expert[-1:]])
    idx = jnp.arange(n_tiles, dtype=jnp.int32)
    group_end = jnp.max(jnp.where(group[None, :] == group[:, None], idx[None, :], -1), axis=1)
    tile_next = jnp.where(is_last_group, -1, following[group_end]).astype(jnp.int32)
    wspec = lambda k, n: pl.BlockSpec((None, None, k, n), lambda i, *_: (layer, _[0][i], 0, 0))
    grid_spec = pltpu.PrefetchScalarGridSpec(
        num_scalar_prefetch=6,
        grid=(n_tiles,),
        in_specs=[pl.BlockSpec((tm, d // 2), lambda i, *_: (i, 0)),
                  pl.BlockSpec(memory_space=pl.ANY), wspec(1, ff), wspec(1, ff),
                  pl.BlockSpec(memory_space=pl.ANY), wspec(1, d),
                  pl.BlockSpec((DEINT_COLS, DEINT_COLS), lambda i, *_: (0, 0))],
        out_specs=pl.BlockSpec((tm, d // 2), lambda i, *_: (i, 0)),
        scratch_shapes=[pltpu.VMEM((2, d, 2 * ff), F32), pltpu.VMEM((2, ff, d), F32),
                        pltpu.VMEM((d, ff), BF16), pltpu.VMEM((d, ff), BF16), pltpu.VMEM((ff, d), BF16),
                        pltpu.SemaphoreType.DMA((2, 2))],
    )
    return pl.pallas_call(
        functools.partial(_experts_kernel, layer),
        grid_spec=grid_spec,
        out_shape=jax.ShapeDtypeStruct((MOE_ROWS, d // 2), jnp.uint32),
        compiler_params=_cparams("arbitrary"),
        name="experts",
    )(tile_expert, tile_first, tile_slot, tile_next, tile_rows, n_valid, xs, w_gu, b_gate, b_up, w_down,
      b_down, sel)


def _combine_kernel(final, x_ref, g_ref, w_ref, mod_ref, fg_ref, *o_refs):
    y_l = y_r = None
    for k in range(TOP_K):
        g_l, g_r = _unpack_bf16_halves(g_ref[k])
        wk = w_ref[:, k:k + 1]
        y_l = wk * g_l.astype(F32) if y_l is None else y_l + wk * g_l.astype(F32)
        y_r = wk * g_r.astype(F32) if y_r is None else y_r + wk * g_r.astype(F32)
    x = x_ref[...] + mod_ref[5:6, :] * jnp.concatenate([y_l, y_r], axis=1)
    if not final:
        o_refs[0][...] = x
        return
    x = x * lax.rsqrt(jnp.mean(x * x, axis=-1, keepdims=True) + RMS_EPS) * fg_ref[...]
    from_prompt = pl.program_id(0) < N_PROMPT // ROUTE_TILE

    @pl.when(from_prompt)
    def _():
        o_refs[0][...] = x

    @pl.when(jnp.logical_not(from_prompt))
    def _():
        o_refs[1][...] = x


def _combine(x1, gathered, wts, mod_l, final_g, final):
    d = D_MODEL
    t = ROUTE_TILE
    n_prompt_tiles = N_PROMPT // t
    if final:
        out_specs = [pl.BlockSpec((t, d), lambda i: (jnp.minimum(i, n_prompt_tiles - 1), 0)),
                     pl.BlockSpec((t, d), lambda i: (jnp.maximum(i - n_prompt_tiles, 0), 0))]
        out_shape = [jax.ShapeDtypeStruct((N_PROMPT, d), F32), jax.ShapeDtypeStruct((N_SAMPLE, d), F32)]
    else:
        out_specs = pl.BlockSpec((t, d), lambda i: (i, 0))
        out_shape = jax.ShapeDtypeStruct((N_TOK, d), F32)
    return pl.pallas_call(
        functools.partial(_combine_kernel, final),
        grid=(N_TOK // t,),
        in_specs=[pl.BlockSpec((t, d), lambda i: (i, 0)),
                  pl.BlockSpec((TOP_K, t, d // 2), lambda i: (0, i, 0)),
                  pl.BlockSpec((t, TOP_K), lambda i: (i, 0)),
                  pl.BlockSpec((None, N_MOD, d), lambda i: (_tile_cond_row(i, t), 0, 0)),
                  pl.BlockSpec((1, d), lambda i: (0, 0))],
        out_specs=out_specs,
        out_shape=out_shape,
        compiler_params=_cparams("arbitrary"),
        name="moe_combine",
    )(x1, gathered, wts, mod_l, final_g.reshape(1, d))


def _moe(x1, h, eid, wts, rank, counts, mod_l, layer, w_gu, b_gu, w_down, b_down, final_g, final):
    d = D_MODEL
    tm = MOE_TILE
    n_tiles = MOE_ROWS // tm
    cnt = counts.reshape(N_EXPERTS).astype(jnp.int32)
    gsz = ((cnt + tm - 1) // tm) * tm
    ends = jnp.cumsum(gsz)
    offs = ends - gsz
    e_ids = jnp.arange(N_EXPERTS, dtype=jnp.int32)
    pos = jnp.sum(jnp.where(eid[..., None] == e_ids, offs, 0), axis=-1) + rank
    tile_start = jnp.arange(n_tiles, dtype=jnp.int32) * tm
    tile_expert = jnp.minimum(jnp.sum((ends[None, :] <= tile_start[:, None]).astype(jnp.int32), axis=1),
                              N_EXPERTS - 1)
    n_valid = (ends[-1:] // tm).astype(jnp.int32)
    last_valid = jnp.maximum(n_valid[0] - 1, 0)
    tile_expert = jnp.where(jnp.arange(n_tiles) < n_valid[0], tile_expert, tile_expert[last_valid])
    tile_first = jnp.concatenate([jnp.ones((1,), jnp.int32),
                                  (tile_expert[1:] != tile_expert[:-1]).astype(jnp.int32)])
    j = jnp.arange(tm, dtype=jnp.int32)[None, :]
    pad_used = j < (gsz - cnt)[:, None]
    n_unused_before = jnp.cumsum((~pad_used).reshape(-1).astype(jnp.int32)) - 1
    pad_pos = jnp.where(pad_used, (offs + cnt)[:, None] + j,
                        ends[-1] + n_unused_before.reshape(N_EXPERTS, tm))
    if N_EXPERTS * tm != N_TOK:
        raise ValueError("the filler slots are laid out as one extra copy of every token")

    xs = _sc_row_scatter(h, jnp.concatenate([pos, pad_pos.reshape(1, N_TOK)], axis=0))
    real_end = (offs + cnt)[tile_expert]
    tile_rows = jnp.clip(real_end - tile_start, 0, tm).astype(jnp.int32)
    ys = _experts(xs, tile_expert, tile_first, tile_rows, n_valid, layer, w_gu,
                  b_gu[:, :, None, 0::2], b_gu[:, :, None, 1::2], w_down, b_down[:, :, None, :])
    gathered = _sc_row_gather(ys, pos.reshape(-1))
    return _combine(x1, gathered.reshape(TOP_K, N_TOK, d // 2), wts.T, mod_l, final_g, final)


def _grid_positions(n_tok, d):
    rows = n_tok // GRID_W
    r, col = jnp.meshgrid(jnp.arange(rows, dtype=F32), jnp.arange(GRID_W, dtype=F32), indexing='ij')
    r = r.reshape(-1)
    col = col.reshape(-1)
    quarter = d // 4
    inv = 1.0 / (10000.0 ** (jnp.arange(quarter, dtype=F32) / quarter))
    ar = r[:, None] * inv[None]
    ac = col[:, None] * inv[None]
    return jnp.concatenate([jnp.sin(ar), jnp.cos(ar), jnp.sin(ac), jnp.cos(ac)], axis=-1)


def kernel(x_prompt, x_sample, state_hgrn, state_mlstm_c, state_mlstm_n, state_mlstm_m, c, c_ctx,
           norm_g, final_g, w_mod, b_mod, ev_w_in, ev_gate_b, ev_conv, hg_lb, ev_w_out,
           hy_w_in, hy_conv, hy_w1, hy_b1, hy_w2, hy_b2, hy_w3, hy_b3, hy_freq, hy_log_rate, hy_bias, hy_w_out,
           w_router, b_router, w_gu, b_gu, w_down, b_down):
    d = D_MODEL
    hd = HEAD_DIM
    cond = jnp.concatenate([c_ctx[None], c, jnp.zeros((N_COND - 1 - DEC_BATCH, d), F32)], axis=0)
    mod = _modulation(cond, w_mod, b_mod)
    pos_tab = jnp.concatenate([jnp.zeros((TOK_TILE, d), F32), _grid_positions(DEC_SEQ, d)], axis=0)

    groups = ((BATCH, SEQ, 0), (DEC_BATCH, DEC_SEQ, N_PROMPT))
    new_states = None
    for l in range(DEPTH):
        if l % 2 == 0:
            e = l // 2
            if l == 0:
                x, p, gate_t = _proj_even(x_prompt.reshape(N_PROMPT, d), x_sample.reshape(N_SAMPLE, d),
                                          pos_tab, mod[l], norm_g[l, 0], ev_w_in[e], ev_gate_b[e])
            else:
                raise NotImplementedError("only the first layer adds grid positions")
            gate_t_h = gate_t.reshape(4, HEADS, N_TOK).transpose(1, 0, 2)
            o_hg, o_ml = [], []
            for gi, (n_seq, seq_len, off) in enumerate(groups):
                if gi == 0:
                    s0 = jnp.zeros((n_seq, 2, HEADS, hd, hd), F32)
                    c0 = jnp.zeros((n_seq, 2, HEADS, hd, hd), F32)
                    n0 = jnp.zeros((n_seq, 2, HEADS, 1, hd), F32)
                    m0 = jnp.zeros((n_seq, 2, HEADS, 1, 1), F32)
                else:
                    s0 = state_hgrn[:, e]
                    c0 = state_mlstm_c[:, e]
                    n0 = state_mlstm_n[:, e].reshape(n_seq, 2, HEADS, 1, hd)
                    m0 = state_mlstm_m[:, e].reshape(n_seq, 2, HEADS, 1, 1)
                og, s_fin = _hgrn(p, hg_lb, l, s0, n_seq, seq_len, off)
                om, c_fin, n_fin, m_fin = _mlstm(p, gate_t_h, ev_conv[e], c0, n0, m0, n_seq, seq_len, off)
                o_hg.append(og)
                o_ml.append(om)
                if gi == 0:
                    new_states = (s_fin[:, None], c_fin[:, None],
                                  n_fin.reshape(n_seq, 1, 2, HEADS, hd), m_fin.reshape(n_seq, 1, 2, HEADS))
            parts = [(*o_hg, ev_w_out[e][:GROUP_W]), (*o_ml, ev_w_out[e][GROUP_W:])]
        else:
            o = l // 2
            u = _proj_odd(x, mod[l], norm_g[l, 0], hy_w_in[o])
            zs = []
            for n_seq, seq_len, off in groups:
                f_tab = _dft_tables(seq_len)
                kf = _hyena_filters(seq_len, hy_w1[o], hy_b1[o], hy_w2[o], hy_b2[o], hy_w3[o], hy_b3[o],
                                    hy_freq[o], hy_log_rate[o], f_tab)
                zs.append(_hyena(u, hy_conv[o], hy_bias[o], kf, f_tab, n_seq, seq_len, off))
            parts = [(*zs, hy_w_out[o])]
        x1, h, eid, wts, rank, counts = _mix_out(parts, x, mod[l], norm_g[l, 1], w_router[l], b_router[l])
        x = _moe(x1, h, eid, wts, rank, counts, mod[l], l, w_gu, b_gu, w_down, b_down,
                 final_g, final=(l == DEPTH - 1))

    y_prompt, y_sample = x
    return (y_prompt.reshape(BATCH, SEQ, d), y_sample.reshape(DEC_BATCH, DEC_SEQ, d)) + new_states
```

```python
import functools

import numpy as np
import jax
import jax.numpy as jnp
from jax import lax
from jax.experimental import pallas as pl
from jax.experimental.pallas import tpu as pltpu
from jax.experimental.pallas import tpu_sc as plsc

F32 = jnp.float32
BF16 = jnp.bfloat16

D_MODEL = 1024
BATCH = 32
SEQ = 256
DEPTH = 2
DEC_BATCH = 8
DEC_SEQ = 1024
GRID_W = 64
RMS_EPS = 1e-6
N_MOD = 6
LOG2_E = 1.4426950408889634
LANES = 128

HEADS = 4
HEAD_DIM = 128
GROUP_W = HEADS * HEAD_DIM
N_GATES = 4 * HEADS
EVEN_MAIN = 9 * GROUP_W

HY_ORDER = 2
HY_BANDS = 16
HY_EMB = 1 + 2 * HY_BANDS
HY_HIDDEN = 64
HY_FREQ_CHUNK = 512
HY_STEP_ROWS = 1024

N_EXPERTS = 32
TOP_K = 4
SWIGLU_LIMIT = 7.0
SWIGLU_ALPHA = 1.702

N_PROMPT = BATCH * SEQ
N_SAMPLE = DEC_BATCH * DEC_SEQ
N_TOK = N_PROMPT + N_SAMPLE
N_COND = 16

TOK_TILE = 512
ROUTE_TILE = 512
SCAN_CHUNK = 128
SUB = 16
HALF = 8
HG_STEP_HEADS = 4
MOE_TILE = 512
MOE_ROWS = N_TOK * TOP_K + N_EXPERTS * MOE_TILE
DEINT_COLS = 256
SC_STEP_BYTES = 256 * 1024
SC_MAX_INDICES = 128

VMEM_LIMIT = 56 * 1024 * 1024


def _cparams(*sem):
    return pltpu.CompilerParams(dimension_semantics=sem, vmem_limit_bytes=VMEM_LIMIT)


def _split3(x):
    hi = x.astype(BF16)
    r = x - hi.astype(F32)
    mid = r.astype(BF16)
    lo = (r - mid.astype(F32)).astype(BF16)
    return hi, mid, lo


def _dot(a, b):
    return jnp.dot(a, b, preferred_element_type=F32)


def _dot_nt(a, b):
    return lax.dot_general(a, b, (((1,), (1,)), ((), ())), preferred_element_type=F32)


def _dot_tn(a, b):
    return lax.dot_general(a, b, (((0,), (0,)), ((), ())), preferred_element_type=F32)


def _dot_w3(a_exact_bf16, x):
    hi, mid, lo = _split3(x)
    return _dot(a_exact_bf16, hi) + _dot(a_exact_bf16, mid) + _dot(a_exact_bf16, lo)


def _sigmoid(x):
    return 1.0 / (1.0 + jnp.exp(-x))


def _silu(x):
    return x * _sigmoid(x)


def _log_sigmoid(x):
    return jnp.minimum(x, 0.0) - jnp.log(1.0 + jnp.exp(-jnp.abs(x)))


def _tile_cond_row(i, tile=TOK_TILE):
    n_prompt_tiles = N_PROMPT // tile
    tiles_per_seq = DEC_SEQ // tile
    return jnp.where(i < n_prompt_tiles, 0, 1 + (i - n_prompt_tiles) // tiles_per_seq)


def _mod_kernel(cond_ref, w_ref, b_ref, o_ref):
    a = _silu(cond_ref[...]).astype(BF16)
    o_ref[...] = _dot(a, w_ref[...].astype(BF16)) + b_ref[...]


def _modulation(cond, w_mod, b_mod):
    d = D_MODEL
    out = pl.pallas_call(
        _mod_kernel,
        grid=(DEPTH, N_MOD),
        in_specs=[
            pl.BlockSpec((N_COND, d), lambda l, j: (0, 0)),
            pl.BlockSpec((None, d, d), lambda l, j: (l, 0, j)),
            pl.BlockSpec((None, 1, d), lambda l, j: (l, 0, j)),
        ],
        out_specs=pl.BlockSpec((None, None, N_COND, d), lambda l, j: (l, j, 0, 0)),
        out_shape=jax.ShapeDtypeStruct((DEPTH, N_MOD, N_COND, d), F32),
        compiler_params=_cparams("parallel", "parallel"),
        name="modulation",
    )(cond, w_mod, b_mod.reshape(DEPTH, 1, N_MOD * d))
    return out.transpose(0, 2, 1, 3)


def _norm_mod(x, g_row, scale_row, shift_row):
    ms = jnp.mean(x * x, axis=-1, keepdims=True)
    y = x * lax.rsqrt(ms + RMS_EPS) * g_row
    return y * (1.0 + scale_row) + shift_row


def _proj_even_kernel(xp_ref, xs_ref, pos_ref, mod_ref, g_ref, w_ref, wgt_ref, gbt_ref,
                      xres_ref, p_ref, gate_t_ref):
    from_prompt = pl.program_id(0) < N_PROMPT // TOK_TILE
    x = jnp.where(from_prompt, xp_ref[...], xs_ref[...]) + pos_ref[...]
    xres_ref[...] = x
    h = _norm_mod(x, g_ref[...], mod_ref[1:2, :], mod_ref[0:1, :]).astype(BF16)
    p_ref[...] = _dot(h, w_ref[...])
    gate_t_ref[...] = _dot_nt(wgt_ref[...], h) + gbt_ref[...]


def _proj_even(x_prompt, x_sample, pos_tab, mod_l, norm_g, w_in, gate_b):
    d = D_MODEL
    n_tiles = N_TOK // TOK_TILE
    n_prompt_tiles = N_PROMPT // TOK_TILE
    tiles_per_seq = DEC_SEQ // TOK_TILE
    w_main = w_in[:, :EVEN_MAIN].astype(BF16)
    w_gate = w_in[:, EVEN_MAIN:].astype(BF16)

    def pos_map(i):
        return (jnp.where(i < n_prompt_tiles, 0, 1 + (i - n_prompt_tiles) % tiles_per_seq), 0)

    return pl.pallas_call(
        _proj_even_kernel,
        grid=(n_tiles,),
        in_specs=[
            pl.BlockSpec((TOK_TILE, d), lambda i: (jnp.minimum(i, n_prompt_tiles - 1), 0)),
            pl.BlockSpec((TOK_TILE, d), lambda i: (jnp.maximum(i - n_prompt_tiles, 0), 0)),
            pl.BlockSpec((TOK_TILE, d), pos_map),
            pl.BlockSpec((None, N_MOD, d), lambda i: (_tile_cond_row(i), 0, 0)),
            pl.BlockSpec((1, d), lambda i: (0, 0)),
            pl.BlockSpec((d, EVEN_MAIN), lambda i: (0, 0), pipeline_mode=pl.Buffered(1)),
            pl.BlockSpec((N_GATES, d), lambda i: (0, 0)),
            pl.BlockSpec((N_GATES, 1), lambda i: (0, 0)),
        ],
        out_specs=[
            pl.BlockSpec((TOK_TILE, d), lambda i: (i, 0)),
            pl.BlockSpec((TOK_TILE, EVEN_MAIN), lambda i: (i, 0)),
            pl.BlockSpec((N_GATES, TOK_TILE), lambda i: (0, i)),
        ],
        out_shape=[
            jax.ShapeDtypeStruct((N_TOK, d), F32),
            jax.ShapeDtypeStruct((N_TOK, EVEN_MAIN), F32),
            jax.ShapeDtypeStruct((N_GATES, N_TOK), F32),
        ],
        compiler_params=_cparams("parallel"),
        name="proj_even",
    )(x_prompt, x_sample, pos_tab, mod_l, norm_g.reshape(1, d), w_main, w_gate.T, gate_b.reshape(N_GATES, 1))


def _proj_odd_kernel(x_ref, mod_ref, g_ref, w_ref, p_ref):
    h = _norm_mod(x_ref[...], g_ref[...], mod_ref[1:2, :], mod_ref[0:1, :]).astype(BF16)
    p_ref[...] = _dot(h, w_ref[...])


def _proj_odd(x, mod_l, norm_g, w_in):
    d = D_MODEL
    width = w_in.shape[1]
    return pl.pallas_call(
        _proj_odd_kernel,
        grid=(N_TOK // TOK_TILE,),
        in_specs=[
            pl.BlockSpec((TOK_TILE, d), lambda i: (i, 0)),
            pl.BlockSpec((None, N_MOD, d), lambda i: (_tile_cond_row(i), 0, 0)),
            pl.BlockSpec((1, d), lambda i: (0, 0)),
            pl.BlockSpec((d, width), lambda i: (0, 0), pipeline_mode=pl.Buffered(1)),
        ],
        out_specs=pl.BlockSpec((TOK_TILE, width), lambda i: (i, 0)),
        out_shape=jax.ShapeDtypeStruct((N_TOK, width), F32),
        compiler_params=_cparams("parallel"),
        name="proj_odd",
    )(x, mod_l, norm_g.reshape(1, d), w_in.astype(BF16))


def _hgrn_chunks(chains):
    c = SCAN_CHUNK
    n = range(len(chains))
    rev, q, k, v, lf, st = zip(*chains)
    row = lax.broadcasted_iota(jnp.int32, (c, c), 0)
    col = lax.broadcasted_iota(jnp.int32, (c, c), 1)
    tris = {False: col <= row, True: col >= row}
    tris_b = {r: jnp.where(t, 1.0, 0.0).astype(BF16) for r, t in tris.items()}
    b = [_dot_w3(tris_b[rev[j]], lf[j]) for j in n]
    b2 = [b[j] * LOG2_E for j in n]
    lane_half = lax.broadcasted_iota(jnp.int32, (HALF, c), 1)
    chunk_row = lax.broadcasted_iota(jnp.int32, (c, 1), 0)
    rows = [[] for _ in n]
    for i in range(c // SUB):
        lo, hi = i * SUB, (i + 1) * SUB
        a_row = []
        for j in n:
            if rev[j]:
                has_off, edge, outside = hi < c, hi, chunk_row >= hi
            else:
                has_off, edge, outside = lo > 0, lo - 1, chunk_row < lo
            if has_off:
                beta = b[j][edge:edge + 1]
                qs = q[j][lo:hi] * jnp.exp(b[j][lo:hi] - beta)
                ks = k[j] * jnp.exp(jnp.where(outside, beta - b[j], -jnp.inf))
                a_row.append(_dot_nt(qs.astype(BF16), ks.astype(BF16)))
            else:
                a_row.append(jnp.zeros((SUB, c), F32))
        for half in range(SUB // HALF):
            h0 = lo + half * HALF
            piece = [a_row[j][half * HALF:(half + 1) * HALF] for j in n]
            for s in range(HALF):
                for j in n:
                    bh2 = b2[j][h0:h0 + HALF]
                    a_col = jnp.sum(jnp.exp2(bh2 - bh2[s:s + 1]) * q[j][h0:h0 + HALF] * k[j][h0 + s:h0 + s + 1],
                                    axis=-1, keepdims=True)
                    piece[j] = jnp.where(lane_half == h0 + s, a_col, piece[j])
            for j in n:
                rows[j].append(piece[j])
    second_half = (chunk_row % SUB) >= HALF
    same_block = (row // SUB) == (col // SUB)
    out = []
    for j in n:
        meet = HALF if rev[j] else HALF - 1
        beta = jnp.concatenate([jnp.broadcast_to(b[j][lo + meet:lo + meet + 1], (SUB, b[j].shape[1]))
                                for lo in range(0, c, SUB)], axis=0)
        t_side = jnp.logical_not(second_half) if rev[j] else second_half
        qs = q[j] * jnp.exp(jnp.where(t_side, b[j] - beta, -jnp.inf))
        ks = k[j] * jnp.exp(jnp.where(t_side, -jnp.inf, beta - b[j]))
        cross = jnp.where(same_block, _dot_nt(qs.astype(BF16), ks.astype(BF16)), 0.0)
        attn = jnp.where(tris[rev[j]], jnp.concatenate(rows[j], axis=0) + cross, 0.0)
        o = _dot(attn.astype(BF16), v[j].astype(BF16)) \
            + _dot_nt((q[j] * jnp.exp(b[j])).astype(BF16), st[j].astype(BF16))
        b_exit = b[j][0:1] if rev[j] else b[j][c - 1:c]
        k_out = k[j] * jnp.exp(b_exit - b[j])
        st_new = jnp.exp(b_exit) * st[j] + _dot_tn(v[j].astype(BF16), k_out.astype(BF16))
        out.append((o, st_new))
    return out


def _hgrn_kernel(seq_len, layer, q_ref, i_ref, g_ref, ff_ref, fb_ref, lb_ref, s0_ref,
                 o_ref, s_out_ref, of_ref, ob_ref, st_ref):
    c = SCAN_CHUNK
    hd = HEAD_DIM
    n_chunks = seq_len // c
    lbp = lb_ref[...]
    e = jnp.exp(lbp - jnp.max(lbp, axis=0, keepdims=True))
    lb = jnp.sum(e[0:layer + 1], axis=0, keepdims=True) / jnp.sum(e, axis=0, keepdims=True)

    for d in range(2):
        for hh in range(HG_STEP_HEADS):
            st_ref[d, hh] = s0_ref[d, hh].T

    def body(n, carry):
        where, chains = [], []
        for d in range(2):
            sl = pl.ds(pl.multiple_of((n_chunks - 1 - n if d else n) * c, c), c)
            for hh in range(HG_STEP_HEADS):
                cols = slice(hh * hd, (hh + 1) * hd)
                f = lb[:, cols] + (1.0 - lb[:, cols]) * _sigmoid((fb_ref if d else ff_ref)[sl, cols])
                where.append((d, hh, sl, cols))
                chains.append((bool(d), q_ref[sl, cols], 1.0 - f, i_ref[sl, cols], jnp.log(f), st_ref[d, hh]))
        for (d, hh, sl, cols), (o, st_new) in zip(where, _hgrn_chunks(chains)):
            st_ref[d, hh] = st_new
            (ob_ref if d else of_ref)[sl, cols] = o
        return carry

    lax.fori_loop(0, n_chunks, body, 0)
    for hh in range(HG_STEP_HEADS):
        cols = slice(hh * hd, (hh + 1) * hd)
        o = of_ref[:, cols] + ob_ref[:, cols]
        o = o * lax.rsqrt(jnp.mean(o * o, axis=-1, keepdims=True) + RMS_EPS)
        o_ref[:, cols] = o * _silu(g_ref[:, cols])
        for d in range(2):
            s_out_ref[d, hh] = st_ref[d, hh].T


def _hgrn(p, hg_lb, layer, s0, n_seq, seq_len, tok_offset):
    hd = HEAD_DIM
    sh = HG_STEP_HEADS
    steps_per_seq = HEADS // sh
    row0 = tok_offset // seq_len

    def col(part):
        return pl.BlockSpec((seq_len, sh * hd), lambda b, h: (row0 + b, part * steps_per_seq + h))

    state_spec = pl.BlockSpec((None, 2, sh, hd, hd), lambda b, h: (b, 0, h, 0, 0))
    return pl.pallas_call(
        functools.partial(_hgrn_kernel, seq_len, layer),
        grid=(n_seq, steps_per_seq),
        in_specs=[col(0), col(1), col(2), col(3), col(4),
                  pl.BlockSpec((DEPTH + 1, sh * hd), lambda b, h: (0, h)),
                  state_spec],
        out_specs=[pl.BlockSpec((seq_len, sh * hd), lambda b, h: (b, h)), state_spec],
        out_shape=[jax.ShapeDtypeStruct((n_seq * seq_len, GROUP_W), F32),
                   jax.ShapeDtypeStruct((n_seq, 2, HEADS, hd, hd), F32)],
        scratch_shapes=[pltpu.VMEM((seq_len, sh * hd), F32), pltpu.VMEM((seq_len, sh * hd), F32),
                        pltpu.VMEM((2, sh, hd, hd), F32)],
        compiler_params=_cparams("parallel", "parallel"),
        name=f"hgrn_l{seq_len}",
    )(p, p, p, p, p, hg_lb, s0)


def _short_conv3(x, w):
    n = x.shape[0]
    r = lax.broadcasted_iota(jnp.int32, (n, 1), 0)
    prev = jnp.where(r == 0, 0.0, pltpu.roll(x, 1, 0))
    nxt = jnp.where(r == n - 1, 0.0, pltpu.roll(x, n - 1, 0))
    return prev * w[0:1] + x * w[1:2] + nxt * w[2:3]


def _conv3_silu_tiles(dst_ref, src_ref, w_ref, scale):
    n, width = src_ref.shape
    c = SCAN_CHUNK
    r = lax.broadcasted_iota(jnp.int32, (c, 1), 0)
    for j in range(n // c):
        r0 = j * c
        for col in range(0, width, LANES):
            cols = slice(col, col + LANES)
            cur = src_ref[r0:r0 + c, cols]
            prev = jnp.where(r == 0, 0.0, pltpu.roll(cur, 1, 0)) if j == 0 else src_ref[r0 - 1:r0 + c - 1, cols]
            nxt = (jnp.where(r == c - 1, 0.0, pltpu.roll(cur, c - 1, 0)) if r0 + c == n
                   else src_ref[r0 + 1:r0 + c + 1, cols])
            w = w_ref[:, cols]
            y = _silu(prev * w[0:1] + cur * w[1:2] + nxt * w[2:3])
            dst_ref[r0:r0 + c, cols] = y if scale == 1.0 else y * scale


def _split2(x):
    hi = x.astype(BF16)
    return hi, (x - hi.astype(F32)).astype(BF16)


def _mlstm_chunks(chains):
    c = SCAN_CHUNK
    row = lax.broadcasted_iota(jnp.int32, (c, c), 0)
    col = lax.broadcasted_iota(jnp.int32, (c, c), 1)
    eye_b = jnp.where(row == col, 1.0, 0.0).astype(BF16)
    tris = {False: row <= col, True: row >= col}
    tris_b = {r: jnp.where(t, 1.0, 0.0).astype(BF16) for r, t in tris.items()}
    n = range(len(chains))
    rev, q, k, vt, ig, fg, ct, nv, m_prev = zip(*chains)

    def each(fn):
        return [fn(i) for i in n]

    def dot3(parts, rhs, nt=False):
        d = _dot_nt if nt else _dot
        return d(parts[0], rhs) + d(parts[1], rhs) + d(parts[2], rhs)

    lf = each(lambda i: _split3(jnp.broadcast_to(_log_sigmoid(fg[i]), (8, c))))
    b = each(lambda i: dot3(lf[i], tris_b[rev[i]])[0:1])
    us = each(lambda i: _split3(jnp.broadcast_to(ig[i] - b[i], (c, c))))
    u = each(lambda i: _dot_nt(eye_b, us[i][0]) + _dot_nt(eye_b, us[i][1]) + _dot_nt(eye_b, us[i][2]))
    dmat = each(lambda i: jnp.where(tris[rev[i]], b[i] + u[i], -jnp.inf))
    m_t = each(lambda i: jnp.maximum(b[i] + m_prev[i], jnp.max(dmat[i], axis=0, keepdims=True)))
    qb = each(lambda i: q[i].astype(BF16))
    kb = each(lambda i: k[i].astype(BF16))
    kq = each(lambda i: _dot_nt(kb[i], qb[i]))
    p = each(lambda i: jnp.exp(dmat[i] - m_t[i]) * kq[i])
    inter = each(lambda i: jnp.exp(b[i] + m_prev[i] - m_t[i]))
    ns = each(lambda i: _split2(jnp.broadcast_to(nv[i], (8, nv[i].shape[1]))))
    qs = each(lambda i: _split2(q[i]))
    qn = each(lambda i: (_dot_nt(ns[i][0], qs[i][0]) + _dot_nt(ns[i][1], qs[i][0])
                         + _dot_nt(ns[i][0], qs[i][1]))[0:1])
    den = each(lambda i: inter[i] * qn[i] + jnp.sum(p[i], axis=0, keepdims=True))
    scale = each(lambda i: 1.0 / jnp.maximum(jnp.abs(den[i]), jnp.exp(-m_t[i])))
    cq = each(lambda i: _dot_nt(ct[i].astype(BF16), qb[i]))
    vp = each(lambda i: _dot(vt[i].astype(BF16), p[i].astype(BF16)))
    ht = each(lambda i: (inter[i] * cq[i] + vp[i]) * scale[i])
    last = each(lambda i: 0 if rev[i] else c - 1)
    m_new = each(lambda i: m_t[i][:, last[i]:last[i] + 1])
    b_exit = each(lambda i: b[i][:, last[i]:last[i] + 1])
    w = each(lambda i: jnp.exp(b_exit[i] - b[i] + ig[i] - m_new[i]))
    dec = each(lambda i: jnp.exp(b_exit[i] + m_prev[i] - m_new[i]))
    vk = each(lambda i: _dot((vt[i] * w[i]).astype(BF16), kb[i]))
    ct_new = each(lambda i: dec[i] * ct[i] + vk[i])
    ws = each(lambda i: _split2(jnp.broadcast_to(w[i], (8, c))))
    ks = each(lambda i: _split2(k[i]))
    wk = each(lambda i: (_dot(ws[i][0], ks[i][0]) + _dot(ws[i][1], ks[i][0]) + _dot(ws[i][0], ks[i][1]))[0:1])
    nv_new = each(lambda i: dec[i] * nv[i] + wk[i])
    return [(ht[i], ct_new[i], nv_new[i], m_new[i]) for i in n]


def _mlstm_kernel(seq_len, q_ref, k_ref, v_ref, og_ref, gate_t_ref, cwq_ref, cwk_ref,
                  c0_ref, n0_ref, m0_ref,
                  o_ref, c_out_ref, n_out_ref, m_out_ref,
                  q2_ref, k2_ref, vt_ref, hf_ref, hb_ref, ct_ref, n_ref, m_ref):
    c = SCAN_CHUNK
    hd = HEAD_DIM
    n_chunks = seq_len // c
    _conv3_silu_tiles(q2_ref, q_ref, cwq_ref, 1.0)
    _conv3_silu_tiles(k2_ref, k_ref, cwk_ref, HEAD_DIM ** -0.5)
    for hh in range(HEADS):
        cols = slice(hh * hd, (hh + 1) * hd)
        for j in range(n_chunks):
            vt_ref[hh, :, j * c:(j + 1) * c] = v_ref[j * c:(j + 1) * c, cols].T
        for d in range(2):
            ct_ref[d, hh] = c0_ref[d, hh].T
    n_ref[...] = n0_ref[...]
    m_ref[...] = m0_ref[...]

    def body(n, carry):
        where, chains = [], []
        for d in range(2):
            sl = pl.ds(pl.multiple_of((n_chunks - 1 - n if d else n) * c, c), c)
            for hh in range(HEADS):
                cols = slice(hh * hd, (hh + 1) * hd)
                gr = gate_t_ref[hh, :, sl]
                where.append((d, hh, sl))
                chains.append((bool(d), q2_ref[sl, cols], k2_ref[sl, cols], vt_ref[hh, :, sl],
                               gr[2 * d:2 * d + 1, :], gr[2 * d + 1:2 * d + 2, :],
                               ct_ref[d, hh], n_ref[d, hh], m_ref[d, hh]))
        for (d, hh, sl), (ht, ct, nv, m_new) in zip(where, _mlstm_chunks(chains)):
            ct_ref[d, hh] = ct
            n_ref[d, hh] = nv
            m_ref[d, hh] = m_new
            (hb_ref if d else hf_ref)[hh, :, sl] = ht
        return carry

    lax.fori_loop(0, n_chunks, body, 0)
    for hh in range(HEADS):
        cols = slice(hh * hd, (hh + 1) * hd)
        for j in range(n_chunks):
            rows = slice(j * c, (j + 1) * c)
            h = (hf_ref[hh, :, rows] + hb_ref[hh, :, rows]).T
            h = h * lax.rsqrt(jnp.mean(h * h, axis=-1, keepdims=True) + RMS_EPS)
            o_ref[rows, cols] = h * _sigmoid(og_ref[rows, cols])
        for d in range(2):
            c_out_ref[d, hh] = ct_ref[d, hh].T
    n_out_ref[...] = n_ref[...]
    m_out_ref[...] = m_ref[...]


def _mlstm(p, gate_t_h, conv_w, c0, n0, m0, n_seq, seq_len, tok_offset):
    hd = HEAD_DIM
    gw = GROUP_W
    row0 = tok_offset // seq_len

    def col(part):
        return pl.BlockSpec((seq_len, gw), lambda b: (row0 + b, part))

    c_spec = pl.BlockSpec((None, 2, HEADS, hd, hd), lambda b: (b, 0, 0, 0, 0))
    n_spec = pl.BlockSpec((None, 2, HEADS, 1, hd), lambda b: (b, 0, 0, 0, 0))
    m_spec = pl.BlockSpec((None, 2, HEADS, 1, 1), lambda b: (b, 0, 0, 0, 0))
    return pl.pallas_call(
        functools.partial(_mlstm_kernel, seq_len),
        grid=(n_seq,),
        in_specs=[col(5), col(6), col(7), col(8),
                  pl.BlockSpec((HEADS, 4, seq_len), lambda b: (0, 0, row0 + b)),
                  pl.BlockSpec((3, gw), lambda b: (0, 0)),
                  pl.BlockSpec((3, gw), lambda b: (0, 1)),
                  c_spec, n_spec, m_spec],
        out_specs=[pl.BlockSpec((seq_len, gw), lambda b: (b, 0)), c_spec, n_spec, m_spec],
        out_shape=[jax.ShapeDtypeStruct((n_seq * seq_len, gw), F32),
                   jax.ShapeDtypeStruct((n_seq, 2, HEADS, hd, hd), F32),
                   jax.ShapeDtypeStruct((n_seq, 2, HEADS, 1, hd), F32),
                   jax.ShapeDtypeStruct((n_seq, 2, HEADS, 1, 1), F32)],
        scratch_shapes=[pltpu.VMEM((seq_len, gw), F32), pltpu.VMEM((seq_len, gw), F32),
                        pltpu.VMEM((HEADS, hd, seq_len), F32),
                        pltpu.VMEM((HEADS, hd, seq_len), F32), pltpu.VMEM((HEADS, hd, seq_len), F32),
                        pltpu.VMEM((2, HEADS, hd, hd), F32), pltpu.VMEM((2, HEADS, 1, hd), F32),
                        pltpu.VMEM((2, HEADS, 1, 1), F32)],
        compiler_params=_cparams("parallel"),
        name=f"mlstm_l{seq_len}",
    )(p, p, p, p, gate_t_h, conv_w, conv_w, c0, n0, m0)


def _dft_tables(seq_len):
    n = 2 * seq_len
    k = jnp.arange(seq_len, dtype=jnp.int32)[:, None]
    t = jnp.arange(seq_len, dtype=jnp.int32)[None, :]
    ang = ((k * t) % n).astype(F32) * (2.0 * np.pi / n)
    fc = jnp.cos(ang)
    fs = jnp.sin(ang)
    nyq = jnp.where(t % 2 == 0, 1.0, -1.0).astype(F32)
    fs = jnp.where(k == 0, nyq, fs)
    return jnp.concatenate([fc, fs], axis=0)


def _filter_kernel(seq_len, z_ref, w1_ref, b1_ref, w2_ref, b2_ref, w3f_ref, w3b_ref, b3f_ref, b3b_ref,
                   f0_ref, f1_ref, rf_ref, rb_ref, fhi_ref, flo_ref, kf_ref, a_ref):
    hp = lax.Precision.HIGHEST
    n = 2 * seq_len
    z = z_ref[...]

    @pl.when(jnp.logical_and(pl.program_id(0) == 0, pl.program_id(1) == 0))
    def _():
        a1 = jnp.sin(f0_ref[...] * (jnp.dot(z, w1_ref[...], precision=hp, preferred_element_type=F32)
                                    + b1_ref[...]))
        a_ref[...] = jnp.sin(f1_ref[...] * (jnp.dot(a1, w2_ref[...], precision=hp, preferred_element_type=F32)
                                            + b2_ref[...]))

    a = a_ref[...]
    t_norm = z[:, 0:1]
    hf = (jnp.dot(a, w3f_ref[...], precision=hp, preferred_element_type=F32) + b3f_ref[...]) \
        * jnp.exp(-t_norm * jnp.exp(rf_ref[...]))
    hb = (jnp.dot(a, w3b_ref[...], precision=hp, preferred_element_type=F32) + b3b_ref[...]) \
        * jnp.exp(-t_norm * jnp.exp(rb_ref[...]))
    inv = lax.rsqrt(jnp.sum(hf * hf, axis=0, keepdims=True) + jnp.sum(hb * hb, axis=0, keepdims=True))
    hf = hf * inv
    r = lax.broadcasted_iota(jnp.int32, (seq_len, 1), 0)
    hb = jnp.where(r == 0, 0.0, hb * inv)
    sh, sl = _split2(hf + hb)
    dh, dl = _split2(hf - hb)
    fhi = fhi_ref[...]
    flo = flo_ref[...]
    kc = _dot(fhi[:seq_len], sh) + _dot(fhi[:seq_len], sl) + _dot(flo[:seq_len], sh)
    ks = _dot(fhi[seq_len:], dh) + _dot(fhi[seq_len:], dl) + _dot(flo[seq_len:], dh)
    sign = jnp.where(r % 2 == 0, 1.0, -1.0)
    k_nyq = jnp.sum(sign * (hf + hb), axis=0, keepdims=True)
    ks = jnp.where(r == 0, k_nyq, ks)
    scale = jnp.where(r == 0, 1.0 / n, 2.0 / n)
    kf_ref[0:seq_len, :] = kc * scale
    kf_ref[seq_len:n, :] = ks * scale


def _hyena_filters(seq_len, w1, b1, w2, b2, w3, b3, freq, log_rate, f_tab):
    d = D_MODEL
    cb = 256
    t = jnp.arange(seq_len, dtype=F32)
    t_norm = t / (seq_len - 1)
    bands = jnp.linspace(1e-4, HY_BANDS - 1, HY_BANDS, dtype=F32)
    ang = (2.0 * np.pi / seq_len) * t[:, None] * bands[None, :]
    z = jnp.concatenate([t_norm[:, None], jnp.cos(ang), jnp.sin(ang)], axis=-1)
    kpad = 128 - HY_EMB
    z = jnp.pad(z, ((0, 0), (0, kpad)))
    w1p = jnp.pad(w1, ((0, kpad), (0, 0)))
    f_hi = f_tab.astype(BF16)
    f_lo = (f_tab - f_hi.astype(F32)).astype(BF16)
    n_cb = d // cb
    hh = HY_HIDDEN
    row = lambda a: a.reshape(1, -1)
    const = lambda shape: pl.BlockSpec(shape, lambda o, j: (0,) * len(shape))
    fwd = lambda rows: pl.BlockSpec((rows, cb), lambda o, j: (0, o * n_cb + j))
    bwd = lambda rows: pl.BlockSpec((rows, cb), lambda o, j: (0, (HY_ORDER + o) * n_cb + j))
    return pl.pallas_call(
        functools.partial(_filter_kernel, seq_len),
        grid=(HY_ORDER, n_cb),
        in_specs=[const((seq_len, 128)), const((128, hh)), const((1, hh)), const((hh, hh)), const((1, hh)),
                  fwd(hh), bwd(hh), fwd(1), bwd(1),
                  const((1, hh)), const((1, hh)), fwd(1), bwd(1),
                  const((2 * seq_len, seq_len)), const((2 * seq_len, seq_len))],
        out_specs=pl.BlockSpec((None, 2 * seq_len, cb), lambda o, j: (o, 0, j)),
        out_shape=jax.ShapeDtypeStruct((HY_ORDER, 2 * seq_len, d), F32),
        scratch_shapes=[pltpu.VMEM((seq_len, hh), F32)],
        compiler_params=_cparams("arbitrary", "arbitrary"),
        name=f"hyena_filter_l{seq_len}",
    )(z, w1p, row(b1), w2, row(b2), w3, w3, row(b3), row(b3),
      row(freq[0]), row(freq[1]), row(log_rate), row(log_rate), f_hi, f_lo)


def _hyena_kernel(seq_len, seqs, v_ref, x1_ref, x2_ref, cwv_ref, cw1_ref, cw2_ref, bias_ref, kf_ref,
                  f_ref, ft_ref, o_ref, z_ref, zb_ref, y_ref):
    kc = min(HY_FREQ_CHUNK, seq_len)
    n_k = seq_len // kc
    r = lax.broadcasted_iota(jnp.int32, (kc, 1), 0)
    gate_refs = ((x1_ref, cw1_ref), (x2_ref, cw2_ref))
    for s in range(seqs):
        rows = slice(s * seq_len, (s + 1) * seq_len)
        z_ref[s] = _short_conv3(v_ref[rows, :], cwv_ref[...])
    for o in range(HY_ORDER):
        for s in range(seqs):
            zb_ref[s] = z_ref[s].astype(BF16)
            y_ref[s] = jnp.zeros(y_ref.shape[1:], F32)

        def freq_chunk(j, carry):
            r0 = pl.multiple_of(j * kc, kc)
            k_cos = kf_ref[o, pl.ds(r0, kc), :]
            k_sin = kf_ref[o, pl.ds(seq_len + r0, kc), :]
            real_row = jnp.logical_and(r == 0, j == 0)
            for s in range(seqs):
                a = _dot(f_ref[pl.ds(r0, kc), :], zb_ref[s])
                bm = _dot(f_ref[pl.ds(seq_len + r0, kc), :], zb_ref[s])
                yc = a * k_cos - jnp.where(real_row, 0.0, bm * k_sin)
                ys = jnp.where(real_row, bm * k_sin, a * k_sin + bm * k_cos)
                y_ref[s] += _dot(ft_ref[j], yc.astype(BF16)) + _dot(ft_ref[n_k + j], ys.astype(BF16))
            return carry

        lax.fori_loop(0, n_k, freq_chunk, 0)
        x_ref, cw_ref = gate_refs[o]
        for s in range(seqs):
            rows = slice(s * seq_len, (s + 1) * seq_len)
            gate = _short_conv3(x_ref[rows, :], cw_ref[...])
            z_ref[s] = gate * (y_ref[s] + z_ref[s] * bias_ref[o:o + 1, :])
    for s in range(seqs):
        o_ref[s * seq_len:(s + 1) * seq_len, :] = z_ref[s]


def _hyena(u, conv_w, bias, kf, f_tab, n_seq, seq_len, tok_offset):
    d = D_MODEL
    cb = 256
    n_cb = d // cb
    seqs = max(1, HY_STEP_ROWS // seq_len)
    rows = seqs * seq_len
    row0 = tok_offset // rows
    kc = min(HY_FREQ_CHUNK, seq_len)
    n_k = seq_len // kc
    f_bf = f_tab.astype(BF16)
    ft = f_bf.T.reshape(seq_len, 2 * n_k, kc).transpose(1, 0, 2)

    def part(k):
        return pl.BlockSpec((rows, cb), lambda j, b: (row0 + b, k * n_cb + j))

    def cw(k):
        return pl.BlockSpec((3, cb), lambda j, b: (0, k * n_cb + j))

    return pl.pallas_call(
        functools.partial(_hyena_kernel, seq_len, seqs),
        grid=(n_cb, n_seq // seqs),
        in_specs=[part(0), part(1), part(2), cw(0), cw(1), cw(2),
                  pl.BlockSpec((HY_ORDER, cb), lambda j, b: (0, j)),
                  pl.BlockSpec((HY_ORDER, 2 * seq_len, cb), lambda j, b: (0, 0, j)),
                  pl.BlockSpec((2 * seq_len, seq_len), lambda j, b: (0, 0)),
                  pl.BlockSpec((2 * n_k, seq_len, kc), lambda j, b: (0, 0, 0))],
        out_specs=pl.BlockSpec((rows, cb), lambda j, b: (b, j)),
        out_shape=jax.ShapeDtypeStruct((n_seq * seq_len, d), F32),
        scratch_shapes=[pltpu.VMEM((seqs, seq_len, cb), F32), pltpu.VMEM((seqs, seq_len, cb), BF16),
                        pltpu.VMEM((seqs, seq_len, cb), F32)],
        compiler_params=_cparams("parallel", "parallel"),
        name=f"hyena_l{seq_len}",
    )(u, u, u, conv_w, conv_w, conv_w, bias, kf, f_bf, ft)


def _mix_out_kernel(n_parts, *refs):
    o_refs = refs[:2 * n_parts]
    w_refs = refs[2 * n_parts:3 * n_parts]
    x_ref, mod_ref, g_ref, wr_both_ref, br_ref = refs[3 * n_parts:3 * n_parts + 5]
    x1_ref, h_ref, eid_ref, wts_ref, rank_ref, cnt_ref, run_ref = refs[3 * n_parts + 5:]
    i = pl.program_id(0)
    t = ROUTE_TILE
    ne = N_EXPERTS

    from_prompt = i < N_PROMPT // ROUTE_TILE
    y = None
    for j in range(n_parts):
        o = jnp.where(from_prompt, o_refs[2 * j][...], o_refs[2 * j + 1][...])
        yj = _dot(o.astype(BF16), w_refs[j][...])
        y = yj if y is None else y + yj
    x1 = x_ref[...] + mod_ref[2:3, :] * y
    x1_ref[...] = x1
    h = _norm_mod(x1, g_ref[...], mod_ref[4:5, :], mod_ref[3:4, :])
    h_ref[...] = _pack_bf16_halves(h)
    h_hi = h.astype(BF16)
    h_lo = (h - h_hi.astype(F32)).astype(BF16)
    hi_terms = _dot(h_hi, wr_both_ref[...])
    logits = (hi_terms[:, :LANES] + hi_terms[:, LANES:] + _dot(h_lo, wr_both_ref[:, :LANES])).T[:ne] \
        + br_ref[...]

    @pl.when(i == 0)
    def _():
        run_ref[...] = jnp.zeros_like(run_ref)

    e_iota = lax.broadcasted_iota(jnp.int32, (ne, t), 0)
    vals, eids, onehots = [], [], []
    for _k in range(TOP_K):
        m = jnp.max(logits, axis=0, keepdims=True)
        eid = jnp.min(jnp.where(logits == m, e_iota, ne), axis=0, keepdims=True)
        sel = e_iota == eid
        logits = jnp.where(sel, -jnp.inf, logits)
        onehots.append(jnp.where(sel, 1.0, 0.0))
        vals.append(m)
        eids.append(eid)
    r2 = lax.broadcasted_iota(jnp.int32, (t, t), 0)
    c2 = lax.broadcasted_iota(jnp.int32, (t, t), 1)
    before = jnp.where(r2 < c2, 1.0, 0.0).astype(BF16)
    earlier = _dot(jnp.concatenate(onehots, axis=0).astype(BF16), before)
    running = run_ref[...]
    ranks = []
    for k, onehot in enumerate(onehots):
        ranks.append(jnp.sum(onehot * (running + earlier[k * ne:(k + 1) * ne]), axis=0, keepdims=True))
        running = running + jnp.sum(onehot, axis=1, keepdims=True)
    run_ref[...] = running
    cnt_ref[...] = running
    v = jnp.concatenate(vals, axis=0)
    ex = jnp.exp(v - v[0:1])
    wts_ref[...] = ex / jnp.sum(ex, axis=0, keepdims=True)
    eid_ref[...] = jnp.concatenate(eids, axis=0)
    rank_ref[...] = jnp.concatenate(ranks, axis=0).astype(jnp.int32)


def _mix_out(parts, x, mod_l, norm_g, w_router, b_router):
    d = D_MODEL
    t = ROUTE_TILE
    ne = N_EXPERTS
    n_parts = len(parts)
    n_prompt_tiles = N_PROMPT // t
    wr = jnp.pad(w_router, ((0, 0), (0, LANES - ne)))
    wr_hi = wr.astype(BF16)
    wr_both = jnp.concatenate([wr_hi, (wr - wr_hi.astype(F32)).astype(BF16)], axis=1)
    in_specs = []
    for o_p, _, _ in parts:
        in_specs.append(pl.BlockSpec((t, o_p.shape[1]), lambda i: (jnp.minimum(i, n_prompt_tiles - 1), 0)))
        in_specs.append(pl.BlockSpec((t, o_p.shape[1]), lambda i: (jnp.maximum(i - n_prompt_tiles, 0), 0)))
    in_specs += [pl.BlockSpec(w.shape, lambda i: (0, 0)) for _, _, w in parts]
    in_specs += [
        pl.BlockSpec((t, d), lambda i: (i, 0)),
        pl.BlockSpec((None, N_MOD, d), lambda i: (_tile_cond_row(i, t), 0, 0)),
        pl.BlockSpec((1, d), lambda i: (0, 0)),
        pl.BlockSpec((d, 2 * LANES), lambda i: (0, 0)),
        pl.BlockSpec((ne, 1), lambda i: (0, 0)),
    ]
    tok_major = pl.BlockSpec((TOP_K, t), lambda i: (0, i))
    return pl.pallas_call(
        functools.partial(_mix_out_kernel, n_parts),
        grid=(N_TOK // t,),
        in_specs=in_specs,
        out_specs=[pl.BlockSpec((t, d), lambda i: (i, 0)), pl.BlockSpec((t, d // 2), lambda i: (i, 0)),
                   tok_major, tok_major, tok_major, pl.BlockSpec((ne, 1), lambda i: (0, 0))],
        out_shape=[jax.ShapeDtypeStruct((N_TOK, d), F32), jax.ShapeDtypeStruct((N_TOK, d // 2), jnp.uint32),
                   jax.ShapeDtypeStruct((TOP_K, N_TOK), jnp.int32), jax.ShapeDtypeStruct((TOP_K, N_TOK), F32),
                   jax.ShapeDtypeStruct((TOP_K, N_TOK), jnp.int32), jax.ShapeDtypeStruct((ne, 1), F32)],
        scratch_shapes=[pltpu.VMEM((ne, 1), F32)],
        compiler_params=_cparams("arbitrary"),
        name="mix_out_router",
    )(*[o for part in parts for o in part[:2]], *[w.astype(BF16) for _, _, w in parts], x, mod_l,
      norm_g.reshape(1, d), wr_both, b_router.reshape(ne, 1))


def _sc_row_gather(src, idx):
    n = idx.shape[0]
    width = src.shape[1]
    step_rows = min(SC_MAX_INDICES, SC_STEP_BYTES // (width * 4))
    sc = plsc.get_sparse_core_info()
    n_workers = sc.num_cores * sc.num_subcores
    per_worker = n // n_workers
    if n % n_workers or per_worker % step_rows:
        raise ValueError("row count must be whole SparseCore steps on every subcore")
    mesh = plsc.VectorSubcoreMesh(core_axis_name="core", subcore_axis_name="subcore")

    @functools.partial(pl.kernel, out_type=jax.ShapeDtypeStruct((n, width), src.dtype), mesh=mesh,
                       scratch_types=[pltpu.VMEM((per_worker,), jnp.int32),
                                      pltpu.VMEM((step_rows, width), src.dtype)],
                       name="sc_row_gather")
    def gather(src_hbm, idx_hbm, dst_hbm, idx_vmem, rows_vmem):
        worker = lax.axis_index("subcore") * sc.num_cores + lax.axis_index("core")
        base = worker * per_worker
        pltpu.sync_copy(idx_hbm.at[pl.ds(base, per_worker)], idx_vmem)

        @pl.loop(0, per_worker // step_rows)
        def _(c):
            pltpu.sync_copy(src_hbm.at[idx_vmem.at[pl.ds(c * step_rows, step_rows)]], rows_vmem)
            pltpu.sync_copy(rows_vmem, dst_hbm.at[pl.ds(base + c * step_rows, step_rows)])

    return gather(src, idx)


def _sc_row_scatter(src, rows_of_src):
    copies, n_src = rows_of_src.shape
    width = src.shape[1]
    step_rows = min(SC_MAX_INDICES, SC_STEP_BYTES // (width * 4))
    sc = plsc.get_sparse_core_info()
    n_workers = sc.num_cores * sc.num_subcores
    per_worker = n_src // n_workers
    steps = per_worker // step_rows
    if src.shape[0] != n_src or n_src % n_workers or per_worker % step_rows:
        raise ValueError("row counts must be whole SparseCore steps on every subcore")
    mesh = plsc.VectorSubcoreMesh(core_axis_name="core", subcore_axis_name="subcore")
    idx = rows_of_src.reshape(copies, n_workers, steps, step_rows).transpose(1, 2, 0, 3)

    @functools.partial(pl.kernel, out_type=jax.ShapeDtypeStruct((copies * n_src, width), src.dtype),
                       mesh=mesh,
                       scratch_types=[pltpu.VMEM((steps, copies, step_rows), jnp.int32),
                                      pltpu.VMEM((step_rows, width), src.dtype)],
                       name="sc_row_scatter")
    def scatter(src_hbm, idx_hbm, dst_hbm, idx_vmem, rows_vmem):
        worker = lax.axis_index("subcore") * sc.num_cores + lax.axis_index("core")
        pltpu.sync_copy(idx_hbm.at[worker], idx_vmem)

        @pl.loop(0, steps)
        def _(c):
            pltpu.sync_copy(src_hbm.at[pl.ds(worker * per_worker + c * step_rows, step_rows)], rows_vmem)
            for j in range(copies):
                pltpu.sync_copy(rows_vmem, dst_hbm.at[idx_vmem.at[c, j]])

    return scatter(src, idx)


def _pack_bf16_halves(x):
    w = x.shape[1] // 2
    bits = pltpu.bitcast(x.astype(BF16).astype(F32), jnp.uint32)
    return bits[:, :w] | (bits[:, w:] >> 16)


def _unpack_bf16_halves(p):
    hi = pltpu.bitcast(p & jnp.uint32(0xFFFF0000), F32).astype(BF16)
    lo = pltpu.bitcast(p << 16, F32).astype(BF16)
    return hi, lo


def _experts_kernel(layer, te_ref, first_ref, slot_ref, next_ref, rows_ref, nv_ref,
                    x_ref, wgu_hbm, bg_ref, bu_ref, wd_hbm, bd_ref, sel_ref,
                    y_ref, wgu_buf, wd_buf, wg_ref, wu_ref, wdb_ref, sem):
    i = pl.program_id(0)
    valid = i < nv_ref[0]
    half = DEINT_COLS // 2
    k_half = x_ref.shape[1]

    def fetch(expert, slot):
        return (pltpu.make_async_copy(wgu_hbm.at[layer, expert], wgu_buf.at[slot], sem.at[slot, 0]),
                pltpu.make_async_copy(wd_hbm.at[layer, expert], wd_buf.at[slot], sem.at[slot, 1]))

    @pl.when(i == 0)
    def _():
        for cp in fetch(te_ref[0], 0):
            cp.start()

    @pl.when(jnp.logical_and(valid, first_ref[i] == 1))
    def _():
        slot = slot_ref[i]
        for cp in fetch(te_ref[i], slot):
            cp.wait()

        @pl.when(next_ref[i] >= 0)
        def _():
            for cp in fetch(next_ref[i], 1 - slot):
                cp.start(priority=1)

        for c in range(wgu_buf.shape[2] // DEINT_COLS):
            w = wgu_buf[slot, :, c * DEINT_COLS:(c + 1) * DEINT_COLS].astype(BF16)
            split = _dot(w, sel_ref[...]).astype(BF16)
            wg_ref[:, c * half:(c + 1) * half] = split[:, :half]
            wu_ref[:, c * half:(c + 1) * half] = split[:, half:]
        wdb_ref[...] = wd_buf[slot].astype(BF16)

    def ffn(rows):
        x_l, x_r = _unpack_bf16_halves(x_ref[:rows, :])
        gl = _dot(x_l, wg_ref[:k_half, :]) + _dot(x_r, wg_ref[k_half:, :]) + bg_ref[...]
        up = _dot(x_l, wu_ref[:k_half, :]) + _dot(x_r, wu_ref[k_half:, :]) + bu_ref[...]
        gl = jnp.minimum(gl, SWIGLU_LIMIT)
        up = jnp.clip(up, -SWIGLU_LIMIT, SWIGLU_LIMIT)
        act = (up + 1.0) * gl * _sigmoid(SWIGLU_ALPHA * gl)
        y_ref[:rows, :] = _pack_bf16_halves(_dot(act.astype(BF16), wdb_ref[...]) + bd_ref[...])

    tm = x_ref.shape[0]
    few = rows_ref[i] <= tm // 2

    @pl.when(jnp.logical_and(valid, jnp.logical_not(few)))
    def _():
        ffn(tm)

    @pl.when(jnp.logical_and(valid, few))
    def _():
        ffn(tm // 2)
        y_ref[tm // 2:, :] = jnp.zeros((tm - tm // 2, y_ref.shape[1]), y_ref.dtype)

    @pl.when(jnp.logical_not(valid))
    def _():
        y_ref[...] = jnp.zeros_like(y_ref)


def _experts(xs, tile_expert, tile_first, tile_rows, n_valid, layer, w_gu, b_gate, b_up, w_down, b_down):
    d = D_MODEL
    tm = MOE_TILE
    n_tiles = MOE_ROWS // tm
    ff = w_down.shape[2]
    half = DEINT_COLS // 2
    r = jnp.arange(DEINT_COLS)[:, None]
    c = jnp.arange(DEINT_COLS)[None, :]
    sel = (r == jnp.where(c < half, 2 * c, 2 * (c - half) + 1)).astype(BF16)
    group = jnp.cumsum(tile_first) - 1
    tile_slot = (group % 2).astype(jnp.int32)
    is_last_group = group == group[-1]
    following = jnp.concatenate([tile_expert[1:], tile_expert[-1:]])
    idx = jnp.arange(n_tiles, dtype=jnp.int32)
    group_end = jnp.max(jnp.where(group[None, :] == group[:, None], idx[None, :], -1), axis=1)
    tile_next = jnp.where(is_last_group, -1, following[group_end]).astype(jnp.int32)
    wspec = lambda k, n: pl.BlockSpec((None, None, k, n), lambda i, *_: (layer, _[0][i], 0, 0))
    grid_spec = pltpu.PrefetchScalarGridSpec(
        num_scalar_prefetch=6,
        grid=(n_tiles,),
        in_specs=[pl.BlockSpec((tm, d // 2), lambda i, *_: (i, 0)),
                  pl.BlockSpec(memory_space=pl.ANY), wspec(1, ff), wspec(1, ff),
                  pl.BlockSpec(memory_space=pl.ANY), wspec(1, d),
                  pl.BlockSpec((DEINT_COLS, DEINT_COLS), lambda i, *_: (0, 0))],
        out_specs=pl.BlockSpec((tm, d // 2), lambda i, *_: (i, 0)),
        scratch_shapes=[pltpu.VMEM((2, d, 2 * ff), F32), pltpu.VMEM((2, ff, d), F32),
                        pltpu.VMEM((d, ff), BF16), pltpu.VMEM((d, ff), BF16), pltpu.VMEM((ff, d), BF16),
                        pltpu.SemaphoreType.DMA((2, 2))],
    )
    return pl.pallas_call(
        functools.partial(_experts_kernel, layer),
        grid_spec=grid_spec,
        out_shape=jax.ShapeDtypeStruct((MOE_ROWS, d // 2), jnp.uint32),
        compiler_params=_cparams("arbitrary"),
        name="experts",
    )(tile_expert, tile_first, tile_slot, tile_next, tile_rows, n_valid, xs, w_gu, b_gate, b_up, w_down,
      b_down, sel)


def _combine_kernel(final, x_ref, g_ref, w_ref, mod_ref, fg_ref, o_ref):
    y_l = y_r = None
    for k in range(TOP_K):
        g_l, g_r = _unpack_bf16_halves(g_ref[k])
        wk = w_ref[:, k:k + 1]
        y_l = wk * g_l.astype(F32) if y_l is None else y_l + wk * g_l.astype(F32)
        y_r = wk * g_r.astype(F32) if y_r is None else y_r + wk * g_r.astype(F32)
    x = x_ref[...] + mod_ref[5:6, :] * jnp.concatenate([y_l, y_r], axis=1)
    if final:
        x = x * lax.rsqrt(jnp.mean(x * x, axis=-1, keepdims=True) + RMS_EPS) * fg_ref[...]
    o_ref[...] = x


def _combine(x1, gathered, wts, mod_l, final_g, final, tok0, n_tok):
    d = D_MODEL
    t = ROUTE_TILE
    tile0 = tok0 // t
    return pl.pallas_call(
        functools.partial(_combine_kernel, final),
        grid=(n_tok // t,),
        in_specs=[pl.BlockSpec((t, d), lambda i: (tile0 + i, 0)),
                  pl.BlockSpec((TOP_K, t, d // 2), lambda i: (0, i, 0)),
                  pl.BlockSpec((t, TOP_K), lambda i: (tile0 + i, 0)),
                  pl.BlockSpec((None, N_MOD, d), lambda i: (_tile_cond_row(tile0 + i, t), 0, 0)),
                  pl.BlockSpec((1, d), lambda i: (0, 0))],
        out_specs=pl.BlockSpec((t, d), lambda i: (i, 0)),
        out_shape=jax.ShapeDtypeStruct((n_tok, d), F32),
        compiler_params=_cparams("parallel"),
        name="moe_combine",
    )(x1, gathered, wts, mod_l, final_g.reshape(1, d))


def _moe(x1, h, eid, wts, rank, counts, mod_l, layer, w_gu, b_gu, w_down, b_down, final_g, final):
    d = D_MODEL
    tm = MOE_TILE
    n_tiles = MOE_ROWS // tm
    cnt = counts.reshape(N_EXPERTS).astype(jnp.int32)
    gsz = ((cnt + tm - 1) // tm) * tm
    ends = jnp.cumsum(gsz)
    offs = ends - gsz
    e_ids = jnp.arange(N_EXPERTS, dtype=jnp.int32)
    pos = jnp.sum(jnp.where(eid[..., None] == e_ids, offs, 0), axis=-1) + rank
    tile_start = jnp.arange(n_tiles, dtype=jnp.int32) * tm
    tile_expert = jnp.minimum(jnp.sum((ends[None, :] <= tile_start[:, None]).astype(jnp.int32), axis=1),
                              N_EXPERTS - 1)
    n_valid = (ends[-1:] // tm).astype(jnp.int32)
    last_valid = jnp.maximum(n_valid[0] - 1, 0)
    tile_expert = jnp.where(jnp.arange(n_tiles) < n_valid[0], tile_expert, tile_expert[last_valid])
    tile_first = jnp.concatenate([jnp.ones((1,), jnp.int32),
                                  (tile_expert[1:] != tile_expert[:-1]).astype(jnp.int32)])
    j = jnp.arange(tm, dtype=jnp.int32)[None, :]
    pad_used = j < (gsz - cnt)[:, None]
    n_unused_before = jnp.cumsum((~pad_used).reshape(-1).astype(jnp.int32)) - 1
    pad_pos = jnp.where(pad_used, (offs + cnt)[:, None] + j,
                        ends[-1] + n_unused_before.reshape(N_EXPERTS, tm))
    if N_EXPERTS * tm != N_TOK:
        raise ValueError("the filler slots are laid out as one extra copy of every token")

    xs = _sc_row_scatter(h, jnp.concatenate([pos, pad_pos.reshape(1, N_TOK)], axis=0))
    real_end = (offs + cnt)[tile_expert]
    tile_rows = jnp.clip(real_end - tile_start, 0, tm).astype(jnp.int32)
    ys = _experts(xs, tile_expert, tile_first, tile_rows, n_valid, layer, w_gu,
                  b_gu[:, :, None, 0::2], b_gu[:, :, None, 1::2], w_down, b_down[:, :, None, :])
    spans = ((0, N_PROMPT), (N_PROMPT, N_SAMPLE)) if final else ((0, N_TOK),)
    outs = []
    for tok0, n_tok in spans:
        gathered = _sc_row_gather(ys, pos[:, tok0:tok0 + n_tok].reshape(-1))
        outs.append(_combine(x1, gathered.reshape(TOP_K, n_tok, d // 2), wts.T, mod_l, final_g, final,
                             tok0, n_tok))
    return tuple(outs) if final else outs[0]


def _grid_positions(n_tok, d):
    rows = n_tok // GRID_W
    r, col = jnp.meshgrid(jnp.arange(rows, dtype=F32), jnp.arange(GRID_W, dtype=F32), indexing='ij')
    r = r.reshape(-1)
    col = col.reshape(-1)
    quarter = d // 4
    inv = 1.0 / (10000.0 ** (jnp.arange(quarter, dtype=F32) / quarter))
    ar = r[:, None] * inv[None]
    ac = col[:, None] * inv[None]
    return jnp.concatenate([jnp.sin(ar), jnp.cos(ar), jnp.sin(ac), jnp.cos(ac)], axis=-1)


def kernel(x_prompt, x_sample, state_hgrn, state_mlstm_c, state_mlstm_n, state_mlstm_m, c, c_ctx,
           norm_g, final_g, w_mod, b_mod, ev_w_in, ev_gate_b, ev_conv, hg_lb, ev_w_out,
           hy_w_in, hy_conv, hy_w1, hy_b1, hy_w2, hy_b2, hy_w3, hy_b3, hy_freq, hy_log_rate, hy_bias, hy_w_out,
           w_router, b_router, w_gu, b_gu, w_down, b_down):
    d = D_MODEL
    hd = HEAD_DIM
    cond = jnp.concatenate([c_ctx[None], c, jnp.zeros((N_COND - 1 - DEC_BATCH, d), F32)], axis=0)
    mod = _modulation(cond, w_mod, b_mod)
    pos_tab = jnp.concatenate([jnp.zeros((TOK_TILE, d), F32), _grid_positions(DEC_SEQ, d)], axis=0)

    groups = ((BATCH, SEQ, 0), (DEC_BATCH, DEC_SEQ, N_PROMPT))
    new_states = None
    for l in range(DEPTH):
        if l % 2 == 0:
            e = l // 2
            if l == 0:
                x, p, gate_t = _proj_even(x_prompt.reshape(N_PROMPT, d), x_sample.reshape(N_SAMPLE, d),
                                          pos_tab, mod[l], norm_g[l, 0], ev_w_in[e], ev_gate_b[e])
            else:
                raise NotImplementedError("only the first layer adds grid positions")
            gate_t_h = gate_t.reshape(4, HEADS, N_TOK).transpose(1, 0, 2)
            o_hg, o_ml = [], []
            for gi, (n_seq, seq_len, off) in enumerate(groups):
                if gi == 0:
                    s0 = jnp.zeros((n_seq, 2, HEADS, hd, hd), F32)
                    c0 = jnp.zeros((n_seq, 2, HEADS, hd, hd), F32)
                    n0 = jnp.zeros((n_seq, 2, HEADS, 1, hd), F32)
                    m0 = jnp.zeros((n_seq, 2, HEADS, 1, 1), F32)
                else:
                    s0 = state_hgrn[:, e]
                    c0 = state_mlstm_c[:, e]
                    n0 = state_mlstm_n[:, e].reshape(n_seq, 2, HEADS, 1, hd)
                    m0 = state_mlstm_m[:, e].reshape(n_seq, 2, HEADS, 1, 1)
                og, s_fin = _hgrn(p, hg_lb, l, s0, n_seq, seq_len, off)
                om, c_fin, n_fin, m_fin = _mlstm(p, gate_t_h, ev_conv[e], c0, n0, m0, n_seq, seq_len, off)
                o_hg.append(og)
                o_ml.append(om)
                if gi == 0:
                    new_states = (s_fin[:, None], c_fin[:, None],
                                  n_fin.reshape(n_seq, 1, 2, HEADS, hd), m_fin.reshape(n_seq, 1, 2, HEADS))
            parts = [(*o_hg, ev_w_out[e][:GROUP_W]), (*o_ml, ev_w_out[e][GROUP_W:])]
        else:
            o = l // 2
            u = _proj_odd(x, mod[l], norm_g[l, 0], hy_w_in[o])
            zs = []
            for n_seq, seq_len, off in groups:
                f_tab = _dft_tables(seq_len)
                kf = _hyena_filters(seq_len, hy_w1[o], hy_b1[o], hy_w2[o], hy_b2[o], hy_w3[o], hy_b3[o],
                                    hy_freq[o], hy_log_rate[o], f_tab)
                zs.append(_hyena(u, hy_conv[o], hy_bias[o], kf, f_tab, n_seq, seq_len, off))
            parts = [(*zs, hy_w_out[o])]
        x1, h, eid, wts, rank, counts = _mix_out(parts, x, mod[l], norm_g[l, 1], w_router[l], b_router[l])
        x = _moe(x1, h, eid, wts, rank, counts, mod[l], l, w_gu, b_gu, w_down, b_down,
                 final_g, final=(l == DEPTH - 1))

    y_prompt, y_sample = x
    return (y_prompt.reshape(BATCH, SEQ, d), y_sample.reshape(DEC_BATCH, DEC_SEQ, d)) + new_states
```

```python
import functools

import numpy as np
import jax
import jax.numpy as jnp
from jax import lax
from jax.experimental import pallas as pl
from jax.experimental.pallas import tpu as pltpu
from jax.experimental.pallas import tpu_sc as plsc

F32 = jnp.float32
BF16 = jnp.bfloat16

D_MODEL = 1024
BATCH = 32
SEQ = 256
DEPTH = 2
DEC_BATCH = 8
DEC_SEQ = 1024
GRID_W = 64
RMS_EPS = 1e-6
N_MOD = 6
LOG2_E = 1.4426950408889634
LANES = 128

HEADS = 4
HEAD_DIM = 128
GROUP_W = HEADS * HEAD_DIM
N_GATES = 4 * HEADS
EVEN_MAIN = 9 * GROUP_W

HY_ORDER = 2
HY_BANDS = 16
HY_EMB = 1 + 2 * HY_BANDS
HY_HIDDEN = 64
HY_FREQ_CHUNK = 512
HY_STEP_ROWS = 1024

N_EXPERTS = 32
TOP_K = 4
SWIGLU_LIMIT = 7.0
SWIGLU_ALPHA = 1.702

N_PROMPT = BATCH * SEQ
N_SAMPLE = DEC_BATCH * DEC_SEQ
N_TOK = N_PROMPT + N_SAMPLE
N_COND = 16

TOK_TILE = 512
ROUTE_TILE = 512
SCAN_CHUNK = 128
SUB = 16
HALF = 8
HG_STEP_HEADS = 4
MOE_TILE = 512
MOE_TILE_PARTS = 4
MOE_ROWS = N_TOK * TOP_K + N_EXPERTS * MOE_TILE
DEINT_COLS = 256
SC_STEP_BYTES = 256 * 1024
SC_MAX_INDICES = 128

VMEM_LIMIT = 56 * 1024 * 1024


def _cparams(*sem):
    return pltpu.CompilerParams(dimension_semantics=sem, vmem_limit_bytes=VMEM_LIMIT)


def _split3(x):
    hi = x.astype(BF16)
    r = x - hi.astype(F32)
    mid = r.astype(BF16)
    lo = (r - mid.astype(F32)).astype(BF16)
    return hi, mid, lo


def _dot(a, b):
    return jnp.dot(a, b, preferred_element_type=F32)


def _dot_nt(a, b):
    return lax.dot_general(a, b, (((1,), (1,)), ((), ())), preferred_element_type=F32)


def _dot_tn(a, b):
    return lax.dot_general(a, b, (((0,), (0,)), ((), ())), preferred_element_type=F32)


def _dot_w3(a_exact_bf16, x):
    hi, mid, lo = _split3(x)
    return _dot(a_exact_bf16, hi) + _dot(a_exact_bf16, mid) + _dot(a_exact_bf16, lo)


def _sigmoid(x):
    return 1.0 / (1.0 + jnp.exp(-x))


def _silu(x):
    return x * _sigmoid(x)


def _log_sigmoid(x):
    return jnp.minimum(x, 0.0) - jnp.log(1.0 + jnp.exp(-jnp.abs(x)))


def _tile_cond_row(i, tile=TOK_TILE):
    n_prompt_tiles = N_PROMPT // tile
    tiles_per_seq = DEC_SEQ // tile
    return jnp.where(i < n_prompt_tiles, 0, 1 + (i - n_prompt_tiles) // tiles_per_seq)


def _mod_kernel(cond_ref, w_ref, b_ref, o_ref):
    a = _silu(cond_ref[...]).astype(BF16)
    o_ref[...] = _dot(a, w_ref[...].astype(BF16)) + b_ref[...]


def _modulation(cond, w_mod, b_mod):
    d = D_MODEL
    out = pl.pallas_call(
        _mod_kernel,
        grid=(DEPTH, N_MOD),
        in_specs=[
            pl.BlockSpec((N_COND, d), lambda l, j: (0, 0)),
            pl.BlockSpec((None, d, d), lambda l, j: (l, 0, j)),
            pl.BlockSpec((None, 1, d), lambda l, j: (l, 0, j)),
        ],
        out_specs=pl.BlockSpec((None, None, N_COND, d), lambda l, j: (l, j, 0, 0)),
        out_shape=jax.ShapeDtypeStruct((DEPTH, N_MOD, N_COND, d), F32),
        compiler_params=_cparams("parallel", "parallel"),
        name="modulation",
    )(cond, w_mod, b_mod.reshape(DEPTH, 1, N_MOD * d))
    return out.transpose(0, 2, 1, 3)


def _norm_mod(x, g_row, scale_row, shift_row):
    ms = jnp.mean(x * x, axis=-1, keepdims=True)
    y = x * lax.rsqrt(ms + RMS_EPS) * g_row
    return y * (1.0 + scale_row) + shift_row


def _proj_even_kernel(xp_ref, xs_ref, pos_ref, mod_ref, g_ref, w_ref, wgt_ref, gbt_ref,
                      xres_ref, p_ref, gate_t_ref):
    from_prompt = pl.program_id(0) < N_PROMPT // TOK_TILE
    x = jnp.where(from_prompt, xp_ref[...], xs_ref[...]) + pos_ref[...]
    xres_ref[...] = x
    h = _norm_mod(x, g_ref[...], mod_ref[1:2, :], mod_ref[0:1, :]).astype(BF16)
    p_ref[...] = _dot(h, w_ref[...])
    gate_t_ref[...] = _dot_nt(wgt_ref[...], h) + gbt_ref[...]


def _proj_even(x_prompt, x_sample, pos_tab, mod_l, norm_g, w_in, gate_b):
    d = D_MODEL
    n_tiles = N_TOK // TOK_TILE
    n_prompt_tiles = N_PROMPT // TOK_TILE
    tiles_per_seq = DEC_SEQ // TOK_TILE
    w_main = w_in[:, :EVEN_MAIN].astype(BF16)
    w_gate = w_in[:, EVEN_MAIN:].astype(BF16)

    def pos_map(i):
        return (jnp.where(i < n_prompt_tiles, 0, 1 + (i - n_prompt_tiles) % tiles_per_seq), 0)

    return pl.pallas_call(
        _proj_even_kernel,
        grid=(n_tiles,),
        in_specs=[
            pl.BlockSpec((TOK_TILE, d), lambda i: (jnp.minimum(i, n_prompt_tiles - 1), 0)),
            pl.BlockSpec((TOK_TILE, d), lambda i: (jnp.maximum(i - n_prompt_tiles, 0), 0)),
            pl.BlockSpec((TOK_TILE, d), pos_map),
            pl.BlockSpec((None, N_MOD, d), lambda i: (_tile_cond_row(i), 0, 0)),
            pl.BlockSpec((1, d), lambda i: (0, 0)),
            pl.BlockSpec((d, EVEN_MAIN), lambda i: (0, 0), pipeline_mode=pl.Buffered(1)),
            pl.BlockSpec((N_GATES, d), lambda i: (0, 0)),
            pl.BlockSpec((N_GATES, 1), lambda i: (0, 0)),
        ],
        out_specs=[
            pl.BlockSpec((TOK_TILE, d), lambda i: (i, 0)),
            pl.BlockSpec((TOK_TILE, EVEN_MAIN), lambda i: (i, 0)),
            pl.BlockSpec((N_GATES, TOK_TILE), lambda i: (0, i)),
        ],
        out_shape=[
            jax.ShapeDtypeStruct((N_TOK, d), F32),
            jax.ShapeDtypeStruct((N_TOK, EVEN_MAIN), F32),
            jax.ShapeDtypeStruct((N_GATES, N_TOK), F32),
        ],
        compiler_params=_cparams("parallel"),
        name="proj_even",
    )(x_prompt, x_sample, pos_tab, mod_l, norm_g.reshape(1, d), w_main, w_gate.T, gate_b.reshape(N_GATES, 1))


def _proj_odd_kernel(x_ref, mod_ref, g_ref, w_ref, p_ref):
    h = _norm_mod(x_ref[...], g_ref[...], mod_ref[1:2, :], mod_ref[0:1, :]).astype(BF16)
    p_ref[...] = _dot(h, w_ref[...])


def _proj_odd(x, mod_l, norm_g, w_in):
    d = D_MODEL
    width = w_in.shape[1]
    return pl.pallas_call(
        _proj_odd_kernel,
        grid=(N_TOK // TOK_TILE,),
        in_specs=[
            pl.BlockSpec((TOK_TILE, d), lambda i: (i, 0)),
            pl.BlockSpec((None, N_MOD, d), lambda i: (_tile_cond_row(i), 0, 0)),
            pl.BlockSpec((1, d), lambda i: (0, 0)),
            pl.BlockSpec((d, width), lambda i: (0, 0), pipeline_mode=pl.Buffered(1)),
        ],
        out_specs=pl.BlockSpec((TOK_TILE, width), lambda i: (i, 0)),
        out_shape=jax.ShapeDtypeStruct((N_TOK, width), F32),
        compiler_params=_cparams("parallel"),
        name="proj_odd",
    )(x, mod_l, norm_g.reshape(1, d), w_in.astype(BF16))


def _hgrn_chunks(chains):
    c = SCAN_CHUNK
    n = range(len(chains))
    rev, q, k, v, lf, st = zip(*chains)
    row = lax.broadcasted_iota(jnp.int32, (c, c), 0)
    col = lax.broadcasted_iota(jnp.int32, (c, c), 1)
    tris = {False: col <= row, True: col >= row}
    tris_b = {r: jnp.where(t, 1.0, 0.0).astype(BF16) for r, t in tris.items()}
    b = [_dot_w3(tris_b[rev[j]], lf[j]) for j in n]
    b2 = [b[j] * LOG2_E for j in n]
    lane_half = lax.broadcasted_iota(jnp.int32, (HALF, c), 1)
    chunk_row = lax.broadcasted_iota(jnp.int32, (c, 1), 0)
    rows = [[] for _ in n]
    for i in range(c // SUB):
        lo, hi = i * SUB, (i + 1) * SUB
        a_row = []
        for j in n:
            if rev[j]:
                has_off, edge, outside = hi < c, hi, chunk_row >= hi
            else:
                has_off, edge, outside = lo > 0, lo - 1, chunk_row < lo
            if has_off:
                beta = b[j][edge:edge + 1]
                qs = q[j][lo:hi] * jnp.exp(b[j][lo:hi] - beta)
                ks = k[j] * jnp.exp(jnp.where(outside, beta - b[j], -jnp.inf))
                a_row.append(_dot_nt(qs.astype(BF16), ks.astype(BF16)))
            else:
                a_row.append(jnp.zeros((SUB, c), F32))
        for half in range(SUB // HALF):
            h0 = lo + half * HALF
            piece = [a_row[j][half * HALF:(half + 1) * HALF] for j in n]
            for s in range(HALF):
                for j in n:
                    bh2 = b2[j][h0:h0 + HALF]
                    a_col = jnp.sum(jnp.exp2(bh2 - bh2[s:s + 1]) * q[j][h0:h0 + HALF] * k[j][h0 + s:h0 + s + 1],
                                    axis=-1, keepdims=True)
                    piece[j] = jnp.where(lane_half == h0 + s, a_col, piece[j])
            for j in n:
                rows[j].append(piece[j])
    second_half = (chunk_row % SUB) >= HALF
    same_block = (row // SUB) == (col // SUB)
    out = []
    for j in n:
        meet = HALF if rev[j] else HALF - 1
        beta = jnp.concatenate([jnp.broadcast_to(b[j][lo + meet:lo + meet + 1], (SUB, b[j].shape[1]))
                                for lo in range(0, c, SUB)], axis=0)
        t_side = jnp.logical_not(second_half) if rev[j] else second_half
        qs = q[j] * jnp.exp(jnp.where(t_side, b[j] - beta, -jnp.inf))
        ks = k[j] * jnp.exp(jnp.where(t_side, -jnp.inf, beta - b[j]))
        cross = jnp.where(same_block, _dot_nt(qs.astype(BF16), ks.astype(BF16)), 0.0)
        attn = jnp.where(tris[rev[j]], jnp.concatenate(rows[j], axis=0) + cross, 0.0)
        o = _dot(attn.astype(BF16), v[j].astype(BF16)) \
            + _dot_nt((q[j] * jnp.exp(b[j])).astype(BF16), st[j].astype(BF16))
        b_exit = b[j][0:1] if rev[j] else b[j][c - 1:c]
        k_out = k[j] * jnp.exp(b_exit - b[j])
        st_new = jnp.exp(b_exit) * st[j] + _dot_tn(v[j].astype(BF16), k_out.astype(BF16))
        out.append((o, st_new))
    return out


def _hgrn_kernel(seq_len, layer, q_ref, i_ref, g_ref, ff_ref, fb_ref, lb_ref, s0_ref,
                 o_ref, s_out_ref, of_ref, ob_ref, st_ref):
    c = SCAN_CHUNK
    hd = HEAD_DIM
    n_chunks = seq_len // c
    lbp = lb_ref[...]
    e = jnp.exp(lbp - jnp.max(lbp, axis=0, keepdims=True))
    lb = jnp.sum(e[0:layer + 1], axis=0, keepdims=True) / jnp.sum(e, axis=0, keepdims=True)

    for d in range(2):
        for hh in range(HG_STEP_HEADS):
            st_ref[d, hh] = s0_ref[d, hh].T

    def body(n, carry):
        where, chains = [], []
        for d in range(2):
            sl = pl.ds(pl.multiple_of((n_chunks - 1 - n if d else n) * c, c), c)
            for hh in range(HG_STEP_HEADS):
                cols = slice(hh * hd, (hh + 1) * hd)
                f = lb[:, cols] + (1.0 - lb[:, cols]) * _sigmoid((fb_ref if d else ff_ref)[sl, cols])
                where.append((d, hh, sl, cols))
                chains.append((bool(d), q_ref[sl, cols], 1.0 - f, i_ref[sl, cols], jnp.log(f), st_ref[d, hh]))
        for (d, hh, sl, cols), (o, st_new) in zip(where, _hgrn_chunks(chains)):
            st_ref[d, hh] = st_new
            (ob_ref if d else of_ref)[sl, cols] = o
        return carry

    lax.fori_loop(0, n_chunks, body, 0)
    for hh in range(HG_STEP_HEADS):
        cols = slice(hh * hd, (hh + 1) * hd)
        o = of_ref[:, cols] + ob_ref[:, cols]
        o = o * lax.rsqrt(jnp.mean(o * o, axis=-1, keepdims=True) + RMS_EPS)
        o_ref[:, cols] = o * _silu(g_ref[:, cols])
        for d in range(2):
            s_out_ref[d, hh] = st_ref[d, hh].T


def _hgrn(p, hg_lb, layer, s0, n_seq, seq_len, tok_offset):
    hd = HEAD_DIM
    sh = HG_STEP_HEADS
    steps_per_seq = HEADS // sh
    row0 = tok_offset // seq_len

    def col(part):
        return pl.BlockSpec((seq_len, sh * hd), lambda b, h: (row0 + b, part * steps_per_seq + h))

    state_spec = pl.BlockSpec((None, 2, sh, hd, hd), lambda b, h: (b, 0, h, 0, 0))
    return pl.pallas_call(
        functools.partial(_hgrn_kernel, seq_len, layer),
        grid=(n_seq, steps_per_seq),
        in_specs=[col(0), col(1), col(2), col(3), col(4),
                  pl.BlockSpec((DEPTH + 1, sh * hd), lambda b, h: (0, h)),
                  state_spec],
        out_specs=[pl.BlockSpec((seq_len, sh * hd), lambda b, h: (b, h)), state_spec],
        out_shape=[jax.ShapeDtypeStruct((n_seq * seq_len, GROUP_W), F32),
                   jax.ShapeDtypeStruct((n_seq, 2, HEADS, hd, hd), F32)],
        scratch_shapes=[pltpu.VMEM((seq_len, sh * hd), F32), pltpu.VMEM((seq_len, sh * hd), F32),
                        pltpu.VMEM((2, sh, hd, hd), F32)],
        compiler_params=_cparams("parallel", "parallel"),
        name=f"hgrn_l{seq_len}",
    )(p, p, p, p, p, hg_lb, s0)


def _short_conv3(x, w):
    n = x.shape[0]
    r = lax.broadcasted_iota(jnp.int32, (n, 1), 0)
    prev = jnp.where(r == 0, 0.0, pltpu.roll(x, 1, 0))
    nxt = jnp.where(r == n - 1, 0.0, pltpu.roll(x, n - 1, 0))
    return prev * w[0:1] + x * w[1:2] + nxt * w[2:3]


def _conv3_silu_tiles(dst_ref, src_ref, w_ref, scale):
    n, width = src_ref.shape
    c = SCAN_CHUNK
    r = lax.broadcasted_iota(jnp.int32, (c, 1), 0)
    for j in range(n // c):
        r0 = j * c
        for col in range(0, width, LANES):
            cols = slice(col, col + LANES)
            cur = src_ref[r0:r0 + c, cols]
            prev = jnp.where(r == 0, 0.0, pltpu.roll(cur, 1, 0)) if j == 0 else src_ref[r0 - 1:r0 + c - 1, cols]
            nxt = (jnp.where(r == c - 1, 0.0, pltpu.roll(cur, c - 1, 0)) if r0 + c == n
                   else src_ref[r0 + 1:r0 + c + 1, cols])
            w = w_ref[:, cols]
            y = _silu(prev * w[0:1] + cur * w[1:2] + nxt * w[2:3])
            dst_ref[r0:r0 + c, cols] = y if scale == 1.0 else y * scale


def _split2(x):
    hi = x.astype(BF16)
    return hi, (x - hi.astype(F32)).astype(BF16)


def _mlstm_chunks(chains):
    c = SCAN_CHUNK
    row = lax.broadcasted_iota(jnp.int32, (c, c), 0)
    col = lax.broadcasted_iota(jnp.int32, (c, c), 1)
    eye_b = jnp.where(row == col, 1.0, 0.0).astype(BF16)
    tris = {False: row <= col, True: row >= col}
    tris_b = {r: jnp.where(t, 1.0, 0.0).astype(BF16) for r, t in tris.items()}
    n = range(len(chains))
    rev, q, k, vt, ig, fg, ct, nv, m_prev = zip(*chains)

    def each(fn):
        return [fn(i) for i in n]

    def dot3(parts, rhs, nt=False):
        d = _dot_nt if nt else _dot
        return d(parts[0], rhs) + d(parts[1], rhs) + d(parts[2], rhs)

    lf = each(lambda i: _split3(jnp.broadcast_to(_log_sigmoid(fg[i]), (8, c))))
    b = each(lambda i: dot3(lf[i], tris_b[rev[i]])[0:1])
    us = each(lambda i: _split3(jnp.broadcast_to(ig[i] - b[i], (c, c))))
    u = each(lambda i: _dot_nt(eye_b, us[i][0]) + _dot_nt(eye_b, us[i][1]) + _dot_nt(eye_b, us[i][2]))
    dmat = each(lambda i: jnp.where(tris[rev[i]], b[i] + u[i], -jnp.inf))
    m_t = each(lambda i: jnp.maximum(b[i] + m_prev[i], jnp.max(dmat[i], axis=0, keepdims=True)))
    qb = each(lambda i: q[i].astype(BF16))
    kb = each(lambda i: k[i].astype(BF16))
    kq = each(lambda i: _dot_nt(kb[i], qb[i]))
    p = each(lambda i: jnp.exp(dmat[i] - m_t[i]) * kq[i])
    inter = each(lambda i: jnp.exp(b[i] + m_prev[i] - m_t[i]))
    ns = each(lambda i: _split2(jnp.broadcast_to(nv[i], (8, nv[i].shape[1]))))
    qs = each(lambda i: _split2(q[i]))
    qn = each(lambda i: (_dot_nt(ns[i][0], qs[i][0]) + _dot_nt(ns[i][1], qs[i][0])
                         + _dot_nt(ns[i][0], qs[i][1]))[0:1])
    den = each(lambda i: inter[i] * qn[i] + jnp.sum(p[i], axis=0, keepdims=True))
    scale = each(lambda i: 1.0 / jnp.maximum(jnp.abs(den[i]), jnp.exp(-m_t[i])))
    cq = each(lambda i: _dot_nt(ct[i].astype(BF16), qb[i]))
    vp = each(lambda i: _dot(vt[i].astype(BF16), p[i].astype(BF16)))
    ht = each(lambda i: (inter[i] * cq[i] + vp[i]) * scale[i])
    last = each(lambda i: 0 if rev[i] else c - 1)
    m_new = each(lambda i: m_t[i][:, last[i]:last[i] + 1])
    b_exit = each(lambda i: b[i][:, last[i]:last[i] + 1])
    w = each(lambda i: jnp.exp(b_exit[i] - b[i] + ig[i] - m_new[i]))
    dec = each(lambda i: jnp.exp(b_exit[i] + m_prev[i] - m_new[i]))
    vk = each(lambda i: _dot((vt[i] * w[i]).astype(BF16), kb[i]))
    ct_new = each(lambda i: dec[i] * ct[i] + vk[i])
    ws = each(lambda i: _split2(jnp.broadcast_to(w[i], (8, c))))
    ks = each(lambda i: _split2(k[i]))
    wk = each(lambda i: (_dot(ws[i][0], ks[i][0]) + _dot(ws[i][1], ks[i][0]) + _dot(ws[i][0], ks[i][1]))[0:1])
    nv_new = each(lambda i: dec[i] * nv[i] + wk[i])
    return [(ht[i], ct_new[i], nv_new[i], m_new[i]) for i in n]


def _mlstm_kernel(seq_len, q_ref, k_ref, v_ref, og_ref, gate_t_ref, cwq_ref, cwk_ref,
                  c0_ref, n0_ref, m0_ref,
                  o_ref, c_out_ref, n_out_ref, m_out_ref,
                  q2_ref, k2_ref, vt_ref, hf_ref, hb_ref, ct_ref, n_ref, m_ref):
    c = SCAN_CHUNK
    hd = HEAD_DIM
    n_chunks = seq_len // c
    _conv3_silu_tiles(q2_ref, q_ref, cwq_ref, 1.0)
    _conv3_silu_tiles(k2_ref, k_ref, cwk_ref, HEAD_DIM ** -0.5)
    for hh in range(HEADS):
        cols = slice(hh * hd, (hh + 1) * hd)
        for j in range(n_chunks):
            vt_ref[hh, :, j * c:(j + 1) * c] = v_ref[j * c:(j + 1) * c, cols].T
        for d in range(2):
            ct_ref[d, hh] = c0_ref[d, hh].T
    n_ref[...] = n0_ref[...]
    m_ref[...] = m0_ref[...]

    def body(n, carry):
        where, chains = [], []
        for d in range(2):
            sl = pl.ds(pl.multiple_of((n_chunks - 1 - n if d else n) * c, c), c)
            for hh in range(HEADS):
                cols = slice(hh * hd, (hh + 1) * hd)
                gr = gate_t_ref[hh, :, sl]
                where.append((d, hh, sl))
                chains.append((bool(d), q2_ref[sl, cols], k2_ref[sl, cols], vt_ref[hh, :, sl],
                               gr[2 * d:2 * d + 1, :], gr[2 * d + 1:2 * d + 2, :],
                               ct_ref[d, hh], n_ref[d, hh], m_ref[d, hh]))
        for (d, hh, sl), (ht, ct, nv, m_new) in zip(where, _mlstm_chunks(chains)):
            ct_ref[d, hh] = ct
            n_ref[d, hh] = nv
            m_ref[d, hh] = m_new
            (hb_ref if d else hf_ref)[hh, :, sl] = ht
        return carry

    lax.fori_loop(0, n_chunks, body, 0)
    for hh in range(HEADS):
        cols = slice(hh * hd, (hh + 1) * hd)
        for j in range(n_chunks):
            rows = slice(j * c, (j + 1) * c)
            h = (hf_ref[hh, :, rows] + hb_ref[hh, :, rows]).T
            h = h * lax.rsqrt(jnp.mean(h * h, axis=-1, keepdims=True) + RMS_EPS)
            o_ref[rows, cols] = h * _sigmoid(og_ref[rows, cols])
        for d in range(2):
            c_out_ref[d, hh] = ct_ref[d, hh].T
    n_out_ref[...] = n_ref[...]
    m_out_ref[...] = m_ref[...]


def _mlstm(p, gate_t_h, conv_w, c0, n0, m0, n_seq, seq_len, tok_offset):
    hd = HEAD_DIM
    gw = GROUP_W
    row0 = tok_offset // seq_len

    def col(part):
        return pl.BlockSpec((seq_len, gw), lambda b: (row0 + b, part))

    c_spec = pl.BlockSpec((None, 2, HEADS, hd, hd), lambda b: (b, 0, 0, 0, 0))
    n_spec = pl.BlockSpec((None, 2, HEADS, 1, hd), lambda b: (b, 0, 0, 0, 0))
    m_spec = pl.BlockSpec((None, 2, HEADS, 1, 1), lambda b: (b, 0, 0, 0, 0))
    return pl.pallas_call(
        functools.partial(_mlstm_kernel, seq_len),
        grid=(n_seq,),
        in_specs=[col(5), col(6), col(7), col(8),
                  pl.BlockSpec((HEADS, 4, seq_len), lambda b: (0, 0, row0 + b)),
                  pl.BlockSpec((3, gw), lambda b: (0, 0)),
                  pl.BlockSpec((3, gw), lambda b: (0, 1)),
                  c_spec, n_spec, m_spec],
        out_specs=[pl.BlockSpec((seq_len, gw), lambda b: (b, 0)), c_spec, n_spec, m_spec],
        out_shape=[jax.ShapeDtypeStruct((n_seq * seq_len, gw), F32),
                   jax.ShapeDtypeStruct((n_seq, 2, HEADS, hd, hd), F32),
                   jax.ShapeDtypeStruct((n_seq, 2, HEADS, 1, hd), F32),
                   jax.ShapeDtypeStruct((n_seq, 2, HEADS, 1, 1), F32)],
        scratch_shapes=[pltpu.VMEM((seq_len, gw), F32), pltpu.VMEM((seq_len, gw), F32),
                        pltpu.VMEM((HEADS, hd, seq_len), F32),
                        pltpu.VMEM((HEADS, hd, seq_len), F32), pltpu.VMEM((HEADS, hd, seq_len), F32),
                        pltpu.VMEM((2, HEADS, hd, hd), F32), pltpu.VMEM((2, HEADS, 1, hd), F32),
                        pltpu.VMEM((2, HEADS, 1, 1), F32)],
        compiler_params=_cparams("parallel"),
        name=f"mlstm_l{seq_len}",
    )(p, p, p, p, gate_t_h, conv_w, conv_w, c0, n0, m0)


def _dft_tables(seq_len):
    n = 2 * seq_len
    k = jnp.arange(seq_len, dtype=jnp.int32)[:, None]
    t = jnp.arange(seq_len, dtype=jnp.int32)[None, :]
    ang = ((k * t) % n).astype(F32) * (2.0 * np.pi / n)
    fc = jnp.cos(ang)
    fs = jnp.sin(ang)
    nyq = jnp.where(t % 2 == 0, 1.0, -1.0).astype(F32)
    fs = jnp.where(k == 0, nyq, fs)
    return jnp.concatenate([fc, fs], axis=0)


def _filter_kernel(seq_len, z_ref, w1_ref, b1_ref, w2_ref, b2_ref, w3f_ref, w3b_ref, b3f_ref, b3b_ref,
                   f0_ref, f1_ref, rf_ref, rb_ref, fhi_ref, flo_ref, kf_ref, a_ref):
    hp = lax.Precision.HIGHEST
    n = 2 * seq_len
    z = z_ref[...]

    @pl.when(jnp.logical_and(pl.program_id(0) == 0, pl.program_id(1) == 0))
    def _():
        a1 = jnp.sin(f0_ref[...] * (jnp.dot(z, w1_ref[...], precision=hp, preferred_element_type=F32)
                                    + b1_ref[...]))
        a_ref[...] = jnp.sin(f1_ref[...] * (jnp.dot(a1, w2_ref[...], precision=hp, preferred_element_type=F32)
                                            + b2_ref[...]))

    a = a_ref[...]
    t_norm = z[:, 0:1]
    hf = (jnp.dot(a, w3f_ref[...], precision=hp, preferred_element_type=F32) + b3f_ref[...]) \
        * jnp.exp(-t_norm * jnp.exp(rf_ref[...]))
    hb = (jnp.dot(a, w3b_ref[...], precision=hp, preferred_element_type=F32) + b3b_ref[...]) \
        * jnp.exp(-t_norm * jnp.exp(rb_ref[...]))
    inv = lax.rsqrt(jnp.sum(hf * hf, axis=0, keepdims=True) + jnp.sum(hb * hb, axis=0, keepdims=True))
    hf = hf * inv
    r = lax.broadcasted_iota(jnp.int32, (seq_len, 1), 0)
    hb = jnp.where(r == 0, 0.0, hb * inv)
    sh, sl = _split2(hf + hb)
    dh, dl = _split2(hf - hb)
    fhi = fhi_ref[...]
    flo = flo_ref[...]
    kc = _dot(fhi[:seq_len], sh) + _dot(fhi[:seq_len], sl) + _dot(flo[:seq_len], sh)
    ks = _dot(fhi[seq_len:], dh) + _dot(fhi[seq_len:], dl) + _dot(flo[seq_len:], dh)
    sign = jnp.where(r % 2 == 0, 1.0, -1.0)
    k_nyq = jnp.sum(sign * (hf + hb), axis=0, keepdims=True)
    ks = jnp.where(r == 0, k_nyq, ks)
    scale = jnp.where(r == 0, 1.0 / n, 2.0 / n)
    kf_ref[0:seq_len, :] = kc * scale
    kf_ref[seq_len:n, :] = ks * scale


def _hyena_filters(seq_len, w1, b1, w2, b2, w3, b3, freq, log_rate, f_tab):
    d = D_MODEL
    cb = 256
    t = jnp.arange(seq_len, dtype=F32)
    t_norm = t / (seq_len - 1)
    bands = jnp.linspace(1e-4, HY_BANDS - 1, HY_BANDS, dtype=F32)
    ang = (2.0 * np.pi / seq_len) * t[:, None] * bands[None, :]
    z = jnp.concatenate([t_norm[:, None], jnp.cos(ang), jnp.sin(ang)], axis=-1)
    kpad = 128 - HY_EMB
    z = jnp.pad(z, ((0, 0), (0, kpad)))
    w1p = jnp.pad(w1, ((0, kpad), (0, 0)))
    f_hi = f_tab.astype(BF16)
    f_lo = (f_tab - f_hi.astype(F32)).astype(BF16)
    n_cb = d // cb
    hh = HY_HIDDEN
    row = lambda a: a.reshape(1, -1)
    const = lambda shape: pl.BlockSpec(shape, lambda o, j: (0,) * len(shape))
    fwd = lambda rows: pl.BlockSpec((rows, cb), lambda o, j: (0, o * n_cb + j))
    bwd = lambda rows: pl.BlockSpec((rows, cb), lambda o, j: (0, (HY_ORDER + o) * n_cb + j))
    return pl.pallas_call(
        functools.partial(_filter_kernel, seq_len),
        grid=(HY_ORDER, n_cb),
        in_specs=[const((seq_len, 128)), const((128, hh)), const((1, hh)), const((hh, hh)), const((1, hh)),
                  fwd(hh), bwd(hh), fwd(1), bwd(1),
                  const((1, hh)), const((1, hh)), fwd(1), bwd(1),
                  const((2 * seq_len, seq_len)), const((2 * seq_len, seq_len))],
        out_specs=pl.BlockSpec((None, 2 * seq_len, cb), lambda o, j: (o, 0, j)),
        out_shape=jax.ShapeDtypeStruct((HY_ORDER, 2 * seq_len, d), F32),
        scratch_shapes=[pltpu.VMEM((seq_len, hh), F32)],
        compiler_params=_cparams("arbitrary", "arbitrary"),
        name=f"hyena_filter_l{seq_len}",
    )(z, w1p, row(b1), w2, row(b2), w3, w3, row(b3), row(b3),
      row(freq[0]), row(freq[1]), row(log_rate), row(log_rate), f_hi, f_lo)


def _hyena_kernel(seq_len, seqs, v_ref, x1_ref, x2_ref, cwv_ref, cw1_ref, cw2_ref, bias_ref, kf_ref,
                  f_ref, ft_ref, o_ref, z_ref, zb_ref, y_ref):
    kc = min(HY_FREQ_CHUNK, seq_len)
    n_k = seq_len // kc
    r = lax.broadcasted_iota(jnp.int32, (kc, 1), 0)
    gate_refs = ((x1_ref, cw1_ref), (x2_ref, cw2_ref))
    for s in range(seqs):
        rows = slice(s * seq_len, (s + 1) * seq_len)
        z_ref[s] = _short_conv3(v_ref[rows, :], cwv_ref[...])
    for o in range(HY_ORDER):
        for s in range(seqs):
            zb_ref[s] = z_ref[s].astype(BF16)
            y_ref[s] = jnp.zeros(y_ref.shape[1:], F32)

        def freq_chunk(j, carry):
            r0 = pl.multiple_of(j * kc, kc)
            k_cos = kf_ref[o, pl.ds(r0, kc), :]
            k_sin = kf_ref[o, pl.ds(seq_len + r0, kc), :]
            real_row = jnp.logical_and(r == 0, j == 0)
            for s in range(seqs):
                a = _dot(f_ref[pl.ds(r0, kc), :], zb_ref[s])
                bm = _dot(f_ref[pl.ds(seq_len + r0, kc), :], zb_ref[s])
                yc = a * k_cos - jnp.where(real_row, 0.0, bm * k_sin)
                ys = jnp.where(real_row, bm * k_sin, a * k_sin + bm * k_cos)
                y_ref[s] += _dot(ft_ref[j], yc.astype(BF16)) + _dot(ft_ref[n_k + j], ys.astype(BF16))
            return carry

        lax.fori_loop(0, n_k, freq_chunk, 0)
        x_ref, cw_ref = gate_refs[o]
        for s in range(seqs):
            rows = slice(s * seq_len, (s + 1) * seq_len)
            gate = _short_conv3(x_ref[rows, :], cw_ref[...])
            z_ref[s] = gate * (y_ref[s] + z_ref[s] * bias_ref[o:o + 1, :])
    for s in range(seqs):
        o_ref[s * seq_len:(s + 1) * seq_len, :] = z_ref[s]


def _hyena(u, conv_w, bias, kf, f_tab, n_seq, seq_len, tok_offset):
    d = D_MODEL
    cb = 256
    n_cb = d // cb
    seqs = max(1, HY_STEP_ROWS // seq_len)
    rows = seqs * seq_len
    row0 = tok_offset // rows
    kc = min(HY_FREQ_CHUNK, seq_len)
    n_k = seq_len // kc
    f_bf = f_tab.astype(BF16)
    ft = f_bf.T.reshape(seq_len, 2 * n_k, kc).transpose(1, 0, 2)

    def part(k):
        return pl.BlockSpec((rows, cb), lambda j, b: (row0 + b, k * n_cb + j))

    def cw(k):
        return pl.BlockSpec((3, cb), lambda j, b: (0, k * n_cb + j))

    return pl.pallas_call(
        functools.partial(_hyena_kernel, seq_len, seqs),
        grid=(n_cb, n_seq // seqs),
        in_specs=[part(0), part(1), part(2), cw(0), cw(1), cw(2),
                  pl.BlockSpec((HY_ORDER, cb), lambda j, b: (0, j)),
                  pl.BlockSpec((HY_ORDER, 2 * seq_len, cb), lambda j, b: (0, 0, j)),
                  pl.BlockSpec((2 * seq_len, seq_len), lambda j, b: (0, 0)),
                  pl.BlockSpec((2 * n_k, seq_len, kc), lambda j, b: (0, 0, 0))],
        out_specs=pl.BlockSpec((rows, cb), lambda j, b: (b, j)),
        out_shape=jax.ShapeDtypeStruct((n_seq * seq_len, d), F32),
        scratch_shapes=[pltpu.VMEM((seqs, seq_len, cb), F32), pltpu.VMEM((seqs, seq_len, cb), BF16),
                        pltpu.VMEM((seqs, seq_len, cb), F32)],
        compiler_params=_cparams("parallel", "parallel"),
        name=f"hyena_l{seq_len}",
    )(u, u, u, conv_w, conv_w, conv_w, bias, kf, f_bf, ft)


def _mix_out_kernel(n_parts, *refs):
    o_refs = refs[:2 * n_parts]
    w_refs = refs[2 * n_parts:3 * n_parts]
    x_ref, mod_ref, g_ref, wr_both_ref, br_ref = refs[3 * n_parts:3 * n_parts + 5]
    x1_ref, h_ref, eid_ref, wts_ref, rank_ref, cnt_ref, run_ref = refs[3 * n_parts + 5:]
    i = pl.program_id(0)
    t = ROUTE_TILE
    ne = N_EXPERTS

    from_prompt = i < N_PROMPT // ROUTE_TILE
    y = None
    for j in range(n_parts):
        o = jnp.where(from_prompt, o_refs[2 * j][...], o_refs[2 * j + 1][...])
        yj = _dot(o.astype(BF16), w_refs[j][...])
        y = yj if y is None else y + yj
    x1 = x_ref[...] + mod_ref[2:3, :] * y
    x1_ref[...] = x1
    h = _norm_mod(x1, g_ref[...], mod_ref[4:5, :], mod_ref[3:4, :])
    h_ref[...] = _pack_bf16_halves(h)
    h_hi = h.astype(BF16)
    h_lo = (h - h_hi.astype(F32)).astype(BF16)
    hi_terms = _dot(h_hi, wr_both_ref[...])
    logits = (hi_terms[:, :LANES] + hi_terms[:, LANES:] + _dot(h_lo, wr_both_ref[:, :LANES])).T[:ne] \
        + br_ref[...]

    @pl.when(i == 0)
    def _():
        run_ref[...] = jnp.zeros_like(run_ref)

    e_iota = lax.broadcasted_iota(jnp.int32, (ne, t), 0)
    vals, eids, onehots = [], [], []
    for _k in range(TOP_K):
        m = jnp.max(logits, axis=0, keepdims=True)
        eid = jnp.min(jnp.where(logits == m, e_iota, ne), axis=0, keepdims=True)
        sel = e_iota == eid
        logits = jnp.where(sel, -jnp.inf, logits)
        onehots.append(jnp.where(sel, 1.0, 0.0))
        vals.append(m)
        eids.append(eid)
    r2 = lax.broadcasted_iota(jnp.int32, (t, t), 0)
    c2 = lax.broadcasted_iota(jnp.int32, (t, t), 1)
    before = jnp.where(r2 < c2, 1.0, 0.0).astype(BF16)
    earlier = _dot(jnp.concatenate(onehots, axis=0).astype(BF16), before)
    running = run_ref[...]
    ranks = []
    for k, onehot in enumerate(onehots):
        ranks.append(jnp.sum(onehot * (running + earlier[k * ne:(k + 1) * ne]), axis=0, keepdims=True))
        running = running + jnp.sum(onehot, axis=1, keepdims=True)
    run_ref[...] = running
    cnt_ref[...] = running
    v = jnp.concatenate(vals, axis=0)
    ex = jnp.exp(v - v[0:1])
    wts_ref[...] = ex / jnp.sum(ex, axis=0, keepdims=True)
    eid_ref[...] = jnp.concatenate(eids, axis=0)
    rank_ref[...] = jnp.concatenate(ranks, axis=0).astype(jnp.int32)


def _mix_out(parts, x, mod_l, norm_g, w_router, b_router):
    d = D_MODEL
    t = ROUTE_TILE
    ne = N_EXPERTS
    n_parts = len(parts)
    n_prompt_tiles = N_PROMPT // t
    wr = jnp.pad(w_router, ((0, 0), (0, LANES - ne)))
    wr_hi = wr.astype(BF16)
    wr_both = jnp.concatenate([wr_hi, (wr - wr_hi.astype(F32)).astype(BF16)], axis=1)
    in_specs = []
    for o_p, _, _ in parts:
        in_specs.append(pl.BlockSpec((t, o_p.shape[1]), lambda i: (jnp.minimum(i, n_prompt_tiles - 1), 0)))
        in_specs.append(pl.BlockSpec((t, o_p.shape[1]), lambda i: (jnp.maximum(i - n_prompt_tiles, 0), 0)))
    in_specs += [pl.BlockSpec(w.shape, lambda i: (0, 0)) for _, _, w in parts]
    in_specs += [
        pl.BlockSpec((t, d), lambda i: (i, 0)),
        pl.BlockSpec((None, N_MOD, d), lambda i: (_tile_cond_row(i, t), 0, 0)),
        pl.BlockSpec((1, d), lambda i: (0, 0)),
        pl.BlockSpec((d, 2 * LANES), lambda i: (0, 0)),
        pl.BlockSpec((ne, 1), lambda i: (0, 0)),
    ]
    tok_major = pl.BlockSpec((TOP_K, t), lambda i: (0, i))
    return pl.pallas_call(
        functools.partial(_mix_out_kernel, n_parts),
        grid=(N_TOK // t,),
        in_specs=in_specs,
        out_specs=[pl.BlockSpec((t, d), lambda i: (i, 0)), pl.BlockSpec((t, d // 2), lambda i: (i, 0)),
                   tok_major, tok_major, tok_major, pl.BlockSpec((ne, 1), lambda i: (0, 0))],
        out_shape=[jax.ShapeDtypeStruct((N_TOK, d), F32), jax.ShapeDtypeStruct((N_TOK, d // 2), jnp.uint32),
                   jax.ShapeDtypeStruct((TOP_K, N_TOK), jnp.int32), jax.ShapeDtypeStruct((TOP_K, N_TOK), F32),
                   jax.ShapeDtypeStruct((TOP_K, N_TOK), jnp.int32), jax.ShapeDtypeStruct((ne, 1), F32)],
        scratch_shapes=[pltpu.VMEM((ne, 1), F32)],
        compiler_params=_cparams("arbitrary"),
        name="mix_out_router",
    )(*[o for part in parts for o in part[:2]], *[w.astype(BF16) for _, _, w in parts], x, mod_l,
      norm_g.reshape(1, d), wr_both, b_router.reshape(ne, 1))


def _sc_row_gather(src, idx):
    n = idx.shape[0]
    width = src.shape[1]
    step_rows = min(SC_MAX_INDICES, SC_STEP_BYTES // (width * 4))
    sc = plsc.get_sparse_core_info()
    n_workers = sc.num_cores * sc.num_subcores
    per_worker = n // n_workers
    if n % n_workers or per_worker % step_rows:
        raise ValueError("row count must be whole SparseCore steps on every subcore")
    mesh = plsc.VectorSubcoreMesh(core_axis_name="core", subcore_axis_name="subcore")

    @functools.partial(pl.kernel, out_type=jax.ShapeDtypeStruct((n, width), src.dtype), mesh=mesh,
                       scratch_types=[pltpu.VMEM((per_worker,), jnp.int32),
                                      pltpu.VMEM((step_rows, width), src.dtype)],
                       name="sc_row_gather")
    def gather(src_hbm, idx_hbm, dst_hbm, idx_vmem, rows_vmem):
        worker = lax.axis_index("subcore") * sc.num_cores + lax.axis_index("core")
        base = worker * per_worker
        pltpu.sync_copy(idx_hbm.at[pl.ds(base, per_worker)], idx_vmem)

        @pl.loop(0, per_worker // step_rows)
        def _(c):
            pltpu.sync_copy(src_hbm.at[idx_vmem.at[pl.ds(c * step_rows, step_rows)]], rows_vmem)
            pltpu.sync_copy(rows_vmem, dst_hbm.at[pl.ds(base + c * step_rows, step_rows)])

    return gather(src, idx)


def _sc_row_scatter(src, rows_of_src):
    copies, n_src = rows_of_src.shape
    width = src.shape[1]
    step_rows = min(SC_MAX_INDICES, SC_STEP_BYTES // (width * 4))
    sc = plsc.get_sparse_core_info()
    n_workers = sc.num_cores * sc.num_subcores
    per_worker = n_src // n_workers
    steps = per_worker // step_rows
    if src.shape[0] != n_src or n_src % n_workers or per_worker % step_rows:
        raise ValueError("row counts must be whole SparseCore steps on every subcore")
    mesh = plsc.VectorSubcoreMesh(core_axis_name="core", subcore_axis_name="subcore")
    idx = rows_of_src.reshape(copies, n_workers, steps, step_rows).transpose(1, 2, 0, 3)

    @functools.partial(pl.kernel, out_type=jax.ShapeDtypeStruct((copies * n_src, width), src.dtype),
                       mesh=mesh,
                       scratch_types=[pltpu.VMEM((steps, copies, step_rows), jnp.int32),
                                      pltpu.VMEM((step_rows, width), src.dtype)],
                       name="sc_row_scatter")
    def scatter(src_hbm, idx_hbm, dst_hbm, idx_vmem, rows_vmem):
        worker = lax.axis_index("subcore") * sc.num_cores + lax.axis_index("core")
        pltpu.sync_copy(idx_hbm.at[worker], idx_vmem)

        @pl.loop(0, steps)
        def _(c):
            pltpu.sync_copy(src_hbm.at[pl.ds(worker * per_worker + c * step_rows, step_rows)], rows_vmem)
            for j in range(copies):
                pltpu.sync_copy(rows_vmem, dst_hbm.at[idx_vmem.at[c, j]])

    return scatter(src, idx)


def _pack_bf16_halves(x):
    w = x.shape[1] // 2
    bits = pltpu.bitcast(x.astype(BF16).astype(F32), jnp.uint32)
    return bits[:, :w] | (bits[:, w:] >> 16)


def _unpack_bf16_halves(p):
    hi = pltpu.bitcast(p & jnp.uint32(0xFFFF0000), F32).astype(BF16)
    lo = pltpu.bitcast(p << 16, F32).astype(BF16)
    return hi, lo


def _experts_kernel(layer, te_ref, first_ref, slot_ref, next_ref, rows_ref, nv_ref,
                    x_ref, wgu_hbm, bg_ref, bu_ref, wd_hbm, bd_ref, sel_ref,
                    y_ref, wgu_buf, wd_buf, wg_ref, wu_ref, wdb_ref, sem):
    i = pl.program_id(0)
    valid = i < nv_ref[0]
    half = DEINT_COLS // 2
    k_half = x_ref.shape[1]

    def fetch(expert, slot):
        return (pltpu.make_async_copy(wgu_hbm.at[layer, expert], wgu_buf.at[slot], sem.at[slot, 0]),
                pltpu.make_async_copy(wd_hbm.at[layer, expert], wd_buf.at[slot], sem.at[slot, 1]))

    @pl.when(i == 0)
    def _():
        for cp in fetch(te_ref[0], 0):
            cp.start()

    @pl.when(jnp.logical_and(valid, first_ref[i] == 1))
    def _():
        slot = slot_ref[i]
        for cp in fetch(te_ref[i], slot):
            cp.wait()

        @pl.when(next_ref[i] >= 0)
        def _():
            for cp in fetch(next_ref[i], 1 - slot):
                cp.start(priority=1)

        for c in range(wgu_buf.shape[2] // DEINT_COLS):
            w = wgu_buf[slot, :, c * DEINT_COLS:(c + 1) * DEINT_COLS].astype(BF16)
            split = _dot(w, sel_ref[...]).astype(BF16)
            wg_ref[:, c * half:(c + 1) * half] = split[:, :half]
            wu_ref[:, c * half:(c + 1) * half] = split[:, half:]
        wdb_ref[...] = wd_buf[slot].astype(BF16)

    def ffn(rows):
        x_l, x_r = _unpack_bf16_halves(x_ref[:rows, :])
        gl = _dot(x_l, wg_ref[:k_half, :]) + _dot(x_r, wg_ref[k_half:, :]) + bg_ref[...]
        up = _dot(x_l, wu_ref[:k_half, :]) + _dot(x_r, wu_ref[k_half:, :]) + bu_ref[...]
        gl = jnp.minimum(gl, SWIGLU_LIMIT)
        up = jnp.clip(up, -SWIGLU_LIMIT, SWIGLU_LIMIT)
        act = (up + 1.0) * gl * _sigmoid(SWIGLU_ALPHA * gl)
        y_ref[:rows, :] = _pack_bf16_halves(_dot(act.astype(BF16), wdb_ref[...]) + bd_ref[...])

    tm = x_ref.shape[0]
    quarter = tm // MOE_TILE_PARTS
    used = rows_ref[i]
    for part in range(1, MOE_TILE_PARTS + 1):
        rows = part * quarter
        fits = jnp.logical_and(used > rows - quarter, used <= rows)

        @pl.when(jnp.logical_and(valid, fits))
        def _(rows=rows):
            ffn(rows)
            if rows < tm:
                y_ref[rows:, :] = jnp.zeros((tm - rows, y_ref.shape[1]), y_ref.dtype)

    @pl.when(jnp.logical_not(valid))
    def _():
        y_ref[...] = jnp.zeros_like(y_ref)


def _experts(xs, tile_expert, tile_first, tile_rows, n_valid, layer, w_gu, b_gate, b_up, w_down, b_down):
    d = D_MODEL
    tm = MOE_TILE
    n_tiles = MOE_ROWS // tm
    ff = w_down.shape[2]
    half = DEINT_COLS // 2
    r = jnp.arange(DEINT_COLS)[:, None]
    c = jnp.arange(DEINT_COLS)[None, :]
    sel = (r == jnp.where(c < half, 2 * c, 2 * (c - half) + 1)).astype(BF16)
    group = jnp.cumsum(tile_first) - 1
    tile_slot = (group % 2).astype(jnp.int32)
    is_last_group = group == group[-1]
    following = jnp.concatenate([tile_expert[1:], tile_expert[-1:]])
    idx = jnp.arange(n_tiles, dtype=jnp.int32)
    group_end = jnp.max(jnp.where(group[None, :] == group[:, None], idx[None, :], -1), axis=1)
    tile_next = jnp.where(is_last_group, -1, following[group_end]).astype(jnp.int32)
    wspec = lambda k, n: pl.BlockSpec((None, None, k, n), lambda i, *_: (layer, _[0][i], 0, 0))
    grid_spec = pltpu.PrefetchScalarGridSpec(
        num_scalar_prefetch=6,
        grid=(n_tiles,),
        in_specs=[pl.BlockSpec((tm, d // 2), lambda i, *_: (i, 0)),
                  pl.BlockSpec(memory_space=pl.ANY), wspec(1, ff), wspec(1, ff),
                  pl.BlockSpec(memory_space=pl.ANY), wspec(1, d),
                  pl.BlockSpec((DEINT_COLS, DEINT_COLS), lambda i, *_: (0, 0))],
        out_specs=pl.BlockSpec((tm, d // 2), lambda i, *_: (i, 0)),
        scratch_shapes=[pltpu.VMEM((2, d, 2 * ff), F32), pltpu.VMEM((2, ff, d), F32),
                        pltpu.VMEM((d, ff), BF16), pltpu.VMEM((d, ff), BF16), pltpu.VMEM((ff, d), BF16),
                        pltpu.SemaphoreType.DMA((2, 2))],
    )
    return pl.pallas_call(
        functools.partial(_experts_kernel, layer),
        grid_spec=grid_spec,
        out_shape=jax.ShapeDtypeStruct((MOE_ROWS, d // 2), jnp.uint32),
        compiler_params=_cparams("arbitrary"),
        name="experts",
    )(tile_expert, tile_first, tile_slot, tile_next, tile_rows, n_valid, xs, w_gu, b_gate, b_up, w_down,
      b_down, sel)


def _combine_kernel(final, x_ref, g_ref, w_ref, mod_ref, fg_ref, o_ref):
    y_l = y_r = None
    for k in range(TOP_K):
        g_l, g_r = _unpack_bf16_halves(g_ref[k])
        wk = w_ref[:, k:k + 1]
        y_l = wk * g_l.astype(F32) if y_l is None else y_l + wk * g_l.astype(F32)
        y_r = wk * g_r.astype(F32) if y_r is None else y_r + wk * g_r.astype(F32)
    x = x_ref[...] + mod_ref[5:6, :] * jnp.concatenate([y_l, y_r], axis=1)
    if final:
        x = x * lax.rsqrt(jnp.mean(x * x, axis=-1, keepdims=True) + RMS_EPS) * fg_ref[...]
    o_ref[...] = x


def _combine(x1, gathered, wts, mod_l, final_g, final, tok0, n_tok):
    d = D_MODEL
    t = ROUTE_TILE
    tile0 = tok0 // t
    return pl.pallas_call(
        functools.partial(_combine_kernel, final),
        grid=(n_tok // t,),
        in_specs=[pl.BlockSpec((t, d), lambda i: (tile0 + i, 0)),
                  pl.BlockSpec((TOP_K, t, d // 2), lambda i: (0, i, 0)),
                  pl.BlockSpec((t, TOP_K), lambda i: (tile0 + i, 0)),
                  pl.BlockSpec((None, N_MOD, d), lambda i: (_tile_cond_row(tile0 + i, t), 0, 0)),
                  pl.BlockSpec((1, d), lambda i: (0, 0))],
        out_specs=pl.BlockSpec((t, d), lambda i: (i, 0)),
        out_shape=jax.ShapeDtypeStruct((n_tok, d), F32),
        compiler_params=_cparams("parallel"),
        name="moe_combine",
    )(x1, gathered, wts, mod_l, final_g.reshape(1, d))


def _moe(x1, h, eid, wts, rank, counts, mod_l, layer, w_gu, b_gu, w_down, b_down, final_g, final):
    d = D_MODEL
    tm = MOE_TILE
    n_tiles = MOE_ROWS // tm
    cnt = counts.reshape(N_EXPERTS).astype(jnp.int32)
    gsz = ((cnt + tm - 1) // tm) * tm
    ends = jnp.cumsum(gsz)
    offs = ends - gsz
    e_ids = jnp.arange(N_EXPERTS, dtype=jnp.int32)
    pos = jnp.sum(jnp.where(eid[..., None] == e_ids, offs, 0), axis=-1) + rank
    tile_start = jnp.arange(n_tiles, dtype=jnp.int32) * tm
    tile_expert = jnp.minimum(jnp.sum((ends[None, :] <= tile_start[:, None]).astype(jnp.int32), axis=1),
                              N_EXPERTS - 1)
    n_valid = (ends[-1:] // tm).astype(jnp.int32)
    last_valid = jnp.maximum(n_valid[0] - 1, 0)
    tile_expert = jnp.where(jnp.arange(n_tiles) < n_valid[0], tile_expert, tile_expert[last_valid])
    tile_first = jnp.concatenate([jnp.ones((1,), jnp.int32),
                                  (tile_expert[1:] != tile_expert[:-1]).astype(jnp.int32)])
    j = jnp.arange(tm, dtype=jnp.int32)[None, :]
    pad_used = j < (gsz - cnt)[:, None]
    n_unused_before = jnp.cumsum((~pad_used).reshape(-1).astype(jnp.int32)) - 1
    pad_pos = jnp.where(pad_used, (offs + cnt)[:, None] + j,
                        ends[-1] + n_unused_before.reshape(N_EXPERTS, tm))
    if N_EXPERTS * tm != N_TOK:
        raise ValueError("the filler slots are laid out as one extra copy of every token")

    xs = _sc_row_scatter(h, jnp.concatenate([pos, pad_pos.reshape(1, N_TOK)], axis=0))
    real_end = (offs + cnt)[tile_expert]
    tile_rows = jnp.clip(real_end - tile_start, 0, tm).astype(jnp.int32)
    ys = _experts(xs, tile_expert, tile_first, tile_rows, n_valid, layer, w_gu,
                  b_gu[:, :, None, 0::2], b_gu[:, :, None, 1::2], w_down, b_down[:, :, None, :])
    spans = ((0, N_PROMPT), (N_PROMPT, N_SAMPLE)) if final else ((0, N_TOK),)
    outs = []
    for tok0, n_tok in spans:
        gathered = _sc_row_gather(ys, pos[:, tok0:tok0 + n_tok].reshape(-1))
        outs.append(_combine(x1, gathered.reshape(TOP_K, n_tok, d // 2), wts.T, mod_l, final_g, final,
                             tok0, n_tok))
    return tuple(outs) if final else outs[0]


def _grid_positions(n_tok, d):
    rows = n_tok // GRID_W
    r, col = jnp.meshgrid(jnp.arange(rows, dtype=F32), jnp.arange(GRID_W, dtype=F32), indexing='ij')
    r = r.reshape(-1)
    col = col.reshape(-1)
    quarter = d // 4
    inv = 1.0 / (10000.0 ** (jnp.arange(quarter, dtype=F32) / quarter))
    ar = r[:, None] * inv[None]
    ac = col[:, None] * inv[None]
    return jnp.concatenate([jnp.sin(ar), jnp.cos(ar), jnp.sin(ac), jnp.cos(ac)], axis=-1)


def kernel(x_prompt, x_sample, state_hgrn, state_mlstm_c, state_mlstm_n, state_mlstm_m, c, c_ctx,
           norm_g, final_g, w_mod, b_mod, ev_w_in, ev_gate_b, ev_conv, hg_lb, ev_w_out,
           hy_w_in, hy_conv, hy_w1, hy_b1, hy_w2, hy_b2, hy_w3, hy_b3, hy_freq, hy_log_rate, hy_bias, hy_w_out,
           w_router, b_router, w_gu, b_gu, w_down, b_down):
    d = D_MODEL
    hd = HEAD_DIM
    cond = jnp.concatenate([c_ctx[None], c, jnp.zeros((N_COND - 1 - DEC_BATCH, d), F32)], axis=0)
    mod = _modulation(cond, w_mod, b_mod)
    pos_tab = jnp.concatenate([jnp.zeros((TOK_TILE, d), F32), _grid_positions(DEC_SEQ, d)], axis=0)

    groups = ((BATCH, SEQ, 0), (DEC_BATCH, DEC_SEQ, N_PROMPT))
    new_states = None
    for l in range(DEPTH):
        if l % 2 == 0:
            e = l // 2
            if l == 0:
                x, p, gate_t = _proj_even(x_prompt.reshape(N_PROMPT, d), x_sample.reshape(N_SAMPLE, d),
                                          pos_tab, mod[l], norm_g[l, 0], ev_w_in[e], ev_gate_b[e])
            else:
                raise NotImplementedError("only the first layer adds grid positions")
            gate_t_h = gate_t.reshape(4, HEADS, N_TOK).transpose(1, 0, 2)
            o_hg, o_ml = [], []
            for gi, (n_seq, seq_len, off) in enumerate(groups):
                if gi == 0:
                    s0 = jnp.zeros((n_seq, 2, HEADS, hd, hd), F32)
                    c0 = jnp.zeros((n_seq, 2, HEADS, hd, hd), F32)
                    n0 = jnp.zeros((n_seq, 2, HEADS, 1, hd), F32)
                    m0 = jnp.zeros((n_seq, 2, HEADS, 1, 1), F32)
                else:
                    s0 = state_hgrn[:, e]
                    c0 = state_mlstm_c[:, e]
                    n0 = state_mlstm_n[:, e].reshape(n_seq, 2, HEADS, 1, hd)
                    m0 = state_mlstm_m[:, e].reshape(n_seq, 2, HEADS, 1, 1)
                og, s_fin = _hgrn(p, hg_lb, l, s0, n_seq, seq_len, off)
                om, c_fin, n_fin, m_fin = _mlstm(p, gate_t_h, ev_conv[e], c0, n0, m0, n_seq, seq_len, off)
                o_hg.append(og)
                o_ml.append(om)
                if gi == 0:
                    new_states = (s_fin[:, None], c_fin[:, None],
                                  n_fin.reshape(n_seq, 1, 2, HEADS, hd), m_fin.reshape(n_seq, 1, 2, HEADS))
            parts = [(*o_hg, ev_w_out[e][:GROUP_W]), (*o_ml, ev_w_out[e][GROUP_W:])]
        else:
            o = l // 2
            u = _proj_odd(x, mod[l], norm_g[l, 0], hy_w_in[o])
            zs = []
            for n_seq, seq_len, off in groups:
                f_tab = _dft_tables(seq_len)
                kf = _hyena_filters(seq_len, hy_w1[o], hy_b1[o], hy_w2[o], hy_b2[o], hy_w3[o], hy_b3[o],
                                    hy_freq[o], hy_log_rate[o], f_tab)
                zs.append(_hyena(u, hy_conv[o], hy_bias[o], kf, f_tab, n_seq, seq_len, off))
            parts = [(*zs, hy_w_out[o])]
        x1, h, eid, wts, rank, counts = _mix_out(parts, x, mod[l], norm_g[l, 1], w_router[l], b_router[l])
        x = _moe(x1, h, eid, wts, rank, counts, mod[l], l, w_gu, b_gu, w_down, b_down,
                 final_g, final=(l == DEPTH - 1))

    y_prompt, y_sample = x
    return (y_prompt.reshape(BATCH, SEQ, d), y_sample.reshape(DEC_BATCH, DEC_SEQ, d)) + new_states
```

```python
import functools

import numpy as np
import jax
import jax.numpy as jnp
from jax import lax
from jax.experimental import pallas as pl
from jax.experimental.pallas import tpu as pltpu
from jax.experimental.pallas import tpu_sc as plsc

F32 = jnp.float32
BF16 = jnp.bfloat16

D_MODEL = 1024
BATCH = 32
SEQ = 256
DEPTH = 2
DEC_BATCH = 8
DEC_SEQ = 1024
GRID_W = 64
RMS_EPS = 1e-6
N_MOD = 6
LOG2_E = 1.4426950408889634
LANES = 128

HEADS = 4
HEAD_DIM = 128
GROUP_W = HEADS * HEAD_DIM
N_GATES = 4 * HEADS
EVEN_MAIN = 9 * GROUP_W

HY_ORDER = 2
HY_BANDS = 16
HY_EMB = 1 + 2 * HY_BANDS
HY_HIDDEN = 64
HY_FREQ_CHUNK = 512
HY_STEP_ROWS = 1024

N_EXPERTS = 32
TOP_K = 4
SWIGLU_LIMIT = 7.0
SWIGLU_ALPHA = 1.702

N_PROMPT = BATCH * SEQ
N_SAMPLE = DEC_BATCH * DEC_SEQ
N_TOK = N_PROMPT + N_SAMPLE
N_COND = 16

TOK_TILE = 512
ROUTE_TILE = 512
SCAN_CHUNK = 128
SUB = 16
HALF = 8
HG_STEP_HEADS = 4
MOE_TILE = 512
MOE_TILE_PARTS = 4
MOE_ROWS = N_TOK * TOP_K + N_EXPERTS * MOE_TILE
DEINT_COLS = 256
SC_STEP_BYTES = 256 * 1024
SC_MAX_INDICES = 128

VMEM_LIMIT = 56 * 1024 * 1024


def _cparams(*sem):
    return pltpu.CompilerParams(dimension_semantics=sem, vmem_limit_bytes=VMEM_LIMIT)


def _split3(x):
    hi = x.astype(BF16)
    r = x - hi.astype(F32)
    mid = r.astype(BF16)
    lo = (r - mid.astype(F32)).astype(BF16)
    return hi, mid, lo


def _dot(a, b):
    return jnp.dot(a, b, preferred_element_type=F32)


def _dot_nt(a, b):
    return lax.dot_general(a, b, (((1,), (1,)), ((), ())), preferred_element_type=F32)


def _dot_tn(a, b):
    return lax.dot_general(a, b, (((0,), (0,)), ((), ())), preferred_element_type=F32)


def _dot_w3(a_exact_bf16, x):
    hi, mid, lo = _split3(x)
    return _dot(a_exact_bf16, hi) + _dot(a_exact_bf16, mid) + _dot(a_exact_bf16, lo)


def _sigmoid(x):
    return 1.0 / (1.0 + jnp.exp(-x))


def _silu(x):
    return x * _sigmoid(x)


def _log_sigmoid(x):
    return jnp.minimum(x, 0.0) - jnp.log(1.0 + jnp.exp(-jnp.abs(x)))


def _tile_cond_row(i, tile=TOK_TILE):
    n_prompt_tiles = N_PROMPT // tile
    tiles_per_seq = DEC_SEQ // tile
    return jnp.where(i < n_prompt_tiles, 0, 1 + (i - n_prompt_tiles) // tiles_per_seq)


def _mod_kernel(cond_ref, w_ref, b_ref, o_ref):
    a = _silu(cond_ref[...]).astype(BF16)
    o_ref[...] = _dot(a, w_ref[...].astype(BF16)) + b_ref[...]


def _modulation(cond, w_mod, b_mod):
    d = D_MODEL
    out = pl.pallas_call(
        _mod_kernel,
        grid=(DEPTH, N_MOD),
        in_specs=[
            pl.BlockSpec((N_COND, d), lambda l, j: (0, 0)),
            pl.BlockSpec((None, d, d), lambda l, j: (l, 0, j)),
            pl.BlockSpec((None, 1, d), lambda l, j: (l, 0, j)),
        ],
        out_specs=pl.BlockSpec((None, None, N_COND, d), lambda l, j: (l, j, 0, 0)),
        out_shape=jax.ShapeDtypeStruct((DEPTH, N_MOD, N_COND, d), F32),
        compiler_params=_cparams("parallel", "parallel"),
        name="modulation",
    )(cond, w_mod, b_mod.reshape(DEPTH, 1, N_MOD * d))
    return out.transpose(0, 2, 1, 3)


def _norm_mod(x, g_row, scale_row, shift_row):
    ms = jnp.mean(x * x, axis=-1, keepdims=True)
    y = x * lax.rsqrt(ms + RMS_EPS) * g_row
    return y * (1.0 + scale_row) + shift_row


def _proj_even_kernel(xp_ref, xs_ref, pos_ref, mod_ref, g_ref, w_ref, wgt_ref, gbt_ref,
                      xres_ref, p_ref, gate_t_ref):
    from_prompt = pl.program_id(0) < N_PROMPT // TOK_TILE
    x = jnp.where(from_prompt, xp_ref[...], xs_ref[...]) + pos_ref[...]
    xres_ref[...] = x
    h = _norm_mod(x, g_ref[...], mod_ref[1:2, :], mod_ref[0:1, :]).astype(BF16)
    p_ref[...] = _dot(h, w_ref[...])
    gate_t_ref[...] = _dot_nt(wgt_ref[...], h) + gbt_ref[...]


def _proj_even(x_prompt, x_sample, pos_tab, mod_l, norm_g, w_in, gate_b):
    d = D_MODEL
    n_tiles = N_TOK // TOK_TILE
    n_prompt_tiles = N_PROMPT // TOK_TILE
    tiles_per_seq = DEC_SEQ // TOK_TILE
    w_main = w_in[:, :EVEN_MAIN].astype(BF16)
    w_gate = w_in[:, EVEN_MAIN:].astype(BF16)

    def pos_map(i):
        return (jnp.where(i < n_prompt_tiles, 0, 1 + (i - n_prompt_tiles) % tiles_per_seq), 0)

    return pl.pallas_call(
        _proj_even_kernel,
        grid=(n_tiles,),
        in_specs=[
            pl.BlockSpec((TOK_TILE, d), lambda i: (jnp.minimum(i, n_prompt_tiles - 1), 0)),
            pl.BlockSpec((TOK_TILE, d), lambda i: (jnp.maximum(i - n_prompt_tiles, 0), 0)),
            pl.BlockSpec((TOK_TILE, d), pos_map),
            pl.BlockSpec((None, N_MOD, d), lambda i: (_tile_cond_row(i), 0, 0)),
            pl.BlockSpec((1, d), lambda i: (0, 0)),
            pl.BlockSpec((d, EVEN_MAIN), lambda i: (0, 0), pipeline_mode=pl.Buffered(1)),
            pl.BlockSpec((N_GATES, d), lambda i: (0, 0)),
            pl.BlockSpec((N_GATES, 1), lambda i: (0, 0)),
        ],
        out_specs=[
            pl.BlockSpec((TOK_TILE, d), lambda i: (i, 0)),
            pl.BlockSpec((TOK_TILE, EVEN_MAIN), lambda i: (i, 0)),
            pl.BlockSpec((N_GATES, TOK_TILE), lambda i: (0, i)),
        ],
        out_shape=[
            jax.ShapeDtypeStruct((N_TOK, d), F32),
            jax.ShapeDtypeStruct((N_TOK, EVEN_MAIN), F32),
            jax.ShapeDtypeStruct((N_GATES, N_TOK), F32),
        ],
        compiler_params=_cparams("parallel"),
        name="proj_even",
    )(x_prompt, x_sample, pos_tab, mod_l, norm_g.reshape(1, d), w_main, w_gate.T, gate_b.reshape(N_GATES, 1))


def _proj_odd_kernel(x_ref, mod_ref, g_ref, w_ref, p_ref):
    h = _norm_mod(x_ref[...], g_ref[...], mod_ref[1:2, :], mod_ref[0:1, :]).astype(BF16)
    p_ref[...] = _dot(h, w_ref[...])


def _proj_odd(x, mod_l, norm_g, w_in):
    d = D_MODEL
    width = w_in.shape[1]
    return pl.pallas_call(
        _proj_odd_kernel,
        grid=(N_TOK // TOK_TILE,),
        in_specs=[
            pl.BlockSpec((TOK_TILE, d), lambda i: (i, 0)),
            pl.BlockSpec((None, N_MOD, d), lambda i: (_tile_cond_row(i), 0, 0)),
            pl.BlockSpec((1, d), lambda i: (0, 0)),
            pl.BlockSpec((d, width), lambda i: (0, 0), pipeline_mode=pl.Buffered(1)),
        ],
        out_specs=pl.BlockSpec((TOK_TILE, width), lambda i: (i, 0)),
        out_shape=jax.ShapeDtypeStruct((N_TOK, width), F32),
        compiler_params=_cparams("parallel"),
        name="proj_odd",
    )(x, mod_l, norm_g.reshape(1, d), w_in.astype(BF16))


def _hgrn_chunks(chains, rows_ref):
    c = SCAN_CHUNK
    n = range(len(chains))
    rev, q, k, v, lf, st = zip(*chains)
    row = lax.broadcasted_iota(jnp.int32, (c, c), 0)
    col = lax.broadcasted_iota(jnp.int32, (c, c), 1)
    tris = {False: col <= row, True: col >= row}
    tris_b = {r: jnp.where(t, 1.0, 0.0).astype(BF16) for r, t in tris.items()}
    b = [_dot_w3(tris_b[rev[j]], lf[j]) for j in n]
    b2 = [b[j] * LOG2_E for j in n]
    for j in n:
        rows_ref[j, 0] = b2[j]
        rows_ref[j, 1] = k[j]
    lane_half = lax.broadcasted_iota(jnp.int32, (HALF, c), 1)
    chunk_row = lax.broadcasted_iota(jnp.int32, (c, 1), 0)
    rows = [[] for _ in n]
    for i in range(c // SUB):
        lo, hi = i * SUB, (i + 1) * SUB
        a_row = []
        for j in n:
            if rev[j]:
                has_off, edge, outside = hi < c, hi, chunk_row >= hi
            else:
                has_off, edge, outside = lo > 0, lo - 1, chunk_row < lo
            if has_off:
                beta = b[j][edge:edge + 1]
                qs = q[j][lo:hi] * jnp.exp(b[j][lo:hi] - beta)
                ks = k[j] * jnp.exp(jnp.where(outside, beta - b[j], -jnp.inf))
                a_row.append(_dot_nt(qs.astype(BF16), ks.astype(BF16)))
            else:
                a_row.append(jnp.zeros((SUB, c), F32))
        for half in range(SUB // HALF):
            h0 = lo + half * HALF
            piece = [a_row[j][half * HALF:(half + 1) * HALF] for j in n]
            for s in range(HALF):
                for j in n:
                    b_s = rows_ref[j, 0, h0 + s:h0 + s + 1, :]
                    k_s = rows_ref[j, 1, h0 + s:h0 + s + 1, :]
                    a_col = jnp.sum(jnp.exp2(b2[j][h0:h0 + HALF] - b_s) * q[j][h0:h0 + HALF] * k_s,
                                    axis=-1, keepdims=True)
                    piece[j] = jnp.where(lane_half == h0 + s, a_col, piece[j])
            for j in n:
                rows[j].append(piece[j])
    second_half = (chunk_row % SUB) >= HALF
    same_block = (row // SUB) == (col // SUB)
    out = []
    for j in n:
        meet = HALF if rev[j] else HALF - 1
        beta = jnp.concatenate([jnp.broadcast_to(b[j][lo + meet:lo + meet + 1], (SUB, b[j].shape[1]))
                                for lo in range(0, c, SUB)], axis=0)
        t_side = jnp.logical_not(second_half) if rev[j] else second_half
        qs = q[j] * jnp.exp(jnp.where(t_side, b[j] - beta, -jnp.inf))
        ks = k[j] * jnp.exp(jnp.where(t_side, -jnp.inf, beta - b[j]))
        cross = jnp.where(same_block, _dot_nt(qs.astype(BF16), ks.astype(BF16)), 0.0)
        attn = jnp.where(tris[rev[j]], jnp.concatenate(rows[j], axis=0) + cross, 0.0)
        o = _dot(attn.astype(BF16), v[j].astype(BF16)) \
            + _dot_nt((q[j] * jnp.exp(b[j])).astype(BF16), st[j].astype(BF16))
        b_exit = b[j][0:1] if rev[j] else b[j][c - 1:c]
        k_out = k[j] * jnp.exp(b_exit - b[j])
        st_new = jnp.exp(b_exit) * st[j] + _dot_tn(v[j].astype(BF16), k_out.astype(BF16))
        out.append((o, st_new))
    return out


def _hgrn_kernel(seq_len, layer, q_ref, i_ref, g_ref, ff_ref, fb_ref, lb_ref, s0_ref,
                 o_ref, s_out_ref, of_ref, ob_ref, st_ref, rows_ref):
    c = SCAN_CHUNK
    hd = HEAD_DIM
    n_chunks = seq_len // c
    lbp = lb_ref[...]
    e = jnp.exp(lbp - jnp.max(lbp, axis=0, keepdims=True))
    lb = jnp.sum(e[0:layer + 1], axis=0, keepdims=True) / jnp.sum(e, axis=0, keepdims=True)

    for d in range(2):
        for hh in range(HG_STEP_HEADS):
            st_ref[d, hh] = s0_ref[d, hh].T

    def body(n, carry):
        where, chains = [], []
        for d in range(2):
            sl = pl.ds(pl.multiple_of((n_chunks - 1 - n if d else n) * c, c), c)
            for hh in range(HG_STEP_HEADS):
                cols = slice(hh * hd, (hh + 1) * hd)
                f = lb[:, cols] + (1.0 - lb[:, cols]) * _sigmoid((fb_ref if d else ff_ref)[sl, cols])
                where.append((d, hh, sl, cols))
                chains.append((bool(d), q_ref[sl, cols], 1.0 - f, i_ref[sl, cols], jnp.log(f), st_ref[d, hh]))
        for (d, hh, sl, cols), (o, st_new) in zip(where, _hgrn_chunks(chains, rows_ref)):
            st_ref[d, hh] = st_new
            (ob_ref if d else of_ref)[sl, cols] = o
        return carry

    lax.fori_loop(0, n_chunks, body, 0)
    for hh in range(HG_STEP_HEADS):
        cols = slice(hh * hd, (hh + 1) * hd)
        o = of_ref[:, cols] + ob_ref[:, cols]
        o = o * lax.rsqrt(jnp.mean(o * o, axis=-1, keepdims=True) + RMS_EPS)
        o_ref[:, cols] = o * _silu(g_ref[:, cols])
        for d in range(2):
            s_out_ref[d, hh] = st_ref[d, hh].T


def _hgrn(p, hg_lb, layer, s0, n_seq, seq_len, tok_offset):
    hd = HEAD_DIM
    sh = HG_STEP_HEADS
    steps_per_seq = HEADS // sh
    row0 = tok_offset // seq_len

    def col(part):
        return pl.BlockSpec((seq_len, sh * hd), lambda b, h: (row0 + b, part * steps_per_seq + h))

    state_spec = pl.BlockSpec((None, 2, sh, hd, hd), lambda b, h: (b, 0, h, 0, 0))
    return pl.pallas_call(
        functools.partial(_hgrn_kernel, seq_len, layer),
        grid=(n_seq, steps_per_seq),
        in_specs=[col(0), col(1), col(2), col(3), col(4),
                  pl.BlockSpec((DEPTH + 1, sh * hd), lambda b, h: (0, h)),
                  state_spec],
        out_specs=[pl.BlockSpec((seq_len, sh * hd), lambda b, h: (b, h)), state_spec],
        out_shape=[jax.ShapeDtypeStruct((n_seq * seq_len, GROUP_W), F32),
                   jax.ShapeDtypeStruct((n_seq, 2, HEADS, hd, hd), F32)],
        scratch_shapes=[pltpu.VMEM((seq_len, sh * hd), F32), pltpu.VMEM((seq_len, sh * hd), F32),
                        pltpu.VMEM((2, sh, hd, hd), F32), pltpu.VMEM((2 * sh, 2, SCAN_CHUNK, hd), F32)],
        compiler_params=_cparams("parallel", "parallel"),
        name=f"hgrn_l{seq_len}",
    )(p, p, p, p, p, hg_lb, s0)


def _short_conv3(x, w):
    n = x.shape[0]
    r = lax.broadcasted_iota(jnp.int32, (n, 1), 0)
    prev = jnp.where(r == 0, 0.0, pltpu.roll(x, 1, 0))
    nxt = jnp.where(r == n - 1, 0.0, pltpu.roll(x, n - 1, 0))
    return prev * w[0:1] + x * w[1:2] + nxt * w[2:3]


def _conv3_silu_tiles(dst_ref, src_ref, w_ref, scale):
    n, width = src_ref.shape
    c = SCAN_CHUNK
    r = lax.broadcasted_iota(jnp.int32, (c, 1), 0)
    for j in range(n // c):
        r0 = j * c
        for col in range(0, width, LANES):
            cols = slice(col, col + LANES)
            cur = src_ref[r0:r0 + c, cols]
            prev = jnp.where(r == 0, 0.0, pltpu.roll(cur, 1, 0)) if j == 0 else src_ref[r0 - 1:r0 + c - 1, cols]
            nxt = (jnp.where(r == c - 1, 0.0, pltpu.roll(cur, c - 1, 0)) if r0 + c == n
                   else src_ref[r0 + 1:r0 + c + 1, cols])
            w = w_ref[:, cols]
            y = _silu(prev * w[0:1] + cur * w[1:2] + nxt * w[2:3])
            dst_ref[r0:r0 + c, cols] = y if scale == 1.0 else y * scale


def _split2(x):
    hi = x.astype(BF16)
    return hi, (x - hi.astype(F32)).astype(BF16)


def _mlstm_chunks(chains):
    c = SCAN_CHUNK
    row = lax.broadcasted_iota(jnp.int32, (c, c), 0)
    col = lax.broadcasted_iota(jnp.int32, (c, c), 1)
    eye_b = jnp.where(row == col, 1.0, 0.0).astype(BF16)
    tris = {False: row <= col, True: row >= col}
    tris_b = {r: jnp.where(t, 1.0, 0.0).astype(BF16) for r, t in tris.items()}
    n = range(len(chains))
    rev, q, k, vt, ig, fg, ct, nv, m_prev = zip(*chains)

    def each(fn):
        return [fn(i) for i in n]

    def dot3(parts, rhs, nt=False):
        d = _dot_nt if nt else _dot
        return d(parts[0], rhs) + d(parts[1], rhs) + d(parts[2], rhs)

    lf = each(lambda i: _split3(jnp.broadcast_to(_log_sigmoid(fg[i]), (8, c))))
    b = each(lambda i: dot3(lf[i], tris_b[rev[i]])[0:1])
    us = each(lambda i: _split3(jnp.broadcast_to(ig[i] - b[i], (c, c))))
    u = each(lambda i: _dot_nt(eye_b, us[i][0]) + _dot_nt(eye_b, us[i][1]) + _dot_nt(eye_b, us[i][2]))
    dmat = each(lambda i: jnp.where(tris[rev[i]], b[i] + u[i], -jnp.inf))
    m_t = each(lambda i: jnp.maximum(b[i] + m_prev[i], jnp.max(dmat[i], axis=0, keepdims=True)))
    qb = each(lambda i: q[i].astype(BF16))
    kb = each(lambda i: k[i].astype(BF16))
    kq = each(lambda i: _dot_nt(kb[i], qb[i]))
    p = each(lambda i: jnp.exp(dmat[i] - m_t[i]) * kq[i])
    inter = each(lambda i: jnp.exp(b[i] + m_prev[i] - m_t[i]))
    ns = each(lambda i: _split2(jnp.broadcast_to(nv[i], (8, nv[i].shape[1]))))
    qs = each(lambda i: _split2(q[i]))
    qn = each(lambda i: (_dot_nt(ns[i][0], qs[i][0]) + _dot_nt(ns[i][1], qs[i][0])
                         + _dot_nt(ns[i][0], qs[i][1]))[0:1])
    den = each(lambda i: inter[i] * qn[i] + jnp.sum(p[i], axis=0, keepdims=True))
    scale = each(lambda i: 1.0 / jnp.maximum(jnp.abs(den[i]), jnp.exp(-m_t[i])))
    cq = each(lambda i: _dot_nt(ct[i].astype(BF16), qb[i]))
    vp = each(lambda i: _dot(vt[i].astype(BF16), p[i].astype(BF16)))
    ht = each(lambda i: (inter[i] * cq[i] + vp[i]) * scale[i])
    last = each(lambda i: 0 if rev[i] else c - 1)
    m_new = each(lambda i: m_t[i][:, last[i]:last[i] + 1])
    b_exit = each(lambda i: b[i][:, last[i]:last[i] + 1])
    w = each(lambda i: jnp.exp(b_exit[i] - b[i] + ig[i] - m_new[i]))
    dec = each(lambda i: jnp.exp(b_exit[i] + m_prev[i] - m_new[i]))
    vk = each(lambda i: _dot((vt[i] * w[i]).astype(BF16), kb[i]))
    ct_new = each(lambda i: dec[i] * ct[i] + vk[i])
    ws = each(lambda i: _split2(jnp.broadcast_to(w[i], (8, c))))
    ks = each(lambda i: _split2(k[i]))
    wk = each(lambda i: (_dot(ws[i][0], ks[i][0]) + _dot(ws[i][1], ks[i][0]) + _dot(ws[i][0], ks[i][1]))[0:1])
    nv_new = each(lambda i: dec[i] * nv[i] + wk[i])
    return [(ht[i], ct_new[i], nv_new[i], m_new[i]) for i in n]


def _mlstm_kernel(seq_len, q_ref, k_ref, v_ref, og_ref, gate_t_ref, cwq_ref, cwk_ref,
                  c0_ref, n0_ref, m0_ref,
                  o_ref, c_out_ref, n_out_ref, m_out_ref,
                  q2_ref, k2_ref, vt_ref, hf_ref, hb_ref, ct_ref, n_ref, m_ref):
    c = SCAN_CHUNK
    hd = HEAD_DIM
    n_chunks = seq_len // c
    _conv3_silu_tiles(q2_ref, q_ref, cwq_ref, 1.0)
    _conv3_silu_tiles(k2_ref, k_ref, cwk_ref, HEAD_DIM ** -0.5)
    for hh in range(HEADS):
        cols = slice(hh * hd, (hh + 1) * hd)
        for j in range(n_chunks):
            vt_ref[hh, :, j * c:(j + 1) * c] = v_ref[j * c:(j + 1) * c, cols].T
        for d in range(2):
            ct_ref[d, hh] = c0_ref[d, hh].T
    n_ref[...] = n0_ref[...]
    m_ref[...] = m0_ref[...]

    def body(n, carry):
        where, chains = [], []
        for d in range(2):
            sl = pl.ds(pl.multiple_of((n_chunks - 1 - n if d else n) * c, c), c)
            for hh in range(HEADS):
                cols = slice(hh * hd, (hh + 1) * hd)
                gr = gate_t_ref[hh, :, sl]
                where.append((d, hh, sl))
                chains.append((bool(d), q2_ref[sl, cols], k2_ref[sl, cols], vt_ref[hh, :, sl],
                               gr[2 * d:2 * d + 1, :], gr[2 * d + 1:2 * d + 2, :],
                               ct_ref[d, hh], n_ref[d, hh], m_ref[d, hh]))
        for (d, hh, sl), (ht, ct, nv, m_new) in zip(where, _mlstm_chunks(chains)):
            ct_ref[d, hh] = ct
            n_ref[d, hh] = nv
            m_ref[d, hh] = m_new
            (hb_ref if d else hf_ref)[hh, :, sl] = ht
        return carry

    lax.fori_loop(0, n_chunks, body, 0)
    for hh in range(HEADS):
        cols = slice(hh * hd, (hh + 1) * hd)
        for j in range(n_chunks):
            rows = slice(j * c, (j + 1) * c)
            h = (hf_ref[hh, :, rows] + hb_ref[hh, :, rows]).T
            h = h * lax.rsqrt(jnp.mean(h * h, axis=-1, keepdims=True) + RMS_EPS)
            o_ref[rows, cols] = h * _sigmoid(og_ref[rows, cols])
        for d in range(2):
            c_out_ref[d, hh] = ct_ref[d, hh].T
    n_out_ref[...] = n_ref[...]
    m_out_ref[...] = m_ref[...]


def _mlstm(p, gate_t_h, conv_w, c0, n0, m0, n_seq, seq_len, tok_offset):
    hd = HEAD_DIM
    gw = GROUP_W
    row0 = tok_offset // seq_len

    def col(part):
        return pl.BlockSpec((seq_len, gw), lambda b: (row0 + b, part))

    c_spec = pl.BlockSpec((None, 2, HEADS, hd, hd), lambda b: (b, 0, 0, 0, 0))
    n_spec = pl.BlockSpec((None, 2, HEADS, 1, hd), lambda b: (b, 0, 0, 0, 0))
    m_spec = pl.BlockSpec((None, 2, HEADS, 1, 1), lambda b: (b, 0, 0, 0, 0))
    return pl.pallas_call(
        functools.partial(_mlstm_kernel, seq_len),
        grid=(n_seq,),
        in_specs=[col(5), col(6), col(7), col(8),
                  pl.BlockSpec((HEADS, 4, seq_len), lambda b: (0, 0, row0 + b)),
                  pl.BlockSpec((3, gw), lambda b: (0, 0)),
                  pl.BlockSpec((3, gw), lambda b: (0, 1)),
                  c_spec, n_spec, m_spec],
        out_specs=[pl.BlockSpec((seq_len, gw), lambda b: (b, 0)), c_spec, n_spec, m_spec],
        out_shape=[jax.ShapeDtypeStruct((n_seq * seq_len, gw), F32),
                   jax.ShapeDtypeStruct((n_seq, 2, HEADS, hd, hd), F32),
                   jax.ShapeDtypeStruct((n_seq, 2, HEADS, 1, hd), F32),
                   jax.ShapeDtypeStruct((n_seq, 2, HEADS, 1, 1), F32)],
        scratch_shapes=[pltpu.VMEM((seq_len, gw), F32), pltpu.VMEM((seq_len, gw), F32),
                        pltpu.VMEM((HEADS, hd, seq_len), F32),
                        pltpu.VMEM((HEADS, hd, seq_len), F32), pltpu.VMEM((HEADS, hd, seq_len), F32),
                        pltpu.VMEM((2, HEADS, hd, hd), F32), pltpu.VMEM((2, HEADS, 1, hd), F32),
                        pltpu.VMEM((2, HEADS, 1, 1), F32)],
        compiler_params=_cparams("parallel"),
        name=f"mlstm_l{seq_len}",
    )(p, p, p, p, gate_t_h, conv_w, conv_w, c0, n0, m0)


def _dft_tables(seq_len):
    n = 2 * seq_len
    k = jnp.arange(seq_len, dtype=jnp.int32)[:, None]
    t = jnp.arange(seq_len, dtype=jnp.int32)[None, :]
    ang = ((k * t) % n).astype(F32) * (2.0 * np.pi / n)
    fc = jnp.cos(ang)
    fs = jnp.sin(ang)
    nyq = jnp.where(t % 2 == 0, 1.0, -1.0).astype(F32)
    fs = jnp.where(k == 0, nyq, fs)
    return jnp.concatenate([fc, fs], axis=0)


def _filter_kernel(seq_len, z_ref, w1_ref, b1_ref, w2_ref, b2_ref, w3f_ref, w3b_ref, b3f_ref, b3b_ref,
                   f0_ref, f1_ref, rf_ref, rb_ref, fhi_ref, flo_ref, kf_ref, a_ref):
    hp = lax.Precision.HIGHEST
    n = 2 * seq_len
    z = z_ref[...]

    @pl.when(jnp.logical_and(pl.program_id(0) == 0, pl.program_id(1) == 0))
    def _():
        a1 = jnp.sin(f0_ref[...] * (jnp.dot(z, w1_ref[...], precision=hp, preferred_element_type=F32)
                                    + b1_ref[...]))
        a_ref[...] = jnp.sin(f1_ref[...] * (jnp.dot(a1, w2_ref[...], precision=hp, preferred_element_type=F32)
                                            + b2_ref[...]))

    a = a_ref[...]
    t_norm = z[:, 0:1]
    hf = (jnp.dot(a, w3f_ref[...], precision=hp, preferred_element_type=F32) + b3f_ref[...]) \
        * jnp.exp(-t_norm * jnp.exp(rf_ref[...]))
    hb = (jnp.dot(a, w3b_ref[...], precision=hp, preferred_element_type=F32) + b3b_ref[...]) \
        * jnp.exp(-t_norm * jnp.exp(rb_ref[...]))
    inv = lax.rsqrt(jnp.sum(hf * hf, axis=0, keepdims=True) + jnp.sum(hb * hb, axis=0, keepdims=True))
    hf = hf * inv
    r = lax.broadcasted_iota(jnp.int32, (seq_len, 1), 0)
    hb = jnp.where(r == 0, 0.0, hb * inv)
    sh, sl = _split2(hf + hb)
    dh, dl = _split2(hf - hb)
    fhi = fhi_ref[...]
    flo = flo_ref[...]
    kc = _dot(fhi[:seq_len], sh) + _dot(fhi[:seq_len], sl) + _dot(flo[:seq_len], sh)
    ks = _dot(fhi[seq_len:], dh) + _dot(fhi[seq_len:], dl) + _dot(flo[seq_len:], dh)
    sign = jnp.where(r % 2 == 0, 1.0, -1.0)
    k_nyq = jnp.sum(sign * (hf + hb), axis=0, keepdims=True)
    ks = jnp.where(r == 0, k_nyq, ks)
    scale = jnp.where(r == 0, 1.0 / n, 2.0 / n)
    kf_ref[0:seq_len, :] = kc * scale
    kf_ref[seq_len:n, :] = ks * scale


def _hyena_filters(seq_len, w1, b1, w2, b2, w3, b3, freq, log_rate, f_tab):
    d = D_MODEL
    cb = 256
    t = jnp.arange(seq_len, dtype=F32)
    t_norm = t / (seq_len - 1)
    bands = jnp.linspace(1e-4, HY_BANDS - 1, HY_BANDS, dtype=F32)
    ang = (2.0 * np.pi / seq_len) * t[:, None] * bands[None, :]
    z = jnp.concatenate([t_norm[:, None], jnp.cos(ang), jnp.sin(ang)], axis=-1)
    kpad = 128 - HY_EMB
    z = jnp.pad(z, ((0, 0), (0, kpad)))
    w1p = jnp.pad(w1, ((0, kpad), (0, 0)))
    f_hi = f_tab.astype(BF16)
    f_lo = (f_tab - f_hi.astype(F32)).astype(BF16)
    n_cb = d // cb
    hh = HY_HIDDEN
    row = lambda a: a.reshape(1, -1)
    const = lambda shape: pl.BlockSpec(shape, lambda o, j: (0,) * len(shape))
    fwd = lambda rows: pl.BlockSpec((rows, cb), lambda o, j: (0, o * n_cb + j))
    bwd = lambda rows: pl.BlockSpec((rows, cb), lambda o, j: (0, (HY_ORDER + o) * n_cb + j))
    return pl.pallas_call(
        functools.partial(_filter_kernel, seq_len),
        grid=(HY_ORDER, n_cb),
        in_specs=[const((seq_len, 128)), const((128, hh)), const((1, hh)), const((hh, hh)), const((1, hh)),
                  fwd(hh), bwd(hh), fwd(1), bwd(1),
                  const((1, hh)), const((1, hh)), fwd(1), bwd(1),
                  const((2 * seq_len, seq_len)), const((2 * seq_len, seq_len))],
        out_specs=pl.BlockSpec((None, 2 * seq_len, cb), lambda o, j: (o, 0, j)),
        out_shape=jax.ShapeDtypeStruct((HY_ORDER, 2 * seq_len, d), F32),
        scratch_shapes=[pltpu.VMEM((seq_len, hh), F32)],
        compiler_params=_cparams("arbitrary", "arbitrary"),
        name=f"hyena_filter_l{seq_len}",
    )(z, w1p, row(b1), w2, row(b2), w3, w3, row(b3), row(b3),
      row(freq[0]), row(freq[1]), row(log_rate), row(log_rate), f_hi, f_lo)


def _hyena_kernel(seq_len, seqs, v_ref, x1_ref, x2_ref, cwv_ref, cw1_ref, cw2_ref, bias_ref, kf_ref,
                  f_ref, ft_ref, o_ref, z_ref, zb_ref, y_ref):
    kc = min(HY_FREQ_CHUNK, seq_len)
    n_k = seq_len // kc
    r = lax.broadcasted_iota(jnp.int32, (kc, 1), 0)
    gate_refs = ((x1_ref, cw1_ref), (x2_ref, cw2_ref))
    for s in range(seqs):
        rows = slice(s * seq_len, (s + 1) * seq_len)
        z_ref[s] = _short_conv3(v_ref[rows, :], cwv_ref[...])
    for o in range(HY_ORDER):
        for s in range(seqs):
            zb_ref[s] = z_ref[s].astype(BF16)
            y_ref[s] = jnp.zeros(y_ref.shape[1:], F32)

        def freq_chunk(j, carry):
            r0 = pl.multiple_of(j * kc, kc)
            k_cos = kf_ref[o, pl.ds(r0, kc), :]
            k_sin = kf_ref[o, pl.ds(seq_len + r0, kc), :]
            real_row = jnp.logical_and(r == 0, j == 0)
            for s in range(seqs):
                a = _dot(f_ref[pl.ds(r0, kc), :], zb_ref[s])
                bm = _dot(f_ref[pl.ds(seq_len + r0, kc), :], zb_ref[s])
                yc = a * k_cos - jnp.where(real_row, 0.0, bm * k_sin)
                ys = jnp.where(real_row, bm * k_sin, a * k_sin + bm * k_cos)
                y_ref[s] += _dot(ft_ref[j], yc.astype(BF16)) + _dot(ft_ref[n_k + j], ys.astype(BF16))
            return carry

        lax.fori_loop(0, n_k, freq_chunk, 0)
        x_ref, cw_ref = gate_refs[o]
        for s in range(seqs):
            rows = slice(s * seq_len, (s + 1) * seq_len)
            gate = _short_conv3(x_ref[rows, :], cw_ref[...])
            z_ref[s] = gate * (y_ref[s] + z_ref[s] * bias_ref[o:o + 1, :])
    for s in range(seqs):
        o_ref[s * seq_len:(s + 1) * seq_len, :] = z_ref[s]


def _hyena(u, conv_w, bias, kf, f_tab, n_seq, seq_len, tok_offset):
    d = D_MODEL
    cb = 256
    n_cb = d // cb
    seqs = max(1, HY_STEP_ROWS // seq_len)
    rows = seqs * seq_len
    row0 = tok_offset // rows
    kc = min(HY_FREQ_CHUNK, seq_len)
    n_k = seq_len // kc
    f_bf = f_tab.astype(BF16)
    ft = f_bf.T.reshape(seq_len, 2 * n_k, kc).transpose(1, 0, 2)

    def part(k):
        return pl.BlockSpec((rows, cb), lambda j, b: (row0 + b, k * n_cb + j))

    def cw(k):
        return pl.BlockSpec((3, cb), lambda j, b: (0, k * n_cb + j))

    return pl.pallas_call(
        functools.partial(_hyena_kernel, seq_len, seqs),
        grid=(n_cb, n_seq // seqs),
        in_specs=[part(0), part(1), part(2), cw(0), cw(1), cw(2),
                  pl.BlockSpec((HY_ORDER, cb), lambda j, b: (0, j)),
                  pl.BlockSpec((HY_ORDER, 2 * seq_len, cb), lambda j, b: (0, 0, j)),
                  pl.BlockSpec((2 * seq_len, seq_len), lambda j, b: (0, 0)),
                  pl.BlockSpec((2 * n_k, seq_len, kc), lambda j, b: (0, 0, 0))],
        out_specs=pl.BlockSpec((rows, cb), lambda j, b: (b, j)),
        out_shape=jax.ShapeDtypeStruct((n_seq * seq_len, d), F32),
        scratch_shapes=[pltpu.VMEM((seqs, seq_len, cb), F32), pltpu.VMEM((seqs, seq_len, cb), BF16),
                        pltpu.VMEM((seqs, seq_len, cb), F32)],
        compiler_params=_cparams("parallel", "parallel"),
        name=f"hyena_l{seq_len}",
    )(u, u, u, conv_w, conv_w, conv_w, bias, kf, f_bf, ft)


def _mix_out_kernel(n_parts, *refs):
    o_refs = refs[:2 * n_parts]
    w_refs = refs[2 * n_parts:3 * n_parts]
    x_ref, mod_ref, g_ref, wr_both_ref, br_ref = refs[3 * n_parts:3 * n_parts + 5]
    x1_ref, h_ref, eid_ref, wts_ref, rank_ref, cnt_ref, run_ref = refs[3 * n_parts + 5:]
    i = pl.program_id(0)
    t = ROUTE_TILE
    ne = N_EXPERTS

    from_prompt = i < N_PROMPT // ROUTE_TILE
    y = None
    for j in range(n_parts):
        o = jnp.where(from_prompt, o_refs[2 * j][...], o_refs[2 * j + 1][...])
        yj = _dot(o.astype(BF16), w_refs[j][...])
        y = yj if y is None else y + yj
    x1 = x_ref[...] + mod_ref[2:3, :] * y
    x1_ref[...] = x1
    h = _norm_mod(x1, g_ref[...], mod_ref[4:5, :], mod_ref[3:4, :])
    h_ref[...] = _pack_bf16_halves(h)
    h_hi = h.astype(BF16)
    h_lo = (h - h_hi.astype(F32)).astype(BF16)
    hi_terms = _dot(h_hi, wr_both_ref[...])
    logits = (hi_terms[:, :LANES] + hi_terms[:, LANES:] + _dot(h_lo, wr_both_ref[:, :LANES])).T[:ne] \
        + br_ref[...]

    @pl.when(i == 0)
    def _():
        run_ref[...] = jnp.zeros_like(run_ref)

    e_iota = lax.broadcasted_iota(jnp.int32, (ne, t), 0)
    vals, eids, onehots = [], [], []
    for _k in range(TOP_K):
        m = jnp.max(logits, axis=0, keepdims=True)
        eid = jnp.min(jnp.where(logits == m, e_iota, ne), axis=0, keepdims=True)
        sel = e_iota == eid
        logits = jnp.where(sel, -jnp.inf, logits)
        onehots.append(jnp.where(sel, 1.0, 0.0))
        vals.append(m)
        eids.append(eid)
    r2 = lax.broadcasted_iota(jnp.int32, (t, t), 0)
    c2 = lax.broadcasted_iota(jnp.int32, (t, t), 1)
    before = jnp.where(r2 < c2, 1.0, 0.0).astype(BF16)
    earlier = _dot(jnp.concatenate(onehots, axis=0).astype(BF16), before)
    running = run_ref[...]
    ranks = []
    for k, onehot in enumerate(onehots):
        ranks.append(jnp.sum(onehot * (running + earlier[k * ne:(k + 1) * ne]), axis=0, keepdims=True))
        running = running + jnp.sum(onehot, axis=1, keepdims=True)
    run_ref[...] = running
    cnt_ref[...] = running
    v = jnp.concatenate(vals, axis=0)
    ex = jnp.exp(v - v[0:1])
    wts_ref[...] = ex / jnp.sum(ex, axis=0, keepdims=True)
    eid_ref[...] = jnp.concatenate(eids, axis=0)
    rank_ref[...] = jnp.concatenate(ranks, axis=0).astype(jnp.int32)


def _mix_out(parts, x, mod_l, norm_g, w_router, b_router):
    d = D_MODEL
    t = ROUTE_TILE
    ne = N_EXPERTS
    n_parts = len(parts)
    n_prompt_tiles = N_PROMPT // t
    wr = jnp.pad(w_router, ((0, 0), (0, LANES - ne)))
    wr_hi = wr.astype(BF16)
    wr_both = jnp.concatenate([wr_hi, (wr - wr_hi.astype(F32)).astype(BF16)], axis=1)
    in_specs = []
    for o_p, _, _ in parts:
        in_specs.append(pl.BlockSpec((t, o_p.shape[1]), lambda i: (jnp.minimum(i, n_prompt_tiles - 1), 0)))
        in_specs.append(pl.BlockSpec((t, o_p.shape[1]), lambda i: (jnp.maximum(i - n_prompt_tiles, 0), 0)))
    in_specs += [pl.BlockSpec(w.shape, lambda i: (0, 0)) for _, _, w in parts]
    in_specs += [
        pl.BlockSpec((t, d), lambda i: (i, 0)),
        pl.BlockSpec((None, N_MOD, d), lambda i: (_tile_cond_row(i, t), 0, 0)),
        pl.BlockSpec((1, d), lambda i: (0, 0)),
        pl.BlockSpec((d, 2 * LANES), lambda i: (0, 0)),
        pl.BlockSpec((ne, 1), lambda i: (0, 0)),
    ]
    tok_major = pl.BlockSpec((TOP_K, t), lambda i: (0, i))
    return pl.pallas_call(
        functools.partial(_mix_out_kernel, n_parts),
        grid=(N_TOK // t,),
        in_specs=in_specs,
        out_specs=[pl.BlockSpec((t, d), lambda i: (i, 0)), pl.BlockSpec((t, d // 2), lambda i: (i, 0)),
                   tok_major, tok_major, tok_major, pl.BlockSpec((ne, 1), lambda i: (0, 0))],
        out_shape=[jax.ShapeDtypeStruct((N_TOK, d), F32), jax.ShapeDtypeStruct((N_TOK, d // 2), jnp.uint32),
                   jax.ShapeDtypeStruct((TOP_K, N_TOK), jnp.int32), jax.ShapeDtypeStruct((TOP_K, N_TOK), F32),
                   jax.ShapeDtypeStruct((TOP_K, N_TOK), jnp.int32), jax.ShapeDtypeStruct((ne, 1), F32)],
        scratch_shapes=[pltpu.VMEM((ne, 1), F32)],
        compiler_params=_cparams("arbitrary"),
        name="mix_out_router",
    )(*[o for part in parts for o in part[:2]], *[w.astype(BF16) for _, _, w in parts], x, mod_l,
      norm_g.reshape(1, d), wr_both, b_router.reshape(ne, 1))


def _sc_row_gather(src, idx):
    n = idx.shape[0]
    width = src.shape[1]
    step_rows = min(SC_MAX_INDICES, SC_STEP_BYTES // (width * 4))
    sc = plsc.get_sparse_core_info()
    n_workers = sc.num_cores * sc.num_subcores
    per_worker = n // n_workers
    if n % n_workers or per_worker % step_rows:
        raise ValueError("row count must be whole SparseCore steps on every subcore")
    mesh = plsc.VectorSubcoreMesh(core_axis_name="core", subcore_axis_name="subcore")

    @functools.partial(pl.kernel, out_type=jax.ShapeDtypeStruct((n, width), src.dtype), mesh=mesh,
                       scratch_types=[pltpu.VMEM((per_worker,), jnp.int32),
                                      pltpu.VMEM((step_rows, width), src.dtype)],
                       name="sc_row_gather")
    def gather(src_hbm, idx_hbm, dst_hbm, idx_vmem, rows_vmem):
        worker = lax.axis_index("subcore") * sc.num_cores + lax.axis_index("core")
        base = worker * per_worker
        pltpu.sync_copy(idx_hbm.at[pl.ds(base, per_worker)], idx_vmem)

        @pl.loop(0, per_worker // step_rows)
        def _(c):
            pltpu.sync_copy(src_hbm.at[idx_vmem.at[pl.ds(c * step_rows, step_rows)]], rows_vmem)
            pltpu.sync_copy(rows_vmem, dst_hbm.at[pl.ds(base + c * step_rows, step_rows)])

    return gather(src, idx)


def _sc_row_scatter(src, rows_of_src):
    copies, n_src = rows_of_src.shape
    width = src.shape[1]
    step_rows = min(SC_MAX_INDICES, SC_STEP_BYTES // (width * 4))
    sc = plsc.get_sparse_core_info()
    n_workers = sc.num_cores * sc.num_subcores
    per_worker = n_src // n_workers
    steps = per_worker // step_rows
    if src.shape[0] != n_src or n_src % n_workers or per_worker % step_rows:
        raise ValueError("row counts must be whole SparseCore steps on every subcore")
    mesh = plsc.VectorSubcoreMesh(core_axis_name="core", subcore_axis_name="subcore")
    idx = rows_of_src.reshape(copies, n_workers, steps, step_rows).transpose(1, 2, 0, 3)

    @functools.partial(pl.kernel, out_type=jax.ShapeDtypeStruct((copies * n_src, width), src.dtype),
                       mesh=mesh,
                       scratch_types=[pltpu.VMEM((steps, copies, step_rows), jnp.int32),
                                      pltpu.VMEM((step_rows, width), src.dtype)],
                       name="sc_row_scatter")
    def scatter(src_hbm, idx_hbm, dst_hbm, idx_vmem, rows_vmem):
        worker = lax.axis_index("subcore") * sc.num_cores + lax.axis_index("core")
        pltpu.sync_copy(idx_hbm.at[worker], idx_vmem)

        @pl.loop(0, steps)
        def _(c):
            pltpu.sync_copy(src_hbm.at[pl.ds(worker * per_worker + c * step_rows, step_rows)], rows_vmem)
            for j in range(copies):
                pltpu.sync_copy(rows_vmem, dst_hbm.at[idx_vmem.at[c, j]])

    return scatter(src, idx)


def _pack_bf16_halves(x):
    w = x.shape[1] // 2
    bits = pltpu.bitcast(x.astype(BF16).astype(F32), jnp.uint32)
    return bits[:, :w] | (bits[:, w:] >> 16)


def _unpack_bf16_halves(p):
    hi = pltpu.bitcast(p & jnp.uint32(0xFFFF0000), F32).astype(BF16)
    lo = pltpu.bitcast(p << 16, F32).astype(BF16)
    return hi, lo


def _experts_kernel(layer, te_ref, first_ref, slot_ref, next_ref, rows_ref, nv_ref,
                    x_ref, wgu_hbm, bg_ref, bu_ref, wd_hbm, bd_ref, sel_ref,
                    y_ref, wgu_buf, wd_buf, wg_ref, wu_ref, wdb_ref, sem):
    i = pl.program_id(0)
    valid = i < nv_ref[0]
    half = DEINT_COLS // 2
    k_half = x_ref.shape[1]

    def fetch(expert, slot):
        return (pltpu.make_async_copy(wgu_hbm.at[layer, expert], wgu_buf.at[slot], sem.at[slot, 0]),
                pltpu.make_async_copy(wd_hbm.at[layer, expert], wd_buf.at[slot], sem.at[slot, 1]))

    @pl.when(i == 0)
    def _():
        for cp in fetch(te_ref[0], 0):
            cp.start()

    @pl.when(jnp.logical_and(valid, first_ref[i] == 1))
    def _():
        slot = slot_ref[i]
        for cp in fetch(te_ref[i], slot):
            cp.wait()

        @pl.when(next_ref[i] >= 0)
        def _():
            for cp in fetch(next_ref[i], 1 - slot):
                cp.start(priority=1)

        for c in range(wgu_buf.shape[2] // DEINT_COLS):
            w = wgu_buf[slot, :, c * DEINT_COLS:(c + 1) * DEINT_COLS].astype(BF16)
            split = _dot(w, sel_ref[...]).astype(BF16)
            wg_ref[:, c * half:(c + 1) * half] = split[:, :half]
            wu_ref[:, c * half:(c + 1) * half] = split[:, half:]
        wdb_ref[...] = wd_buf[slot].astype(BF16)

    def ffn(rows):
        x_l, x_r = _unpack_bf16_halves(x_ref[:rows, :])
        gl = _dot(x_l, wg_ref[:k_half, :]) + _dot(x_r, wg_ref[k_half:, :]) + bg_ref[...]
        up = _dot(x_l, wu_ref[:k_half, :]) + _dot(x_r, wu_ref[k_half:, :]) + bu_ref[...]
        gl = jnp.minimum(gl, SWIGLU_LIMIT)
        up = jnp.clip(up, -SWIGLU_LIMIT, SWIGLU_LIMIT)
        act = (up + 1.0) * gl * _sigmoid(SWIGLU_ALPHA * gl)
        y_ref[:rows, :] = _pack_bf16_halves(_dot(act.astype(BF16), wdb_ref[...]) + bd_ref[...])

    tm = x_ref.shape[0]
    quarter = tm // MOE_TILE_PARTS
    used = rows_ref[i]
    for part in range(1, MOE_TILE_PARTS + 1):
        rows = part * quarter
        fits = jnp.logical_and(used > rows - quarter, used <= rows)

        @pl.when(jnp.logical_and(valid, fits))
        def _(rows=rows):
            ffn(rows)
            if rows < tm:
                y_ref[rows:, :] = jnp.zeros((tm - rows, y_ref.shape[1]), y_ref.dtype)

    @pl.when(jnp.logical_not(valid))
    def _():
        y_ref[...] = jnp.zeros_like(y_ref)


def _experts(xs, tile_expert, tile_first, tile_rows, n_valid, layer, w_gu, b_gate, b_up, w_down, b_down):
    d = D_MODEL
    tm = MOE_TILE
    n_tiles = MOE_ROWS // tm
    ff = w_down.shape[2]
    half = DEINT_COLS // 2
    r = jnp.arange(DEINT_COLS)[:, None]
    c = jnp.arange(DEINT_COLS)[None, :]
    sel = (r == jnp.where(c < half, 2 * c, 2 * (c - half) + 1)).astype(BF16)
    group = jnp.cumsum(tile_first) - 1
    tile_slot = (group % 2).astype(jnp.int32)
    is_last_group = group == group[-1]
    following = jnp.concatenate([tile_expert[1:], tile_expert[-1:]])
    idx = jnp.arange(n_tiles, dtype=jnp.int32)
    group_end = jnp.max(jnp.where(group[None, :] == group[:, None], idx[None, :], -1), axis=1)
    tile_next = jnp.where(is_last_group, -1, following[group_end]).astype(jnp.int32)
    wspec = lambda k, n: pl.BlockSpec((None, None, k, n), lambda i, *_: (layer, _[0][i], 0, 0))
    grid_spec = pltpu.PrefetchScalarGridSpec(
        num_scalar_prefetch=6,
        grid=(n_tiles,),
        in_specs=[pl.BlockSpec((tm, d // 2), lambda i, *_: (i, 0)),
                  pl.BlockSpec(memory_space=pl.ANY), wspec(1, ff), wspec(1, ff),
                  pl.BlockSpec(memory_space=pl.ANY), wspec(1, d),
                  pl.BlockSpec((DEINT_COLS, DEINT_COLS), lambda i, *_: (0, 0))],
        out_specs=pl.BlockSpec((tm, d // 2), lambda i, *_: (i, 0)),
        scratch_shapes=[pltpu.VMEM((2, d, 2 * ff), F32), pltpu.VMEM((2, ff, d), F32),
                        pltpu.VMEM((d, ff), BF16), pltpu.VMEM((d, ff), BF16), pltpu.VMEM((ff, d), BF16),
                        pltpu.SemaphoreType.DMA((2, 2))],
    )
    return pl.pallas_call(
        functools.partial(_experts_kernel, layer),
        grid_spec=grid_spec,
        out_shape=jax.ShapeDtypeStruct((MOE_ROWS, d // 2), jnp.uint32),
        compiler_params=_cparams("arbitrary"),
        name="experts",
    )(tile_expert, tile_first, tile_slot, tile_next, tile_rows, n_valid, xs, w_gu, b_gate, b_up, w_down,
      b_down, sel)


def _combine_kernel(final, x_ref, g_ref, w_ref, mod_ref, fg_ref, o_ref):
    y_l = y_r = None
    for k in range(TOP_K):
        g_l, g_r = _unpack_bf16_halves(g_ref[k])
        wk = w_ref[:, k:k + 1]
        y_l = wk * g_l.astype(F32) if y_l is None else y_l + wk * g_l.astype(F32)
        y_r = wk * g_r.astype(F32) if y_r is None else y_r + wk * g_r.astype(F32)
    x = x_ref[...] + mod_ref[5:6, :] * jnp.concatenate([y_l, y_r], axis=1)
    if final:
        x = x * lax.rsqrt(jnp.mean(x * x, axis=-1, keepdims=True) + RMS_EPS) * fg_ref[...]
    o_ref[...] = x


def _combine(x1, gathered, wts, mod_l, final_g, final, tok0, n_tok):
    d = D_MODEL
    t = ROUTE_TILE
    tile0 = tok0 // t
    return pl.pallas_call(
        functools.partial(_combine_kernel, final),
        grid=(n_tok // t,),
        in_specs=[pl.BlockSpec((t, d), lambda i: (tile0 + i, 0)),
                  pl.BlockSpec((TOP_K, t, d // 2), lambda i: (0, i, 0)),
                  pl.BlockSpec((t, TOP_K), lambda i: (tile0 + i, 0)),
                  pl.BlockSpec((None, N_MOD, d), lambda i: (_tile_cond_row(tile0 + i, t), 0, 0)),
                  pl.BlockSpec((1, d), lambda i: (0, 0))],
        out_specs=pl.BlockSpec((t, d), lambda i: (i, 0)),
        out_shape=jax.ShapeDtypeStruct((n_tok, d), F32),
        compiler_params=_cparams("parallel"),
        name="moe_combine",
    )(x1, gathered, wts, mod_l, final_g.reshape(1, d))


def _moe(x1, h, eid, wts, rank, counts, mod_l, layer, w_gu, b_gu, w_down, b_down, final_g, final):
    d = D_MODEL
    tm = MOE_TILE
    n_tiles = MOE_ROWS // tm
    cnt = counts.reshape(N_EXPERTS).astype(jnp.int32)
    gsz = ((cnt + tm - 1) // tm) * tm
    ends = jnp.cumsum(gsz)
    offs = ends - gsz
    e_ids = jnp.arange(N_EXPERTS, dtype=jnp.int32)
    pos = jnp.sum(jnp.where(eid[..., None] == e_ids, offs, 0), axis=-1) + rank
    tile_start = jnp.arange(n_tiles, dtype=jnp.int32) * tm
    tile_expert = jnp.minimum(jnp.sum((ends[None, :] <= tile_start[:, None]).astype(jnp.int32), axis=1),
                              N_EXPERTS - 1)
    n_valid = (ends[-1:] // tm).astype(jnp.int32)
    last_valid = jnp.maximum(n_valid[0] - 1, 0)
    tile_expert = jnp.where(jnp.arange(n_tiles) < n_valid[0], tile_expert, tile_expert[last_valid])
    tile_first = jnp.concatenate([jnp.ones((1,), jnp.int32),
                                  (tile_expert[1:] != tile_expert[:-1]).astype(jnp.int32)])
    j = jnp.arange(tm, dtype=jnp.int32)[None, :]
    pad_used = j < (gsz - cnt)[:, None]
    n_unused_before = jnp.cumsum((~pad_used).reshape(-1).astype(jnp.int32)) - 1
    pad_pos = jnp.where(pad_used, (offs + cnt)[:, None] + j,
                        ends[-1] + n_unused_before.reshape(N_EXPERTS, tm))
    if N_EXPERTS * tm != N_TOK:
        raise ValueError("the filler slots are laid out as one extra copy of every token")

    xs = _sc_row_scatter(h, jnp.concatenate([pos, pad_pos.reshape(1, N_TOK)], axis=0))
    real_end = (offs + cnt)[tile_expert]
    tile_rows = jnp.clip(real_end - tile_start, 0, tm).astype(jnp.int32)
    ys = _experts(xs, tile_expert, tile_first, tile_rows, n_valid, layer, w_gu,
                  b_gu[:, :, None, 0::2], b_gu[:, :, None, 1::2], w_down, b_down[:, :, None, :])
    spans = ((0, N_PROMPT), (N_PROMPT, N_SAMPLE)) if final else ((0, N_TOK),)
    outs = []
    for tok0, n_tok in spans:
        gathered = _sc_row_gather(ys, pos[:, tok0:tok0 + n_tok].reshape(-1))
        outs.append(_combine(x1, gathered.reshape(TOP_K, n_tok, d // 2), wts.T, mod_l, final_g, final,
                             tok0, n_tok))
    return tuple(outs) if final else outs[0]


def _grid_positions(n_tok, d):
    rows = n_tok // GRID_W
    r, col = jnp.meshgrid(jnp.arange(rows, dtype=F32), jnp.arange(GRID_W, dtype=F32), indexing='ij')
    r = r.reshape(-1)
    col = col.reshape(-1)
    quarter = d // 4
    inv = 1.0 / (10000.0 ** (jnp.arange(quarter, dtype=F32) / quarter))
    ar = r[:, None] * inv[None]
    ac = col[:, None] * inv[None]
    return jnp.concatenate([jnp.sin(ar), jnp.cos(ar), jnp.sin(ac), jnp.cos(ac)], axis=-1)


def kernel(x_prompt, x_sample, state_hgrn, state_mlstm_c, state_mlstm_n, state_mlstm_m, c, c_ctx,
           norm_g, final_g, w_mod, b_mod, ev_w_in, ev_gate_b, ev_conv, hg_lb, ev_w_out,
           hy_w_in, hy_conv, hy_w1, hy_b1, hy_w2, hy_b2, hy_w3, hy_b3, hy_freq, hy_log_rate, hy_bias, hy_w_out,
           w_router, b_router, w_gu, b_gu, w_down, b_down):
    d = D_MODEL
    hd = HEAD_DIM
    cond = jnp.concatenate([c_ctx[None], c, jnp.zeros((N_COND - 1 - DEC_BATCH, d), F32)], axis=0)
    mod = _modulation(cond, w_mod, b_mod)
    pos_tab = jnp.concatenate([jnp.zeros((TOK_TILE, d), F32), _grid_positions(DEC_SEQ, d)], axis=0)

    groups = ((BATCH, SEQ, 0), (DEC_BATCH, DEC_SEQ, N_PROMPT))
    new_states = None
    for l in range(DEPTH):
        if l % 2 == 0:
            e = l // 2
            if l == 0:
                x, p, gate_t = _proj_even(x_prompt.reshape(N_PROMPT, d), x_sample.reshape(N_SAMPLE, d),
                                          pos_tab, mod[l], norm_g[l, 0], ev_w_in[e], ev_gate_b[e])
            else:
                raise NotImplementedError("only the first layer adds grid positions")
            gate_t_h = gate_t.reshape(4, HEADS, N_TOK).transpose(1, 0, 2)
            o_hg, o_ml = [], []
            for gi, (n_seq, seq_len, off) in enumerate(groups):
                if gi == 0:
                    s0 = jnp.zeros((n_seq, 2, HEADS, hd, hd), F32)
                    c0 = jnp.zeros((n_seq, 2, HEADS, hd, hd), F32)
                    n0 = jnp.zeros((n_seq, 2, HEADS, 1, hd), F32)
                    m0 = jnp.zeros((n_seq, 2, HEADS, 1, 1), F32)
                else:
                    s0 = state_hgrn[:, e]
                    c0 = state_mlstm_c[:, e]
                    n0 = state_mlstm_n[:, e].reshape(n_seq, 2, HEADS, 1, hd)
                    m0 = state_mlstm_m[:, e].reshape(n_seq, 2, HEADS, 1, 1)
                og, s_fin = _hgrn(p, hg_lb, l, s0, n_seq, seq_len, off)
                om, c_fin, n_fin, m_fin = _mlstm(p, gate_t_h, ev_conv[e], c0, n0, m0, n_seq, seq_len, off)
                o_hg.append(og)
                o_ml.append(om)
                if gi == 0:
                    new_states = (s_fin[:, None], c_fin[:, None],
                                  n_fin.reshape(n_seq, 1, 2, HEADS, hd), m_fin.reshape(n_seq, 1, 2, HEADS))
            parts = [(*o_hg, ev_w_out[e][:GROUP_W]), (*o_ml, ev_w_out[e][GROUP_W:])]
        else:
            o = l // 2
            u = _proj_odd(x, mod[l], norm_g[l, 0], hy_w_in[o])
            zs = []
            for n_seq, seq_len, off in groups:
                f_tab = _dft_tables(seq_len)
                kf = _hyena_filters(seq_len, hy_w1[o], hy_b1[o], hy_w2[o], hy_b2[o], hy_w3[o], hy_b3[o],
                                    hy_freq[o], hy_log_rate[o], f_tab)
                zs.append(_hyena(u, hy_conv[o], hy_bias[o], kf, f_tab, n_seq, seq_len, off))
            parts = [(*zs, hy_w_out[o])]
        x1, h, eid, wts, rank, counts = _mix_out(parts, x, mod[l], norm_g[l, 1], w_router[l], b_router[l])
        x = _moe(x1, h, eid, wts, rank, counts, mod[l], l, w_gu, b_gu, w_down, b_down,
                 final_g, final=(l == DEPTH - 1))

    y_prompt, y_sample = x
    return (y_prompt.reshape(BATCH, SEQ, d), y_sample.reshape(DEC_BATCH, DEC_SEQ, d)) + new_states
```

```python
import functools

import numpy as np
import jax
import jax.numpy as jnp
from jax import lax
from jax.experimental import pallas as pl
from jax.experimental.pallas import tpu as pltpu
from jax.experimental.pallas import tpu_sc as plsc

F32 = jnp.float32
BF16 = jnp.bfloat16

D_MODEL = 1024
BATCH = 32
SEQ = 256
DEPTH = 2
DEC_BATCH = 8
DEC_SEQ = 1024
GRID_W = 64
RMS_EPS = 1e-6
N_MOD = 6
LOG2_E = 1.4426950408889634
LANES = 128

HEADS = 4
HEAD_DIM = 128
GROUP_W = HEADS * HEAD_DIM
N_GATES = 4 * HEADS
EVEN_MAIN = 9 * GROUP_W

HY_ORDER = 2
HY_BANDS = 16
HY_EMB = 1 + 2 * HY_BANDS
HY_HIDDEN = 64
HY_FREQ_CHUNK = 512
HY_STEP_ROWS = 1024

N_EXPERTS = 32
TOP_K = 4
SWIGLU_LIMIT = 7.0
SWIGLU_ALPHA = 1.702

N_PROMPT = BATCH * SEQ
N_SAMPLE = DEC_BATCH * DEC_SEQ
N_TOK = N_PROMPT + N_SAMPLE
N_COND = 16

TOK_TILE = 512
ROUTE_TILE = 512
SCAN_CHUNK = 128
SUB = 16
HALF = 8
HG_STEP_HEADS = 4
MOE_TILE = 512
MOE_TILE_PARTS = 8
MOE_ROWS = N_TOK * TOP_K + N_EXPERTS * MOE_TILE
DEINT_COLS = 256
SC_STEP_BYTES = 256 * 1024
SC_MAX_INDICES = 128

VMEM_LIMIT = 56 * 1024 * 1024


def _cparams(*sem):
    return pltpu.CompilerParams(dimension_semantics=sem, vmem_limit_bytes=VMEM_LIMIT)


def _split3(x):
    hi = x.astype(BF16)
    r = x - hi.astype(F32)
    mid = r.astype(BF16)
    lo = (r - mid.astype(F32)).astype(BF16)
    return hi, mid, lo


def _dot(a, b):
    return jnp.dot(a, b, preferred_element_type=F32)


def _dot_nt(a, b):
    return lax.dot_general(a, b, (((1,), (1,)), ((), ())), preferred_element_type=F32)


def _dot_tn(a, b):
    return lax.dot_general(a, b, (((0,), (0,)), ((), ())), preferred_element_type=F32)


def _dot_w3(a_exact_bf16, x):
    hi, mid, lo = _split3(x)
    return _dot(a_exact_bf16, hi) + _dot(a_exact_bf16, mid) + _dot(a_exact_bf16, lo)


def _sigmoid(x):
    return 1.0 / (1.0 + jnp.exp(-x))


def _silu(x):
    return x * _sigmoid(x)


def _log_sigmoid(x):
    return jnp.minimum(x, 0.0) - jnp.log(1.0 + jnp.exp(-jnp.abs(x)))


def _tile_cond_row(i, tile=TOK_TILE):
    n_prompt_tiles = N_PROMPT // tile
    tiles_per_seq = DEC_SEQ // tile
    return jnp.where(i < n_prompt_tiles, 0, 1 + (i - n_prompt_tiles) // tiles_per_seq)


def _mod_kernel(cond_ref, w_ref, b_ref, o_ref):
    a = _silu(cond_ref[...]).astype(BF16)
    o_ref[...] = _dot(a, w_ref[...].astype(BF16)) + b_ref[...]


def _modulation(cond, w_mod, b_mod):
    d = D_MODEL
    out = pl.pallas_call(
        _mod_kernel,
        grid=(DEPTH, N_MOD),
        in_specs=[
            pl.BlockSpec((N_COND, d), lambda l, j: (0, 0)),
            pl.BlockSpec((None, d, d), lambda l, j: (l, 0, j)),
            pl.BlockSpec((None, 1, d), lambda l, j: (l, 0, j)),
        ],
        out_specs=pl.BlockSpec((None, None, N_COND, d), lambda l, j: (l, j, 0, 0)),
        out_shape=jax.ShapeDtypeStruct((DEPTH, N_MOD, N_COND, d), F32),
        compiler_params=_cparams("parallel", "parallel"),
        name="modulation",
    )(cond, w_mod, b_mod.reshape(DEPTH, 1, N_MOD * d))
    return out.transpose(0, 2, 1, 3)


def _norm_mod(x, g_row, scale_row, shift_row):
    ms = jnp.mean(x * x, axis=-1, keepdims=True)
    y = x * lax.rsqrt(ms + RMS_EPS) * g_row
    return y * (1.0 + scale_row) + shift_row


def _proj_even_kernel(xp_ref, xs_ref, pos_ref, mod_ref, g_ref, w_ref, wgt_ref, gbt_ref,
                      xres_ref, p_ref, gate_t_ref):
    from_prompt = pl.program_id(0) < N_PROMPT // TOK_TILE
    x = jnp.where(from_prompt, xp_ref[...], xs_ref[...]) + pos_ref[...]
    xres_ref[...] = x
    h = _norm_mod(x, g_ref[...], mod_ref[1:2, :], mod_ref[0:1, :]).astype(BF16)
    p_ref[...] = _dot(h, w_ref[...])
    gate_t_ref[...] = _dot_nt(wgt_ref[...], h) + gbt_ref[...]


def _proj_even(x_prompt, x_sample, pos_tab, mod_l, norm_g, w_in, gate_b):
    d = D_MODEL
    n_tiles = N_TOK // TOK_TILE
    n_prompt_tiles = N_PROMPT // TOK_TILE
    tiles_per_seq = DEC_SEQ // TOK_TILE
    w_main = w_in[:, :EVEN_MAIN].astype(BF16)
    w_gate = w_in[:, EVEN_MAIN:].astype(BF16)

    def pos_map(i):
        return (jnp.where(i < n_prompt_tiles, 0, 1 + (i - n_prompt_tiles) % tiles_per_seq), 0)

    return pl.pallas_call(
        _proj_even_kernel,
        grid=(n_tiles,),
        in_specs=[
            pl.BlockSpec((TOK_TILE, d), lambda i: (jnp.minimum(i, n_prompt_tiles - 1), 0)),
            pl.BlockSpec((TOK_TILE, d), lambda i: (jnp.maximum(i - n_prompt_tiles, 0), 0)),
            pl.BlockSpec((TOK_TILE, d), pos_map),
            pl.BlockSpec((None, N_MOD, d), lambda i: (_tile_cond_row(i), 0, 0)),
            pl.BlockSpec((1, d), lambda i: (0, 0)),
            pl.BlockSpec((d, EVEN_MAIN), lambda i: (0, 0), pipeline_mode=pl.Buffered(1)),
            pl.BlockSpec((N_GATES, d), lambda i: (0, 0)),
            pl.BlockSpec((N_GATES, 1), lambda i: (0, 0)),
        ],
        out_specs=[
            pl.BlockSpec((TOK_TILE, d), lambda i: (i, 0)),
            pl.BlockSpec((TOK_TILE, EVEN_MAIN), lambda i: (i, 0)),
            pl.BlockSpec((N_GATES, TOK_TILE), lambda i: (0, i)),
        ],
        out_shape=[
            jax.ShapeDtypeStruct((N_TOK, d), F32),
            jax.ShapeDtypeStruct((N_TOK, EVEN_MAIN), F32),
            jax.ShapeDtypeStruct((N_GATES, N_TOK), F32),
        ],
        compiler_params=_cparams("parallel"),
        name="proj_even",
    )(x_prompt, x_sample, pos_tab, mod_l, norm_g.reshape(1, d), w_main, w_gate.T, gate_b.reshape(N_GATES, 1))


def _proj_odd_kernel(x_ref, mod_ref, g_ref, w_ref, p_ref):
    h = _norm_mod(x_ref[...], g_ref[...], mod_ref[1:2, :], mod_ref[0:1, :]).astype(BF16)
    p_ref[...] = _dot(h, w_ref[...])


def _proj_odd(x, mod_l, norm_g, w_in):
    d = D_MODEL
    width = w_in.shape[1]
    return pl.pallas_call(
        _proj_odd_kernel,
        grid=(N_TOK // TOK_TILE,),
        in_specs=[
            pl.BlockSpec((TOK_TILE, d), lambda i: (i, 0)),
            pl.BlockSpec((None, N_MOD, d), lambda i: (_tile_cond_row(i), 0, 0)),
            pl.BlockSpec((1, d), lambda i: (0, 0)),
            pl.BlockSpec((d, width), lambda i: (0, 0), pipeline_mode=pl.Buffered(1)),
        ],
        out_specs=pl.BlockSpec((TOK_TILE, width), lambda i: (i, 0)),
        out_shape=jax.ShapeDtypeStruct((N_TOK, width), F32),
        compiler_params=_cparams("parallel"),
        name="proj_odd",
    )(x, mod_l, norm_g.reshape(1, d), w_in.astype(BF16))


def _hgrn_chunks(chains, rows_ref):
    c = SCAN_CHUNK
    n = range(len(chains))
    rev, q, k, v, lf, st = zip(*chains)
    row = lax.broadcasted_iota(jnp.int32, (c, c), 0)
    col = lax.broadcasted_iota(jnp.int32, (c, c), 1)
    tris = {False: col <= row, True: col >= row}
    tris_b = {r: jnp.where(t, 1.0, 0.0).astype(BF16) for r, t in tris.items()}
    b = [_dot_w3(tris_b[rev[j]], lf[j]) for j in n]
    b2 = [b[j] * LOG2_E for j in n]
    for j in n:
        rows_ref[j, 0] = b2[j]
        rows_ref[j, 1] = k[j]
    lane_half = lax.broadcasted_iota(jnp.int32, (HALF, c), 1)
    chunk_row = lax.broadcasted_iota(jnp.int32, (c, 1), 0)
    rows = [[] for _ in n]
    for i in range(c // SUB):
        lo, hi = i * SUB, (i + 1) * SUB
        a_row = []
        for j in n:
            if rev[j]:
                has_off, edge, outside = hi < c, hi, chunk_row >= hi
            else:
                has_off, edge, outside = lo > 0, lo - 1, chunk_row < lo
            if has_off:
                beta = b[j][edge:edge + 1]
                qs = q[j][lo:hi] * jnp.exp(b[j][lo:hi] - beta)
                ks = k[j] * jnp.exp(jnp.where(outside, beta - b[j], -jnp.inf))
                a_row.append(_dot_nt(qs.astype(BF16), ks.astype(BF16)))
            else:
                a_row.append(jnp.zeros((SUB, c), F32))
        for half in range(SUB // HALF):
            h0 = lo + half * HALF
            piece = [a_row[j][half * HALF:(half + 1) * HALF] for j in n]
            for s in range(HALF):
                for j in n:
                    b_s = rows_ref[j, 0, h0 + s:h0 + s + 1, :]
                    k_s = rows_ref[j, 1, h0 + s:h0 + s + 1, :]
                    a_col = jnp.sum(jnp.exp2(b2[j][h0:h0 + HALF] - b_s) * q[j][h0:h0 + HALF] * k_s,
                                    axis=-1, keepdims=True)
                    piece[j] = jnp.where(lane_half == h0 + s, a_col, piece[j])
            for j in n:
                rows[j].append(piece[j])
    second_half = (chunk_row % SUB) >= HALF
    same_block = (row // SUB) == (col // SUB)
    out = []
    for j in n:
        meet = HALF if rev[j] else HALF - 1
        beta = jnp.concatenate([jnp.broadcast_to(b[j][lo + meet:lo + meet + 1], (SUB, b[j].shape[1]))
                                for lo in range(0, c, SUB)], axis=0)
        t_side = jnp.logical_not(second_half) if rev[j] else second_half
        qs = q[j] * jnp.exp(jnp.where(t_side, b[j] - beta, -jnp.inf))
        ks = k[j] * jnp.exp(jnp.where(t_side, -jnp.inf, beta - b[j]))
        cross = jnp.where(same_block, _dot_nt(qs.astype(BF16), ks.astype(BF16)), 0.0)
        attn = jnp.where(tris[rev[j]], jnp.concatenate(rows[j], axis=0) + cross, 0.0)
        o = _dot(attn.astype(BF16), v[j].astype(BF16)) \
            + _dot_nt((q[j] * jnp.exp(b[j])).astype(BF16), st[j].astype(BF16))
        b_exit = b[j][0:1] if rev[j] else b[j][c - 1:c]
        k_out = k[j] * jnp.exp(b_exit - b[j])
        st_new = jnp.exp(b_exit) * st[j] + _dot_tn(v[j].astype(BF16), k_out.astype(BF16))
        out.append((o, st_new))
    return out


def _hgrn_kernel(seq_len, layer, q_ref, i_ref, g_ref, ff_ref, fb_ref, lb_ref, s0_ref,
                 o_ref, s_out_ref, of_ref, ob_ref, st_ref, rows_ref):
    c = SCAN_CHUNK
    hd = HEAD_DIM
    n_chunks = seq_len // c
    lbp = lb_ref[...]
    e = jnp.exp(lbp - jnp.max(lbp, axis=0, keepdims=True))
    lb = jnp.sum(e[0:layer + 1], axis=0, keepdims=True) / jnp.sum(e, axis=0, keepdims=True)

    for d in range(2):
        for hh in range(HG_STEP_HEADS):
            st_ref[d, hh] = s0_ref[d, hh].T

    def body(n, carry):
        where, chains = [], []
        for d in range(2):
            sl = pl.ds(pl.multiple_of((n_chunks - 1 - n if d else n) * c, c), c)
            for hh in range(HG_STEP_HEADS):
                cols = slice(hh * hd, (hh + 1) * hd)
                f = lb[:, cols] + (1.0 - lb[:, cols]) * _sigmoid((fb_ref if d else ff_ref)[sl, cols])
                where.append((d, hh, sl, cols))
                chains.append((bool(d), q_ref[sl, cols], 1.0 - f, i_ref[sl, cols], jnp.log(f), st_ref[d, hh]))
        for (d, hh, sl, cols), (o, st_new) in zip(where, _hgrn_chunks(chains, rows_ref)):
            st_ref[d, hh] = st_new
            (ob_ref if d else of_ref)[sl, cols] = o
        return carry

    lax.fori_loop(0, n_chunks, body, 0)
    for hh in range(HG_STEP_HEADS):
        cols = slice(hh * hd, (hh + 1) * hd)
        o = of_ref[:, cols] + ob_ref[:, cols]
        o = o * lax.rsqrt(jnp.mean(o * o, axis=-1, keepdims=True) + RMS_EPS)
        o_ref[:, cols] = o * _silu(g_ref[:, cols])
        for d in range(2):
            s_out_ref[d, hh] = st_ref[d, hh].T


def _hgrn(p, hg_lb, layer, s0, n_seq, seq_len, tok_offset):
    hd = HEAD_DIM
    sh = HG_STEP_HEADS
    steps_per_seq = HEADS // sh
    row0 = tok_offset // seq_len

    def col(part):
        return pl.BlockSpec((seq_len, sh * hd), lambda b, h: (row0 + b, part * steps_per_seq + h))

    state_spec = pl.BlockSpec((None, 2, sh, hd, hd), lambda b, h: (b, 0, h, 0, 0))
    return pl.pallas_call(
        functools.partial(_hgrn_kernel, seq_len, layer),
        grid=(n_seq, steps_per_seq),
        in_specs=[col(0), col(1), col(2), col(3), col(4),
                  pl.BlockSpec((DEPTH + 1, sh * hd), lambda b, h: (0, h)),
                  state_spec],
        out_specs=[pl.BlockSpec((seq_len, sh * hd), lambda b, h: (b, h)), state_spec],
        out_shape=[jax.ShapeDtypeStruct((n_seq * seq_len, GROUP_W), F32),
                   jax.ShapeDtypeStruct((n_seq, 2, HEADS, hd, hd), F32)],
        scratch_shapes=[pltpu.VMEM((seq_len, sh * hd), F32), pltpu.VMEM((seq_len, sh * hd), F32),
                        pltpu.VMEM((2, sh, hd, hd), F32), pltpu.VMEM((2 * sh, 2, SCAN_CHUNK, hd), F32)],
        compiler_params=_cparams("parallel", "parallel"),
        name=f"hgrn_l{seq_len}",
    )(p, p, p, p, p, hg_lb, s0)


def _short_conv3(x, w):
    n = x.shape[0]
    r = lax.broadcasted_iota(jnp.int32, (n, 1), 0)
    prev = jnp.where(r == 0, 0.0, pltpu.roll(x, 1, 0))
    nxt = jnp.where(r == n - 1, 0.0, pltpu.roll(x, n - 1, 0))
    return prev * w[0:1] + x * w[1:2] + nxt * w[2:3]


def _conv3_silu_tiles(dst_ref, src_ref, w_ref, scale):
    n, width = src_ref.shape
    c = SCAN_CHUNK
    r = lax.broadcasted_iota(jnp.int32, (c, 1), 0)
    for j in range(n // c):
        r0 = j * c
        for col in range(0, width, LANES):
            cols = slice(col, col + LANES)
            cur = src_ref[r0:r0 + c, cols]
            prev = jnp.where(r == 0, 0.0, pltpu.roll(cur, 1, 0)) if j == 0 else src_ref[r0 - 1:r0 + c - 1, cols]
            nxt = (jnp.where(r == c - 1, 0.0, pltpu.roll(cur, c - 1, 0)) if r0 + c == n
                   else src_ref[r0 + 1:r0 + c + 1, cols])
            w = w_ref[:, cols]
            y = _silu(prev * w[0:1] + cur * w[1:2] + nxt * w[2:3])
            dst_ref[r0:r0 + c, cols] = y if scale == 1.0 else y * scale


def _split2(x):
    hi = x.astype(BF16)
    return hi, (x - hi.astype(F32)).astype(BF16)


def _mlstm_chunks(chains):
    c = SCAN_CHUNK
    row = lax.broadcasted_iota(jnp.int32, (c, c), 0)
    col = lax.broadcasted_iota(jnp.int32, (c, c), 1)
    eye_b = jnp.where(row == col, 1.0, 0.0).astype(BF16)
    tris = {False: row <= col, True: row >= col}
    tris_b = {r: jnp.where(t, 1.0, 0.0).astype(BF16) for r, t in tris.items()}
    n = range(len(chains))
    rev, q, k, vt, ig, fg, ct, nv, m_prev = zip(*chains)

    def each(fn):
        return [fn(i) for i in n]

    def dot3(parts, rhs, nt=False):
        d = _dot_nt if nt else _dot
        return d(parts[0], rhs) + d(parts[1], rhs) + d(parts[2], rhs)

    lf = each(lambda i: _split3(jnp.broadcast_to(_log_sigmoid(fg[i]), (8, c))))
    b = each(lambda i: dot3(lf[i], tris_b[rev[i]])[0:1])
    us = each(lambda i: _split3(jnp.broadcast_to(ig[i] - b[i], (c, c))))
    u = each(lambda i: _dot_nt(eye_b, us[i][0]) + _dot_nt(eye_b, us[i][1]) + _dot_nt(eye_b, us[i][2]))
    dmat = each(lambda i: jnp.where(tris[rev[i]], b[i] + u[i], -jnp.inf))
    m_t = each(lambda i: jnp.maximum(b[i] + m_prev[i], jnp.max(dmat[i], axis=0, keepdims=True)))
    qb = each(lambda i: q[i].astype(BF16))
    kb = each(lambda i: k[i].astype(BF16))
    kq = each(lambda i: _dot_nt(kb[i], qb[i]))
    p = each(lambda i: jnp.exp(dmat[i] - m_t[i]) * kq[i])
    inter = each(lambda i: jnp.exp(b[i] + m_prev[i] - m_t[i]))
    ns = each(lambda i: _split2(jnp.broadcast_to(nv[i], (8, nv[i].shape[1]))))
    qs = each(lambda i: _split2(q[i]))
    qn = each(lambda i: (_dot_nt(ns[i][0], qs[i][0]) + _dot_nt(ns[i][1], qs[i][0])
                         + _dot_nt(ns[i][0], qs[i][1]))[0:1])
    den = each(lambda i: inter[i] * qn[i] + jnp.sum(p[i], axis=0, keepdims=True))
    scale = each(lambda i: 1.0 / jnp.maximum(jnp.abs(den[i]), jnp.exp(-m_t[i])))
    cq = each(lambda i: _dot_nt(ct[i].astype(BF16), qb[i]))
    vp = each(lambda i: _dot(vt[i].astype(BF16), p[i].astype(BF16)))
    ht = each(lambda i: (inter[i] * cq[i] + vp[i]) * scale[i])
    last = each(lambda i: 0 if rev[i] else c - 1)
    m_new = each(lambda i: m_t[i][:, last[i]:last[i] + 1])
    b_exit = each(lambda i: b[i][:, last[i]:last[i] + 1])
    w = each(lambda i: jnp.exp(b_exit[i] - b[i] + ig[i] - m_new[i]))
    dec = each(lambda i: jnp.exp(b_exit[i] + m_prev[i] - m_new[i]))
    vk = each(lambda i: _dot((vt[i] * w[i]).astype(BF16), kb[i]))
    ct_new = each(lambda i: dec[i] * ct[i] + vk[i])
    ws = each(lambda i: _split2(jnp.broadcast_to(w[i], (8, c))))
    ks = each(lambda i: _split2(k[i]))
    wk = each(lambda i: (_dot(ws[i][0], ks[i][0]) + _dot(ws[i][1], ks[i][0]) + _dot(ws[i][0], ks[i][1]))[0:1])
    nv_new = each(lambda i: dec[i] * nv[i] + wk[i])
    return [(ht[i], ct_new[i], nv_new[i], m_new[i]) for i in n]


def _mlstm_kernel(seq_len, q_ref, k_ref, v_ref, og_ref, gate_t_ref, cwq_ref, cwk_ref,
                  c0_ref, n0_ref, m0_ref,
                  o_ref, c_out_ref, n_out_ref, m_out_ref,
                  q2_ref, k2_ref, vt_ref, hf_ref, hb_ref, ct_ref, n_ref, m_ref):
    c = SCAN_CHUNK
    hd = HEAD_DIM
    n_chunks = seq_len // c
    _conv3_silu_tiles(q2_ref, q_ref, cwq_ref, 1.0)
    _conv3_silu_tiles(k2_ref, k_ref, cwk_ref, HEAD_DIM ** -0.5)
    for hh in range(HEADS):
        cols = slice(hh * hd, (hh + 1) * hd)
        for j in range(n_chunks):
            vt_ref[hh, :, j * c:(j + 1) * c] = v_ref[j * c:(j + 1) * c, cols].T
        for d in range(2):
            ct_ref[d, hh] = c0_ref[d, hh].T
    n_ref[...] = n0_ref[...]
    m_ref[...] = m0_ref[...]

    def body(n, carry):
        where, chains = [], []
        for d in range(2):
            sl = pl.ds(pl.multiple_of((n_chunks - 1 - n if d else n) * c, c), c)
            for hh in range(HEADS):
                cols = slice(hh * hd, (hh + 1) * hd)
                gr = gate_t_ref[hh, :, sl]
                where.append((d, hh, sl))
                chains.append((bool(d), q2_ref[sl, cols], k2_ref[sl, cols], vt_ref[hh, :, sl],
                               gr[2 * d:2 * d + 1, :], gr[2 * d + 1:2 * d + 2, :],
                               ct_ref[d, hh], n_ref[d, hh], m_ref[d, hh]))
        for (d, hh, sl), (ht, ct, nv, m_new) in zip(where, _mlstm_chunks(chains)):
            ct_ref[d, hh] = ct
            n_ref[d, hh] = nv
            m_ref[d, hh] = m_new
            (hb_ref if d else hf_ref)[hh, :, sl] = ht
        return carry

    lax.fori_loop(0, n_chunks, body, 0)
    for hh in range(HEADS):
        cols = slice(hh * hd, (hh + 1) * hd)
        for j in range(n_chunks):
            rows = slice(j * c, (j + 1) * c)
            h = (hf_ref[hh, :, rows] + hb_ref[hh, :, rows]).T
            h = h * lax.rsqrt(jnp.mean(h * h, axis=-1, keepdims=True) + RMS_EPS)
            o_ref[rows, cols] = h * _sigmoid(og_ref[rows, cols])
        for d in range(2):
            c_out_ref[d, hh] = ct_ref[d, hh].T
    n_out_ref[...] = n_ref[...]
    m_out_ref[...] = m_ref[...]


def _mlstm(p, gate_t_h, conv_w, c0, n0, m0, n_seq, seq_len, tok_offset):
    hd = HEAD_DIM
    gw = GROUP_W
    row0 = tok_offset // seq_len

    def col(part):
        return pl.BlockSpec((seq_len, gw), lambda b: (row0 + b, part))

    c_spec = pl.BlockSpec((None, 2, HEADS, hd, hd), lambda b: (b, 0, 0, 0, 0))
    n_spec = pl.BlockSpec((None, 2, HEADS, 1, hd), lambda b: (b, 0, 0, 0, 0))
    m_spec = pl.BlockSpec((None, 2, HEADS, 1, 1), lambda b: (b, 0, 0, 0, 0))
    return pl.pallas_call(
        functools.partial(_mlstm_kernel, seq_len),
        grid=(n_seq,),
        in_specs=[col(5), col(6), col(7), col(8),
                  pl.BlockSpec((HEADS, 4, seq_len), lambda b: (0, 0, row0 + b)),
                  pl.BlockSpec((3, gw), lambda b: (0, 0)),
                  pl.BlockSpec((3, gw), lambda b: (0, 1)),
                  c_spec, n_spec, m_spec],
        out_specs=[pl.BlockSpec((seq_len, gw), lambda b: (b, 0)), c_spec, n_spec, m_spec],
        out_shape=[jax.ShapeDtypeStruct((n_seq * seq_len, gw), F32),
                   jax.ShapeDtypeStruct((n_seq, 2, HEADS, hd, hd), F32),
                   jax.ShapeDtypeStruct((n_seq, 2, HEADS, 1, hd), F32),
                   jax.ShapeDtypeStruct((n_seq, 2, HEADS, 1, 1), F32)],
        scratch_shapes=[pltpu.VMEM((seq_len, gw), F32), pltpu.VMEM((seq_len, gw), F32),
                        pltpu.VMEM((HEADS, hd, seq_len), F32),
                        pltpu.VMEM((HEADS, hd, seq_len), F32), pltpu.VMEM((HEADS, hd, seq_len), F32),
                        pltpu.VMEM((2, HEADS, hd, hd), F32), pltpu.VMEM((2, HEADS, 1, hd), F32),
                        pltpu.VMEM((2, HEADS, 1, 1), F32)],
        compiler_params=_cparams("parallel"),
        name=f"mlstm_l{seq_len}",
    )(p, p, p, p, gate_t_h, conv_w, conv_w, c0, n0, m0)


def _dft_tables(seq_len):
    n = 2 * seq_len
    k = jnp.arange(seq_len, dtype=jnp.int32)[:, None]
    t = jnp.arange(seq_len, dtype=jnp.int32)[None, :]
    ang = ((k * t) % n).astype(F32) * (2.0 * np.pi / n)
    fc = jnp.cos(ang)
    fs = jnp.sin(ang)
    nyq = jnp.where(t % 2 == 0, 1.0, -1.0).astype(F32)
    fs = jnp.where(k == 0, nyq, fs)
    return jnp.concatenate([fc, fs], axis=0)


def _filter_kernel(seq_len, z_ref, w1_ref, b1_ref, w2_ref, b2_ref, w3f_ref, w3b_ref, b3f_ref, b3b_ref,
                   f0_ref, f1_ref, rf_ref, rb_ref, fhi_ref, flo_ref, kf_ref, a_ref):
    hp = lax.Precision.HIGHEST
    n = 2 * seq_len
    z = z_ref[...]

    @pl.when(jnp.logical_and(pl.program_id(0) == 0, pl.program_id(1) == 0))
    def _():
        a1 = jnp.sin(f0_ref[...] * (jnp.dot(z, w1_ref[...], precision=hp, preferred_element_type=F32)
                                    + b1_ref[...]))
        a_ref[...] = jnp.sin(f1_ref[...] * (jnp.dot(a1, w2_ref[...], precision=hp, preferred_element_type=F32)
                                            + b2_ref[...]))

    a = a_ref[...]
    t_norm = z[:, 0:1]
    hf = (jnp.dot(a, w3f_ref[...], precision=hp, preferred_element_type=F32) + b3f_ref[...]) \
        * jnp.exp(-t_norm * jnp.exp(rf_ref[...]))
    hb = (jnp.dot(a, w3b_ref[...], precision=hp, preferred_element_type=F32) + b3b_ref[...]) \
        * jnp.exp(-t_norm * jnp.exp(rb_ref[...]))
    inv = lax.rsqrt(jnp.sum(hf * hf, axis=0, keepdims=True) + jnp.sum(hb * hb, axis=0, keepdims=True))
    hf = hf * inv
    r = lax.broadcasted_iota(jnp.int32, (seq_len, 1), 0)
    hb = jnp.where(r == 0, 0.0, hb * inv)
    sh, sl = _split2(hf + hb)
    dh, dl = _split2(hf - hb)
    fhi = fhi_ref[...]
    flo = flo_ref[...]
    kc = _dot(fhi[:seq_len], sh) + _dot(fhi[:seq_len], sl) + _dot(flo[:seq_len], sh)
    ks = _dot(fhi[seq_len:], dh) + _dot(fhi[seq_len:], dl) + _dot(flo[seq_len:], dh)
    sign = jnp.where(r % 2 == 0, 1.0, -1.0)
    k_nyq = jnp.sum(sign * (hf + hb), axis=0, keepdims=True)
    ks = jnp.where(r == 0, k_nyq, ks)
    scale = jnp.where(r == 0, 1.0 / n, 2.0 / n)
    kf_ref[0:seq_len, :] = kc * scale
    kf_ref[seq_len:n, :] = ks * scale


def _hyena_filters(seq_len, w1, b1, w2, b2, w3, b3, freq, log_rate, f_tab):
    d = D_MODEL
    cb = 256
    t = jnp.arange(seq_len, dtype=F32)
    t_norm = t / (seq_len - 1)
    bands = jnp.linspace(1e-4, HY_BANDS - 1, HY_BANDS, dtype=F32)
    ang = (2.0 * np.pi / seq_len) * t[:, None] * bands[None, :]
    z = jnp.concatenate([t_norm[:, None], jnp.cos(ang), jnp.sin(ang)], axis=-1)
    kpad = 128 - HY_EMB
    z = jnp.pad(z, ((0, 0), (0, kpad)))
    w1p = jnp.pad(w1, ((0, kpad), (0, 0)))
    f_hi = f_tab.astype(BF16)
    f_lo = (f_tab - f_hi.astype(F32)).astype(BF16)
    n_cb = d // cb
    hh = HY_HIDDEN
    row = lambda a: a.reshape(1, -1)
    const = lambda shape: pl.BlockSpec(shape, lambda o, j: (0,) * len(shape))
    fwd = lambda rows: pl.BlockSpec((rows, cb), lambda o, j: (0, o * n_cb + j))
    bwd = lambda rows: pl.BlockSpec((rows, cb), lambda o, j: (0, (HY_ORDER + o) * n_cb + j))
    return pl.pallas_call(
        functools.partial(_filter_kernel, seq_len),
        grid=(HY_ORDER, n_cb),
        in_specs=[const((seq_len, 128)), const((128, hh)), const((1, hh)), const((hh, hh)), const((1, hh)),
                  fwd(hh), bwd(hh), fwd(1), bwd(1),
                  const((1, hh)), const((1, hh)), fwd(1), bwd(1),
                  const((2 * seq_len, seq_len)), const((2 * seq_len, seq_len))],
        out_specs=pl.BlockSpec((None, 2 * seq_len, cb), lambda o, j: (o, 0, j)),
        out_shape=jax.ShapeDtypeStruct((HY_ORDER, 2 * seq_len, d), F32),
        scratch_shapes=[pltpu.VMEM((seq_len, hh), F32)],
        compiler_params=_cparams("arbitrary", "arbitrary"),
        name=f"hyena_filter_l{seq_len}",
    )(z, w1p, row(b1), w2, row(b2), w3, w3, row(b3), row(b3),
      row(freq[0]), row(freq[1]), row(log_rate), row(log_rate), f_hi, f_lo)


def _hyena_kernel(seq_len, seqs, v_ref, x1_ref, x2_ref, cwv_ref, cw1_ref, cw2_ref, bias_ref, kf_ref,
                  f_ref, ft_ref, o_ref, z_ref, zb_ref, y_ref):
    kc = min(HY_FREQ_CHUNK, seq_len)
    n_k = seq_len // kc
    r = lax.broadcasted_iota(jnp.int32, (kc, 1), 0)
    gate_refs = ((x1_ref, cw1_ref), (x2_ref, cw2_ref))
    for s in range(seqs):
        rows = slice(s * seq_len, (s + 1) * seq_len)
        z_ref[s] = _short_conv3(v_ref[rows, :], cwv_ref[...])
    for o in range(HY_ORDER):
        for s in range(seqs):
            zb_ref[s] = z_ref[s].astype(BF16)
            y_ref[s] = jnp.zeros(y_ref.shape[1:], F32)

        def freq_chunk(j, carry):
            r0 = pl.multiple_of(j * kc, kc)
            k_cos = kf_ref[o, pl.ds(r0, kc), :]
            k_sin = kf_ref[o, pl.ds(seq_len + r0, kc), :]
            real_row = jnp.logical_and(r == 0, j == 0)
            for s in range(seqs):
                a = _dot(f_ref[pl.ds(r0, kc), :], zb_ref[s])
                bm = _dot(f_ref[pl.ds(seq_len + r0, kc), :], zb_ref[s])
                yc = a * k_cos - jnp.where(real_row, 0.0, bm * k_sin)
                ys = jnp.where(real_row, bm * k_sin, a * k_sin + bm * k_cos)
                y_ref[s] += _dot(ft_ref[j], yc.astype(BF16)) + _dot(ft_ref[n_k + j], ys.astype(BF16))
            return carry

        lax.fori_loop(0, n_k, freq_chunk, 0)
        x_ref, cw_ref = gate_refs[o]
        for s in range(seqs):
            rows = slice(s * seq_len, (s + 1) * seq_len)
            gate = _short_conv3(x_ref[rows, :], cw_ref[...])
            z_ref[s] = gate * (y_ref[s] + z_ref[s] * bias_ref[o:o + 1, :])
    for s in range(seqs):
        o_ref[s * seq_len:(s + 1) * seq_len, :] = z_ref[s]


def _hyena(u, conv_w, bias, kf, f_tab, n_seq, seq_len, tok_offset):
    d = D_MODEL
    cb = 256
    n_cb = d // cb
    seqs = max(1, HY_STEP_ROWS // seq_len)
    rows = seqs * seq_len
    row0 = tok_offset // rows
    kc = min(HY_FREQ_CHUNK, seq_len)
    n_k = seq_len // kc
    f_bf = f_tab.astype(BF16)
    ft = f_bf.T.reshape(seq_len, 2 * n_k, kc).transpose(1, 0, 2)

    def part(k):
        return pl.BlockSpec((rows, cb), lambda j, b: (row0 + b, k * n_cb + j))

    def cw(k):
        return pl.BlockSpec((3, cb), lambda j, b: (0, k * n_cb + j))

    return pl.pallas_call(
        functools.partial(_hyena_kernel, seq_len, seqs),
        grid=(n_cb, n_seq // seqs),
        in_specs=[part(0), part(1), part(2), cw(0), cw(1), cw(2),
                  pl.BlockSpec((HY_ORDER, cb), lambda j, b: (0, j)),
                  pl.BlockSpec((HY_ORDER, 2 * seq_len, cb), lambda j, b: (0, 0, j)),
                  pl.BlockSpec((2 * seq_len, seq_len), lambda j, b: (0, 0)),
                  pl.BlockSpec((2 * n_k, seq_len, kc), lambda j, b: (0, 0, 0))],
        out_specs=pl.BlockSpec((rows, cb), lambda j, b: (b, j)),
        out_shape=jax.ShapeDtypeStruct((n_seq * seq_len, d), F32),
        scratch_shapes=[pltpu.VMEM((seqs, seq_len, cb), F32), pltpu.VMEM((seqs, seq_len, cb), BF16),
                        pltpu.VMEM((seqs, seq_len, cb), F32)],
        compiler_params=_cparams("parallel", "parallel"),
        name=f"hyena_l{seq_len}",
    )(u, u, u, conv_w, conv_w, conv_w, bias, kf, f_bf, ft)


def _mix_out_kernel(n_parts, *refs):
    o_refs = refs[:2 * n_parts]
    w_refs = refs[2 * n_parts:3 * n_parts]
    x_ref, mod_ref, g_ref, wr_both_ref, br_ref = refs[3 * n_parts:3 * n_parts + 5]
    x1_ref, h_ref, eid_ref, wts_ref, rank_ref, cnt_ref, run_ref = refs[3 * n_parts + 5:]
    i = pl.program_id(0)
    t = ROUTE_TILE
    ne = N_EXPERTS

    from_prompt = i < N_PROMPT // ROUTE_TILE
    y = None
    for j in range(n_parts):
        o = jnp.where(from_prompt, o_refs[2 * j][...], o_refs[2 * j + 1][...])
        yj = _dot(o.astype(BF16), w_refs[j][...])
        y = yj if y is None else y + yj
    x1 = x_ref[...] + mod_ref[2:3, :] * y
    x1_ref[...] = x1
    h = _norm_mod(x1, g_ref[...], mod_ref[4:5, :], mod_ref[3:4, :])
    h_ref[...] = _pack_bf16_halves(h)
    h_hi = h.astype(BF16)
    h_lo = (h - h_hi.astype(F32)).astype(BF16)
    hi_terms = _dot(h_hi, wr_both_ref[...])
    logits = (hi_terms[:, :LANES] + hi_terms[:, LANES:] + _dot(h_lo, wr_both_ref[:, :LANES])).T[:ne] \
        + br_ref[...]

    @pl.when(i == 0)
    def _():
        run_ref[...] = jnp.zeros_like(run_ref)

    e_iota = lax.broadcasted_iota(jnp.int32, (ne, t), 0)
    vals, eids, onehots = [], [], []
    for _k in range(TOP_K):
        m = jnp.max(logits, axis=0, keepdims=True)
        eid = jnp.min(jnp.where(logits == m, e_iota, ne), axis=0, keepdims=True)
        sel = e_iota == eid
        logits = jnp.where(sel, -jnp.inf, logits)
        onehots.append(jnp.where(sel, 1.0, 0.0))
        vals.append(m)
        eids.append(eid)
    r2 = lax.broadcasted_iota(jnp.int32, (t, t), 0)
    c2 = lax.broadcasted_iota(jnp.int32, (t, t), 1)
    before = jnp.where(r2 < c2, 1.0, 0.0).astype(BF16)
    earlier = _dot(jnp.concatenate(onehots, axis=0).astype(BF16), before)
    running = run_ref[...]
    ranks = []
    for k, onehot in enumerate(onehots):
        ranks.append(jnp.sum(onehot * (running + earlier[k * ne:(k + 1) * ne]), axis=0, keepdims=True))
        running = running + jnp.sum(onehot, axis=1, keepdims=True)
    run_ref[...] = running
    cnt_ref[...] = running
    v = jnp.concatenate(vals, axis=0)
    ex = jnp.exp(v - v[0:1])
    wts_ref[...] = ex / jnp.sum(ex, axis=0, keepdims=True)
    eid_ref[...] = jnp.concatenate(eids, axis=0)
    rank_ref[...] = jnp.concatenate(ranks, axis=0).astype(jnp.int32)


def _mix_out(parts, x, mod_l, norm_g, w_router, b_router):
    d = D_MODEL
    t = ROUTE_TILE
    ne = N_EXPERTS
    n_parts = len(parts)
    n_prompt_tiles = N_PROMPT // t
    wr = jnp.pad(w_router, ((0, 0), (0, LANES - ne)))
    wr_hi = wr.astype(BF16)
    wr_both = jnp.concatenate([wr_hi, (wr - wr_hi.astype(F32)).astype(BF16)], axis=1)
    in_specs = []
    for o_p, _, _ in parts:
        in_specs.append(pl.BlockSpec((t, o_p.shape[1]), lambda i: (jnp.minimum(i, n_prompt_tiles - 1), 0)))
        in_specs.append(pl.BlockSpec((t, o_p.shape[1]), lambda i: (jnp.maximum(i - n_prompt_tiles, 0), 0)))
    in_specs += [pl.BlockSpec(w.shape, lambda i: (0, 0)) for _, _, w in parts]
    in_specs += [
        pl.BlockSpec((t, d), lambda i: (i, 0)),
        pl.BlockSpec((None, N_MOD, d), lambda i: (_tile_cond_row(i, t), 0, 0)),
        pl.BlockSpec((1, d), lambda i: (0, 0)),
        pl.BlockSpec((d, 2 * LANES), lambda i: (0, 0)),
        pl.BlockSpec((ne, 1), lambda i: (0, 0)),
    ]
    tok_major = pl.BlockSpec((TOP_K, t), lambda i: (0, i))
    return pl.pallas_call(
        functools.partial(_mix_out_kernel, n_parts),
        grid=(N_TOK // t,),
        in_specs=in_specs,
        out_specs=[pl.BlockSpec((t, d), lambda i: (i, 0)), pl.BlockSpec((t, d // 2), lambda i: (i, 0)),
                   tok_major, tok_major, tok_major, pl.BlockSpec((ne, 1), lambda i: (0, 0))],
        out_shape=[jax.ShapeDtypeStruct((N_TOK, d), F32), jax.ShapeDtypeStruct((N_TOK, d // 2), jnp.uint32),
                   jax.ShapeDtypeStruct((TOP_K, N_TOK), jnp.int32), jax.ShapeDtypeStruct((TOP_K, N_TOK), F32),
                   jax.ShapeDtypeStruct((TOP_K, N_TOK), jnp.int32), jax.ShapeDtypeStruct((ne, 1), F32)],
        scratch_shapes=[pltpu.VMEM((ne, 1), F32)],
        compiler_params=_cparams("arbitrary"),
        name="mix_out_router",
    )(*[o for part in parts for o in part[:2]], *[w.astype(BF16) for _, _, w in parts], x, mod_l,
      norm_g.reshape(1, d), wr_both, b_router.reshape(ne, 1))


def _sc_row_gather(src, idx):
    n = idx.shape[0]
    width = src.shape[1]
    step_rows = min(SC_MAX_INDICES, SC_STEP_BYTES // (width * 4))
    sc = plsc.get_sparse_core_info()
    n_workers = sc.num_cores * sc.num_subcores
    per_worker = n // n_workers
    if n % n_workers or per_worker % step_rows:
        raise ValueError("row count must be whole SparseCore steps on every subcore")
    mesh = plsc.VectorSubcoreMesh(core_axis_name="core", subcore_axis_name="subcore")

    @functools.partial(pl.kernel, out_type=jax.ShapeDtypeStruct((n, width), src.dtype), mesh=mesh,
                       scratch_types=[pltpu.VMEM((per_worker,), jnp.int32),
                                      pltpu.VMEM((step_rows, width), src.dtype)],
                       name="sc_row_gather")
    def gather(src_hbm, idx_hbm, dst_hbm, idx_vmem, rows_vmem):
        worker = lax.axis_index("subcore") * sc.num_cores + lax.axis_index("core")
        base = worker * per_worker
        pltpu.sync_copy(idx_hbm.at[pl.ds(base, per_worker)], idx_vmem)

        @pl.loop(0, per_worker // step_rows)
        def _(c):
            pltpu.sync_copy(src_hbm.at[idx_vmem.at[pl.ds(c * step_rows, step_rows)]], rows_vmem)
            pltpu.sync_copy(rows_vmem, dst_hbm.at[pl.ds(base + c * step_rows, step_rows)])

    return gather(src, idx)


def _sc_row_scatter(src, rows_of_src):
    copies, n_src = rows_of_src.shape
    width = src.shape[1]
    step_rows = min(SC_MAX_INDICES, SC_STEP_BYTES // (width * 4))
    sc = plsc.get_sparse_core_info()
    n_workers = sc.num_cores * sc.num_subcores
    per_worker = n_src // n_workers
    steps = per_worker // step_rows
    if src.shape[0] != n_src or n_src % n_workers or per_worker % step_rows:
        raise ValueError("row counts must be whole SparseCore steps on every subcore")
    mesh = plsc.VectorSubcoreMesh(core_axis_name="core", subcore_axis_name="subcore")
    idx = rows_of_src.reshape(copies, n_workers, steps, step_rows).transpose(1, 2, 0, 3)

    @functools.partial(pl.kernel, out_type=jax.ShapeDtypeStruct((copies * n_src, width), src.dtype),
                       mesh=mesh,
                       scratch_types=[pltpu.VMEM((steps, copies, step_rows), jnp.int32),
                                      pltpu.VMEM((step_rows, width), src.dtype)],
                       name="sc_row_scatter")
    def scatter(src_hbm, idx_hbm, dst_hbm, idx_vmem, rows_vmem):
        worker = lax.axis_index("subcore") * sc.num_cores + lax.axis_index("core")
        pltpu.sync_copy(idx_hbm.at[worker], idx_vmem)

        @pl.loop(0, steps)
        def _(c):
            pltpu.sync_copy(src_hbm.at[pl.ds(worker * per_worker + c * step_rows, step_rows)], rows_vmem)
            for j in range(copies):
                pltpu.sync_copy(rows_vmem, dst_hbm.at[idx_vmem.at[c, j]])

    return scatter(src, idx)


def _pack_bf16_halves(x):
    w = x.shape[1] // 2
    bits = pltpu.bitcast(x.astype(BF16).astype(F32), jnp.uint32)
    return bits[:, :w] | (bits[:, w:] >> 16)


def _unpack_bf16_halves(p):
    hi = pltpu.bitcast(p & jnp.uint32(0xFFFF0000), F32).astype(BF16)
    lo = pltpu.bitcast(p << 16, F32).astype(BF16)
    return hi, lo


def _experts_kernel(layer, te_ref, first_ref, slot_ref, next_ref, rows_ref, nv_ref,
                    x_ref, wgu_hbm, bg_ref, bu_ref, wd_hbm, bd_ref, sel_ref,
                    y_ref, wgu_buf, wd_buf, wg_ref, wu_ref, wdb_ref, sem):
    i = pl.program_id(0)
    valid = i < nv_ref[0]
    half = DEINT_COLS // 2
    k_half = x_ref.shape[1]

    def fetch(expert, slot):
        return (pltpu.make_async_copy(wgu_hbm.at[layer, expert], wgu_buf.at[slot], sem.at[slot, 0]),
                pltpu.make_async_copy(wd_hbm.at[layer, expert], wd_buf.at[slot], sem.at[slot, 1]))

    @pl.when(i == 0)
    def _():
        for cp in fetch(te_ref[0], 0):
            cp.start()

    @pl.when(jnp.logical_and(valid, first_ref[i] == 1))
    def _():
        slot = slot_ref[i]
        for cp in fetch(te_ref[i], slot):
            cp.wait()

        @pl.when(next_ref[i] >= 0)
        def _():
            for cp in fetch(next_ref[i], 1 - slot):
                cp.start(priority=1)

        for c in range(wgu_buf.shape[2] // DEINT_COLS):
            w = wgu_buf[slot, :, c * DEINT_COLS:(c + 1) * DEINT_COLS].astype(BF16)
            split = _dot(w, sel_ref[...]).astype(BF16)
            wg_ref[:, c * half:(c + 1) * half] = split[:, :half]
            wu_ref[:, c * half:(c + 1) * half] = split[:, half:]
        wdb_ref[...] = wd_buf[slot].astype(BF16)

    def ffn(rows):
        x_l, x_r = _unpack_bf16_halves(x_ref[:rows, :])
        gl = _dot(x_l, wg_ref[:k_half, :]) + _dot(x_r, wg_ref[k_half:, :]) + bg_ref[...]
        up = _dot(x_l, wu_ref[:k_half, :]) + _dot(x_r, wu_ref[k_half:, :]) + bu_ref[...]
        gl = jnp.minimum(gl, SWIGLU_LIMIT)
        up = jnp.clip(up, -SWIGLU_LIMIT, SWIGLU_LIMIT)
        act = (up + 1.0) * gl * _sigmoid(SWIGLU_ALPHA * gl)
        y_ref[:rows, :] = _pack_bf16_halves(_dot(act.astype(BF16), wdb_ref[...]) + bd_ref[...])

    tm = x_ref.shape[0]
    quarter = tm // MOE_TILE_PARTS
    used = rows_ref[i]
    for part in range(1, MOE_TILE_PARTS + 1):
        rows = part * quarter
        fits = jnp.logical_and(used > rows - quarter, used <= rows)

        @pl.when(jnp.logical_and(valid, fits))
        def _(rows=rows):
            ffn(rows)
            if rows < tm:
                y_ref[rows:, :] = jnp.zeros((tm - rows, y_ref.shape[1]), y_ref.dtype)

    @pl.when(jnp.logical_not(valid))
    def _():
        y_ref[...] = jnp.zeros_like(y_ref)


def _experts(xs, tile_expert, tile_first, tile_rows, n_valid, layer, w_gu, b_gate, b_up, w_down, b_down):
    d = D_MODEL
    tm = MOE_TILE
    n_tiles = MOE_ROWS // tm
    ff = w_down.shape[2]
    half = DEINT_COLS // 2
    r = jnp.arange(DEINT_COLS)[:, None]
    c = jnp.arange(DEINT_COLS)[None, :]
    sel = (r == jnp.where(c < half, 2 * c, 2 * (c - half) + 1)).astype(BF16)
    group = jnp.cumsum(tile_first) - 1
    tile_slot = (group % 2).astype(jnp.int32)
    is_last_group = group == group[-1]
    following = jnp.concatenate([tile_expert[1:], tile_expert[-1:]])
    idx = jnp.arange(n_tiles, dtype=jnp.int32)
    group_end = jnp.max(jnp.where(group[None, :] == group[:, None], idx[None, :], -1), axis=1)
    tile_next = jnp.where(is_last_group, -1, following[group_end]).astype(jnp.int32)
    wspec = lambda k, n: pl.BlockSpec((None, None, k, n), lambda i, *_: (layer, _[0][i], 0, 0))
    grid_spec = pltpu.PrefetchScalarGridSpec(
        num_scalar_prefetch=6,
        grid=(n_tiles,),
        in_specs=[pl.BlockSpec((tm, d // 2), lambda i, *_: (i, 0)),
                  pl.BlockSpec(memory_space=pl.ANY), wspec(1, ff), wspec(1, ff),
                  pl.BlockSpec(memory_space=pl.ANY), wspec(1, d),
                  pl.BlockSpec((DEINT_COLS, DEINT_COLS), lambda i, *_: (0, 0))],
        out_specs=pl.BlockSpec((tm, d // 2), lambda i, *_: (i, 0)),
        scratch_shapes=[pltpu.VMEM((2, d, 2 * ff), F32), pltpu.VMEM((2, ff, d), F32),
                        pltpu.VMEM((d, ff), BF16), pltpu.VMEM((d, ff), BF16), pltpu.VMEM((ff, d), BF16),
                        pltpu.SemaphoreType.DMA((2, 2))],
    )
    return pl.pallas_call(
        functools.partial(_experts_kernel, layer),
        grid_spec=grid_spec,
        out_shape=jax.ShapeDtypeStruct((MOE_ROWS, d // 2), jnp.uint32),
        compiler_params=_cparams("arbitrary"),
        name="experts",
    )(tile_expert, tile_first, tile_slot, tile_next, tile_rows, n_valid, xs, w_gu, b_gate, b_up, w_down,
      b_down, sel)


def _combine_kernel(final, x_ref, g_ref, w_ref, mod_ref, fg_ref, o_ref):
    y_l = y_r = None
    for k in range(TOP_K):
        g_l, g_r = _unpack_bf16_halves(g_ref[k])
        wk = w_ref[:, k:k + 1]
        y_l = wk * g_l.astype(F32) if y_l is None else y_l + wk * g_l.astype(F32)
        y_r = wk * g_r.astype(F32) if y_r is None else y_r + wk * g_r.astype(F32)
    x = x_ref[...] + mod_ref[5:6, :] * jnp.concatenate([y_l, y_r], axis=1)
    if final:
        x = x * lax.rsqrt(jnp.mean(x * x, axis=-1, keepdims=True) + RMS_EPS) * fg_ref[...]
    o_ref[...] = x


def _combine(x1, gathered, wts, mod_l, final_g, final, tok0, n_tok):
    d = D_MODEL
    t = ROUTE_TILE
    tile0 = tok0 // t
    return pl.pallas_call(
        functools.partial(_combine_kernel, final),
        grid=(n_tok // t,),
        in_specs=[pl.BlockSpec((t, d), lambda i: (tile0 + i, 0)),
                  pl.BlockSpec((TOP_K, t, d // 2), lambda i: (0, i, 0)),
                  pl.BlockSpec((t, TOP_K), lambda i: (tile0 + i, 0)),
                  pl.BlockSpec((None, N_MOD, d), lambda i: (_tile_cond_row(tile0 + i, t), 0, 0)),
                  pl.BlockSpec((1, d), lambda i: (0, 0))],
        out_specs=pl.BlockSpec((t, d), lambda i: (i, 0)),
        out_shape=jax.ShapeDtypeStruct((n_tok, d), F32),
        compiler_params=_cparams("parallel"),
        name="moe_combine",
    )(x1, gathered, wts, mod_l, final_g.reshape(1, d))


def _moe(x1, h, eid, wts, rank, counts, mod_l, layer, w_gu, b_gu, w_down, b_down, final_g, final):
    d = D_MODEL
    tm = MOE_TILE
    n_tiles = MOE_ROWS // tm
    cnt = counts.reshape(N_EXPERTS).astype(jnp.int32)
    gsz = ((cnt + tm - 1) // tm) * tm
    ends = jnp.cumsum(gsz)
    offs = ends - gsz
    e_ids = jnp.arange(N_EXPERTS, dtype=jnp.int32)
    pos = jnp.sum(jnp.where(eid[..., None] == e_ids, offs, 0), axis=-1) + rank
    tile_start = jnp.arange(n_tiles, dtype=jnp.int32) * tm
    tile_expert = jnp.minimum(jnp.sum((ends[None, :] <= tile_start[:, None]).astype(jnp.int32), axis=1),
                              N_EXPERTS - 1)
    n_valid = (ends[-1:] // tm).astype(jnp.int32)
    last_valid = jnp.maximum(n_valid[0] - 1, 0)
    tile_expert = jnp.where(jnp.arange(n_tiles) < n_valid[0], tile_expert, tile_expert[last_valid])
    tile_first = jnp.concatenate([jnp.ones((1,), jnp.int32),
                                  (tile_expert[1:] != tile_expert[:-1]).astype(jnp.int32)])
    j = jnp.arange(tm, dtype=jnp.int32)[None, :]
    pad_used = j < (gsz - cnt)[:, None]
    n_unused_before = jnp.cumsum((~pad_used).reshape(-1).astype(jnp.int32)) - 1
    pad_pos = jnp.where(pad_used, (offs + cnt)[:, None] + j,
                        ends[-1] + n_unused_before.reshape(N_EXPERTS, tm))
    if N_EXPERTS * tm != N_TOK:
        raise ValueError("the filler slots are laid out as one extra copy of every token")

    xs = _sc_row_scatter(h, jnp.concatenate([pos, pad_pos.reshape(1, N_TOK)], axis=0))
    real_end = (offs + cnt)[tile_expert]
    tile_rows = jnp.clip(real_end - tile_start, 0, tm).astype(jnp.int32)
    ys = _experts(xs, tile_expert, tile_first, tile_rows, n_valid, layer, w_gu,
                  b_gu[:, :, None, 0::2], b_gu[:, :, None, 1::2], w_down, b_down[:, :, None, :])
    spans = ((0, N_PROMPT), (N_PROMPT, N_SAMPLE)) if final else ((0, N_TOK),)
    outs = []
    for tok0, n_tok in spans:
        gathered = _sc_row_gather(ys, pos[:, tok0:tok0 + n_tok].reshape(-1))
        outs.append(_combine(x1, gathered.reshape(TOP_K, n_tok, d // 2), wts.T, mod_l, final_g, final,
                             tok0, n_tok))
    return tuple(outs) if final else outs[0]


def _grid_positions(n_tok, d):
    rows = n_tok // GRID_W
    r, col = jnp.meshgrid(jnp.arange(rows, dtype=F32), jnp.arange(GRID_W, dtype=F32), indexing='ij')
    r = r.reshape(-1)
    col = col.reshape(-1)
    quarter = d // 4
    inv = 1.0 / (10000.0 ** (jnp.arange(quarter, dtype=F32) / quarter))
    ar = r[:, None] * inv[None]
    ac = col[:, None] * inv[None]
    return jnp.concatenate([jnp.sin(ar), jnp.cos(ar), jnp.sin(ac), jnp.cos(ac)], axis=-1)


def kernel(x_prompt, x_sample, state_hgrn, state_mlstm_c, state_mlstm_n, state_mlstm_m, c, c_ctx,
           norm_g, final_g, w_mod, b_mod, ev_w_in, ev_gate_b, ev_conv, hg_lb, ev_w_out,
           hy_w_in, hy_conv, hy_w1, hy_b1, hy_w2, hy_b2, hy_w3, hy_b3, hy_freq, hy_log_rate, hy_bias, hy_w_out,
           w_router, b_router, w_gu, b_gu, w_down, b_down):
    d = D_MODEL
    hd = HEAD_DIM
    cond = jnp.concatenate([c_ctx[None], c, jnp.zeros((N_COND - 1 - DEC_BATCH, d), F32)], axis=0)
    mod = _modulation(cond, w_mod, b_mod)
    pos_tab = jnp.concatenate([jnp.zeros((TOK_TILE, d), F32), _grid_positions(DEC_SEQ, d)], axis=0)

    groups = ((BATCH, SEQ, 0), (DEC_BATCH, DEC_SEQ, N_PROMPT))
    new_states = None
    for l in range(DEPTH):
        if l % 2 == 0:
            e = l // 2
            if l == 0:
                x, p, gate_t = _proj_even(x_prompt.reshape(N_PROMPT, d), x_sample.reshape(N_SAMPLE, d),
                                          pos_tab, mod[l], norm_g[l, 0], ev_w_in[e], ev_gate_b[e])
            else:
                raise NotImplementedError("only the first layer adds grid positions")
            gate_t_h = gate_t.reshape(4, HEADS, N_TOK).transpose(1, 0, 2)
            o_hg, o_ml = [], []
            for gi, (n_seq, seq_len, off) in enumerate(groups):
                if gi == 0:
                    s0 = jnp.zeros((n_seq, 2, HEADS, hd, hd), F32)
                    c0 = jnp.zeros((n_seq, 2, HEADS, hd, hd), F32)
                    n0 = jnp.zeros((n_seq, 2, HEADS, 1, hd), F32)
                    m0 = jnp.zeros((n_seq, 2, HEADS, 1, 1), F32)
                else:
                    s0 = state_hgrn[:, e]
                    c0 = state_mlstm_c[:, e]
                    n0 = state_mlstm_n[:, e].reshape(n_seq, 2, HEADS, 1, hd)
                    m0 = state_mlstm_m[:, e].reshape(n_seq, 2, HEADS, 1, 1)
                og, s_fin = _hgrn(p, hg_lb, l, s0, n_seq, seq_len, off)
                om, c_fin, n_fin, m_fin = _mlstm(p, gate_t_h, ev_conv[e], c0, n0, m0, n_seq, seq_len, off)
                o_hg.append(og)
                o_ml.append(om)
                if gi == 0:
                    new_states = (s_fin[:, None], c_fin[:, None],
                                  n_fin.reshape(n_seq, 1, 2, HEADS, hd), m_fin.reshape(n_seq, 1, 2, HEADS))
            parts = [(*o_hg, ev_w_out[e][:GROUP_W]), (*o_ml, ev_w_out[e][GROUP_W:])]
        else:
            o = l // 2
            u = _proj_odd(x, mod[l], norm_g[l, 0], hy_w_in[o])
            zs = []
            for n_seq, seq_len, off in groups:
                f_tab = _dft_tables(seq_len)
                kf = _hyena_filters(seq_len, hy_w1[o], hy_b1[o], hy_w2[o], hy_b2[o], hy_w3[o], hy_b3[o],
                                    hy_freq[o], hy_log_rate[o], f_tab)
                zs.append(_hyena(u, hy_conv[o], hy_bias[o], kf, f_tab, n_seq, seq_len, off))
            parts = [(*zs, hy_w_out[o])]
        x1, h, eid, wts, rank, counts = _mix_out(parts, x, mod[l], norm_g[l, 1], w_router[l], b_router[l])
        x = _moe(x1, h, eid, wts, rank, counts, mod[l], l, w_gu, b_gu, w_down, b_down,
                 final_g, final=(l == DEPTH - 1))

    y_prompt, y_sample = x
    return (y_prompt.reshape(BATCH, SEQ, d), y_sample.reshape(DEC_BATCH, DEC_SEQ, d)) + new_states
```
